```python
import math
import jax, jax.numpy as jnp
from jax import lax
import numpy as np

D_MODEL = 1024
BATCH = 4
SEQ = 8192
DEPTH = 1

N_MEM = 256
EPS = 1e-6

DN_HEADS = 4
DN_DK = 128
DN_DV = 128
DN_CONV = 4
DN_CHUNK = 64
DN_QKV_W = DN_HEADS * (2 * DN_DK + DN_DV)
DN_V_W = DN_HEADS * DN_DV

DA_HEADS = 4
DA_DH = 64
DA_DV = 2 * DA_DH
DA_QK_W = DA_HEADS * 2 * DA_DH
DA_V_W = DA_HEADS * DA_DV
DA_QKV_W = 2 * DA_QK_W + DA_V_W
Q_BLOCK = 128

MX_HEADS = 4
MX_DH = 128
MX_W = MX_HEADS * MX_DH

REL_BUCKETS = 32
REL_MAX_DIST = 128

N_BRANCH = 3
BRANCH_W = 512

IN_SIZES = (DN_QKV_W, DN_V_W, DN_HEADS, DN_HEADS, DA_QKV_W, MX_W, N_BRANCH * D_MODEL)
IN_W = DN_QKV_W + DN_V_W + 2 * DN_HEADS + DA_QKV_W + MX_W + N_BRANCH * D_MODEL

N_EXPERTS = 32
TOP_K = 4
D_FF = 1024
SWIGLU_LIMIT = 7.0
SWIGLU_ALPHA = 1.702
EXPERT_BLOCK = 512

kernel_name = 'hybrid_deltanet_diffattn_memx_moe_block'


def rms_norm(x, gain):
    xf = x.astype(jnp.float32)
    return xf * lax.rsqrt(jnp.mean(xf * xf, axis=-1, keepdims=True) + EPS) * gain.astype(jnp.float32)


def l2_norm(x):
    return x * lax.rsqrt(jnp.sum(x * x, axis=-1, keepdims=True) + EPS)


def t5_bucket(rel):
    n = jnp.maximum(-rel, 0)
    max_exact = REL_BUCKETS // 2
    nf = jnp.maximum(n, 1).astype(jnp.float32)
    large = max_exact + (jnp.log(nf / max_exact) / math.log(REL_MAX_DIST / max_exact)
                         * (REL_BUCKETS - max_exact)).astype(jnp.int32)
    large = jnp.minimum(large, REL_BUCKETS - 1)
    return jnp.where(n < max_exact, n, large)


def causal_depthwise_conv(x, w):
    c = x.shape[-1]
    return lax.conv_general_dilated(x, w[:, None, :], window_strides=(1,),
                                    padding=[(w.shape[0] - 1, 0)],
                                    dimension_numbers=('NWC', 'WIO', 'NWC'),
                                    feature_group_count=c)


def chunk_gated_delta_rule(q, k, v, g, beta):
    b, h, s, dk = q.shape
    dv = v.shape[-1]
    c = DN_CHUNK
    n = s // c
    q = q.reshape(b, h, n, c, dk)
    k = k.reshape(b, h, n, c, dk)
    v = v.reshape(b, h, n, c, dv)
    beta = beta.reshape(b, h, n, c)
    g = jnp.cumsum(g.reshape(b, h, n, c), axis=-1)
    incl = jnp.tril(jnp.ones((c, c), dtype=bool))
    strict = jnp.tril(jnp.ones((c, c), dtype=bool), k=-1)
    decay = jnp.exp(jnp.where(incl, g[..., :, None] - g[..., None, :], -jnp.inf))
    kk = jnp.einsum('bhnid,bhnjd->bhnij', k, k)
    lower = jnp.where(strict, kk * decay * beta[..., :, None], 0.0)
    a_mat = lower + jnp.eye(c, dtype=lower.dtype)
    rhs = jnp.concatenate([v * beta[..., None], k * (beta * jnp.exp(g))[..., None]], axis=-1)
    sol = lax.linalg.triangular_solve(a_mat, rhs, left_side=True, lower=True, unit_diagonal=True)
    u, w = sol[..., :dv], sol[..., dv:]
    qk = jnp.where(incl, jnp.einsum('bhnid,bhnjd->bhnij', q, k) * decay, 0.0)
    g_last = g[..., -1]
    q_dec = q * jnp.exp(g)[..., None]
    k_dec = k * jnp.exp(g_last[..., None] - g)[..., None]

    def step(state, xs):
        u_c, w_c, q_c, qk_c, k_c, gl_c = xs
        v_new = u_c - jnp.einsum('bhck,bhkv->bhcv', w_c, state)
        o_c = jnp.einsum('bhck,bhkv->bhcv', q_c, state) + jnp.einsum('bhij,bhjv->bhiv', qk_c, v_new)
        state = state * jnp.exp(gl_c)[..., None, None] + jnp.einsum('bhck,bhcv->bhkv', k_c, v_new)
        return state, o_c

    xs = tuple(jnp.moveaxis(t, 2, 0) for t in (u, w, q_dec, qk, k_dec, g_last))
    state0 = jnp.zeros((b, h, dk, dv), q.dtype)
    _, o = lax.scan(step, state0, xs)
    return jnp.moveaxis(o, 0, 2).reshape(b, h, s, dv)


def diff_attention(q, k, v, lam, rel_table):
    b, h, _, s, dh = q.shape
    nb = s // Q_BLOCK
    qb = q.reshape(b, h, 2, nb, Q_BLOCK, dh).transpose(3, 0, 1, 2, 4, 5)
    k_pos = jnp.arange(s, dtype=jnp.int32)
    scale = dh ** -0.5

    def one_block(args):
        i, q_blk = args
        q_pos = i * Q_BLOCK + jnp.arange(Q_BLOCK, dtype=jnp.int32)
        rel = k_pos[None, :] - q_pos[:, None]
        bias = jnp.take(rel_table, t5_bucket(rel), axis=0)
        bias = jnp.transpose(bias, (2, 0, 1))[None, :, None]
        sc = jnp.einsum('bhmqd,bhmkd->bhmqk', q_blk, k) * scale + bias
        sc = jnp.where(rel <= 0, sc, -1e30)
        p = jax.nn.softmax(sc, axis=-1)
        p = p[:, :, 0] - lam * p[:, :, 1]
        return jnp.einsum('bhqk,bhkd->bhqd', p, v)

    o = lax.map(one_block, (jnp.arange(nb, dtype=jnp.int32), qb))
    return o.transpose(1, 2, 0, 3, 4).reshape(b, h, s, -1)


def memory_attention(q, k, v):
    sc = jnp.einsum('bhqd,bhkd->bhqk', q, k) * (q.shape[-1] ** -0.5)
    p = jax.nn.softmax(sc, axis=-1)
    return jnp.einsum('bhqk,bhkd->bhqd', p, v)


def hybrid_mixer(h, mem, rel_table, w_in, b_gate, dn_conv, dn_a_log, dn_dt_bias, dn_out_norm,
                 da_q_norm, da_k_norm, da_lambda, da_subln, mem_norm, w_mem_kv,
                 mx_q_norm, mx_k_norm, w_branch, w_out, lam_init):
    f32 = jnp.float32
    b, s, d = h.shape
    proj = jnp.einsum('bsd,dc->bsc', h, w_in)
    offs = np.cumsum(IN_SIZES)[:-1].tolist()
    dn_qkv, dn_z, dn_a, dn_b, da_qkv, mx_q, gate_logits = jnp.split(proj, offs, axis=-1)

    qkv = jax.nn.silu(causal_depthwise_conv(dn_qkv.astype(f32), dn_conv.astype(f32)))
    q_a, k_a, v_a = jnp.split(qkv, [DN_HEADS * DN_DK, 2 * DN_HEADS * DN_DK], axis=-1)
    q_a = l2_norm(q_a.reshape(b, s, DN_HEADS, DN_DK)) * (DN_DK ** -0.5)
    k_a = l2_norm(k_a.reshape(b, s, DN_HEADS, DN_DK))
    v_a = v_a.reshape(b, s, DN_HEADS, DN_DV)
    beta = jax.nn.sigmoid(dn_b.astype(f32))
    g = -jnp.exp(dn_a_log.astype(f32)) * jax.nn.softplus(dn_a.astype(f32) + dn_dt_bias.astype(f32))
    o_a = chunk_gated_delta_rule(q_a.transpose(0, 2, 1, 3), k_a.transpose(0, 2, 1, 3),
                                 v_a.transpose(0, 2, 1, 3), g.transpose(0, 2, 1),
                                 beta.transpose(0, 2, 1))
    o_a = rms_norm(o_a.transpose(0, 2, 1, 3), dn_out_norm) * jax.nn.silu(
        dn_z.astype(f32).reshape(b, s, DN_HEADS, DN_DV))
    o_dn = o_a.reshape(b, s, DN_V_W)

    da_q, da_k, da_v = jnp.split(da_qkv.astype(f32), [DA_QK_W, 2 * DA_QK_W], axis=-1)
    q_b = rms_norm(da_q.reshape(b, s, DA_HEADS, 2, DA_DH), da_q_norm).transpose(0, 2, 3, 1, 4)
    k_b = rms_norm(da_k.reshape(b, s, DA_HEADS, 2, DA_DH), da_k_norm).transpose(0, 2, 3, 1, 4)
    v_b = da_v.reshape(b, s, DA_HEADS, DA_DV).transpose(0, 2, 1, 3)
    lam_p = da_lambda.astype(f32)
    lam = (jnp.exp(jnp.sum(lam_p[0] * lam_p[1])) - jnp.exp(jnp.sum(lam_p[2] * lam_p[3]))
           + lam_init)
    o_b = diff_attention(q_b, k_b, v_b, lam, rel_table.astype(f32))
    o_b = rms_norm(o_b, da_subln) * (1.0 - lam_init)
    o_da = o_b.transpose(0, 2, 1, 3).reshape(b, s, DA_V_W)

    n_mem = mem.shape[1]
    mem_n = rms_norm(mem, mem_norm)
    mkv = jnp.einsum('bmd,dc->bmc', mem_n, w_mem_kv.astype(f32))
    mk, mv = jnp.split(mkv, [MX_W], axis=-1)
    q_c = rms_norm(mx_q.astype(f32).reshape(b, s, MX_HEADS, MX_DH), mx_q_norm).transpose(0, 2, 1, 3)
    k_c = rms_norm(mk.reshape(b, n_mem, MX_HEADS, MX_DH), mx_k_norm).transpose(0, 2, 1, 3)
    v_c = mv.reshape(b, n_mem, MX_HEADS, MX_DH).transpose(0, 2, 1, 3)
    o_mx = memory_attention(q_c, k_c, v_c).transpose(0, 2, 1, 3).reshape(b, s, MX_W)

    branches = jnp.stack([o_dn, o_da, o_mx], axis=2)
    y = jnp.einsum('bsrc,rcd->bsrd', branches, w_branch.astype(f32))
    gates = jax.nn.sigmoid(gate_logits.astype(f32).reshape(b, s, N_BRANCH, d)
                           + b_gate.astype(f32).reshape(N_BRANCH, d))
    merged = jnp.sum(gates * y, axis=2)
    return jnp.einsum('bsd,de->bse', merged, w_out.astype(f32))


def moe_ffn(h, w_router, b_router, w_gate_up, b_gate_up, w_down, b_down):
    f32 = jnp.float32
    b, s, d = h.shape
    t = b * s
    xt = h.reshape(t, d).astype(f32)
    logits = xt @ w_router.astype(f32) + b_router.astype(f32)
    top_logits, top_idx = lax.top_k(logits, TOP_K)
    top_w = jax.nn.softmax(top_logits, axis=-1)
    m = t * TOP_K
    e_flat = top_idx.reshape(m).astype(jnp.int32)
    tok_flat = jnp.repeat(jnp.arange(t, dtype=jnp.int32), TOP_K)
    w_flat = top_w.reshape(m)
    order = jnp.argsort(e_flat)
    e_sorted = e_flat[order]
    counts = jnp.bincount(e_flat, length=N_EXPERTS).astype(jnp.int32)
    starts = jnp.cumsum(counts) - counts
    padded = (counts + EXPERT_BLOCK - 1) // EXPERT_BLOCK * EXPERT_BLOCK
    pends = jnp.cumsum(padded)
    pstarts = pends - padded
    dest = pstarts[e_sorted] + jnp.arange(m, dtype=jnp.int32) - starts[e_sorted]
    n_blocks = -(-m // EXPERT_BLOCK) + N_EXPERTS
    n_slots = n_blocks * EXPERT_BLOCK
    slot_tok = jnp.full((n_slots,), t, jnp.int32).at[dest].set(tok_flat[order])
    slot_w = jnp.zeros((n_slots,), f32).at[dest].set(w_flat[order])
    block_e = jnp.minimum(jnp.searchsorted(pends, jnp.arange(n_blocks, dtype=jnp.int32) * EXPERT_BLOCK,
                                           side='right'), N_EXPERTS - 1).astype(jnp.int32)
    x_pad = jnp.concatenate([xt, jnp.zeros((1, d), f32)], axis=0)

    def expert_block(args):
        tok_b, e = args
        xb = x_pad[tok_b]
        gu = xb @ w_gate_up[e].astype(f32) + b_gate_up[e].astype(f32)
        gate, up = gu[:, :D_FF], gu[:, D_FF:]
        gate = jnp.minimum(gate, SWIGLU_LIMIT)
        up = jnp.clip(up, -SWIGLU_LIMIT, SWIGLU_LIMIT)
        act = (up + 1.0) * gate * jax.nn.sigmoid(SWIGLU_ALPHA * gate)
        return act @ w_down[e].astype(f32) + b_down[e].astype(f32)

    y = lax.map(expert_block, (slot_tok.reshape(n_blocks, EXPERT_BLOCK), block_e))
    y = y.reshape(n_slots, d) * slot_w[:, None]
    out = jax.ops.segment_sum(y, slot_tok, num_segments=t + 1)[:t]
    return out.reshape(b, s, d)


def setup_inputs(seed: int = 0) -> dict:
    key = jax.random.key(seed)
    ks = jax.random.split(key, 32)
    f32 = jnp.float32

    def dense(k, shape, fan_in):
        return jax.random.normal(k, shape, f32) * fan_in ** -0.5

    def gain(k, shape):
        return 1.0 + 0.02 * jax.random.normal(k, shape, f32)

    def small(k, shape, scale):
        return scale * jax.random.normal(k, shape, f32)

    dt = jnp.exp(jax.random.uniform(ks[7], (DEPTH, DN_HEADS), f32,
                                    math.log(1e-3), math.log(1e-1)))
    return {
        'x': jax.random.normal(ks[0], (BATCH, SEQ, D_MODEL), f32),
        'mem': jax.random.normal(ks[1], (BATCH, N_MEM, D_MODEL), f32),
        'rel_table': small(ks[2], (REL_BUCKETS, DA_HEADS), 0.1),
        'attn_norm': gain(ks[3], (DEPTH, D_MODEL)),
        'w_in': dense(ks[4], (DEPTH, D_MODEL, IN_W), D_MODEL),
        'b_gate': small(ks[5], (DEPTH, N_BRANCH * D_MODEL), 0.02),
        'dn_conv': dense(ks[6], (DEPTH, DN_CONV, DN_QKV_W), DN_CONV),
        'dn_a_log': jnp.log(jax.random.uniform(ks[8], (DEPTH, DN_HEADS), f32, 1.0, 16.0)),
        'dn_dt_bias': dt + jnp.log(-jnp.expm1(-dt)),
        'dn_out_norm': gain(ks[9], (DEPTH, DN_DV)),
        'da_q_norm': gain(ks[10], (DEPTH, DA_DH)),
        'da_k_norm': gain(ks[11], (DEPTH, DA_DH)),
        'da_lambda': small(ks[12], (DEPTH, 4, DA_DH), 0.1),
        'da_subln': gain(ks[13], (DEPTH, DA_DV)),
        'mem_norm': gain(ks[14], (DEPTH, D_MODEL)),
        'w_mem_kv': dense(ks[15], (DEPTH, D_MODEL, 2 * MX_W), D_MODEL),
        'mx_q_norm': gain(ks[16], (DEPTH, MX_DH)),
        'mx_k_norm': gain(ks[17], (DEPTH, MX_DH)),
        'w_branch': dense(ks[18], (DEPTH, N_BRANCH, BRANCH_W, D_MODEL), BRANCH_W),
        'w_out': dense(ks[19], (DEPTH, D_MODEL, D_MODEL), D_MODEL),
        'ffn_norm': gain(ks[20], (DEPTH, D_MODEL)),
        'w_router': dense(ks[21], (DEPTH, D_MODEL, N_EXPERTS), D_MODEL),
        'b_router': small(ks[22], (DEPTH, N_EXPERTS), 0.01),
        'w_gate_up': dense(ks[23], (DEPTH, N_EXPERTS, D_MODEL, 2 * D_FF), D_MODEL),
        'b_gate_up': small(ks[24], (DEPTH, N_EXPERTS, 2 * D_FF), 0.02),
        'w_down': dense(ks[25], (DEPTH, N_EXPERTS, D_FF, D_MODEL), D_FF),
        'b_down': small(ks[26], (DEPTH, N_EXPERTS, D_MODEL), 0.02),
    }


def reference(x, mem, rel_table, attn_norm, w_in, b_gate, dn_conv, dn_a_log, dn_dt_bias,
              dn_out_norm, da_q_norm, da_k_norm, da_lambda, da_subln, mem_norm, w_mem_kv,
              mx_q_norm, mx_k_norm, w_branch, w_out, ffn_norm, w_router, b_router,
              w_gate_up, b_gate_up, w_down, b_down):
    for l in range(DEPTH):
        lam_init = 0.8 - 0.6 * math.exp(-0.3 * l)
        h = rms_norm(x, attn_norm[l]).astype(x.dtype)
        mix = hybrid_mixer(h, mem, rel_table, w_in[l], b_gate[l], dn_conv[l], dn_a_log[l],
                           dn_dt_bias[l], dn_out_norm[l], da_q_norm[l], da_k_norm[l],
                           da_lambda[l], da_subln[l], mem_norm[l], w_mem_kv[l],
                           mx_q_norm[l], mx_k_norm[l], w_branch[l], w_out[l], lam_init)
        x = x + mix.astype(x.dtype)
        h = rms_norm(x, ffn_norm[l]).astype(x.dtype)
        x = x + moe_ffn(h, w_router[l], b_router[l], w_gate_up[l], b_gate_up[l],
                        w_down[l], b_down[l]).astype(x.dtype)
    return x
```

```python
import functools
import math

import jax
import jax.numpy as jnp
from jax import lax
from jax.experimental import pallas as pl
from jax.experimental.pallas import tpu as pltpu

F32 = jnp.float32
BF16 = jnp.bfloat16
I32 = jnp.int32

D_MODEL = 1024
EPS = 1e-6
LANES = 128

DN_HEADS = 4
DN_DK = 128
DN_CHUNK = 64
DN_CONV = 4

DA_HEADS = 4
DA_DH = 64

MX_HEADS = 4
MX_DH = 128

REL_BUCKETS = 32
REL_MAX_DIST = 128

N_EXPERTS = 32
TOP_K = 4
D_FF = 1024
SWIGLU_LIMIT = 7.0
SWIGLU_ALPHA = 1.702
EXPERT_BLOCK = 512

LAM_INIT = 0.8 - 0.6 * math.exp(-0.3 * 0)
LOG2E = 1.4426950408889634
NEG = -1e30

P_DNQ, P_DNK, P_DNV, P_DNZ = 0, 512, 1024, 1536
P_DAQ, P_DAK, P_DAV = 2048, 2560, 3072
P_MXQ = 3584
P_GATE = 4096
P_COLS = 7168
W_AB_LO, W_AB_HI = 2048, 2056


def _dot(a, b):
    return jnp.dot(a, b, preferred_element_type=F32)


def _dot_nt(a, b):
    return lax.dot_general(a, b, (((1,), (1,)), ((), ())), preferred_element_type=F32)


def _dot_tn(a, b):
    return lax.dot_general(a, b, (((0,), (0,)), ((), ())), preferred_element_type=F32)


def _split(x):
    hi = x.astype(BF16)
    lo = (x - hi.astype(F32)).astype(BF16)
    return hi, lo


def _dot3(a, b):
    ah, al = _split(a)
    bh, bl = _split(b)
    return _dot(ah, bh) + _dot(ah, bl) + _dot(al, bh)


def _sigmoid(x):
    return 1.0 / (1.0 + jnp.exp(-x))


def _rms(x, n):
    return lax.rsqrt(jnp.sum(x * x, axis=-1, keepdims=True) * (1.0 / n) + EPS)


def _inproj_body(x_ref, g_ref, w_ref, wab_ref, p_ref, ab_ref, h_scr):
    @pl.when(pl.program_id(1) == 0)
    def _():
        x = x_ref[...]
        h = x * _rms(x, D_MODEL) * g_ref[...]
        h_scr[...] = h.astype(BF16)
        ab_ref[...] = _dot3(h, wab_ref[...])

    p_ref[...] = _dot(h_scr[...], w_ref[...]).astype(p_ref.dtype)


def _inproj(x2, gain, wp, wab):
    t = x2.shape[0]
    tm = min(1024, t)
    tn = 512
    return pl.pallas_call(
        _inproj_body,
        grid=(t // tm, P_COLS // tn),
        in_specs=[
            pl.BlockSpec((tm, D_MODEL), lambda i, j: (i, 0)),
            pl.BlockSpec((1, D_MODEL), lambda i, j: (0, 0)),
            pl.BlockSpec((D_MODEL, tn), lambda i, j: (0, j)),
            pl.BlockSpec((D_MODEL, LANES), lambda i, j: (0, 0)),
        ],
        out_specs=[
            pl.BlockSpec((tm, tn), lambda i, j: (i, j)),
            pl.BlockSpec((tm, LANES), lambda i, j: (i, 0)),
        ],
        out_shape=[
            jax.ShapeDtypeStruct((t, P_COLS), BF16),
            jax.ShapeDtypeStruct((t, LANES), F32),
        ],
        scratch_shapes=[pltpu.VMEM((tm, D_MODEL), BF16)],
        compiler_params=pltpu.CompilerParams(dimension_semantics=("parallel", "arbitrary")),
        name="inproj",
    )(x2, gain, wp, wab)


DN_HALO = 16


def _deltanet_body(q_ref, k_ref, v_ref, z_ref, qh_ref, kh_ref, vh_ref, ab_ref, cw_ref, alog_ref,
                   dtb_ref, on_ref, o_ref, stage, qs, ks, vs, s_scr):
    i = pl.program_id(1)
    tc = q_ref.shape[0]
    hw = DN_HEADS * DN_DK

    @pl.when(i == 0)
    def _():
        s_scr[...] = jnp.zeros_like(s_scr)

    for src, halo, dst, off, kind in ((q_ref, qh_ref, qs, 0, "q"), (k_ref, kh_ref, ks, hw, "k"),
                                      (v_ref, vh_ref, vs, 2 * hw, "v")):
        hal = halo[...].astype(F32)
        stage[0:DN_HALO, :] = jnp.where(i == 0, 0.0, hal)
        stage[DN_HALO:DN_HALO + tc, :] = src[...].astype(F32)
        base = DN_HALO - (DN_CONV - 1)
        y = stage[base:base + tc, :] * cw_ref[0:1, off:off + hw]
        for j in range(1, DN_CONV):
            y = y + stage[base + j:base + j + tc, :] * cw_ref[j:j + 1, off:off + hw]
        y = y * _sigmoid(y)
        if kind == "v":
            dst[...] = y
        else:
            for h in range(DN_HEADS):
                sl = slice(h * DN_DK, (h + 1) * DN_DK)
                yh = y[:, sl]
                r = lax.rsqrt(jnp.sum(yh * yh, axis=-1, keepdims=True) + EPS)
                if kind == "q":
                    r = r * (DN_DK ** -0.5)
                dst[:, sl] = yh * r

    c = DN_CHUNK
    row = lax.broadcasted_iota(I32, (c, c), 0)
    col = lax.broadcasted_iota(I32, (c, c), 1)
    incl = row >= col
    strict = row > col
    tri = jnp.where(incl, 1.0, 0.0).astype(BF16)
    eye = jnp.where(row == col, 1.0, 0.0)
    neg_a = -jnp.exp(alog_ref[...])
    dtb = dtb_ref[...]

    def chunk(ci, carry):
        r0 = pl.multiple_of(ci * c, c)
        abc = ab_ref[pl.ds(r0, c), :]
        a_in = abc + dtb
        g_all = neg_a * (jnp.maximum(a_in, 0.0) + jnp.log(1.0 + jnp.exp(-jnp.abs(a_in))))
        beta_all = _sigmoid(abc)
        zc = z_ref[pl.ds(r0, c), :].astype(F32)
        for h in range(DN_HEADS):
            sl = slice(h * DN_DK, (h + 1) * DN_DK)
            q = qs[pl.ds(r0, c), sl]
            k = ks[pl.ds(r0, c), sl]
            v = vs[pl.ds(r0, c), sl]
            g = g_all[:, h:h + 1]
            beta = beta_all[:, DN_HEADS + h:DN_HEADS + h + 1]
            g_hi, g_lo = _split(jnp.where(strict, g, 0.0))
            diff = _dot(tri, g_hi) + _dot(tri, g_lo)
            gc = diff[:, 0:1] + g[0:1, :]
            decay = jnp.where(incl, jnp.exp(diff), 0.0)
            kb = k.astype(BF16)
            kk = _dot_nt(kb, kb)
            qk = _dot_nt(q.astype(BF16), kb)
            lower = jnp.where(strict, kk * decay * beta, 0.0)
            pw = -lower
            inv = eye + pw
            for _ in range(5):
                pwb = pw.astype(BF16)
                pw = _dot(pwb, pwb)
                inv = inv + _dot(inv.astype(BF16), pw.astype(BF16))
            egc = jnp.exp(gc)
            rhs = jnp.concatenate([v * beta, k * (beta * egc)], axis=1)
            sol = _dot(inv.astype(BF16), rhs.astype(BF16))
            u = sol[:, :DN_DK]
            w = sol[:, DN_DK:]
            qkm = jnp.where(incl, qk * decay, 0.0)
            gl = gc[c - 1:c, :]
            state = s_scr[h]
            sb = state.astype(BF16)
            v_new = u - _dot(w.astype(BF16), sb)
            o = _dot((q * egc).astype(BF16), sb) + _dot(qkm.astype(BF16), v_new.astype(BF16))
            s_scr[h] = state * jnp.exp(gl) + _dot_tn(kb, (v_new * jnp.exp(gl - gc)).astype(BF16))
            zz = zc[:, sl]
            on = o * _rms(o, DN_DK) * on_ref[...]
            o_ref[pl.ds(r0, c), sl] = (on * (zz * _sigmoid(zz))).astype(o_ref.dtype)
        return carry

    lax.fori_loop(0, tc // c, chunk, 0)


def _deltanet(p, ab, conv_w, alog_row, dtb_row, out_norm, batch, seq):
    t = batch * seq
    tc = min(512, seq)
    nt = seq // tc
    hw = DN_HEADS * DN_DK
    wb = hw // hw

    def main(cb):
        return pl.BlockSpec((tc, hw), lambda b, i: (b * nt + i, cb))

    def halo(cb):
        return pl.BlockSpec(
            (DN_HALO, hw),
            lambda b, i: (jnp.maximum((b * seq + i * tc) // DN_HALO - 1, 0), cb))

    del wb
    return pl.pallas_call(
        _deltanet_body,
        grid=(batch, nt),
        in_specs=[
            main(P_DNQ // hw), main(P_DNK // hw), main(P_DNV // hw), main(P_DNZ // hw),
            halo(P_DNQ // hw), halo(P_DNK // hw), halo(P_DNV // hw),
            pl.BlockSpec((tc, LANES), lambda b, i: (b * nt + i, 0)),
            pl.BlockSpec((DN_CONV, 3 * hw), lambda b, i: (0, 0)),
            pl.BlockSpec((1, LANES), lambda b, i: (0, 0)),
            pl.BlockSpec((1, LANES), lambda b, i: (0, 0)),
            pl.BlockSpec((1, DN_DK), lambda b, i: (0, 0)),
        ],
        out_specs=pl.BlockSpec((tc, hw), lambda b, i: (b * nt + i, 0)),
        out_shape=jax.ShapeDtypeStruct((t, hw), BF16),
        scratch_shapes=[
            pltpu.VMEM((DN_HALO + tc, hw), F32),
            pltpu.VMEM((tc, hw), F32),
            pltpu.VMEM((tc, hw), F32),
            pltpu.VMEM((tc, hw), F32),
            pltpu.VMEM((DN_HEADS, DN_DK, DN_DK), F32),
        ],
        compiler_params=pltpu.CompilerParams(dimension_semantics=("parallel", "arbitrary")),
        name="deltanet",
    )(p, p, p, p, p, p, p, ab, conv_w, alog_row, dtb_row, out_norm)


ATT_BLOCK = 512


def _bias_body(tbl_ref, o_ref):
    h = pl.program_id(0)
    tq = o_ref.shape[2]
    row = lax.broadcasted_iota(I32, (tq, tq), 0)
    col = lax.broadcasted_iota(I32, (tq, tq), 1)
    max_exact = REL_BUCKETS // 2
    far = tbl_ref[h, REL_BUCKETS - 1]
    for d in range(2):
        n = row - col + d * tq
        nn = jnp.maximum(n, 0)
        nf = jnp.maximum(nn, 1).astype(F32)
        large = max_exact + (jnp.log(nf / max_exact) / math.log(REL_MAX_DIST / max_exact)
                             * (REL_BUCKETS - max_exact)).astype(I32)
        large = jnp.minimum(large, REL_BUCKETS - 1)
        bucket = jnp.where(nn < max_exact, nn, large)
        val = jnp.zeros((tq, tq), F32)
        for b in range(REL_BUCKETS):
            val = jnp.where(bucket == b, tbl_ref[h, b], val)
        o_ref[0, d] = jnp.where(n >= 0, (val - far) * LOG2E, NEG)


def _bias_tiles(tbl_t, tq):
    return pl.pallas_call(
        _bias_body,
        grid=(DA_HEADS,),
        in_specs=[pl.BlockSpec(memory_space=pltpu.SMEM)],
        out_specs=pl.BlockSpec((1, 2, tq, tq), lambda h: (h, 0, 0, 0)),
        out_shape=jax.ShapeDtypeStruct((DA_HEADS, 2, tq, tq), F32),
        name="t5_bias_tiles",
    )(tbl_t)


def _attn_body(q_ref, k_ref, v_ref, bias_ref, qg_ref, kg_ref, lam_ref, sg_ref, o_ref,
               kn, m_s, l_s, acc_s):
    qi = pl.program_id(2)
    tq = q_ref.shape[0]
    seq = k_ref.shape[0]
    tk = tq
    lo_mask = lax.broadcasted_iota(I32, (1, 2 * DA_DH), 1) < DA_DH

    def group_norm(x, gain):
        x2 = x * x
        lo = jnp.sum(jnp.where(lo_mask, x2, 0.0), axis=-1, keepdims=True)
        hi = jnp.sum(jnp.where(lo_mask, 0.0, x2), axis=-1, keepdims=True)
        r = jnp.where(lo_mask, lax.rsqrt(lo * (1.0 / DA_DH) + EPS), lax.rsqrt(hi * (1.0 / DA_DH) + EPS))
        return x * r * gain

    @pl.when(qi == 0)
    def _():
        def body(c, carry):
            r0 = pl.multiple_of(c * tk, tk)
            kn[pl.ds(r0, tk), :] = group_norm(k_ref[pl.ds(r0, tk), :].astype(F32), kg_ref[...]).astype(BF16)
            return carry
        lax.fori_loop(0, seq // tk, body, 0)

    q = group_norm(q_ref[...].astype(F32), qg_ref[...]) * (DA_DH ** -0.5 * LOG2E)
    qm = (jnp.where(lo_mask, q, 0.0).astype(BF16), jnp.where(lo_mask, 0.0, q).astype(BF16))
    m_s[...] = jnp.full(m_s.shape, NEG, F32)
    l_s[...] = jnp.zeros_like(l_s)
    acc_s[...] = jnp.zeros_like(acc_s)

    def block(j, d):
        r0 = pl.multiple_of(j * tk, tk)
        kb = kn[pl.ds(r0, tk), :]
        vb = v_ref[pl.ds(r0, tk), :]
        for m in range(2):
            s = _dot_nt(qm[m], kb)
            if d is not None:
                s = s + bias_ref[0, d]
            m_prev = m_s[m]
            m_new = jnp.maximum(m_prev, jnp.max(s, axis=-1, keepdims=True))
            alpha = jnp.exp2(m_prev - m_new)
            p = jnp.exp2(s - m_new)
            l_s[m] = alpha * l_s[m] + jnp.sum(p, axis=-1, keepdims=True)
            acc_s[m] = alpha * acc_s[m] + _dot(p.astype(BF16), vb)
            m_s[m] = m_new

    def far_block(j, carry):
        block(j, None)
        return carry

    lax.fori_loop(0, jnp.maximum(qi - 1, 0), far_block, 0)

    @pl.when(qi >= 1)
    def _():
        block(qi - 1, 1)

    block(qi, 0)

    lam_p = lam_ref[...]
    lam = (jnp.exp(jnp.sum(lam_p[0:1, :] * lam_p[1:2, :], axis=-1, keepdims=True))
           - jnp.exp(jnp.sum(lam_p[2:3, :] * lam_p[3:4, :], axis=-1, keepdims=True)) + LAM_INIT)
    o = acc_s[0] / l_s[0] - lam * (acc_s[1] / l_s[1])
    o_ref[...] = (o * _rms(o, 2 * DA_DH) * sg_ref[...] * (1.0 - LAM_INIT)).astype(o_ref.dtype)


def _attention(p, bias, qg, kg, lam_p, subln, batch, seq):
    t = batch * seq
    tq = min(ATT_BLOCK, seq)
    nq = seq // tq
    dv = 2 * DA_DH
    return pl.pallas_call(
        _attn_body,
        grid=(batch, DA_HEADS, nq),
        in_specs=[
            pl.BlockSpec((tq, dv), lambda b, h, i: (b * nq + i, P_DAQ // dv + h)),
            pl.BlockSpec((seq, dv), lambda b, h, i: (b, P_DAK // dv + h)),
            pl.BlockSpec((seq, dv), lambda b, h, i: (b, P_DAV // dv + h)),
            pl.BlockSpec((1, 2, tq, tq), lambda b, h, i: (h, 0, 0, 0)),
            pl.BlockSpec((1, dv), lambda b, h, i: (0, 0)),
            pl.BlockSpec((1, dv), lambda b, h, i: (0, 0)),
            pl.BlockSpec((4, DA_DH), lambda b, h, i: (0, 0)),
            pl.BlockSpec((1, dv), lambda b, h, i: (0, 0)),
        ],
        out_specs=pl.BlockSpec((tq, dv), lambda b, h, i: (b * nq + i, h)),
        out_shape=jax.ShapeDtypeStruct((t, DA_HEADS * dv), BF16),
        scratch_shapes=[
            pltpu.VMEM((seq, dv), BF16),
            pltpu.VMEM((2, tq, 1), F32),
            pltpu.VMEM((2, tq, 1), F32),
            pltpu.VMEM((2, tq, dv), F32),
        ],
        compiler_params=pltpu.CompilerParams(dimension_semantics=("parallel", "parallel", "arbitrary")),
        name="diff_attention",
    )(p, p, p, bias, qg, kg, lam_p, subln)


def _memkv_body(mem_ref, mg_ref, w_ref, kg_ref, mk_ref, mv_ref):
    x = mem_ref[0]
    xn = x * _rms(x, D_MODEL) * mg_ref[...]
    kv = _dot(xn.astype(BF16), w_ref[...])
    hw = MX_HEADS * MX_DH
    for h in range(MX_HEADS):
        sl = slice(h * MX_DH, (h + 1) * MX_DH)
        kh = kv[:, sl]
        mk_ref[0, :, sl] = (kh * _rms(kh, MX_DH) * kg_ref[...]).astype(BF16)
    mv_ref[0] = kv[:, hw:].astype(BF16)


def _memkv(mem, mem_norm, w_kv, k_norm):
    b, n, _ = mem.shape
    hw = MX_HEADS * MX_DH
    return pl.pallas_call(
        _memkv_body,
        grid=(b,),
        in_specs=[
            pl.BlockSpec((1, n, D_MODEL), lambda i: (i, 0, 0)),
            pl.BlockSpec((1, D_MODEL), lambda i: (0, 0)),
            pl.BlockSpec((D_MODEL, 2 * hw), lambda i: (0, 0)),
            pl.BlockSpec((1, MX_DH), lambda i: (0, 0)),
        ],
        out_specs=[pl.BlockSpec((1, n, hw), lambda i: (i, 0, 0))] * 2,
        out_shape=[jax.ShapeDtypeStruct((b, n, hw), BF16)] * 2,
        name="memory_kv",
    )(mem, mem_norm, w_kv, k_norm)


def _merge_body(x_ref, odn_ref, oda_ref, mxq_ref, g0_ref, g1_ref, g2_ref, mk_ref, mv_ref, qg_ref,
                bg_ref, wb_ref, wo_ref, out_ref, omx):
    for h in range(MX_HEADS):
        sl = slice(h * MX_DH, (h + 1) * MX_DH)
        qh = mxq_ref[:, sl].astype(F32)
        qh = qh * _rms(qh, MX_DH) * qg_ref[...] * (MX_DH ** -0.5 * LOG2E)
        s = _dot_nt(qh.astype(BF16), mk_ref[0, :, sl])
        p = jnp.exp2(s - jnp.max(s, axis=-1, keepdims=True))
        oh = _dot(p.astype(BF16), mv_ref[0, :, sl]) / jnp.sum(p, axis=-1, keepdims=True)
        omx[:, sl] = oh.astype(BF16)
    y = None
    for r, (o_r, g_r) in enumerate(((odn_ref, g0_ref), (oda_ref, g1_ref), (omx, g2_ref))):
        gate = _sigmoid(g_r[...].astype(F32) + bg_ref[r:r + 1, :])
        term = gate * _dot(o_r[...], wb_ref[r])
        y = term if y is None else y + term
    out_ref[...] = x_ref[...] + _dot(y.astype(BF16), wo_ref[...])


def _merge(x2, o_dn, o_da, p, mk, mv, q_norm, b_gate, w_branch, w_out, seq):
    t = x2.shape[0]
    tm = min(512, seq)
    nt = seq // tm
    bw = 512
    n_mem = mk.shape[1]
    return pl.pallas_call(
        _merge_body,
        grid=(t // tm,),
        in_specs=[
            pl.BlockSpec((tm, D_MODEL), lambda i: (i, 0)),
            pl.BlockSpec((tm, bw), lambda i: (i, 0)),
            pl.BlockSpec((tm, bw), lambda i: (i, 0)),
            pl.BlockSpec((tm, bw), lambda i: (i, P_MXQ // bw)),
            pl.BlockSpec((tm, D_MODEL), lambda i: (i, P_GATE // D_MODEL)),
            pl.BlockSpec((tm, D_MODEL), lambda i: (i, P_GATE // D_MODEL + 1)),
            pl.BlockSpec((tm, D_MODEL), lambda i: (i, P_GATE // D_MODEL + 2)),
            pl.BlockSpec((1, n_mem, bw), lambda i: (i // nt, 0, 0)),
            pl.BlockSpec((1, n_mem, bw), lambda i: (i // nt, 0, 0)),
            pl.BlockSpec((1, MX_DH), lambda i: (0, 0)),
            pl.BlockSpec((3, D_MODEL), lambda i: (0, 0)),
            pl.BlockSpec((3, bw, D_MODEL), lambda i: (0, 0, 0)),
            pl.BlockSpec((D_MODEL, D_MODEL), lambda i: (0, 0)),
        ],
        out_specs=pl.BlockSpec((tm, D_MODEL), lambda i: (i, 0)),
        out_shape=jax.ShapeDtypeStruct((t, D_MODEL), F32),
        scratch_shapes=[pltpu.VMEM((tm, bw), BF16)],
        compiler_params=pltpu.CompilerParams(dimension_semantics=("parallel",)),
        name="merge",
    )(x2, o_dn, o_da, p, p, p, p, mk, mv, q_norm, b_gate, w_branch, w_out)


def _router_body(x_ref, g_ref, wr_ref, br_ref, h_ref, idx_ref, wt_ref, rank_ref, cnt_ref, carry):
    i = pl.program_id(0)
    tm = x_ref.shape[0]

    @pl.when(i == 0)
    def _():
        carry[...] = jnp.zeros_like(carry)

    x = x_ref[...]
    h = x * _rms(x, D_MODEL) * g_ref[...]
    h_ref[...] = h
    logits = _dot3(h, wr_ref[...]) + br_ref[...]
    lane = lax.broadcasted_iota(I32, (tm, LANES), 1)
    lane_f = lane.astype(F32)
    work = logits
    sel = jnp.zeros((tm, LANES), F32)
    vals, idxs = [], []
    for _ in range(TOP_K):
        mx = jnp.max(work, axis=-1, keepdims=True)
        ik = jnp.min(jnp.where(work == mx, lane_f, float(LANES)), axis=-1, keepdims=True)
        hit = lane_f == ik
        sel = jnp.where(hit, 1.0, sel)
        work = jnp.where(hit, -jnp.inf, work)
        vals.append(mx)
        idxs.append(ik)
    es = [jnp.exp(v - vals[0]) for v in vals]
    den = es[0] + es[1] + es[2] + es[3]
    r = lax.broadcasted_iota(I32, (tm, tm), 0)
    c = lax.broadcasted_iota(I32, (tm, tm), 1)
    tril = jnp.where(r > c, 1.0, 0.0).astype(BF16)
    cum = _dot(tril, sel.astype(BF16)) + carry[0:1, :]
    idx_o = jnp.zeros((tm, LANES), F32)
    wt_o = jnp.zeros((tm, LANES), F32)
    rank_o = jnp.zeros((tm, LANES), F32)
    for k in range(TOP_K):
        rk = jnp.sum(jnp.where(lane_f == idxs[k], cum, 0.0), axis=-1, keepdims=True)
        idx_o = jnp.where(lane == k, idxs[k], idx_o)
        wt_o = jnp.where(lane == k, es[k] / den, wt_o)
        rank_o = jnp.where(lane == k, rk, rank_o)
    idx_ref[...] = idx_o.astype(I32)
    wt_ref[...] = wt_o
    rank_ref[...] = rank_o.astype(I32)
    new = carry[...] + jnp.sum(sel, axis=0, keepdims=True)
    carry[...] = new
    cnt_ref[...] = new


def _router(x1, gain, w_r, b_r):
    t = x1.shape[0]
    tm = min(512, t)
    row = lambda i: (i, 0)
    fixed = lambda i: (0, 0)
    return pl.pallas_call(
        _router_body,
        grid=(t // tm,),
        in_specs=[
            pl.BlockSpec((tm, D_MODEL), row),
            pl.BlockSpec((1, D_MODEL), fixed),
            pl.BlockSpec((D_MODEL, LANES), fixed),
            pl.BlockSpec((1, LANES), fixed),
        ],
        out_specs=[
            pl.BlockSpec((tm, D_MODEL), row),
            pl.BlockSpec((tm, LANES), row),
            pl.BlockSpec((tm, LANES), row),
            pl.BlockSpec((tm, LANES), row),
            pl.BlockSpec((8, LANES), fixed),
        ],
        out_shape=[
            jax.ShapeDtypeStruct((t, D_MODEL), F32),
            jax.ShapeDtypeStruct((t, LANES), I32),
            jax.ShapeDtypeStruct((t, LANES), F32),
            jax.ShapeDtypeStruct((t, LANES), I32),
            jax.ShapeDtypeStruct((8, LANES), F32),
        ],
        scratch_shapes=[pltpu.VMEM((8, LANES), F32)],
        compiler_params=pltpu.CompilerParams(dimension_semantics=("arbitrary",)),
        name="router",
    )(x1, gain, w_r, b_r)


def _plan_body(cnt_ref, idx_ref, rank_ref, dest_ref, meta_ref):
    i = pl.program_id(0)
    tm = idx_ref.shape[0]
    eb = float(EXPERT_BLOCK)
    cnt = cnt_ref[...]
    lane8 = lax.broadcasted_iota(I32, (8, LANES), 1)
    padded = jnp.where(lane8 < N_EXPERTS, jnp.floor((cnt + (eb - 1.0)) * (1.0 / eb)) * eb, 0.0)
    pends = padded
    for s in (1, 2, 4, 8, 16):
        pends = pends + jnp.where(lane8 >= s, pltpu.roll(pends, s, axis=1), 0.0)
    pstarts = (pends - padded)[0:1, :]
    lane = lax.broadcasted_iota(I32, (tm, LANES), 1)
    idx = idx_ref[...]
    rank = rank_ref[...].astype(F32)
    dest = jnp.zeros((tm, LANES), F32)
    for k in range(TOP_K):
        base = jnp.sum(jnp.where(lane == idx[:, k:k + 1], pstarts, 0.0), axis=-1, keepdims=True)
        dest = jnp.where(lane == k, base + rank[:, k:k + 1], dest)
    dest_ref[...] = dest.astype(I32)

    @pl.when(i == 0)
    def _():
        nb = meta_ref.shape[0]
        ln = lax.broadcasted_iota(I32, (nb, LANES), 1)
        blk = lax.broadcasted_iota(I32, (nb, LANES), 0).astype(F32) * eb
        pe = pends[0:1, :]
        be = jnp.sum(jnp.where((ln < N_EXPERTS) & (pe <= blk), 1.0, 0.0), axis=-1, keepdims=True)
        be = jnp.minimum(be, float(N_EXPERTS - 1))
        used = jnp.sum(jnp.where(ln == N_EXPERTS - 1, pe, 0.0), axis=-1, keepdims=True) * (1.0 / eb)
        meta_ref[...] = jnp.where(ln == 0, be, jnp.where(ln == 1, used, 0.0)).astype(I32)


def _plan(counts, idx, rank, n_blocks_pad):
    t = idx.shape[0]
    tm = min(1024, t)
    row = lambda i: (i, 0)
    fixed = lambda i: (0, 0)
    return pl.pallas_call(
        _plan_body,
        grid=(t // tm,),
        in_specs=[pl.BlockSpec((8, LANES), fixed), pl.BlockSpec((tm, LANES), row),
                  pl.BlockSpec((tm, LANES), row)],
        out_specs=[pl.BlockSpec((tm, LANES), row), pl.BlockSpec((n_blocks_pad, LANES), fixed)],
        out_shape=[jax.ShapeDtypeStruct((t, LANES), I32),
                   jax.ShapeDtypeStruct((n_blocks_pad, LANES), I32)],
        compiler_params=pltpu.CompilerParams(dimension_semantics=("arbitrary",)),
        name="dispatch_plan",
    )(counts, idx, rank)


def _row_copy(src_ref, src_row, dst_ref, dst_row, sem):
    return pltpu.make_async_copy(src_ref.at[pl.ds(src_row, 1), :], dst_ref.at[pl.ds(dst_row, 1), :], sem)


def _dispatch_body(dest_ref, h_ref, xs_in_ref, xs_ref, sem):
    del xs_in_ref
    n = dest_ref.shape[2]

    def issue(pidx, carry):
        _row_copy(h_ref, pidx // TOP_K, xs_ref, dest_ref[0, 0, pidx], sem).start()
        return carry

    lax.fori_loop(0, n, issue, 0)

    def drain(pidx, carry):
        _row_copy(h_ref, 0, xs_ref, 0, sem).wait()
        return carry

    lax.fori_loop(0, n, drain, 0)


def _dispatch(dest3, h2, xs_init):
    t = h2.shape[0]
    tm = dest3.shape[2] // TOP_K
    return pl.pallas_call(
        _dispatch_body,
        grid=(t // tm,),
        in_specs=[
            pl.BlockSpec((1, 1, TOP_K * tm), lambda i: (i, 0, 0), memory_space=pltpu.SMEM),
            pl.BlockSpec((tm, D_MODEL), lambda i: (i, 0)),
            pl.BlockSpec(memory_space=pl.ANY),
        ],
        out_specs=pl.BlockSpec(memory_space=pl.ANY),
        out_shape=jax.ShapeDtypeStruct(xs_init.shape, xs_init.dtype),
        scratch_shapes=[pltpu.SemaphoreType.DMA(())],
        input_output_aliases={2: 0},
        compiler_params=pltpu.CompilerParams(dimension_semantics=("arbitrary",)),
        name="moe_dispatch",
    )(dest3, h2, xs_init)


def _expert_body(be_ref, nu_ref, x_ref, wgu_ref, bgu_ref, wd_ref, bd_ref, y_ref):
    del be_ref

    @pl.when(pl.program_id(0) < nu_ref[0])
    def _():
        gu = _dot(x_ref[...].astype(BF16), wgu_ref[0]) + bgu_ref[0]
        gate = jnp.minimum(gu[:, :D_FF], SWIGLU_LIMIT)
        up = jnp.clip(gu[:, D_FF:], -SWIGLU_LIMIT, SWIGLU_LIMIT)
        act = (up + 1.0) * gate * _sigmoid(SWIGLU_ALPHA * gate)
        y_ref[...] = _dot(act.astype(BF16), wd_ref[0]) + bd_ref[0]


def _experts(block_e, n_used, xs, wgu, bgu, wd, bd):
    n_slots = xs.shape[0]
    nb = n_slots // EXPERT_BLOCK

    def blk(j, be, nu):
        return jnp.minimum(j, nu[0] - 1)

    grid_spec = pltpu.PrefetchScalarGridSpec(
        num_scalar_prefetch=2,
        grid=(nb,),
        in_specs=[
            pl.BlockSpec((EXPERT_BLOCK, D_MODEL), lambda j, be, nu: (blk(j, be, nu), 0)),
            pl.BlockSpec((1, D_MODEL, 2 * D_FF), lambda j, be, nu: (be[blk(j, be, nu)], 0, 0)),
            pl.BlockSpec((1, 1, 2 * D_FF), lambda j, be, nu: (be[blk(j, be, nu)], 0, 0)),
            pl.BlockSpec((1, D_FF, D_MODEL), lambda j, be, nu: (be[blk(j, be, nu)], 0, 0)),
            pl.BlockSpec((1, 1, D_MODEL), lambda j, be, nu: (be[blk(j, be, nu)], 0, 0)),
        ],
        out_specs=pl.BlockSpec((EXPERT_BLOCK, D_MODEL), lambda j, be, nu: (blk(j, be, nu), 0)),
    )
    return pl.pallas_call(
        _expert_body,
        grid_spec=grid_spec,
        out_shape=jax.ShapeDtypeStruct((n_slots, D_MODEL), F32),
        compiler_params=pltpu.CompilerParams(dimension_semantics=("arbitrary",)),
        name="moe_experts",
    )(block_e, n_used, xs, wgu, bgu, wd, bd)


def _combine_body(dest_ref, x_ref, wt_ref, y_ref, out_ref, ybuf, sem):
    n = dest_ref.shape[2]
    tm = x_ref.shape[0]

    def issue(pidx, carry):
        t = pidx // TOP_K
        k = pidx - t * TOP_K
        _row_copy(y_ref, dest_ref[0, 0, pidx], ybuf, k * tm + t, sem).start()
        return carry

    lax.fori_loop(0, n, issue, 0)

    def drain(pidx, carry):
        _row_copy(y_ref, 0, ybuf, 0, sem).wait()
        return carry

    lax.fori_loop(0, n, drain, 0)

    acc = x_ref[...]
    wt = wt_ref[...]
    for k in range(TOP_K):
        acc = acc + wt[:, k:k + 1] * ybuf[k * tm:(k + 1) * tm, :]
    out_ref[...] = acc


def _combine(dest3, x1, wts, y):
    t = x1.shape[0]
    tm = dest3.shape[2] // TOP_K
    return pl.pallas_call(
        _combine_body,
        grid=(t // tm,),
        in_specs=[
            pl.BlockSpec((1, 1, TOP_K * tm), lambda i: (i, 0, 0), memory_space=pltpu.SMEM),
            pl.BlockSpec((tm, D_MODEL), lambda i: (i, 0)),
            pl.BlockSpec((tm, LANES), lambda i: (i, 0)),
            pl.BlockSpec(memory_space=pl.ANY),
        ],
        out_specs=pl.BlockSpec((tm, D_MODEL), lambda i: (i, 0)),
        out_shape=jax.ShapeDtypeStruct((t, D_MODEL), F32),
        scratch_shapes=[pltpu.VMEM((TOP_K * tm, D_MODEL), F32), pltpu.SemaphoreType.DMA(())],
        compiler_params=pltpu.CompilerParams(dimension_semantics=("arbitrary",)),
        name="moe_combine",
    )(dest3, x1, wts, y)


def _pad_lanes(v, fill=0.0):
    v = v.astype(F32).reshape(1, -1)
    return jnp.pad(v, ((0, 0), (0, LANES - v.shape[1])), constant_values=fill)


def _mixer(x2, mem, rel_table, attn_norm, w_in, b_gate, dn_conv, dn_a_log, dn_dt_bias, dn_out_norm,
           da_q_norm, da_k_norm, da_lambda, da_subln, mem_norm, w_mem_kv, mx_q_norm, mx_k_norm,
           w_branch, w_out, batch, seq):
    wp = jnp.concatenate([w_in[:, :W_AB_LO], w_in[:, W_AB_HI:]], axis=1).astype(BF16)
    wab = jnp.pad(w_in[:, W_AB_LO:W_AB_HI], ((0, 0), (0, LANES - (W_AB_HI - W_AB_LO))))
    p, ab = _inproj(x2, attn_norm.reshape(1, -1), wp, wab)

    o_dn = _deltanet(p, ab, dn_conv, _pad_lanes(dn_a_log), _pad_lanes(dn_dt_bias),
                     dn_out_norm.reshape(1, -1), batch, seq)

    tq = min(ATT_BLOCK, seq)
    bias = _bias_tiles(rel_table.T, tq)
    o_da = _attention(p, bias, jnp.tile(da_q_norm, 2).reshape(1, -1), jnp.tile(da_k_norm, 2).reshape(1, -1),
                      da_lambda, da_subln.reshape(1, -1), batch, seq)

    mk, mv = _memkv(mem, mem_norm.reshape(1, -1), w_mem_kv.astype(BF16), mx_k_norm.reshape(1, -1))
    return _merge(x2, o_dn, o_da, p, mk, mv, mx_q_norm.reshape(1, -1), b_gate.reshape(3, D_MODEL),
                  w_branch.astype(BF16), w_out.astype(BF16), seq)


def _moe(x1, ffn_norm, w_router, b_router, w_gate_up, b_gate_up, w_down, b_down):
    t = x1.shape[0]
    m = t * TOP_K
    n_blocks = -(-m // EXPERT_BLOCK) + N_EXPERTS
    n_blocks_pad = -(-n_blocks // 8) * 8
    n_slots = n_blocks * EXPERT_BLOCK

    wr = jnp.pad(w_router, ((0, 0), (0, LANES - N_EXPERTS)))
    h2, idx, wts, rank, counts = _router(x1, ffn_norm.reshape(1, -1), wr, _pad_lanes(b_router, NEG))
    dest, meta = _plan(counts, idx, rank, n_blocks_pad)
    block_e = meta[:n_blocks, 0]
    n_used = meta[0:1, 1]

    tm_d = min(512, t)
    dest_d = dest[:, :TOP_K].reshape(t // tm_d, 1, TOP_K * tm_d)
    xs = _dispatch(dest_d, h2, jnp.zeros((n_slots, D_MODEL), F32))
    y = _experts(block_e, n_used, xs, w_gate_up.astype(BF16), b_gate_up.reshape(N_EXPERTS, 1, -1),
                 w_down.astype(BF16), b_down.reshape(N_EXPERTS, 1, -1))
    tm_c = min(256, t)
    dest_c = dest[:, :TOP_K].reshape(t // tm_c, 1, TOP_K * tm_c)
    return _combine(dest_c, x1, wts, y)


def kernel(x, mem, rel_table, attn_norm, w_in, b_gate, dn_conv, dn_a_log, dn_dt_bias, dn_out_norm,
           da_q_norm, da_k_norm, da_lambda, da_subln, mem_norm, w_mem_kv, mx_q_norm, mx_k_norm,
           w_branch, w_out, ffn_norm, w_router, b_router, w_gate_up, b_gate_up, w_down, b_down):
    batch, seq, d = x.shape
    x2 = x.reshape(batch * seq, d)
    x1 = _mixer(x2, mem, rel_table, attn_norm[0], w_in[0], b_gate[0], dn_conv[0], dn_a_log[0],
                dn_dt_bias[0], dn_out_norm[0], da_q_norm[0], da_k_norm[0], da_lambda[0], da_subln[0],
                mem_norm[0], w_mem_kv[0], mx_q_norm[0], mx_k_norm[0], w_branch[0], w_out[0], batch, seq)
    out = _moe(x1, ffn_norm[0], w_router[0], b_router[0], w_gate_up[0], b_gate_up[0], w_down[0],
               b_down[0])
    return out.reshape(batch, seq, d)
```

```python
import functools
import math

import jax
import jax.numpy as jnp
from jax import lax
from jax.experimental import pallas as pl
from jax.experimental.pallas import tpu as pltpu

F32 = jnp.float32
BF16 = jnp.bfloat16
I32 = jnp.int32

D_MODEL = 1024
EPS = 1e-6
LANES = 128

DN_HEADS = 4
DN_DK = 128
DN_CHUNK = 64
DN_CONV = 4

DA_HEADS = 4
DA_DH = 64

MX_HEADS = 4
MX_DH = 128

REL_BUCKETS = 32
REL_MAX_DIST = 128

N_EXPERTS = 32
TOP_K = 4
D_FF = 1024
SWIGLU_LIMIT = 7.0
SWIGLU_ALPHA = 1.702
EXPERT_BLOCK = 512

LAM_INIT = 0.8 - 0.6 * math.exp(-0.3 * 0)
LOG2E = 1.4426950408889634
NEG = -1e30

P_DNQ, P_DNK, P_DNV, P_DNZ = 0, 512, 1024, 1536
P_DAQ, P_DAK, P_DAV = 2048, 2560, 3072
P_MXQ = 3584
P_GATE = 4096
P_COLS = 7168
W_AB_LO, W_AB_HI = 2048, 2056


def _dot(a, b):
    return jnp.dot(a, b, preferred_element_type=F32)


def _dot_nt(a, b):
    return lax.dot_general(a, b, (((1,), (1,)), ((), ())), preferred_element_type=F32)


def _dot_tn(a, b):
    return lax.dot_general(a, b, (((0,), (0,)), ((), ())), preferred_element_type=F32)


def _split(x):
    hi = x.astype(BF16)
    lo = (x - hi.astype(F32)).astype(BF16)
    return hi, lo


def _dot3(a, b):
    ah, al = _split(a)
    bh, bl = _split(b)
    return _dot(ah, bh) + _dot(ah, bl) + _dot(al, bh)


def _sigmoid(x):
    return 1.0 / (1.0 + jnp.exp(-x))


def _rms(x, n):
    return lax.rsqrt(jnp.sum(x * x, axis=-1, keepdims=True) * (1.0 / n) + EPS)


def _inproj_body(x_ref, g_ref, w_ref, wab_ref, p_ref, ab_ref, h_scr):
    @pl.when(pl.program_id(1) == 0)
    def _():
        x = x_ref[...]
        h = x * _rms(x, D_MODEL) * g_ref[...]
        h_scr[...] = h.astype(BF16)
        ab_ref[...] = _dot3(h, wab_ref[...])

    p_ref[...] = _dot(h_scr[...], w_ref[...]).astype(p_ref.dtype)


def _inproj(x2, gain, wp, wab):
    t = x2.shape[0]
    tm = min(1024, t)
    tn = 512
    return pl.pallas_call(
        _inproj_body,
        grid=(t // tm, P_COLS // tn),
        in_specs=[
            pl.BlockSpec((tm, D_MODEL), lambda i, j: (i, 0)),
            pl.BlockSpec((1, D_MODEL), lambda i, j: (0, 0)),
            pl.BlockSpec((D_MODEL, tn), lambda i, j: (0, j)),
            pl.BlockSpec((D_MODEL, LANES), lambda i, j: (0, 0)),
        ],
        out_specs=[
            pl.BlockSpec((tm, tn), lambda i, j: (i, j)),
            pl.BlockSpec((tm, LANES), lambda i, j: (i, 0)),
        ],
        out_shape=[
            jax.ShapeDtypeStruct((t, P_COLS), BF16),
            jax.ShapeDtypeStruct((t, LANES), F32),
        ],
        scratch_shapes=[pltpu.VMEM((tm, D_MODEL), BF16)],
        compiler_params=pltpu.CompilerParams(dimension_semantics=("parallel", "arbitrary")),
        name="inproj",
    )(x2, gain, wp, wab)


DN_HALO = 16


def _deltanet_body(q_ref, k_ref, v_ref, z_ref, qh_ref, kh_ref, vh_ref, ab_ref, cw_ref, alog_ref,
                   dtb_ref, on_ref, o_ref, stage, qs, ks, vs, s_scr):
    i = pl.program_id(1)
    tc = q_ref.shape[0]
    hw = DN_HEADS * DN_DK

    @pl.when(i == 0)
    def _():
        s_scr[...] = jnp.zeros_like(s_scr)

    for src, halo, dst, off, kind in ((q_ref, qh_ref, qs, 0, "q"), (k_ref, kh_ref, ks, hw, "k"),
                                      (v_ref, vh_ref, vs, 2 * hw, "v")):
        hal = halo[...].astype(F32)
        stage[0:DN_HALO, :] = jnp.where(i == 0, 0.0, hal)
        stage[DN_HALO:DN_HALO + tc, :] = src[...].astype(F32)
        base = DN_HALO - (DN_CONV - 1)
        y = stage[base:base + tc, :] * cw_ref[0:1, off:off + hw]
        for j in range(1, DN_CONV):
            y = y + stage[base + j:base + j + tc, :] * cw_ref[j:j + 1, off:off + hw]
        y = y * _sigmoid(y)
        if kind == "v":
            dst[...] = y
        else:
            for h in range(DN_HEADS):
                sl = slice(h * DN_DK, (h + 1) * DN_DK)
                yh = y[:, sl]
                r = lax.rsqrt(jnp.sum(yh * yh, axis=-1, keepdims=True) + EPS)
                if kind == "q":
                    r = r * (DN_DK ** -0.5)
                dst[:, sl] = yh * r

    c = DN_CHUNK
    row = lax.broadcasted_iota(I32, (c, c), 0)
    col = lax.broadcasted_iota(I32, (c, c), 1)
    incl = row >= col
    strict = row > col
    tri = jnp.where(incl, 1.0, 0.0).astype(BF16)
    eye = jnp.where(row == col, 1.0, 0.0)
    neg_a = -jnp.exp(alog_ref[...])
    dtb = dtb_ref[...]

    def chunk(ci, carry):
        r0 = pl.multiple_of(ci * c, c)
        abc = ab_ref[pl.ds(r0, c), :]
        a_in = abc + dtb
        g_all = neg_a * (jnp.maximum(a_in, 0.0) + jnp.log(1.0 + jnp.exp(-jnp.abs(a_in))))
        beta_all = _sigmoid(abc)
        zc = z_ref[pl.ds(r0, c), :].astype(F32)
        for h in range(DN_HEADS):
            sl = slice(h * DN_DK, (h + 1) * DN_DK)
            q = qs[pl.ds(r0, c), sl]
            k = ks[pl.ds(r0, c), sl]
            v = vs[pl.ds(r0, c), sl]
            g = g_all[:, h:h + 1]
            beta = beta_all[:, DN_HEADS + h:DN_HEADS + h + 1]
            g_hi, g_lo = _split(jnp.where(strict, g, 0.0))
            diff = _dot(tri, g_hi) + _dot(tri, g_lo)
            gc = diff[:, 0:1] + g[0:1, :]
            decay = jnp.where(incl, jnp.exp(diff), 0.0)
            kb = k.astype(BF16)
            kk = _dot_nt(kb, kb)
            qk = _dot_nt(q.astype(BF16), kb)
            lower = jnp.where(strict, kk * decay * beta, 0.0)
            pw = -lower
            inv = eye + pw
            for _ in range(5):
                pwb = pw.astype(BF16)
                pw = _dot(pwb, pwb)
                inv = inv + _dot(inv.astype(BF16), pw.astype(BF16))
            egc = jnp.exp(gc)
            rhs = jnp.concatenate([v * beta, k * (beta * egc)], axis=1)
            sol = _dot(inv.astype(BF16), rhs.astype(BF16))
            u = sol[:, :DN_DK]
            w = sol[:, DN_DK:]
            qkm = jnp.where(incl, qk * decay, 0.0)
            gl = gc[c - 1:c, :]
            state = s_scr[h]
            sb = state.astype(BF16)
            v_new = u - _dot(w.astype(BF16), sb)
            o = _dot((q * egc).astype(BF16), sb) + _dot(qkm.astype(BF16), v_new.astype(BF16))
            s_scr[h] = state * jnp.exp(gl) + _dot_tn(kb, (v_new * jnp.exp(gl - gc)).astype(BF16))
            zz = zc[:, sl]
            on = o * _rms(o, DN_DK) * on_ref[...]
            o_ref[pl.ds(r0, c), sl] = (on * (zz * _sigmoid(zz))).astype(o_ref.dtype)
        return carry

    lax.fori_loop(0, tc // c, chunk, 0)


def _deltanet(p, ab, conv_w, alog_row, dtb_row, out_norm, batch, seq):
    t = batch * seq
    tc = min(512, seq)
    nt = seq // tc
    hw = DN_HEADS * DN_DK
    wb = hw // hw

    def main(cb):
        return pl.BlockSpec((tc, hw), lambda b, i: (b * nt + i, cb))

    def halo(cb):
        return pl.BlockSpec(
            (DN_HALO, hw),
            lambda b, i: (jnp.maximum((b * seq + i * tc) // DN_HALO - 1, 0), cb))

    del wb
    return pl.pallas_call(
        _deltanet_body,
        grid=(batch, nt),
        in_specs=[
            main(P_DNQ // hw), main(P_DNK // hw), main(P_DNV // hw), main(P_DNZ // hw),
            halo(P_DNQ // hw), halo(P_DNK // hw), halo(P_DNV // hw),
            pl.BlockSpec((tc, LANES), lambda b, i: (b * nt + i, 0)),
            pl.BlockSpec((DN_CONV, 3 * hw), lambda b, i: (0, 0)),
            pl.BlockSpec((1, LANES), lambda b, i: (0, 0)),
            pl.BlockSpec((1, LANES), lambda b, i: (0, 0)),
            pl.BlockSpec((1, DN_DK), lambda b, i: (0, 0)),
        ],
        out_specs=pl.BlockSpec((tc, hw), lambda b, i: (b * nt + i, 0)),
        out_shape=jax.ShapeDtypeStruct((t, hw), BF16),
        scratch_shapes=[
            pltpu.VMEM((DN_HALO + tc, hw), F32),
            pltpu.VMEM((tc, hw), F32),
            pltpu.VMEM((tc, hw), F32),
            pltpu.VMEM((tc, hw), F32),
            pltpu.VMEM((DN_HEADS, DN_DK, DN_DK), F32),
        ],
        compiler_params=pltpu.CompilerParams(dimension_semantics=("parallel", "arbitrary")),
        name="deltanet",
    )(p, p, p, p, p, p, p, ab, conv_w, alog_row, dtb_row, out_norm)


ATT_BLOCK = 512


def _bias_body(tbl_ref, o_ref):
    h = pl.program_id(0)
    tq = o_ref.shape[2]
    key = lax.broadcasted_iota(I32, (tq, tq), 0)
    qry = lax.broadcasted_iota(I32, (tq, tq), 1)
    max_exact = REL_BUCKETS // 2
    far = tbl_ref[h, REL_BUCKETS - 1]
    for d in range(2):
        n = qry - key + d * tq
        nn = jnp.maximum(n, 0)
        nf = jnp.maximum(nn, 1).astype(F32)
        large = max_exact + (jnp.log(nf / max_exact) / math.log(REL_MAX_DIST / max_exact)
                             * (REL_BUCKETS - max_exact)).astype(I32)
        large = jnp.minimum(large, REL_BUCKETS - 1)
        bucket = jnp.where(nn < max_exact, nn, large)
        val = jnp.zeros((tq, tq), F32)
        for b in range(REL_BUCKETS):
            val = jnp.where(bucket == b, tbl_ref[h, b], val)
        o_ref[0, d] = jnp.where(n >= 0, (val - far) * LOG2E, NEG)


def _bias_tiles(tbl_t, tq):
    return pl.pallas_call(
        _bias_body,
        grid=(DA_HEADS,),
        in_specs=[pl.BlockSpec(memory_space=pltpu.SMEM)],
        out_specs=pl.BlockSpec((1, 2, tq, tq), lambda h: (h, 0, 0, 0)),
        out_shape=jax.ShapeDtypeStruct((DA_HEADS, 2, tq, tq), F32),
        name="t5_bias_tiles",
    )(tbl_t)


DA_DV = 2 * DA_DH
DA_VROWS = DA_DV + 16


def _attn_body(q_ref, k_ref, v_ref, bias_ref, qg_ref, kg_ref, lam_ref, sg_ref, o_ref,
               kn, vt, m_s, acc_s):
    qi = pl.program_id(2)
    tq = q_ref.shape[0]
    seq = k_ref.shape[0]
    tk = tq
    lo_mask = lax.broadcasted_iota(I32, (1, DA_DV), 1) < DA_DH

    def group_norm(x, gain):
        x2 = x * x
        lo = jnp.sum(jnp.where(lo_mask, x2, 0.0), axis=-1, keepdims=True)
        hi = jnp.sum(jnp.where(lo_mask, 0.0, x2), axis=-1, keepdims=True)
        r = jnp.where(lo_mask, lax.rsqrt(lo * (1.0 / DA_DH) + EPS), lax.rsqrt(hi * (1.0 / DA_DH) + EPS))
        return x * r * gain

    @pl.when(qi == 0)
    def _():
        ones = jnp.ones((DA_VROWS - DA_DV, tk), BF16)

        def body(c, carry):
            r0 = pl.multiple_of(c * tk, tk)
            kn[pl.ds(r0, tk), :] = group_norm(k_ref[pl.ds(r0, tk), :].astype(F32), kg_ref[...]).astype(BF16)
            vt[c, 0:DA_DV, :] = v_ref[pl.ds(r0, tk), :].astype(F32).T.astype(BF16)
            vt[c, DA_DV:DA_VROWS, :] = ones
            return carry
        lax.fori_loop(0, seq // tk, body, 0)

    q = group_norm(q_ref[...].astype(F32), qg_ref[...]) * (DA_DH ** -0.5 * LOG2E)
    qm = (jnp.where(lo_mask, q, 0.0).astype(BF16), jnp.where(lo_mask, 0.0, q).astype(BF16))
    m_s[...] = jnp.full(m_s.shape, NEG, F32)
    acc_s[...] = jnp.zeros_like(acc_s)

    def block(j, d):
        r0 = pl.multiple_of(j * tk, tk)
        kb = kn[pl.ds(r0, tk), :]
        vb = vt[j]
        for m in range(2):
            st = _dot_nt(kb, qm[m])
            if d is not None:
                st = st + bias_ref[0, d]
            m_prev = m_s[m]
            m_new = jnp.maximum(m_prev, jnp.max(st, axis=0, keepdims=True))
            alpha = jnp.exp2(m_prev - m_new)
            pt = jnp.exp2(st - m_new).astype(BF16)
            acc_s[m] = alpha * acc_s[m] + _dot(vb, pt)
            m_s[m] = m_new

    def far_block(j, carry):
        block(j, None)
        return carry

    lax.fori_loop(0, jnp.maximum(qi - 1, 0), far_block, 0)

    @pl.when(qi >= 1)
    def _():
        block(qi - 1, 1)

    block(qi, 0)

    lam_p = lam_ref[...]
    lam = (jnp.exp(jnp.sum(lam_p[0:1, :] * lam_p[1:2, :], axis=-1, keepdims=True))
           - jnp.exp(jnp.sum(lam_p[2:3, :] * lam_p[3:4, :], axis=-1, keepdims=True)) + LAM_INIT)
    a0 = acc_s[0]
    a1 = acc_s[1]
    ot = a0[0:DA_DV] / a0[DA_DV:DA_DV + 1] - lam * (a1[0:DA_DV] / a1[DA_DV:DA_DV + 1])
    r = lax.rsqrt(jnp.sum(ot * ot, axis=0, keepdims=True) * (1.0 / DA_DV) + EPS)
    ot = ot * r * (sg_ref[...] * (1.0 - LAM_INIT))
    o_ref[...] = ot.T.astype(o_ref.dtype)


def _attention(p, bias, qg, kg, lam_p, subln, batch, seq):
    t = batch * seq
    tq = min(ATT_BLOCK, seq)
    nq = seq // tq
    dv = DA_DV
    return pl.pallas_call(
        _attn_body,
        grid=(batch, DA_HEADS, nq),
        in_specs=[
            pl.BlockSpec((tq, dv), lambda b, h, i: (b * nq + i, P_DAQ // dv + h)),
            pl.BlockSpec((seq, dv), lambda b, h, i: (b, P_DAK // dv + h)),
            pl.BlockSpec((seq, dv), lambda b, h, i: (b, P_DAV // dv + h)),
            pl.BlockSpec((1, 2, tq, tq), lambda b, h, i: (h, 0, 0, 0)),
            pl.BlockSpec((1, dv), lambda b, h, i: (0, 0)),
            pl.BlockSpec((1, dv), lambda b, h, i: (0, 0)),
            pl.BlockSpec((4, DA_DH), lambda b, h, i: (0, 0)),
            pl.BlockSpec((dv, 1), lambda b, h, i: (0, 0)),
        ],
        out_specs=pl.BlockSpec((tq, dv), lambda b, h, i: (b * nq + i, h)),
        out_shape=jax.ShapeDtypeStruct((t, DA_HEADS * dv), BF16),
        scratch_shapes=[
            pltpu.VMEM((seq, dv), BF16),
            pltpu.VMEM((seq // tq, DA_VROWS, tq), BF16),
            pltpu.VMEM((2, 1, tq), F32),
            pltpu.VMEM((2, DA_VROWS, tq), F32),
        ],
        compiler_params=pltpu.CompilerParams(dimension_semantics=("parallel", "parallel", "arbitrary")),
        name="diff_attention",
    )(p, p, p, bias, qg, kg, lam_p, subln)


def _memkv_body(mem_ref, mg_ref, w_ref, kg_ref, mk_ref, mv_ref):
    x = mem_ref[0]
    xn = x * _rms(x, D_MODEL) * mg_ref[...]
    kv = _dot(xn.astype(BF16), w_ref[...])
    hw = MX_HEADS * MX_DH
    for h in range(MX_HEADS):
        sl = slice(h * MX_DH, (h + 1) * MX_DH)
        kh = kv[:, sl]
        mk_ref[0, :, sl] = (kh * _rms(kh, MX_DH) * kg_ref[...]).astype(BF16)
    mv_ref[0] = kv[:, hw:].astype(BF16)


def _memkv(mem, mem_norm, w_kv, k_norm):
    b, n, _ = mem.shape
    hw = MX_HEADS * MX_DH
    return pl.pallas_call(
        _memkv_body,
        grid=(b,),
        in_specs=[
            pl.BlockSpec((1, n, D_MODEL), lambda i: (i, 0, 0)),
            pl.BlockSpec((1, D_MODEL), lambda i: (0, 0)),
            pl.BlockSpec((D_MODEL, 2 * hw), lambda i: (0, 0)),
            pl.BlockSpec((1, MX_DH), lambda i: (0, 0)),
        ],
        out_specs=[pl.BlockSpec((1, n, hw), lambda i: (i, 0, 0))] * 2,
        out_shape=[jax.ShapeDtypeStruct((b, n, hw), BF16)] * 2,
        name="memory_kv",
    )(mem, mem_norm, w_kv, k_norm)


def _merge_body(x_ref, odn_ref, oda_ref, mxq_ref, g0_ref, g1_ref, g2_ref, mk_ref, mv_ref, qg_ref,
                bg_ref, wb_ref, wo_ref, out_ref, omx):
    for h in range(MX_HEADS):
        sl = slice(h * MX_DH, (h + 1) * MX_DH)
        qh = mxq_ref[:, sl].astype(F32)
        qh = qh * _rms(qh, MX_DH) * qg_ref[...] * (MX_DH ** -0.5 * LOG2E)
        s = _dot_nt(qh.astype(BF16), mk_ref[0, :, sl])
        p = jnp.exp2(s - jnp.max(s, axis=-1, keepdims=True))
        oh = _dot(p.astype(BF16), mv_ref[0, :, sl]) / jnp.sum(p, axis=-1, keepdims=True)
        omx[:, sl] = oh.astype(BF16)
    y = None
    for r, (o_r, g_r) in enumerate(((odn_ref, g0_ref), (oda_ref, g1_ref), (omx, g2_ref))):
        gate = _sigmoid(g_r[...].astype(F32) + bg_ref[r:r + 1, :])
        term = gate * _dot(o_r[...], wb_ref[r])
        y = term if y is None else y + term
    out_ref[...] = x_ref[...] + _dot(y.astype(BF16), wo_ref[...])


def _merge(x2, o_dn, o_da, p, mk, mv, q_norm, b_gate, w_branch, w_out, seq):
    t = x2.shape[0]
    tm = min(512, seq)
    nt = seq // tm
    bw = 512
    n_mem = mk.shape[1]
    return pl.pallas_call(
        _merge_body,
        grid=(t // tm,),
        in_specs=[
            pl.BlockSpec((tm, D_MODEL), lambda i: (i, 0)),
            pl.BlockSpec((tm, bw), lambda i: (i, 0)),
            pl.BlockSpec((tm, bw), lambda i: (i, 0)),
            pl.BlockSpec((tm, bw), lambda i: (i, P_MXQ // bw)),
            pl.BlockSpec((tm, D_MODEL), lambda i: (i, P_GATE // D_MODEL)),
            pl.BlockSpec((tm, D_MODEL), lambda i: (i, P_GATE // D_MODEL + 1)),
            pl.BlockSpec((tm, D_MODEL), lambda i: (i, P_GATE // D_MODEL + 2)),
            pl.BlockSpec((1, n_mem, bw), lambda i: (i // nt, 0, 0)),
            pl.BlockSpec((1, n_mem, bw), lambda i: (i // nt, 0, 0)),
            pl.BlockSpec((1, MX_DH), lambda i: (0, 0)),
            pl.BlockSpec((3, D_MODEL), lambda i: (0, 0)),
            pl.BlockSpec((3, bw, D_MODEL), lambda i: (0, 0, 0)),
            pl.BlockSpec((D_MODEL, D_MODEL), lambda i: (0, 0)),
        ],
        out_specs=pl.BlockSpec((tm, D_MODEL), lambda i: (i, 0)),
        out_shape=jax.ShapeDtypeStruct((t, D_MODEL), F32),
        scratch_shapes=[pltpu.VMEM((tm, bw), BF16)],
        compiler_params=pltpu.CompilerParams(dimension_semantics=("parallel",)),
        name="merge",
    )(x2, o_dn, o_da, p, p, p, p, mk, mv, q_norm, b_gate, w_branch, w_out)


def _router_body(x_ref, g_ref, wr_ref, br_ref, h_ref, idx_ref, wt_ref, rank_ref, cnt_ref, carry):
    i = pl.program_id(0)
    tm = x_ref.shape[0]

    @pl.when(i == 0)
    def _():
        carry[...] = jnp.zeros_like(carry)

    x = x_ref[...]
    h = x * _rms(x, D_MODEL) * g_ref[...]
    h_ref[...] = h
    logits = _dot3(h, wr_ref[...]) + br_ref[...]
    lane = lax.broadcasted_iota(I32, (tm, LANES), 1)
    lane_f = lane.astype(F32)
    work = logits
    sel = jnp.zeros((tm, LANES), F32)
    vals, idxs = [], []
    for _ in range(TOP_K):
        mx = jnp.max(work, axis=-1, keepdims=True)
        ik = jnp.min(jnp.where(work == mx, lane_f, float(LANES)), axis=-1, keepdims=True)
        hit = lane_f == ik
        sel = jnp.where(hit, 1.0, sel)
        work = jnp.where(hit, -jnp.inf, work)
        vals.append(mx)
        idxs.append(ik)
    es = [jnp.exp(v - vals[0]) for v in vals]
    den = es[0] + es[1] + es[2] + es[3]
    r = lax.broadcasted_iota(I32, (tm, tm), 0)
    c = lax.broadcasted_iota(I32, (tm, tm), 1)
    tril = jnp.where(r > c, 1.0, 0.0).astype(BF16)
    cum = _dot(tril, sel.astype(BF16)) + carry[0:1, :]
    idx_o = jnp.zeros((tm, LANES), F32)
    wt_o = jnp.zeros((tm, LANES), F32)
    rank_o = jnp.zeros((tm, LANES), F32)
    for k in range(TOP_K):
        rk = jnp.sum(jnp.where(lane_f == idxs[k], cum, 0.0), axis=-1, keepdims=True)
        idx_o = jnp.where(lane == k, idxs[k], idx_o)
        wt_o = jnp.where(lane == k, es[k] / den, wt_o)
        rank_o = jnp.where(lane == k, rk, rank_o)
    idx_ref[...] = idx_o.astype(I32)
    wt_ref[...] = wt_o
    rank_ref[...] = rank_o.astype(I32)
    new = carry[...] + jnp.sum(sel, axis=0, keepdims=True)
    carry[...] = new
    cnt_ref[...] = new


def _router(x1, gain, w_r, b_r):
    t = x1.shape[0]
    tm = min(512, t)
    row = lambda i: (i, 0)
    fixed = lambda i: (0, 0)
    return pl.pallas_call(
        _router_body,
        grid=(t // tm,),
        in_specs=[
            pl.BlockSpec((tm, D_MODEL), row),
            pl.BlockSpec((1, D_MODEL), fixed),
            pl.BlockSpec((D_MODEL, LANES), fixed),
            pl.BlockSpec((1, LANES), fixed),
        ],
        out_specs=[
            pl.BlockSpec((tm, D_MODEL), row),
            pl.BlockSpec((tm, LANES), row),
            pl.BlockSpec((tm, LANES), row),
            pl.BlockSpec((tm, LANES), row),
            pl.BlockSpec((8, LANES), fixed),
        ],
        out_shape=[
            jax.ShapeDtypeStruct((t, D_MODEL), F32),
            jax.ShapeDtypeStruct((t, LANES), I32),
            jax.ShapeDtypeStruct((t, LANES), F32),
            jax.ShapeDtypeStruct((t, LANES), I32),
            jax.ShapeDtypeStruct((8, LANES), F32),
        ],
        scratch_shapes=[pltpu.VMEM((8, LANES), F32)],
        compiler_params=pltpu.CompilerParams(dimension_semantics=("arbitrary",)),
        name="router",
    )(x1, gain, w_r, b_r)


def _plan_body(cnt_ref, idx_ref, rank_ref, dest_ref, meta_ref):
    i = pl.program_id(0)
    tm = idx_ref.shape[0]
    eb = float(EXPERT_BLOCK)
    cnt = cnt_ref[...]
    lane8 = lax.broadcasted_iota(I32, (8, LANES), 1)
    padded = jnp.where(lane8 < N_EXPERTS, jnp.floor((cnt + (eb - 1.0)) * (1.0 / eb)) * eb, 0.0)
    pends = padded
    for s in (1, 2, 4, 8, 16):
        pends = pends + jnp.where(lane8 >= s, pltpu.roll(pends, s, axis=1), 0.0)
    pstarts = (pends - padded)[0:1, :]
    lane = lax.broadcasted_iota(I32, (tm, LANES), 1)
    idx = idx_ref[...]
    rank = rank_ref[...].astype(F32)
    dest = jnp.zeros((tm, LANES), F32)
    for k in range(TOP_K):
        base = jnp.sum(jnp.where(lane == idx[:, k:k + 1], pstarts, 0.0), axis=-1, keepdims=True)
        dest = jnp.where(lane == k, base + rank[:, k:k + 1], dest)
    dest_ref[...] = dest.astype(I32)

    @pl.when(i == 0)
    def _():
        nb = meta_ref.shape[0]
        ln = lax.broadcasted_iota(I32, (nb, LANES), 1)
        blk = lax.broadcasted_iota(I32, (nb, LANES), 0).astype(F32) * eb
        pe = pends[0:1, :]
        be = jnp.sum(jnp.where((ln < N_EXPERTS) & (pe <= blk), 1.0, 0.0), axis=-1, keepdims=True)
        be = jnp.minimum(be, float(N_EXPERTS - 1))
        used = jnp.sum(jnp.where(ln == N_EXPERTS - 1, pe, 0.0), axis=-1, keepdims=True) * (1.0 / eb)
        meta_ref[...] = jnp.where(ln == 0, be, jnp.where(ln == 1, used, 0.0)).astype(I32)


def _plan(counts, idx, rank, n_blocks_pad):
    t = idx.shape[0]
    tm = min(1024, t)
    row = lambda i: (i, 0)
    fixed = lambda i: (0, 0)
    return pl.pallas_call(
        _plan_body,
        grid=(t // tm,),
        in_specs=[pl.BlockSpec((8, LANES), fixed), pl.BlockSpec((tm, LANES), row),
                  pl.BlockSpec((tm, LANES), row)],
        out_specs=[pl.BlockSpec((tm, LANES), row), pl.BlockSpec((n_blocks_pad, LANES), fixed)],
        out_shape=[jax.ShapeDtypeStruct((t, LANES), I32),
                   jax.ShapeDtypeStruct((n_blocks_pad, LANES), I32)],
        compiler_params=pltpu.CompilerParams(dimension_semantics=("arbitrary",)),
        name="dispatch_plan",
    )(counts, idx, rank)


def _row_copy(src_ref, src_row, dst_ref, dst_row, sem):
    return pltpu.make_async_copy(src_ref.at[pl.ds(src_row, 1), :], dst_ref.at[pl.ds(dst_row, 1), :], sem)


def _dispatch_body(dest_ref, h_ref, xs_in_ref, xs_ref, sem):
    del xs_in_ref
    tm = h_ref.shape[0]

    def issue(t, carry):
        for k in range(TOP_K):
            _row_copy(h_ref, t, xs_ref, dest_ref[0, 0, t * TOP_K + k], sem).start()
        return carry

    lax.fori_loop(0, tm, issue, 0, unroll=4)

    def drain(t, carry):
        for _ in range(TOP_K):
            _row_copy(h_ref, 0, xs_ref, 0, sem).wait()
        return carry

    lax.fori_loop(0, tm, drain, 0, unroll=8)


def _dispatch(dest3, h2, xs_init):
    t = h2.shape[0]
    tm = dest3.shape[2] // TOP_K
    return pl.pallas_call(
        _dispatch_body,
        grid=(t // tm,),
        in_specs=[
            pl.BlockSpec((1, 1, TOP_K * tm), lambda i: (i, 0, 0), memory_space=pltpu.SMEM),
            pl.BlockSpec((tm, D_MODEL), lambda i: (i, 0)),
            pl.BlockSpec(memory_space=pl.ANY),
        ],
        out_specs=pl.BlockSpec(memory_space=pl.ANY),
        out_shape=jax.ShapeDtypeStruct(xs_init.shape, xs_init.dtype),
        scratch_shapes=[pltpu.SemaphoreType.DMA(())],
        input_output_aliases={2: 0},
        compiler_params=pltpu.CompilerParams(dimension_semantics=("arbitrary",)),
        name="moe_dispatch",
    )(dest3, h2, xs_init)


def _expert_body(be_ref, nu_ref, x_ref, wgu_ref, bgu_ref, wd_ref, bd_ref, y_ref):
    del be_ref

    @pl.when(pl.program_id(0) < nu_ref[0])
    def _():
        gu = _dot(x_ref[...].astype(BF16), wgu_ref[0]) + bgu_ref[0]
        gate = jnp.minimum(gu[:, :D_FF], SWIGLU_LIMIT)
        up = jnp.clip(gu[:, D_FF:], -SWIGLU_LIMIT, SWIGLU_LIMIT)
        act = (up + 1.0) * gate * _sigmoid(SWIGLU_ALPHA * gate)
        y_ref[...] = _dot(act.astype(BF16), wd_ref[0]) + bd_ref[0]


def _experts(block_e, n_used, xs, wgu, bgu, wd, bd):
    n_slots = xs.shape[0]
    nb = n_slots // EXPERT_BLOCK

    def blk(j, be, nu):
        return jnp.minimum(j, nu[0] - 1)

    grid_spec = pltpu.PrefetchScalarGridSpec(
        num_scalar_prefetch=2,
        grid=(nb,),
        in_specs=[
            pl.BlockSpec((EXPERT_BLOCK, D_MODEL), lambda j, be, nu: (blk(j, be, nu), 0)),
            pl.BlockSpec((1, D_MODEL, 2 * D_FF), lambda j, be, nu: (be[blk(j, be, nu)], 0, 0)),
            pl.BlockSpec((1, 1, 2 * D_FF), lambda j, be, nu: (be[blk(j, be, nu)], 0, 0)),
            pl.BlockSpec((1, D_FF, D_MODEL), lambda j, be, nu: (be[blk(j, be, nu)], 0, 0)),
            pl.BlockSpec((1, 1, D_MODEL), lambda j, be, nu: (be[blk(j, be, nu)], 0, 0)),
        ],
        out_specs=pl.BlockSpec((EXPERT_BLOCK, D_MODEL), lambda j, be, nu: (blk(j, be, nu), 0)),
    )
    return pl.pallas_call(
        _expert_body,
        grid_spec=grid_spec,
        out_shape=jax.ShapeDtypeStruct((n_slots, D_MODEL), F32),
        compiler_params=pltpu.CompilerParams(dimension_semantics=("arbitrary",)),
        name="moe_experts",
    )(block_e, n_used, xs, wgu, bgu, wd, bd)


def _combine_body(dest_ref, x_ref, wt_ref, y_ref, out_ref, ybuf, sem):
    tm = x_ref.shape[0]

    def issue(t, carry):
        for k in range(TOP_K):
            _row_copy(y_ref, dest_ref[0, 0, t * TOP_K + k], ybuf, k * tm + t, sem).start()
        return carry

    lax.fori_loop(0, tm, issue, 0, unroll=4)

    def drain(t, carry):
        for _ in range(TOP_K):
            _row_copy(y_ref, 0, ybuf, 0, sem).wait()
        return carry

    lax.fori_loop(0, tm, drain, 0, unroll=8)

    acc = x_ref[...]
    wt = wt_ref[...]
    for k in range(TOP_K):
        acc = acc + wt[:, k:k + 1] * ybuf[k * tm:(k + 1) * tm, :]
    out_ref[...] = acc


def _combine(dest3, x1, wts, y):
    t = x1.shape[0]
    tm = dest3.shape[2] // TOP_K
    return pl.pallas_call(
        _combine_body,
        grid=(t // tm,),
        in_specs=[
            pl.BlockSpec((1, 1, TOP_K * tm), lambda i: (i, 0, 0), memory_space=pltpu.SMEM),
            pl.BlockSpec((tm, D_MODEL), lambda i: (i, 0)),
            pl.BlockSpec((tm, LANES), lambda i: (i, 0)),
            pl.BlockSpec(memory_space=pl.ANY),
        ],
        out_specs=pl.BlockSpec((tm, D_MODEL), lambda i: (i, 0)),
        out_shape=jax.ShapeDtypeStruct((t, D_MODEL), F32),
        scratch_shapes=[pltpu.VMEM((TOP_K * tm, D_MODEL), F32), pltpu.SemaphoreType.DMA(())],
        compiler_params=pltpu.CompilerParams(dimension_semantics=("arbitrary",)),
        name="moe_combine",
    )(dest3, x1, wts, y)


def _pad_lanes(v, fill=0.0):
    v = v.astype(F32).reshape(1, -1)
    return jnp.pad(v, ((0, 0), (0, LANES - v.shape[1])), constant_values=fill)


def _mixer(x2, mem, rel_table, attn_norm, w_in, b_gate, dn_conv, dn_a_log, dn_dt_bias, dn_out_norm,
           da_q_norm, da_k_norm, da_lambda, da_subln, mem_norm, w_mem_kv, mx_q_norm, mx_k_norm,
           w_branch, w_out, batch, seq):
    wp = jnp.concatenate([w_in[:, :W_AB_LO], w_in[:, W_AB_HI:]], axis=1).astype(BF16)
    wab = jnp.pad(w_in[:, W_AB_LO:W_AB_HI], ((0, 0), (0, LANES - (W_AB_HI - W_AB_LO))))
    p, ab = _inproj(x2, attn_norm.reshape(1, -1), wp, wab)

    o_dn = _deltanet(p, ab, dn_conv, _pad_lanes(dn_a_log), _pad_lanes(dn_dt_bias),
                     dn_out_norm.reshape(1, -1), batch, seq)

    tq = min(ATT_BLOCK, seq)
    bias = _bias_tiles(rel_table.T, tq)
    o_da = _attention(p, bias, jnp.tile(da_q_norm, 2).reshape(1, -1), jnp.tile(da_k_norm, 2).reshape(1, -1),
                      da_lambda, da_subln.reshape(-1, 1), batch, seq)

    mk, mv = _memkv(mem, mem_norm.reshape(1, -1), w_mem_kv.astype(BF16), mx_k_norm.reshape(1, -1))
    return _merge(x2, o_dn, o_da, p, mk, mv, mx_q_norm.reshape(1, -1), b_gate.reshape(3, D_MODEL),
                  w_branch.astype(BF16), w_out.astype(BF16), seq)


def _moe(x1, ffn_norm, w_router, b_router, w_gate_up, b_gate_up, w_down, b_down):
    t = x1.shape[0]
    m = t * TOP_K
    n_blocks = -(-m // EXPERT_BLOCK) + N_EXPERTS
    n_blocks_pad = -(-n_blocks // 8) * 8
    n_slots = n_blocks * EXPERT_BLOCK

    wr = jnp.pad(w_router, ((0, 0), (0, LANES - N_EXPERTS)))
    h2, idx, wts, rank, counts = _router(x1, ffn_norm.reshape(1, -1), wr, _pad_lanes(b_router, NEG))
    dest, meta = _plan(counts, idx, rank, n_blocks_pad)
    block_e = meta[:n_blocks, 0]
    n_used = meta[0:1, 1]

    tm_d = min(512, t)
    dest_d = dest[:, :TOP_K].reshape(t // tm_d, 1, TOP_K * tm_d)
    xs = _dispatch(dest_d, h2, jnp.zeros((n_slots, D_MODEL), F32))
    y = _experts(block_e, n_used, xs, w_gate_up.astype(BF16), b_gate_up.reshape(N_EXPERTS, 1, -1),
                 w_down.astype(BF16), b_down.reshape(N_EXPERTS, 1, -1))
    tm_c = min(256, t)
    dest_c = dest[:, :TOP_K].reshape(t // tm_c, 1, TOP_K * tm_c)
    return _combine(dest_c, x1, wts, y)


def kernel(x, mem, rel_table, attn_norm, w_in, b_gate, dn_conv, dn_a_log, dn_dt_bias, dn_out_norm,
           da_q_norm, da_k_norm, da_lambda, da_subln, mem_norm, w_mem_kv, mx_q_norm, mx_k_norm,
           w_branch, w_out, ffn_norm, w_router, b_router, w_gate_up, b_gate_up, w_down, b_down):
    batch, seq, d = x.shape
    x2 = x.reshape(batch * seq, d)
    x1 = _mixer(x2, mem, rel_table, attn_norm[0], w_in[0], b_gate[0], dn_conv[0], dn_a_log[0],
                dn_dt_bias[0], dn_out_norm[0], da_q_norm[0], da_k_norm[0], da_lambda[0], da_subln[0],
                mem_norm[0], w_mem_kv[0], mx_q_norm[0], mx_k_norm[0], w_branch[0], w_out[0], batch, seq)
    out = _moe(x1, ffn_norm[0], w_router[0], b_router[0], w_gate_up[0], b_gate_up[0], w_down[0],
               b_down[0])
    return out.reshape(batch, seq, d)
```

```python
import functools
import math

import jax
import jax.numpy as jnp
from jax import lax
from jax.experimental import pallas as pl
from jax.experimental.pallas import tpu as pltpu

F32 = jnp.float32
BF16 = jnp.bfloat16
I32 = jnp.int32

D_MODEL = 1024
EPS = 1e-6
LANES = 128

DN_HEADS = 4
DN_DK = 128
DN_CHUNK = 64
DN_CONV = 4

DA_HEADS = 4
DA_DH = 64

MX_HEADS = 4
MX_DH = 128

REL_BUCKETS = 32
REL_MAX_DIST = 128

N_EXPERTS = 32
TOP_K = 4
D_FF = 1024
SWIGLU_LIMIT = 7.0
SWIGLU_ALPHA = 1.702
EXPERT_BLOCK = 512

LAM_INIT = 0.8 - 0.6 * math.exp(-0.3 * 0)
LOG2E = 1.4426950408889634
NEG = -1e30

P_DNQ, P_DNK, P_DNV, P_DNZ = 0, 512, 1024, 1536
P_DAQ, P_DAK, P_DAV = 2048, 2560, 3072
P_MXQ = 3584
P_GATE = 4096
P_COLS = 7168
W_AB_LO, W_AB_HI = 2048, 2056


def _dot(a, b):
    return jnp.dot(a, b, preferred_element_type=F32)


def _dot_nt(a, b):
    return lax.dot_general(a, b, (((1,), (1,)), ((), ())), preferred_element_type=F32)


def _dot_tn(a, b):
    return lax.dot_general(a, b, (((0,), (0,)), ((), ())), preferred_element_type=F32)


def _split(x):
    hi = x.astype(BF16)
    lo = (x - hi.astype(F32)).astype(BF16)
    return hi, lo


def _dot3(a, b):
    ah, al = _split(a)
    bh, bl = _split(b)
    return _dot(ah, bh) + _dot(ah, bl) + _dot(al, bh)


def _sigmoid(x):
    return 1.0 / (1.0 + jnp.exp(-x))


def _rms(x, n):
    return lax.rsqrt(jnp.sum(x * x, axis=-1, keepdims=True) * (1.0 / n) + EPS)


def _inproj_body(x_ref, g_ref, w_ref, wab_ref, p_ref, ab_ref, h_scr):
    @pl.when(pl.program_id(1) == 0)
    def _():
        x = x_ref[...]
        h = x * _rms(x, D_MODEL) * g_ref[...]
        h_scr[...] = h.astype(BF16)
        ab_ref[...] = _dot3(h, wab_ref[...])

    p_ref[...] = _dot(h_scr[...], w_ref[...]).astype(p_ref.dtype)


def _inproj(x2, gain, wp, wab):
    t = x2.shape[0]
    tm = min(1024, t)
    tn = 1024
    return pl.pallas_call(
        _inproj_body,
        grid=(t // tm, P_COLS // tn),
        in_specs=[
            pl.BlockSpec((tm, D_MODEL), lambda i, j: (i, 0)),
            pl.BlockSpec((1, D_MODEL), lambda i, j: (0, 0)),
            pl.BlockSpec((D_MODEL, tn), lambda i, j: (0, j)),
            pl.BlockSpec((D_MODEL, LANES), lambda i, j: (0, 0)),
        ],
        out_specs=[
            pl.BlockSpec((tm, tn), lambda i, j: (i, j)),
            pl.BlockSpec((tm, LANES), lambda i, j: (i, 0)),
        ],
        out_shape=[
            jax.ShapeDtypeStruct((t, P_COLS), BF16),
            jax.ShapeDtypeStruct((t, LANES), F32),
        ],
        scratch_shapes=[pltpu.VMEM((tm, D_MODEL), BF16)],
        compiler_params=pltpu.CompilerParams(dimension_semantics=("parallel", "arbitrary")),
        name="inproj",
    )(x2, gain, wp, wab)


DN_HALO = 16
DN_SCAN_CHUNK = 256


def _deltanet_body(q_ref, k_ref, v_ref, z_ref, qh_ref, kh_ref, vh_ref, ab_ref, cw_ref, alog_ref,
                   dtb_ref, on_ref, o_ref, stage, qs, ks, vs, s_scr):
    i = pl.program_id(1)
    tc = q_ref.shape[0]
    hw = DN_HEADS * DN_DK

    @pl.when(i == 0)
    def _():
        s_scr[...] = jnp.zeros_like(s_scr)

    for src, halo, dst, off, kind in ((q_ref, qh_ref, qs, 0, "q"), (k_ref, kh_ref, ks, hw, "k"),
                                      (v_ref, vh_ref, vs, 2 * hw, "v")):
        hal = halo[...].astype(F32)
        stage[0:DN_HALO, :] = jnp.where(i == 0, 0.0, hal)
        stage[DN_HALO:DN_HALO + tc, :] = src[...].astype(F32)
        base = DN_HALO - (DN_CONV - 1)
        y = stage[base:base + tc, :] * cw_ref[0:1, off:off + hw]
        for j in range(1, DN_CONV):
            y = y + stage[base + j:base + j + tc, :] * cw_ref[j:j + 1, off:off + hw]
        y = y * _sigmoid(y)
        if kind == "v":
            dst[...] = y
        else:
            for h in range(DN_HEADS):
                sl = slice(h * DN_DK, (h + 1) * DN_DK)
                yh = y[:, sl]
                r = lax.rsqrt(jnp.sum(yh * yh, axis=-1, keepdims=True) + EPS)
                if kind == "q":
                    r = r * (DN_DK ** -0.5)
                dst[:, sl] = yh * r

    c = min(DN_SCAN_CHUNK, tc)
    row = lax.broadcasted_iota(I32, (c, c), 0)
    col = lax.broadcasted_iota(I32, (c, c), 1)
    incl = row >= col
    strict = row > col
    same_blk = (row // DN_CHUNK) == (col // DN_CHUNK)
    tri = jnp.where(incl, 1.0, 0.0).astype(BF16)
    eye = jnp.where(row == col, 1.0, 0.0)
    neg_a = -jnp.exp(alog_ref[...])
    dtb = dtb_ref[...]

    def chunk(ci, carry):
        r0 = pl.multiple_of(ci * c, c)
        abc = ab_ref[pl.ds(r0, c), :]
        a_in = abc + dtb
        g_all = neg_a * (jnp.maximum(a_in, 0.0) + jnp.log(1.0 + jnp.exp(-jnp.abs(a_in))))
        beta_all = _sigmoid(abc)
        zc = z_ref[pl.ds(r0, c), :].astype(F32)
        for h in range(DN_HEADS):
            sl = slice(h * DN_DK, (h + 1) * DN_DK)
            q = qs[pl.ds(r0, c), sl]
            k = ks[pl.ds(r0, c), sl]
            v = vs[pl.ds(r0, c), sl]
            g = g_all[:, h:h + 1]
            beta = beta_all[:, DN_HEADS + h:DN_HEADS + h + 1]
            g_hi, g_lo = _split(jnp.where(strict, g, 0.0))
            diff = _dot(tri, g_hi) + _dot(tri, g_lo)
            gc = diff[:, 0:1] + g[0:1, :]
            decay = jnp.where(incl, jnp.exp(diff), 0.0)
            kb = k.astype(BF16)
            qkk = _dot_nt(jnp.concatenate([q.astype(BF16), kb], axis=0), kb)
            qk = qkk[:c]
            kk = qkk[c:]
            lower = jnp.where(strict, kk * decay * beta, 0.0)
            pw = jnp.where(same_blk, -lower, 0.0)
            dinv = eye + pw
            for _ in range(int(math.log2(DN_CHUNK)) - 1):
                pwb = pw.astype(BF16)
                pw = _dot(pwb, pwb)
                dinv = dinv + _dot(dinv.astype(BF16), pw.astype(BF16))
            dinv_b = dinv.astype(BF16)
            pw = -_dot(dinv_b, jnp.where(same_blk, 0.0, lower).astype(BF16))
            xm = eye + pw
            for _ in range(int(math.log2(c // DN_CHUNK)) - 1):
                pwb = pw.astype(BF16)
                pw = _dot(pwb, pwb)
                xm = xm + _dot(xm.astype(BF16), pw.astype(BF16))
            inv = _dot(xm.astype(BF16), dinv_b)
            egc = jnp.exp(gc)
            rhs = jnp.concatenate([v * beta, k * (beta * egc)], axis=1)
            sol = _dot(inv.astype(BF16), rhs.astype(BF16))
            u = sol[:, :DN_DK]
            w = sol[:, DN_DK:]
            qkm = jnp.where(incl, qk * decay, 0.0)
            gl = gc[c - 1:c, :]
            state = s_scr[h]
            sb = state.astype(BF16)
            ws = _dot(jnp.concatenate([w.astype(BF16), (q * egc).astype(BF16)], axis=0), sb)
            v_new = u - ws[:c]
            o = ws[c:] + _dot(qkm.astype(BF16), v_new.astype(BF16))
            s_scr[h] = state * jnp.exp(gl) + _dot_tn(kb, (v_new * jnp.exp(gl - gc)).astype(BF16))
            zz = zc[:, sl]
            on = o * _rms(o, DN_DK) * on_ref[...]
            o_ref[pl.ds(r0, c), sl] = (on * (zz * _sigmoid(zz))).astype(o_ref.dtype)
        return carry

    lax.fori_loop(0, tc // c, chunk, 0)


def _deltanet(p, ab, conv_w, alog_row, dtb_row, out_norm, batch, seq):
    t = batch * seq
    tc = min(512, seq)
    nt = seq // tc
    hw = DN_HEADS * DN_DK

    def main(cb):
        return pl.BlockSpec((tc, hw), lambda b, i: (b * nt + i, cb))

    def halo(cb):
        return pl.BlockSpec(
            (DN_HALO, hw),
            lambda b, i: (jnp.maximum((b * seq + i * tc) // DN_HALO - 1, 0), cb))

    return pl.pallas_call(
        _deltanet_body,
        grid=(batch, nt),
        in_specs=[
            main(P_DNQ // hw), main(P_DNK // hw), main(P_DNV // hw), main(P_DNZ // hw),
            halo(P_DNQ // hw), halo(P_DNK // hw), halo(P_DNV // hw),
            pl.BlockSpec((tc, LANES), lambda b, i: (b * nt + i, 0)),
            pl.BlockSpec((DN_CONV, 3 * hw), lambda b, i: (0, 0)),
            pl.BlockSpec((1, LANES), lambda b, i: (0, 0)),
            pl.BlockSpec((1, LANES), lambda b, i: (0, 0)),
            pl.BlockSpec((1, DN_DK), lambda b, i: (0, 0)),
        ],
        out_specs=pl.BlockSpec((tc, hw), lambda b, i: (b * nt + i, 0)),
        out_shape=jax.ShapeDtypeStruct((t, hw), BF16),
        scratch_shapes=[
            pltpu.VMEM((DN_HALO + tc, hw), F32),
            pltpu.VMEM((tc, hw), F32),
            pltpu.VMEM((tc, hw), F32),
            pltpu.VMEM((tc, hw), F32),
            pltpu.VMEM((DN_HEADS, DN_DK, DN_DK), F32),
        ],
        compiler_params=pltpu.CompilerParams(dimension_semantics=("parallel", "arbitrary")),
        name="deltanet",
    )(p, p, p, p, p, p, p, ab, conv_w, alog_row, dtb_row, out_norm)


ATT_BLOCK = 512


def _bias_body(tbl_ref, o_ref):
    h = pl.program_id(0)
    tq = o_ref.shape[2]
    key = lax.broadcasted_iota(I32, (tq, tq), 0)
    qry = lax.broadcasted_iota(I32, (tq, tq), 1)
    max_exact = REL_BUCKETS // 2
    far = tbl_ref[h, REL_BUCKETS - 1]
    for d in range(2):
        n = qry - key + d * tq
        nn = jnp.maximum(n, 0)
        nf = jnp.maximum(nn, 1).astype(F32)
        large = max_exact + (jnp.log(nf / max_exact) / math.log(REL_MAX_DIST / max_exact)
                             * (REL_BUCKETS - max_exact)).astype(I32)
        large = jnp.minimum(large, REL_BUCKETS - 1)
        bucket = jnp.where(nn < max_exact, nn, large)
        val = jnp.zeros((tq, tq), F32)
        for b in range(REL_BUCKETS):
            val = jnp.where(bucket == b, tbl_ref[h, b], val)
        o_ref[0, d] = jnp.where(n >= 0, (val - far) * LOG2E, NEG)


def _bias_tiles(tbl_t, tq):
    return pl.pallas_call(
        _bias_body,
        grid=(DA_HEADS,),
        in_specs=[pl.BlockSpec(memory_space=pltpu.SMEM)],
        out_specs=pl.BlockSpec((1, 2, tq, tq), lambda h: (h, 0, 0, 0)),
        out_shape=jax.ShapeDtypeStruct((DA_HEADS, 2, tq, tq), F32),
        name="t5_bias_tiles",
    )(tbl_t)


DA_DV = 2 * DA_DH
DA_VROWS = DA_DV + 16


BOUND_SLACK = 1.02
MAX_SHIFT_GAP = 110.0


def _attn_body(q_ref, k_ref, v_ref, bias_ref, qg_ref, kg_ref, lam_ref, sg_ref, o_ref,
               kn, vt, kst, m_s, acc_s):
    qi = pl.program_id(2)
    tq = q_ref.shape[0]
    seq = k_ref.shape[0]
    tk = tq
    lo_mask = lax.broadcasted_iota(I32, (1, DA_DV), 1) < DA_DH

    def group_norm(x, gain):
        x2 = x * x
        lo = jnp.sum(jnp.where(lo_mask, x2, 0.0), axis=-1, keepdims=True)
        hi = jnp.sum(jnp.where(lo_mask, 0.0, x2), axis=-1, keepdims=True)
        r = jnp.where(lo_mask, lax.rsqrt(lo * (1.0 / DA_DH) + EPS), lax.rsqrt(hi * (1.0 / DA_DH) + EPS))
        return x * r * gain

    @pl.when(qi == 0)
    def _():
        ones = jnp.ones((DA_VROWS - DA_DV, tk), BF16)

        def body(c, kmax2):
            r0 = pl.multiple_of(c * tk, tk)
            kb = group_norm(k_ref[pl.ds(r0, tk), :].astype(F32), kg_ref[...]).astype(BF16)
            kn[pl.ds(r0, tk), :] = kb
            vt[c, 0:DA_DV, :] = v_ref[pl.ds(r0, tk), :].astype(F32).T.astype(BF16)
            vt[c, DA_DV:DA_VROWS, :] = ones
            k2 = kb.astype(F32)
            k2 = k2 * k2
            lo = jnp.max(jnp.sum(jnp.where(lo_mask, k2, 0.0), axis=-1, keepdims=True), axis=0, keepdims=True)
            hi = jnp.max(jnp.sum(jnp.where(lo_mask, 0.0, k2), axis=-1, keepdims=True), axis=0, keepdims=True)
            return jnp.maximum(kmax2, jnp.where(lo_mask, lo, hi))
        kst[0:1, :] = lax.fori_loop(0, seq // tk, body, jnp.zeros((1, DA_DV), F32))
        b0 = bias_ref[0, 0]
        b1 = bias_ref[0, 1]
        bmax = jnp.maximum(jnp.max(jnp.maximum(b0, b1), axis=0, keepdims=True), 0.0)
        bmin = jnp.minimum(jnp.min(jnp.minimum(jnp.where(b0 > 0.5 * NEG, b0, 0.0), b1), axis=0, keepdims=True), 0.0)
        kst[1:2, :] = jnp.broadcast_to(jnp.max(bmax, axis=1, keepdims=True), (1, DA_DV))
        kst[2:3, :] = jnp.broadcast_to(jnp.min(bmin, axis=1, keepdims=True), (1, DA_DV))

    q = group_norm(q_ref[...].astype(F32), qg_ref[...]) * (DA_DH ** -0.5 * LOG2E)
    qm = (jnp.where(lo_mask, q, 0.0).astype(BF16), jnp.where(lo_mask, 0.0, q).astype(BF16))
    acc_s[...] = jnp.zeros_like(acc_s)

    q2 = q * q * kst[0:1, :]
    ones8 = jnp.ones((8, DA_DV), BF16)
    bmax = kst[1:2, 0:1]
    bmin = kst[2:3, 0:1]
    bound = []
    for m in range(2):
        q2m = jnp.where(lo_mask, q2, 0.0) if m == 0 else jnp.where(lo_mask, 0.0, q2)
        bound.append(jnp.sqrt(_dot_nt(ones8, q2m.astype(BF16))[0:1, :]) * BOUND_SLACK)
    worst = jnp.max(2.0 * jnp.maximum(bound[0], bound[1]), axis=1, keepdims=True) + bmax - bmin
    safe = worst[0, 0] <= MAX_SHIFT_GAP

    def block(j, d, fixed_shift):
        r0 = pl.multiple_of(j * tk, tk)
        kb = kn[pl.ds(r0, tk), :]
        vb = vt[j]
        for m in range(2):
            st = _dot_nt(kb, qm[m])
            if d is not None:
                st = st + bias_ref[0, d]
            if fixed_shift:
                acc_s[m] = acc_s[m] + _dot(vb, jnp.exp2(st - m_s[m]).astype(BF16))
            else:
                m_prev = m_s[m]
                m_new = jnp.maximum(m_prev, jnp.max(st, axis=0, keepdims=True))
                alpha = jnp.exp2(m_prev - m_new)
                acc_s[m] = alpha * acc_s[m] + _dot(vb, jnp.exp2(st - m_new).astype(BF16))
                m_s[m] = m_new

    def run(fixed_shift):
        def far_block(j, carry):
            block(j, None, fixed_shift)
            return carry

        lax.fori_loop(0, jnp.maximum(qi - 1, 0), far_block, 0)

        @pl.when(qi >= 1)
        def _():
            block(qi - 1, 1, fixed_shift)

        block(qi, 0, fixed_shift)

    @pl.when(safe)
    def _():
        m_s[0] = bound[0] + bmax
        m_s[1] = bound[1] + bmax
        run(True)

    @pl.when(jnp.logical_not(safe))
    def _():
        m_s[...] = jnp.full(m_s.shape, NEG, F32)
        run(False)

    lam_p = lam_ref[...]
    lam = (jnp.exp(jnp.sum(lam_p[0:1, :] * lam_p[1:2, :], axis=-1, keepdims=True))
           - jnp.exp(jnp.sum(lam_p[2:3, :] * lam_p[3:4, :], axis=-1, keepdims=True)) + LAM_INIT)
    a0 = acc_s[0]
    a1 = acc_s[1]
    ot = a0[0:DA_DV] / a0[DA_DV:DA_DV + 1] - lam * (a1[0:DA_DV] / a1[DA_DV:DA_DV + 1])
    r = lax.rsqrt(jnp.sum(ot * ot, axis=0, keepdims=True) * (1.0 / DA_DV) + EPS)
    ot = ot * r * (sg_ref[...] * (1.0 - LAM_INIT))
    o_ref[...] = ot.T.astype(o_ref.dtype)


def _attention(p, bias, qg, kg, lam_p, subln, batch, seq):
    t = batch * seq
    tq = min(ATT_BLOCK, seq)
    nq = seq // tq
    dv = DA_DV
    return pl.pallas_call(
        _attn_body,
        grid=(batch, DA_HEADS, nq),
        in_specs=[
            pl.BlockSpec((tq, dv), lambda b, h, i: (b * nq + i, P_DAQ // dv + h)),
            pl.BlockSpec((seq, dv), lambda b, h, i: (b, P_DAK // dv + h)),
            pl.BlockSpec((seq, dv), lambda b, h, i: (b, P_DAV // dv + h)),
            pl.BlockSpec((1, 2, tq, tq), lambda b, h, i: (h, 0, 0, 0)),
            pl.BlockSpec((1, dv), lambda b, h, i: (0, 0)),
            pl.BlockSpec((1, dv), lambda b, h, i: (0, 0)),
            pl.BlockSpec((4, DA_DH), lambda b, h, i: (0, 0)),
            pl.BlockSpec((dv, 1), lambda b, h, i: (0, 0)),
        ],
        out_specs=pl.BlockSpec((tq, dv), lambda b, h, i: (b * nq + i, h)),
        out_shape=jax.ShapeDtypeStruct((t, DA_HEADS * dv), BF16),
        scratch_shapes=[
            pltpu.VMEM((seq, dv), BF16),
            pltpu.VMEM((seq // tq, DA_VROWS, tq), BF16),
            pltpu.VMEM((8, dv), F32),
            pltpu.VMEM((2, 1, tq), F32),
            pltpu.VMEM((2, DA_VROWS, tq), F32),
        ],
        compiler_params=pltpu.CompilerParams(dimension_semantics=("parallel", "parallel", "arbitrary")),
        name="diff_attention",
    )(p, p, p, bias, qg, kg, lam_p, subln)


def _memkv_body(mem_ref, mg_ref, w_ref, kg_ref, mk_ref, mv_ref):
    x = mem_ref[0]
    xn = x * _rms(x, D_MODEL) * mg_ref[...]
    kv = _dot(xn.astype(BF16), w_ref[...])
    hw = MX_HEADS * MX_DH
    for h in range(MX_HEADS):
        sl = slice(h * MX_DH, (h + 1) * MX_DH)
        kh = kv[:, sl]
        mk_ref[0, :, sl] = (kh * _rms(kh, MX_DH) * kg_ref[...]).astype(BF16)
    mv_ref[0] = kv[:, hw:].astype(BF16)


def _memkv(mem, mem_norm, w_kv, k_norm):
    b, n, _ = mem.shape
    hw = MX_HEADS * MX_DH
    return pl.pallas_call(
        _memkv_body,
        grid=(b,),
        in_specs=[
            pl.BlockSpec((1, n, D_MODEL), lambda i: (i, 0, 0)),
            pl.BlockSpec((1, D_MODEL), lambda i: (0, 0)),
            pl.BlockSpec((D_MODEL, 2 * hw), lambda i: (0, 0)),
            pl.BlockSpec((1, MX_DH), lambda i: (0, 0)),
        ],
        out_specs=[pl.BlockSpec((1, n, hw), lambda i: (i, 0, 0))] * 2,
        out_shape=[jax.ShapeDtypeStruct((b, n, hw), BF16)] * 2,
        name="memory_kv",
    )(mem, mem_norm, w_kv, k_norm)


def _merge_body(x_ref, odn_ref, oda_ref, mxq_ref, g0_ref, g1_ref, g2_ref, mk_ref, mv_ref, qg_ref,
                bg_ref, wb_ref, wo_ref, out_ref, omx):
    for h in range(MX_HEADS):
        sl = slice(h * MX_DH, (h + 1) * MX_DH)
        qh = mxq_ref[:, sl].astype(F32)
        qh = qh * _rms(qh, MX_DH) * qg_ref[...] * (MX_DH ** -0.5 * LOG2E)
        s = _dot_nt(qh.astype(BF16), mk_ref[0, :, sl])
        p = jnp.exp2(s - jnp.max(s, axis=-1, keepdims=True))
        oh = _dot(p.astype(BF16), mv_ref[0, :, sl]) / jnp.sum(p, axis=-1, keepdims=True)
        omx[:, sl] = oh.astype(BF16)
    y = None
    for r, (o_r, g_r) in enumerate(((odn_ref, g0_ref), (oda_ref, g1_ref), (omx, g2_ref))):
        gate = _sigmoid(g_r[...].astype(F32) + bg_ref[r:r + 1, :])
        term = gate * _dot(o_r[...], wb_ref[r])
        y = term if y is None else y + term
    out_ref[...] = x_ref[...] + _dot(y.astype(BF16), wo_ref[...])


def _merge(x2, o_dn, o_da, p, mk, mv, q_norm, b_gate, w_branch, w_out, seq):
    t = x2.shape[0]
    tm = min(512, seq)
    nt = seq // tm
    bw = 512
    n_mem = mk.shape[1]
    return pl.pallas_call(
        _merge_body,
        grid=(t // tm,),
        in_specs=[
            pl.BlockSpec((tm, D_MODEL), lambda i: (i, 0)),
            pl.BlockSpec((tm, bw), lambda i: (i, 0)),
            pl.BlockSpec((tm, bw), lambda i: (i, 0)),
            pl.BlockSpec((tm, bw), lambda i: (i, P_MXQ // bw)),
            pl.BlockSpec((tm, D_MODEL), lambda i: (i, P_GATE // D_MODEL)),
            pl.BlockSpec((tm, D_MODEL), lambda i: (i, P_GATE // D_MODEL + 1)),
            pl.BlockSpec((tm, D_MODEL), lambda i: (i, P_GATE // D_MODEL + 2)),
            pl.BlockSpec((1, n_mem, bw), lambda i: (i // nt, 0, 0)),
            pl.BlockSpec((1, n_mem, bw), lambda i: (i // nt, 0, 0)),
            pl.BlockSpec((1, MX_DH), lambda i: (0, 0)),
            pl.BlockSpec((3, D_MODEL), lambda i: (0, 0)),
            pl.BlockSpec((3, bw, D_MODEL), lambda i: (0, 0, 0)),
            pl.BlockSpec((D_MODEL, D_MODEL), lambda i: (0, 0)),
        ],
        out_specs=pl.BlockSpec((tm, D_MODEL), lambda i: (i, 0)),
        out_shape=jax.ShapeDtypeStruct((t, D_MODEL), F32),
        scratch_shapes=[pltpu.VMEM((tm, bw), BF16)],
        compiler_params=pltpu.CompilerParams(dimension_semantics=("parallel",)),
        name="merge",
    )(x2, o_dn, o_da, p, p, p, p, mk, mv, q_norm, b_gate, w_branch, w_out)


def _router_body(x_ref, g_ref, wr_ref, br_ref, h_ref, idx_ref, wt_ref, rank_ref, cnt_ref, carry):
    i = pl.program_id(0)
    tm = x_ref.shape[0]

    @pl.when(i == 0)
    def _():
        carry[...] = jnp.zeros_like(carry)

    x = x_ref[...]
    h = x * _rms(x, D_MODEL) * g_ref[...]
    h_ref[...] = h
    logits = _dot3(h, wr_ref[...]) + br_ref[...]
    lane = lax.broadcasted_iota(I32, (tm, LANES), 1)
    lane_f = lane.astype(F32)
    work = logits
    sel = jnp.zeros((tm, LANES), F32)
    vals, idxs = [], []
    for _ in range(TOP_K):
        mx = jnp.max(work, axis=-1, keepdims=True)
        ik = jnp.min(jnp.where(work == mx, lane_f, float(LANES)), axis=-1, keepdims=True)
        hit = lane_f == ik
        sel = jnp.where(hit, 1.0, sel)
        work = jnp.where(hit, -jnp.inf, work)
        vals.append(mx)
        idxs.append(ik)
    es = [jnp.exp(v - vals[0]) for v in vals]
    den = es[0] + es[1] + es[2] + es[3]
    r = lax.broadcasted_iota(I32, (tm, tm), 0)
    c = lax.broadcasted_iota(I32, (tm, tm), 1)
    tril = jnp.where(r > c, 1.0, 0.0).astype(BF16)
    cum = _dot(tril, sel.astype(BF16)) + carry[0:1, :]
    idx_o = jnp.zeros((tm, LANES), F32)
    wt_o = jnp.zeros((tm, LANES), F32)
    rank_o = jnp.zeros((tm, LANES), F32)
    for k in range(TOP_K):
        rk = jnp.sum(jnp.where(lane_f == idxs[k], cum, 0.0), axis=-1, keepdims=True)
        idx_o = jnp.where(lane == k, idxs[k], idx_o)
        wt_o = jnp.where(lane == k, es[k] / den, wt_o)
        rank_o = jnp.where(lane == k, rk, rank_o)
    idx_ref[...] = idx_o.astype(I32)
    wt_ref[...] = wt_o
    rank_ref[...] = rank_o.astype(I32)
    new = carry[...] + jnp.sum(sel, axis=0, keepdims=True)
    carry[...] = new
    cnt_ref[...] = new


def _router(x1, gain, w_r, b_r):
    t = x1.shape[0]
    tm = min(512, t)
    row = lambda i: (i, 0)
    fixed = lambda i: (0, 0)
    return pl.pallas_call(
        _router_body,
        grid=(t // tm,),
        in_specs=[
            pl.BlockSpec((tm, D_MODEL), row),
            pl.BlockSpec((1, D_MODEL), fixed),
            pl.BlockSpec((D_MODEL, LANES), fixed),
            pl.BlockSpec((1, LANES), fixed),
        ],
        out_specs=[
            pl.BlockSpec((tm, D_MODEL), row),
            pl.BlockSpec((tm, LANES), row),
            pl.BlockSpec((tm, LANES), row),
            pl.BlockSpec((tm, LANES), row),
            pl.BlockSpec((8, LANES), fixed),
        ],
        out_shape=[
            jax.ShapeDtypeStruct((t, D_MODEL), F32),
            jax.ShapeDtypeStruct((t, LANES), I32),
            jax.ShapeDtypeStruct((t, LANES), F32),
            jax.ShapeDtypeStruct((t, LANES), I32),
            jax.ShapeDtypeStruct((8, LANES), F32),
        ],
        scratch_shapes=[pltpu.VMEM((8, LANES), F32)],
        compiler_params=pltpu.CompilerParams(dimension_semantics=("arbitrary",)),
        name="router",
    )(x1, gain, w_r, b_r)


def _plan_body(cnt_ref, idx_ref, rank_ref, dest_ref, meta_ref):
    i = pl.program_id(0)
    tm = idx_ref.shape[0]
    eb = float(EXPERT_BLOCK)
    cnt = cnt_ref[...]
    lane8 = lax.broadcasted_iota(I32, (8, LANES), 1)
    padded = jnp.where(lane8 < N_EXPERTS, jnp.floor((cnt + (eb - 1.0)) * (1.0 / eb)) * eb, 0.0)
    pends = padded
    for s in (1, 2, 4, 8, 16):
        pends = pends + jnp.where(lane8 >= s, pltpu.roll(pends, s, axis=1), 0.0)
    pstarts = (pends - padded)[0:1, :]
    lane = lax.broadcasted_iota(I32, (tm, LANES), 1)
    idx = idx_ref[...]
    rank = rank_ref[...].astype(F32)
    dest = jnp.zeros((tm, LANES), F32)
    for k in range(TOP_K):
        base = jnp.sum(jnp.where(lane == idx[:, k:k + 1], pstarts, 0.0), axis=-1, keepdims=True)
        dest = jnp.where(lane == k, base + rank[:, k:k + 1], dest)
    dest_ref[...] = dest.astype(I32)

    @pl.when(i == 0)
    def _():
        nb = meta_ref.shape[0]
        ln = lax.broadcasted_iota(I32, (nb, LANES), 1)
        blk = lax.broadcasted_iota(I32, (nb, LANES), 0).astype(F32) * eb
        pe = pends[0:1, :]
        be = jnp.sum(jnp.where((ln < N_EXPERTS) & (pe <= blk), 1.0, 0.0), axis=-1, keepdims=True)
        be = jnp.minimum(be, float(N_EXPERTS - 1))
        used = jnp.sum(jnp.where(ln == N_EXPERTS - 1, pe, 0.0), axis=-1, keepdims=True) * (1.0 / eb)
        meta_ref[...] = jnp.where(ln == 0, be, jnp.where(ln == 1, used, 0.0)).astype(I32)


def _plan(counts, idx, rank, n_blocks_pad):
    t = idx.shape[0]
    tm = min(1024, t)
    row = lambda i: (i, 0)
    fixed = lambda i: (0, 0)
    return pl.pallas_call(
        _plan_body,
        grid=(t // tm,),
        in_specs=[pl.BlockSpec((8, LANES), fixed), pl.BlockSpec((tm, LANES), row),
                  pl.BlockSpec((tm, LANES), row)],
        out_specs=[pl.BlockSpec((tm, LANES), row), pl.BlockSpec((n_blocks_pad, LANES), fixed)],
        out_shape=[jax.ShapeDtypeStruct((t, LANES), I32),
                   jax.ShapeDtypeStruct((n_blocks_pad, LANES), I32)],
        compiler_params=pltpu.CompilerParams(dimension_semantics=("arbitrary",)),
        name="dispatch_plan",
    )(counts, idx, rank)


def _row_copy(src_ref, src_row, dst_ref, dst_row, sem):
    return pltpu.make_async_copy(src_ref.at[pl.ds(src_row, 1), :], dst_ref.at[pl.ds(dst_row, 1), :], sem)


def _dispatch_body(dest_ref, h_ref, xs_in_ref, xs_ref, sem):
    del xs_in_ref
    tm = h_ref.shape[0]

    def issue(t, carry):
        for k in range(TOP_K):
            _row_copy(h_ref, t, xs_ref, dest_ref[0, 0, t * TOP_K + k], sem).start()
        return carry

    lax.fori_loop(0, tm, issue, 0, unroll=4)

    def drain(t, carry):
        for _ in range(TOP_K):
            _row_copy(h_ref, 0, xs_ref, 0, sem).wait()
        return carry

    lax.fori_loop(0, tm, drain, 0, unroll=8)


def _dispatch(dest3, h2, xs_init):
    t = h2.shape[0]
    tm = dest3.shape[2] // TOP_K
    return pl.pallas_call(
        _dispatch_body,
        grid=(t // tm,),
        in_specs=[
            pl.BlockSpec((1, 1, TOP_K * tm), lambda i: (i, 0, 0), memory_space=pltpu.SMEM),
            pl.BlockSpec((tm, D_MODEL), lambda i: (i, 0)),
            pl.BlockSpec(memory_space=pl.ANY),
        ],
        out_specs=pl.BlockSpec(memory_space=pl.ANY),
        out_shape=jax.ShapeDtypeStruct(xs_init.shape, xs_init.dtype),
        scratch_shapes=[pltpu.SemaphoreType.DMA(())],
        input_output_aliases={2: 0},
        compiler_params=pltpu.CompilerParams(dimension_semantics=("arbitrary",)),
        name="moe_dispatch",
    )(dest3, h2, xs_init)


def _expert_body(be_ref, nu_ref, x_ref, wgu_ref, bgu_ref, wd_ref, bd_ref, y_ref, wgu_b, wd_b):
    j = pl.program_id(0)

    @pl.when(j < nu_ref[0])
    def _():
        @pl.when(jnp.logical_or(j == 0, be_ref[j] != be_ref[jnp.maximum(j - 1, 0)]))
        def _():
            wgu_b[...] = wgu_ref[0].astype(BF16)
            wd_b[...] = wd_ref[0].astype(BF16)

        gu = _dot(x_ref[...].astype(BF16), wgu_b[...]) + bgu_ref[0]
        gate = jnp.minimum(gu[:, :D_FF], SWIGLU_LIMIT)
        up = jnp.clip(gu[:, D_FF:], -SWIGLU_LIMIT, SWIGLU_LIMIT)
        act = (up + 1.0) * gate * _sigmoid(SWIGLU_ALPHA * gate)
        y_ref[...] = _dot(act.astype(BF16), wd_b[...]) + bd_ref[0]


def _experts(block_e, n_used, xs, wgu, bgu, wd, bd):
    n_slots = xs.shape[0]
    nb = n_slots // EXPERT_BLOCK

    def blk(j, be, nu):
        return jnp.minimum(j, nu[0] - 1)

    grid_spec = pltpu.PrefetchScalarGridSpec(
        num_scalar_prefetch=2,
        grid=(nb,),
        in_specs=[
            pl.BlockSpec((EXPERT_BLOCK, D_MODEL), lambda j, be, nu: (blk(j, be, nu), 0)),
            pl.BlockSpec((1, D_MODEL, 2 * D_FF), lambda j, be, nu: (be[blk(j, be, nu)], 0, 0)),
            pl.BlockSpec((1, 1, 2 * D_FF), lambda j, be, nu: (be[blk(j, be, nu)], 0, 0)),
            pl.BlockSpec((1, D_FF, D_MODEL), lambda j, be, nu: (be[blk(j, be, nu)], 0, 0)),
            pl.BlockSpec((1, 1, D_MODEL), lambda j, be, nu: (be[blk(j, be, nu)], 0, 0)),
        ],
        out_specs=pl.BlockSpec((EXPERT_BLOCK, D_MODEL), lambda j, be, nu: (blk(j, be, nu), 0)),
        scratch_shapes=[pltpu.VMEM((D_MODEL, 2 * D_FF), BF16), pltpu.VMEM((D_FF, D_MODEL), BF16)],
    )
    return pl.pallas_call(
        _expert_body,
        grid_spec=grid_spec,
        out_shape=jax.ShapeDtypeStruct((n_slots, D_MODEL), F32),
        compiler_params=pltpu.CompilerParams(dimension_semantics=("arbitrary",)),
        name="moe_experts",
    )(block_e, n_used, xs, wgu, bgu, wd, bd)


def _combine_body(dest_ref, x_ref, wt_ref, y_ref, out_ref, ybuf, sem):
    tm = x_ref.shape[0]

    def issue(t, carry):
        for k in range(TOP_K):
            _row_copy(y_ref, dest_ref[0, 0, t * TOP_K + k], ybuf, k * tm + t, sem).start()
        return carry

    lax.fori_loop(0, tm, issue, 0, unroll=4)

    def drain(t, carry):
        for _ in range(TOP_K):
            _row_copy(y_ref, 0, ybuf, 0, sem).wait()
        return carry

    lax.fori_loop(0, tm, drain, 0, unroll=8)

    acc = x_ref[...]
    wt = wt_ref[...]
    for k in range(TOP_K):
        acc = acc + wt[:, k:k + 1] * ybuf[k * tm:(k + 1) * tm, :]
    out_ref[...] = acc


def _combine(dest3, x1, wts, y):
    t = x1.shape[0]
    tm = dest3.shape[2] // TOP_K
    return pl.pallas_call(
        _combine_body,
        grid=(t // tm,),
        in_specs=[
            pl.BlockSpec((1, 1, TOP_K * tm), lambda i: (i, 0, 0), memory_space=pltpu.SMEM),
            pl.BlockSpec((tm, D_MODEL), lambda i: (i, 0)),
            pl.BlockSpec((tm, LANES), lambda i: (i, 0)),
            pl.BlockSpec(memory_space=pl.ANY),
        ],
        out_specs=pl.BlockSpec((tm, D_MODEL), lambda i: (i, 0)),
        out_shape=jax.ShapeDtypeStruct((t, D_MODEL), F32),
        scratch_shapes=[pltpu.VMEM((TOP_K * tm, D_MODEL), F32), pltpu.SemaphoreType.DMA(())],
        compiler_params=pltpu.CompilerParams(dimension_semantics=("arbitrary",)),
        name="moe_combine",
    )(dest3, x1, wts, y)


def _pad_lanes(v, fill=0.0):
    v = v.astype(F32).reshape(1, -1)
    return jnp.pad(v, ((0, 0), (0, LANES - v.shape[1])), constant_values=fill)


def _mixer(x2, mem, rel_table, attn_norm, w_in, b_gate, dn_conv, dn_a_log, dn_dt_bias, dn_out_norm,
           da_q_norm, da_k_norm, da_lambda, da_subln, mem_norm, w_mem_kv, mx_q_norm, mx_k_norm,
           w_branch, w_out, batch, seq):
    wp = jnp.concatenate([w_in[:, :W_AB_LO], w_in[:, W_AB_HI:]], axis=1).astype(BF16)
    wab = jnp.pad(w_in[:, W_AB_LO:W_AB_HI], ((0, 0), (0, LANES - (W_AB_HI - W_AB_LO))))
    p, ab = _inproj(x2, attn_norm.reshape(1, -1), wp, wab)

    o_dn = _deltanet(p, ab, dn_conv, _pad_lanes(dn_a_log), _pad_lanes(dn_dt_bias),
                     dn_out_norm.reshape(1, -1), batch, seq)

    tq = min(ATT_BLOCK, seq)
    bias = _bias_tiles(rel_table.T, tq)
    o_da = _attention(p, bias, jnp.tile(da_q_norm, 2).reshape(1, -1), jnp.tile(da_k_norm, 2).reshape(1, -1),
                      da_lambda, da_subln.reshape(-1, 1), batch, seq)

    mk, mv = _memkv(mem, mem_norm.reshape(1, -1), w_mem_kv.astype(BF16), mx_k_norm.reshape(1, -1))
    return _merge(x2, o_dn, o_da, p, mk, mv, mx_q_norm.reshape(1, -1), b_gate.reshape(3, D_MODEL),
                  w_branch.astype(BF16), w_out.astype(BF16), seq)


def _moe(x1, ffn_norm, w_router, b_router, w_gate_up, b_gate_up, w_down, b_down):
    t = x1.shape[0]
    m = t * TOP_K
    n_blocks = -(-m // EXPERT_BLOCK) + N_EXPERTS
    n_blocks_pad = -(-n_blocks // 8) * 8
    n_slots = n_blocks * EXPERT_BLOCK

    wr = jnp.pad(w_router, ((0, 0), (0, LANES - N_EXPERTS)))
    h2, idx, wts, rank, counts = _router(x1, ffn_norm.reshape(1, -1), wr, _pad_lanes(b_router, NEG))
    dest, meta = _plan(counts, idx, rank, n_blocks_pad)
    block_e = meta[:n_blocks, 0]
    n_used = meta[0:1, 1]

    tm_d = min(512, t)
    dest_d = dest[:, :TOP_K].reshape(t // tm_d, 1, TOP_K * tm_d)
    xs = _dispatch(dest_d, h2, jnp.zeros((n_slots, D_MODEL), F32))
    y = _experts(block_e, n_used, xs, w_gate_up, b_gate_up.reshape(N_EXPERTS, 1, -1),
                 w_down, b_down.reshape(N_EXPERTS, 1, -1))
    tm_c = min(256, t)
    dest_c = dest[:, :TOP_K].reshape(t // tm_c, 1, TOP_K * tm_c)
    return _combine(dest_c, x1, wts, y)


def kernel(x, mem, rel_table, attn_norm, w_in, b_gate, dn_conv, dn_a_log, dn_dt_bias, dn_out_norm,
           da_q_norm, da_k_norm, da_lambda, da_subln, mem_norm, w_mem_kv, mx_q_norm, mx_k_norm,
           w_branch, w_out, ffn_norm, w_router, b_router, w_gate_up, b_gate_up, w_down, b_down):
    batch, seq, d = x.shape
    x2 = x.reshape(batch * seq, d)
    x1 = _mixer(x2, mem, rel_table, attn_norm[0], w_in[0], b_gate[0], dn_conv[0], dn_a_log[0],
                dn_dt_bias[0], dn_out_norm[0], da_q_norm[0], da_k_norm[0], da_lambda[0], da_subln[0],
                mem_norm[0], w_mem_kv[0], mx_q_norm[0], mx_k_norm[0], w_branch[0], w_out[0], batch, seq)
    out = _moe(x1, ffn_norm[0], w_router[0], b_router[0], w_gate_up[0], b_gate_up[0], w_down[0],
               b_down[0])
    return out.reshape(batch, seq, d)
```

```python
import functools
import math

import jax
import jax.numpy as jnp
from jax import lax
from jax.experimental import pallas as pl
from jax.experimental.pallas import tpu as pltpu

F32 = jnp.float32
BF16 = jnp.bfloat16
I32 = jnp.int32

D_MODEL = 1024
EPS = 1e-6
LANES = 128

DN_HEADS = 4
DN_DK = 128
DN_CHUNK = 64
DN_CONV = 4

DA_HEADS = 4
DA_DH = 64

MX_HEADS = 4
MX_DH = 128

REL_BUCKETS = 32
REL_MAX_DIST = 128

N_EXPERTS = 32
TOP_K = 4
D_FF = 1024
SWIGLU_LIMIT = 7.0
SWIGLU_ALPHA = 1.702
EXPERT_BLOCK = 512

LAM_INIT = 0.8 - 0.6 * math.exp(-0.3 * 0)
LOG2E = 1.4426950408889634
NEG = -1e30

P_DNQ, P_DNK, P_DNV, P_DNZ = 0, 512, 1024, 1536
P_DAQ, P_DAK, P_DAV = 2048, 2560, 3072
P_MXQ = 3584
P_GATE = 4096
P_COLS = 7168
W_AB_LO, W_AB_HI = 2048, 2056


def _dot(a, b):
    return jnp.dot(a, b, preferred_element_type=F32)


def _dot_nt(a, b):
    return lax.dot_general(a, b, (((1,), (1,)), ((), ())), preferred_element_type=F32)


def _dot_tn(a, b):
    return lax.dot_general(a, b, (((0,), (0,)), ((), ())), preferred_element_type=F32)


def _split(x):
    hi = x.astype(BF16)
    lo = (x - hi.astype(F32)).astype(BF16)
    return hi, lo


def _dot3(a, b):
    ah, al = _split(a)
    bh, bl = _split(b)
    return _dot(ah, bh) + _dot(ah, bl) + _dot(al, bh)


def _sigmoid(x):
    return 1.0 / (1.0 + jnp.exp(-x))


def _rms(x, n):
    return lax.rsqrt(jnp.sum(x * x, axis=-1, keepdims=True) * (1.0 / n) + EPS)


def _inproj_body(x_ref, g_ref, w_ref, wab_ref, p_ref, ab_ref, h_scr):
    @pl.when(pl.program_id(1) == 0)
    def _():
        x = x_ref[...]
        h = x * _rms(x, D_MODEL) * g_ref[...]
        h_scr[...] = h.astype(BF16)
        ab_ref[...] = _dot3(h, wab_ref[...])

    p_ref[...] = _dot(h_scr[...], w_ref[...]).astype(p_ref.dtype)


def _inproj(x2, gain, wp, wab):
    t = x2.shape[0]
    tm = min(1024, t)
    tn = 1024
    return pl.pallas_call(
        _inproj_body,
        grid=(t // tm, P_COLS // tn),
        in_specs=[
            pl.BlockSpec((tm, D_MODEL), lambda i, j: (i, 0)),
            pl.BlockSpec((1, D_MODEL), lambda i, j: (0, 0)),
            pl.BlockSpec((D_MODEL, tn), lambda i, j: (0, j)),
            pl.BlockSpec((D_MODEL, LANES), lambda i, j: (0, 0)),
        ],
        out_specs=[
            pl.BlockSpec((tm, tn), lambda i, j: (i, j)),
            pl.BlockSpec((tm, LANES), lambda i, j: (i, 0)),
        ],
        out_shape=[
            jax.ShapeDtypeStruct((t, P_COLS), BF16),
            jax.ShapeDtypeStruct((t, LANES), F32),
        ],
        scratch_shapes=[pltpu.VMEM((tm, D_MODEL), BF16)],
        compiler_params=pltpu.CompilerParams(dimension_semantics=("parallel", "arbitrary")),
        name="inproj",
    )(x2, gain, wp, wab)


DN_HALO = 16
DN_SCAN_CHUNK = 256


def _deltanet_body(q_ref, k_ref, v_ref, z_ref, qh_ref, kh_ref, vh_ref, ab_ref, cw_ref, alog_ref,
                   dtb_ref, on_ref, o_ref, stage, qs, ks, vs, s_scr):
    i = pl.program_id(1)
    tc = q_ref.shape[0]
    hw = DN_HEADS * DN_DK

    @pl.when(i == 0)
    def _():
        s_scr[...] = jnp.zeros_like(s_scr)

    for src, halo, dst, off, kind in ((q_ref, qh_ref, qs, 0, "q"), (k_ref, kh_ref, ks, hw, "k"),
                                      (v_ref, vh_ref, vs, 2 * hw, "v")):
        hal = halo[...].astype(F32)
        stage[0:DN_HALO, :] = jnp.where(i == 0, 0.0, hal)
        stage[DN_HALO:DN_HALO + tc, :] = src[...].astype(F32)
        base = DN_HALO - (DN_CONV - 1)
        y = stage[base:base + tc, :] * cw_ref[0:1, off:off + hw]
        for j in range(1, DN_CONV):
            y = y + stage[base + j:base + j + tc, :] * cw_ref[j:j + 1, off:off + hw]
        y = y * _sigmoid(y)
        if kind == "v":
            dst[...] = y
        else:
            for h in range(DN_HEADS):
                sl = slice(h * DN_DK, (h + 1) * DN_DK)
                yh = y[:, sl]
                r = lax.rsqrt(jnp.sum(yh * yh, axis=-1, keepdims=True) + EPS)
                if kind == "q":
                    r = r * (DN_DK ** -0.5)
                dst[:, sl] = yh * r

    c = min(DN_SCAN_CHUNK, tc)
    row = lax.broadcasted_iota(I32, (c, c), 0)
    col = lax.broadcasted_iota(I32, (c, c), 1)
    incl = row >= col
    strict = row > col
    same_blk = (row // DN_CHUNK) == (col // DN_CHUNK)
    tri = jnp.where(incl, 1.0, 0.0).astype(BF16)
    eye = jnp.where(row == col, 1.0, 0.0)
    neg_a = -jnp.exp(alog_ref[...])
    dtb = dtb_ref[...]

    def chunk(ci, carry):
        r0 = pl.multiple_of(ci * c, c)
        abc = ab_ref[pl.ds(r0, c), :]
        a_in = abc + dtb
        g_all = neg_a * (jnp.maximum(a_in, 0.0) + jnp.log(1.0 + jnp.exp(-jnp.abs(a_in))))
        beta_all = _sigmoid(abc)
        zc = z_ref[pl.ds(r0, c), :].astype(F32)
        for h in range(DN_HEADS):
            sl = slice(h * DN_DK, (h + 1) * DN_DK)
            q = qs[pl.ds(r0, c), sl]
            k = ks[pl.ds(r0, c), sl]
            v = vs[pl.ds(r0, c), sl]
            g = g_all[:, h:h + 1]
            beta = beta_all[:, DN_HEADS + h:DN_HEADS + h + 1]
            g_hi, g_lo = _split(jnp.where(strict, g, 0.0))
            diff = _dot(tri, g_hi) + _dot(tri, g_lo)
            gc = diff[:, 0:1] + g[0:1, :]
            decay = jnp.where(incl, jnp.exp(diff), 0.0)
            kb = k.astype(BF16)
            qkk = _dot_nt(jnp.concatenate([q.astype(BF16), kb], axis=0), kb)
            qk = qkk[:c]
            kk = qkk[c:]
            lower = jnp.where(strict, kk * decay * beta, 0.0)
            pw = jnp.where(same_blk, -lower, 0.0)
            dinv = eye + pw
            for _ in range(int(math.log2(DN_CHUNK)) - 1):
                pwb = pw.astype(BF16)
                pw = _dot(pwb, pwb)
                dinv = dinv + _dot(dinv.astype(BF16), pw.astype(BF16))
            dinv_b = dinv.astype(BF16)
            pw = -_dot(dinv_b, jnp.where(same_blk, 0.0, lower).astype(BF16))
            xm = eye + pw
            for _ in range(int(math.log2(c // DN_CHUNK)) - 1):
                pwb = pw.astype(BF16)
                pw = _dot(pwb, pwb)
                xm = xm + _dot(xm.astype(BF16), pw.astype(BF16))
            inv = _dot(xm.astype(BF16), dinv_b)
            egc = jnp.exp(gc)
            rhs = jnp.concatenate([v * beta, k * (beta * egc)], axis=1)
            sol = _dot(inv.astype(BF16), rhs.astype(BF16))
            u = sol[:, :DN_DK]
            w = sol[:, DN_DK:]
            qkm = jnp.where(incl, qk * decay, 0.0)
            gl = gc[c - 1:c, :]
            state = s_scr[h]
            sb = state.astype(BF16)
            ws = _dot(jnp.concatenate([w.astype(BF16), (q * egc).astype(BF16)], axis=0), sb)
            v_new = u - ws[:c]
            o = ws[c:] + _dot(qkm.astype(BF16), v_new.astype(BF16))
            s_scr[h] = state * jnp.exp(gl) + _dot_tn(kb, (v_new * jnp.exp(gl - gc)).astype(BF16))
            zz = zc[:, sl]
            on = o * _rms(o, DN_DK) * on_ref[...]
            o_ref[pl.ds(r0, c), sl] = (on * (zz * _sigmoid(zz))).astype(o_ref.dtype)
        return carry

    lax.fori_loop(0, tc // c, chunk, 0)


def _deltanet(p, ab, conv_w, alog_row, dtb_row, out_norm, batch, seq):
    t = batch * seq
    tc = min(512, seq)
    nt = seq // tc
    hw = DN_HEADS * DN_DK

    def main(cb):
        return pl.BlockSpec((tc, hw), lambda b, i: (b * nt + i, cb))

    def halo(cb):
        return pl.BlockSpec(
            (DN_HALO, hw),
            lambda b, i: (jnp.maximum((b * seq + i * tc) // DN_HALO - 1, 0), cb))

    return pl.pallas_call(
        _deltanet_body,
        grid=(batch, nt),
        in_specs=[
            main(P_DNQ // hw), main(P_DNK // hw), main(P_DNV // hw), main(P_DNZ // hw),
            halo(P_DNQ // hw), halo(P_DNK // hw), halo(P_DNV // hw),
            pl.BlockSpec((tc, LANES), lambda b, i: (b * nt + i, 0)),
            pl.BlockSpec((DN_CONV, 3 * hw), lambda b, i: (0, 0)),
            pl.BlockSpec((1, LANES), lambda b, i: (0, 0)),
            pl.BlockSpec((1, LANES), lambda b, i: (0, 0)),
            pl.BlockSpec((1, DN_DK), lambda b, i: (0, 0)),
        ],
        out_specs=pl.BlockSpec((tc, hw), lambda b, i: (b * nt + i, 0)),
        out_shape=jax.ShapeDtypeStruct((t, hw), BF16),
        scratch_shapes=[
            pltpu.VMEM((DN_HALO + tc, hw), F32),
            pltpu.VMEM((tc, hw), F32),
            pltpu.VMEM((tc, hw), F32),
            pltpu.VMEM((tc, hw), F32),
            pltpu.VMEM((DN_HEADS, DN_DK, DN_DK), F32),
        ],
        compiler_params=pltpu.CompilerParams(dimension_semantics=("parallel", "arbitrary")),
        name="deltanet",
    )(p, p, p, p, p, p, p, ab, conv_w, alog_row, dtb_row, out_norm)


ATT_BLOCK = 512


def _bias_body(tbl_ref, o_ref):
    h = pl.program_id(0)
    tq = o_ref.shape[2]
    key = lax.broadcasted_iota(I32, (tq, tq), 0)
    qry = lax.broadcasted_iota(I32, (tq, tq), 1)
    max_exact = REL_BUCKETS // 2
    far = tbl_ref[h, REL_BUCKETS - 1]
    for d in range(2):
        n = qry - key + d * tq
        nn = jnp.maximum(n, 0)
        nf = jnp.maximum(nn, 1).astype(F32)
        large = max_exact + (jnp.log(nf / max_exact) / math.log(REL_MAX_DIST / max_exact)
                             * (REL_BUCKETS - max_exact)).astype(I32)
        large = jnp.minimum(large, REL_BUCKETS - 1)
        bucket = jnp.where(nn < max_exact, nn, large)
        val = jnp.zeros((tq, tq), F32)
        for b in range(REL_BUCKETS):
            val = jnp.where(bucket == b, tbl_ref[h, b], val)
        o_ref[0, d] = jnp.where(n >= 0, (val - far) * LOG2E, NEG)


def _bias_tiles(tbl_t, tq):
    return pl.pallas_call(
        _bias_body,
        grid=(DA_HEADS,),
        in_specs=[pl.BlockSpec(memory_space=pltpu.SMEM)],
        out_specs=pl.BlockSpec((1, 2, tq, tq), lambda h: (h, 0, 0, 0)),
        out_shape=jax.ShapeDtypeStruct((DA_HEADS, 2, tq, tq), F32),
        name="t5_bias_tiles",
    )(tbl_t)


DA_DV = 2 * DA_DH
DA_VROWS = DA_DV + 16


BOUND_SLACK = 1.02
MAX_SHIFT_GAP = 110.0


def _attn_body(q_ref, k_ref, v_ref, bias_ref, qg_ref, kg_ref, lam_ref, sg_ref, o_ref,
               kn, vt, kst, m_s, acc_s):
    qi = pl.program_id(2)
    tq = q_ref.shape[0]
    seq = k_ref.shape[0]
    tk = tq
    lo_mask = lax.broadcasted_iota(I32, (1, DA_DV), 1) < DA_DH

    def group_norm(x, gain):
        x2 = x * x
        lo = jnp.sum(jnp.where(lo_mask, x2, 0.0), axis=-1, keepdims=True)
        hi = jnp.sum(jnp.where(lo_mask, 0.0, x2), axis=-1, keepdims=True)
        r = jnp.where(lo_mask, lax.rsqrt(lo * (1.0 / DA_DH) + EPS), lax.rsqrt(hi * (1.0 / DA_DH) + EPS))
        return x * r * gain

    @pl.when(qi == 0)
    def _():
        ones = jnp.ones((DA_VROWS - DA_DV, tk), BF16)

        def body(c, kmax2):
            r0 = pl.multiple_of(c * tk, tk)
            kb = group_norm(k_ref[pl.ds(r0, tk), :].astype(F32), kg_ref[...]).astype(BF16)
            kn[pl.ds(r0, tk), :] = kb
            vt[c, 0:DA_DV, :] = v_ref[pl.ds(r0, tk), :].astype(F32).T.astype(BF16)
            vt[c, DA_DV:DA_VROWS, :] = ones
            k2 = kb.astype(F32)
            k2 = k2 * k2
            lo = jnp.max(jnp.sum(jnp.where(lo_mask, k2, 0.0), axis=-1, keepdims=True), axis=0, keepdims=True)
            hi = jnp.max(jnp.sum(jnp.where(lo_mask, 0.0, k2), axis=-1, keepdims=True), axis=0, keepdims=True)
            return jnp.maximum(kmax2, jnp.where(lo_mask, lo, hi))
        kst[0:1, :] = lax.fori_loop(0, seq // tk, body, jnp.zeros((1, DA_DV), F32))
        b0 = bias_ref[0, 0]
        b1 = bias_ref[0, 1]
        bmax = jnp.maximum(jnp.max(jnp.maximum(b0, b1), axis=0, keepdims=True), 0.0)
        bmin = jnp.minimum(jnp.min(jnp.minimum(jnp.where(b0 > 0.5 * NEG, b0, 0.0), b1), axis=0, keepdims=True), 0.0)
        kst[1:2, :] = jnp.broadcast_to(jnp.max(bmax, axis=1, keepdims=True), (1, DA_DV))
        kst[2:3, :] = jnp.broadcast_to(jnp.min(bmin, axis=1, keepdims=True), (1, DA_DV))

    q = group_norm(q_ref[...].astype(F32), qg_ref[...]) * (DA_DH ** -0.5 * LOG2E)
    qcat = jnp.concatenate([jnp.where(lo_mask, q, 0.0), jnp.where(lo_mask, 0.0, q)], axis=0).astype(BF16)
    acc_s[...] = jnp.zeros_like(acc_s)

    q2 = q * q * kst[0:1, :]
    ones8 = jnp.ones((8, DA_DV), BF16)
    bmax = kst[1:2, 0:1]
    bmin = kst[2:3, 0:1]
    bound = []
    for m in range(2):
        q2m = jnp.where(lo_mask, q2, 0.0) if m == 0 else jnp.where(lo_mask, 0.0, q2)
        bound.append(jnp.sqrt(_dot_nt(ones8, q2m.astype(BF16))[0:1, :]) * BOUND_SLACK)
    bound = jnp.concatenate(bound, axis=1)
    worst = jnp.max(2.0 * bound, axis=1, keepdims=True) + bmax - bmin
    safe = worst[0, 0] <= MAX_SHIFT_GAP

    def block(j, d, fixed_shift):
        r0 = pl.multiple_of(j * tk, tk)
        st = _dot_nt(kn[pl.ds(r0, tk), :], qcat)
        if d is not None:
            bias = bias_ref[0, d]
            st = st + jnp.concatenate([bias, bias], axis=1)
        if fixed_shift:
            acc_s[...] = acc_s[...] + _dot(vt[j], jnp.exp2(st - m_s[...]).astype(BF16))
        else:
            m_prev = m_s[...]
            m_new = jnp.maximum(m_prev, jnp.max(st, axis=0, keepdims=True))
            alpha = jnp.exp2(m_prev - m_new)
            acc_s[...] = alpha * acc_s[...] + _dot(vt[j], jnp.exp2(st - m_new).astype(BF16))
            m_s[...] = m_new

    def run(fixed_shift):
        n_far = jnp.maximum(qi - 1, 0)

        def far_pair(jj, carry):
            block(2 * jj, None, fixed_shift)
            block(2 * jj + 1, None, fixed_shift)
            return carry

        lax.fori_loop(0, n_far // 2, far_pair, 0)

        @pl.when(n_far % 2 == 1)
        def _():
            block(n_far - 1, None, fixed_shift)

        @pl.when(qi >= 1)
        def _():
            block(qi - 1, 1, fixed_shift)

        block(qi, 0, fixed_shift)

    @pl.when(safe)
    def _():
        m_s[...] = bound + bmax
        run(True)

    @pl.when(jnp.logical_not(safe))
    def _():
        m_s[...] = jnp.full(m_s.shape, NEG, F32)
        run(False)

    lam_p = lam_ref[...]
    lam = (jnp.exp(jnp.sum(lam_p[0:1, :] * lam_p[1:2, :], axis=-1, keepdims=True))
           - jnp.exp(jnp.sum(lam_p[2:3, :] * lam_p[3:4, :], axis=-1, keepdims=True)) + LAM_INIT)
    a0 = acc_s[:, 0:tq]
    a1 = acc_s[:, tq:2 * tq]
    ot = a0[0:DA_DV] / a0[DA_DV:DA_DV + 1] - lam * (a1[0:DA_DV] / a1[DA_DV:DA_DV + 1])
    r = lax.rsqrt(jnp.sum(ot * ot, axis=0, keepdims=True) * (1.0 / DA_DV) + EPS)
    ot = ot * r * (sg_ref[...] * (1.0 - LAM_INIT))
    o_ref[...] = ot.T.astype(o_ref.dtype)


def _attention(p, bias, qg, kg, lam_p, subln, batch, seq):
    t = batch * seq
    tq = min(ATT_BLOCK, seq)
    nq = seq // tq
    dv = DA_DV
    return pl.pallas_call(
        _attn_body,
        grid=(batch, DA_HEADS, nq),
        in_specs=[
            pl.BlockSpec((tq, dv), lambda b, h, i: (b * nq + i, P_DAQ // dv + h)),
            pl.BlockSpec((seq, dv), lambda b, h, i: (b, P_DAK // dv + h)),
            pl.BlockSpec((seq, dv), lambda b, h, i: (b, P_DAV // dv + h)),
            pl.BlockSpec((1, 2, tq, tq), lambda b, h, i: (h, 0, 0, 0)),
            pl.BlockSpec((1, dv), lambda b, h, i: (0, 0)),
            pl.BlockSpec((1, dv), lambda b, h, i: (0, 0)),
            pl.BlockSpec((4, DA_DH), lambda b, h, i: (0, 0)),
            pl.BlockSpec((dv, 1), lambda b, h, i: (0, 0)),
        ],
        out_specs=pl.BlockSpec((tq, dv), lambda b, h, i: (b * nq + i, h)),
        out_shape=jax.ShapeDtypeStruct((t, DA_HEADS * dv), BF16),
        scratch_shapes=[
            pltpu.VMEM((seq, dv), BF16),
            pltpu.VMEM((seq // tq, DA_VROWS, tq), BF16),
            pltpu.VMEM((8, dv), F32),
            pltpu.VMEM((1, 2 * tq), F32),
            pltpu.VMEM((DA_VROWS, 2 * tq), F32),
        ],
        compiler_params=pltpu.CompilerParams(dimension_semantics=("parallel", "parallel", "arbitrary")),
        name="diff_attention",
    )(p, p, p, bias, qg, kg, lam_p, subln)


def _memkv_body(mem_ref, mg_ref, w_ref, kg_ref, mk_ref, mv_ref):
    x = mem_ref[0]
    xn = x * _rms(x, D_MODEL) * mg_ref[...]
    kv = _dot(xn.astype(BF16), w_ref[...])
    hw = MX_HEADS * MX_DH
    for h in range(MX_HEADS):
        sl = slice(h * MX_DH, (h + 1) * MX_DH)
        kh = kv[:, sl]
        mk_ref[0, :, sl] = (kh * _rms(kh, MX_DH) * kg_ref[...]).astype(BF16)
    mv_ref[0] = kv[:, hw:].astype(BF16)


def _memkv(mem, mem_norm, w_kv, k_norm):
    b, n, _ = mem.shape
    hw = MX_HEADS * MX_DH
    return pl.pallas_call(
        _memkv_body,
        grid=(b,),
        in_specs=[
            pl.BlockSpec((1, n, D_MODEL), lambda i: (i, 0, 0)),
            pl.BlockSpec((1, D_MODEL), lambda i: (0, 0)),
            pl.BlockSpec((D_MODEL, 2 * hw), lambda i: (0, 0)),
            pl.BlockSpec((1, MX_DH), lambda i: (0, 0)),
        ],
        out_specs=[pl.BlockSpec((1, n, hw), lambda i: (i, 0, 0))] * 2,
        out_shape=[jax.ShapeDtypeStruct((b, n, hw), BF16)] * 2,
        name="memory_kv",
    )(mem, mem_norm, w_kv, k_norm)


def _merge_body(x_ref, odn_ref, oda_ref, mxq_ref, g0_ref, g1_ref, g2_ref, mk_ref, mv_ref, qg_ref,
                bg_ref, wb_ref, wo_ref, out_ref, omx):
    for h in range(MX_HEADS):
        sl = slice(h * MX_DH, (h + 1) * MX_DH)
        qh = mxq_ref[:, sl].astype(F32)
        qh = qh * _rms(qh, MX_DH) * qg_ref[...] * (MX_DH ** -0.5 * LOG2E)
        s = _dot_nt(qh.astype(BF16), mk_ref[0, :, sl])
        p = jnp.exp2(s - jnp.max(s, axis=-1, keepdims=True))
        oh = _dot(p.astype(BF16), mv_ref[0, :, sl]) / jnp.sum(p, axis=-1, keepdims=True)
        omx[:, sl] = oh.astype(BF16)
    y = None
    for r, (o_r, g_r) in enumerate(((odn_ref, g0_ref), (oda_ref, g1_ref), (omx, g2_ref))):
        gate = _sigmoid(g_r[...].astype(F32) + bg_ref[r:r + 1, :])
        term = gate * _dot(o_r[...], wb_ref[r])
        y = term if y is None else y + term
    out_ref[...] = x_ref[...] + _dot(y.astype(BF16), wo_ref[...])


def _merge(x2, o_dn, o_da, p, mk, mv, q_norm, b_gate, w_branch, w_out, seq):
    t = x2.shape[0]
    tm = min(512, seq)
    nt = seq // tm
    bw = 512
    n_mem = mk.shape[1]
    return pl.pallas_call(
        _merge_body,
        grid=(t // tm,),
        in_specs=[
            pl.BlockSpec((tm, D_MODEL), lambda i: (i, 0)),
            pl.BlockSpec((tm, bw), lambda i: (i, 0)),
            pl.BlockSpec((tm, bw), lambda i: (i, 0)),
            pl.BlockSpec((tm, bw), lambda i: (i, P_MXQ // bw)),
            pl.BlockSpec((tm, D_MODEL), lambda i: (i, P_GATE // D_MODEL)),
            pl.BlockSpec((tm, D_MODEL), lambda i: (i, P_GATE // D_MODEL + 1)),
            pl.BlockSpec((tm, D_MODEL), lambda i: (i, P_GATE // D_MODEL + 2)),
            pl.BlockSpec((1, n_mem, bw), lambda i: (i // nt, 0, 0)),
            pl.BlockSpec((1, n_mem, bw), lambda i: (i // nt, 0, 0)),
            pl.BlockSpec((1, MX_DH), lambda i: (0, 0)),
            pl.BlockSpec((3, D_MODEL), lambda i: (0, 0)),
            pl.BlockSpec((3, bw, D_MODEL), lambda i: (0, 0, 0)),
            pl.BlockSpec((D_MODEL, D_MODEL), lambda i: (0, 0)),
        ],
        out_specs=pl.BlockSpec((tm, D_MODEL), lambda i: (i, 0)),
        out_shape=jax.ShapeDtypeStruct((t, D_MODEL), F32),
        scratch_shapes=[pltpu.VMEM((tm, bw), BF16)],
        compiler_params=pltpu.CompilerParams(dimension_semantics=("parallel",)),
        name="merge",
    )(x2, o_dn, o_da, p, p, p, p, mk, mv, q_norm, b_gate, w_branch, w_out)


def _router_body(x_ref, g_ref, wr_ref, br_ref, h_ref, idx_ref, wt_ref, rank_ref, cnt_ref, carry):
    i = pl.program_id(0)
    tm = x_ref.shape[0]

    @pl.when(i == 0)
    def _():
        carry[...] = jnp.zeros_like(carry)

    x = x_ref[...]
    h = x * _rms(x, D_MODEL) * g_ref[...]
    h_ref[...] = h
    logits = _dot3(h, wr_ref[...]) + br_ref[...]
    lane = lax.broadcasted_iota(I32, (tm, LANES), 1)
    lane_f = lane.astype(F32)
    work = logits
    sel = jnp.zeros((tm, LANES), F32)
    vals, idxs = [], []
    for _ in range(TOP_K):
        mx = jnp.max(work, axis=-1, keepdims=True)
        ik = jnp.min(jnp.where(work == mx, lane_f, float(LANES)), axis=-1, keepdims=True)
        hit = lane_f == ik
        sel = jnp.where(hit, 1.0, sel)
        work = jnp.where(hit, -jnp.inf, work)
        vals.append(mx)
        idxs.append(ik)
    es = [jnp.exp(v - vals[0]) for v in vals]
    den = es[0] + es[1] + es[2] + es[3]
    r = lax.broadcasted_iota(I32, (tm, tm), 0)
    c = lax.broadcasted_iota(I32, (tm, tm), 1)
    tril = jnp.where(r > c, 1.0, 0.0).astype(BF16)
    cum = _dot(tril, sel.astype(BF16)) + carry[0:1, :]
    idx_o = jnp.zeros((tm, LANES), F32)
    wt_o = jnp.zeros((tm, LANES), F32)
    rank_o = jnp.zeros((tm, LANES), F32)
    for k in range(TOP_K):
        rk = jnp.sum(jnp.where(lane_f == idxs[k], cum, 0.0), axis=-1, keepdims=True)
        idx_o = jnp.where(lane == k, idxs[k], idx_o)
        wt_o = jnp.where(lane == k, es[k] / den, wt_o)
        rank_o = jnp.where(lane == k, rk, rank_o)
    idx_ref[...] = idx_o.astype(I32)
    wt_ref[...] = wt_o
    rank_ref[...] = rank_o.astype(I32)
    new = carry[...] + jnp.sum(sel, axis=0, keepdims=True)
    carry[...] = new
    cnt_ref[...] = new


def _router(x1, gain, w_r, b_r):
    t = x1.shape[0]
    tm = min(512, t)
    row = lambda i: (i, 0)
    fixed = lambda i: (0, 0)
    return pl.pallas_call(
        _router_body,
        grid=(t // tm,),
        in_specs=[
            pl.BlockSpec((tm, D_MODEL), row),
            pl.BlockSpec((1, D_MODEL), fixed),
            pl.BlockSpec((D_MODEL, LANES), fixed),
            pl.BlockSpec((1, LANES), fixed),
        ],
        out_specs=[
            pl.BlockSpec((tm, D_MODEL), row),
            pl.BlockSpec((tm, LANES), row),
            pl.BlockSpec((tm, LANES), row),
            pl.BlockSpec((tm, LANES), row),
            pl.BlockSpec((8, LANES), fixed),
        ],
        out_shape=[
            jax.ShapeDtypeStruct((t, D_MODEL), F32),
            jax.ShapeDtypeStruct((t, LANES), I32),
            jax.ShapeDtypeStruct((t, LANES), F32),
            jax.ShapeDtypeStruct((t, LANES), I32),
            jax.ShapeDtypeStruct((8, LANES), F32),
        ],
        scratch_shapes=[pltpu.VMEM((8, LANES), F32)],
        compiler_params=pltpu.CompilerParams(dimension_semantics=("arbitrary",)),
        name="router",
    )(x1, gain, w_r, b_r)


def _plan_body(cnt_ref, idx_ref, rank_ref, dest_ref, meta_ref):
    i = pl.program_id(0)
    tm = idx_ref.shape[0]
    eb = float(EXPERT_BLOCK)
    cnt = cnt_ref[...]
    lane8 = lax.broadcasted_iota(I32, (8, LANES), 1)
    padded = jnp.where(lane8 < N_EXPERTS, jnp.floor((cnt + (eb - 1.0)) * (1.0 / eb)) * eb, 0.0)
    pends = padded
    for s in (1, 2, 4, 8, 16):
        pends = pends + jnp.where(lane8 >= s, pltpu.roll(pends, s, axis=1), 0.0)
    pstarts = (pends - padded)[0:1, :]
    lane = lax.broadcasted_iota(I32, (tm, LANES), 1)
    idx = idx_ref[...]
    rank = rank_ref[...].astype(F32)
    dest = jnp.zeros((tm, LANES), F32)
    for k in range(TOP_K):
        base = jnp.sum(jnp.where(lane == idx[:, k:k + 1], pstarts, 0.0), axis=-1, keepdims=True)
        dest = jnp.where(lane == k, base + rank[:, k:k + 1], dest)
    dest_ref[...] = dest.astype(I32)

    @pl.when(i == 0)
    def _():
        nb = meta_ref.shape[0]
        ln = lax.broadcasted_iota(I32, (nb, LANES), 1)
        blk = lax.broadcasted_iota(I32, (nb, LANES), 0).astype(F32) * eb
        pe = pends[0:1, :]
        be = jnp.sum(jnp.where((ln < N_EXPERTS) & (pe <= blk), 1.0, 0.0), axis=-1, keepdims=True)
        be = jnp.minimum(be, float(N_EXPERTS - 1))
        used = jnp.sum(jnp.where(ln == N_EXPERTS - 1, pe, 0.0), axis=-1, keepdims=True) * (1.0 / eb)
        meta_ref[...] = jnp.where(ln == 0, be, jnp.where(ln == 1, used, 0.0)).astype(I32)


def _plan(counts, idx, rank, n_blocks_pad):
    t = idx.shape[0]
    tm = min(1024, t)
    row = lambda i: (i, 0)
    fixed = lambda i: (0, 0)
    return pl.pallas_call(
        _plan_body,
        grid=(t // tm,),
        in_specs=[pl.BlockSpec((8, LANES), fixed), pl.BlockSpec((tm, LANES), row),
                  pl.BlockSpec((tm, LANES), row)],
        out_specs=[pl.BlockSpec((tm, LANES), row), pl.BlockSpec((n_blocks_pad, LANES), fixed)],
        out_shape=[jax.ShapeDtypeStruct((t, LANES), I32),
                   jax.ShapeDtypeStruct((n_blocks_pad, LANES), I32)],
        compiler_params=pltpu.CompilerParams(dimension_semantics=("arbitrary",)),
        name="dispatch_plan",
    )(counts, idx, rank)


def _row_copy(src_ref, src_row, dst_ref, dst_row, sem):
    return pltpu.make_async_copy(src_ref.at[pl.ds(src_row, 1), :], dst_ref.at[pl.ds(dst_row, 1), :], sem)


def _dispatch_body(dest_ref, h_ref, xs_in_ref, xs_ref, sem):
    del xs_in_ref
    tm = h_ref.shape[0]

    def issue(t, carry):
        for k in range(TOP_K):
            _row_copy(h_ref, t, xs_ref, dest_ref[0, 0, t * TOP_K + k], sem).start(priority=k % 2)
        return carry

    lax.fori_loop(0, tm, issue, 0, unroll=4)

    def drain(t, carry):
        for _ in range(TOP_K):
            _row_copy(h_ref, 0, xs_ref, 0, sem).wait()
        return carry

    lax.fori_loop(0, tm, drain, 0, unroll=8)


def _dispatch(dest3, h2, xs_init):
    t = h2.shape[0]
    tm = dest3.shape[2] // TOP_K
    return pl.pallas_call(
        _dispatch_body,
        grid=(t // tm,),
        in_specs=[
            pl.BlockSpec((1, 1, TOP_K * tm), lambda i: (i, 0, 0), memory_space=pltpu.SMEM),
            pl.BlockSpec((tm, D_MODEL), lambda i: (i, 0)),
            pl.BlockSpec(memory_space=pl.ANY),
        ],
        out_specs=pl.BlockSpec(memory_space=pl.ANY),
        out_shape=jax.ShapeDtypeStruct(xs_init.shape, xs_init.dtype),
        scratch_shapes=[pltpu.SemaphoreType.DMA(())],
        input_output_aliases={2: 0},
        compiler_params=pltpu.CompilerParams(dimension_semantics=("arbitrary",)),
        name="moe_dispatch",
    )(dest3, h2, xs_init)


def _expert_body(be_ref, nu_ref, x_ref, wgu_ref, bgu_ref, wd_ref, bd_ref, y_ref, wgu_b, wd_b):
    j = pl.program_id(0)

    @pl.when(j < nu_ref[0])
    def _():
        @pl.when(jnp.logical_or(j == 0, be_ref[j] != be_ref[jnp.maximum(j - 1, 0)]))
        def _():
            wgu_b[...] = wgu_ref[0].astype(BF16)
            wd_b[...] = wd_ref[0].astype(BF16)

        gu = _dot(x_ref[...].astype(BF16), wgu_b[...]) + bgu_ref[0]
        gate = jnp.minimum(gu[:, :D_FF], SWIGLU_LIMIT)
        up = jnp.clip(gu[:, D_FF:], -SWIGLU_LIMIT, SWIGLU_LIMIT)
        act = (up + 1.0) * gate * _sigmoid(SWIGLU_ALPHA * gate)
        y_ref[...] = _dot(act.astype(BF16), wd_b[...]) + bd_ref[0]


def _experts(block_e, n_used, xs, wgu, bgu, wd, bd):
    n_slots = xs.shape[0]
    nb = n_slots // EXPERT_BLOCK

    def blk(j, be, nu):
        return jnp.minimum(j, nu[0] - 1)

    grid_spec = pltpu.PrefetchScalarGridSpec(
        num_scalar_prefetch=2,
        grid=(nb,),
        in_specs=[
            pl.BlockSpec((EXPERT_BLOCK, D_MODEL), lambda j, be, nu: (blk(j, be, nu), 0)),
            pl.BlockSpec((1, D_MODEL, 2 * D_FF), lambda j, be, nu: (be[blk(j, be, nu)], 0, 0)),
            pl.BlockSpec((1, 1, 2 * D_FF), lambda j, be, nu: (be[blk(j, be, nu)], 0, 0)),
            pl.BlockSpec((1, D_FF, D_MODEL), lambda j, be, nu: (be[blk(j, be, nu)], 0, 0)),
            pl.BlockSpec((1, 1, D_MODEL), lambda j, be, nu: (be[blk(j, be, nu)], 0, 0)),
        ],
        out_specs=pl.BlockSpec((EXPERT_BLOCK, D_MODEL), lambda j, be, nu: (blk(j, be, nu), 0)),
        scratch_shapes=[pltpu.VMEM((D_MODEL, 2 * D_FF), BF16), pltpu.VMEM((D_FF, D_MODEL), BF16)],
    )
    return pl.pallas_call(
        _expert_body,
        grid_spec=grid_spec,
        out_shape=jax.ShapeDtypeStruct((n_slots, D_MODEL), F32),
        compiler_params=pltpu.CompilerParams(dimension_semantics=("arbitrary",)),
        name="moe_experts",
    )(block_e, n_used, xs, wgu, bgu, wd, bd)


def _combine_body(dest_ref, x_ref, wt_ref, y_ref, out_ref, ybuf, sem):
    tm = x_ref.shape[0]

    def issue(t, carry):
        for k in range(TOP_K):
            _row_copy(y_ref, dest_ref[0, 0, t * TOP_K + k], ybuf, k * tm + t, sem).start(priority=k % 2)
        return carry

    lax.fori_loop(0, tm, issue, 0, unroll=4)

    def drain(t, carry):
        for _ in range(TOP_K):
            _row_copy(y_ref, 0, ybuf, 0, sem).wait()
        return carry

    lax.fori_loop(0, tm, drain, 0, unroll=8)

    acc = x_ref[...]
    wt = wt_ref[...]
    for k in range(TOP_K):
        acc = acc + wt[:, k:k + 1] * ybuf[k * tm:(k + 1) * tm, :]
    out_ref[...] = acc


def _combine(dest3, x1, wts, y):
    t = x1.shape[0]
    tm = dest3.shape[2] // TOP_K
    return pl.pallas_call(
        _combine_body,
        grid=(t // tm,),
        in_specs=[
            pl.BlockSpec((1, 1, TOP_K * tm), lambda i: (i, 0, 0), memory_space=pltpu.SMEM),
            pl.BlockSpec((tm, D_MODEL), lambda i: (i, 0)),
            pl.BlockSpec((tm, LANES), lambda i: (i, 0)),
            pl.BlockSpec(memory_space=pl.ANY),
        ],
        out_specs=pl.BlockSpec((tm, D_MODEL), lambda i: (i, 0)),
        out_shape=jax.ShapeDtypeStruct((t, D_MODEL), F32),
        scratch_shapes=[pltpu.VMEM((TOP_K * tm, D_MODEL), F32), pltpu.SemaphoreType.DMA(())],
        compiler_params=pltpu.CompilerParams(dimension_semantics=("arbitrary",)),
        name="moe_combine",
    )(dest3, x1, wts, y)


def _pad_lanes(v, fill=0.0):
    v = v.astype(F32).reshape(1, -1)
    return jnp.pad(v, ((0, 0), (0, LANES - v.shape[1])), constant_values=fill)


def _mixer(x2, mem, rel_table, attn_norm, w_in, b_gate, dn_conv, dn_a_log, dn_dt_bias, dn_out_norm,
           da_q_norm, da_k_norm, da_lambda, da_subln, mem_norm, w_mem_kv, mx_q_norm, mx_k_norm,
           w_branch, w_out, batch, seq):
    wp = jnp.concatenate([w_in[:, :W_AB_LO], w_in[:, W_AB_HI:]], axis=1).astype(BF16)
    wab = jnp.pad(w_in[:, W_AB_LO:W_AB_HI], ((0, 0), (0, LANES - (W_AB_HI - W_AB_LO))))
    p, ab = _inproj(x2, attn_norm.reshape(1, -1), wp, wab)

    o_dn = _deltanet(p, ab, dn_conv, _pad_lanes(dn_a_log), _pad_lanes(dn_dt_bias),
                     dn_out_norm.reshape(1, -1), batch, seq)

    tq = min(ATT_BLOCK, seq)
    bias = _bias_tiles(rel_table.T, tq)
    o_da = _attention(p, bias, jnp.tile(da_q_norm, 2).reshape(1, -1), jnp.tile(da_k_norm, 2).reshape(1, -1),
                      da_lambda, da_subln.reshape(-1, 1), batch, seq)

    mk, mv = _memkv(mem, mem_norm.reshape(1, -1), w_mem_kv.astype(BF16), mx_k_norm.reshape(1, -1))
    return _merge(x2, o_dn, o_da, p, mk, mv, mx_q_norm.reshape(1, -1), b_gate.reshape(3, D_MODEL),
                  w_branch.astype(BF16), w_out.astype(BF16), seq)


def _moe(x1, ffn_norm, w_router, b_router, w_gate_up, b_gate_up, w_down, b_down):
    t = x1.shape[0]
    m = t * TOP_K
    n_blocks = -(-m // EXPERT_BLOCK) + N_EXPERTS
    n_blocks_pad = -(-n_blocks // 8) * 8
    n_slots = n_blocks * EXPERT_BLOCK

    wr = jnp.pad(w_router, ((0, 0), (0, LANES - N_EXPERTS)))
    h2, idx, wts, rank, counts = _router(x1, ffn_norm.reshape(1, -1), wr, _pad_lanes(b_router, NEG))
    dest, meta = _plan(counts, idx, rank, n_blocks_pad)
    block_e = meta[:n_blocks, 0]
    n_used = meta[0:1, 1]

    tm_d = min(512, t)
    dest_d = dest[:, :TOP_K].reshape(t // tm_d, 1, TOP_K * tm_d)
    xs = _dispatch(dest_d, h2, jnp.zeros((n_slots, D_MODEL), F32))
    y = _experts(block_e, n_used, xs, w_gate_up, b_gate_up.reshape(N_EXPERTS, 1, -1),
                 w_down, b_down.reshape(N_EXPERTS, 1, -1))
    tm_c = min(256, t)
    dest_c = dest[:, :TOP_K].reshape(t // tm_c, 1, TOP_K * tm_c)
    return _combine(dest_c, x1, wts, y)


def kernel(x, mem, rel_table, attn_norm, w_in, b_gate, dn_conv, dn_a_log, dn_dt_bias, dn_out_norm,
           da_q_norm, da_k_norm, da_lambda, da_subln, mem_norm, w_mem_kv, mx_q_norm, mx_k_norm,
           w_branch, w_out, ffn_norm, w_router, b_router, w_gate_up, b_gate_up, w_down, b_down):
    batch, seq, d = x.shape
    x2 = x.reshape(batch * seq, d)
    x1 = _mixer(x2, mem, rel_table, attn_norm[0], w_in[0], b_gate[0], dn_conv[0], dn_a_log[0],
                dn_dt_bias[0], dn_out_norm[0], da_q_norm[0], da_k_norm[0], da_lambda[0], da_subln[0],
                mem_norm[0], w_mem_kv[0], mx_q_norm[0], mx_k_norm[0], w_branch[0], w_out[0], batch, seq)
    out = _moe(x1, ffn_norm[0], w_router[0], b_router[0], w_gate_up[0], b_gate_up[0], w_down[0],
               b_down[0])
    return out.reshape(batch, seq, d)
```

```python
import functools
import math

import jax
import jax.numpy as jnp
from jax import lax
from jax.experimental import pallas as pl
from jax.experimental.pallas import tpu as pltpu

F32 = jnp.float32
BF16 = jnp.bfloat16
I32 = jnp.int32

D_MODEL = 1024
EPS = 1e-6
LANES = 128

DN_HEADS = 4
DN_DK = 128
DN_CHUNK = 64
DN_CONV = 4

DA_HEADS = 4
DA_DH = 64

MX_HEADS = 4
MX_DH = 128

REL_BUCKETS = 32
REL_MAX_DIST = 128

N_EXPERTS = 32
TOP_K = 4
D_FF = 1024
SWIGLU_LIMIT = 7.0
SWIGLU_ALPHA = 1.702
EXPERT_BLOCK = 512
MOE_TILE = 512
GROUP_ALIGN = 8

LAM_INIT = 0.8 - 0.6 * math.exp(-0.3 * 0)
LOG2E = 1.4426950408889634
NEG = -1e30

P_DNQ, P_DNK, P_DNV, P_DNZ = 0, 512, 1024, 1536
P_DAQ, P_DAK, P_DAV = 2048, 2560, 3072
P_MXQ = 3584
P_GATE = 4096
P_COLS = 7168
W_AB_LO, W_AB_HI = 2048, 2056


def _dot(a, b):
    return jnp.dot(a, b, preferred_element_type=F32)


def _dot_nt(a, b):
    return lax.dot_general(a, b, (((1,), (1,)), ((), ())), preferred_element_type=F32)


def _dot_tn(a, b):
    return lax.dot_general(a, b, (((0,), (0,)), ((), ())), preferred_element_type=F32)


def _split(x):
    hi = x.astype(BF16)
    lo = (x - hi.astype(F32)).astype(BF16)
    return hi, lo


def _dot3(a, b):
    ah, al = _split(a)
    bh, bl = _split(b)
    return _dot(ah, bh) + _dot(ah, bl) + _dot(al, bh)


def _sigmoid(x):
    return 1.0 / (1.0 + jnp.exp(-x))


def _rms(x, n):
    return lax.rsqrt(jnp.sum(x * x, axis=-1, keepdims=True) * (1.0 / n) + EPS)


def _inproj_body(x_ref, g_ref, w_ref, wab_ref, p_ref, ab_ref, h_scr):
    @pl.when(pl.program_id(1) == 0)
    def _():
        x = x_ref[...]
        h = x * _rms(x, D_MODEL) * g_ref[...]
        h_scr[...] = h.astype(BF16)
        ab_ref[...] = _dot3(h, wab_ref[...])

    p_ref[...] = _dot(h_scr[...], w_ref[...]).astype(p_ref.dtype)


def _inproj(x2, gain, wp, wab):
    t = x2.shape[0]
    tm = min(1024, t)
    tn = 1024
    return pl.pallas_call(
        _inproj_body,
        grid=(t // tm, P_COLS // tn),
        in_specs=[
            pl.BlockSpec((tm, D_MODEL), lambda i, j: (i, 0)),
            pl.BlockSpec((1, D_MODEL), lambda i, j: (0, 0)),
            pl.BlockSpec((D_MODEL, tn), lambda i, j: (0, j)),
            pl.BlockSpec((D_MODEL, LANES), lambda i, j: (0, 0)),
        ],
        out_specs=[
            pl.BlockSpec((tm, tn), lambda i, j: (i, j)),
            pl.BlockSpec((tm, LANES), lambda i, j: (i, 0)),
        ],
        out_shape=[
            jax.ShapeDtypeStruct((t, P_COLS), BF16),
            jax.ShapeDtypeStruct((t, LANES), F32),
        ],
        scratch_shapes=[pltpu.VMEM((tm, D_MODEL), BF16)],
        compiler_params=pltpu.CompilerParams(dimension_semantics=("parallel", "arbitrary")),
        name="inproj",
    )(x2, gain, wp, wab)


DN_HALO = 16
DN_SCAN_CHUNK = 256


def _deltanet_body(q_ref, k_ref, v_ref, z_ref, qh_ref, kh_ref, vh_ref, ab_ref, cw_ref, alog_ref,
                   dtb_ref, on_ref, o_ref, stage, qs, ks, vs, s_scr):
    i = pl.program_id(1)
    tc = q_ref.shape[0]
    hw = DN_HEADS * DN_DK

    @pl.when(i == 0)
    def _():
        s_scr[...] = jnp.zeros_like(s_scr)

    for src, halo, dst, off, kind in ((q_ref, qh_ref, qs, 0, "q"), (k_ref, kh_ref, ks, hw, "k"),
                                      (v_ref, vh_ref, vs, 2 * hw, "v")):
        hal = halo[...].astype(F32)
        stage[0:DN_HALO, :] = jnp.where(i == 0, 0.0, hal)
        stage[DN_HALO:DN_HALO + tc, :] = src[...].astype(F32)
        base = DN_HALO - (DN_CONV - 1)
        y = stage[base:base + tc, :] * cw_ref[0:1, off:off + hw]
        for j in range(1, DN_CONV):
            y = y + stage[base + j:base + j + tc, :] * cw_ref[j:j + 1, off:off + hw]
        y = y * _sigmoid(y)
        if kind == "v":
            dst[...] = y
        else:
            for h in range(DN_HEADS):
                sl = slice(h * DN_DK, (h + 1) * DN_DK)
                yh = y[:, sl]
                r = lax.rsqrt(jnp.sum(yh * yh, axis=-1, keepdims=True) + EPS)
                if kind == "q":
                    r = r * (DN_DK ** -0.5)
                dst[:, sl] = yh * r

    c = min(DN_SCAN_CHUNK, tc)
    row = lax.broadcasted_iota(I32, (c, c), 0)
    col = lax.broadcasted_iota(I32, (c, c), 1)
    incl = row >= col
    strict = row > col
    same_blk = (row // DN_CHUNK) == (col // DN_CHUNK)
    tri = jnp.where(incl, 1.0, 0.0).astype(BF16)
    eye = jnp.where(row == col, 1.0, 0.0)
    neg_a = -jnp.exp(alog_ref[...])
    dtb = dtb_ref[...]

    def chunk(ci, carry):
        r0 = pl.multiple_of(ci * c, c)
        abc = ab_ref[pl.ds(r0, c), :]
        a_in = abc + dtb
        g_all = neg_a * (jnp.maximum(a_in, 0.0) + jnp.log(1.0 + jnp.exp(-jnp.abs(a_in))))
        beta_all = _sigmoid(abc)
        zc = z_ref[pl.ds(r0, c), :].astype(F32)
        for h in range(DN_HEADS):
            sl = slice(h * DN_DK, (h + 1) * DN_DK)
            q = qs[pl.ds(r0, c), sl]
            k = ks[pl.ds(r0, c), sl]
            v = vs[pl.ds(r0, c), sl]
            g = g_all[:, h:h + 1]
            beta = beta_all[:, DN_HEADS + h:DN_HEADS + h + 1]
            g_hi, g_lo = _split(jnp.where(strict, g, 0.0))
            diff = _dot(tri, g_hi) + _dot(tri, g_lo)
            gc = diff[:, 0:1] + g[0:1, :]
            decay = jnp.where(incl, jnp.exp(diff), 0.0)
            kb = k.astype(BF16)
            qkk = _dot_nt(jnp.concatenate([q.astype(BF16), kb], axis=0), kb)
            qk = qkk[:c]
            kk = qkk[c:]
            lower = jnp.where(strict, kk * decay * beta, 0.0)
            pw = jnp.where(same_blk, -lower, 0.0)
            dinv = eye + pw
            for _ in range(int(math.log2(DN_CHUNK)) - 1):
                pwb = pw.astype(BF16)
                pw = _dot(pwb, pwb)
                dinv = dinv + _dot(dinv.astype(BF16), pw.astype(BF16))
            dinv_b = dinv.astype(BF16)
            pw = -_dot(dinv_b, jnp.where(same_blk, 0.0, lower).astype(BF16))
            xm = eye + pw
            for _ in range(int(math.log2(c // DN_CHUNK)) - 1):
                pwb = pw.astype(BF16)
                pw = _dot(pwb, pwb)
                xm = xm + _dot(xm.astype(BF16), pw.astype(BF16))
            inv = _dot(xm.astype(BF16), dinv_b)
            egc = jnp.exp(gc)
            rhs = jnp.concatenate([v * beta, k * (beta * egc)], axis=1)
            sol = _dot(inv.astype(BF16), rhs.astype(BF16))
            u = sol[:, :DN_DK]
            w = sol[:, DN_DK:]
            qkm = jnp.where(incl, qk * decay, 0.0)
            gl = gc[c - 1:c, :]
            state = s_scr[h]
            sb = state.astype(BF16)
            ws = _dot(jnp.concatenate([w.astype(BF16), (q * egc).astype(BF16)], axis=0), sb)
            v_new = u - ws[:c]
            o = ws[c:] + _dot(qkm.astype(BF16), v_new.astype(BF16))
            s_scr[h] = state * jnp.exp(gl) + _dot_tn(kb, (v_new * jnp.exp(gl - gc)).astype(BF16))
            zz = zc[:, sl]
            on = o * _rms(o, DN_DK) * on_ref[...]
            o_ref[pl.ds(r0, c), sl] = (on * (zz * _sigmoid(zz))).astype(o_ref.dtype)
        return carry

    lax.fori_loop(0, tc // c, chunk, 0, unroll=True)


def _deltanet(p, ab, conv_w, alog_row, dtb_row, out_norm, batch, seq):
    t = batch * seq
    tc = min(512, seq)
    nt = seq // tc
    hw = DN_HEADS * DN_DK

    def main(cb):
        return pl.BlockSpec((tc, hw), lambda b, i: (b * nt + i, cb))

    def halo(cb):
        return pl.BlockSpec(
            (DN_HALO, hw),
            lambda b, i: (jnp.maximum((b * seq + i * tc) // DN_HALO - 1, 0), cb))

    return pl.pallas_call(
        _deltanet_body,
        grid=(batch, nt),
        in_specs=[
            main(P_DNQ // hw), main(P_DNK // hw), main(P_DNV // hw), main(P_DNZ // hw),
            halo(P_DNQ // hw), halo(P_DNK // hw), halo(P_DNV // hw),
            pl.BlockSpec((tc, LANES), lambda b, i: (b * nt + i, 0)),
            pl.BlockSpec((DN_CONV, 3 * hw), lambda b, i: (0, 0)),
            pl.BlockSpec((1, LANES), lambda b, i: (0, 0)),
            pl.BlockSpec((1, LANES), lambda b, i: (0, 0)),
            pl.BlockSpec((1, DN_DK), lambda b, i: (0, 0)),
        ],
        out_specs=pl.BlockSpec((tc, hw), lambda b, i: (b * nt + i, 0)),
        out_shape=jax.ShapeDtypeStruct((t, hw), BF16),
        scratch_shapes=[
            pltpu.VMEM((DN_HALO + tc, hw), F32),
            pltpu.VMEM((tc, hw), F32),
            pltpu.VMEM((tc, hw), F32),
            pltpu.VMEM((tc, hw), F32),
            pltpu.VMEM((DN_HEADS, DN_DK, DN_DK), F32),
        ],
        compiler_params=pltpu.CompilerParams(dimension_semantics=("parallel", "arbitrary")),
        name="deltanet",
    )(p, p, p, p, p, p, p, ab, conv_w, alog_row, dtb_row, out_norm)


ATT_BLOCK = 512


def _bias_body(tbl_ref, o_ref):
    h = pl.program_id(0)
    tq = o_ref.shape[2]
    key = lax.broadcasted_iota(I32, (tq, tq), 0)
    qry = lax.broadcasted_iota(I32, (tq, tq), 1)
    max_exact = REL_BUCKETS // 2
    far = tbl_ref[h, REL_BUCKETS - 1]
    for d in range(2):
        n = qry - key + d * tq
        nn = jnp.maximum(n, 0)
        nf = jnp.maximum(nn, 1).astype(F32)
        large = max_exact + (jnp.log(nf / max_exact) / math.log(REL_MAX_DIST / max_exact)
                             * (REL_BUCKETS - max_exact)).astype(I32)
        large = jnp.minimum(large, REL_BUCKETS - 1)
        bucket = jnp.where(nn < max_exact, nn, large)
        val = jnp.zeros((tq, tq), F32)
        for b in range(REL_BUCKETS):
            val = jnp.where(bucket == b, tbl_ref[h, b], val)
        o_ref[0, d] = jnp.where(n >= 0, (val - far) * LOG2E, NEG)


def _bias_tiles(tbl_t, tq):
    return pl.pallas_call(
        _bias_body,
        grid=(DA_HEADS,),
        in_specs=[pl.BlockSpec(memory_space=pltpu.SMEM)],
        out_specs=pl.BlockSpec((1, 2, tq, tq), lambda h: (h, 0, 0, 0)),
        out_shape=jax.ShapeDtypeStruct((DA_HEADS, 2, tq, tq), F32),
        name="t5_bias_tiles",
    )(tbl_t)


DA_DV = 2 * DA_DH
DA_VROWS = DA_DV + 16


BOUND_SLACK = 1.02
MAX_SHIFT_GAP = 110.0


def _attn_body(q_ref, k_ref, v_ref, bias_ref, qg_ref, kg_ref, lam_ref, sg_ref, o_ref,
               kn, vt, kst, m_s, acc_s):
    qi = pl.program_id(2)
    tq = q_ref.shape[0]
    seq = k_ref.shape[0]
    tk = tq
    lo_mask = lax.broadcasted_iota(I32, (1, DA_DV), 1) < DA_DH

    def group_norm(x, gain):
        x2 = x * x
        lo = jnp.sum(jnp.where(lo_mask, x2, 0.0), axis=-1, keepdims=True)
        hi = jnp.sum(jnp.where(lo_mask, 0.0, x2), axis=-1, keepdims=True)
        r = jnp.where(lo_mask, lax.rsqrt(lo * (1.0 / DA_DH) + EPS), lax.rsqrt(hi * (1.0 / DA_DH) + EPS))
        return x * r * gain

    @pl.when(qi == 0)
    def _():
        ones = jnp.ones((DA_VROWS - DA_DV, tk), BF16)

        def body(c, kmax2):
            r0 = pl.multiple_of(c * tk, tk)
            kb = group_norm(k_ref[pl.ds(r0, tk), :].astype(F32), kg_ref[...]).astype(BF16)
            kn[pl.ds(r0, tk), :] = kb
            vt[c, 0:DA_DV, :] = v_ref[pl.ds(r0, tk), :].astype(F32).T.astype(BF16)
            vt[c, DA_DV:DA_VROWS, :] = ones
            k2 = kb.astype(F32)
            k2 = k2 * k2
            lo = jnp.max(jnp.sum(jnp.where(lo_mask, k2, 0.0), axis=-1, keepdims=True), axis=0, keepdims=True)
            hi = jnp.max(jnp.sum(jnp.where(lo_mask, 0.0, k2), axis=-1, keepdims=True), axis=0, keepdims=True)
            return jnp.maximum(kmax2, jnp.where(lo_mask, lo, hi))
        kst[0:1, :] = lax.fori_loop(0, seq // tk, body, jnp.zeros((1, DA_DV), F32))
        b0 = bias_ref[0, 0]
        b1 = bias_ref[0, 1]
        bmax = jnp.maximum(jnp.max(jnp.maximum(b0, b1), axis=0, keepdims=True), 0.0)
        bmin = jnp.minimum(jnp.min(jnp.minimum(jnp.where(b0 > 0.5 * NEG, b0, 0.0), b1), axis=0, keepdims=True), 0.0)
        kst[1:2, :] = jnp.broadcast_to(jnp.max(bmax, axis=1, keepdims=True), (1, DA_DV))
        kst[2:3, :] = jnp.broadcast_to(jnp.min(bmin, axis=1, keepdims=True), (1, DA_DV))

    q = group_norm(q_ref[...].astype(F32), qg_ref[...]) * (DA_DH ** -0.5 * LOG2E)
    qcat = jnp.concatenate([jnp.where(lo_mask, q, 0.0), jnp.where(lo_mask, 0.0, q)], axis=0).astype(BF16)
    acc_s[...] = jnp.zeros_like(acc_s)

    q2 = q * q * kst[0:1, :]
    ones8 = jnp.ones((8, DA_DV), BF16)
    bmax = kst[1:2, 0:1]
    bmin = kst[2:3, 0:1]
    bound = []
    for m in range(2):
        q2m = jnp.where(lo_mask, q2, 0.0) if m == 0 else jnp.where(lo_mask, 0.0, q2)
        bound.append(jnp.sqrt(_dot_nt(ones8, q2m.astype(BF16))[0:1, :]) * BOUND_SLACK)
    bound = jnp.concatenate(bound, axis=1)
    worst = jnp.max(2.0 * bound, axis=1, keepdims=True) + bmax - bmin
    safe = worst[0, 0] <= MAX_SHIFT_GAP

    def block(j, d, fixed_shift):
        r0 = pl.multiple_of(j * tk, tk)
        st = _dot_nt(kn[pl.ds(r0, tk), :], qcat)
        if d is not None:
            bias = bias_ref[0, d]
            st = st + jnp.concatenate([bias, bias], axis=1)
        if fixed_shift:
            acc_s[...] = acc_s[...] + _dot(vt[j], jnp.exp2(st - m_s[...]).astype(BF16))
        else:
            m_prev = m_s[...]
            m_new = jnp.maximum(m_prev, jnp.max(st, axis=0, keepdims=True))
            alpha = jnp.exp2(m_prev - m_new)
            acc_s[...] = alpha * acc_s[...] + _dot(vt[j], jnp.exp2(st - m_new).astype(BF16))
            m_s[...] = m_new

    def run(fixed_shift):
        n_far = jnp.maximum(qi - 1, 0)

        def far_pair(jj, carry):
            block(2 * jj, None, fixed_shift)
            block(2 * jj + 1, None, fixed_shift)
            return carry

        lax.fori_loop(0, n_far // 2, far_pair, 0)

        @pl.when(n_far % 2 == 1)
        def _():
            block(n_far - 1, None, fixed_shift)

        @pl.when(qi >= 1)
        def _():
            block(qi - 1, 1, fixed_shift)

        block(qi, 0, fixed_shift)

    @pl.when(safe)
    def _():
        m_s[...] = bound + bmax
        run(True)

    @pl.when(jnp.logical_not(safe))
    def _():
        m_s[...] = jnp.full(m_s.shape, NEG, F32)
        run(False)

    lam_p = lam_ref[...]
    lam = (jnp.exp(jnp.sum(lam_p[0:1, :] * lam_p[1:2, :], axis=-1, keepdims=True))
           - jnp.exp(jnp.sum(lam_p[2:3, :] * lam_p[3:4, :], axis=-1, keepdims=True)) + LAM_INIT)
    a0 = acc_s[:, 0:tq]
    a1 = acc_s[:, tq:2 * tq]
    ot = a0[0:DA_DV] / a0[DA_DV:DA_DV + 1] - lam * (a1[0:DA_DV] / a1[DA_DV:DA_DV + 1])
    r = lax.rsqrt(jnp.sum(ot * ot, axis=0, keepdims=True) * (1.0 / DA_DV) + EPS)
    ot = ot * r * (sg_ref[...] * (1.0 - LAM_INIT))
    o_ref[...] = ot.T.astype(o_ref.dtype)


def _attention(p, bias, qg, kg, lam_p, subln, batch, seq):
    t = batch * seq
    tq = min(ATT_BLOCK, seq)
    nq = seq // tq
    dv = DA_DV
    return pl.pallas_call(
        _attn_body,
        grid=(batch, DA_HEADS, nq),
        in_specs=[
            pl.BlockSpec((tq, dv), lambda b, h, i: (b * nq + i, P_DAQ // dv + h)),
            pl.BlockSpec((seq, dv), lambda b, h, i: (b, P_DAK // dv + h)),
            pl.BlockSpec((seq, dv), lambda b, h, i: (b, P_DAV // dv + h)),
            pl.BlockSpec((1, 2, tq, tq), lambda b, h, i: (h, 0, 0, 0)),
            pl.BlockSpec((1, dv), lambda b, h, i: (0, 0)),
            pl.BlockSpec((1, dv), lambda b, h, i: (0, 0)),
            pl.BlockSpec((4, DA_DH), lambda b, h, i: (0, 0)),
            pl.BlockSpec((dv, 1), lambda b, h, i: (0, 0)),
        ],
        out_specs=pl.BlockSpec((tq, dv), lambda b, h, i: (b * nq + i, h)),
        out_shape=jax.ShapeDtypeStruct((t, DA_HEADS * dv), BF16),
        scratch_shapes=[
            pltpu.VMEM((seq, dv), BF16),
            pltpu.VMEM((seq // tq, DA_VROWS, tq), BF16),
            pltpu.VMEM((8, dv), F32),
            pltpu.VMEM((1, 2 * tq), F32),
            pltpu.VMEM((DA_VROWS, 2 * tq), F32),
        ],
        compiler_params=pltpu.CompilerParams(dimension_semantics=("parallel", "parallel", "arbitrary")),
        name="diff_attention",
    )(p, p, p, bias, qg, kg, lam_p, subln)


def _memkv_body(mem_ref, mg_ref, w_ref, kg_ref, mk_ref, mv_ref):
    x = mem_ref[0]
    xn = x * _rms(x, D_MODEL) * mg_ref[...]
    kv = _dot(xn.astype(BF16), w_ref[...])
    hw = MX_HEADS * MX_DH
    for h in range(MX_HEADS):
        sl = slice(h * MX_DH, (h + 1) * MX_DH)
        kh = kv[:, sl]
        mk_ref[0, :, sl] = (kh * _rms(kh, MX_DH) * kg_ref[...]).astype(BF16)
    mv_ref[0] = kv[:, hw:].astype(BF16)


def _memkv(mem, mem_norm, w_kv, k_norm):
    b, n, _ = mem.shape
    hw = MX_HEADS * MX_DH
    return pl.pallas_call(
        _memkv_body,
        grid=(b,),
        in_specs=[
            pl.BlockSpec((1, n, D_MODEL), lambda i: (i, 0, 0)),
            pl.BlockSpec((1, D_MODEL), lambda i: (0, 0)),
            pl.BlockSpec((D_MODEL, 2 * hw), lambda i: (0, 0)),
            pl.BlockSpec((1, MX_DH), lambda i: (0, 0)),
        ],
        out_specs=[pl.BlockSpec((1, n, hw), lambda i: (i, 0, 0))] * 2,
        out_shape=[jax.ShapeDtypeStruct((b, n, hw), BF16)] * 2,
        name="memory_kv",
    )(mem, mem_norm, w_kv, k_norm)


def _merge_body(x_ref, odn_ref, oda_ref, mxq_ref, g0_ref, g1_ref, g2_ref, mk_ref, mv_ref, qg_ref,
                bg_ref, wb_ref, wo_ref, out_ref, omx):
    for h in range(MX_HEADS):
        sl = slice(h * MX_DH, (h + 1) * MX_DH)
        qh = mxq_ref[:, sl].astype(F32)
        qh = qh * _rms(qh, MX_DH) * qg_ref[...] * (MX_DH ** -0.5 * LOG2E)
        s = _dot_nt(qh.astype(BF16), mk_ref[0, :, sl])
        p = jnp.exp2(s - jnp.max(s, axis=-1, keepdims=True))
        oh = _dot(p.astype(BF16), mv_ref[0, :, sl]) / jnp.sum(p, axis=-1, keepdims=True)
        omx[:, sl] = oh.astype(BF16)
    y = None
    for r, (o_r, g_r) in enumerate(((odn_ref, g0_ref), (oda_ref, g1_ref), (omx, g2_ref))):
        gate = _sigmoid(g_r[...].astype(F32) + bg_ref[r:r + 1, :])
        term = gate * _dot(o_r[...], wb_ref[r])
        y = term if y is None else y + term
    out_ref[...] = x_ref[...] + _dot(y.astype(BF16), wo_ref[...])


def _merge(x2, o_dn, o_da, p, mk, mv, q_norm, b_gate, w_branch, w_out, seq):
    t = x2.shape[0]
    tm = min(512, seq)
    nt = seq // tm
    bw = 512
    n_mem = mk.shape[1]
    return pl.pallas_call(
        _merge_body,
        grid=(t // tm,),
        in_specs=[
            pl.BlockSpec((tm, D_MODEL), lambda i: (i, 0)),
            pl.BlockSpec((tm, bw), lambda i: (i, 0)),
            pl.BlockSpec((tm, bw), lambda i: (i, 0)),
            pl.BlockSpec((tm, bw), lambda i: (i, P_MXQ // bw)),
            pl.BlockSpec((tm, D_MODEL), lambda i: (i, P_GATE // D_MODEL)),
            pl.BlockSpec((tm, D_MODEL), lambda i: (i, P_GATE // D_MODEL + 1)),
            pl.BlockSpec((tm, D_MODEL), lambda i: (i, P_GATE // D_MODEL + 2)),
            pl.BlockSpec((1, n_mem, bw), lambda i: (i // nt, 0, 0)),
            pl.BlockSpec((1, n_mem, bw), lambda i: (i // nt, 0, 0)),
            pl.BlockSpec((1, MX_DH), lambda i: (0, 0)),
            pl.BlockSpec((3, D_MODEL), lambda i: (0, 0)),
            pl.BlockSpec((3, bw, D_MODEL), lambda i: (0, 0, 0)),
            pl.BlockSpec((D_MODEL, D_MODEL), lambda i: (0, 0)),
        ],
        out_specs=pl.BlockSpec((tm, D_MODEL), lambda i: (i, 0)),
        out_shape=jax.ShapeDtypeStruct((t, D_MODEL), F32),
        scratch_shapes=[pltpu.VMEM((tm, bw), BF16)],
        compiler_params=pltpu.CompilerParams(dimension_semantics=("parallel",)),
        name="merge",
    )(x2, o_dn, o_da, p, p, p, p, mk, mv, q_norm, b_gate, w_branch, w_out)


def _router_body(x_ref, g_ref, wr_ref, br_ref, h_ref, idx_ref, wt_ref, rank_ref, cnt_ref):
    tm = x_ref.shape[0]
    x = x_ref[...]
    h = x * _rms(x, D_MODEL) * g_ref[...]
    h_ref[...] = h.astype(BF16)
    logits = _dot3(h, wr_ref[...]) + br_ref[...]
    lane = lax.broadcasted_iota(I32, (tm, LANES), 1)
    lane_f = lane.astype(F32)
    work = logits
    sel = jnp.zeros((tm, LANES), F32)
    vals, idxs = [], []
    for _ in range(TOP_K):
        mx = jnp.max(work, axis=-1, keepdims=True)
        ik = jnp.min(jnp.where(work == mx, lane_f, float(LANES)), axis=-1, keepdims=True)
        hit = lane_f == ik
        sel = jnp.where(hit, 1.0, sel)
        work = jnp.where(hit, -jnp.inf, work)
        vals.append(mx)
        idxs.append(ik)
    es = [jnp.exp(v - vals[0]) for v in vals]
    den = es[0] + es[1] + es[2] + es[3]
    r = lax.broadcasted_iota(I32, (tm, tm), 0)
    c = lax.broadcasted_iota(I32, (tm, tm), 1)
    tril = jnp.where(r > c, 1.0, 0.0).astype(BF16)
    cum = _dot(tril, sel.astype(BF16))
    idx_o = jnp.zeros((tm, LANES), F32)
    wt_o = jnp.zeros((tm, LANES), F32)
    rank_o = jnp.zeros((tm, LANES), F32)
    for k in range(TOP_K):
        rk = jnp.sum(jnp.where(lane_f == idxs[k], cum, 0.0), axis=-1, keepdims=True)
        idx_o = jnp.where(lane == k, idxs[k], idx_o)
        wt_o = jnp.where(lane == k, es[k] / den, wt_o)
        rank_o = jnp.where(lane == k, rk, rank_o)
    idx_ref[...] = idx_o.astype(I32)
    wt_ref[...] = wt_o
    rank_ref[...] = rank_o.astype(I32)
    cnt_ref[0] = jnp.broadcast_to(jnp.sum(sel, axis=0, keepdims=True), (8, LANES))


def _router(x1, gain, w_r, b_r):
    t = x1.shape[0]
    tm = min(MOE_TILE, t)
    row = lambda i: (i, 0)
    fixed = lambda i: (0, 0)
    return pl.pallas_call(
        _router_body,
        grid=(t // tm,),
        in_specs=[
            pl.BlockSpec((tm, D_MODEL), row),
            pl.BlockSpec((1, D_MODEL), fixed),
            pl.BlockSpec((D_MODEL, LANES), fixed),
            pl.BlockSpec((1, LANES), fixed),
        ],
        out_specs=[
            pl.BlockSpec((tm, D_MODEL), row),
            pl.BlockSpec((tm, LANES), row),
            pl.BlockSpec((tm, LANES), row),
            pl.BlockSpec((tm, LANES), row),
            pl.BlockSpec((1, 8, LANES), lambda i: (i, 0, 0)),
        ],
        out_shape=[
            jax.ShapeDtypeStruct((t, D_MODEL), BF16),
            jax.ShapeDtypeStruct((t, LANES), I32),
            jax.ShapeDtypeStruct((t, LANES), F32),
            jax.ShapeDtypeStruct((t, LANES), I32),
            jax.ShapeDtypeStruct((t // tm, 8, LANES), F32),
        ],
        compiler_params=pltpu.CompilerParams(dimension_semantics=("arbitrary",)),
        name="router",
    )(x1, gain, w_r, b_r)


def _lane_cumsum(x):
    lane = lax.broadcasted_iota(I32, x.shape, 1)
    s = 1
    while s < N_EXPERTS:
        x = x + jnp.where(lane >= s, pltpu.roll(x, s, axis=1), 0.0)
        s *= 2
    return x


def _plan_body(cnt_ref, gs_ref, ls_ref, n8_ref, lsf_ref, tail_ref, meta_ref):
    nt = cnt_ref.shape[0]
    ga = float(GROUP_ALIGN)
    eb = float(EXPERT_BLOCK)
    lane = lax.broadcasted_iota(I32, (nt, LANES), 1)
    r8 = jnp.where(lane < N_EXPERTS, jnp.floor((cnt_ref[...] + (ga - 1.0)) * (1.0 / ga)) * ga, 0.0)
    ri = lax.broadcasted_iota(I32, (nt, nt), 0)
    ci = lax.broadcasted_iota(I32, (nt, nt), 1)
    before = _dot(jnp.where(ri > ci, 1.0, 0.0).astype(BF16), r8.astype(BF16))
    tot = jnp.sum(r8, axis=0, keepdims=True)
    region = jnp.floor((tot + (eb - 1.0)) * (1.0 / eb)) * eb
    pends = _lane_cumsum(jnp.broadcast_to(region, (8, LANES)))[0:1, :]
    pstart = pends - region
    lstart = _lane_cumsum(r8) - r8
    gs_ref[...] = (pstart + before).astype(I32)
    ls_ref[...] = lstart.astype(I32)
    n8_ref[...] = (r8 * (1.0 / ga)).astype(I32)
    lsf_ref[...] = lstart
    row8 = lax.broadcasted_iota(I32, (8, LANES), 0)
    tail = jnp.where(row8 == 0, pstart + tot, jnp.where(row8 == 1, (region - tot) * (1.0 / ga), 0.0))
    tail_ref[...] = tail.astype(I32)
    nb = meta_ref.shape[0]
    ln = lax.broadcasted_iota(I32, (nb, LANES), 1)
    blk = lax.broadcasted_iota(I32, (nb, LANES), 0).astype(F32) * eb
    be = jnp.sum(jnp.where((ln < N_EXPERTS) & (pends <= blk), 1.0, 0.0), axis=-1, keepdims=True)
    be = jnp.minimum(be, float(N_EXPERTS - 1))
    used = jnp.sum(jnp.where(ln == N_EXPERTS - 1, pends, 0.0), axis=-1, keepdims=True) * (1.0 / eb)
    meta_ref[...] = jnp.where(ln == 0, be, jnp.where(ln == 1, used, 0.0)).astype(I32)


def _plan(cnt, n_blocks_pad):
    nt = cnt.shape[0]
    shp = jax.ShapeDtypeStruct((nt, LANES), I32)
    return pl.pallas_call(
        _plan_body,
        out_shape=[shp, shp, shp, jax.ShapeDtypeStruct((nt, LANES), F32),
                   jax.ShapeDtypeStruct((8, LANES), I32),
                   jax.ShapeDtypeStruct((n_blocks_pad, LANES), I32)],
        name="dispatch_plan",
    )(cnt)


def _local_positions(idx_ref, rank_ref, lsf_ref):
    tm = idx_ref.shape[0]
    lane = lax.broadcasted_iota(I32, (tm, LANES), 1)
    idx = idx_ref[...]
    rank = rank_ref[...].astype(F32)
    ls_row = lsf_ref[0]
    out = []
    for k in range(TOP_K):
        base = jnp.sum(jnp.where(lane == idx[:, k:k + 1], ls_row, 0.0), axis=-1, keepdims=True)
        out.append((base + rank[:, k:k + 1]).astype(I32))
    return out


def _group_copies(gs_ref, ls_ref, n8_ref, make_copy, start):
    def group(e, carry):
        g0 = gs_ref[0, 0, e]
        l0 = ls_ref[0, 0, e]

        def one(j, c):
            cp = make_copy(pl.multiple_of(l0 + j * GROUP_ALIGN, GROUP_ALIGN),
                           pl.multiple_of(g0 + j * GROUP_ALIGN, GROUP_ALIGN))
            cp.start() if start else cp.wait()
            return c

        lax.fori_loop(0, n8_ref[0, 0, e], one, 0)
        return carry

    lax.fori_loop(0, N_EXPERTS, group, 0)


def _dispatch_body(gs_ref, ls_ref, n8_ref, tail_ref, h_ref, idx_ref, rank_ref, lsf_ref, xs_ref,
                   xl, zbuf, sem):
    tm = h_ref.shape[0]
    lrows = xl.shape[0]
    pos = lax.broadcasted_iota(I32, (tm, lrows), 1)
    lp = _local_positions(idx_ref, rank_ref, lsf_ref)
    hit = pos == lp[0]
    for k in range(1, TOP_K):
        hit = hit | (pos == lp[k])
    xl[...] = _dot_tn(jnp.where(hit, 1.0, 0.0).astype(BF16), h_ref[...])

    def make_copy(l, g):
        return pltpu.make_async_copy(xl.at[pl.ds(l, GROUP_ALIGN), :], xs_ref.at[pl.ds(g, GROUP_ALIGN), :], sem)

    _group_copies(gs_ref, ls_ref, n8_ref, make_copy, True)
    last = pl.program_id(0) == pl.num_programs(0) - 1

    def tails(start):
        def per_expert(e, carry):
            g0 = tail_ref[0, e]

            def one(j, c):
                cp = pltpu.make_async_copy(
                    zbuf, xs_ref.at[pl.ds(pl.multiple_of(g0 + j * GROUP_ALIGN, GROUP_ALIGN), GROUP_ALIGN), :], sem)
                cp.start() if start else cp.wait()
                return c

            lax.fori_loop(0, tail_ref[1, e], one, 0)
            return carry

        lax.fori_loop(0, N_EXPERTS, per_expert, 0)

    @pl.when(last)
    def _():
        zbuf[...] = jnp.zeros_like(zbuf)
        tails(True)

    _group_copies(gs_ref, ls_ref, n8_ref, make_copy, False)

    @pl.when(last)
    def _():
        tails(False)


def _smem_row(nt):
    return pl.BlockSpec((1, 1, LANES), lambda i: (i, 0, 0), memory_space=pltpu.SMEM)


def _dispatch(gs, ls, n8, tail, h2, idx, rank, lsf, n_slots):
    t = h2.shape[0]
    tm = min(MOE_TILE, t)
    nt = t // tm
    lrows = tm * TOP_K + N_EXPERTS * GROUP_ALIGN
    row = lambda i: (i, 0)
    return pl.pallas_call(
        _dispatch_body,
        grid=(nt,),
        in_specs=[
            _smem_row(nt), _smem_row(nt), _smem_row(nt),
            pl.BlockSpec(memory_space=pltpu.SMEM),
            pl.BlockSpec((tm, D_MODEL), row),
            pl.BlockSpec((tm, LANES), row),
            pl.BlockSpec((tm, LANES), row),
            pl.BlockSpec((1, 1, LANES), lambda i: (i, 0, 0)),
        ],
        out_specs=pl.BlockSpec(memory_space=pl.ANY),
        out_shape=jax.ShapeDtypeStruct((n_slots, D_MODEL), F32),
        scratch_shapes=[pltpu.VMEM((lrows, D_MODEL), F32), pltpu.VMEM((GROUP_ALIGN, D_MODEL), F32),
                        pltpu.SemaphoreType.DMA(())],
        compiler_params=pltpu.CompilerParams(dimension_semantics=("arbitrary",)),
        name="moe_dispatch",
    )(gs, ls, n8, tail, h2, idx, rank, lsf)


def _expert_body(be_ref, nu_ref, x_ref, wgu_ref, bgu_ref, wd_ref, bd_ref, y_ref, wgu_b, wd_b):
    j = pl.program_id(0)

    @pl.when(j < nu_ref[0])
    def _():
        @pl.when(jnp.logical_or(j == 0, be_ref[j] != be_ref[jnp.maximum(j - 1, 0)]))
        def _():
            wgu_b[...] = wgu_ref[0].astype(BF16)
            wd_b[...] = wd_ref[0].astype(BF16)

        gu = _dot(x_ref[...].astype(BF16), wgu_b[...]) + bgu_ref[0]
        gate = jnp.minimum(gu[:, :D_FF], SWIGLU_LIMIT)
        up = jnp.clip(gu[:, D_FF:], -SWIGLU_LIMIT, SWIGLU_LIMIT)
        act = (up + 1.0) * gate * _sigmoid(SWIGLU_ALPHA * gate)
        y_ref[...] = _dot(act.astype(BF16), wd_b[...]) + bd_ref[0]


def _experts(block_e, n_used, xs, wgu, bgu, wd, bd):
    n_slots = xs.shape[0]
    nb = n_slots // EXPERT_BLOCK

    def blk(j, be, nu):
        return jnp.minimum(j, nu[0] - 1)

    grid_spec = pltpu.PrefetchScalarGridSpec(
        num_scalar_prefetch=2,
        grid=(nb,),
        in_specs=[
            pl.BlockSpec((EXPERT_BLOCK, D_MODEL), lambda j, be, nu: (blk(j, be, nu), 0)),
            pl.BlockSpec((1, D_MODEL, 2 * D_FF), lambda j, be, nu: (be[blk(j, be, nu)], 0, 0)),
            pl.BlockSpec((1, 1, 2 * D_FF), lambda j, be, nu: (be[blk(j, be, nu)], 0, 0)),
            pl.BlockSpec((1, D_FF, D_MODEL), lambda j, be, nu: (be[blk(j, be, nu)], 0, 0)),
            pl.BlockSpec((1, 1, D_MODEL), lambda j, be, nu: (be[blk(j, be, nu)], 0, 0)),
        ],
        out_specs=pl.BlockSpec((EXPERT_BLOCK, D_MODEL), lambda j, be, nu: (blk(j, be, nu), 0)),
        scratch_shapes=[pltpu.VMEM((D_MODEL, 2 * D_FF), BF16), pltpu.VMEM((D_FF, D_MODEL), BF16)],
    )
    return pl.pallas_call(
        _expert_body,
        grid_spec=grid_spec,
        out_shape=jax.ShapeDtypeStruct((n_slots, D_MODEL), F32),
        compiler_params=pltpu.CompilerParams(dimension_semantics=("arbitrary",)),
        name="moe_experts",
    )(block_e, n_used, xs, wgu, bgu, wd, bd)


def _combine_body(gs_ref, ls_ref, n8_ref, x_ref, wt_ref, idx_ref, rank_ref, lsf_ref, y_ref, out_ref,
                  yl, sem):
    tm = x_ref.shape[0]
    lrows = yl.shape[0]
    yl[...] = jnp.zeros_like(yl)

    def make_copy(l, g):
        return pltpu.make_async_copy(y_ref.at[pl.ds(g, GROUP_ALIGN), :], yl.at[pl.ds(l, GROUP_ALIGN), :], sem)

    _group_copies(gs_ref, ls_ref, n8_ref, make_copy, True)
    pos = lax.broadcasted_iota(I32, (tm, lrows), 1)
    lp = _local_positions(idx_ref, rank_ref, lsf_ref)
    wt = wt_ref[...]
    wm = jnp.zeros((tm, lrows), F32)
    for k in range(TOP_K):
        wm = jnp.where(pos == lp[k], wt[:, k:k + 1], wm)
    _group_copies(gs_ref, ls_ref, n8_ref, make_copy, False)
    out_ref[...] = x_ref[...] + _dot(wm.astype(BF16), yl[...].astype(BF16))


def _combine(gs, ls, n8, x1, wts, idx, rank, lsf, y):
    t = x1.shape[0]
    tm = min(MOE_TILE, t)
    nt = t // tm
    lrows = tm * TOP_K + N_EXPERTS * GROUP_ALIGN
    row = lambda i: (i, 0)
    return pl.pallas_call(
        _combine_body,
        grid=(nt,),
        in_specs=[
            _smem_row(nt), _smem_row(nt), _smem_row(nt),
            pl.BlockSpec((tm, D_MODEL), row),
            pl.BlockSpec((tm, LANES), row),
            pl.BlockSpec((tm, LANES), row),
            pl.BlockSpec((tm, LANES), row),
            pl.BlockSpec((1, 1, LANES), lambda i: (i, 0, 0)),
            pl.BlockSpec(memory_space=pl.ANY),
        ],
        out_specs=pl.BlockSpec((tm, D_MODEL), row),
        out_shape=jax.ShapeDtypeStruct((t, D_MODEL), F32),
        scratch_shapes=[pltpu.VMEM((lrows, D_MODEL), F32), pltpu.SemaphoreType.DMA(())],
        compiler_params=pltpu.CompilerParams(dimension_semantics=("arbitrary",)),
        name="moe_combine",
    )(gs, ls, n8, x1, wts, idx, rank, lsf, y)


def _pad_lanes(v, fill=0.0):
    v = v.astype(F32).reshape(1, -1)
    return jnp.pad(v, ((0, 0), (0, LANES - v.shape[1])), constant_values=fill)


def _mixer(x2, mem, rel_table, attn_norm, w_in, b_gate, dn_conv, dn_a_log, dn_dt_bias, dn_out_norm,
           da_q_norm, da_k_norm, da_lambda, da_subln, mem_norm, w_mem_kv, mx_q_norm, mx_k_norm,
           w_branch, w_out, batch, seq):
    wp = jnp.concatenate([w_in[:, :W_AB_LO], w_in[:, W_AB_HI:]], axis=1).astype(BF16)
    wab = jnp.pad(w_in[:, W_AB_LO:W_AB_HI], ((0, 0), (0, LANES - (W_AB_HI - W_AB_LO))))
    p, ab = _inproj(x2, attn_norm.reshape(1, -1), wp, wab)

    o_dn = _deltanet(p, ab, dn_conv, _pad_lanes(dn_a_log), _pad_lanes(dn_dt_bias),
                     dn_out_norm.reshape(1, -1), batch, seq)

    tq = min(ATT_BLOCK, seq)
    bias = _bias_tiles(rel_table.T, tq)
    o_da = _attention(p, bias, jnp.tile(da_q_norm, 2).reshape(1, -1), jnp.tile(da_k_norm, 2).reshape(1, -1),
                      da_lambda, da_subln.reshape(-1, 1), batch, seq)

    mk, mv = _memkv(mem, mem_norm.reshape(1, -1), w_mem_kv.astype(BF16), mx_k_norm.reshape(1, -1))
    return _merge(x2, o_dn, o_da, p, mk, mv, mx_q_norm.reshape(1, -1), b_gate.reshape(3, D_MODEL),
                  w_branch.astype(BF16), w_out.astype(BF16), seq)


def _moe(x1, ffn_norm, w_router, b_router, w_gate_up, b_gate_up, w_down, b_down):
    t = x1.shape[0]
    nt = t // min(MOE_TILE, t)
    max_rows = t * TOP_K + nt * N_EXPERTS * (GROUP_ALIGN - 1)
    n_blocks = -(-max_rows // EXPERT_BLOCK) + N_EXPERTS
    n_blocks_pad = -(-n_blocks // 8) * 8
    n_slots = n_blocks * EXPERT_BLOCK

    wr = jnp.pad(w_router, ((0, 0), (0, LANES - N_EXPERTS)))
    h2, idx, wts, rank, cnt = _router(x1, ffn_norm.reshape(1, -1), wr, _pad_lanes(b_router, NEG))
    gs, ls, n8, lsf, tail, meta = _plan(cnt[:, 0, :], n_blocks_pad)
    block_e = meta[:n_blocks, 0]
    n_used = meta[0:1, 1]
    gs, ls, n8, lsf = (a.reshape(nt, 1, LANES) for a in (gs, ls, n8, lsf))

    xs = _dispatch(gs, ls, n8, tail, h2, idx, rank, lsf, n_slots)
    y = _experts(block_e, n_used, xs, w_gate_up, b_gate_up.reshape(N_EXPERTS, 1, -1),
                 w_down, b_down.reshape(N_EXPERTS, 1, -1))
    return _combine(gs, ls, n8, x1, wts, idx, rank, lsf, y)


def kernel(x, mem, rel_table, attn_norm, w_in, b_gate, dn_conv, dn_a_log, dn_dt_bias, dn_out_norm,
           da_q_norm, da_k_norm, da_lambda, da_subln, mem_norm, w_mem_kv, mx_q_norm, mx_k_norm,
           w_branch, w_out, ffn_norm, w_router, b_router, w_gate_up, b_gate_up, w_down, b_down):
    batch, seq, d = x.shape
    x2 = x.reshape(batch * seq, d)
    x1 = _mixer(x2, mem, rel_table, attn_norm[0], w_in[0], b_gate[0], dn_conv[0], dn_a_log[0],
                dn_dt_bias[0], dn_out_norm[0], da_q_norm[0], da_k_norm[0], da_lambda[0], da_subln[0],
                mem_norm[0], w_mem_kv[0], mx_q_norm[0], mx_k_norm[0], w_branch[0], w_out[0], batch, seq)
    out = _moe(x1, ffn_norm[0], w_router[0], b_router[0], w_gate_up[0], b_gate_up[0], w_down[0],
               b_down[0])
    return out.reshape(batch, seq, d)
```

```python
import functools
import math

import jax
import jax.numpy as jnp
from jax import lax
from jax.experimental import pallas as pl
from jax.experimental.pallas import tpu as pltpu

F32 = jnp.float32
BF16 = jnp.bfloat16
I32 = jnp.int32

D_MODEL = 1024
EPS = 1e-6
LANES = 128

DN_HEADS = 4
DN_DK = 128
DN_CHUNK = 64
DN_CONV = 4

DA_HEADS = 4
DA_DH = 64

MX_HEADS = 4
MX_DH = 128

REL_BUCKETS = 32
REL_MAX_DIST = 128

N_EXPERTS = 32
TOP_K = 4
D_FF = 1024
SWIGLU_LIMIT = 7.0
SWIGLU_ALPHA = 1.702
EXPERT_BLOCK = 512
MOE_TILE = 512
GROUP_ALIGN = 8

LAM_INIT = 0.8 - 0.6 * math.exp(-0.3 * 0)
LOG2E = 1.4426950408889634
NEG = -1e30

P_DNQ, P_DNK, P_DNV, P_DNZ = 0, 512, 1024, 1536
P_DAQ, P_DAK, P_DAV = 2048, 2560, 3072
P_MXQ = 3584
P_GATE = 4096
P_COLS = 7168
W_AB_LO, W_AB_HI = 2048, 2056


def _dot(a, b):
    return jnp.dot(a, b, preferred_element_type=F32)


def _dot_nt(a, b):
    return lax.dot_general(a, b, (((1,), (1,)), ((), ())), preferred_element_type=F32)


def _dot_tn(a, b):
    return lax.dot_general(a, b, (((0,), (0,)), ((), ())), preferred_element_type=F32)


def _split(x):
    hi = x.astype(BF16)
    lo = (x - hi.astype(F32)).astype(BF16)
    return hi, lo


def _dot3(a, b):
    ah, al = _split(a)
    bh, bl = _split(b)
    return _dot(ah, bh) + _dot(ah, bl) + _dot(al, bh)


def _sigmoid(x):
    return 1.0 / (1.0 + jnp.exp(-x))


def _rms(x, n):
    return lax.rsqrt(jnp.sum(x * x, axis=-1, keepdims=True) * (1.0 / n) + EPS)


def _inproj_body(x_ref, g_ref, w_ref, wab_ref, p_ref, ab_ref, h_scr):
    @pl.when(pl.program_id(1) == 0)
    def _():
        x = x_ref[...]
        h = x * _rms(x, D_MODEL) * g_ref[...]
        h_scr[...] = h.astype(BF16)
        ab_ref[...] = _dot3(h, wab_ref[...])

    p_ref[...] = _dot(h_scr[...], w_ref[...]).astype(p_ref.dtype)


def _inproj(x2, gain, wp, wab):
    t = x2.shape[0]
    tm = min(1024, t)
    tn = 1024
    return pl.pallas_call(
        _inproj_body,
        grid=(t // tm, P_COLS // tn),
        in_specs=[
            pl.BlockSpec((tm, D_MODEL), lambda i, j: (i, 0)),
            pl.BlockSpec((1, D_MODEL), lambda i, j: (0, 0)),
            pl.BlockSpec((D_MODEL, tn), lambda i, j: (0, j)),
            pl.BlockSpec((D_MODEL, LANES), lambda i, j: (0, 0)),
        ],
        out_specs=[
            pl.BlockSpec((tm, tn), lambda i, j: (i, j)),
            pl.BlockSpec((tm, LANES), lambda i, j: (i, 0)),
        ],
        out_shape=[
            jax.ShapeDtypeStruct((t, P_COLS), BF16),
            jax.ShapeDtypeStruct((t, LANES), F32),
        ],
        scratch_shapes=[pltpu.VMEM((tm, D_MODEL), BF16)],
        compiler_params=pltpu.CompilerParams(dimension_semantics=("parallel", "arbitrary")),
        name="inproj",
    )(x2, gain, wp, wab)


DN_HALO = 16
DN_SCAN_CHUNK = 256


def _deltanet_body(q_ref, k_ref, v_ref, z_ref, qh_ref, kh_ref, vh_ref, ab_ref, cw_ref, alog_ref,
                   dtb_ref, on_ref, o_ref, stage, qs, ks, vs, s_scr):
    i = pl.program_id(1)
    tc = q_ref.shape[0]
    hw = DN_HEADS * DN_DK

    @pl.when(i == 0)
    def _():
        s_scr[...] = jnp.zeros_like(s_scr)

    for src, halo, dst, off, kind in ((q_ref, qh_ref, qs, 0, "q"), (k_ref, kh_ref, ks, hw, "k"),
                                      (v_ref, vh_ref, vs, 2 * hw, "v")):
        hal = halo[...].astype(F32)
        stage[0:DN_HALO, :] = jnp.where(i == 0, 0.0, hal)
        stage[DN_HALO:DN_HALO + tc, :] = src[...].astype(F32)
        base = DN_HALO - (DN_CONV - 1)
        y = stage[base:base + tc, :] * cw_ref[0:1, off:off + hw]
        for j in range(1, DN_CONV):
            y = y + stage[base + j:base + j + tc, :] * cw_ref[j:j + 1, off:off + hw]
        y = y * _sigmoid(y)
        if kind == "v":
            dst[...] = y
        else:
            for h in range(DN_HEADS):
                sl = slice(h * DN_DK, (h + 1) * DN_DK)
                yh = y[:, sl]
                r = lax.rsqrt(jnp.sum(yh * yh, axis=-1, keepdims=True) + EPS)
                if kind == "q":
                    r = r * (DN_DK ** -0.5)
                dst[:, sl] = yh * r

    c = min(DN_SCAN_CHUNK, tc)
    row = lax.broadcasted_iota(I32, (c, c), 0)
    col = lax.broadcasted_iota(I32, (c, c), 1)
    incl = row >= col
    strict = row > col
    same_blk = (row // DN_CHUNK) == (col // DN_CHUNK)
    tri = jnp.where(incl, 1.0, 0.0).astype(BF16)
    eye = jnp.where(row == col, 1.0, 0.0)
    neg_a = -jnp.exp(alog_ref[...])
    dtb = dtb_ref[...]

    def chunk(ci, carry):
        r0 = pl.multiple_of(ci * c, c)
        abc = ab_ref[pl.ds(r0, c), :]
        a_in = abc + dtb
        g_all = neg_a * (jnp.maximum(a_in, 0.0) + jnp.log(1.0 + jnp.exp(-jnp.abs(a_in))))
        beta_all = _sigmoid(abc)
        zc = z_ref[pl.ds(r0, c), :].astype(F32)
        for h in range(DN_HEADS):
            sl = slice(h * DN_DK, (h + 1) * DN_DK)
            q = qs[pl.ds(r0, c), sl]
            k = ks[pl.ds(r0, c), sl]
            v = vs[pl.ds(r0, c), sl]
            g = g_all[:, h:h + 1]
            beta = beta_all[:, DN_HEADS + h:DN_HEADS + h + 1]
            g_hi, g_lo = _split(jnp.where(strict, g, 0.0))
            diff = _dot(tri, g_hi) + _dot(tri, g_lo)
            gc = diff[:, 0:1] + g[0:1, :]
            decay = jnp.where(incl, jnp.exp(diff), 0.0)
            kb = k.astype(BF16)
            qkk = _dot_nt(jnp.concatenate([q.astype(BF16), kb], axis=0), kb)
            qk = qkk[:c]
            kk = qkk[c:]
            lower = jnp.where(strict, kk * decay * beta, 0.0)
            pw = jnp.where(same_blk, -lower, 0.0)
            dinv = eye + pw
            for _ in range(int(math.log2(DN_CHUNK)) - 1):
                pwb = pw.astype(BF16)
                pw = _dot(pwb, pwb)
                dinv = dinv + _dot(dinv.astype(BF16), pw.astype(BF16))
            dinv_b = dinv.astype(BF16)
            pw = -_dot(dinv_b, jnp.where(same_blk, 0.0, lower).astype(BF16))
            xm = eye + pw
            for _ in range(int(math.log2(c // DN_CHUNK)) - 1):
                pwb = pw.astype(BF16)
                pw = _dot(pwb, pwb)
                xm = xm + _dot(xm.astype(BF16), pw.astype(BF16))
            inv = _dot(xm.astype(BF16), dinv_b)
            egc = jnp.exp(gc)
            rhs = jnp.concatenate([v * beta, k * (beta * egc)], axis=1)
            sol = _dot(inv.astype(BF16), rhs.astype(BF16))
            u = sol[:, :DN_DK]
            w = sol[:, DN_DK:]
            qkm = jnp.where(incl, qk * decay, 0.0)
            gl = gc[c - 1:c, :]
            state = s_scr[h]
            sb = state.astype(BF16)
            ws = _dot(jnp.concatenate([w.astype(BF16), (q * egc).astype(BF16)], axis=0), sb)
            v_new = u - ws[:c]
            o = ws[c:] + _dot(qkm.astype(BF16), v_new.astype(BF16))
            s_scr[h] = state * jnp.exp(gl) + _dot_tn(kb, (v_new * jnp.exp(gl - gc)).astype(BF16))
            zz = zc[:, sl]
            on = o * _rms(o, DN_DK) * on_ref[...]
            o_ref[pl.ds(r0, c), sl] = (on * (zz * _sigmoid(zz))).astype(o_ref.dtype)
        return carry

    lax.fori_loop(0, tc // c, chunk, 0, unroll=True)


def _deltanet(p, ab, conv_w, alog_row, dtb_row, out_norm, batch, seq):
    t = batch * seq
    tc = min(512, seq)
    nt = seq // tc
    hw = DN_HEADS * DN_DK

    def main(cb):
        return pl.BlockSpec((tc, hw), lambda b, i: (b * nt + i, cb))

    def halo(cb):
        return pl.BlockSpec(
            (DN_HALO, hw),
            lambda b, i: (jnp.maximum((b * seq + i * tc) // DN_HALO - 1, 0), cb))

    return pl.pallas_call(
        _deltanet_body,
        grid=(batch, nt),
        in_specs=[
            main(P_DNQ // hw), main(P_DNK // hw), main(P_DNV // hw), main(P_DNZ // hw),
            halo(P_DNQ // hw), halo(P_DNK // hw), halo(P_DNV // hw),
            pl.BlockSpec((tc, LANES), lambda b, i: (b * nt + i, 0)),
            pl.BlockSpec((DN_CONV, 3 * hw), lambda b, i: (0, 0)),
            pl.BlockSpec((1, LANES), lambda b, i: (0, 0)),
            pl.BlockSpec((1, LANES), lambda b, i: (0, 0)),
            pl.BlockSpec((1, DN_DK), lambda b, i: (0, 0)),
        ],
        out_specs=pl.BlockSpec((tc, hw), lambda b, i: (b * nt + i, 0)),
        out_shape=jax.ShapeDtypeStruct((t, hw), BF16),
        scratch_shapes=[
            pltpu.VMEM((DN_HALO + tc, hw), F32),
            pltpu.VMEM((tc, hw), F32),
            pltpu.VMEM((tc, hw), F32),
            pltpu.VMEM((tc, hw), F32),
            pltpu.VMEM((DN_HEADS, DN_DK, DN_DK), F32),
        ],
        compiler_params=pltpu.CompilerParams(dimension_semantics=("parallel", "arbitrary")),
        name="deltanet",
    )(p, p, p, p, p, p, p, ab, conv_w, alog_row, dtb_row, out_norm)


ATT_BLOCK = 512


def _bias_body(tbl_ref, o_ref):
    h = pl.program_id(0)
    tq = o_ref.shape[2]
    key = lax.broadcasted_iota(I32, (tq, tq), 0)
    qry = lax.broadcasted_iota(I32, (tq, tq), 1)
    max_exact = REL_BUCKETS // 2
    far = tbl_ref[h, REL_BUCKETS - 1]
    for d in range(2):
        n = qry - key + d * tq
        nn = jnp.maximum(n, 0)
        nf = jnp.maximum(nn, 1).astype(F32)
        large = max_exact + (jnp.log(nf / max_exact) / math.log(REL_MAX_DIST / max_exact)
                             * (REL_BUCKETS - max_exact)).astype(I32)
        large = jnp.minimum(large, REL_BUCKETS - 1)
        bucket = jnp.where(nn < max_exact, nn, large)
        val = jnp.zeros((tq, tq), F32)
        for b in range(REL_BUCKETS):
            val = jnp.where(bucket == b, tbl_ref[h, b], val)
        o_ref[0, d] = jnp.where(n >= 0, (val - far) * LOG2E, NEG)


def _bias_tiles(tbl_t, tq):
    return pl.pallas_call(
        _bias_body,
        grid=(DA_HEADS,),
        in_specs=[pl.BlockSpec(memory_space=pltpu.SMEM)],
        out_specs=pl.BlockSpec((1, 2, tq, tq), lambda h: (h, 0, 0, 0)),
        out_shape=jax.ShapeDtypeStruct((DA_HEADS, 2, tq, tq), F32),
        name="t5_bias_tiles",
    )(tbl_t)


DA_DV = 2 * DA_DH
DA_VROWS = DA_DV + 16


BOUND_SLACK = 1.02
MAX_SHIFT_GAP = 110.0


def _attn_body(q_ref, k_ref, v_ref, bias_ref, qg_ref, kg_ref, lam_ref, sg_ref, o_ref,
               kn, vt, kst, m_s, acc_s):
    qi = pl.program_id(2)
    tq = q_ref.shape[0]
    seq = k_ref.shape[0]
    tk = tq
    lo_mask = lax.broadcasted_iota(I32, (1, DA_DV), 1) < DA_DH

    def group_norm(x, gain):
        x2 = x * x
        lo = jnp.sum(jnp.where(lo_mask, x2, 0.0), axis=-1, keepdims=True)
        hi = jnp.sum(jnp.where(lo_mask, 0.0, x2), axis=-1, keepdims=True)
        r = jnp.where(lo_mask, lax.rsqrt(lo * (1.0 / DA_DH) + EPS), lax.rsqrt(hi * (1.0 / DA_DH) + EPS))
        return x * r * gain

    @pl.when(qi == 0)
    def _():
        ones = jnp.ones((DA_VROWS - DA_DV, tk), BF16)

        def body(c, kmax2):
            r0 = pl.multiple_of(c * tk, tk)
            kb = group_norm(k_ref[pl.ds(r0, tk), :].astype(F32), kg_ref[...]).astype(BF16)
            kn[pl.ds(r0, tk), :] = kb
            vt[c, 0:DA_DV, :] = v_ref[pl.ds(r0, tk), :].astype(F32).T.astype(BF16)
            vt[c, DA_DV:DA_VROWS, :] = ones
            k2 = kb.astype(F32)
            k2 = k2 * k2
            lo = jnp.max(jnp.sum(jnp.where(lo_mask, k2, 0.0), axis=-1, keepdims=True), axis=0, keepdims=True)
            hi = jnp.max(jnp.sum(jnp.where(lo_mask, 0.0, k2), axis=-1, keepdims=True), axis=0, keepdims=True)
            return jnp.maximum(kmax2, jnp.where(lo_mask, lo, hi))
        kst[0:1, :] = lax.fori_loop(0, seq // tk, body, jnp.zeros((1, DA_DV), F32))
        b0 = bias_ref[0, 0]
        b1 = bias_ref[0, 1]
        bmax = jnp.maximum(jnp.max(jnp.maximum(b0, b1), axis=0, keepdims=True), 0.0)
        bmin = jnp.minimum(jnp.min(jnp.minimum(jnp.where(b0 > 0.5 * NEG, b0, 0.0), b1), axis=0, keepdims=True), 0.0)
        kst[1:2, :] = jnp.broadcast_to(jnp.max(bmax, axis=1, keepdims=True), (1, DA_DV))
        kst[2:3, :] = jnp.broadcast_to(jnp.min(bmin, axis=1, keepdims=True), (1, DA_DV))

    q = group_norm(q_ref[...].astype(F32), qg_ref[...]) * (DA_DH ** -0.5 * LOG2E)
    qcat = jnp.concatenate([jnp.where(lo_mask, q, 0.0), jnp.where(lo_mask, 0.0, q)], axis=0).astype(BF16)
    acc_s[...] = jnp.zeros_like(acc_s)

    q2 = q * q * kst[0:1, :]
    ones8 = jnp.ones((8, DA_DV), BF16)
    bmax = kst[1:2, 0:1]
    bmin = kst[2:3, 0:1]
    bound = []
    for m in range(2):
        q2m = jnp.where(lo_mask, q2, 0.0) if m == 0 else jnp.where(lo_mask, 0.0, q2)
        bound.append(jnp.sqrt(_dot_nt(ones8, q2m.astype(BF16))[0:1, :]) * BOUND_SLACK)
    bound = jnp.concatenate(bound, axis=1)
    worst = jnp.max(2.0 * bound, axis=1, keepdims=True) + bmax - bmin
    safe = worst[0, 0] <= MAX_SHIFT_GAP

    def block(j, d, fixed_shift):
        r0 = pl.multiple_of(j * tk, tk)
        st = _dot_nt(kn[pl.ds(r0, tk), :], qcat)
        if d is not None:
            bias = bias_ref[0, d]
            st = st + jnp.concatenate([bias, bias], axis=1)
        if fixed_shift:
            acc_s[...] = acc_s[...] + _dot(vt[j], jnp.exp2(st - m_s[...]).astype(BF16))
        else:
            m_prev = m_s[...]
            m_new = jnp.maximum(m_prev, jnp.max(st, axis=0, keepdims=True))
            alpha = jnp.exp2(m_prev - m_new)
            acc_s[...] = alpha * acc_s[...] + _dot(vt[j], jnp.exp2(st - m_new).astype(BF16))
            m_s[...] = m_new

    def run(fixed_shift):
        n_far = jnp.maximum(qi - 1, 0)

        def far_pair(jj, carry):
            block(2 * jj, None, fixed_shift)
            block(2 * jj + 1, None, fixed_shift)
            return carry

        lax.fori_loop(0, n_far // 2, far_pair, 0)

        @pl.when(n_far % 2 == 1)
        def _():
            block(n_far - 1, None, fixed_shift)

        @pl.when(qi >= 1)
        def _():
            block(qi - 1, 1, fixed_shift)

        block(qi, 0, fixed_shift)

    @pl.when(safe)
    def _():
        m_s[...] = bound + bmax
        run(True)

    @pl.when(jnp.logical_not(safe))
    def _():
        m_s[...] = jnp.full(m_s.shape, NEG, F32)
        run(False)

    lam_p = lam_ref[...]
    lam = (jnp.exp(jnp.sum(lam_p[0:1, :] * lam_p[1:2, :], axis=-1, keepdims=True))
           - jnp.exp(jnp.sum(lam_p[2:3, :] * lam_p[3:4, :], axis=-1, keepdims=True)) + LAM_INIT)
    a0 = acc_s[:, 0:tq]
    a1 = acc_s[:, tq:2 * tq]
    ot = a0[0:DA_DV] / a0[DA_DV:DA_DV + 1] - lam * (a1[0:DA_DV] / a1[DA_DV:DA_DV + 1])
    r = lax.rsqrt(jnp.sum(ot * ot, axis=0, keepdims=True) * (1.0 / DA_DV) + EPS)
    ot = ot * r * (sg_ref[...] * (1.0 - LAM_INIT))
    o_ref[...] = ot.T.astype(o_ref.dtype)


def _attention(p, bias, qg, kg, lam_p, subln, batch, seq):
    t = batch * seq
    tq = min(ATT_BLOCK, seq)
    nq = seq // tq
    dv = DA_DV
    return pl.pallas_call(
        _attn_body,
        grid=(batch, DA_HEADS, nq),
        in_specs=[
            pl.BlockSpec((tq, dv), lambda b, h, i: (b * nq + i, P_DAQ // dv + h)),
            pl.BlockSpec((seq, dv), lambda b, h, i: (b, P_DAK // dv + h)),
            pl.BlockSpec((seq, dv), lambda b, h, i: (b, P_DAV // dv + h)),
            pl.BlockSpec((1, 2, tq, tq), lambda b, h, i: (h, 0, 0, 0)),
            pl.BlockSpec((1, dv), lambda b, h, i: (0, 0)),
            pl.BlockSpec((1, dv), lambda b, h, i: (0, 0)),
            pl.BlockSpec((4, DA_DH), lambda b, h, i: (0, 0)),
            pl.BlockSpec((dv, 1), lambda b, h, i: (0, 0)),
        ],
        out_specs=pl.BlockSpec((tq, dv), lambda b, h, i: (b * nq + i, h)),
        out_shape=jax.ShapeDtypeStruct((t, DA_HEADS * dv), BF16),
        scratch_shapes=[
            pltpu.VMEM((seq, dv), BF16),
            pltpu.VMEM((seq // tq, DA_VROWS, tq), BF16),
            pltpu.VMEM((8, dv), F32),
            pltpu.VMEM((1, 2 * tq), F32),
            pltpu.VMEM((DA_VROWS, 2 * tq), F32),
        ],
        compiler_params=pltpu.CompilerParams(dimension_semantics=("parallel", "parallel", "arbitrary")),
        name="diff_attention",
    )(p, p, p, bias, qg, kg, lam_p, subln)


def _memkv_body(mem_ref, mg_ref, w_ref, kg_ref, mk_ref, mv_ref):
    x = mem_ref[0]
    xn = x * _rms(x, D_MODEL) * mg_ref[...]
    kv = _dot(xn.astype(BF16), w_ref[...])
    hw = MX_HEADS * MX_DH
    for h in range(MX_HEADS):
        sl = slice(h * MX_DH, (h + 1) * MX_DH)
        kh = kv[:, sl]
        mk_ref[0, :, sl] = (kh * _rms(kh, MX_DH) * kg_ref[...]).astype(BF16)
    mv_ref[0] = kv[:, hw:].astype(BF16)


def _memkv(mem, mem_norm, w_kv, k_norm):
    b, n, _ = mem.shape
    hw = MX_HEADS * MX_DH
    return pl.pallas_call(
        _memkv_body,
        grid=(b,),
        in_specs=[
            pl.BlockSpec((1, n, D_MODEL), lambda i: (i, 0, 0)),
            pl.BlockSpec((1, D_MODEL), lambda i: (0, 0)),
            pl.BlockSpec((D_MODEL, 2 * hw), lambda i: (0, 0)),
            pl.BlockSpec((1, MX_DH), lambda i: (0, 0)),
        ],
        out_specs=[pl.BlockSpec((1, n, hw), lambda i: (i, 0, 0))] * 2,
        out_shape=[jax.ShapeDtypeStruct((b, n, hw), BF16)] * 2,
        name="memory_kv",
    )(mem, mem_norm, w_kv, k_norm)


def _merge_body(x_ref, odn_ref, oda_ref, mxq_ref, g0_ref, g1_ref, g2_ref, mk_ref, mv_ref, qg_ref,
                bg_ref, wb_ref, wo_ref, out_ref, omx):
    for h in range(MX_HEADS):
        sl = slice(h * MX_DH, (h + 1) * MX_DH)
        qh = mxq_ref[:, sl].astype(F32)
        qh = qh * _rms(qh, MX_DH) * qg_ref[...] * (MX_DH ** -0.5 * LOG2E)
        s = _dot_nt(qh.astype(BF16), mk_ref[0, :, sl])
        p = jnp.exp2(s - jnp.max(s, axis=-1, keepdims=True))
        oh = _dot(p.astype(BF16), mv_ref[0, :, sl]) / jnp.sum(p, axis=-1, keepdims=True)
        omx[:, sl] = oh.astype(BF16)
    y = None
    for r, (o_r, g_r) in enumerate(((odn_ref, g0_ref), (oda_ref, g1_ref), (omx, g2_ref))):
        gate = _sigmoid(g_r[...].astype(F32) + bg_ref[r:r + 1, :])
        term = gate * _dot(o_r[...], wb_ref[r])
        y = term if y is None else y + term
    out_ref[...] = x_ref[...] + _dot(y.astype(BF16), wo_ref[...])


def _merge(x2, o_dn, o_da, p, mk, mv, q_norm, b_gate, w_branch, w_out, seq):
    t = x2.shape[0]
    tm = min(512, seq)
    nt = seq // tm
    bw = 512
    n_mem = mk.shape[1]
    return pl.pallas_call(
        _merge_body,
        grid=(t // tm,),
        in_specs=[
            pl.BlockSpec((tm, D_MODEL), lambda i: (i, 0)),
            pl.BlockSpec((tm, bw), lambda i: (i, 0)),
            pl.BlockSpec((tm, bw), lambda i: (i, 0)),
            pl.BlockSpec((tm, bw), lambda i: (i, P_MXQ // bw)),
            pl.BlockSpec((tm, D_MODEL), lambda i: (i, P_GATE // D_MODEL)),
            pl.BlockSpec((tm, D_MODEL), lambda i: (i, P_GATE // D_MODEL + 1)),
            pl.BlockSpec((tm, D_MODEL), lambda i: (i, P_GATE // D_MODEL + 2)),
            pl.BlockSpec((1, n_mem, bw), lambda i: (i // nt, 0, 0)),
            pl.BlockSpec((1, n_mem, bw), lambda i: (i // nt, 0, 0)),
            pl.BlockSpec((1, MX_DH), lambda i: (0, 0)),
            pl.BlockSpec((3, D_MODEL), lambda i: (0, 0)),
            pl.BlockSpec((3, bw, D_MODEL), lambda i: (0, 0, 0)),
            pl.BlockSpec((D_MODEL, D_MODEL), lambda i: (0, 0)),
        ],
        out_specs=pl.BlockSpec((tm, D_MODEL), lambda i: (i, 0)),
        out_shape=jax.ShapeDtypeStruct((t, D_MODEL), F32),
        scratch_shapes=[pltpu.VMEM((tm, bw), BF16)],
        compiler_params=pltpu.CompilerParams(dimension_semantics=("parallel",)),
        name="merge",
    )(x2, o_dn, o_da, p, p, p, p, mk, mv, q_norm, b_gate, w_branch, w_out)


def _router_body(x_ref, g_ref, wr_ref, br_ref, h_ref, idx_ref, wt_ref, rank_ref, cnt_ref):
    tm = x_ref.shape[0]
    x = x_ref[...]
    h = x * _rms(x, D_MODEL) * g_ref[...]
    h_ref[...] = h.astype(BF16)
    logits = _dot3(h, wr_ref[...]) + br_ref[...]
    lane = lax.broadcasted_iota(I32, (tm, LANES), 1)
    lane_f = lane.astype(F32)
    work = logits
    sel = jnp.zeros((tm, LANES), F32)
    vals, idxs = [], []
    for _ in range(TOP_K):
        mx = jnp.max(work, axis=-1, keepdims=True)
        ik = jnp.min(jnp.where(work == mx, lane_f, float(LANES)), axis=-1, keepdims=True)
        hit = lane_f == ik
        sel = jnp.where(hit, 1.0, sel)
        work = jnp.where(hit, -jnp.inf, work)
        vals.append(mx)
        idxs.append(ik)
    es = [jnp.exp(v - vals[0]) for v in vals]
    den = es[0] + es[1] + es[2] + es[3]
    r = lax.broadcasted_iota(I32, (tm, tm), 0)
    c = lax.broadcasted_iota(I32, (tm, tm), 1)
    tril = jnp.where(r > c, 1.0, 0.0).astype(BF16)
    cum = _dot(tril, sel.astype(BF16))
    idx_o = jnp.zeros((tm, LANES), F32)
    wt_o = jnp.zeros((tm, LANES), F32)
    rank_o = jnp.zeros((tm, LANES), F32)
    for k in range(TOP_K):
        rk = jnp.sum(jnp.where(lane_f == idxs[k], cum, 0.0), axis=-1, keepdims=True)
        idx_o = jnp.where(lane == k, idxs[k], idx_o)
        wt_o = jnp.where(lane == k, es[k] / den, wt_o)
        rank_o = jnp.where(lane == k, rk, rank_o)
    idx_ref[...] = idx_o.astype(I32)
    wt_ref[...] = wt_o
    rank_ref[...] = rank_o.astype(I32)
    cnt_ref[0] = jnp.broadcast_to(jnp.sum(sel, axis=0, keepdims=True), (8, LANES))


def _router(x1, gain, w_r, b_r):
    t = x1.shape[0]
    tm = min(MOE_TILE, t)
    row = lambda i: (i, 0)
    fixed = lambda i: (0, 0)
    return pl.pallas_call(
        _router_body,
        grid=(t // tm,),
        in_specs=[
            pl.BlockSpec((tm, D_MODEL), row),
            pl.BlockSpec((1, D_MODEL), fixed),
            pl.BlockSpec((D_MODEL, LANES), fixed),
            pl.BlockSpec((1, LANES), fixed),
        ],
        out_specs=[
            pl.BlockSpec((tm, D_MODEL), row),
            pl.BlockSpec((tm, LANES), row),
            pl.BlockSpec((tm, LANES), row),
            pl.BlockSpec((tm, LANES), row),
            pl.BlockSpec((1, 8, LANES), lambda i: (i, 0, 0)),
        ],
        out_shape=[
            jax.ShapeDtypeStruct((t, D_MODEL), BF16),
            jax.ShapeDtypeStruct((t, LANES), I32),
            jax.ShapeDtypeStruct((t, LANES), F32),
            jax.ShapeDtypeStruct((t, LANES), I32),
            jax.ShapeDtypeStruct((t // tm, 8, LANES), F32),
        ],
        compiler_params=pltpu.CompilerParams(dimension_semantics=("arbitrary",)),
        name="router",
    )(x1, gain, w_r, b_r)


def _lane_cumsum(x):
    lane = lax.broadcasted_iota(I32, x.shape, 1)
    s = 1
    while s < N_EXPERTS:
        x = x + jnp.where(lane >= s, pltpu.roll(x, s, axis=1), 0.0)
        s *= 2
    return x


def _plan_body(cnt_ref, gs_ref, ls_ref, n8_ref, lsf_ref, tail_ref, meta_ref):
    nt = cnt_ref.shape[0]
    ga = float(GROUP_ALIGN)
    eb = float(EXPERT_BLOCK)
    lane = lax.broadcasted_iota(I32, (nt, LANES), 1)
    r8 = jnp.where(lane < N_EXPERTS, jnp.floor((cnt_ref[...] + (ga - 1.0)) * (1.0 / ga)) * ga, 0.0)
    ri = lax.broadcasted_iota(I32, (nt, nt), 0)
    ci = lax.broadcasted_iota(I32, (nt, nt), 1)
    before = _dot(jnp.where(ri > ci, 1.0, 0.0).astype(BF16), r8.astype(BF16))
    tot = jnp.sum(r8, axis=0, keepdims=True)
    region = jnp.floor((tot + (eb - 1.0)) * (1.0 / eb)) * eb
    pends = _lane_cumsum(jnp.broadcast_to(region, (8, LANES)))[0:1, :]
    pstart = pends - region
    lstart = _lane_cumsum(r8) - r8
    gs_ref[...] = (pstart + before).astype(I32)
    ls_ref[...] = lstart.astype(I32)
    n8_ref[...] = (r8 * (1.0 / ga)).astype(I32)
    lsf_ref[...] = lstart
    row8 = lax.broadcasted_iota(I32, (8, LANES), 0)
    tail = jnp.where(row8 == 0, pstart + tot, jnp.where(row8 == 1, (region - tot) * (1.0 / ga), 0.0))
    tail_ref[...] = tail.astype(I32)
    nb = meta_ref.shape[0]
    ln = lax.broadcasted_iota(I32, (nb, LANES), 1)
    blk = lax.broadcasted_iota(I32, (nb, LANES), 0).astype(F32) * eb
    be = jnp.sum(jnp.where((ln < N_EXPERTS) & (pends <= blk), 1.0, 0.0), axis=-1, keepdims=True)
    be = jnp.minimum(be, float(N_EXPERTS - 1))
    used = jnp.sum(jnp.where(ln == N_EXPERTS - 1, pends, 0.0), axis=-1, keepdims=True) * (1.0 / eb)
    meta_ref[...] = jnp.where(ln == 0, be, jnp.where(ln == 1, used, 0.0)).astype(I32)


def _plan(cnt, n_blocks_pad):
    nt = cnt.shape[0]
    shp = jax.ShapeDtypeStruct((nt, LANES), I32)
    return pl.pallas_call(
        _plan_body,
        out_shape=[shp, shp, shp, jax.ShapeDtypeStruct((nt, LANES), F32),
                   jax.ShapeDtypeStruct((8, LANES), I32),
                   jax.ShapeDtypeStruct((n_blocks_pad, LANES), I32)],
        name="dispatch_plan",
    )(cnt)


def _local_positions(idx_ref, rank_ref, lsf_ref):
    tm = idx_ref.shape[0]
    lane = lax.broadcasted_iota(I32, (tm, LANES), 1)
    idx = idx_ref[...]
    rank = rank_ref[...].astype(F32)
    ls_row = lsf_ref[0]
    out = []
    for k in range(TOP_K):
        base = jnp.sum(jnp.where(lane == idx[:, k:k + 1], ls_row, 0.0), axis=-1, keepdims=True)
        out.append((base + rank[:, k:k + 1]).astype(I32))
    return out


def _group_copies(gs_ref, ls_ref, n8_ref, make_copy, start):
    def group(e, carry):
        g0 = gs_ref[0, 0, e]
        l0 = ls_ref[0, 0, e]

        def one(j, c):
            cp = make_copy(pl.multiple_of(l0 + j * GROUP_ALIGN, GROUP_ALIGN),
                           pl.multiple_of(g0 + j * GROUP_ALIGN, GROUP_ALIGN))
            cp.start() if start else cp.wait()
            return c

        lax.fori_loop(0, n8_ref[0, 0, e], one, 0)
        return carry

    lax.fori_loop(0, N_EXPERTS, group, 0)


def _dispatch_body(gs_ref, ls_ref, n8_ref, gsp_ref, lsp_ref, n8p_ref, tail_ref, h_ref, idx_ref, rank_ref,
                   lsf_ref, xs_ref, xl, zbuf, sem):
    i = pl.program_id(0)
    last = i == pl.num_programs(0) - 1
    slot = i % 2
    tm = h_ref.shape[0]
    lrows = xl.shape[1]
    pos = lax.broadcasted_iota(I32, (tm, lrows), 1)
    lp = _local_positions(idx_ref, rank_ref, lsf_ref)
    hit = pos == lp[0]
    for k in range(1, TOP_K):
        hit = hit | (pos == lp[k])
    xl[slot] = _dot_tn(jnp.where(hit, 1.0, 0.0).astype(BF16), h_ref[...])

    def copy_from(s):
        def make_copy(l, g):
            return pltpu.make_async_copy(xl.at[s, pl.ds(l, GROUP_ALIGN), :],
                                         xs_ref.at[pl.ds(g, GROUP_ALIGN), :], sem.at[s])
        return make_copy

    @pl.when(i >= 1)
    def _():
        _group_copies(gsp_ref, lsp_ref, n8p_ref, copy_from(1 - slot), False)

    _group_copies(gs_ref, ls_ref, n8_ref, copy_from(slot), True)

    def tails(start):
        def per_expert(e, carry):
            g0 = tail_ref[0, e]

            def one(j, c):
                cp = pltpu.make_async_copy(
                    zbuf, xs_ref.at[pl.ds(pl.multiple_of(g0 + j * GROUP_ALIGN, GROUP_ALIGN), GROUP_ALIGN), :],
                    sem.at[2])
                cp.start() if start else cp.wait()
                return c

            lax.fori_loop(0, tail_ref[1, e], one, 0)
            return carry

        lax.fori_loop(0, N_EXPERTS, per_expert, 0)

    @pl.when(last)
    def _():
        zbuf[...] = jnp.zeros_like(zbuf)
        tails(True)
        _group_copies(gs_ref, ls_ref, n8_ref, copy_from(slot), False)
        tails(False)


def _smem_row(shift=0):
    return pl.BlockSpec((1, 1, LANES),
                        lambda i: (jnp.clip(i + shift, 0, pl.num_programs(0) - 1), 0, 0),
                        memory_space=pltpu.SMEM)


def _dispatch(gs, ls, n8, tail, h2, idx, rank, lsf, n_slots):
    t = h2.shape[0]
    tm = min(MOE_TILE, t)
    nt = t // tm
    lrows = tm * TOP_K + N_EXPERTS * GROUP_ALIGN
    row = lambda i: (i, 0)
    return pl.pallas_call(
        _dispatch_body,
        grid=(nt,),
        in_specs=[
            _smem_row(), _smem_row(), _smem_row(),
            _smem_row(-1), _smem_row(-1), _smem_row(-1),
            pl.BlockSpec(memory_space=pltpu.SMEM),
            pl.BlockSpec((tm, D_MODEL), row),
            pl.BlockSpec((tm, LANES), row),
            pl.BlockSpec((tm, LANES), row),
            pl.BlockSpec((1, 1, LANES), lambda i: (i, 0, 0)),
        ],
        out_specs=pl.BlockSpec(memory_space=pl.ANY),
        out_shape=jax.ShapeDtypeStruct((n_slots, D_MODEL), F32),
        scratch_shapes=[pltpu.VMEM((2, lrows, D_MODEL), F32), pltpu.VMEM((GROUP_ALIGN, D_MODEL), F32),
                        pltpu.SemaphoreType.DMA((3,))],
        compiler_params=pltpu.CompilerParams(dimension_semantics=("arbitrary",)),
        name="moe_dispatch",
    )(gs, ls, n8, gs, ls, n8, tail, h2, idx, rank, lsf)


def _expert_body(be_ref, nu_ref, x_ref, wgu_ref, bgu_ref, wd_ref, bd_ref, y_ref, wgu_b, wd_b):
    j = pl.program_id(0)

    @pl.when(j < nu_ref[0])
    def _():
        @pl.when(jnp.logical_or(j == 0, be_ref[j] != be_ref[jnp.maximum(j - 1, 0)]))
        def _():
            wgu_b[...] = wgu_ref[0].astype(BF16)
            wd_b[...] = wd_ref[0].astype(BF16)

        gu = _dot(x_ref[...].astype(BF16), wgu_b[...]) + bgu_ref[0]
        gate = jnp.minimum(gu[:, :D_FF], SWIGLU_LIMIT)
        up = jnp.clip(gu[:, D_FF:], -SWIGLU_LIMIT, SWIGLU_LIMIT)
        act = (up + 1.0) * gate * _sigmoid(SWIGLU_ALPHA * gate)
        y_ref[...] = _dot(act.astype(BF16), wd_b[...]) + bd_ref[0]


def _experts(block_e, n_used, xs, wgu, bgu, wd, bd):
    n_slots = xs.shape[0]
    nb = n_slots // EXPERT_BLOCK

    def blk(j, be, nu):
        return jnp.minimum(j, nu[0] - 1)

    grid_spec = pltpu.PrefetchScalarGridSpec(
        num_scalar_prefetch=2,
        grid=(nb,),
        in_specs=[
            pl.BlockSpec((EXPERT_BLOCK, D_MODEL), lambda j, be, nu: (blk(j, be, nu), 0)),
            pl.BlockSpec((1, D_MODEL, 2 * D_FF), lambda j, be, nu: (be[blk(j, be, nu)], 0, 0)),
            pl.BlockSpec((1, 1, 2 * D_FF), lambda j, be, nu: (be[blk(j, be, nu)], 0, 0)),
            pl.BlockSpec((1, D_FF, D_MODEL), lambda j, be, nu: (be[blk(j, be, nu)], 0, 0)),
            pl.BlockSpec((1, 1, D_MODEL), lambda j, be, nu: (be[blk(j, be, nu)], 0, 0)),
        ],
        out_specs=pl.BlockSpec((EXPERT_BLOCK, D_MODEL), lambda j, be, nu: (blk(j, be, nu), 0)),
        scratch_shapes=[pltpu.VMEM((D_MODEL, 2 * D_FF), BF16), pltpu.VMEM((D_FF, D_MODEL), BF16)],
    )
    return pl.pallas_call(
        _expert_body,
        grid_spec=grid_spec,
        out_shape=jax.ShapeDtypeStruct((n_slots, D_MODEL), F32),
        compiler_params=pltpu.CompilerParams(dimension_semantics=("arbitrary",)),
        name="moe_experts",
    )(block_e, n_used, xs, wgu, bgu, wd, bd)


def _combine_body(gs_ref, ls_ref, n8_ref, gsn_ref, lsn_ref, n8n_ref, x_ref, wt_ref, idx_ref, rank_ref,
                  lsf_ref, y_ref, out_ref, yl, sem):
    i = pl.program_id(0)
    slot = i % 2
    tm = x_ref.shape[0]
    lrows = yl.shape[1]

    def copy_into(s):
        def make_copy(l, g):
            return pltpu.make_async_copy(y_ref.at[pl.ds(g, GROUP_ALIGN), :],
                                         yl.at[s, pl.ds(l, GROUP_ALIGN), :], sem.at[s])
        return make_copy

    def fetch(s, tables):
        yl[s] = jnp.zeros((lrows, D_MODEL), F32)
        _group_copies(*tables, copy_into(s), True)

    @pl.when(i == 0)
    def _():
        fetch(slot, (gs_ref, ls_ref, n8_ref))

    @pl.when(i + 1 < pl.num_programs(0))
    def _():
        fetch(1 - slot, (gsn_ref, lsn_ref, n8n_ref))

    pos = lax.broadcasted_iota(I32, (tm, lrows), 1)
    lp = _local_positions(idx_ref, rank_ref, lsf_ref)
    wt = wt_ref[...]
    wm = jnp.zeros((tm, lrows), F32)
    for k in range(TOP_K):
        wm = jnp.where(pos == lp[k], wt[:, k:k + 1], wm)
    _group_copies(gs_ref, ls_ref, n8_ref, copy_into(slot), False)
    out_ref[...] = x_ref[...] + _dot(wm.astype(BF16), yl[slot].astype(BF16))


def _combine(gs, ls, n8, x1, wts, idx, rank, lsf, y):
    t = x1.shape[0]
    tm = min(MOE_TILE, t)
    nt = t // tm
    lrows = tm * TOP_K + N_EXPERTS * GROUP_ALIGN
    row = lambda i: (i, 0)
    return pl.pallas_call(
        _combine_body,
        grid=(nt,),
        in_specs=[
            _smem_row(), _smem_row(), _smem_row(),
            _smem_row(1), _smem_row(1), _smem_row(1),
            pl.BlockSpec((tm, D_MODEL), row),
            pl.BlockSpec((tm, LANES), row),
            pl.BlockSpec((tm, LANES), row),
            pl.BlockSpec((tm, LANES), row),
            pl.BlockSpec((1, 1, LANES), lambda i: (i, 0, 0)),
            pl.BlockSpec(memory_space=pl.ANY),
        ],
        out_specs=pl.BlockSpec((tm, D_MODEL), row),
        out_shape=jax.ShapeDtypeStruct((t, D_MODEL), F32),
        scratch_shapes=[pltpu.VMEM((2, lrows, D_MODEL), F32), pltpu.SemaphoreType.DMA((2,))],
        compiler_params=pltpu.CompilerParams(dimension_semantics=("arbitrary",)),
        name="moe_combine",
    )(gs, ls, n8, gs, ls, n8, x1, wts, idx, rank, lsf, y)


def _pad_lanes(v, fill=0.0):
    v = v.astype(F32).reshape(1, -1)
    return jnp.pad(v, ((0, 0), (0, LANES - v.shape[1])), constant_values=fill)


def _mixer(x2, mem, rel_table, attn_norm, w_in, b_gate, dn_conv, dn_a_log, dn_dt_bias, dn_out_norm,
           da_q_norm, da_k_norm, da_lambda, da_subln, mem_norm, w_mem_kv, mx_q_norm, mx_k_norm,
           w_branch, w_out, batch, seq):
    wp = jnp.concatenate([w_in[:, :W_AB_LO], w_in[:, W_AB_HI:]], axis=1).astype(BF16)
    wab = jnp.pad(w_in[:, W_AB_LO:W_AB_HI], ((0, 0), (0, LANES - (W_AB_HI - W_AB_LO))))
    p, ab = _inproj(x2, attn_norm.reshape(1, -1), wp, wab)

    o_dn = _deltanet(p, ab, dn_conv, _pad_lanes(dn_a_log), _pad_lanes(dn_dt_bias),
                     dn_out_norm.reshape(1, -1), batch, seq)

    tq = min(ATT_BLOCK, seq)
    bias = _bias_tiles(rel_table.T, tq)
    o_da = _attention(p, bias, jnp.tile(da_q_norm, 2).reshape(1, -1), jnp.tile(da_k_norm, 2).reshape(1, -1),
                      da_lambda, da_subln.reshape(-1, 1), batch, seq)

    mk, mv = _memkv(mem, mem_norm.reshape(1, -1), w_mem_kv.astype(BF16), mx_k_norm.reshape(1, -1))
    return _merge(x2, o_dn, o_da, p, mk, mv, mx_q_norm.reshape(1, -1), b_gate.reshape(3, D_MODEL),
                  w_branch.astype(BF16), w_out.astype(BF16), seq)


def _moe(x1, ffn_norm, w_router, b_router, w_gate_up, b_gate_up, w_down, b_down):
    t = x1.shape[0]
    nt = t // min(MOE_TILE, t)
    max_rows = t * TOP_K + nt * N_EXPERTS * (GROUP_ALIGN - 1)
    n_blocks = -(-max_rows // EXPERT_BLOCK) + N_EXPERTS
    n_blocks_pad = -(-n_blocks // 8) * 8
    n_slots = n_blocks * EXPERT_BLOCK

    wr = jnp.pad(w_router, ((0, 0), (0, LANES - N_EXPERTS)))
    h2, idx, wts, rank, cnt = _router(x1, ffn_norm.reshape(1, -1), wr, _pad_lanes(b_router, NEG))
    gs, ls, n8, lsf, tail, meta = _plan(cnt[:, 0, :], n_blocks_pad)
    block_e = meta[:n_blocks, 0]
    n_used = meta[0:1, 1]
    gs, ls, n8, lsf = (a.reshape(nt, 1, LANES) for a in (gs, ls, n8, lsf))

    xs = _dispatch(gs, ls, n8, tail, h2, idx, rank, lsf, n_slots)
    y = _experts(block_e, n_used, xs, w_gate_up, b_gate_up.reshape(N_EXPERTS, 1, -1),
                 w_down, b_down.reshape(N_EXPERTS, 1, -1))
    return _combine(gs, ls, n8, x1, wts, idx, rank, lsf, y)


def kernel(x, mem, rel_table, attn_norm, w_in, b_gate, dn_conv, dn_a_log, dn_dt_bias, dn_out_norm,
           da_q_norm, da_k_norm, da_lambda, da_subln, mem_norm, w_mem_kv, mx_q_norm, mx_k_norm,
           w_branch, w_out, ffn_norm, w_router, b_router, w_gate_up, b_gate_up, w_down, b_down):
    batch, seq, d = x.shape
    x2 = x.reshape(batch * seq, d)
    x1 = _mixer(x2, mem, rel_table, attn_norm[0], w_in[0], b_gate[0], dn_conv[0], dn_a_log[0],
                dn_dt_bias[0], dn_out_norm[0], da_q_norm[0], da_k_norm[0], da_lambda[0], da_subln[0],
                mem_norm[0], w_mem_kv[0], mx_q_norm[0], mx_k_norm[0], w_branch[0], w_out[0], batch, seq)
    out = _moe(x1, ffn_norm[0], w_router[0], b_router[0], w_gate_up[0], b_gate_up[0], w_down[0],
               b_down[0])
    return out.reshape(batch, seq, d)
```

```python
import functools
import math

import jax
import jax.numpy as jnp
from jax import lax
from jax.experimental import pallas as pl
from jax.experimental.pallas import tpu as pltpu

F32 = jnp.float32
BF16 = jnp.bfloat16
I32 = jnp.int32

D_MODEL = 1024
EPS = 1e-6
LANES = 128

DN_HEADS = 4
DN_DK = 128
DN_CHUNK = 64
DN_CONV = 4

DA_HEADS = 4
DA_DH = 64

MX_HEADS = 4
MX_DH = 128

REL_BUCKETS = 32
REL_MAX_DIST = 128

N_EXPERTS = 32
TOP_K = 4
D_FF = 1024
SWIGLU_LIMIT = 7.0
SWIGLU_ALPHA = 1.702
EXPERT_BLOCK = 512
MOE_TILE = 256
GROUP_ALIGN = 8

LAM_INIT = 0.8 - 0.6 * math.exp(-0.3 * 0)
LOG2E = 1.4426950408889634
NEG = -1e30

P_DNQ, P_DNK, P_DNV, P_DNZ = 0, 512, 1024, 1536
P_DAQ, P_DAK, P_DAV = 2048, 2560, 3072
P_MXQ = 3584
P_GATE = 4096
P_COLS = 7168
W_AB_LO, W_AB_HI = 2048, 2056


def _dot(a, b):
    return jnp.dot(a, b, preferred_element_type=F32)


def _dot_nt(a, b):
    return lax.dot_general(a, b, (((1,), (1,)), ((), ())), preferred_element_type=F32)


def _dot_tn(a, b):
    return lax.dot_general(a, b, (((0,), (0,)), ((), ())), preferred_element_type=F32)


def _split(x):
    hi = x.astype(BF16)
    lo = (x - hi.astype(F32)).astype(BF16)
    return hi, lo


def _dot3(a, b):
    ah, al = _split(a)
    bh, bl = _split(b)
    return _dot(ah, bh) + _dot(ah, bl) + _dot(al, bh)


def _sigmoid(x):
    return 1.0 / (1.0 + jnp.exp(-x))


def _rms(x, n):
    return lax.rsqrt(jnp.sum(x * x, axis=-1, keepdims=True) * (1.0 / n) + EPS)


def _inproj_body(x_ref, g_ref, w_ref, wab_ref, p_ref, ab_ref, h_scr):
    @pl.when(pl.program_id(1) == 0)
    def _():
        x = x_ref[...]
        h = x * _rms(x, D_MODEL) * g_ref[...]
        h_scr[...] = h.astype(BF16)
        ab_ref[...] = _dot3(h, wab_ref[...])

    p_ref[...] = _dot(h_scr[...], w_ref[...]).astype(p_ref.dtype)


def _inproj(x2, gain, wp, wab):
    t = x2.shape[0]
    tm = min(1024, t)
    tn = 1024
    return pl.pallas_call(
        _inproj_body,
        grid=(t // tm, P_COLS // tn),
        in_specs=[
            pl.BlockSpec((tm, D_MODEL), lambda i, j: (i, 0)),
            pl.BlockSpec((1, D_MODEL), lambda i, j: (0, 0)),
            pl.BlockSpec((D_MODEL, tn), lambda i, j: (0, j)),
            pl.BlockSpec((D_MODEL, LANES), lambda i, j: (0, 0)),
        ],
        out_specs=[
            pl.BlockSpec((tm, tn), lambda i, j: (i, j)),
            pl.BlockSpec((tm, LANES), lambda i, j: (i, 0)),
        ],
        out_shape=[
            jax.ShapeDtypeStruct((t, P_COLS), BF16),
            jax.ShapeDtypeStruct((t, LANES), F32),
        ],
        scratch_shapes=[pltpu.VMEM((tm, D_MODEL), BF16)],
        compiler_params=pltpu.CompilerParams(dimension_semantics=("parallel", "arbitrary")),
        name="inproj",
    )(x2, gain, wp, wab)


DN_HALO = 16
DN_SCAN_CHUNK = 256


def _deltanet_body(q_ref, k_ref, v_ref, z_ref, qh_ref, kh_ref, vh_ref, ab_ref, cw_ref, alog_ref,
                   dtb_ref, on_ref, o_ref, stage, qs, ks, vs, s_scr):
    i = pl.program_id(1)
    tc = q_ref.shape[0]
    hw = DN_HEADS * DN_DK

    @pl.when(i == 0)
    def _():
        s_scr[...] = jnp.zeros_like(s_scr)

    for src, halo, dst, off, kind in ((q_ref, qh_ref, qs, 0, "q"), (k_ref, kh_ref, ks, hw, "k"),
                                      (v_ref, vh_ref, vs, 2 * hw, "v")):
        hal = halo[...].astype(F32)
        stage[0:DN_HALO, :] = jnp.where(i == 0, 0.0, hal)
        stage[DN_HALO:DN_HALO + tc, :] = src[...].astype(F32)
        base = DN_HALO - (DN_CONV - 1)
        y = stage[base:base + tc, :] * cw_ref[0:1, off:off + hw]
        for j in range(1, DN_CONV):
            y = y + stage[base + j:base + j + tc, :] * cw_ref[j:j + 1, off:off + hw]
        y = y * _sigmoid(y)
        if kind == "v":
            dst[...] = y
        else:
            for h in range(DN_HEADS):
                sl = slice(h * DN_DK, (h + 1) * DN_DK)
                yh = y[:, sl]
                r = lax.rsqrt(jnp.sum(yh * yh, axis=-1, keepdims=True) + EPS)
                if kind == "q":
                    r = r * (DN_DK ** -0.5)
                dst[:, sl] = yh * r

    c = min(DN_SCAN_CHUNK, tc)
    row = lax.broadcasted_iota(I32, (c, c), 0)
    col = lax.broadcasted_iota(I32, (c, c), 1)
    incl = row >= col
    strict = row > col
    same_blk = (row // DN_CHUNK) == (col // DN_CHUNK)
    tri = jnp.where(incl, 1.0, 0.0).astype(BF16)
    eye = jnp.where(row == col, 1.0, 0.0)
    neg_a = -jnp.exp(alog_ref[...])
    dtb = dtb_ref[...]

    def chunk(ci, carry):
        r0 = pl.multiple_of(ci * c, c)
        abc = ab_ref[pl.ds(r0, c), :]
        a_in = abc + dtb
        g_all = neg_a * (jnp.maximum(a_in, 0.0) + jnp.log(1.0 + jnp.exp(-jnp.abs(a_in))))
        beta_all = _sigmoid(abc)
        zc = z_ref[pl.ds(r0, c), :].astype(F32)
        for h in range(DN_HEADS):
            sl = slice(h * DN_DK, (h + 1) * DN_DK)
            q = qs[pl.ds(r0, c), sl]
            k = ks[pl.ds(r0, c), sl]
            v = vs[pl.ds(r0, c), sl]
            g = g_all[:, h:h + 1]
            beta = beta_all[:, DN_HEADS + h:DN_HEADS + h + 1]
            g_hi, g_lo = _split(jnp.where(strict, g, 0.0))
            diff = _dot(tri, g_hi) + _dot(tri, g_lo)
            gc = diff[:, 0:1] + g[0:1, :]
            decay = jnp.where(incl, jnp.exp(diff), 0.0)
            kb = k.astype(BF16)
            qkk = _dot_nt(jnp.concatenate([q.astype(BF16), kb], axis=0), kb)
            qk = qkk[:c]
            kk = qkk[c:]
            lower = jnp.where(strict, kk * decay * beta, 0.0)
            pw = jnp.where(same_blk, -lower, 0.0)
            dinv = eye + pw
            for _ in range(int(math.log2(DN_CHUNK)) - 1):
                pwb = pw.astype(BF16)
                pw = _dot(pwb, pwb)
                dinv = dinv + _dot(dinv.astype(BF16), pw.astype(BF16))
            dinv_b = dinv.astype(BF16)
            pw = -_dot(dinv_b, jnp.where(same_blk, 0.0, lower).astype(BF16))
            xm = eye + pw
            for _ in range(int(math.log2(c // DN_CHUNK)) - 1):
                pwb = pw.astype(BF16)
                pw = _dot(pwb, pwb)
                xm = xm + _dot(xm.astype(BF16), pw.astype(BF16))
            inv = _dot(xm.astype(BF16), dinv_b)
            egc = jnp.exp(gc)
            rhs = jnp.concatenate([v * beta, k * (beta * egc)], axis=1)
            sol = _dot(inv.astype(BF16), rhs.astype(BF16))
            u = sol[:, :DN_DK]
            w = sol[:, DN_DK:]
            qkm = jnp.where(incl, qk * decay, 0.0)
            gl = gc[c - 1:c, :]
            state = s_scr[h]
            sb = state.astype(BF16)
            ws = _dot(jnp.concatenate([w.astype(BF16), (q * egc).astype(BF16)], axis=0), sb)
            v_new = u - ws[:c]
            o = ws[c:] + _dot(qkm.astype(BF16), v_new.astype(BF16))
            s_scr[h] = state * jnp.exp(gl) + _dot_tn(kb, (v_new * jnp.exp(gl - gc)).astype(BF16))
            zz = zc[:, sl]
            on = o * _rms(o, DN_DK) * on_ref[...]
            o_ref[pl.ds(r0, c), sl] = (on * (zz * _sigmoid(zz))).astype(o_ref.dtype)
        return carry

    lax.fori_loop(0, tc // c, chunk, 0, unroll=True)


def _deltanet(p, ab, conv_w, alog_row, dtb_row, out_norm, batch, seq):
    t = batch * seq
    tc = min(512, seq)
    nt = seq // tc
    hw = DN_HEADS * DN_DK

    def main(cb):
        return pl.BlockSpec((tc, hw), lambda b, i: (b * nt + i, cb))

    def halo(cb):
        return pl.BlockSpec(
            (DN_HALO, hw),
            lambda b, i: (jnp.maximum((b * seq + i * tc) // DN_HALO - 1, 0), cb))

    return pl.pallas_call(
        _deltanet_body,
        grid=(batch, nt),
        in_specs=[
            main(P_DNQ // hw), main(P_DNK // hw), main(P_DNV // hw), main(P_DNZ // hw),
            halo(P_DNQ // hw), halo(P_DNK // hw), halo(P_DNV // hw),
            pl.BlockSpec((tc, LANES), lambda b, i: (b * nt + i, 0)),
            pl.BlockSpec((DN_CONV, 3 * hw), lambda b, i: (0, 0)),
            pl.BlockSpec((1, LANES), lambda b, i: (0, 0)),
            pl.BlockSpec((1, LANES), lambda b, i: (0, 0)),
            pl.BlockSpec((1, DN_DK), lambda b, i: (0, 0)),
        ],
        out_specs=pl.BlockSpec((tc, hw), lambda b, i: (b * nt + i, 0)),
        out_shape=jax.ShapeDtypeStruct((t, hw), BF16),
        scratch_shapes=[
            pltpu.VMEM((DN_HALO + tc, hw), F32),
            pltpu.VMEM((tc, hw), F32),
            pltpu.VMEM((tc, hw), F32),
            pltpu.VMEM((tc, hw), F32),
            pltpu.VMEM((DN_HEADS, DN_DK, DN_DK), F32),
        ],
        compiler_params=pltpu.CompilerParams(dimension_semantics=("parallel", "arbitrary")),
        name="deltanet",
    )(p, p, p, p, p, p, p, ab, conv_w, alog_row, dtb_row, out_norm)


ATT_BLOCK = 512


def _bias_body(tbl_ref, o_ref):
    h = pl.program_id(0)
    tq = o_ref.shape[2]
    key = lax.broadcasted_iota(I32, (tq, tq), 0)
    qry = lax.broadcasted_iota(I32, (tq, tq), 1)
    max_exact = REL_BUCKETS // 2
    far = tbl_ref[h, REL_BUCKETS - 1]
    for d in range(2):
        n = qry - key + d * tq
        nn = jnp.maximum(n, 0)
        nf = jnp.maximum(nn, 1).astype(F32)
        large = max_exact + (jnp.log(nf / max_exact) / math.log(REL_MAX_DIST / max_exact)
                             * (REL_BUCKETS - max_exact)).astype(I32)
        large = jnp.minimum(large, REL_BUCKETS - 1)
        bucket = jnp.where(nn < max_exact, nn, large)
        val = jnp.zeros((tq, tq), F32)
        for b in range(REL_BUCKETS):
            val = jnp.where(bucket == b, tbl_ref[h, b], val)
        o_ref[0, d] = jnp.where(n >= 0, (val - far) * LOG2E, NEG)


def _bias_tiles(tbl_t, tq):
    return pl.pallas_call(
        _bias_body,
        grid=(DA_HEADS,),
        in_specs=[pl.BlockSpec(memory_space=pltpu.SMEM)],
        out_specs=pl.BlockSpec((1, 2, tq, tq), lambda h: (h, 0, 0, 0)),
        out_shape=jax.ShapeDtypeStruct((DA_HEADS, 2, tq, tq), F32),
        name="t5_bias_tiles",
    )(tbl_t)


DA_DV = 2 * DA_DH
DA_VROWS = DA_DV + 16


BOUND_SLACK = 1.02
MAX_SHIFT_GAP = 110.0


def _attn_body(q_ref, k_ref, v_ref, bias_ref, qg_ref, kg_ref, lam_ref, sg_ref, o_ref,
               kn, vt, kst, m_s, acc_s):
    qi = pl.program_id(2)
    tq = q_ref.shape[0]
    seq = k_ref.shape[0]
    tk = tq
    lo_mask = lax.broadcasted_iota(I32, (1, DA_DV), 1) < DA_DH

    def group_norm(x, gain):
        x2 = x * x
        lo = jnp.sum(jnp.where(lo_mask, x2, 0.0), axis=-1, keepdims=True)
        hi = jnp.sum(jnp.where(lo_mask, 0.0, x2), axis=-1, keepdims=True)
        r = jnp.where(lo_mask, lax.rsqrt(lo * (1.0 / DA_DH) + EPS), lax.rsqrt(hi * (1.0 / DA_DH) + EPS))
        return x * r * gain

    @pl.when(qi == 0)
    def _():
        ones = jnp.ones((DA_VROWS - DA_DV, tk), BF16)

        def body(c, kmax2):
            r0 = pl.multiple_of(c * tk, tk)
            kb = group_norm(k_ref[pl.ds(r0, tk), :].astype(F32), kg_ref[...]).astype(BF16)
            kn[pl.ds(r0, tk), :] = kb
            vt[c, 0:DA_DV, :] = v_ref[pl.ds(r0, tk), :].astype(F32).T.astype(BF16)
            vt[c, DA_DV:DA_VROWS, :] = ones
            k2 = kb.astype(F32)
            k2 = k2 * k2
            lo = jnp.max(jnp.sum(jnp.where(lo_mask, k2, 0.0), axis=-1, keepdims=True), axis=0, keepdims=True)
            hi = jnp.max(jnp.sum(jnp.where(lo_mask, 0.0, k2), axis=-1, keepdims=True), axis=0, keepdims=True)
            return jnp.maximum(kmax2, jnp.where(lo_mask, lo, hi))
        kst[0:1, :] = lax.fori_loop(0, seq // tk, body, jnp.zeros((1, DA_DV), F32))
        b0 = bias_ref[0, 0]
        b1 = bias_ref[0, 1]
        bmax = jnp.maximum(jnp.max(jnp.maximum(b0, b1), axis=0, keepdims=True), 0.0)
        bmin = jnp.minimum(jnp.min(jnp.minimum(jnp.where(b0 > 0.5 * NEG, b0, 0.0), b1), axis=0, keepdims=True), 0.0)
        kst[1:2, :] = jnp.broadcast_to(jnp.max(bmax, axis=1, keepdims=True), (1, DA_DV))
        kst[2:3, :] = jnp.broadcast_to(jnp.min(bmin, axis=1, keepdims=True), (1, DA_DV))

    q = group_norm(q_ref[...].astype(F32), qg_ref[...]) * (DA_DH ** -0.5 * LOG2E)
    qcat = jnp.concatenate([jnp.where(lo_mask, q, 0.0), jnp.where(lo_mask, 0.0, q)], axis=0).astype(BF16)
    acc_s[...] = jnp.zeros_like(acc_s)

    q2 = q * q * kst[0:1, :]
    ones8 = jnp.ones((8, DA_DV), BF16)
    bmax = kst[1:2, 0:1]
    bmin = kst[2:3, 0:1]
    bound = []
    for m in range(2):
        q2m = jnp.where(lo_mask, q2, 0.0) if m == 0 else jnp.where(lo_mask, 0.0, q2)
        bound.append(jnp.sqrt(_dot_nt(ones8, q2m.astype(BF16))[0:1, :]) * BOUND_SLACK)
    bound = jnp.concatenate(bound, axis=1)
    worst = jnp.max(2.0 * bound, axis=1, keepdims=True) + bmax - bmin
    safe = worst[0, 0] <= MAX_SHIFT_GAP

    def block(j, d, fixed_shift):
        r0 = pl.multiple_of(j * tk, tk)
        st = _dot_nt(kn[pl.ds(r0, tk), :], qcat)
        if d is not None:
            bias = bias_ref[0, d]
            st = st + jnp.concatenate([bias, bias], axis=1)
        if fixed_shift:
            acc_s[...] = acc_s[...] + _dot(vt[j], jnp.exp2(st - m_s[...]).astype(BF16))
        else:
            m_prev = m_s[...]
            m_new = jnp.maximum(m_prev, jnp.max(st, axis=0, keepdims=True))
            alpha = jnp.exp2(m_prev - m_new)
            acc_s[...] = alpha * acc_s[...] + _dot(vt[j], jnp.exp2(st - m_new).astype(BF16))
            m_s[...] = m_new

    def run(fixed_shift):
        n_far = jnp.maximum(qi - 1, 0)

        def far_pair(jj, carry):
            block(2 * jj, None, fixed_shift)
            block(2 * jj + 1, None, fixed_shift)
            return carry

        lax.fori_loop(0, n_far // 2, far_pair, 0)

        @pl.when(n_far % 2 == 1)
        def _():
            block(n_far - 1, None, fixed_shift)

        @pl.when(qi >= 1)
        def _():
            block(qi - 1, 1, fixed_shift)

        block(qi, 0, fixed_shift)

    @pl.when(safe)
    def _():
        m_s[...] = bound + bmax
        run(True)

    @pl.when(jnp.logical_not(safe))
    def _():
        m_s[...] = jnp.full(m_s.shape, NEG, F32)
        run(False)

    lam_p = lam_ref[...]
    lam = (jnp.exp(jnp.sum(lam_p[0:1, :] * lam_p[1:2, :], axis=-1, keepdims=True))
           - jnp.exp(jnp.sum(lam_p[2:3, :] * lam_p[3:4, :], axis=-1, keepdims=True)) + LAM_INIT)
    a0 = acc_s[:, 0:tq]
    a1 = acc_s[:, tq:2 * tq]
    ot = a0[0:DA_DV] / a0[DA_DV:DA_DV + 1] - lam * (a1[0:DA_DV] / a1[DA_DV:DA_DV + 1])
    r = lax.rsqrt(jnp.sum(ot * ot, axis=0, keepdims=True) * (1.0 / DA_DV) + EPS)
    ot = ot * r * (sg_ref[...] * (1.0 - LAM_INIT))
    o_ref[...] = ot.T.astype(o_ref.dtype)


def _attention(p, bias, qg, kg, lam_p, subln, batch, seq):
    t = batch * seq
    tq = min(ATT_BLOCK, seq)
    nq = seq // tq
    dv = DA_DV
    return pl.pallas_call(
        _attn_body,
        grid=(batch, DA_HEADS, nq),
        in_specs=[
            pl.BlockSpec((tq, dv), lambda b, h, i: (b * nq + i, P_DAQ // dv + h)),
            pl.BlockSpec((seq, dv), lambda b, h, i: (b, P_DAK // dv + h)),
            pl.BlockSpec((seq, dv), lambda b, h, i: (b, P_DAV // dv + h)),
            pl.BlockSpec((1, 2, tq, tq), lambda b, h, i: (h, 0, 0, 0)),
            pl.BlockSpec((1, dv), lambda b, h, i: (0, 0)),
            pl.BlockSpec((1, dv), lambda b, h, i: (0, 0)),
            pl.BlockSpec((4, DA_DH), lambda b, h, i: (0, 0)),
            pl.BlockSpec((dv, 1), lambda b, h, i: (0, 0)),
        ],
        out_specs=pl.BlockSpec((tq, dv), lambda b, h, i: (b * nq + i, h)),
        out_shape=jax.ShapeDtypeStruct((t, DA_HEADS * dv), BF16),
        scratch_shapes=[
            pltpu.VMEM((seq, dv), BF16),
            pltpu.VMEM((seq // tq, DA_VROWS, tq), BF16),
            pltpu.VMEM((8, dv), F32),
            pltpu.VMEM((1, 2 * tq), F32),
            pltpu.VMEM((DA_VROWS, 2 * tq), F32),
        ],
        compiler_params=pltpu.CompilerParams(dimension_semantics=("parallel", "parallel", "arbitrary")),
        name="diff_attention",
    )(p, p, p, bias, qg, kg, lam_p, subln)


def _memkv_body(mem_ref, mg_ref, w_ref, kg_ref, mk_ref, mv_ref):
    x = mem_ref[0]
    xn = x * _rms(x, D_MODEL) * mg_ref[...]
    kv = _dot(xn.astype(BF16), w_ref[...])
    hw = MX_HEADS * MX_DH
    for h in range(MX_HEADS):
        sl = slice(h * MX_DH, (h + 1) * MX_DH)
        kh = kv[:, sl]
        mk_ref[0, :, sl] = (kh * _rms(kh, MX_DH) * kg_ref[...]).astype(BF16)
    mv_ref[0] = kv[:, hw:].astype(BF16)


def _memkv(mem, mem_norm, w_kv, k_norm):
    b, n, _ = mem.shape
    hw = MX_HEADS * MX_DH
    return pl.pallas_call(
        _memkv_body,
        grid=(b,),
        in_specs=[
            pl.BlockSpec((1, n, D_MODEL), lambda i: (i, 0, 0)),
            pl.BlockSpec((1, D_MODEL), lambda i: (0, 0)),
            pl.BlockSpec((D_MODEL, 2 * hw), lambda i: (0, 0)),
            pl.BlockSpec((1, MX_DH), lambda i: (0, 0)),
        ],
        out_specs=[pl.BlockSpec((1, n, hw), lambda i: (i, 0, 0))] * 2,
        out_shape=[jax.ShapeDtypeStruct((b, n, hw), BF16)] * 2,
        name="memory_kv",
    )(mem, mem_norm, w_kv, k_norm)


def _merge_body(x_ref, odn_ref, oda_ref, mxq_ref, g0_ref, g1_ref, g2_ref, mk_ref, mv_ref, qg_ref,
                bg_ref, wb_ref, wo_ref, out_ref, omx):
    for h in range(MX_HEADS):
        sl = slice(h * MX_DH, (h + 1) * MX_DH)
        qh = mxq_ref[:, sl].astype(F32)
        qh = qh * _rms(qh, MX_DH) * qg_ref[...] * (MX_DH ** -0.5 * LOG2E)
        s = _dot_nt(qh.astype(BF16), mk_ref[0, :, sl])
        p = jnp.exp2(s - jnp.max(s, axis=-1, keepdims=True))
        oh = _dot(p.astype(BF16), mv_ref[0, :, sl]) / jnp.sum(p, axis=-1, keepdims=True)
        omx[:, sl] = oh.astype(BF16)
    y = None
    for r, (o_r, g_r) in enumerate(((odn_ref, g0_ref), (oda_ref, g1_ref), (omx, g2_ref))):
        gate = _sigmoid(g_r[...].astype(F32) + bg_ref[r:r + 1, :])
        term = gate * _dot(o_r[...], wb_ref[r])
        y = term if y is None else y + term
    out_ref[...] = x_ref[...] + _dot(y.astype(BF16), wo_ref[...])


def _merge(x2, o_dn, o_da, p, mk, mv, q_norm, b_gate, w_branch, w_out, seq):
    t = x2.shape[0]
    tm = min(512, seq)
    nt = seq // tm
    bw = 512
    n_mem = mk.shape[1]
    return pl.pallas_call(
        _merge_body,
        grid=(t // tm,),
        in_specs=[
            pl.BlockSpec((tm, D_MODEL), lambda i: (i, 0)),
            pl.BlockSpec((tm, bw), lambda i: (i, 0)),
            pl.BlockSpec((tm, bw), lambda i: (i, 0)),
            pl.BlockSpec((tm, bw), lambda i: (i, P_MXQ // bw)),
            pl.BlockSpec((tm, D_MODEL), lambda i: (i, P_GATE // D_MODEL)),
            pl.BlockSpec((tm, D_MODEL), lambda i: (i, P_GATE // D_MODEL + 1)),
            pl.BlockSpec((tm, D_MODEL), lambda i: (i, P_GATE // D_MODEL + 2)),
            pl.BlockSpec((1, n_mem, bw), lambda i: (i // nt, 0, 0)),
            pl.BlockSpec((1, n_mem, bw), lambda i: (i // nt, 0, 0)),
            pl.BlockSpec((1, MX_DH), lambda i: (0, 0)),
            pl.BlockSpec((3, D_MODEL), lambda i: (0, 0)),
            pl.BlockSpec((3, bw, D_MODEL), lambda i: (0, 0, 0)),
            pl.BlockSpec((D_MODEL, D_MODEL), lambda i: (0, 0)),
        ],
        out_specs=pl.BlockSpec((tm, D_MODEL), lambda i: (i, 0)),
        out_shape=jax.ShapeDtypeStruct((t, D_MODEL), F32),
        scratch_shapes=[pltpu.VMEM((tm, bw), BF16)],
        compiler_params=pltpu.CompilerParams(dimension_semantics=("parallel",)),
        name="merge",
    )(x2, o_dn, o_da, p, p, p, p, mk, mv, q_norm, b_gate, w_branch, w_out)


def _router_body(x_ref, g_ref, wr_ref, br_ref, h_ref, idx_ref, wt_ref, rank_ref, cnt_ref):
    tm = x_ref.shape[0]
    x = x_ref[...]
    h = x * _rms(x, D_MODEL) * g_ref[...]
    h_ref[...] = h.astype(BF16)
    logits = _dot3(h, wr_ref[...]) + br_ref[...]
    lane = lax.broadcasted_iota(I32, (tm, LANES), 1)
    lane_f = lane.astype(F32)
    work = logits
    sel = jnp.zeros((tm, LANES), F32)
    vals, idxs = [], []
    for _ in range(TOP_K):
        mx = jnp.max(work, axis=-1, keepdims=True)
        ik = jnp.min(jnp.where(work == mx, lane_f, float(LANES)), axis=-1, keepdims=True)
        hit = lane_f == ik
        sel = jnp.where(hit, 1.0, sel)
        work = jnp.where(hit, -jnp.inf, work)
        vals.append(mx)
        idxs.append(ik)
    es = [jnp.exp(v - vals[0]) for v in vals]
    den = es[0] + es[1] + es[2] + es[3]
    r = lax.broadcasted_iota(I32, (tm, tm), 0)
    c = lax.broadcasted_iota(I32, (tm, tm), 1)
    tril = jnp.where(r > c, 1.0, 0.0).astype(BF16)
    cum = _dot(tril, sel.astype(BF16))
    idx_o = jnp.zeros((tm, LANES), F32)
    wt_o = jnp.zeros((tm, LANES), F32)
    rank_o = jnp.zeros((tm, LANES), F32)
    for k in range(TOP_K):
        rk = jnp.sum(jnp.where(lane_f == idxs[k], cum, 0.0), axis=-1, keepdims=True)
        idx_o = jnp.where(lane == k, idxs[k], idx_o)
        wt_o = jnp.where(lane == k, es[k] / den, wt_o)
        rank_o = jnp.where(lane == k, rk, rank_o)
    idx_ref[...] = idx_o.astype(I32)
    wt_ref[...] = wt_o
    rank_ref[...] = rank_o.astype(I32)
    cnt_ref[0] = jnp.broadcast_to(jnp.sum(sel, axis=0, keepdims=True), (8, LANES))


def _router(x1, gain, w_r, b_r):
    t = x1.shape[0]
    tm = min(MOE_TILE, t)
    row = lambda i: (i, 0)
    fixed = lambda i: (0, 0)
    return pl.pallas_call(
        _router_body,
        grid=(t // tm,),
        in_specs=[
            pl.BlockSpec((tm, D_MODEL), row),
            pl.BlockSpec((1, D_MODEL), fixed),
            pl.BlockSpec((D_MODEL, LANES), fixed),
            pl.BlockSpec((1, LANES), fixed),
        ],
        out_specs=[
            pl.BlockSpec((tm, D_MODEL), row),
            pl.BlockSpec((tm, LANES), row),
            pl.BlockSpec((tm, LANES), row),
            pl.BlockSpec((tm, LANES), row),
            pl.BlockSpec((1, 8, LANES), lambda i: (i, 0, 0)),
        ],
        out_shape=[
            jax.ShapeDtypeStruct((t, D_MODEL), BF16),
            jax.ShapeDtypeStruct((t, LANES), I32),
            jax.ShapeDtypeStruct((t, LANES), F32),
            jax.ShapeDtypeStruct((t, LANES), I32),
            jax.ShapeDtypeStruct((t // tm, 8, LANES), F32),
        ],
        compiler_params=pltpu.CompilerParams(dimension_semantics=("arbitrary",)),
        name="router",
    )(x1, gain, w_r, b_r)


def _lane_cumsum(x):
    lane = lax.broadcasted_iota(I32, x.shape, 1)
    s = 1
    while s < N_EXPERTS:
        x = x + jnp.where(lane >= s, pltpu.roll(x, s, axis=1), 0.0)
        s *= 2
    return x


def _plan_body(cnt_ref, gs_ref, ls_ref, n8_ref, lsf_ref, tail_ref, meta_ref):
    nt = cnt_ref.shape[0]
    ga = float(GROUP_ALIGN)
    eb = float(EXPERT_BLOCK)
    lane = lax.broadcasted_iota(I32, (nt, LANES), 1)
    r8 = jnp.where(lane < N_EXPERTS, jnp.floor((cnt_ref[...] + (ga - 1.0)) * (1.0 / ga)) * ga, 0.0)
    ri = lax.broadcasted_iota(I32, (nt, nt), 0)
    ci = lax.broadcasted_iota(I32, (nt, nt), 1)
    before = _dot(jnp.where(ri > ci, 1.0, 0.0).astype(BF16), r8.astype(BF16))
    tot = jnp.sum(r8, axis=0, keepdims=True)
    region = jnp.floor((tot + (eb - 1.0)) * (1.0 / eb)) * eb
    pends = _lane_cumsum(jnp.broadcast_to(region, (8, LANES)))[0:1, :]
    pstart = pends - region
    lstart = _lane_cumsum(r8) - r8
    gs_ref[...] = (pstart + before).astype(I32)
    ls_ref[...] = lstart.astype(I32)
    n8_ref[...] = (r8 * (1.0 / ga)).astype(I32)
    lsf_ref[...] = lstart
    row8 = lax.broadcasted_iota(I32, (8, LANES), 0)
    tail = jnp.where(row8 == 0, pstart + tot, jnp.where(row8 == 1, (region - tot) * (1.0 / ga), 0.0))
    tail_ref[...] = tail.astype(I32)
    nb = meta_ref.shape[0]
    ln = lax.broadcasted_iota(I32, (nb, LANES), 1)
    blk = lax.broadcasted_iota(I32, (nb, LANES), 0).astype(F32) * eb
    be = jnp.sum(jnp.where((ln < N_EXPERTS) & (pends <= blk), 1.0, 0.0), axis=-1, keepdims=True)
    be = jnp.minimum(be, float(N_EXPERTS - 1))
    used = jnp.sum(jnp.where(ln == N_EXPERTS - 1, pends, 0.0), axis=-1, keepdims=True) * (1.0 / eb)
    meta_ref[...] = jnp.where(ln == 0, be, jnp.where(ln == 1, used, 0.0)).astype(I32)


def _plan(cnt, n_blocks_pad):
    nt = cnt.shape[0]
    shp = jax.ShapeDtypeStruct((nt, LANES), I32)
    return pl.pallas_call(
        _plan_body,
        out_shape=[shp, shp, shp, jax.ShapeDtypeStruct((nt, LANES), F32),
                   jax.ShapeDtypeStruct((8, LANES), I32),
                   jax.ShapeDtypeStruct((n_blocks_pad, LANES), I32)],
        name="dispatch_plan",
    )(cnt)


def _local_positions(idx_ref, rank_ref, lsf_ref):
    tm = idx_ref.shape[0]
    lane = lax.broadcasted_iota(I32, (tm, LANES), 1)
    idx = idx_ref[...]
    rank = rank_ref[...].astype(F32)
    ls_row = lsf_ref[0]
    out = []
    for k in range(TOP_K):
        base = jnp.sum(jnp.where(lane == idx[:, k:k + 1], ls_row, 0.0), axis=-1, keepdims=True)
        out.append((base + rank[:, k:k + 1]).astype(I32))
    return out


def _local_positions_lanes(idx_ref, rank_ref, lsf_ref):
    tm = idx_ref.shape[0]
    idx_t = idx_ref[...].astype(F32).T
    rank_t = rank_ref[...].astype(F32).T
    ls_col = jnp.broadcast_to(lsf_ref[0], (LANES, LANES)).T[:, 0:1]
    expert = lax.broadcasted_iota(I32, (LANES, tm), 0).astype(F32)
    out = []
    for k in range(TOP_K):
        base = jnp.sum(jnp.where(expert == idx_t[k:k + 1, :], ls_col, 0.0), axis=0, keepdims=True)
        out.append((base + rank_t[k:k + 1, :]).astype(I32))
    return out


def _group_copies(gs_ref, ls_ref, n8_ref, make_copy, start):
    def group(e, carry):
        g0 = gs_ref[0, 0, e]
        l0 = ls_ref[0, 0, e]

        def one(j, c):
            cp = make_copy(pl.multiple_of(l0 + j * GROUP_ALIGN, GROUP_ALIGN),
                           pl.multiple_of(g0 + j * GROUP_ALIGN, GROUP_ALIGN))
            cp.start() if start else cp.wait()
            return c

        lax.fori_loop(0, n8_ref[0, 0, e], one, 0)
        return carry

    lax.fori_loop(0, N_EXPERTS, group, 0)


def _dispatch_body(gs_ref, ls_ref, n8_ref, gsp_ref, lsp_ref, n8p_ref, tail_ref, h_ref, idx_ref, rank_ref,
                   lsf_ref, xs_ref, xl, zbuf, sem):
    i = pl.program_id(0)
    last = i == pl.num_programs(0) - 1
    slot = i % 2
    tm = h_ref.shape[0]
    lrows = xl.shape[1]
    pos = lax.broadcasted_iota(I32, (lrows, tm), 0)
    lp = _local_positions_lanes(idx_ref, rank_ref, lsf_ref)
    hit = pos == lp[0]
    for k in range(1, TOP_K):
        hit = hit | (pos == lp[k])
    xl[slot] = _dot(jnp.where(hit, 1.0, 0.0).astype(BF16), h_ref[...])

    def copy_from(s):
        def make_copy(l, g):
            return pltpu.make_async_copy(xl.at[s, pl.ds(l, GROUP_ALIGN), :],
                                         xs_ref.at[pl.ds(g, GROUP_ALIGN), :], sem.at[s])
        return make_copy

    @pl.when(i >= 1)
    def _():
        _group_copies(gsp_ref, lsp_ref, n8p_ref, copy_from(1 - slot), False)

    _group_copies(gs_ref, ls_ref, n8_ref, copy_from(slot), True)

    def tails(start):
        def per_expert(e, carry):
            g0 = tail_ref[0, e]

            def one(j, c):
                cp = pltpu.make_async_copy(
                    zbuf, xs_ref.at[pl.ds(pl.multiple_of(g0 + j * GROUP_ALIGN, GROUP_ALIGN), GROUP_ALIGN), :],
                    sem.at[2])
                cp.start() if start else cp.wait()
                return c

            lax.fori_loop(0, tail_ref[1, e], one, 0)
            return carry

        lax.fori_loop(0, N_EXPERTS, per_expert, 0)

    @pl.when(last)
    def _():
        zbuf[...] = jnp.zeros_like(zbuf)
        tails(True)
        _group_copies(gs_ref, ls_ref, n8_ref, copy_from(slot), False)
        tails(False)


def _smem_row(shift=0):
    return pl.BlockSpec((1, 1, LANES),
                        lambda i: (jnp.clip(i + shift, 0, pl.num_programs(0) - 1), 0, 0),
                        memory_space=pltpu.SMEM)


def _dispatch(gs, ls, n8, tail, h2, idx, rank, lsf, n_slots):
    t = h2.shape[0]
    tm = min(MOE_TILE, t)
    nt = t // tm
    lrows = tm * TOP_K + N_EXPERTS * GROUP_ALIGN
    row = lambda i: (i, 0)
    return pl.pallas_call(
        _dispatch_body,
        grid=(nt,),
        in_specs=[
            _smem_row(), _smem_row(), _smem_row(),
            _smem_row(-1), _smem_row(-1), _smem_row(-1),
            pl.BlockSpec(memory_space=pltpu.SMEM),
            pl.BlockSpec((tm, D_MODEL), row),
            pl.BlockSpec((tm, LANES), row),
            pl.BlockSpec((tm, LANES), row),
            pl.BlockSpec((1, 1, LANES), lambda i: (i, 0, 0)),
        ],
        out_specs=pl.BlockSpec(memory_space=pl.ANY),
        out_shape=jax.ShapeDtypeStruct((n_slots, D_MODEL), F32),
        scratch_shapes=[pltpu.VMEM((2, lrows, D_MODEL), F32), pltpu.VMEM((GROUP_ALIGN, D_MODEL), F32),
                        pltpu.SemaphoreType.DMA((3,))],
        compiler_params=pltpu.CompilerParams(dimension_semantics=("arbitrary",)),
        name="moe_dispatch",
    )(gs, ls, n8, gs, ls, n8, tail, h2, idx, rank, lsf)


def _expert_body(be_ref, nu_ref, x_ref, wgu_ref, bgu_ref, wd_ref, bd_ref, y_ref, wgu_b, wd_b):
    j = pl.program_id(0)

    @pl.when(j < nu_ref[0])
    def _():
        @pl.when(jnp.logical_or(j == 0, be_ref[j] != be_ref[jnp.maximum(j - 1, 0)]))
        def _():
            wgu_b[...] = wgu_ref[0].astype(BF16)
            wd_b[...] = wd_ref[0].astype(BF16)

        gu = _dot(x_ref[...].astype(BF16), wgu_b[...]) + bgu_ref[0]
        gate = jnp.minimum(gu[:, :D_FF], SWIGLU_LIMIT)
        up = jnp.clip(gu[:, D_FF:], -SWIGLU_LIMIT, SWIGLU_LIMIT)
        act = (up + 1.0) * gate * _sigmoid(SWIGLU_ALPHA * gate)
        y_ref[...] = _dot(act.astype(BF16), wd_b[...]) + bd_ref[0]


def _experts(block_e, n_used, xs, wgu, bgu, wd, bd):
    n_slots = xs.shape[0]
    nb = n_slots // EXPERT_BLOCK

    def blk(j, be, nu):
        return jnp.minimum(j, nu[0] - 1)

    grid_spec = pltpu.PrefetchScalarGridSpec(
        num_scalar_prefetch=2,
        grid=(nb,),
        in_specs=[
            pl.BlockSpec((EXPERT_BLOCK, D_MODEL), lambda j, be, nu: (blk(j, be, nu), 0)),
            pl.BlockSpec((1, D_MODEL, 2 * D_FF), lambda j, be, nu: (be[blk(j, be, nu)], 0, 0)),
            pl.BlockSpec((1, 1, 2 * D_FF), lambda j, be, nu: (be[blk(j, be, nu)], 0, 0)),
            pl.BlockSpec((1, D_FF, D_MODEL), lambda j, be, nu: (be[blk(j, be, nu)], 0, 0)),
            pl.BlockSpec((1, 1, D_MODEL), lambda j, be, nu: (be[blk(j, be, nu)], 0, 0)),
        ],
        out_specs=pl.BlockSpec((EXPERT_BLOCK, D_MODEL), lambda j, be, nu: (blk(j, be, nu), 0)),
        scratch_shapes=[pltpu.VMEM((D_MODEL, 2 * D_FF), BF16), pltpu.VMEM((D_FF, D_MODEL), BF16)],
    )
    return pl.pallas_call(
        _expert_body,
        grid_spec=grid_spec,
        out_shape=jax.ShapeDtypeStruct((n_slots, D_MODEL), F32),
        compiler_params=pltpu.CompilerParams(dimension_semantics=("arbitrary",)),
        name="moe_experts",
    )(block_e, n_used, xs, wgu, bgu, wd, bd)


def _combine_body(gs_ref, ls_ref, n8_ref, gsn_ref, lsn_ref, n8n_ref, x_ref, wt_ref, idx_ref, rank_ref,
                  lsf_ref, y_ref, out_ref, yl, sem):
    i = pl.program_id(0)
    slot = i % 2
    tm = x_ref.shape[0]
    lrows = yl.shape[1]

    def copy_into(s):
        def make_copy(l, g):
            return pltpu.make_async_copy(y_ref.at[pl.ds(g, GROUP_ALIGN), :],
                                         yl.at[s, pl.ds(l, GROUP_ALIGN), :], sem.at[s])
        return make_copy

    @pl.when(i == 0)
    def _():
        yl[...] = jnp.zeros_like(yl)
        _group_copies(gs_ref, ls_ref, n8_ref, copy_into(slot), True)

    @pl.when(i + 1 < pl.num_programs(0))
    def _():
        _group_copies(gsn_ref, lsn_ref, n8n_ref, copy_into(1 - slot), True)

    pos = lax.broadcasted_iota(I32, (tm, lrows), 1)
    lp = _local_positions(idx_ref, rank_ref, lsf_ref)
    wt = wt_ref[...]
    wm = jnp.zeros((tm, lrows), F32)
    for k in range(TOP_K):
        wm = jnp.where(pos == lp[k], wt[:, k:k + 1], wm)
    _group_copies(gs_ref, ls_ref, n8_ref, copy_into(slot), False)
    out_ref[...] = x_ref[...] + _dot(wm.astype(BF16), yl[slot].astype(BF16))


def _combine(gs, ls, n8, x1, wts, idx, rank, lsf, y):
    t = x1.shape[0]
    tm = min(MOE_TILE, t)
    nt = t // tm
    lrows = tm * TOP_K + N_EXPERTS * GROUP_ALIGN
    row = lambda i: (i, 0)
    return pl.pallas_call(
        _combine_body,
        grid=(nt,),
        in_specs=[
            _smem_row(), _smem_row(), _smem_row(),
            _smem_row(1), _smem_row(1), _smem_row(1),
            pl.BlockSpec((tm, D_MODEL), row),
            pl.BlockSpec((tm, LANES), row),
            pl.BlockSpec((tm, LANES), row),
            pl.BlockSpec((tm, LANES), row),
            pl.BlockSpec((1, 1, LANES), lambda i: (i, 0, 0)),
            pl.BlockSpec(memory_space=pl.ANY),
        ],
        out_specs=pl.BlockSpec((tm, D_MODEL), row),
        out_shape=jax.ShapeDtypeStruct((t, D_MODEL), F32),
        scratch_shapes=[pltpu.VMEM((2, lrows, D_MODEL), F32), pltpu.SemaphoreType.DMA((2,))],
        compiler_params=pltpu.CompilerParams(dimension_semantics=("arbitrary",)),
        name="moe_combine",
    )(gs, ls, n8, gs, ls, n8, x1, wts, idx, rank, lsf, y)


def _pad_lanes(v, fill=0.0):
    v = v.astype(F32).reshape(1, -1)
    return jnp.pad(v, ((0, 0), (0, LANES - v.shape[1])), constant_values=fill)


def _mixer(x2, mem, rel_table, attn_norm, w_in, b_gate, dn_conv, dn_a_log, dn_dt_bias, dn_out_norm,
           da_q_norm, da_k_norm, da_lambda, da_subln, mem_norm, w_mem_kv, mx_q_norm, mx_k_norm,
           w_branch, w_out, batch, seq):
    wp = jnp.concatenate([w_in[:, :W_AB_LO], w_in[:, W_AB_HI:]], axis=1).astype(BF16)
    wab = jnp.pad(w_in[:, W_AB_LO:W_AB_HI], ((0, 0), (0, LANES - (W_AB_HI - W_AB_LO))))
    p, ab = _inproj(x2, attn_norm.reshape(1, -1), wp, wab)

    o_dn = _deltanet(p, ab, dn_conv, _pad_lanes(dn_a_log), _pad_lanes(dn_dt_bias),
                     dn_out_norm.reshape(1, -1), batch, seq)

    tq = min(ATT_BLOCK, seq)
    bias = _bias_tiles(rel_table.T, tq)
    o_da = _attention(p, bias, jnp.tile(da_q_norm, 2).reshape(1, -1), jnp.tile(da_k_norm, 2).reshape(1, -1),
                      da_lambda, da_subln.reshape(-1, 1), batch, seq)

    mk, mv = _memkv(mem, mem_norm.reshape(1, -1), w_mem_kv.astype(BF16), mx_k_norm.reshape(1, -1))
    return _merge(x2, o_dn, o_da, p, mk, mv, mx_q_norm.reshape(1, -1), b_gate.reshape(3, D_MODEL),
                  w_branch.astype(BF16), w_out.astype(BF16), seq)


def _moe(x1, ffn_norm, w_router, b_router, w_gate_up, b_gate_up, w_down, b_down):
    t = x1.shape[0]
    nt = t // min(MOE_TILE, t)
    max_rows = t * TOP_K + nt * N_EXPERTS * (GROUP_ALIGN - 1)
    n_blocks = -(-max_rows // EXPERT_BLOCK) + N_EXPERTS
    n_blocks_pad = -(-n_blocks // 8) * 8
    n_slots = n_blocks * EXPERT_BLOCK

    wr = jnp.pad(w_router, ((0, 0), (0, LANES - N_EXPERTS)))
    h2, idx, wts, rank, cnt = _router(x1, ffn_norm.reshape(1, -1), wr, _pad_lanes(b_router, NEG))
    gs, ls, n8, lsf, tail, meta = _plan(cnt[:, 0, :], n_blocks_pad)
    block_e = meta[:n_blocks, 0]
    n_used = meta[0:1, 1]
    gs, ls, n8, lsf = (a.reshape(nt, 1, LANES) for a in (gs, ls, n8, lsf))

    xs = _dispatch(gs, ls, n8, tail, h2, idx, rank, lsf, n_slots)
    y = _experts(block_e, n_used, xs, w_gate_up, b_gate_up.reshape(N_EXPERTS, 1, -1),
                 w_down, b_down.reshape(N_EXPERTS, 1, -1))
    return _combine(gs, ls, n8, x1, wts, idx, rank, lsf, y)


def kernel(x, mem, rel_table, attn_norm, w_in, b_gate, dn_conv, dn_a_log, dn_dt_bias, dn_out_norm,
           da_q_norm, da_k_norm, da_lambda, da_subln, mem_norm, w_mem_kv, mx_q_norm, mx_k_norm,
           w_branch, w_out, ffn_norm, w_router, b_router, w_gate_up, b_gate_up, w_down, b_down):
    batch, seq, d = x.shape
    x2 = x.reshape(batch * seq, d)
    x1 = _mixer(x2, mem, rel_table, attn_norm[0], w_in[0], b_gate[0], dn_conv[0], dn_a_log[0],
                dn_dt_bias[0], dn_out_norm[0], da_q_norm[0], da_k_norm[0], da_lambda[0], da_subln[0],
                mem_norm[0], w_mem_kv[0], mx_q_norm[0], mx_k_norm[0], w_branch[0], w_out[0], batch, seq)
    out = _moe(x1, ffn_norm[0], w_router[0], b_router[0], w_gate_up[0], b_gate_up[0], w_down[0],
               b_down[0])
    return out.reshape(batch, seq, d)
```

```python
import functools
import math

import jax
import jax.numpy as jnp
from jax import lax
from jax.experimental import pallas as pl
from jax.experimental.pallas import tpu as pltpu

F32 = jnp.float32
BF16 = jnp.bfloat16
I32 = jnp.int32

D_MODEL = 1024
EPS = 1e-6
LANES = 128

DN_HEADS = 4
DN_DK = 128
DN_CHUNK = 64
DN_CONV = 4

DA_HEADS = 4
DA_DH = 64

MX_HEADS = 4
MX_DH = 128

REL_BUCKETS = 32
REL_MAX_DIST = 128

N_EXPERTS = 32
TOP_K = 4
D_FF = 1024
SWIGLU_LIMIT = 7.0
SWIGLU_ALPHA = 1.702
EXPERT_BLOCK = 512
MOE_TILE = 256
GROUP_ALIGN = 8

LAM_INIT = 0.8 - 0.6 * math.exp(-0.3 * 0)
LOG2E = 1.4426950408889634
NEG = -1e30

P_DNQ, P_DNK, P_DNV, P_DNZ = 0, 512, 1024, 1536
P_DAQ, P_DAK, P_DAV = 2048, 2560, 3072
P_MXQ = 3584
P_GATE = 4096
P_COLS = 7168
W_AB_LO, W_AB_HI = 2048, 2056


def _dot(a, b):
    return jnp.dot(a, b, preferred_element_type=F32)


def _dot_nt(a, b):
    return lax.dot_general(a, b, (((1,), (1,)), ((), ())), preferred_element_type=F32)


def _dot_tn(a, b):
    return lax.dot_general(a, b, (((0,), (0,)), ((), ())), preferred_element_type=F32)


def _split(x):
    hi = x.astype(BF16)
    lo = (x - hi.astype(F32)).astype(BF16)
    return hi, lo


def _dot3(a, b):
    ah, al = _split(a)
    bh, bl = _split(b)
    return _dot(ah, bh) + _dot(ah, bl) + _dot(al, bh)


def _sigmoid(x):
    return 1.0 / (1.0 + jnp.exp(-x))


def _rms(x, n):
    return lax.rsqrt(jnp.sum(x * x, axis=-1, keepdims=True) * (1.0 / n) + EPS)


def _inproj_body(x_ref, g_ref, w_ref, wab_ref, p_ref, ab_ref, h_scr):
    @pl.when(pl.program_id(1) == 0)
    def _():
        x = x_ref[...]
        h = x * _rms(x, D_MODEL) * g_ref[...]
        h_scr[...] = h.astype(BF16)
        ab_ref[...] = _dot3(h, wab_ref[...])

    p_ref[...] = _dot(h_scr[...], w_ref[...]).astype(p_ref.dtype)


def _inproj(x2, gain, wp, wab):
    t = x2.shape[0]
    tm = min(1024, t)
    tn = 1024
    return pl.pallas_call(
        _inproj_body,
        grid=(t // tm, P_COLS // tn),
        in_specs=[
            pl.BlockSpec((tm, D_MODEL), lambda i, j: (i, 0)),
            pl.BlockSpec((1, D_MODEL), lambda i, j: (0, 0)),
            pl.BlockSpec((D_MODEL, tn), lambda i, j: (0, j)),
            pl.BlockSpec((D_MODEL, LANES), lambda i, j: (0, 0)),
        ],
        out_specs=[
            pl.BlockSpec((tm, tn), lambda i, j: (i, j)),
            pl.BlockSpec((tm, LANES), lambda i, j: (i, 0)),
        ],
        out_shape=[
            jax.ShapeDtypeStruct((t, P_COLS), BF16),
            jax.ShapeDtypeStruct((t, LANES), F32),
        ],
        scratch_shapes=[pltpu.VMEM((tm, D_MODEL), BF16)],
        compiler_params=pltpu.CompilerParams(dimension_semantics=("parallel", "arbitrary")),
        name="inproj",
    )(x2, gain, wp, wab)


DN_HALO = 16
DN_SCAN_CHUNK = 256


def _deltanet_body(q_ref, k_ref, v_ref, z_ref, qh_ref, kh_ref, vh_ref, ab_ref, cw_ref, alog_ref,
                   dtb_ref, on_ref, o_ref, stage, qs, ks, vs, s_scr):
    i = pl.program_id(1)
    tc = q_ref.shape[0]
    hw = DN_HEADS * DN_DK

    @pl.when(i == 0)
    def _():
        s_scr[...] = jnp.zeros_like(s_scr)

    for src, halo, dst, off, kind in ((q_ref, qh_ref, qs, 0, "q"), (k_ref, kh_ref, ks, hw, "k"),
                                      (v_ref, vh_ref, vs, 2 * hw, "v")):
        hal = halo[...].astype(F32)
        stage[0:DN_HALO, :] = jnp.where(i == 0, 0.0, hal)
        stage[DN_HALO:DN_HALO + tc, :] = src[...].astype(F32)
        base = DN_HALO - (DN_CONV - 1)
        y = stage[base:base + tc, :] * cw_ref[0:1, off:off + hw]
        for j in range(1, DN_CONV):
            y = y + stage[base + j:base + j + tc, :] * cw_ref[j:j + 1, off:off + hw]
        y = y * _sigmoid(y)
        if kind == "v":
            dst[...] = y
        else:
            for h in range(DN_HEADS):
                sl = slice(h * DN_DK, (h + 1) * DN_DK)
                yh = y[:, sl]
                r = lax.rsqrt(jnp.sum(yh * yh, axis=-1, keepdims=True) + EPS)
                if kind == "q":
                    r = r * (DN_DK ** -0.5)
                dst[:, sl] = yh * r

    c = min(DN_SCAN_CHUNK, tc)
    row = lax.broadcasted_iota(I32, (c, c), 0)
    col = lax.broadcasted_iota(I32, (c, c), 1)
    incl = row >= col
    strict = row > col
    same_blk = (row // DN_CHUNK) == (col // DN_CHUNK)
    tri = jnp.where(incl, 1.0, 0.0).astype(BF16)
    eye = jnp.where(row == col, 1.0, 0.0)
    neg_a = -jnp.exp(alog_ref[...])
    dtb = dtb_ref[...]

    def chunk(ci, carry):
        r0 = pl.multiple_of(ci * c, c)
        abc = ab_ref[pl.ds(r0, c), :]
        a_in = abc + dtb
        g_all = neg_a * (jnp.maximum(a_in, 0.0) + jnp.log(1.0 + jnp.exp(-jnp.abs(a_in))))
        beta_all = _sigmoid(abc)
        zc = z_ref[pl.ds(r0, c), :].astype(F32)
        for h in range(DN_HEADS):
            sl = slice(h * DN_DK, (h + 1) * DN_DK)
            q = qs[pl.ds(r0, c), sl]
            k = ks[pl.ds(r0, c), sl]
            v = vs[pl.ds(r0, c), sl]
            g = g_all[:, h:h + 1]
            beta = beta_all[:, DN_HEADS + h:DN_HEADS + h + 1]
            g_hi, g_lo = _split(jnp.where(strict, g, 0.0))
            diff = _dot(tri, g_hi) + _dot(tri, g_lo)
            gc = diff[:, 0:1] + g[0:1, :]
            decay = jnp.where(incl, jnp.exp(diff), 0.0)
            kb = k.astype(BF16)
            qkk = _dot_nt(jnp.concatenate([q.astype(BF16), kb], axis=0), kb)
            qk = qkk[:c]
            kk = qkk[c:]
            lower = jnp.where(strict, kk * decay * beta, 0.0)
            pw = jnp.where(same_blk, -lower, 0.0)
            dinv = eye + pw
            for _ in range(int(math.log2(DN_CHUNK)) - 1):
                pwb = pw.astype(BF16)
                pw = _dot(pwb, pwb)
                dinv = dinv + _dot(dinv.astype(BF16), pw.astype(BF16))
            dinv_b = dinv.astype(BF16)
            pw = -_dot(dinv_b, jnp.where(same_blk, 0.0, lower).astype(BF16))
            xm = eye + pw
            for _ in range(int(math.log2(c // DN_CHUNK)) - 1):
                pwb = pw.astype(BF16)
                pw = _dot(pwb, pwb)
                xm = xm + _dot(xm.astype(BF16), pw.astype(BF16))
            inv = _dot(xm.astype(BF16), dinv_b)
            egc = jnp.exp(gc)
            rhs = jnp.concatenate([v * beta, k * (beta * egc)], axis=1)
            sol = _dot(inv.astype(BF16), rhs.astype(BF16))
            u = sol[:, :DN_DK]
            w = sol[:, DN_DK:]
            qkm = jnp.where(incl, qk * decay, 0.0)
            gl = gc[c - 1:c, :]
            state = s_scr[h]
            sb = state.astype(BF16)
            ws = _dot(jnp.concatenate([w.astype(BF16), (q * egc).astype(BF16)], axis=0), sb)
            v_new = u - ws[:c]
            o = ws[c:] + _dot(qkm.astype(BF16), v_new.astype(BF16))
            s_scr[h] = state * jnp.exp(gl) + _dot_tn(kb, (v_new * jnp.exp(gl - gc)).astype(BF16))
            zz = zc[:, sl]
            on = o * _rms(o, DN_DK) * on_ref[...]
            o_ref[pl.ds(r0, c), sl] = (on * (zz * _sigmoid(zz))).astype(o_ref.dtype)
        return carry

    lax.fori_loop(0, tc // c, chunk, 0, unroll=True)


def _deltanet(p, ab, conv_w, alog_row, dtb_row, out_norm, batch, seq):
    t = batch * seq
    tc = min(512, seq)
    nt = seq // tc
    hw = DN_HEADS * DN_DK

    def main(cb):
        return pl.BlockSpec((tc, hw), lambda b, i: (b * nt + i, cb))

    def halo(cb):
        return pl.BlockSpec(
            (DN_HALO, hw),
            lambda b, i: (jnp.maximum((b * seq + i * tc) // DN_HALO - 1, 0), cb))

    return pl.pallas_call(
        _deltanet_body,
        grid=(batch, nt),
        in_specs=[
            main(P_DNQ // hw), main(P_DNK // hw), main(P_DNV // hw), main(P_DNZ // hw),
            halo(P_DNQ // hw), halo(P_DNK // hw), halo(P_DNV // hw),
            pl.BlockSpec((tc, LANES), lambda b, i: (b * nt + i, 0)),
            pl.BlockSpec((DN_CONV, 3 * hw), lambda b, i: (0, 0)),
            pl.BlockSpec((1, LANES), lambda b, i: (0, 0)),
            pl.BlockSpec((1, LANES), lambda b, i: (0, 0)),
            pl.BlockSpec((1, DN_DK), lambda b, i: (0, 0)),
        ],
        out_specs=pl.BlockSpec((tc, hw), lambda b, i: (b * nt + i, 0)),
        out_shape=jax.ShapeDtypeStruct((t, hw), BF16),
        scratch_shapes=[
            pltpu.VMEM((DN_HALO + tc, hw), F32),
            pltpu.VMEM((tc, hw), F32),
            pltpu.VMEM((tc, hw), F32),
            pltpu.VMEM((tc, hw), F32),
            pltpu.VMEM((DN_HEADS, DN_DK, DN_DK), F32),
        ],
        compiler_params=pltpu.CompilerParams(dimension_semantics=("parallel", "arbitrary")),
        name="deltanet",
    )(p, p, p, p, p, p, p, ab, conv_w, alog_row, dtb_row, out_norm)


ATT_BLOCK = 512


def _bias_body(tbl_ref, o_ref):
    h = pl.program_id(0)
    tq = o_ref.shape[2]
    key = lax.broadcasted_iota(I32, (tq, tq), 0)
    qry = lax.broadcasted_iota(I32, (tq, tq), 1)
    max_exact = REL_BUCKETS // 2
    far = tbl_ref[h, REL_BUCKETS - 1]
    for d in range(2):
        n = qry - key + d * tq
        nn = jnp.maximum(n, 0)
        nf = jnp.maximum(nn, 1).astype(F32)
        large = max_exact + (jnp.log(nf / max_exact) / math.log(REL_MAX_DIST / max_exact)
                             * (REL_BUCKETS - max_exact)).astype(I32)
        large = jnp.minimum(large, REL_BUCKETS - 1)
        bucket = jnp.where(nn < max_exact, nn, large)
        val = jnp.zeros((tq, tq), F32)
        for b in range(REL_BUCKETS):
            val = jnp.where(bucket == b, tbl_ref[h, b], val)
        o_ref[0, d] = jnp.where(n >= 0, (val - far) * LOG2E, NEG)


def _bias_tiles(tbl_t, tq):
    return pl.pallas_call(
        _bias_body,
        grid=(DA_HEADS,),
        in_specs=[pl.BlockSpec(memory_space=pltpu.SMEM)],
        out_specs=pl.BlockSpec((1, 2, tq, tq), lambda h: (h, 0, 0, 0)),
        out_shape=jax.ShapeDtypeStruct((DA_HEADS, 2, tq, tq), F32),
        name="t5_bias_tiles",
    )(tbl_t)


DA_DV = 2 * DA_DH
DA_VROWS = DA_DV + 16


BOUND_SLACK = 1.02
MAX_SHIFT_GAP = 110.0


def _attn_body(q_ref, k_ref, v_ref, bias_ref, qg_ref, kg_ref, lam_ref, sg_ref, o_ref,
               kn, vt, kst, m_s, acc_s):
    qi = pl.program_id(2)
    tq = q_ref.shape[0]
    seq = k_ref.shape[0]
    tk = tq
    lo_mask = lax.broadcasted_iota(I32, (1, DA_DV), 1) < DA_DH

    def group_norm(x, gain):
        x2 = x * x
        lo = jnp.sum(jnp.where(lo_mask, x2, 0.0), axis=-1, keepdims=True)
        hi = jnp.sum(jnp.where(lo_mask, 0.0, x2), axis=-1, keepdims=True)
        r = jnp.where(lo_mask, lax.rsqrt(lo * (1.0 / DA_DH) + EPS), lax.rsqrt(hi * (1.0 / DA_DH) + EPS))
        return x * r * gain

    @pl.when(qi == 0)
    def _():
        ones = jnp.ones((DA_VROWS - DA_DV, tk), BF16)

        def body(c, kmax2):
            r0 = pl.multiple_of(c * tk, tk)
            kb = group_norm(k_ref[pl.ds(r0, tk), :].astype(F32), kg_ref[...]).astype(BF16)
            kn[pl.ds(r0, tk), :] = kb
            vt[c, 0:DA_DV, :] = v_ref[pl.ds(r0, tk), :].astype(F32).T.astype(BF16)
            vt[c, DA_DV:DA_VROWS, :] = ones
            k2 = kb.astype(F32)
            k2 = k2 * k2
            lo = jnp.max(jnp.sum(jnp.where(lo_mask, k2, 0.0), axis=-1, keepdims=True), axis=0, keepdims=True)
            hi = jnp.max(jnp.sum(jnp.where(lo_mask, 0.0, k2), axis=-1, keepdims=True), axis=0, keepdims=True)
            return jnp.maximum(kmax2, jnp.where(lo_mask, lo, hi))
        kst[0:1, :] = lax.fori_loop(0, seq // tk, body, jnp.zeros((1, DA_DV), F32))
        b0 = bias_ref[0, 0]
        b1 = bias_ref[0, 1]
        bmax = jnp.maximum(jnp.max(jnp.maximum(b0, b1), axis=0, keepdims=True), 0.0)
        bmin = jnp.minimum(jnp.min(jnp.minimum(jnp.where(b0 > 0.5 * NEG, b0, 0.0), b1), axis=0, keepdims=True), 0.0)
        kst[1:2, :] = jnp.broadcast_to(jnp.max(bmax, axis=1, keepdims=True), (1, DA_DV))
        kst[2:3, :] = jnp.broadcast_to(jnp.min(bmin, axis=1, keepdims=True), (1, DA_DV))

    q = group_norm(q_ref[...].astype(F32), qg_ref[...]) * (DA_DH ** -0.5 * LOG2E)
    qcat = jnp.concatenate([jnp.where(lo_mask, q, 0.0), jnp.where(lo_mask, 0.0, q)], axis=0).astype(BF16)
    acc_s[...] = jnp.zeros_like(acc_s)

    q2 = q * q * kst[0:1, :]
    ones8 = jnp.ones((8, DA_DV), BF16)
    bmax = kst[1:2, 0:1]
    bmin = kst[2:3, 0:1]
    bound = []
    for m in range(2):
        q2m = jnp.where(lo_mask, q2, 0.0) if m == 0 else jnp.where(lo_mask, 0.0, q2)
        bound.append(jnp.sqrt(_dot_nt(ones8, q2m.astype(BF16))[0:1, :]) * BOUND_SLACK)
    bound = jnp.concatenate(bound, axis=1)
    worst = jnp.max(2.0 * bound, axis=1, keepdims=True) + bmax - bmin
    safe = worst[0, 0] <= MAX_SHIFT_GAP

    def block(j, d, fixed_shift):
        r0 = pl.multiple_of(j * tk, tk)
        st = _dot_nt(kn[pl.ds(r0, tk), :], qcat)
        if d is not None:
            bias = bias_ref[0, d]
            st = st + jnp.concatenate([bias, bias], axis=1)
        if fixed_shift:
            acc_s[...] = acc_s[...] + _dot(vt[j], jnp.exp2(st - m_s[...]).astype(BF16))
        else:
            m_prev = m_s[...]
            m_new = jnp.maximum(m_prev, jnp.max(st, axis=0, keepdims=True))
            alpha = jnp.exp2(m_prev - m_new)
            acc_s[...] = alpha * acc_s[...] + _dot(vt[j], jnp.exp2(st - m_new).astype(BF16))
            m_s[...] = m_new

    def run(fixed_shift):
        n_far = jnp.maximum(qi - 1, 0)

        def far_pair(jj, carry):
            block(2 * jj, None, fixed_shift)
            block(2 * jj + 1, None, fixed_shift)
            return carry

        lax.fori_loop(0, n_far // 2, far_pair, 0)

        @pl.when(n_far % 2 == 1)
        def _():
            block(n_far - 1, None, fixed_shift)

        @pl.when(qi >= 1)
        def _():
            block(qi - 1, 1, fixed_shift)

        block(qi, 0, fixed_shift)

    @pl.when(safe)
    def _():
        m_s[...] = bound + bmax
        run(True)

    @pl.when(jnp.logical_not(safe))
    def _():
        m_s[...] = jnp.full(m_s.shape, NEG, F32)
        run(False)

    lam_p = lam_ref[...]
    lam = (jnp.exp(jnp.sum(lam_p[0:1, :] * lam_p[1:2, :], axis=-1, keepdims=True))
           - jnp.exp(jnp.sum(lam_p[2:3, :] * lam_p[3:4, :], axis=-1, keepdims=True)) + LAM_INIT)
    a0 = acc_s[:, 0:tq]
    a1 = acc_s[:, tq:2 * tq]
    ot = a0[0:DA_DV] / a0[DA_DV:DA_DV + 1] - lam * (a1[0:DA_DV] / a1[DA_DV:DA_DV + 1])
    r = lax.rsqrt(jnp.sum(ot * ot, axis=0, keepdims=True) * (1.0 / DA_DV) + EPS)
    ot = ot * r * (sg_ref[...] * (1.0 - LAM_INIT))
    o_ref[...] = ot.T.astype(o_ref.dtype)


def _attention(p, bias, qg, kg, lam_p, subln, batch, seq):
    t = batch * seq
    tq = min(ATT_BLOCK, seq)
    nq = seq // tq
    dv = DA_DV
    return pl.pallas_call(
        _attn_body,
        grid=(batch, DA_HEADS, nq),
        in_specs=[
            pl.BlockSpec((tq, dv), lambda b, h, i: (b * nq + i, P_DAQ // dv + h)),
            pl.BlockSpec((seq, dv), lambda b, h, i: (b, P_DAK // dv + h)),
            pl.BlockSpec((seq, dv), lambda b, h, i: (b, P_DAV // dv + h)),
            pl.BlockSpec((1, 2, tq, tq), lambda b, h, i: (h, 0, 0, 0)),
            pl.BlockSpec((1, dv), lambda b, h, i: (0, 0)),
            pl.BlockSpec((1, dv), lambda b, h, i: (0, 0)),
            pl.BlockSpec((4, DA_DH), lambda b, h, i: (0, 0)),
            pl.BlockSpec((dv, 1), lambda b, h, i: (0, 0)),
        ],
        out_specs=pl.BlockSpec((tq, dv), lambda b, h, i: (b * nq + i, h)),
        out_shape=jax.ShapeDtypeStruct((t, DA_HEADS * dv), BF16),
        scratch_shapes=[
            pltpu.VMEM((seq, dv), BF16),
            pltpu.VMEM((seq // tq, DA_VROWS, tq), BF16),
            pltpu.VMEM((8, dv), F32),
            pltpu.VMEM((1, 2 * tq), F32),
            pltpu.VMEM((DA_VROWS, 2 * tq), F32),
        ],
        compiler_params=pltpu.CompilerParams(dimension_semantics=("parallel", "parallel", "arbitrary")),
        name="diff_attention",
    )(p, p, p, bias, qg, kg, lam_p, subln)


def _memkv_body(mem_ref, mg_ref, w_ref, kg_ref, mk_ref, mv_ref):
    x = mem_ref[0]
    xn = x * _rms(x, D_MODEL) * mg_ref[...]
    kv = _dot(xn.astype(BF16), w_ref[...])
    hw = MX_HEADS * MX_DH
    for h in range(MX_HEADS):
        sl = slice(h * MX_DH, (h + 1) * MX_DH)
        kh = kv[:, sl]
        mk_ref[0, :, sl] = (kh * _rms(kh, MX_DH) * kg_ref[...]).astype(BF16)
    mv_ref[0] = kv[:, hw:].astype(BF16)


def _memkv(mem, mem_norm, w_kv, k_norm):
    b, n, _ = mem.shape
    hw = MX_HEADS * MX_DH
    return pl.pallas_call(
        _memkv_body,
        grid=(b,),
        in_specs=[
            pl.BlockSpec((1, n, D_MODEL), lambda i: (i, 0, 0)),
            pl.BlockSpec((1, D_MODEL), lambda i: (0, 0)),
            pl.BlockSpec((D_MODEL, 2 * hw), lambda i: (0, 0)),
            pl.BlockSpec((1, MX_DH), lambda i: (0, 0)),
        ],
        out_specs=[pl.BlockSpec((1, n, hw), lambda i: (i, 0, 0))] * 2,
        out_shape=[jax.ShapeDtypeStruct((b, n, hw), BF16)] * 2,
        name="memory_kv",
    )(mem, mem_norm, w_kv, k_norm)


def _merge_body(x_ref, odn_ref, oda_ref, mxq_ref, g0_ref, g1_ref, g2_ref, mk_ref, mv_ref, qg_ref,
                bg_ref, wb_ref, wo_ref, out_ref, omx):
    for h in range(MX_HEADS):
        sl = slice(h * MX_DH, (h + 1) * MX_DH)
        qh = mxq_ref[:, sl].astype(F32)
        qh = qh * _rms(qh, MX_DH) * qg_ref[...] * (MX_DH ** -0.5 * LOG2E)
        s = _dot_nt(qh.astype(BF16), mk_ref[0, :, sl])
        p = jnp.exp2(s - jnp.max(s, axis=-1, keepdims=True))
        oh = _dot(p.astype(BF16), mv_ref[0, :, sl]) / jnp.sum(p, axis=-1, keepdims=True)
        omx[:, sl] = oh.astype(BF16)
    y = None
    for r, (o_r, g_r) in enumerate(((odn_ref, g0_ref), (oda_ref, g1_ref), (omx, g2_ref))):
        gate = _sigmoid(g_r[...].astype(F32) + bg_ref[r:r + 1, :])
        term = gate * _dot(o_r[...], wb_ref[r])
        y = term if y is None else y + term
    out_ref[...] = x_ref[...] + _dot(y.astype(BF16), wo_ref[...])


def _merge(x2, o_dn, o_da, p, mk, mv, q_norm, b_gate, w_branch, w_out, seq):
    t = x2.shape[0]
    tm = min(512, seq)
    nt = seq // tm
    bw = 512
    n_mem = mk.shape[1]
    return pl.pallas_call(
        _merge_body,
        grid=(t // tm,),
        in_specs=[
            pl.BlockSpec((tm, D_MODEL), lambda i: (i, 0)),
            pl.BlockSpec((tm, bw), lambda i: (i, 0)),
            pl.BlockSpec((tm, bw), lambda i: (i, 0)),
            pl.BlockSpec((tm, bw), lambda i: (i, P_MXQ // bw)),
            pl.BlockSpec((tm, D_MODEL), lambda i: (i, P_GATE // D_MODEL)),
            pl.BlockSpec((tm, D_MODEL), lambda i: (i, P_GATE // D_MODEL + 1)),
            pl.BlockSpec((tm, D_MODEL), lambda i: (i, P_GATE // D_MODEL + 2)),
            pl.BlockSpec((1, n_mem, bw), lambda i: (i // nt, 0, 0)),
            pl.BlockSpec((1, n_mem, bw), lambda i: (i // nt, 0, 0)),
            pl.BlockSpec((1, MX_DH), lambda i: (0, 0)),
            pl.BlockSpec((3, D_MODEL), lambda i: (0, 0)),
            pl.BlockSpec((3, bw, D_MODEL), lambda i: (0, 0, 0)),
            pl.BlockSpec((D_MODEL, D_MODEL), lambda i: (0, 0)),
        ],
        out_specs=pl.BlockSpec((tm, D_MODEL), lambda i: (i, 0)),
        out_shape=jax.ShapeDtypeStruct((t, D_MODEL), F32),
        scratch_shapes=[pltpu.VMEM((tm, bw), BF16)],
        compiler_params=pltpu.CompilerParams(dimension_semantics=("parallel",)),
        name="merge",
    )(x2, o_dn, o_da, p, p, p, p, mk, mv, q_norm, b_gate, w_branch, w_out)


def _router_body(x_ref, g_ref, wr_ref, br_ref, h_ref, idx_ref, wt_ref, rank_ref, cnt_ref):
    tm = x_ref.shape[0]
    x = x_ref[...]
    h = x * _rms(x, D_MODEL) * g_ref[...]
    h_ref[...] = h.astype(BF16)
    logits = _dot3(h, wr_ref[...]) + br_ref[...]
    lane = lax.broadcasted_iota(I32, (tm, LANES), 1)
    lane_f = lane.astype(F32)
    work = logits
    sel = jnp.zeros((tm, LANES), F32)
    vals, idxs = [], []
    for _ in range(TOP_K):
        mx = jnp.max(work, axis=-1, keepdims=True)
        ik = jnp.min(jnp.where(work == mx, lane_f, float(LANES)), axis=-1, keepdims=True)
        hit = lane_f == ik
        sel = jnp.where(hit, 1.0, sel)
        work = jnp.where(hit, -jnp.inf, work)
        vals.append(mx)
        idxs.append(ik)
    es = [jnp.exp(v - vals[0]) for v in vals]
    den = es[0] + es[1] + es[2] + es[3]
    sub = cnt_ref.shape[0]
    mt = tm // sub
    r = lax.broadcasted_iota(I32, (tm, tm), 0)
    c = lax.broadcasted_iota(I32, (tm, tm), 1)
    tril = jnp.where((r > c) & (r // mt == c // mt), 1.0, 0.0).astype(BF16)
    cum = _dot(tril, sel.astype(BF16))
    idx_o = jnp.zeros((tm, LANES), F32)
    wt_o = jnp.zeros((tm, LANES), F32)
    rank_o = jnp.zeros((tm, LANES), F32)
    for k in range(TOP_K):
        rk = jnp.sum(jnp.where(lane_f == idxs[k], cum, 0.0), axis=-1, keepdims=True)
        idx_o = jnp.where(lane == k, idxs[k], idx_o)
        wt_o = jnp.where(lane == k, es[k] / den, wt_o)
        rank_o = jnp.where(lane == k, rk, rank_o)
    idx_ref[...] = idx_o.astype(I32)
    wt_ref[...] = wt_o
    rank_ref[...] = rank_o.astype(I32)
    for s in range(sub):
        cnt_ref[s] = jnp.broadcast_to(jnp.sum(sel[s * mt:(s + 1) * mt], axis=0, keepdims=True), (8, LANES))


def _router(x1, gain, w_r, b_r):
    t = x1.shape[0]
    mt = min(MOE_TILE, t)
    sub = 2 if t % (2 * mt) == 0 else 1
    tm = sub * mt
    row = lambda i: (i, 0)
    fixed = lambda i: (0, 0)
    return pl.pallas_call(
        _router_body,
        grid=(t // tm,),
        in_specs=[
            pl.BlockSpec((tm, D_MODEL), row),
            pl.BlockSpec((1, D_MODEL), fixed),
            pl.BlockSpec((D_MODEL, LANES), fixed),
            pl.BlockSpec((1, LANES), fixed),
        ],
        out_specs=[
            pl.BlockSpec((tm, D_MODEL), row),
            pl.BlockSpec((tm, LANES), row),
            pl.BlockSpec((tm, LANES), row),
            pl.BlockSpec((tm, LANES), row),
            pl.BlockSpec((sub, 8, LANES), lambda i: (i, 0, 0)),
        ],
        out_shape=[
            jax.ShapeDtypeStruct((t, D_MODEL), BF16),
            jax.ShapeDtypeStruct((t, LANES), I32),
            jax.ShapeDtypeStruct((t, LANES), F32),
            jax.ShapeDtypeStruct((t, LANES), I32),
            jax.ShapeDtypeStruct((t // mt, 8, LANES), F32),
        ],
        compiler_params=pltpu.CompilerParams(dimension_semantics=("arbitrary",)),
        name="router",
    )(x1, gain, w_r, b_r)


def _lane_cumsum(x):
    lane = lax.broadcasted_iota(I32, x.shape, 1)
    s = 1
    while s < N_EXPERTS:
        x = x + jnp.where(lane >= s, pltpu.roll(x, s, axis=1), 0.0)
        s *= 2
    return x


def _plan_body(cnt_ref, gs_ref, ls_ref, n8_ref, lsf_ref, tail_ref, meta_ref):
    nt = cnt_ref.shape[0]
    ga = float(GROUP_ALIGN)
    eb = float(EXPERT_BLOCK)
    lane = lax.broadcasted_iota(I32, (nt, LANES), 1)
    r8 = jnp.where(lane < N_EXPERTS, jnp.floor((cnt_ref[...] + (ga - 1.0)) * (1.0 / ga)) * ga, 0.0)
    ri = lax.broadcasted_iota(I32, (nt, nt), 0)
    ci = lax.broadcasted_iota(I32, (nt, nt), 1)
    before = _dot(jnp.where(ri > ci, 1.0, 0.0).astype(BF16), r8.astype(BF16))
    tot = jnp.sum(r8, axis=0, keepdims=True)
    region = jnp.floor((tot + (eb - 1.0)) * (1.0 / eb)) * eb
    pends = _lane_cumsum(jnp.broadcast_to(region, (8, LANES)))[0:1, :]
    pstart = pends - region
    lstart = _lane_cumsum(r8) - r8
    gs_ref[...] = (pstart + before).astype(I32)
    ls_ref[...] = lstart.astype(I32)
    n8_ref[...] = (r8 * (1.0 / ga)).astype(I32)
    lsf_ref[...] = lstart
    row8 = lax.broadcasted_iota(I32, (8, LANES), 0)
    tail = jnp.where(row8 == 0, pstart + tot, jnp.where(row8 == 1, (region - tot) * (1.0 / ga), 0.0))
    tail_ref[...] = tail.astype(I32)
    nb = meta_ref.shape[0]
    ln = lax.broadcasted_iota(I32, (nb, LANES), 1)
    blk = lax.broadcasted_iota(I32, (nb, LANES), 0).astype(F32) * eb
    be = jnp.sum(jnp.where((ln < N_EXPERTS) & (pends <= blk), 1.0, 0.0), axis=-1, keepdims=True)
    be = jnp.minimum(be, float(N_EXPERTS - 1))
    used = jnp.sum(jnp.where(ln == N_EXPERTS - 1, pends, 0.0), axis=-1, keepdims=True) * (1.0 / eb)
    meta_ref[...] = jnp.where(ln == 0, be, jnp.where(ln == 1, used, 0.0)).astype(I32)


def _plan(cnt, n_blocks_pad):
    nt = cnt.shape[0]
    shp = jax.ShapeDtypeStruct((nt, LANES), I32)
    return pl.pallas_call(
        _plan_body,
        out_shape=[shp, shp, shp, jax.ShapeDtypeStruct((nt, LANES), F32),
                   jax.ShapeDtypeStruct((8, LANES), I32),
                   jax.ShapeDtypeStruct((n_blocks_pad, LANES), I32)],
        name="dispatch_plan",
    )(cnt)


def _local_positions(idx_ref, rank_ref, lsf_ref):
    tm = idx_ref.shape[0]
    lane = lax.broadcasted_iota(I32, (tm, LANES), 1)
    idx = idx_ref[...]
    rank = rank_ref[...].astype(F32)
    ls_row = lsf_ref[0]
    out = []
    for k in range(TOP_K):
        base = jnp.sum(jnp.where(lane == idx[:, k:k + 1], ls_row, 0.0), axis=-1, keepdims=True)
        out.append((base + rank[:, k:k + 1]).astype(I32))
    return out


def _local_positions_lanes(idx_ref, rank_ref, lsf_ref):
    tm = idx_ref.shape[0]
    idx_t = idx_ref[...].astype(F32).T
    rank_t = rank_ref[...].astype(F32).T
    ls_col = jnp.broadcast_to(lsf_ref[0], (LANES, LANES)).T[:, 0:1]
    expert = lax.broadcasted_iota(I32, (LANES, tm), 0).astype(F32)
    out = []
    for k in range(TOP_K):
        base = jnp.sum(jnp.where(expert == idx_t[k:k + 1, :], ls_col, 0.0), axis=0, keepdims=True)
        out.append((base + rank_t[k:k + 1, :]).astype(I32))
    return out


def _group_copies(tables, tile, make_copy, start):
    gs_ref, ls_ref, n8_ref = tables

    def group(e, carry):
        g0 = gs_ref[tile, e]
        l0 = ls_ref[tile, e]

        def one(j, c):
            cp = make_copy(pl.multiple_of(l0 + j * GROUP_ALIGN, GROUP_ALIGN),
                           pl.multiple_of(g0 + j * GROUP_ALIGN, GROUP_ALIGN))
            cp.start() if start else cp.wait()
            return c

        lax.fori_loop(0, n8_ref[tile, e], one, 0)
        return carry

    lax.fori_loop(0, N_EXPERTS, group, 0)


def _dispatch_body(gs_ref, ls_ref, n8_ref, tail_ref, h_ref, idx_ref, rank_ref, lsf_ref, xs_ref,
                   xl, zbuf, sem):
    i = pl.program_id(0)
    last = i == pl.num_programs(0) - 1
    slot = i % 2
    tables = (gs_ref, ls_ref, n8_ref)
    tm = h_ref.shape[0]
    lrows = xl.shape[1]
    pos = lax.broadcasted_iota(I32, (lrows, tm), 0)
    lp = _local_positions_lanes(idx_ref, rank_ref, lsf_ref)
    hit = pos == lp[0]
    for k in range(1, TOP_K):
        hit = hit | (pos == lp[k])
    xl[slot] = _dot(jnp.where(hit, 1.0, 0.0).astype(BF16), h_ref[...])

    def copy_from(s):
        def make_copy(l, g):
            return pltpu.make_async_copy(xl.at[s, pl.ds(l, GROUP_ALIGN), :],
                                         xs_ref.at[pl.ds(g, GROUP_ALIGN), :], sem.at[s])
        return make_copy

    @pl.when(i >= 1)
    def _():
        _group_copies(tables, i - 1, copy_from(1 - slot), False)

    _group_copies(tables, i, copy_from(slot), True)

    def tails(start):
        def per_expert(e, carry):
            g0 = tail_ref[0, e]

            def one(j, c):
                cp = pltpu.make_async_copy(
                    zbuf, xs_ref.at[pl.ds(pl.multiple_of(g0 + j * GROUP_ALIGN, GROUP_ALIGN), GROUP_ALIGN), :],
                    sem.at[2])
                cp.start() if start else cp.wait()
                return c

            lax.fori_loop(0, tail_ref[1, e], one, 0)
            return carry

        lax.fori_loop(0, N_EXPERTS, per_expert, 0)

    @pl.when(last)
    def _():
        zbuf[...] = jnp.zeros_like(zbuf)
        tails(True)
        _group_copies(tables, i, copy_from(slot), False)
        tails(False)


def _dispatch(gs, ls, n8, tail, h2, idx, rank, lsf, n_slots):
    t = h2.shape[0]
    tm = min(MOE_TILE, t)
    nt = t // tm
    lrows = tm * TOP_K + N_EXPERTS * GROUP_ALIGN
    row = lambda i, *_: (i, 0)
    grid_spec = pltpu.PrefetchScalarGridSpec(
        num_scalar_prefetch=4,
        grid=(nt,),
        in_specs=[
            pl.BlockSpec((tm, D_MODEL), row),
            pl.BlockSpec((tm, LANES), row),
            pl.BlockSpec((tm, LANES), row),
            pl.BlockSpec((1, 1, LANES), lambda i, *_: (i, 0, 0)),
        ],
        out_specs=pl.BlockSpec(memory_space=pl.ANY),
        scratch_shapes=[pltpu.VMEM((2, lrows, D_MODEL), F32), pltpu.VMEM((GROUP_ALIGN, D_MODEL), F32),
                        pltpu.SemaphoreType.DMA((3,))],
    )
    return pl.pallas_call(
        _dispatch_body,
        grid_spec=grid_spec,
        out_shape=jax.ShapeDtypeStruct((n_slots, D_MODEL), F32),
        compiler_params=pltpu.CompilerParams(dimension_semantics=("arbitrary",)),
        name="moe_dispatch",
    )(gs, ls, n8, tail, h2, idx, rank, lsf)


def _expert_body(be_ref, nu_ref, x_ref, wgu_ref, bgu_ref, wd_ref, bd_ref, y_ref, wgu_b, wd_b):
    j = pl.program_id(0)

    @pl.when(j < nu_ref[0])
    def _():
        @pl.when(jnp.logical_or(j == 0, be_ref[j] != be_ref[jnp.maximum(j - 1, 0)]))
        def _():
            wgu_b[...] = wgu_ref[0].astype(BF16)
            wd_b[...] = wd_ref[0].astype(BF16)

        gu = _dot(x_ref[...].astype(BF16), wgu_b[...]) + bgu_ref[0]
        gate = jnp.minimum(gu[:, :D_FF], SWIGLU_LIMIT)
        up = jnp.clip(gu[:, D_FF:], -SWIGLU_LIMIT, SWIGLU_LIMIT)
        act = (up + 1.0) * gate * _sigmoid(SWIGLU_ALPHA * gate)
        y_ref[...] = _dot(act.astype(BF16), wd_b[...]) + bd_ref[0]


def _experts(block_e, n_used, xs, wgu, bgu, wd, bd):
    n_slots = xs.shape[0]
    nb = n_slots // EXPERT_BLOCK

    def blk(j, be, nu):
        return jnp.minimum(j, nu[0] - 1)

    grid_spec = pltpu.PrefetchScalarGridSpec(
        num_scalar_prefetch=2,
        grid=(nb,),
        in_specs=[
            pl.BlockSpec((EXPERT_BLOCK, D_MODEL), lambda j, be, nu: (blk(j, be, nu), 0)),
            pl.BlockSpec((1, D_MODEL, 2 * D_FF), lambda j, be, nu: (be[blk(j, be, nu)], 0, 0)),
            pl.BlockSpec((1, 1, 2 * D_FF), lambda j, be, nu: (be[blk(j, be, nu)], 0, 0)),
            pl.BlockSpec((1, D_FF, D_MODEL), lambda j, be, nu: (be[blk(j, be, nu)], 0, 0)),
            pl.BlockSpec((1, 1, D_MODEL), lambda j, be, nu: (be[blk(j, be, nu)], 0, 0)),
        ],
        out_specs=pl.BlockSpec((EXPERT_BLOCK, D_MODEL), lambda j, be, nu: (blk(j, be, nu), 0)),
        scratch_shapes=[pltpu.VMEM((D_MODEL, 2 * D_FF), BF16), pltpu.VMEM((D_FF, D_MODEL), BF16)],
    )
    return pl.pallas_call(
        _expert_body,
        grid_spec=grid_spec,
        out_shape=jax.ShapeDtypeStruct((n_slots, D_MODEL), F32),
        compiler_params=pltpu.CompilerParams(dimension_semantics=("arbitrary",)),
        name="moe_experts",
    )(block_e, n_used, xs, wgu, bgu, wd, bd)


def _combine_body(gs_ref, ls_ref, n8_ref, x_ref, wt_ref, idx_ref, rank_ref, lsf_ref, y_ref, out_ref,
                  yl, sem):
    i = pl.program_id(0)
    slot = i % 2
    tables = (gs_ref, ls_ref, n8_ref)
    tm = x_ref.shape[0]
    lrows = yl.shape[1]

    def copy_into(s):
        def make_copy(l, g):
            return pltpu.make_async_copy(y_ref.at[pl.ds(g, GROUP_ALIGN), :],
                                         yl.at[s, pl.ds(l, GROUP_ALIGN), :], sem.at[s])
        return make_copy

    @pl.when(i == 0)
    def _():
        yl[...] = jnp.zeros_like(yl)
        _group_copies(tables, i, copy_into(slot), True)

    @pl.when(i + 1 < pl.num_programs(0))
    def _():
        _group_copies(tables, i + 1, copy_into(1 - slot), True)

    pos = lax.broadcasted_iota(I32, (tm, lrows), 1)
    lp = _local_positions(idx_ref, rank_ref, lsf_ref)
    wt = wt_ref[...]
    wm = jnp.zeros((tm, lrows), F32)
    for k in range(TOP_K):
        wm = jnp.where(pos == lp[k], wt[:, k:k + 1], wm)
    _group_copies(tables, i, copy_into(slot), False)
    out_ref[...] = x_ref[...] + _dot(wm.astype(BF16), yl[slot].astype(BF16))


def _combine(gs, ls, n8, x1, wts, idx, rank, lsf, y):
    t = x1.shape[0]
    tm = min(MOE_TILE, t)
    nt = t // tm
    lrows = tm * TOP_K + N_EXPERTS * GROUP_ALIGN
    row = lambda i, *_: (i, 0)
    grid_spec = pltpu.PrefetchScalarGridSpec(
        num_scalar_prefetch=3,
        grid=(nt,),
        in_specs=[
            pl.BlockSpec((tm, D_MODEL), row),
            pl.BlockSpec((tm, LANES), row),
            pl.BlockSpec((tm, LANES), row),
            pl.BlockSpec((tm, LANES), row),
            pl.BlockSpec((1, 1, LANES), lambda i, *_: (i, 0, 0)),
            pl.BlockSpec(memory_space=pl.ANY),
        ],
        out_specs=pl.BlockSpec((tm, D_MODEL), row),
        scratch_shapes=[pltpu.VMEM((2, lrows, D_MODEL), F32), pltpu.SemaphoreType.DMA((2,))],
    )
    return pl.pallas_call(
        _combine_body,
        grid_spec=grid_spec,
        out_shape=jax.ShapeDtypeStruct((t, D_MODEL), F32),
        compiler_params=pltpu.CompilerParams(dimension_semantics=("arbitrary",)),
        name="moe_combine",
    )(gs, ls, n8, x1, wts, idx, rank, lsf, y)


def _pad_lanes(v, fill=0.0):
    v = v.astype(F32).reshape(1, -1)
    return jnp.pad(v, ((0, 0), (0, LANES - v.shape[1])), constant_values=fill)


def _mixer(x2, mem, rel_table, attn_norm, w_in, b_gate, dn_conv, dn_a_log, dn_dt_bias, dn_out_norm,
           da_q_norm, da_k_norm, da_lambda, da_subln, mem_norm, w_mem_kv, mx_q_norm, mx_k_norm,
           w_branch, w_out, batch, seq):
    wp = jnp.concatenate([w_in[:, :W_AB_LO], w_in[:, W_AB_HI:]], axis=1).astype(BF16)
    wab = jnp.pad(w_in[:, W_AB_LO:W_AB_HI], ((0, 0), (0, LANES - (W_AB_HI - W_AB_LO))))
    p, ab = _inproj(x2, attn_norm.reshape(1, -1), wp, wab)

    o_dn = _deltanet(p, ab, dn_conv, _pad_lanes(dn_a_log), _pad_lanes(dn_dt_bias),
                     dn_out_norm.reshape(1, -1), batch, seq)

    tq = min(ATT_BLOCK, seq)
    bias = _bias_tiles(rel_table.T, tq)
    o_da = _attention(p, bias, jnp.tile(da_q_norm, 2).reshape(1, -1), jnp.tile(da_k_norm, 2).reshape(1, -1),
                      da_lambda, da_subln.reshape(-1, 1), batch, seq)

    mk, mv = _memkv(mem, mem_norm.reshape(1, -1), w_mem_kv.astype(BF16), mx_k_norm.reshape(1, -1))
    return _merge(x2, o_dn, o_da, p, mk, mv, mx_q_norm.reshape(1, -1), b_gate.reshape(3, D_MODEL),
                  w_branch.astype(BF16), w_out.astype(BF16), seq)


def _moe(x1, ffn_norm, w_router, b_router, w_gate_up, b_gate_up, w_down, b_down):
    t = x1.shape[0]
    nt = t // min(MOE_TILE, t)
    max_rows = t * TOP_K + nt * N_EXPERTS * (GROUP_ALIGN - 1)
    n_blocks = -(-max_rows // EXPERT_BLOCK) + N_EXPERTS
    n_blocks_pad = -(-n_blocks // 8) * 8
    n_slots = n_blocks * EXPERT_BLOCK

    wr = jnp.pad(w_router, ((0, 0), (0, LANES - N_EXPERTS)))
    h2, idx, wts, rank, cnt = _router(x1, ffn_norm.reshape(1, -1), wr, _pad_lanes(b_router, NEG))
    gs, ls, n8, lsf, tail, meta = _plan(cnt[:, 0, :], n_blocks_pad)
    block_e = meta[:n_blocks, 0]
    n_used = meta[0:1, 1]
    lsf = lsf.reshape(nt, 1, LANES)

    xs = _dispatch(gs, ls, n8, tail, h2, idx, rank, lsf, n_slots)
    y = _experts(block_e, n_used, xs, w_gate_up, b_gate_up.reshape(N_EXPERTS, 1, -1),
                 w_down, b_down.reshape(N_EXPERTS, 1, -1))
    return _combine(gs, ls, n8, x1, wts, idx, rank, lsf, y)


def kernel(x, mem, rel_table, attn_norm, w_in, b_gate, dn_conv, dn_a_log, dn_dt_bias, dn_out_norm,
           da_q_norm, da_k_norm, da_lambda, da_subln, mem_norm, w_mem_kv, mx_q_norm, mx_k_norm,
           w_branch, w_out, ffn_norm, w_router, b_router, w_gate_up, b_gate_up, w_down, b_down):
    batch, seq, d = x.shape
    x2 = x.reshape(batch * seq, d)
    x1 = _mixer(x2, mem, rel_table, attn_norm[0], w_in[0], b_gate[0], dn_conv[0], dn_a_log[0],
                dn_dt_bias[0], dn_out_norm[0], da_q_norm[0], da_k_norm[0], da_lambda[0], da_subln[0],
                mem_norm[0], w_mem_kv[0], mx_q_norm[0], mx_k_norm[0], w_branch[0], w_out[0], batch, seq)
    out = _moe(x1, ffn_norm[0], w_router[0], b_router[0], w_gate_up[0], b_gate_up[0], w_down[0],
               b_down[0])
    return out.reshape(batch, seq, d)
```

```python
import functools
import math

import jax
import jax.numpy as jnp
from jax import lax
from jax.experimental import pallas as pl
from jax.experimental.pallas import tpu as pltpu

F32 = jnp.float32
BF16 = jnp.bfloat16
I32 = jnp.int32

D_MODEL = 1024
EPS = 1e-6
LANES = 128

DN_HEADS = 4
DN_DK = 128
DN_CHUNK = 64
DN_CONV = 4

DA_HEADS = 4
DA_DH = 64

MX_HEADS = 4
MX_DH = 128

REL_BUCKETS = 32
REL_MAX_DIST = 128

N_EXPERTS = 32
TOP_K = 4
D_FF = 1024
SWIGLU_LIMIT = 7.0
SWIGLU_ALPHA = 1.702
EXPERT_BLOCK = 512
MOE_TILE = 256
GROUP_ALIGN = 8

LAM_INIT = 0.8 - 0.6 * math.exp(-0.3 * 0)
LOG2E = 1.4426950408889634
NEG = -1e30

P_DNQ, P_DNK, P_DNV, P_DNZ = 0, 512, 1024, 1536
P_DAQ, P_DAK, P_DAV = 2048, 2560, 3072
P_MXQ = 3584
P_GATE = 4096
P_COLS = 7168
W_AB_LO, W_AB_HI = 2048, 2056


def _dot(a, b):
    return jnp.dot(a, b, preferred_element_type=F32)


def _dot_nt(a, b):
    return lax.dot_general(a, b, (((1,), (1,)), ((), ())), preferred_element_type=F32)


def _dot_tn(a, b):
    return lax.dot_general(a, b, (((0,), (0,)), ((), ())), preferred_element_type=F32)


def _split(x):
    hi = x.astype(BF16)
    lo = (x - hi.astype(F32)).astype(BF16)
    return hi, lo


def _dot3(a, b):
    ah, al = _split(a)
    bh, bl = _split(b)
    return _dot(ah, bh) + _dot(ah, bl) + _dot(al, bh)


def _sigmoid(x):
    return 1.0 / (1.0 + jnp.exp(-x))


def _rms(x, n):
    return lax.rsqrt(jnp.sum(x * x, axis=-1, keepdims=True) * (1.0 / n) + EPS)


def _inproj_body(x_ref, g_ref, w_ref, wab_ref, p_ref, ab_ref, h_scr):
    @pl.when(pl.program_id(1) == 0)
    def _():
        x = x_ref[...]
        h = x * _rms(x, D_MODEL) * g_ref[...]
        h_scr[...] = h.astype(BF16)
        ab_ref[...] = _dot3(h, wab_ref[...])

    p_ref[...] = _dot(h_scr[...], w_ref[...]).astype(p_ref.dtype)


def _inproj(x2, gain, wp, wab):
    t = x2.shape[0]
    tm = min(1024, t)
    tn = 1024
    return pl.pallas_call(
        _inproj_body,
        grid=(t // tm, P_COLS // tn),
        in_specs=[
            pl.BlockSpec((tm, D_MODEL), lambda i, j: (i, 0)),
            pl.BlockSpec((1, D_MODEL), lambda i, j: (0, 0)),
            pl.BlockSpec((D_MODEL, tn), lambda i, j: (0, j)),
            pl.BlockSpec((D_MODEL, LANES), lambda i, j: (0, 0)),
        ],
        out_specs=[
            pl.BlockSpec((tm, tn), lambda i, j: (i, j)),
            pl.BlockSpec((tm, LANES), lambda i, j: (i, 0)),
        ],
        out_shape=[
            jax.ShapeDtypeStruct((t, P_COLS), BF16),
            jax.ShapeDtypeStruct((t, LANES), F32),
        ],
        scratch_shapes=[pltpu.VMEM((tm, D_MODEL), BF16)],
        compiler_params=pltpu.CompilerParams(dimension_semantics=("parallel", "arbitrary")),
        name="inproj",
    )(x2, gain, wp, wab)


DN_HALO = 16
DN_SCAN_CHUNK = 256


def _deltanet_body(q_ref, k_ref, v_ref, z_ref, qh_ref, kh_ref, vh_ref, ab_ref, cw_ref, alog_ref,
                   dtb_ref, on_ref, o_ref, stage, qs, ks, vs, s_scr):
    i = pl.program_id(1)
    tc = q_ref.shape[0]
    hw = DN_HEADS * DN_DK

    @pl.when(i == 0)
    def _():
        s_scr[...] = jnp.zeros_like(s_scr)

    for src, halo, dst, off, kind in ((q_ref, qh_ref, qs, 0, "q"), (k_ref, kh_ref, ks, hw, "k"),
                                      (v_ref, vh_ref, vs, 2 * hw, "v")):
        hal = halo[...].astype(F32)
        stage[0:DN_HALO, :] = jnp.where(i == 0, 0.0, hal)
        stage[DN_HALO:DN_HALO + tc, :] = src[...].astype(F32)
        base = DN_HALO - (DN_CONV - 1)
        y = stage[base:base + tc, :] * cw_ref[0:1, off:off + hw]
        for j in range(1, DN_CONV):
            y = y + stage[base + j:base + j + tc, :] * cw_ref[j:j + 1, off:off + hw]
        y = y * _sigmoid(y)
        if kind == "v":
            dst[...] = y
        else:
            for h in range(DN_HEADS):
                sl = slice(h * DN_DK, (h + 1) * DN_DK)
                yh = y[:, sl]
                r = lax.rsqrt(jnp.sum(yh * yh, axis=-1, keepdims=True) + EPS)
                if kind == "q":
                    r = r * (DN_DK ** -0.5)
                dst[:, sl] = yh * r

    c = min(DN_SCAN_CHUNK, tc)
    row = lax.broadcasted_iota(I32, (c, c), 0)
    col = lax.broadcasted_iota(I32, (c, c), 1)
    incl = row >= col
    strict = row > col
    same_blk = (row // DN_CHUNK) == (col // DN_CHUNK)
    tri = jnp.where(incl, 1.0, 0.0).astype(BF16)
    eye = jnp.where(row == col, 1.0, 0.0)
    neg_a = -jnp.exp(alog_ref[...])
    dtb = dtb_ref[...]

    def chunk(ci, carry):
        r0 = pl.multiple_of(ci * c, c)
        abc = ab_ref[pl.ds(r0, c), :]
        a_in = abc + dtb
        g_all = neg_a * (jnp.maximum(a_in, 0.0) + jnp.log(1.0 + jnp.exp(-jnp.abs(a_in))))
        beta_all = _sigmoid(abc)
        zc = z_ref[pl.ds(r0, c), :].astype(F32)
        for h in range(DN_HEADS):
            sl = slice(h * DN_DK, (h + 1) * DN_DK)
            q = qs[pl.ds(r0, c), sl]
            k = ks[pl.ds(r0, c), sl]
            v = vs[pl.ds(r0, c), sl]
            g = g_all[:, h:h + 1]
            beta = beta_all[:, DN_HEADS + h:DN_HEADS + h + 1]
            g_hi, g_lo = _split(jnp.where(strict, g, 0.0))
            diff = _dot(tri, g_hi) + _dot(tri, g_lo)
            gc = diff[:, 0:1] + g[0:1, :]
            decay = jnp.where(incl, jnp.exp(diff), 0.0)
            kb = k.astype(BF16)
            qkk = _dot_nt(jnp.concatenate([q.astype(BF16), kb], axis=0), kb)
            qk = qkk[:c]
            kk = qkk[c:]
            lower = jnp.where(strict, kk * decay * beta, 0.0)
            pw = jnp.where(same_blk, -lower, 0.0)
            dinv = eye + pw
            for _ in range(int(math.log2(DN_CHUNK)) - 1):
                pwb = pw.astype(BF16)
                pw = _dot(pwb, pwb)
                dinv = dinv + _dot(dinv.astype(BF16), pw.astype(BF16))
            dinv_b = dinv.astype(BF16)
            pw = -_dot(dinv_b, jnp.where(same_blk, 0.0, lower).astype(BF16))
            xm = eye + pw
            for _ in range(int(math.log2(c // DN_CHUNK)) - 1):
                pwb = pw.astype(BF16)
                pw = _dot(pwb, pwb)
                xm = xm + _dot(xm.astype(BF16), pw.astype(BF16))
            inv = _dot(xm.astype(BF16), dinv_b)
            egc = jnp.exp(gc)
            rhs = jnp.concatenate([v * beta, k * (beta * egc)], axis=1)
            sol = _dot(inv.astype(BF16), rhs.astype(BF16))
            u = sol[:, :DN_DK]
            w = sol[:, DN_DK:]
            qkm = jnp.where(incl, qk * decay, 0.0)
            gl = gc[c - 1:c, :]
            state = s_scr[h]
            sb = state.astype(BF16)
            ws = _dot(jnp.concatenate([w.astype(BF16), (q * egc).astype(BF16)], axis=0), sb)
            v_new = u - ws[:c]
            o = ws[c:] + _dot(qkm.astype(BF16), v_new.astype(BF16))
            s_scr[h] = state * jnp.exp(gl) + _dot_tn(kb, (v_new * jnp.exp(gl - gc)).astype(BF16))
            zz = zc[:, sl]
            on = o * _rms(o, DN_DK) * on_ref[...]
            o_ref[pl.ds(r0, c), sl] = (on * (zz * _sigmoid(zz))).astype(o_ref.dtype)
        return carry

    lax.fori_loop(0, tc // c, chunk, 0, unroll=True)


def _deltanet(p, ab, conv_w, alog_row, dtb_row, out_norm, batch, seq):
    t = batch * seq
    tc = min(512, seq)
    nt = seq // tc
    hw = DN_HEADS * DN_DK

    def main(cb):
        return pl.BlockSpec((tc, hw), lambda b, i: (b * nt + i, cb))

    def halo(cb):
        return pl.BlockSpec(
            (DN_HALO, hw),
            lambda b, i: (jnp.maximum((b * seq + i * tc) // DN_HALO - 1, 0), cb))

    return pl.pallas_call(
        _deltanet_body,
        grid=(batch, nt),
        in_specs=[
            main(P_DNQ // hw), main(P_DNK // hw), main(P_DNV // hw), main(P_DNZ // hw),
            halo(P_DNQ // hw), halo(P_DNK // hw), halo(P_DNV // hw),
            pl.BlockSpec((tc, LANES), lambda b, i: (b * nt + i, 0)),
            pl.BlockSpec((DN_CONV, 3 * hw), lambda b, i: (0, 0)),
            pl.BlockSpec((1, LANES), lambda b, i: (0, 0)),
            pl.BlockSpec((1, LANES), lambda b, i: (0, 0)),
            pl.BlockSpec((1, DN_DK), lambda b, i: (0, 0)),
        ],
        out_specs=pl.BlockSpec((tc, hw), lambda b, i: (b * nt + i, 0)),
        out_shape=jax.ShapeDtypeStruct((t, hw), BF16),
        scratch_shapes=[
            pltpu.VMEM((DN_HALO + tc, hw), F32),
            pltpu.VMEM((tc, hw), F32),
            pltpu.VMEM((tc, hw), F32),
            pltpu.VMEM((tc, hw), F32),
            pltpu.VMEM((DN_HEADS, DN_DK, DN_DK), F32),
        ],
        compiler_params=pltpu.CompilerParams(dimension_semantics=("parallel", "arbitrary")),
        name="deltanet",
    )(p, p, p, p, p, p, p, ab, conv_w, alog_row, dtb_row, out_norm)


ATT_BLOCK = 512


def _bias_body(tbl_ref, o_ref):
    h = pl.program_id(0)
    tq = o_ref.shape[2]
    key = lax.broadcasted_iota(I32, (tq, tq), 0)
    qry = lax.broadcasted_iota(I32, (tq, tq), 1)
    max_exact = REL_BUCKETS // 2
    far = tbl_ref[h, REL_BUCKETS - 1]
    for d in range(2):
        n = qry - key + d * tq
        nn = jnp.maximum(n, 0)
        nf = jnp.maximum(nn, 1).astype(F32)
        large = max_exact + (jnp.log(nf / max_exact) / math.log(REL_MAX_DIST / max_exact)
                             * (REL_BUCKETS - max_exact)).astype(I32)
        large = jnp.minimum(large, REL_BUCKETS - 1)
        bucket = jnp.where(nn < max_exact, nn, large)
        val = jnp.zeros((tq, tq), F32)
        for b in range(REL_BUCKETS):
            val = jnp.where(bucket == b, tbl_ref[h, b], val)
        o_ref[0, d] = jnp.where(n >= 0, (val - far) * LOG2E, NEG)


def _bias_tiles(tbl_t, tq):
    return pl.pallas_call(
        _bias_body,
        grid=(DA_HEADS,),
        in_specs=[pl.BlockSpec(memory_space=pltpu.SMEM)],
        out_specs=pl.BlockSpec((1, 2, tq, tq), lambda h: (h, 0, 0, 0)),
        out_shape=jax.ShapeDtypeStruct((DA_HEADS, 2, tq, tq), F32),
        name="t5_bias_tiles",
    )(tbl_t)


DA_DV = 2 * DA_DH
DA_VROWS = DA_DV + 16


BOUND_SLACK = 1.02
MAX_SHIFT_GAP = 110.0


def _attn_body(q_ref, k_ref, v_ref, bias_ref, qg_ref, kg_ref, lam_ref, sg_ref, o_ref,
               kn, vt, kst, m_s, acc_s):
    qi = pl.program_id(2)
    tq = q_ref.shape[0]
    seq = k_ref.shape[0]
    tk = tq
    lo_mask = lax.broadcasted_iota(I32, (1, DA_DV), 1) < DA_DH

    def group_norm(x, gain):
        x2 = x * x
        lo = jnp.sum(jnp.where(lo_mask, x2, 0.0), axis=-1, keepdims=True)
        hi = jnp.sum(jnp.where(lo_mask, 0.0, x2), axis=-1, keepdims=True)
        r = jnp.where(lo_mask, lax.rsqrt(lo * (1.0 / DA_DH) + EPS), lax.rsqrt(hi * (1.0 / DA_DH) + EPS))
        return x * r * gain

    @pl.when(qi == 0)
    def _():
        ones = jnp.ones((DA_VROWS - DA_DV, tk), BF16)

        def body(c, kmax2):
            r0 = pl.multiple_of(c * tk, tk)
            kb = group_norm(k_ref[pl.ds(r0, tk), :].astype(F32), kg_ref[...]).astype(BF16)
            kn[pl.ds(r0, tk), :] = kb
            vt[c, 0:DA_DV, :] = v_ref[pl.ds(r0, tk), :].astype(F32).T.astype(BF16)
            vt[c, DA_DV:DA_VROWS, :] = ones
            k2 = kb.astype(F32)
            k2 = k2 * k2
            lo = jnp.max(jnp.sum(jnp.where(lo_mask, k2, 0.0), axis=-1, keepdims=True), axis=0, keepdims=True)
            hi = jnp.max(jnp.sum(jnp.where(lo_mask, 0.0, k2), axis=-1, keepdims=True), axis=0, keepdims=True)
            return jnp.maximum(kmax2, jnp.where(lo_mask, lo, hi))
        kst[0:1, :] = lax.fori_loop(0, seq // tk, body, jnp.zeros((1, DA_DV), F32))
        b0 = bias_ref[0, 0]
        b1 = bias_ref[0, 1]
        bmax = jnp.maximum(jnp.max(jnp.maximum(b0, b1), axis=0, keepdims=True), 0.0)
        bmin = jnp.minimum(jnp.min(jnp.minimum(jnp.where(b0 > 0.5 * NEG, b0, 0.0), b1), axis=0, keepdims=True), 0.0)
        kst[1:2, :] = jnp.broadcast_to(jnp.max(bmax, axis=1, keepdims=True), (1, DA_DV))
        kst[2:3, :] = jnp.broadcast_to(jnp.min(bmin, axis=1, keepdims=True), (1, DA_DV))

    q = group_norm(q_ref[...].astype(F32), qg_ref[...]) * (DA_DH ** -0.5 * LOG2E)
    qcat = jnp.concatenate([jnp.where(lo_mask, q, 0.0), jnp.where(lo_mask, 0.0, q)], axis=0).astype(BF16)
    acc_s[...] = jnp.zeros_like(acc_s)

    q2 = q * q * kst[0:1, :]
    ones8 = jnp.ones((8, DA_DV), BF16)
    bmax = kst[1:2, 0:1]
    bmin = kst[2:3, 0:1]
    bound = []
    for m in range(2):
        q2m = jnp.where(lo_mask, q2, 0.0) if m == 0 else jnp.where(lo_mask, 0.0, q2)
        bound.append(jnp.sqrt(_dot_nt(ones8, q2m.astype(BF16))[0:1, :]) * BOUND_SLACK)
    bound = jnp.concatenate(bound, axis=1)
    worst = jnp.max(2.0 * bound, axis=1, keepdims=True) + bmax - bmin
    safe = worst[0, 0] <= MAX_SHIFT_GAP

    def block(j, d, fixed_shift):
        r0 = pl.multiple_of(j * tk, tk)
        st = _dot_nt(kn[pl.ds(r0, tk), :], qcat)
        if d is not None:
            bias = bias_ref[0, d]
            st = st + jnp.concatenate([bias, bias], axis=1)
        if fixed_shift:
            acc_s[...] = acc_s[...] + _dot(vt[j], jnp.exp2(st - m_s[...]).astype(BF16))
        else:
            m_prev = m_s[...]
            m_new = jnp.maximum(m_prev, jnp.max(st, axis=0, keepdims=True))
            alpha = jnp.exp2(m_prev - m_new)
            acc_s[...] = alpha * acc_s[...] + _dot(vt[j], jnp.exp2(st - m_new).astype(BF16))
            m_s[...] = m_new

    def run(fixed_shift):
        n_far = jnp.maximum(qi - 1, 0)

        def far_pair(jj, carry):
            block(2 * jj, None, fixed_shift)
            block(2 * jj + 1, None, fixed_shift)
            return carry

        lax.fori_loop(0, n_far // 2, far_pair, 0)

        @pl.when(n_far % 2 == 1)
        def _():
            block(n_far - 1, None, fixed_shift)

        @pl.when(qi >= 1)
        def _():
            block(qi - 1, 1, fixed_shift)

        block(qi, 0, fixed_shift)

    @pl.when(safe)
    def _():
        m_s[...] = bound + bmax
        run(True)

    @pl.when(jnp.logical_not(safe))
    def _():
        m_s[...] = jnp.full(m_s.shape, NEG, F32)
        run(False)

    lam_p = lam_ref[...]
    lam = (jnp.exp(jnp.sum(lam_p[0:1, :] * lam_p[1:2, :], axis=-1, keepdims=True))
           - jnp.exp(jnp.sum(lam_p[2:3, :] * lam_p[3:4, :], axis=-1, keepdims=True)) + LAM_INIT)
    a0 = acc_s[:, 0:tq]
    a1 = acc_s[:, tq:2 * tq]
    ot = a0[0:DA_DV] / a0[DA_DV:DA_DV + 1] - lam * (a1[0:DA_DV] / a1[DA_DV:DA_DV + 1])
    r = lax.rsqrt(jnp.sum(ot * ot, axis=0, keepdims=True) * (1.0 / DA_DV) + EPS)
    ot = ot * r * (sg_ref[...] * (1.0 - LAM_INIT))
    o_ref[...] = ot.T.astype(o_ref.dtype)


def _attention(p, bias, qg, kg, lam_p, subln, batch, seq):
    t = batch * seq
    tq = min(ATT_BLOCK, seq)
    nq = seq // tq
    dv = DA_DV
    return pl.pallas_call(
        _attn_body,
        grid=(batch, DA_HEADS, nq),
        in_specs=[
            pl.BlockSpec((tq, dv), lambda b, h, i: (b * nq + i, P_DAQ // dv + h)),
            pl.BlockSpec((seq, dv), lambda b, h, i: (b, P_DAK // dv + h)),
            pl.BlockSpec((seq, dv), lambda b, h, i: (b, P_DAV // dv + h)),
            pl.BlockSpec((1, 2, tq, tq), lambda b, h, i: (h, 0, 0, 0)),
            pl.BlockSpec((1, dv), lambda b, h, i: (0, 0)),
            pl.BlockSpec((1, dv), lambda b, h, i: (0, 0)),
            pl.BlockSpec((4, DA_DH), lambda b, h, i: (0, 0)),
            pl.BlockSpec((dv, 1), lambda b, h, i: (0, 0)),
        ],
        out_specs=pl.BlockSpec((tq, dv), lambda b, h, i: (b * nq + i, h)),
        out_shape=jax.ShapeDtypeStruct((t, DA_HEADS * dv), BF16),
        scratch_shapes=[
            pltpu.VMEM((seq, dv), BF16),
            pltpu.VMEM((seq // tq, DA_VROWS, tq), BF16),
            pltpu.VMEM((8, dv), F32),
            pltpu.VMEM((1, 2 * tq), F32),
            pltpu.VMEM((DA_VROWS, 2 * tq), F32),
        ],
        compiler_params=pltpu.CompilerParams(dimension_semantics=("parallel", "parallel", "arbitrary")),
        name="diff_attention",
    )(p, p, p, bias, qg, kg, lam_p, subln)


def _memkv_body(mem_ref, mg_ref, w_ref, kg_ref, mk_ref, mv_ref):
    x = mem_ref[0]
    xn = x * _rms(x, D_MODEL) * mg_ref[...]
    kv = _dot(xn.astype(BF16), w_ref[...])
    hw = MX_HEADS * MX_DH
    for h in range(MX_HEADS):
        sl = slice(h * MX_DH, (h + 1) * MX_DH)
        kh = kv[:, sl]
        mk_ref[0, :, sl] = (kh * _rms(kh, MX_DH) * kg_ref[...]).astype(BF16)
    mv_ref[0] = kv[:, hw:].astype(BF16)


def _memkv(mem, mem_norm, w_kv, k_norm):
    b, n, _ = mem.shape
    hw = MX_HEADS * MX_DH
    return pl.pallas_call(
        _memkv_body,
        grid=(b,),
        in_specs=[
            pl.BlockSpec((1, n, D_MODEL), lambda i: (i, 0, 0)),
            pl.BlockSpec((1, D_MODEL), lambda i: (0, 0)),
            pl.BlockSpec((D_MODEL, 2 * hw), lambda i: (0, 0)),
            pl.BlockSpec((1, MX_DH), lambda i: (0, 0)),
        ],
        out_specs=[pl.BlockSpec((1, n, hw), lambda i: (i, 0, 0))] * 2,
        out_shape=[jax.ShapeDtypeStruct((b, n, hw), BF16)] * 2,
        name="memory_kv",
    )(mem, mem_norm, w_kv, k_norm)


def _merge_body(x_ref, odn_ref, oda_ref, mxq_ref, g0_ref, g1_ref, g2_ref, mk_ref, mv_ref, qg_ref,
                bg_ref, wb_ref, wo_ref, out_ref, omx):
    for h in range(MX_HEADS):
        sl = slice(h * MX_DH, (h + 1) * MX_DH)
        qh = mxq_ref[:, sl].astype(F32)
        qh = qh * _rms(qh, MX_DH) * qg_ref[...] * (MX_DH ** -0.5 * LOG2E)
        s = _dot_nt(qh.astype(BF16), mk_ref[0, :, sl])
        p = jnp.exp2(s - jnp.max(s, axis=-1, keepdims=True))
        oh = _dot(p.astype(BF16), mv_ref[0, :, sl]) / jnp.sum(p, axis=-1, keepdims=True)
        omx[:, sl] = oh.astype(BF16)
    y = None
    for r, (o_r, g_r) in enumerate(((odn_ref, g0_ref), (oda_ref, g1_ref), (omx, g2_ref))):
        gate = _sigmoid(g_r[...].astype(F32) + bg_ref[r:r + 1, :])
        term = gate * _dot(o_r[...], wb_ref[r])
        y = term if y is None else y + term
    out_ref[...] = x_ref[...] + _dot(y.astype(BF16), wo_ref[...])


def _merge(x2, o_dn, o_da, p, mk, mv, q_norm, b_gate, w_branch, w_out, seq):
    t = x2.shape[0]
    tm = min(512, seq)
    nt = seq // tm
    bw = 512
    n_mem = mk.shape[1]
    return pl.pallas_call(
        _merge_body,
        grid=(t // tm,),
        in_specs=[
            pl.BlockSpec((tm, D_MODEL), lambda i: (i, 0)),
            pl.BlockSpec((tm, bw), lambda i: (i, 0)),
            pl.BlockSpec((tm, bw), lambda i: (i, 0)),
            pl.BlockSpec((tm, bw), lambda i: (i, P_MXQ // bw)),
            pl.BlockSpec((tm, D_MODEL), lambda i: (i, P_GATE // D_MODEL)),
            pl.BlockSpec((tm, D_MODEL), lambda i: (i, P_GATE // D_MODEL + 1)),
            pl.BlockSpec((tm, D_MODEL), lambda i: (i, P_GATE // D_MODEL + 2)),
            pl.BlockSpec((1, n_mem, bw), lambda i: (i // nt, 0, 0)),
            pl.BlockSpec((1, n_mem, bw), lambda i: (i // nt, 0, 0)),
            pl.BlockSpec((1, MX_DH), lambda i: (0, 0)),
            pl.BlockSpec((3, D_MODEL), lambda i: (0, 0)),
            pl.BlockSpec((3, bw, D_MODEL), lambda i: (0, 0, 0)),
            pl.BlockSpec((D_MODEL, D_MODEL), lambda i: (0, 0)),
        ],
        out_specs=pl.BlockSpec((tm, D_MODEL), lambda i: (i, 0)),
        out_shape=jax.ShapeDtypeStruct((t, D_MODEL), F32),
        scratch_shapes=[pltpu.VMEM((tm, bw), BF16)],
        compiler_params=pltpu.CompilerParams(dimension_semantics=("parallel",)),
        name="merge",
    )(x2, o_dn, o_da, p, p, p, p, mk, mv, q_norm, b_gate, w_branch, w_out)


def _router_body(x_ref, g_ref, wr_ref, br_ref, h_ref, idx_ref, wt_ref, rank_ref, cnt_ref):
    tm = x_ref.shape[0]
    x = x_ref[...]
    h = x * _rms(x, D_MODEL) * g_ref[...]
    h_ref[...] = h.astype(BF16)
    logits = _dot3(h, wr_ref[...]) + br_ref[...]
    lane = lax.broadcasted_iota(I32, (tm, LANES), 1)
    lane_f = lane.astype(F32)
    work = logits
    sel = jnp.zeros((tm, LANES), F32)
    vals, idxs = [], []
    for _ in range(TOP_K):
        mx = jnp.max(work, axis=-1, keepdims=True)
        ik = jnp.min(jnp.where(work == mx, lane_f, float(LANES)), axis=-1, keepdims=True)
        hit = lane_f == ik
        sel = jnp.where(hit, 1.0, sel)
        work = jnp.where(hit, -jnp.inf, work)
        vals.append(mx)
        idxs.append(ik)
    es = [jnp.exp(v - vals[0]) for v in vals]
    den = es[0] + es[1] + es[2] + es[3]
    sub = cnt_ref.shape[0]
    mt = tm // sub
    r = lax.broadcasted_iota(I32, (tm, tm), 0)
    c = lax.broadcasted_iota(I32, (tm, tm), 1)
    tril = jnp.where((r > c) & (r // mt == c // mt), 1.0, 0.0).astype(BF16)
    cum = _dot(tril, sel.astype(BF16))
    idx_o = jnp.zeros((tm, LANES), F32)
    wt_o = jnp.zeros((tm, LANES), F32)
    rank_o = jnp.zeros((tm, LANES), F32)
    for k in range(TOP_K):
        rk = jnp.sum(jnp.where(lane_f == idxs[k], cum, 0.0), axis=-1, keepdims=True)
        idx_o = jnp.where(lane == k, idxs[k], idx_o)
        wt_o = jnp.where(lane == k, es[k] / den, wt_o)
        rank_o = jnp.where(lane == k, rk, rank_o)
    idx_ref[...] = idx_o.astype(I32)
    wt_ref[...] = wt_o
    rank_ref[...] = rank_o.astype(I32)
    for s in range(sub):
        cnt_ref[s] = jnp.broadcast_to(jnp.sum(sel[s * mt:(s + 1) * mt], axis=0, keepdims=True), (8, LANES))


def _router(x1, gain, w_r, b_r):
    t = x1.shape[0]
    mt = min(MOE_TILE, t)
    sub = 2 if t % (2 * mt) == 0 else 1
    tm = sub * mt
    row = lambda i: (i, 0)
    fixed = lambda i: (0, 0)
    return pl.pallas_call(
        _router_body,
        grid=(t // tm,),
        in_specs=[
            pl.BlockSpec((tm, D_MODEL), row),
            pl.BlockSpec((1, D_MODEL), fixed),
            pl.BlockSpec((D_MODEL, LANES), fixed),
            pl.BlockSpec((1, LANES), fixed),
        ],
        out_specs=[
            pl.BlockSpec((tm, D_MODEL), row),
            pl.BlockSpec((tm, LANES), row),
            pl.BlockSpec((tm, LANES), row),
            pl.BlockSpec((tm, LANES), row),
            pl.BlockSpec((sub, 8, LANES), lambda i: (i, 0, 0)),
        ],
        out_shape=[
            jax.ShapeDtypeStruct((t, D_MODEL), BF16),
            jax.ShapeDtypeStruct((t, LANES), I32),
            jax.ShapeDtypeStruct((t, LANES), F32),
            jax.ShapeDtypeStruct((t, LANES), I32),
            jax.ShapeDtypeStruct((t // mt, 8, LANES), F32),
        ],
        compiler_params=pltpu.CompilerParams(dimension_semantics=("arbitrary",)),
        name="router",
    )(x1, gain, w_r, b_r)


def _lane_cumsum(x):
    lane = lax.broadcasted_iota(I32, x.shape, 1)
    s = 1
    while s < N_EXPERTS:
        x = x + jnp.where(lane >= s, pltpu.roll(x, s, axis=1), 0.0)
        s *= 2
    return x


def _plan_body(cnt_ref, gs_ref, ls_ref, n8_ref, lsf_ref, tail_ref, meta_ref):
    nt = cnt_ref.shape[0]
    ga = float(GROUP_ALIGN)
    eb = float(EXPERT_BLOCK)
    lane = lax.broadcasted_iota(I32, (nt, LANES), 1)
    r8 = jnp.where(lane < N_EXPERTS, jnp.floor((cnt_ref[...] + (ga - 1.0)) * (1.0 / ga)) * ga, 0.0)
    ri = lax.broadcasted_iota(I32, (nt, nt), 0)
    ci = lax.broadcasted_iota(I32, (nt, nt), 1)
    before = _dot(jnp.where(ri > ci, 1.0, 0.0).astype(BF16), r8.astype(BF16))
    tot = jnp.sum(r8, axis=0, keepdims=True)
    region = jnp.floor((tot + (eb - 1.0)) * (1.0 / eb)) * eb
    pends = _lane_cumsum(jnp.broadcast_to(region, (8, LANES)))[0:1, :]
    pstart = pends - region
    lstart = _lane_cumsum(r8) - r8
    gs_ref[...] = (pstart + before).astype(I32)
    ls_ref[...] = lstart.astype(I32)
    n8_ref[...] = (r8 * (1.0 / ga)).astype(I32)
    lsf_ref[...] = lstart
    row8 = lax.broadcasted_iota(I32, (8, LANES), 0)
    tail = jnp.where(row8 == 0, pstart + tot, jnp.where(row8 == 1, (region - tot) * (1.0 / ga), 0.0))
    tail_ref[...] = tail.astype(I32)
    nb = meta_ref.shape[0]
    ln = lax.broadcasted_iota(I32, (nb, LANES), 1)
    blk = lax.broadcasted_iota(I32, (nb, LANES), 0).astype(F32) * eb
    be = jnp.sum(jnp.where((ln < N_EXPERTS) & (pends <= blk), 1.0, 0.0), axis=-1, keepdims=True)
    be = jnp.minimum(be, float(N_EXPERTS - 1))
    used = jnp.sum(jnp.where(ln == N_EXPERTS - 1, pends, 0.0), axis=-1, keepdims=True) * (1.0 / eb)
    meta_ref[...] = jnp.where(ln == 0, be, jnp.where(ln == 1, used, 0.0)).astype(I32)


def _plan(cnt, n_blocks_pad):
    nt = cnt.shape[0]
    shp = jax.ShapeDtypeStruct((nt, LANES), I32)
    return pl.pallas_call(
        _plan_body,
        out_shape=[shp, shp, shp, jax.ShapeDtypeStruct((nt, LANES), F32),
                   jax.ShapeDtypeStruct((8, LANES), I32),
                   jax.ShapeDtypeStruct((n_blocks_pad, LANES), I32)],
        name="dispatch_plan",
    )(cnt)


def _local_positions(idx_ref, rank_ref, lsf_ref):
    tm = idx_ref.shape[0]
    lane = lax.broadcasted_iota(I32, (tm, LANES), 1)
    idx = idx_ref[...]
    rank = rank_ref[...].astype(F32)
    ls_row = lsf_ref[0]
    out = []
    for k in range(TOP_K):
        base = jnp.sum(jnp.where(lane == idx[:, k:k + 1], ls_row, 0.0), axis=-1, keepdims=True)
        out.append((base + rank[:, k:k + 1]).astype(I32))
    return out


def _local_positions_lanes(idx_ref, rank_ref, lsf_ref):
    tm = idx_ref.shape[0]
    idx_t = idx_ref[...].astype(F32).T
    rank_t = rank_ref[...].astype(F32).T
    ls_col = jnp.broadcast_to(lsf_ref[0], (LANES, LANES)).T[:, 0:1]
    expert = lax.broadcasted_iota(I32, (LANES, tm), 0).astype(F32)
    out = []
    for k in range(TOP_K):
        base = jnp.sum(jnp.where(expert == idx_t[k:k + 1, :], ls_col, 0.0), axis=0, keepdims=True)
        out.append((base + rank_t[k:k + 1, :]).astype(I32))
    return out


PACK_W = D_MODEL // 2
U32 = jnp.uint32


def _pack_rows(x):
    xb = x.astype(BF16).astype(F32)
    hi = lax.bitcast_convert_type(xb[:, :PACK_W], U32)
    lo = lax.bitcast_convert_type(xb[:, PACK_W:], U32)
    return hi | (lo >> 16)


def _unpack_rows(w):
    hi = lax.bitcast_convert_type(w & jnp.uint32(0xFFFF0000), F32)
    lo = lax.bitcast_convert_type(w << 16, F32)
    return hi.astype(BF16), lo.astype(BF16)


def _group_copies(tables, tile, make_copy, start):
    gs_ref, ls_ref, n8_ref = tables

    def group(e, carry):
        g0 = gs_ref[tile, e]
        l0 = ls_ref[tile, e]

        def one(j, c):
            cp = make_copy(pl.multiple_of(l0 + j * GROUP_ALIGN, GROUP_ALIGN),
                           pl.multiple_of(g0 + j * GROUP_ALIGN, GROUP_ALIGN))
            cp.start() if start else cp.wait()
            return c

        lax.fori_loop(0, n8_ref[tile, e], one, 0)
        return carry

    lax.fori_loop(0, N_EXPERTS, group, 0)


def _dispatch_body(gs_ref, ls_ref, n8_ref, tail_ref, h_ref, idx_ref, rank_ref, lsf_ref, xs_ref,
                   xl, zbuf, sem):
    i = pl.program_id(0)
    last = i == pl.num_programs(0) - 1
    slot = i % 2
    tables = (gs_ref, ls_ref, n8_ref)
    tm = h_ref.shape[0]
    lrows = xl.shape[1]
    pos = lax.broadcasted_iota(I32, (lrows, tm), 0)
    lp = _local_positions_lanes(idx_ref, rank_ref, lsf_ref)
    hit = pos == lp[0]
    for k in range(1, TOP_K):
        hit = hit | (pos == lp[k])
    xl[slot] = _pack_rows(_dot(jnp.where(hit, 1.0, 0.0).astype(BF16), h_ref[...]))

    def copy_from(s):
        def make_copy(l, g):
            return pltpu.make_async_copy(xl.at[s, pl.ds(l, GROUP_ALIGN), :],
                                         xs_ref.at[pl.ds(g, GROUP_ALIGN), :], sem.at[s])
        return make_copy

    @pl.when(i >= 1)
    def _():
        _group_copies(tables, i - 1, copy_from(1 - slot), False)

    _group_copies(tables, i, copy_from(slot), True)

    def tails(start):
        def per_expert(e, carry):
            g0 = tail_ref[0, e]

            def one(j, c):
                cp = pltpu.make_async_copy(
                    zbuf, xs_ref.at[pl.ds(pl.multiple_of(g0 + j * GROUP_ALIGN, GROUP_ALIGN), GROUP_ALIGN), :],
                    sem.at[2])
                cp.start() if start else cp.wait()
                return c

            lax.fori_loop(0, tail_ref[1, e], one, 0)
            return carry

        lax.fori_loop(0, N_EXPERTS, per_expert, 0)

    @pl.when(last)
    def _():
        zbuf[...] = jnp.zeros_like(zbuf)
        tails(True)
        _group_copies(tables, i, copy_from(slot), False)
        tails(False)


def _dispatch(gs, ls, n8, tail, h2, idx, rank, lsf, n_slots):
    t = h2.shape[0]
    tm = min(MOE_TILE, t)
    nt = t // tm
    lrows = tm * TOP_K + N_EXPERTS * GROUP_ALIGN
    row = lambda i, *_: (i, 0)
    grid_spec = pltpu.PrefetchScalarGridSpec(
        num_scalar_prefetch=4,
        grid=(nt,),
        in_specs=[
            pl.BlockSpec((tm, D_MODEL), row),
            pl.BlockSpec((tm, LANES), row),
            pl.BlockSpec((tm, LANES), row),
            pl.BlockSpec((1, 1, LANES), lambda i, *_: (i, 0, 0)),
        ],
        out_specs=pl.BlockSpec(memory_space=pl.ANY),
        scratch_shapes=[pltpu.VMEM((2, lrows, PACK_W), U32), pltpu.VMEM((GROUP_ALIGN, PACK_W), U32),
                        pltpu.SemaphoreType.DMA((3,))],
    )
    return pl.pallas_call(
        _dispatch_body,
        grid_spec=grid_spec,
        out_shape=jax.ShapeDtypeStruct((n_slots, PACK_W), U32),
        compiler_params=pltpu.CompilerParams(dimension_semantics=("arbitrary",)),
        name="moe_dispatch",
    )(gs, ls, n8, tail, h2, idx, rank, lsf)


def _expert_body(be_ref, nu_ref, x_ref, wgu_ref, bgu_ref, wd_ref, bd_ref, y_ref, wgu_b, wd_b):
    j = pl.program_id(0)

    @pl.when(j < nu_ref[0])
    def _():
        @pl.when(jnp.logical_or(j == 0, be_ref[j] != be_ref[jnp.maximum(j - 1, 0)]))
        def _():
            wgu_b[...] = wgu_ref[0].astype(BF16)
            wd_b[...] = wd_ref[0].astype(BF16)

        gu = _dot(jnp.concatenate(_unpack_rows(x_ref[...]), axis=1), wgu_b[...]) + bgu_ref[0]
        gate = jnp.minimum(gu[:, :D_FF], SWIGLU_LIMIT)
        up = jnp.clip(gu[:, D_FF:], -SWIGLU_LIMIT, SWIGLU_LIMIT)
        act = (up + 1.0) * gate * _sigmoid(SWIGLU_ALPHA * gate)
        y_ref[...] = _pack_rows(_dot(act.astype(BF16), wd_b[...]) + bd_ref[0])


def _experts(block_e, n_used, xs, wgu, bgu, wd, bd):
    n_slots = xs.shape[0]
    nb = n_slots // EXPERT_BLOCK

    def blk(j, be, nu):
        return jnp.minimum(j, nu[0] - 1)

    grid_spec = pltpu.PrefetchScalarGridSpec(
        num_scalar_prefetch=2,
        grid=(nb,),
        in_specs=[
            pl.BlockSpec((EXPERT_BLOCK, PACK_W), lambda j, be, nu: (blk(j, be, nu), 0)),
            pl.BlockSpec((1, D_MODEL, 2 * D_FF), lambda j, be, nu: (be[blk(j, be, nu)], 0, 0)),
            pl.BlockSpec((1, 1, 2 * D_FF), lambda j, be, nu: (be[blk(j, be, nu)], 0, 0)),
            pl.BlockSpec((1, D_FF, D_MODEL), lambda j, be, nu: (be[blk(j, be, nu)], 0, 0)),
            pl.BlockSpec((1, 1, D_MODEL), lambda j, be, nu: (be[blk(j, be, nu)], 0, 0)),
        ],
        out_specs=pl.BlockSpec((EXPERT_BLOCK, PACK_W), lambda j, be, nu: (blk(j, be, nu), 0)),
        scratch_shapes=[pltpu.VMEM((D_MODEL, 2 * D_FF), BF16), pltpu.VMEM((D_FF, D_MODEL), BF16)],
    )
    return pl.pallas_call(
        _expert_body,
        grid_spec=grid_spec,
        out_shape=jax.ShapeDtypeStruct((n_slots, PACK_W), U32),
        compiler_params=pltpu.CompilerParams(dimension_semantics=("arbitrary",)),
        name="moe_experts",
    )(block_e, n_used, xs, wgu, bgu, wd, bd)


def _combine_body(gs_ref, ls_ref, n8_ref, x_ref, wt_ref, idx_ref, rank_ref, lsf_ref, y_ref, out_ref,
                  yl, sem):
    i = pl.program_id(0)
    slot = i % 2
    tables = (gs_ref, ls_ref, n8_ref)
    tm = x_ref.shape[0]
    lrows = yl.shape[1]

    def copy_into(s):
        def make_copy(l, g):
            return pltpu.make_async_copy(y_ref.at[pl.ds(g, GROUP_ALIGN), :],
                                         yl.at[s, pl.ds(l, GROUP_ALIGN), :], sem.at[s])
        return make_copy

    @pl.when(i == 0)
    def _():
        yl[...] = jnp.zeros_like(yl)
        _group_copies(tables, i, copy_into(slot), True)

    @pl.when(i + 1 < pl.num_programs(0))
    def _():
        _group_copies(tables, i + 1, copy_into(1 - slot), True)

    pos = lax.broadcasted_iota(I32, (tm, lrows), 1)
    lp = _local_positions(idx_ref, rank_ref, lsf_ref)
    wt = wt_ref[...]
    wm = jnp.zeros((tm, lrows), F32)
    for k in range(TOP_K):
        wm = jnp.where(pos == lp[k], wt[:, k:k + 1], wm)
    _group_copies(tables, i, copy_into(slot), False)
    wmb = wm.astype(BF16)
    y_hi, y_lo = _unpack_rows(yl[slot])
    out_ref[:, :PACK_W] = x_ref[:, :PACK_W] + _dot(wmb, y_hi)
    out_ref[:, PACK_W:] = x_ref[:, PACK_W:] + _dot(wmb, y_lo)


def _combine(gs, ls, n8, x1, wts, idx, rank, lsf, y):
    t = x1.shape[0]
    tm = min(MOE_TILE, t)
    nt = t // tm
    lrows = tm * TOP_K + N_EXPERTS * GROUP_ALIGN
    row = lambda i, *_: (i, 0)
    grid_spec = pltpu.PrefetchScalarGridSpec(
        num_scalar_prefetch=3,
        grid=(nt,),
        in_specs=[
            pl.BlockSpec((tm, D_MODEL), row),
            pl.BlockSpec((tm, LANES), row),
            pl.BlockSpec((tm, LANES), row),
            pl.BlockSpec((tm, LANES), row),
            pl.BlockSpec((1, 1, LANES), lambda i, *_: (i, 0, 0)),
            pl.BlockSpec(memory_space=pl.ANY),
        ],
        out_specs=pl.BlockSpec((tm, D_MODEL), row),
        scratch_shapes=[pltpu.VMEM((2, lrows, PACK_W), U32), pltpu.SemaphoreType.DMA((2,))],
    )
    return pl.pallas_call(
        _combine_body,
        grid_spec=grid_spec,
        out_shape=jax.ShapeDtypeStruct((t, D_MODEL), F32),
        compiler_params=pltpu.CompilerParams(dimension_semantics=("arbitrary",)),
        name="moe_combine",
    )(gs, ls, n8, x1, wts, idx, rank, lsf, y)


def _pad_lanes(v, fill=0.0):
    v = v.astype(F32).reshape(1, -1)
    return jnp.pad(v, ((0, 0), (0, LANES - v.shape[1])), constant_values=fill)


def _mixer(x2, mem, rel_table, attn_norm, w_in, b_gate, dn_conv, dn_a_log, dn_dt_bias, dn_out_norm,
           da_q_norm, da_k_norm, da_lambda, da_subln, mem_norm, w_mem_kv, mx_q_norm, mx_k_norm,
           w_branch, w_out, batch, seq):
    wp = jnp.concatenate([w_in[:, :W_AB_LO], w_in[:, W_AB_HI:]], axis=1).astype(BF16)
    wab = jnp.pad(w_in[:, W_AB_LO:W_AB_HI], ((0, 0), (0, LANES - (W_AB_HI - W_AB_LO))))
    p, ab = _inproj(x2, attn_norm.reshape(1, -1), wp, wab)

    o_dn = _deltanet(p, ab, dn_conv, _pad_lanes(dn_a_log), _pad_lanes(dn_dt_bias),
                     dn_out_norm.reshape(1, -1), batch, seq)

    tq = min(ATT_BLOCK, seq)
    bias = _bias_tiles(rel_table.T, tq)
    o_da = _attention(p, bias, jnp.tile(da_q_norm, 2).reshape(1, -1), jnp.tile(da_k_norm, 2).reshape(1, -1),
                      da_lambda, da_subln.reshape(-1, 1), batch, seq)

    mk, mv = _memkv(mem, mem_norm.reshape(1, -1), w_mem_kv.astype(BF16), mx_k_norm.reshape(1, -1))
    return _merge(x2, o_dn, o_da, p, mk, mv, mx_q_norm.reshape(1, -1), b_gate.reshape(3, D_MODEL),
                  w_branch.astype(BF16), w_out.astype(BF16), seq)


def _moe(x1, ffn_norm, w_router, b_router, w_gate_up, b_gate_up, w_down, b_down):
    t = x1.shape[0]
    nt = t // min(MOE_TILE, t)
    max_rows = t * TOP_K + nt * N_EXPERTS * (GROUP_ALIGN - 1)
    n_blocks = -(-max_rows // EXPERT_BLOCK) + N_EXPERTS
    n_blocks_pad = -(-n_blocks // 8) * 8
    n_slots = n_blocks * EXPERT_BLOCK

    wr = jnp.pad(w_router, ((0, 0), (0, LANES - N_EXPERTS)))
    h2, idx, wts, rank, cnt = _router(x1, ffn_norm.reshape(1, -1), wr, _pad_lanes(b_router, NEG))
    gs, ls, n8, lsf, tail, meta = _plan(cnt[:, 0, :], n_blocks_pad)
    block_e = meta[:n_blocks, 0]
    n_used = meta[0:1, 1]
    lsf = lsf.reshape(nt, 1, LANES)

    xs = _dispatch(gs, ls, n8, tail, h2, idx, rank, lsf, n_slots)
    y = _experts(block_e, n_used, xs, w_gate_up, b_gate_up.reshape(N_EXPERTS, 1, -1),
                 w_down, b_down.reshape(N_EXPERTS, 1, -1))
    return _combine(gs, ls, n8, x1, wts, idx, rank, lsf, y)


def kernel(x, mem, rel_table, attn_norm, w_in, b_gate, dn_conv, dn_a_log, dn_dt_bias, dn_out_norm,
           da_q_norm, da_k_norm, da_lambda, da_subln, mem_norm, w_mem_kv, mx_q_norm, mx_k_norm,
           w_branch, w_out, ffn_norm, w_router, b_router, w_gate_up, b_gate_up, w_down, b_down):
    batch, seq, d = x.shape
    x2 = x.reshape(batch * seq, d)
    x1 = _mixer(x2, mem, rel_table, attn_norm[0], w_in[0], b_gate[0], dn_conv[0], dn_a_log[0],
                dn_dt_bias[0], dn_out_norm[0], da_q_norm[0], da_k_norm[0], da_lambda[0], da_subln[0],
                mem_norm[0], w_mem_kv[0], mx_q_norm[0], mx_k_norm[0], w_branch[0], w_out[0], batch, seq)
    out = _moe(x1, ffn_norm[0], w_router[0], b_router[0], w_gate_up[0], b_gate_up[0], w_down[0],
               b_down[0])
    return out.reshape(batch, seq, d)
```

```python
import functools
import math

import jax
import jax.numpy as jnp
from jax import lax
from jax.experimental import pallas as pl
from jax.experimental.pallas import tpu as pltpu

F32 = jnp.float32
BF16 = jnp.bfloat16
I32 = jnp.int32

D_MODEL = 1024
EPS = 1e-6
LANES = 128

DN_HEADS = 4
DN_DK = 128
DN_CHUNK = 64
DN_CONV = 4

DA_HEADS = 4
DA_DH = 64

MX_HEADS = 4
MX_DH = 128

REL_BUCKETS = 32
REL_MAX_DIST = 128

N_EXPERTS = 32
TOP_K = 4
D_FF = 1024
SWIGLU_LIMIT = 7.0
SWIGLU_ALPHA = 1.702
EXPERT_BLOCK = 512
MOE_TILE = 256
GROUP_ALIGN = 8

LAM_INIT = 0.8 - 0.6 * math.exp(-0.3 * 0)
LOG2E = 1.4426950408889634
NEG = -1e30

P_DNQ, P_DNK, P_DNV, P_DNZ = 0, 512, 1024, 1536
P_DAQ, P_DAK, P_DAV = 2048, 2560, 3072
P_MXQ = 3584
P_GATE = 4096
P_COLS = 7168
W_AB_LO, W_AB_HI = 2048, 2056


def _dot(a, b):
    return jnp.dot(a, b, preferred_element_type=F32)


def _dot_nt(a, b):
    return lax.dot_general(a, b, (((1,), (1,)), ((), ())), preferred_element_type=F32)


def _dot_tn(a, b):
    return lax.dot_general(a, b, (((0,), (0,)), ((), ())), preferred_element_type=F32)


def _split(x):
    hi = x.astype(BF16)
    lo = (x - hi.astype(F32)).astype(BF16)
    return hi, lo


def _dot3(a, b):
    ah, al = _split(a)
    bh, bl = _split(b)
    return _dot(ah, bh) + _dot(ah, bl) + _dot(al, bh)


def _sigmoid(x):
    return 1.0 / (1.0 + jnp.exp(-x))


def _rms(x, n):
    return lax.rsqrt(jnp.sum(x * x, axis=-1, keepdims=True) * (1.0 / n) + EPS)


def _inproj_body(x_ref, g_ref, w_ref, wab_ref, p_ref, ab_ref, h_scr):
    @pl.when(pl.program_id(1) == 0)
    def _():
        x = x_ref[...]
        h = x * _rms(x, D_MODEL) * g_ref[...]
        h_scr[...] = h.astype(BF16)
        ab_ref[...] = _dot3(h, wab_ref[...])

    p_ref[...] = _dot(h_scr[...], w_ref[...]).astype(p_ref.dtype)


def _inproj(x2, gain, wp, wab):
    t = x2.shape[0]
    tm = min(1024, t)
    tn = 1024
    return pl.pallas_call(
        _inproj_body,
        grid=(t // tm, P_COLS // tn),
        in_specs=[
            pl.BlockSpec((tm, D_MODEL), lambda i, j: (i, 0)),
            pl.BlockSpec((1, D_MODEL), lambda i, j: (0, 0)),
            pl.BlockSpec((D_MODEL, tn), lambda i, j: (0, j)),
            pl.BlockSpec((D_MODEL, LANES), lambda i, j: (0, 0)),
        ],
        out_specs=[
            pl.BlockSpec((tm, tn), lambda i, j: (i, j)),
            pl.BlockSpec((tm, LANES), lambda i, j: (i, 0)),
        ],
        out_shape=[
            jax.ShapeDtypeStruct((t, P_COLS), BF16),
            jax.ShapeDtypeStruct((t, LANES), F32),
        ],
        scratch_shapes=[pltpu.VMEM((tm, D_MODEL), BF16)],
        compiler_params=pltpu.CompilerParams(dimension_semantics=("parallel", "arbitrary")),
        name="inproj",
    )(x2, gain, wp, wab)


DN_HALO = 16
DN_SCAN_CHUNK = 256


def _deltanet_body(q_ref, k_ref, v_ref, z_ref, qh_ref, kh_ref, vh_ref, ab_ref, cw_ref, alog_ref,
                   dtb_ref, on_ref, o_ref, stage, qs, ks, vs, s_scr):
    i = pl.program_id(1)
    tc = q_ref.shape[0]
    hw = DN_HEADS * DN_DK

    @pl.when(i == 0)
    def _():
        s_scr[...] = jnp.zeros_like(s_scr)

    for src, halo, dst, off, kind in ((q_ref, qh_ref, qs, 0, "q"), (k_ref, kh_ref, ks, hw, "k"),
                                      (v_ref, vh_ref, vs, 2 * hw, "v")):
        hal = halo[...].astype(F32)
        stage[0:DN_HALO, :] = jnp.where(i == 0, 0.0, hal)
        stage[DN_HALO:DN_HALO + tc, :] = src[...].astype(F32)
        base = DN_HALO - (DN_CONV - 1)
        y = stage[base:base + tc, :] * cw_ref[0:1, off:off + hw]
        for j in range(1, DN_CONV):
            y = y + stage[base + j:base + j + tc, :] * cw_ref[j:j + 1, off:off + hw]
        y = y * _sigmoid(y)
        if kind == "v":
            dst[...] = y
        else:
            for h in range(DN_HEADS):
                sl = slice(h * DN_DK, (h + 1) * DN_DK)
                yh = y[:, sl]
                r = lax.rsqrt(jnp.sum(yh * yh, axis=-1, keepdims=True) + EPS)
                if kind == "q":
                    r = r * (DN_DK ** -0.5)
                dst[:, sl] = yh * r

    c = min(DN_SCAN_CHUNK, tc)
    row = lax.broadcasted_iota(I32, (c, c), 0)
    col = lax.broadcasted_iota(I32, (c, c), 1)
    incl = row >= col
    strict = row > col
    same_blk = (row // DN_CHUNK) == (col // DN_CHUNK)
    tri = jnp.where(incl, 1.0, 0.0).astype(BF16)
    eye = jnp.where(row == col, 1.0, 0.0)
    neg_a = -jnp.exp(alog_ref[...])
    dtb = dtb_ref[...]

    def chunk(ci, carry):
        r0 = pl.multiple_of(ci * c, c)
        abc = ab_ref[pl.ds(r0, c), :]
        a_in = abc + dtb
        g_all = neg_a * (jnp.maximum(a_in, 0.0) + jnp.log(1.0 + jnp.exp(-jnp.abs(a_in))))
        beta_all = _sigmoid(abc)
        zc = z_ref[pl.ds(r0, c), :].astype(F32)
        for h in range(DN_HEADS):
            sl = slice(h * DN_DK, (h + 1) * DN_DK)
            q = qs[pl.ds(r0, c), sl]
            k = ks[pl.ds(r0, c), sl]
            v = vs[pl.ds(r0, c), sl]
            g = g_all[:, h:h + 1]
            beta = beta_all[:, DN_HEADS + h:DN_HEADS + h + 1]
            g_hi, g_lo = _split(jnp.where(strict, g, 0.0))
            diff = _dot(tri, g_hi) + _dot(tri, g_lo)
            gc = diff[:, 0:1] + g[0:1, :]
            decay = jnp.where(incl, jnp.exp(diff), 0.0)
            kb = k.astype(BF16)
            qkk = _dot_nt(jnp.concatenate([q.astype(BF16), kb], axis=0), kb)
            qk = qkk[:c]
            kk = qkk[c:]
            lower = jnp.where(strict, kk * decay * beta, 0.0)
            pw = jnp.where(same_blk, -lower, 0.0)
            dinv = eye + pw
            for _ in range(int(math.log2(DN_CHUNK)) - 1):
                pwb = pw.astype(BF16)
                pw = _dot(pwb, pwb)
                dinv = dinv + _dot(dinv.astype(BF16), pw.astype(BF16))
            dinv_b = dinv.astype(BF16)
            pw = -_dot(dinv_b, jnp.where(same_blk, 0.0, lower).astype(BF16))
            xm = eye + pw
            for _ in range(int(math.log2(c // DN_CHUNK)) - 1):
                pwb = pw.astype(BF16)
                pw = _dot(pwb, pwb)
                xm = xm + _dot(xm.astype(BF16), pw.astype(BF16))
            inv = _dot(xm.astype(BF16), dinv_b)
            egc = jnp.exp(gc)
            rhs = jnp.concatenate([v * beta, k * (beta * egc)], axis=1)
            sol = _dot(inv.astype(BF16), rhs.astype(BF16))
            u = sol[:, :DN_DK]
            w = sol[:, DN_DK:]
            qkm = jnp.where(incl, qk * decay, 0.0)
            gl = gc[c - 1:c, :]
            state = s_scr[h]
            sb = state.astype(BF16)
            ws = _dot(jnp.concatenate([w.astype(BF16), (q * egc).astype(BF16)], axis=0), sb)
            v_new = u - ws[:c]
            o = ws[c:] + _dot(qkm.astype(BF16), v_new.astype(BF16))
            s_scr[h] = state * jnp.exp(gl) + _dot_tn(kb, (v_new * jnp.exp(gl - gc)).astype(BF16))
            zz = zc[:, sl]
            on = o * _rms(o, DN_DK) * on_ref[...]
            o_ref[pl.ds(r0, c), sl] = (on * (zz * _sigmoid(zz))).astype(o_ref.dtype)
        return carry

    lax.fori_loop(0, tc // c, chunk, 0, unroll=True)


def _deltanet(p, ab, conv_w, alog_row, dtb_row, out_norm, batch, seq):
    t = batch * seq
    tc = min(512, seq)
    nt = seq // tc
    hw = DN_HEADS * DN_DK

    def main(cb):
        return pl.BlockSpec((tc, hw), lambda b, i: (b * nt + i, cb))

    def halo(cb):
        return pl.BlockSpec(
            (DN_HALO, hw),
            lambda b, i: (jnp.maximum((b * seq + i * tc) // DN_HALO - 1, 0), cb))

    return pl.pallas_call(
        _deltanet_body,
        grid=(batch, nt),
        in_specs=[
            main(P_DNQ // hw), main(P_DNK // hw), main(P_DNV // hw), main(P_DNZ // hw),
            halo(P_DNQ // hw), halo(P_DNK // hw), halo(P_DNV // hw),
            pl.BlockSpec((tc, LANES), lambda b, i: (b * nt + i, 0)),
            pl.BlockSpec((DN_CONV, 3 * hw), lambda b, i: (0, 0)),
            pl.BlockSpec((1, LANES), lambda b, i: (0, 0)),
            pl.BlockSpec((1, LANES), lambda b, i: (0, 0)),
            pl.BlockSpec((1, DN_DK), lambda b, i: (0, 0)),
        ],
        out_specs=pl.BlockSpec((tc, hw), lambda b, i: (b * nt + i, 0)),
        out_shape=jax.ShapeDtypeStruct((t, hw), BF16),
        scratch_shapes=[
            pltpu.VMEM((DN_HALO + tc, hw), F32),
            pltpu.VMEM((tc, hw), F32),
            pltpu.VMEM((tc, hw), F32),
            pltpu.VMEM((tc, hw), F32),
            pltpu.VMEM((DN_HEADS, DN_DK, DN_DK), F32),
        ],
        compiler_params=pltpu.CompilerParams(dimension_semantics=("parallel", "arbitrary")),
        name="deltanet",
    )(p, p, p, p, p, p, p, ab, conv_w, alog_row, dtb_row, out_norm)


ATT_BLOCK = 512


def _bias_body(tbl_ref, o_ref):
    h = pl.program_id(0)
    tq = o_ref.shape[2]
    key = lax.broadcasted_iota(I32, (tq, tq), 0)
    qry = lax.broadcasted_iota(I32, (tq, tq), 1)
    max_exact = REL_BUCKETS // 2
    far = tbl_ref[h, REL_BUCKETS - 1]
    for d in range(2):
        n = qry - key + d * tq
        nn = jnp.maximum(n, 0)
        nf = jnp.maximum(nn, 1).astype(F32)
        large = max_exact + (jnp.log(nf / max_exact) / math.log(REL_MAX_DIST / max_exact)
                             * (REL_BUCKETS - max_exact)).astype(I32)
        large = jnp.minimum(large, REL_BUCKETS - 1)
        bucket = jnp.where(nn < max_exact, nn, large)
        val = jnp.zeros((tq, tq), F32)
        for b in range(REL_BUCKETS):
            val = jnp.where(bucket == b, tbl_ref[h, b], val)
        o_ref[0, d] = jnp.where(n >= 0, (val - far) * LOG2E, NEG)


def _bias_tiles(tbl_t, tq):
    return pl.pallas_call(
        _bias_body,
        grid=(DA_HEADS,),
        in_specs=[pl.BlockSpec(memory_space=pltpu.SMEM)],
        out_specs=pl.BlockSpec((1, 2, tq, tq), lambda h: (h, 0, 0, 0)),
        out_shape=jax.ShapeDtypeStruct((DA_HEADS, 2, tq, tq), F32),
        name="t5_bias_tiles",
    )(tbl_t)


DA_DV = 2 * DA_DH
DA_VROWS = DA_DV + 16


BOUND_SLACK = 1.02
MAX_SHIFT_GAP = 110.0


def _attn_body(q_ref, k_ref, v_ref, bias_ref, qg_ref, kg_ref, lam_ref, sg_ref, o_ref,
               kn, vt, kst, m_s, acc_s):
    qi = pl.program_id(2)
    tq = q_ref.shape[0]
    seq = k_ref.shape[0]
    tk = tq
    lo_mask = lax.broadcasted_iota(I32, (1, DA_DV), 1) < DA_DH

    def group_norm(x, gain):
        x2 = x * x
        lo = jnp.sum(jnp.where(lo_mask, x2, 0.0), axis=-1, keepdims=True)
        hi = jnp.sum(jnp.where(lo_mask, 0.0, x2), axis=-1, keepdims=True)
        r = jnp.where(lo_mask, lax.rsqrt(lo * (1.0 / DA_DH) + EPS), lax.rsqrt(hi * (1.0 / DA_DH) + EPS))
        return x * r * gain

    @pl.when(qi == 0)
    def _():
        ones = jnp.ones((DA_VROWS - DA_DV, tk), BF16)

        def body(c, kmax2):
            r0 = pl.multiple_of(c * tk, tk)
            kb = group_norm(k_ref[pl.ds(r0, tk), :].astype(F32), kg_ref[...]).astype(BF16)
            kn[pl.ds(r0, tk), :] = kb
            vt[c, 0:DA_DV, :] = v_ref[pl.ds(r0, tk), :].astype(F32).T.astype(BF16)
            vt[c, DA_DV:DA_VROWS, :] = ones
            k2 = kb.astype(F32)
            k2 = k2 * k2
            lo = jnp.max(jnp.sum(jnp.where(lo_mask, k2, 0.0), axis=-1, keepdims=True), axis=0, keepdims=True)
            hi = jnp.max(jnp.sum(jnp.where(lo_mask, 0.0, k2), axis=-1, keepdims=True), axis=0, keepdims=True)
            return jnp.maximum(kmax2, jnp.where(lo_mask, lo, hi))
        kst[0:1, :] = lax.fori_loop(0, seq // tk, body, jnp.zeros((1, DA_DV), F32))
        b0 = bias_ref[0, 0]
        b1 = bias_ref[0, 1]
        bmax = jnp.maximum(jnp.max(jnp.maximum(b0, b1), axis=0, keepdims=True), 0.0)
        bmin = jnp.minimum(jnp.min(jnp.minimum(jnp.where(b0 > 0.5 * NEG, b0, 0.0), b1), axis=0, keepdims=True), 0.0)
        kst[1:2, :] = jnp.broadcast_to(jnp.max(bmax, axis=1, keepdims=True), (1, DA_DV))
        kst[2:3, :] = jnp.broadcast_to(jnp.min(bmin, axis=1, keepdims=True), (1, DA_DV))

    q = group_norm(q_ref[...].astype(F32), qg_ref[...]) * (DA_DH ** -0.5 * LOG2E)
    qcat = jnp.concatenate([jnp.where(lo_mask, q, 0.0), jnp.where(lo_mask, 0.0, q)], axis=0).astype(BF16)
    acc_s[...] = jnp.zeros_like(acc_s)

    q2 = q * q * kst[0:1, :]
    ones8 = jnp.ones((8, DA_DV), BF16)
    bmax = kst[1:2, 0:1]
    bmin = kst[2:3, 0:1]
    bound = []
    for m in range(2):
        q2m = jnp.where(lo_mask, q2, 0.0) if m == 0 else jnp.where(lo_mask, 0.0, q2)
        bound.append(jnp.sqrt(_dot_nt(ones8, q2m.astype(BF16))[0:1, :]) * BOUND_SLACK)
    bound = jnp.concatenate(bound, axis=1)
    worst = jnp.max(2.0 * bound, axis=1, keepdims=True) + bmax - bmin
    safe = worst[0, 0] <= MAX_SHIFT_GAP

    def block(j, d, fixed_shift):
        r0 = pl.multiple_of(j * tk, tk)
        st = _dot_nt(kn[pl.ds(r0, tk), :], qcat)
        if d is not None:
            bias = bias_ref[0, d]
            st = st + jnp.concatenate([bias, bias], axis=1)
        if fixed_shift:
            acc_s[...] = acc_s[...] + _dot(vt[j], jnp.exp2(st - m_s[...]).astype(BF16))
        else:
            m_prev = m_s[...]
            m_new = jnp.maximum(m_prev, jnp.max(st, axis=0, keepdims=True))
            alpha = jnp.exp2(m_prev - m_new)
            acc_s[...] = alpha * acc_s[...] + _dot(vt[j], jnp.exp2(st - m_new).astype(BF16))
            m_s[...] = m_new

    def run(fixed_shift):
        n_far = jnp.maximum(qi - 1, 0)

        def far_pair(jj, carry):
            block(2 * jj, None, fixed_shift)
            block(2 * jj + 1, None, fixed_shift)
            return carry

        lax.fori_loop(0, n_far // 2, far_pair, 0)

        @pl.when(n_far % 2 == 1)
        def _():
            block(n_far - 1, None, fixed_shift)

        @pl.when(qi >= 1)
        def _():
            block(qi - 1, 1, fixed_shift)

        block(qi, 0, fixed_shift)

    @pl.when(safe)
    def _():
        m_s[...] = bound + bmax
        run(True)

    @pl.when(jnp.logical_not(safe))
    def _():
        m_s[...] = jnp.full(m_s.shape, NEG, F32)
        run(False)

    lam_p = lam_ref[...]
    lam = (jnp.exp(jnp.sum(lam_p[0:1, :] * lam_p[1:2, :], axis=-1, keepdims=True))
           - jnp.exp(jnp.sum(lam_p[2:3, :] * lam_p[3:4, :], axis=-1, keepdims=True)) + LAM_INIT)
    a0 = acc_s[:, 0:tq]
    a1 = acc_s[:, tq:2 * tq]
    ot = a0[0:DA_DV] / a0[DA_DV:DA_DV + 1] - lam * (a1[0:DA_DV] / a1[DA_DV:DA_DV + 1])
    r = lax.rsqrt(jnp.sum(ot * ot, axis=0, keepdims=True) * (1.0 / DA_DV) + EPS)
    ot = ot * r * (sg_ref[...] * (1.0 - LAM_INIT))
    o_ref[...] = ot.T.astype(o_ref.dtype)


def _attention(p, bias, qg, kg, lam_p, subln, batch, seq):
    t = batch * seq
    tq = min(ATT_BLOCK, seq)
    nq = seq // tq
    dv = DA_DV
    return pl.pallas_call(
        _attn_body,
        grid=(batch, DA_HEADS, nq),
        in_specs=[
            pl.BlockSpec((tq, dv), lambda b, h, i: (b * nq + i, P_DAQ // dv + h)),
            pl.BlockSpec((seq, dv), lambda b, h, i: (b, P_DAK // dv + h)),
            pl.BlockSpec((seq, dv), lambda b, h, i: (b, P_DAV // dv + h)),
            pl.BlockSpec((1, 2, tq, tq), lambda b, h, i: (h, 0, 0, 0)),
            pl.BlockSpec((1, dv), lambda b, h, i: (0, 0)),
            pl.BlockSpec((1, dv), lambda b, h, i: (0, 0)),
            pl.BlockSpec((4, DA_DH), lambda b, h, i: (0, 0)),
            pl.BlockSpec((dv, 1), lambda b, h, i: (0, 0)),
        ],
        out_specs=pl.BlockSpec((tq, dv), lambda b, h, i: (b * nq + i, h)),
        out_shape=jax.ShapeDtypeStruct((t, DA_HEADS * dv), BF16),
        scratch_shapes=[
            pltpu.VMEM((seq, dv), BF16),
            pltpu.VMEM((seq // tq, DA_VROWS, tq), BF16),
            pltpu.VMEM((8, dv), F32),
            pltpu.VMEM((1, 2 * tq), F32),
            pltpu.VMEM((DA_VROWS, 2 * tq), F32),
        ],
        compiler_params=pltpu.CompilerParams(dimension_semantics=("parallel", "parallel", "arbitrary")),
        name="diff_attention",
    )(p, p, p, bias, qg, kg, lam_p, subln)


def _memkv_body(mem_ref, mg_ref, w_ref, kg_ref, mk_ref, mv_ref):
    x = mem_ref[0]
    xn = x * _rms(x, D_MODEL) * mg_ref[...]
    kv = _dot(xn.astype(BF16), w_ref[...])
    hw = MX_HEADS * MX_DH
    for h in range(MX_HEADS):
        sl = slice(h * MX_DH, (h + 1) * MX_DH)
        kh = kv[:, sl]
        mk_ref[0, :, sl] = (kh * _rms(kh, MX_DH) * kg_ref[...]).astype(BF16)
    mv_ref[0] = kv[:, hw:].astype(BF16)


def _memkv(mem, mem_norm, w_kv, k_norm):
    b, n, _ = mem.shape
    hw = MX_HEADS * MX_DH
    return pl.pallas_call(
        _memkv_body,
        grid=(b,),
        in_specs=[
            pl.BlockSpec((1, n, D_MODEL), lambda i: (i, 0, 0)),
            pl.BlockSpec((1, D_MODEL), lambda i: (0, 0)),
            pl.BlockSpec((D_MODEL, 2 * hw), lambda i: (0, 0)),
            pl.BlockSpec((1, MX_DH), lambda i: (0, 0)),
        ],
        out_specs=[pl.BlockSpec((1, n, hw), lambda i: (i, 0, 0))] * 2,
        out_shape=[jax.ShapeDtypeStruct((b, n, hw), BF16)] * 2,
        name="memory_kv",
    )(mem, mem_norm, w_kv, k_norm)


def _merge_body(x_ref, odn_ref, oda_ref, mxq_ref, g0_ref, g1_ref, g2_ref, mk_ref, mv_ref, qg_ref,
                bg_ref, wb_ref, wo_ref, out_ref, omx):
    for h in range(MX_HEADS):
        sl = slice(h * MX_DH, (h + 1) * MX_DH)
        qh = mxq_ref[:, sl].astype(F32)
        qh = qh * _rms(qh, MX_DH) * qg_ref[...] * (MX_DH ** -0.5 * LOG2E)
        s = _dot_nt(qh.astype(BF16), mk_ref[0, :, sl])
        p = jnp.exp2(s - jnp.max(s, axis=-1, keepdims=True))
        oh = _dot(p.astype(BF16), mv_ref[0, :, sl]) / jnp.sum(p, axis=-1, keepdims=True)
        omx[:, sl] = oh.astype(BF16)
    y = None
    for r, (o_r, g_r) in enumerate(((odn_ref, g0_ref), (oda_ref, g1_ref), (omx, g2_ref))):
        gate = _sigmoid(g_r[...].astype(F32) + bg_ref[r:r + 1, :])
        term = gate * _dot(o_r[...], wb_ref[r])
        y = term if y is None else y + term
    out_ref[...] = x_ref[...] + _dot(y.astype(BF16), wo_ref[...])


def _merge(x2, o_dn, o_da, p, mk, mv, q_norm, b_gate, w_branch, w_out, seq):
    t = x2.shape[0]
    tm = min(512, seq)
    nt = seq // tm
    bw = 512
    n_mem = mk.shape[1]
    return pl.pallas_call(
        _merge_body,
        grid=(t // tm,),
        in_specs=[
            pl.BlockSpec((tm, D_MODEL), lambda i: (i, 0)),
            pl.BlockSpec((tm, bw), lambda i: (i, 0)),
            pl.BlockSpec((tm, bw), lambda i: (i, 0)),
            pl.BlockSpec((tm, bw), lambda i: (i, P_MXQ // bw)),
            pl.BlockSpec((tm, D_MODEL), lambda i: (i, P_GATE // D_MODEL)),
            pl.BlockSpec((tm, D_MODEL), lambda i: (i, P_GATE // D_MODEL + 1)),
            pl.BlockSpec((tm, D_MODEL), lambda i: (i, P_GATE // D_MODEL + 2)),
            pl.BlockSpec((1, n_mem, bw), lambda i: (i // nt, 0, 0)),
            pl.BlockSpec((1, n_mem, bw), lambda i: (i // nt, 0, 0)),
            pl.BlockSpec((1, MX_DH), lambda i: (0, 0)),
            pl.BlockSpec((3, D_MODEL), lambda i: (0, 0)),
            pl.BlockSpec((3, bw, D_MODEL), lambda i: (0, 0, 0)),
            pl.BlockSpec((D_MODEL, D_MODEL), lambda i: (0, 0)),
        ],
        out_specs=pl.BlockSpec((tm, D_MODEL), lambda i: (i, 0)),
        out_shape=jax.ShapeDtypeStruct((t, D_MODEL), F32),
        scratch_shapes=[pltpu.VMEM((tm, bw), BF16)],
        compiler_params=pltpu.CompilerParams(dimension_semantics=("parallel",)),
        name="merge",
    )(x2, o_dn, o_da, p, p, p, p, mk, mv, q_norm, b_gate, w_branch, w_out)


def _router_body(x_ref, g_ref, wr_ref, br_ref, h_ref, idx_ref, wt_ref, rank_ref, cnt_ref):
    tm = x_ref.shape[0]
    x = x_ref[...]
    h = x * _rms(x, D_MODEL) * g_ref[...]
    h_ref[...] = h.astype(BF16)
    logits = _dot3(h, wr_ref[...]) + br_ref[...]
    lane = lax.broadcasted_iota(I32, (tm, LANES), 1)
    lane_f = lane.astype(F32)
    work = logits
    sel = jnp.zeros((tm, LANES), F32)
    vals, idxs = [], []
    for _ in range(TOP_K):
        mx = jnp.max(work, axis=-1, keepdims=True)
        ik = jnp.min(jnp.where(work == mx, lane_f, float(LANES)), axis=-1, keepdims=True)
        hit = lane_f == ik
        sel = jnp.where(hit, 1.0, sel)
        work = jnp.where(hit, -jnp.inf, work)
        vals.append(mx)
        idxs.append(ik)
    es = [jnp.exp(v - vals[0]) for v in vals]
    den = es[0] + es[1] + es[2] + es[3]
    sub = cnt_ref.shape[0]
    mt = tm // sub
    r = lax.broadcasted_iota(I32, (tm, tm), 0)
    c = lax.broadcasted_iota(I32, (tm, tm), 1)
    tril = jnp.where((r > c) & (r // mt == c // mt), 1.0, 0.0).astype(BF16)
    cum = _dot(tril, sel.astype(BF16))
    idx_o = jnp.zeros((tm, LANES), F32)
    wt_o = jnp.zeros((tm, LANES), F32)
    rank_o = jnp.zeros((tm, LANES), F32)
    for k in range(TOP_K):
        rk = jnp.sum(jnp.where(lane_f == idxs[k], cum, 0.0), axis=-1, keepdims=True)
        idx_o = jnp.where(lane == k, idxs[k], idx_o)
        wt_o = jnp.where(lane == k, es[k] / den, wt_o)
        rank_o = jnp.where(lane == k, rk, rank_o)
    idx_ref[...] = idx_o.astype(I32)
    wt_ref[...] = wt_o
    rank_ref[...] = rank_o.astype(I32)
    for s in range(sub):
        cnt_ref[s] = jnp.broadcast_to(jnp.sum(sel[s * mt:(s + 1) * mt], axis=0, keepdims=True), (8, LANES))


def _router(x1, gain, w_r, b_r):
    t = x1.shape[0]
    mt = min(MOE_TILE, t)
    sub = 2 if t % (2 * mt) == 0 else 1
    tm = sub * mt
    row = lambda i: (i, 0)
    fixed = lambda i: (0, 0)
    return pl.pallas_call(
        _router_body,
        grid=(t // tm,),
        in_specs=[
            pl.BlockSpec((tm, D_MODEL), row),
            pl.BlockSpec((1, D_MODEL), fixed),
            pl.BlockSpec((D_MODEL, LANES), fixed),
            pl.BlockSpec((1, LANES), fixed),
        ],
        out_specs=[
            pl.BlockSpec((tm, D_MODEL), row),
            pl.BlockSpec((tm, LANES), row),
            pl.BlockSpec((tm, LANES), row),
            pl.BlockSpec((tm, LANES), row),
            pl.BlockSpec((sub, 8, LANES), lambda i: (i, 0, 0)),
        ],
        out_shape=[
            jax.ShapeDtypeStruct((t, D_MODEL), BF16),
            jax.ShapeDtypeStruct((t, LANES), I32),
            jax.ShapeDtypeStruct((t, LANES), F32),
            jax.ShapeDtypeStruct((t, LANES), I32),
            jax.ShapeDtypeStruct((t // mt, 8, LANES), F32),
        ],
        compiler_params=pltpu.CompilerParams(dimension_semantics=("arbitrary",)),
        name="router",
    )(x1, gain, w_r, b_r)


def _lane_cumsum(x):
    lane = lax.broadcasted_iota(I32, x.shape, 1)
    s = 1
    while s < N_EXPERTS:
        x = x + jnp.where(lane >= s, pltpu.roll(x, s, axis=1), 0.0)
        s *= 2
    return x


def _plan_body(cnt_ref, gs_ref, ls_ref, n8_ref, lsf_ref, tail_ref, meta_ref):
    nt = cnt_ref.shape[0]
    ga = float(GROUP_ALIGN)
    eb = float(EXPERT_BLOCK)
    lane = lax.broadcasted_iota(I32, (nt, LANES), 1)
    r8 = jnp.where(lane < N_EXPERTS, jnp.floor((cnt_ref[...] + (ga - 1.0)) * (1.0 / ga)) * ga, 0.0)
    ri = lax.broadcasted_iota(I32, (nt, nt), 0)
    ci = lax.broadcasted_iota(I32, (nt, nt), 1)
    before = _dot(jnp.where(ri > ci, 1.0, 0.0).astype(BF16), r8.astype(BF16))
    tot = jnp.sum(r8, axis=0, keepdims=True)
    region = jnp.floor((tot + (eb - 1.0)) * (1.0 / eb)) * eb
    pends = _lane_cumsum(jnp.broadcast_to(region, (8, LANES)))[0:1, :]
    pstart = pends - region
    lstart = _lane_cumsum(r8) - r8
    gs_ref[...] = (pstart + before).astype(I32)
    ls_ref[...] = lstart.astype(I32)
    pieces = r8 * (1.0 / ga)
    n8_ref[...] = jnp.where(lane == TOTAL_LANE, jnp.sum(pieces, axis=1, keepdims=True), pieces).astype(I32)
    lsf_ref[...] = lstart
    row8 = lax.broadcasted_iota(I32, (8, LANES), 0)
    tail_n = (region - tot) * (1.0 / ga)
    lane8 = lax.broadcasted_iota(I32, (8, LANES), 1)
    tail_n = jnp.where(lane8 == TOTAL_LANE, jnp.sum(tail_n, axis=1, keepdims=True), tail_n)
    tail = jnp.where(row8 == 0, pstart + tot, jnp.where(row8 == 1, tail_n, 0.0))
    tail_ref[...] = tail.astype(I32)
    nb = meta_ref.shape[0]
    ln = lax.broadcasted_iota(I32, (nb, LANES), 1)
    blk = lax.broadcasted_iota(I32, (nb, LANES), 0).astype(F32) * eb
    be = jnp.sum(jnp.where((ln < N_EXPERTS) & (pends <= blk), 1.0, 0.0), axis=-1, keepdims=True)
    be = jnp.minimum(be, float(N_EXPERTS - 1))
    used = jnp.sum(jnp.where(ln == N_EXPERTS - 1, pends, 0.0), axis=-1, keepdims=True) * (1.0 / eb)
    meta_ref[...] = jnp.where(ln == 0, be, jnp.where(ln == 1, used, 0.0)).astype(I32)


def _plan(cnt, n_blocks_pad):
    nt = cnt.shape[0]
    shp = jax.ShapeDtypeStruct((nt, LANES), I32)
    return pl.pallas_call(
        _plan_body,
        out_shape=[shp, shp, shp, jax.ShapeDtypeStruct((nt, LANES), F32),
                   jax.ShapeDtypeStruct((8, LANES), I32),
                   jax.ShapeDtypeStruct((n_blocks_pad, LANES), I32)],
        name="dispatch_plan",
    )(cnt)


def _local_positions(idx_ref, rank_ref, lsf_ref):
    tm = idx_ref.shape[0]
    lane = lax.broadcasted_iota(I32, (tm, LANES), 1)
    idx = idx_ref[...]
    rank = rank_ref[...].astype(F32)
    ls_row = lsf_ref[0]
    out = []
    for k in range(TOP_K):
        base = jnp.sum(jnp.where(lane == idx[:, k:k + 1], ls_row, 0.0), axis=-1, keepdims=True)
        out.append((base + rank[:, k:k + 1]).astype(I32))
    return out


def _local_positions_lanes(idx_ref, rank_ref, lsf_ref):
    tm = idx_ref.shape[0]
    idx_t = idx_ref[...].astype(F32).T
    rank_t = rank_ref[...].astype(F32).T
    ls_col = jnp.broadcast_to(lsf_ref[0], (LANES, LANES)).T[:, 0:1]
    expert = lax.broadcasted_iota(I32, (LANES, tm), 0).astype(F32)
    out = []
    for k in range(TOP_K):
        base = jnp.sum(jnp.where(expert == idx_t[k:k + 1, :], ls_col, 0.0), axis=0, keepdims=True)
        out.append((base + rank_t[k:k + 1, :]).astype(I32))
    return out


PACK_W = D_MODEL // 2
U32 = jnp.uint32


def _pack_rows(x):
    xb = x.astype(BF16).astype(F32)
    hi = lax.bitcast_convert_type(xb[:, :PACK_W], U32)
    lo = lax.bitcast_convert_type(xb[:, PACK_W:], U32)
    return hi | (lo >> 16)


def _unpack_rows(w):
    hi = lax.bitcast_convert_type(w & jnp.uint32(0xFFFF0000), F32)
    lo = lax.bitcast_convert_type(w << 16, F32)
    return hi.astype(BF16), lo.astype(BF16)


TOTAL_LANE = LANES - 1
WAIT_CHUNK = 128


def _start_group_copies(tables, tile, make_copy):
    gs_ref, ls_ref, n8_ref = tables

    def group(e, carry):
        g0 = gs_ref[tile, e]
        l0 = ls_ref[tile, e]

        def one(j, c):
            make_copy(pl.multiple_of(l0 + j * GROUP_ALIGN, GROUP_ALIGN),
                      pl.multiple_of(g0 + j * GROUP_ALIGN, GROUP_ALIGN)).start()
            return c

        lax.fori_loop(0, n8_ref[tile, e], one, 0)
        return carry

    lax.fori_loop(0, N_EXPERTS, group, 0)


def _wait_pieces(n, make_wait):
    def chunk(j, c):
        make_wait(WAIT_CHUNK).wait()
        return c

    lax.fori_loop(0, n // WAIT_CHUNK, chunk, 0)
    b = WAIT_CHUNK // 2
    while b >= 1:
        def _(b=b):
            make_wait(b).wait()
        pl.when((n & b) != 0)(_)
        b //= 2


def _dispatch_body(gs_ref, ls_ref, n8_ref, tail_ref, h_ref, idx_ref, rank_ref, lsf_ref, xs_ref,
                   xl, zbuf, sem):
    i = pl.program_id(0)
    last = i == pl.num_programs(0) - 1
    slot = i % 2
    tables = (gs_ref, ls_ref, n8_ref)
    tm = h_ref.shape[0]
    lrows = xl.shape[1]
    pos = lax.broadcasted_iota(I32, (lrows, tm), 0)
    lp = _local_positions_lanes(idx_ref, rank_ref, lsf_ref)
    hit = pos == lp[0]
    for k in range(1, TOP_K):
        hit = hit | (pos == lp[k])
    xl[slot] = _pack_rows(_dot(jnp.where(hit, 1.0, 0.0).astype(BF16), h_ref[...]))

    def copy_from(s):
        def make_copy(l, g):
            return pltpu.make_async_copy(xl.at[s, pl.ds(l, GROUP_ALIGN), :],
                                         xs_ref.at[pl.ds(g, GROUP_ALIGN), :], sem.at[s])
        return make_copy

    def wait_on(s):
        def make_wait(pieces):
            rows = pieces * GROUP_ALIGN
            return pltpu.make_async_copy(xl.at[0, pl.ds(0, rows), :], xs_ref.at[pl.ds(0, rows), :], sem.at[s])
        return make_wait

    @pl.when(i >= 1)
    def _():
        _wait_pieces(n8_ref[i - 1, TOTAL_LANE], wait_on(1 - slot))

    _start_group_copies(tables, i, copy_from(slot))

    def start_tails():
        def per_expert(e, carry):
            g0 = tail_ref[0, e]

            def one(j, c):
                pltpu.make_async_copy(
                    zbuf, xs_ref.at[pl.ds(pl.multiple_of(g0 + j * GROUP_ALIGN, GROUP_ALIGN), GROUP_ALIGN), :],
                    sem.at[2]).start()
                return c

            lax.fori_loop(0, tail_ref[1, e], one, 0)
            return carry

        lax.fori_loop(0, N_EXPERTS, per_expert, 0)

    @pl.when(last)
    def _():
        zbuf[...] = jnp.zeros_like(zbuf)
        start_tails()
        _wait_pieces(n8_ref[i, TOTAL_LANE], wait_on(slot))
        _wait_pieces(tail_ref[1, TOTAL_LANE], wait_on(2))


def _dispatch(gs, ls, n8, tail, h2, idx, rank, lsf, n_slots):
    t = h2.shape[0]
    tm = min(MOE_TILE, t)
    nt = t // tm
    lrows = tm * TOP_K + N_EXPERTS * GROUP_ALIGN
    row = lambda i, *_: (i, 0)
    grid_spec = pltpu.PrefetchScalarGridSpec(
        num_scalar_prefetch=4,
        grid=(nt,),
        in_specs=[
            pl.BlockSpec((tm, D_MODEL), row),
            pl.BlockSpec((tm, LANES), row),
            pl.BlockSpec((tm, LANES), row),
            pl.BlockSpec((1, 1, LANES), lambda i, *_: (i, 0, 0)),
        ],
        out_specs=pl.BlockSpec(memory_space=pl.ANY),
        scratch_shapes=[pltpu.VMEM((2, lrows, PACK_W), U32), pltpu.VMEM((GROUP_ALIGN, PACK_W), U32),
                        pltpu.SemaphoreType.DMA((3,))],
    )
    return pl.pallas_call(
        _dispatch_body,
        grid_spec=grid_spec,
        out_shape=jax.ShapeDtypeStruct((n_slots, PACK_W), U32),
        compiler_params=pltpu.CompilerParams(dimension_semantics=("arbitrary",)),
        name="moe_dispatch",
    )(gs, ls, n8, tail, h2, idx, rank, lsf)


def _expert_body(be_ref, nu_ref, x_ref, wgu_ref, bgu_ref, wd_ref, bd_ref, y_ref, wgu_b, wd_b):
    j = pl.program_id(0)

    @pl.when(j < nu_ref[0])
    def _():
        @pl.when(jnp.logical_or(j == 0, be_ref[j] != be_ref[jnp.maximum(j - 1, 0)]))
        def _():
            wgu_b[...] = wgu_ref[0].astype(BF16)
            wd_b[...] = wd_ref[0].astype(BF16)

        gu = _dot(jnp.concatenate(_unpack_rows(x_ref[...]), axis=1), wgu_b[...]) + bgu_ref[0]
        gate = jnp.minimum(gu[:, :D_FF], SWIGLU_LIMIT)
        up = jnp.clip(gu[:, D_FF:], -SWIGLU_LIMIT, SWIGLU_LIMIT)
        act = (up + 1.0) * gate * _sigmoid(SWIGLU_ALPHA * gate)
        y_ref[...] = _pack_rows(_dot(act.astype(BF16), wd_b[...]) + bd_ref[0])


def _experts(block_e, n_used, xs, wgu, bgu, wd, bd):
    n_slots = xs.shape[0]
    nb = n_slots // EXPERT_BLOCK

    def blk(j, be, nu):
        return jnp.minimum(j, nu[0] - 1)

    grid_spec = pltpu.PrefetchScalarGridSpec(
        num_scalar_prefetch=2,
        grid=(nb,),
        in_specs=[
            pl.BlockSpec((EXPERT_BLOCK, PACK_W), lambda j, be, nu: (blk(j, be, nu), 0)),
            pl.BlockSpec((1, D_MODEL, 2 * D_FF), lambda j, be, nu: (be[blk(j, be, nu)], 0, 0)),
            pl.BlockSpec((1, 1, 2 * D_FF), lambda j, be, nu: (be[blk(j, be, nu)], 0, 0)),
            pl.BlockSpec((1, D_FF, D_MODEL), lambda j, be, nu: (be[blk(j, be, nu)], 0, 0)),
            pl.BlockSpec((1, 1, D_MODEL), lambda j, be, nu: (be[blk(j, be, nu)], 0, 0)),
        ],
        out_specs=pl.BlockSpec((EXPERT_BLOCK, PACK_W), lambda j, be, nu: (blk(j, be, nu), 0)),
        scratch_shapes=[pltpu.VMEM((D_MODEL, 2 * D_FF), BF16), pltpu.VMEM((D_FF, D_MODEL), BF16)],
    )
    return pl.pallas_call(
        _expert_body,
        grid_spec=grid_spec,
        out_shape=jax.ShapeDtypeStruct((n_slots, PACK_W), U32),
        compiler_params=pltpu.CompilerParams(dimension_semantics=("arbitrary",)),
        name="moe_experts",
    )(block_e, n_used, xs, wgu, bgu, wd, bd)


def _combine_body(gs_ref, ls_ref, n8_ref, x_ref, wt_ref, idx_ref, rank_ref, lsf_ref, y_ref, out_ref,
                  yl, sem):
    i = pl.program_id(0)
    slot = i % 2
    tables = (gs_ref, ls_ref, n8_ref)
    tm = x_ref.shape[0]
    lrows = yl.shape[1]

    def copy_into(s):
        def make_copy(l, g):
            return pltpu.make_async_copy(y_ref.at[pl.ds(g, GROUP_ALIGN), :],
                                         yl.at[s, pl.ds(l, GROUP_ALIGN), :], sem.at[s])
        return make_copy

    def make_wait(pieces):
        rows = pieces * GROUP_ALIGN
        return pltpu.make_async_copy(y_ref.at[pl.ds(0, rows), :], yl.at[0, pl.ds(0, rows), :], sem.at[slot])

    @pl.when(i == 0)
    def _():
        yl[...] = jnp.zeros_like(yl)
        _start_group_copies(tables, i, copy_into(slot))

    @pl.when(i + 1 < pl.num_programs(0))
    def _():
        _start_group_copies(tables, i + 1, copy_into(1 - slot))

    pos = lax.broadcasted_iota(I32, (tm, lrows), 1)
    lp = _local_positions(idx_ref, rank_ref, lsf_ref)
    wt = wt_ref[...]
    wm = jnp.zeros((tm, lrows), F32)
    for k in range(TOP_K):
        wm = jnp.where(pos == lp[k], wt[:, k:k + 1], wm)
    _wait_pieces(n8_ref[i, TOTAL_LANE], make_wait)
    wmb = wm.astype(BF16)
    y_hi, y_lo = _unpack_rows(yl[slot])
    out_ref[:, :PACK_W] = x_ref[:, :PACK_W] + _dot(wmb, y_hi)
    out_ref[:, PACK_W:] = x_ref[:, PACK_W:] + _dot(wmb, y_lo)


def _combine(gs, ls, n8, x1, wts, idx, rank, lsf, y):
    t = x1.shape[0]
    tm = min(MOE_TILE, t)
    nt = t // tm
    lrows = tm * TOP_K + N_EXPERTS * GROUP_ALIGN
    row = lambda i, *_: (i, 0)
    grid_spec = pltpu.PrefetchScalarGridSpec(
        num_scalar_prefetch=3,
        grid=(nt,),
        in_specs=[
            pl.BlockSpec((tm, D_MODEL), row),
            pl.BlockSpec((tm, LANES), row),
            pl.BlockSpec((tm, LANES), row),
            pl.BlockSpec((tm, LANES), row),
            pl.BlockSpec((1, 1, LANES), lambda i, *_: (i, 0, 0)),
            pl.BlockSpec(memory_space=pl.ANY),
        ],
        out_specs=pl.BlockSpec((tm, D_MODEL), row),
        scratch_shapes=[pltpu.VMEM((2, lrows, PACK_W), U32), pltpu.SemaphoreType.DMA((2,))],
    )
    return pl.pallas_call(
        _combine_body,
        grid_spec=grid_spec,
        out_shape=jax.ShapeDtypeStruct((t, D_MODEL), F32),
        compiler_params=pltpu.CompilerParams(dimension_semantics=("arbitrary",)),
        name="moe_combine",
    )(gs, ls, n8, x1, wts, idx, rank, lsf, y)


def _pad_lanes(v, fill=0.0):
    v = v.astype(F32).reshape(1, -1)
    return jnp.pad(v, ((0, 0), (0, LANES - v.shape[1])), constant_values=fill)


def _mixer(x2, mem, rel_table, attn_norm, w_in, b_gate, dn_conv, dn_a_log, dn_dt_bias, dn_out_norm,
           da_q_norm, da_k_norm, da_lambda, da_subln, mem_norm, w_mem_kv, mx_q_norm, mx_k_norm,
           w_branch, w_out, batch, seq):
    wp = jnp.concatenate([w_in[:, :W_AB_LO], w_in[:, W_AB_HI:]], axis=1).astype(BF16)
    wab = jnp.pad(w_in[:, W_AB_LO:W_AB_HI], ((0, 0), (0, LANES - (W_AB_HI - W_AB_LO))))
    p, ab = _inproj(x2, attn_norm.reshape(1, -1), wp, wab)

    o_dn = _deltanet(p, ab, dn_conv, _pad_lanes(dn_a_log), _pad_lanes(dn_dt_bias),
                     dn_out_norm.reshape(1, -1), batch, seq)

    tq = min(ATT_BLOCK, seq)
    bias = _bias_tiles(rel_table.T, tq)
    o_da = _attention(p, bias, jnp.tile(da_q_norm, 2).reshape(1, -1), jnp.tile(da_k_norm, 2).reshape(1, -1),
                      da_lambda, da_subln.reshape(-1, 1), batch, seq)

    mk, mv = _memkv(mem, mem_norm.reshape(1, -1), w_mem_kv.astype(BF16), mx_k_norm.reshape(1, -1))
    return _merge(x2, o_dn, o_da, p, mk, mv, mx_q_norm.reshape(1, -1), b_gate.reshape(3, D_MODEL),
                  w_branch.astype(BF16), w_out.astype(BF16), seq)


def _moe(x1, ffn_norm, w_router, b_router, w_gate_up, b_gate_up, w_down, b_down):
    t = x1.shape[0]
    nt = t // min(MOE_TILE, t)
    max_rows = t * TOP_K + nt * N_EXPERTS * (GROUP_ALIGN - 1)
    n_blocks = -(-max_rows // EXPERT_BLOCK) + N_EXPERTS
    n_blocks_pad = -(-n_blocks // 8) * 8
    n_slots = n_blocks * EXPERT_BLOCK

    wr = jnp.pad(w_router, ((0, 0), (0, LANES - N_EXPERTS)))
    h2, idx, wts, rank, cnt = _router(x1, ffn_norm.reshape(1, -1), wr, _pad_lanes(b_router, NEG))
    gs, ls, n8, lsf, tail, meta = _plan(cnt[:, 0, :], n_blocks_pad)
    block_e = meta[:n_blocks, 0]
    n_used = meta[0:1, 1]
    lsf = lsf.reshape(nt, 1, LANES)

    xs = _dispatch(gs, ls, n8, tail, h2, idx, rank, lsf, n_slots)
    y = _experts(block_e, n_used, xs, w_gate_up, b_gate_up.reshape(N_EXPERTS, 1, -1),
                 w_down, b_down.reshape(N_EXPERTS, 1, -1))
    return _combine(gs, ls, n8, x1, wts, idx, rank, lsf, y)


def kernel(x, mem, rel_table, attn_norm, w_in, b_gate, dn_conv, dn_a_log, dn_dt_bias, dn_out_norm,
           da_q_norm, da_k_norm, da_lambda, da_subln, mem_norm, w_mem_kv, mx_q_norm, mx_k_norm,
           w_branch, w_out, ffn_norm, w_router, b_router, w_gate_up, b_gate_up, w_down, b_down):
    batch, seq, d = x.shape
    x2 = x.reshape(batch * seq, d)
    x1 = _mixer(x2, mem, rel_table, attn_norm[0], w_in[0], b_gate[0], dn_conv[0], dn_a_log[0],
                dn_dt_bias[0], dn_out_norm[0], da_q_norm[0], da_k_norm[0], da_lambda[0], da_subln[0],
                mem_norm[0], w_mem_kv[0], mx_q_norm[0], mx_k_norm[0], w_branch[0], w_out[0], batch, seq)
    out = _moe(x1, ffn_norm[0], w_router[0], b_router[0], w_gate_up[0], b_gate_up[0], w_down[0],
               b_down[0])
    return out.reshape(batch, seq, d)
```

```python
import functools
import math

import jax
import jax.numpy as jnp
from jax import lax
from jax.experimental import pallas as pl
from jax.experimental.pallas import tpu as pltpu

F32 = jnp.float32
BF16 = jnp.bfloat16
I32 = jnp.int32

D_MODEL = 1024
EPS = 1e-6
LANES = 128

DN_HEADS = 4
DN_DK = 128
DN_CHUNK = 64
DN_CONV = 4

DA_HEADS = 4
DA_DH = 64

MX_HEADS = 4
MX_DH = 128

REL_BUCKETS = 32
REL_MAX_DIST = 128

N_EXPERTS = 32
TOP_K = 4
D_FF = 1024
SWIGLU_LIMIT = 7.0
SWIGLU_ALPHA = 1.702
EXPERT_BLOCK = 512
MOE_TILE = 256
GROUP_ALIGN = 8

LAM_INIT = 0.8 - 0.6 * math.exp(-0.3 * 0)
LOG2E = 1.4426950408889634
NEG = -1e30

P_DNQ, P_DNK, P_DNV, P_DNZ = 0, 512, 1024, 1536
P_DAQ, P_DAK, P_DAV = 2048, 2560, 3072
P_MXQ = 3584
P_GATE = 4096
P_COLS = 7168
W_AB_LO, W_AB_HI = 2048, 2056


def _dot(a, b):
    return jnp.dot(a, b, preferred_element_type=F32)


def _dot_nt(a, b):
    return lax.dot_general(a, b, (((1,), (1,)), ((), ())), preferred_element_type=F32)


def _dot_tn(a, b):
    return lax.dot_general(a, b, (((0,), (0,)), ((), ())), preferred_element_type=F32)


def _split(x):
    hi = x.astype(BF16)
    lo = (x - hi.astype(F32)).astype(BF16)
    return hi, lo


def _dot3(a, b):
    ah, al = _split(a)
    bh, bl = _split(b)
    return _dot(ah, bh) + _dot(ah, bl) + _dot(al, bh)


def _sigmoid(x):
    return 1.0 / (1.0 + jnp.exp(-x))


def _rms(x, n):
    return lax.rsqrt(jnp.sum(x * x, axis=-1, keepdims=True) * (1.0 / n) + EPS)


def _inproj_body(x_ref, g_ref, w_ref, wab_ref, p_ref, ab_ref, h_scr):
    @pl.when(pl.program_id(1) == 0)
    def _():
        x = x_ref[...]
        h = x * _rms(x, D_MODEL) * g_ref[...]
        h_scr[...] = h.astype(BF16)
        ab_ref[...] = _dot3(h, wab_ref[...])

    p_ref[...] = _dot(h_scr[...], w_ref[...]).astype(p_ref.dtype)


def _inproj(x2, gain, wp, wab):
    t = x2.shape[0]
    tm = min(1024, t)
    tn = 1024
    return pl.pallas_call(
        _inproj_body,
        grid=(t // tm, P_COLS // tn),
        in_specs=[
            pl.BlockSpec((tm, D_MODEL), lambda i, j: (i, 0)),
            pl.BlockSpec((1, D_MODEL), lambda i, j: (0, 0)),
            pl.BlockSpec((D_MODEL, tn), lambda i, j: (0, j)),
            pl.BlockSpec((D_MODEL, LANES), lambda i, j: (0, 0)),
        ],
        out_specs=[
            pl.BlockSpec((tm, tn), lambda i, j: (i, j)),
            pl.BlockSpec((tm, LANES), lambda i, j: (i, 0)),
        ],
        out_shape=[
            jax.ShapeDtypeStruct((t, P_COLS), BF16),
            jax.ShapeDtypeStruct((t, LANES), F32),
        ],
        scratch_shapes=[pltpu.VMEM((tm, D_MODEL), BF16)],
        compiler_params=pltpu.CompilerParams(dimension_semantics=("parallel", "arbitrary")),
        name="inproj",
    )(x2, gain, wp, wab)


DN_HALO = 16
DN_SCAN_CHUNK = 256


def _deltanet_body(q_ref, k_ref, v_ref, z_ref, qh_ref, kh_ref, vh_ref, ab_ref, cw_ref, alog_ref,
                   dtb_ref, on_ref, o_ref, stage, qs, ks, vs, s_scr):
    i = pl.program_id(1)
    tc = q_ref.shape[0]
    hw = DN_HEADS * DN_DK

    @pl.when(i == 0)
    def _():
        s_scr[...] = jnp.zeros_like(s_scr)

    for src, halo, dst, off, kind in ((q_ref, qh_ref, qs, 0, "q"), (k_ref, kh_ref, ks, hw, "k"),
                                      (v_ref, vh_ref, vs, 2 * hw, "v")):
        hal = halo[...].astype(F32)
        stage[0:DN_HALO, :] = jnp.where(i == 0, 0.0, hal)
        stage[DN_HALO:DN_HALO + tc, :] = src[...].astype(F32)
        base = DN_HALO - (DN_CONV - 1)
        y = stage[base:base + tc, :] * cw_ref[0:1, off:off + hw]
        for j in range(1, DN_CONV):
            y = y + stage[base + j:base + j + tc, :] * cw_ref[j:j + 1, off:off + hw]
        y = y * _sigmoid(y)
        if kind == "v":
            dst[...] = y
        else:
            for h in range(DN_HEADS):
                sl = slice(h * DN_DK, (h + 1) * DN_DK)
                yh = y[:, sl]
                r = lax.rsqrt(jnp.sum(yh * yh, axis=-1, keepdims=True) + EPS)
                if kind == "q":
                    r = r * (DN_DK ** -0.5)
                dst[:, sl] = yh * r

    c = min(DN_SCAN_CHUNK, tc)
    row = lax.broadcasted_iota(I32, (c, c), 0)
    col = lax.broadcasted_iota(I32, (c, c), 1)
    incl = row >= col
    strict = row > col
    same_blk = (row // DN_CHUNK) == (col // DN_CHUNK)
    tri = jnp.where(incl, 1.0, 0.0).astype(BF16)
    eye = jnp.where(row == col, 1.0, 0.0)
    neg_a = -jnp.exp(alog_ref[...])
    dtb = dtb_ref[...]

    def chunk(ci, carry):
        r0 = pl.multiple_of(ci * c, c)
        abc = ab_ref[pl.ds(r0, c), :]
        a_in = abc + dtb
        g_all = neg_a * (jnp.maximum(a_in, 0.0) + jnp.log(1.0 + jnp.exp(-jnp.abs(a_in))))
        beta_all = _sigmoid(abc)
        zc = z_ref[pl.ds(r0, c), :].astype(F32)
        hs = range(DN_HEADS)
        sls = [slice(h * DN_DK, (h + 1) * DN_DK) for h in hs]
        q = [qs[pl.ds(r0, c), sl] for sl in sls]
        k = [ks[pl.ds(r0, c), sl] for sl in sls]
        v = [vs[pl.ds(r0, c), sl] for sl in sls]
        g = [g_all[:, h:h + 1] for h in hs]
        beta = [beta_all[:, DN_HEADS + h:DN_HEADS + h + 1] for h in hs]
        g_split = [_split(jnp.where(strict, g[h], 0.0)) for h in hs]
        diff = [_dot(tri, g_split[h][0]) + _dot(tri, g_split[h][1]) for h in hs]
        kb = [k[h].astype(BF16) for h in hs]
        qkk = [_dot_nt(jnp.concatenate([q[h].astype(BF16), kb[h]], axis=0), kb[h]) for h in hs]
        gc = [diff[h][:, 0:1] + g[h][0:1, :] for h in hs]
        decay = [jnp.where(incl, jnp.exp(diff[h]), 0.0) for h in hs]
        lower = [jnp.where(strict, qkk[h][c:] * decay[h] * beta[h], 0.0) for h in hs]
        pw = [jnp.where(same_blk, -lower[h], 0.0) for h in hs]
        dinv = [eye + pw[h] for h in hs]
        pwb = [pw[h].astype(BF16) for h in hs]
        for _ in range(int(math.log2(DN_CHUNK)) - 1):
            pwb = [_dot(pwb[h], pwb[h]).astype(BF16) for h in hs]
            dinv = [dinv[h] + _dot(dinv[h].astype(BF16), pwb[h]) for h in hs]
        dinv_b = [dinv[h].astype(BF16) for h in hs]
        pw = [-_dot(dinv_b[h], jnp.where(same_blk, 0.0, lower[h]).astype(BF16)) for h in hs]
        xm = [eye + pw[h] for h in hs]
        pwb = [pw[h].astype(BF16) for h in hs]
        for _ in range(int(math.log2(c // DN_CHUNK)) - 1):
            pwb = [_dot(pwb[h], pwb[h]).astype(BF16) for h in hs]
            xm = [xm[h] + _dot(xm[h].astype(BF16), pwb[h]) for h in hs]
        inv = [_dot(xm[h].astype(BF16), dinv_b[h]).astype(BF16) for h in hs]
        egc = [jnp.exp(gc[h]) for h in hs]
        rhs = [jnp.concatenate([v[h] * beta[h], k[h] * (beta[h] * egc[h])], axis=1).astype(BF16) for h in hs]
        sol = [_dot(inv[h], rhs[h]) for h in hs]
        qkm = [jnp.where(incl, qkk[h][:c] * decay[h], 0.0).astype(BF16) for h in hs]
        gl = [gc[h][c - 1:c, :] for h in hs]
        state = [s_scr[h] for h in hs]
        ws = [_dot(jnp.concatenate([sol[h][:, DN_DK:].astype(BF16), (q[h] * egc[h]).astype(BF16)], axis=0),
                   state[h].astype(BF16)) for h in hs]
        v_new = [sol[h][:, :DN_DK] - ws[h][:c] for h in hs]
        o = [ws[h][c:] + _dot(qkm[h], v_new[h].astype(BF16)) for h in hs]
        for h in hs:
            s_scr[h] = state[h] * jnp.exp(gl[h]) + _dot_tn(kb[h], (v_new[h] * jnp.exp(gl[h] - gc[h])).astype(BF16))
        for h in hs:
            zz = zc[:, sls[h]]
            on = o[h] * _rms(o[h], DN_DK) * on_ref[...]
            o_ref[pl.ds(r0, c), sls[h]] = (on * (zz * _sigmoid(zz))).astype(o_ref.dtype)
        return carry

    lax.fori_loop(0, tc // c, chunk, 0, unroll=True)


def _deltanet(p, ab, conv_w, alog_row, dtb_row, out_norm, batch, seq):
    t = batch * seq
    tc = min(512, seq)
    nt = seq // tc
    hw = DN_HEADS * DN_DK

    def main(cb):
        return pl.BlockSpec((tc, hw), lambda b, i: (b * nt + i, cb))

    def halo(cb):
        return pl.BlockSpec(
            (DN_HALO, hw),
            lambda b, i: (jnp.maximum((b * seq + i * tc) // DN_HALO - 1, 0), cb))

    return pl.pallas_call(
        _deltanet_body,
        grid=(batch, nt),
        in_specs=[
            main(P_DNQ // hw), main(P_DNK // hw), main(P_DNV // hw), main(P_DNZ // hw),
            halo(P_DNQ // hw), halo(P_DNK // hw), halo(P_DNV // hw),
            pl.BlockSpec((tc, LANES), lambda b, i: (b * nt + i, 0)),
            pl.BlockSpec((DN_CONV, 3 * hw), lambda b, i: (0, 0)),
            pl.BlockSpec((1, LANES), lambda b, i: (0, 0)),
            pl.BlockSpec((1, LANES), lambda b, i: (0, 0)),
            pl.BlockSpec((1, DN_DK), lambda b, i: (0, 0)),
        ],
        out_specs=pl.BlockSpec((tc, hw), lambda b, i: (b * nt + i, 0)),
        out_shape=jax.ShapeDtypeStruct((t, hw), BF16),
        scratch_shapes=[
            pltpu.VMEM((DN_HALO + tc, hw), F32),
            pltpu.VMEM((tc, hw), F32),
            pltpu.VMEM((tc, hw), F32),
            pltpu.VMEM((tc, hw), F32),
            pltpu.VMEM((DN_HEADS, DN_DK, DN_DK), F32),
        ],
        compiler_params=pltpu.CompilerParams(dimension_semantics=("parallel", "arbitrary")),
        name="deltanet",
    )(p, p, p, p, p, p, p, ab, conv_w, alog_row, dtb_row, out_norm)


ATT_BLOCK = 512


def _bias_body(tbl_ref, o_ref):
    h = pl.program_id(0)
    tq = o_ref.shape[2]
    key = lax.broadcasted_iota(I32, (tq, tq), 0)
    qry = lax.broadcasted_iota(I32, (tq, tq), 1)
    max_exact = REL_BUCKETS // 2
    far = tbl_ref[h, REL_BUCKETS - 1]
    for d in range(2):
        n = qry - key + d * tq
        nn = jnp.maximum(n, 0)
        nf = jnp.maximum(nn, 1).astype(F32)
        large = max_exact + (jnp.log(nf / max_exact) / math.log(REL_MAX_DIST / max_exact)
                             * (REL_BUCKETS - max_exact)).astype(I32)
        large = jnp.minimum(large, REL_BUCKETS - 1)
        bucket = jnp.where(nn < max_exact, nn, large)
        val = jnp.zeros((tq, tq), F32)
        for b in range(REL_BUCKETS):
            val = jnp.where(bucket == b, tbl_ref[h, b], val)
        o_ref[0, d] = jnp.where(n >= 0, (val - far) * LOG2E, NEG)


def _bias_tiles(tbl_t, tq):
    return pl.pallas_call(
        _bias_body,
        grid=(DA_HEADS,),
        in_specs=[pl.BlockSpec(memory_space=pltpu.SMEM)],
        out_specs=pl.BlockSpec((1, 2, tq, tq), lambda h: (h, 0, 0, 0)),
        out_shape=jax.ShapeDtypeStruct((DA_HEADS, 2, tq, tq), F32),
        name="t5_bias_tiles",
    )(tbl_t)


DA_DV = 2 * DA_DH
DA_VROWS = DA_DV + 16


BOUND_SLACK = 1.02
MAX_SHIFT_GAP = 110.0


def _attn_body(q_ref, k_ref, v_ref, bias_ref, qg_ref, kg_ref, lam_ref, sg_ref, o_ref,
               kn, vt, kst, m_s, acc_s):
    qi = pl.program_id(2)
    tq = q_ref.shape[0]
    seq = k_ref.shape[0]
    tk = tq
    lo_mask = lax.broadcasted_iota(I32, (1, DA_DV), 1) < DA_DH

    def group_norm(x, gain):
        x2 = x * x
        lo = jnp.sum(jnp.where(lo_mask, x2, 0.0), axis=-1, keepdims=True)
        hi = jnp.sum(jnp.where(lo_mask, 0.0, x2), axis=-1, keepdims=True)
        r = jnp.where(lo_mask, lax.rsqrt(lo * (1.0 / DA_DH) + EPS), lax.rsqrt(hi * (1.0 / DA_DH) + EPS))
        return x * r * gain

    @pl.when(qi == 0)
    def _():
        ones = jnp.ones((DA_VROWS - DA_DV, tk), BF16)

        def body(c, kmax2):
            r0 = pl.multiple_of(c * tk, tk)
            kb = group_norm(k_ref[pl.ds(r0, tk), :].astype(F32), kg_ref[...]).astype(BF16)
            kn[pl.ds(r0, tk), :] = kb
            vt[c, 0:DA_DV, :] = v_ref[pl.ds(r0, tk), :].astype(F32).T.astype(BF16)
            vt[c, DA_DV:DA_VROWS, :] = ones
            k2 = kb.astype(F32)
            k2 = k2 * k2
            lo = jnp.max(jnp.sum(jnp.where(lo_mask, k2, 0.0), axis=-1, keepdims=True), axis=0, keepdims=True)
            hi = jnp.max(jnp.sum(jnp.where(lo_mask, 0.0, k2), axis=-1, keepdims=True), axis=0, keepdims=True)
            return jnp.maximum(kmax2, jnp.where(lo_mask, lo, hi))
        kst[0:1, :] = lax.fori_loop(0, seq // tk, body, jnp.zeros((1, DA_DV), F32))
        b0 = bias_ref[0, 0]
        b1 = bias_ref[0, 1]
        bmax = jnp.maximum(jnp.max(jnp.maximum(b0, b1), axis=0, keepdims=True), 0.0)
        bmin = jnp.minimum(jnp.min(jnp.minimum(jnp.where(b0 > 0.5 * NEG, b0, 0.0), b1), axis=0, keepdims=True), 0.0)
        kst[1:2, :] = jnp.broadcast_to(jnp.max(bmax, axis=1, keepdims=True), (1, DA_DV))
        kst[2:3, :] = jnp.broadcast_to(jnp.min(bmin, axis=1, keepdims=True), (1, DA_DV))

    q = group_norm(q_ref[...].astype(F32), qg_ref[...]) * (DA_DH ** -0.5 * LOG2E)
    qcat = jnp.concatenate([jnp.where(lo_mask, q, 0.0), jnp.where(lo_mask, 0.0, q)], axis=0).astype(BF16)
    acc_s[...] = jnp.zeros_like(acc_s)

    q2 = q * q * kst[0:1, :]
    ones8 = jnp.ones((8, DA_DV), BF16)
    bmax = kst[1:2, 0:1]
    bmin = kst[2:3, 0:1]
    bound = []
    for m in range(2):
        q2m = jnp.where(lo_mask, q2, 0.0) if m == 0 else jnp.where(lo_mask, 0.0, q2)
        bound.append(jnp.sqrt(_dot_nt(ones8, q2m.astype(BF16))[0:1, :]) * BOUND_SLACK)
    bound = jnp.concatenate(bound, axis=1)
    worst = jnp.max(2.0 * bound, axis=1, keepdims=True) + bmax - bmin
    safe = worst[0, 0] <= MAX_SHIFT_GAP

    def block(j, d, fixed_shift):
        r0 = pl.multiple_of(j * tk, tk)
        st = _dot_nt(kn[pl.ds(r0, tk), :], qcat)
        if d is not None:
            bias = bias_ref[0, d]
            st = st + jnp.concatenate([bias, bias], axis=1)
        if fixed_shift:
            acc_s[...] = acc_s[...] + _dot(vt[j], jnp.exp2(st - m_s[...]).astype(BF16))
        else:
            m_prev = m_s[...]
            m_new = jnp.maximum(m_prev, jnp.max(st, axis=0, keepdims=True))
            alpha = jnp.exp2(m_prev - m_new)
            acc_s[...] = alpha * acc_s[...] + _dot(vt[j], jnp.exp2(st - m_new).astype(BF16))
            m_s[...] = m_new

    def run(fixed_shift):
        n_far = jnp.maximum(qi - 1, 0)

        def far_pair(jj, carry):
            block(2 * jj, None, fixed_shift)
            block(2 * jj + 1, None, fixed_shift)
            return carry

        lax.fori_loop(0, n_far // 2, far_pair, 0)

        @pl.when(n_far % 2 == 1)
        def _():
            block(n_far - 1, None, fixed_shift)

        @pl.when(qi >= 1)
        def _():
            block(qi - 1, 1, fixed_shift)

        block(qi, 0, fixed_shift)

    @pl.when(safe)
    def _():
        m_s[...] = bound + bmax
        run(True)

    @pl.when(jnp.logical_not(safe))
    def _():
        m_s[...] = jnp.full(m_s.shape, NEG, F32)
        run(False)

    lam_p = lam_ref[...]
    lam = (jnp.exp(jnp.sum(lam_p[0:1, :] * lam_p[1:2, :], axis=-1, keepdims=True))
           - jnp.exp(jnp.sum(lam_p[2:3, :] * lam_p[3:4, :], axis=-1, keepdims=True)) + LAM_INIT)
    a0 = acc_s[:, 0:tq]
    a1 = acc_s[:, tq:2 * tq]
    ot = a0[0:DA_DV] / a0[DA_DV:DA_DV + 1] - lam * (a1[0:DA_DV] / a1[DA_DV:DA_DV + 1])
    r = lax.rsqrt(jnp.sum(ot * ot, axis=0, keepdims=True) * (1.0 / DA_DV) + EPS)
    ot = ot * r * (sg_ref[...] * (1.0 - LAM_INIT))
    o_ref[...] = ot.T.astype(o_ref.dtype)


def _attention(p, bias, qg, kg, lam_p, subln, batch, seq):
    t = batch * seq
    tq = min(ATT_BLOCK, seq)
    nq = seq // tq
    dv = DA_DV
    return pl.pallas_call(
        _attn_body,
        grid=(batch, DA_HEADS, nq),
        in_specs=[
            pl.BlockSpec((tq, dv), lambda b, h, i: (b * nq + i, P_DAQ // dv + h)),
            pl.BlockSpec((seq, dv), lambda b, h, i: (b, P_DAK // dv + h)),
            pl.BlockSpec((seq, dv), lambda b, h, i: (b, P_DAV // dv + h)),
            pl.BlockSpec((1, 2, tq, tq), lambda b, h, i: (h, 0, 0, 0)),
            pl.BlockSpec((1, dv), lambda b, h, i: (0, 0)),
            pl.BlockSpec((1, dv), lambda b, h, i: (0, 0)),
            pl.BlockSpec((4, DA_DH), lambda b, h, i: (0, 0)),
            pl.BlockSpec((dv, 1), lambda b, h, i: (0, 0)),
        ],
        out_specs=pl.BlockSpec((tq, dv), lambda b, h, i: (b * nq + i, h)),
        out_shape=jax.ShapeDtypeStruct((t, DA_HEADS * dv), BF16),
        scratch_shapes=[
            pltpu.VMEM((seq, dv), BF16),
            pltpu.VMEM((seq // tq, DA_VROWS, tq), BF16),
            pltpu.VMEM((8, dv), F32),
            pltpu.VMEM((1, 2 * tq), F32),
            pltpu.VMEM((DA_VROWS, 2 * tq), F32),
        ],
        compiler_params=pltpu.CompilerParams(dimension_semantics=("parallel", "parallel", "arbitrary")),
        name="diff_attention",
    )(p, p, p, bias, qg, kg, lam_p, subln)


def _memkv_body(mem_ref, mg_ref, w_ref, kg_ref, mk_ref, mv_ref):
    x = mem_ref[0]
    xn = x * _rms(x, D_MODEL) * mg_ref[...]
    kv = _dot(xn.astype(BF16), w_ref[...])
    hw = MX_HEADS * MX_DH
    for h in range(MX_HEADS):
        sl = slice(h * MX_DH, (h + 1) * MX_DH)
        kh = kv[:, sl]
        mk_ref[0, :, sl] = (kh * _rms(kh, MX_DH) * kg_ref[...]).astype(BF16)
    mv_ref[0] = kv[:, hw:].astype(BF16)


def _memkv(mem, mem_norm, w_kv, k_norm):
    b, n, _ = mem.shape
    hw = MX_HEADS * MX_DH
    return pl.pallas_call(
        _memkv_body,
        grid=(b,),
        in_specs=[
            pl.BlockSpec((1, n, D_MODEL), lambda i: (i, 0, 0)),
            pl.BlockSpec((1, D_MODEL), lambda i: (0, 0)),
            pl.BlockSpec((D_MODEL, 2 * hw), lambda i: (0, 0)),
            pl.BlockSpec((1, MX_DH), lambda i: (0, 0)),
        ],
        out_specs=[pl.BlockSpec((1, n, hw), lambda i: (i, 0, 0))] * 2,
        out_shape=[jax.ShapeDtypeStruct((b, n, hw), BF16)] * 2,
        name="memory_kv",
    )(mem, mem_norm, w_kv, k_norm)


def _merge_body(x_ref, odn_ref, oda_ref, mxq_ref, g0_ref, g1_ref, g2_ref, mk_ref, mv_ref, qg_ref,
                bg_ref, wb_ref, wo_ref, out_ref, omx):
    for h in range(MX_HEADS):
        sl = slice(h * MX_DH, (h + 1) * MX_DH)
        qh = mxq_ref[:, sl].astype(F32)
        qh = qh * _rms(qh, MX_DH) * qg_ref[...] * (MX_DH ** -0.5 * LOG2E)
        s = _dot_nt(qh.astype(BF16), mk_ref[0, :, sl])
        p = jnp.exp2(s - jnp.max(s, axis=-1, keepdims=True))
        oh = _dot(p.astype(BF16), mv_ref[0, :, sl]) / jnp.sum(p, axis=-1, keepdims=True)
        omx[:, sl] = oh.astype(BF16)
    y = None
    for r, (o_r, g_r) in enumerate(((odn_ref, g0_ref), (oda_ref, g1_ref), (omx, g2_ref))):
        gate = _sigmoid(g_r[...].astype(F32) + bg_ref[r:r + 1, :])
        term = gate * _dot(o_r[...], wb_ref[r])
        y = term if y is None else y + term
    out_ref[...] = x_ref[...] + _dot(y.astype(BF16), wo_ref[...])


def _merge(x2, o_dn, o_da, p, mk, mv, q_norm, b_gate, w_branch, w_out, seq):
    t = x2.shape[0]
    tm = min(512, seq)
    nt = seq // tm
    bw = 512
    n_mem = mk.shape[1]
    return pl.pallas_call(
        _merge_body,
        grid=(t // tm,),
        in_specs=[
            pl.BlockSpec((tm, D_MODEL), lambda i: (i, 0)),
            pl.BlockSpec((tm, bw), lambda i: (i, 0)),
            pl.BlockSpec((tm, bw), lambda i: (i, 0)),
            pl.BlockSpec((tm, bw), lambda i: (i, P_MXQ // bw)),
            pl.BlockSpec((tm, D_MODEL), lambda i: (i, P_GATE // D_MODEL)),
            pl.BlockSpec((tm, D_MODEL), lambda i: (i, P_GATE // D_MODEL + 1)),
            pl.BlockSpec((tm, D_MODEL), lambda i: (i, P_GATE // D_MODEL + 2)),
            pl.BlockSpec((1, n_mem, bw), lambda i: (i // nt, 0, 0)),
            pl.BlockSpec((1, n_mem, bw), lambda i: (i // nt, 0, 0)),
            pl.BlockSpec((1, MX_DH), lambda i: (0, 0)),
            pl.BlockSpec((3, D_MODEL), lambda i: (0, 0)),
            pl.BlockSpec((3, bw, D_MODEL), lambda i: (0, 0, 0)),
            pl.BlockSpec((D_MODEL, D_MODEL), lambda i: (0, 0)),
        ],
        out_specs=pl.BlockSpec((tm, D_MODEL), lambda i: (i, 0)),
        out_shape=jax.ShapeDtypeStruct((t, D_MODEL), F32),
        scratch_shapes=[pltpu.VMEM((tm, bw), BF16)],
        compiler_params=pltpu.CompilerParams(dimension_semantics=("parallel",)),
        name="merge",
    )(x2, o_dn, o_da, p, p, p, p, mk, mv, q_norm, b_gate, w_branch, w_out)


def _router_body(x_ref, g_ref, wr_ref, br_ref, h_ref, idx_ref, wt_ref, rank_ref, cnt_ref):
    tm = x_ref.shape[0]
    x = x_ref[...]
    h = x * _rms(x, D_MODEL) * g_ref[...]
    h_ref[...] = h.astype(BF16)
    logits = _dot3(h, wr_ref[...]) + br_ref[...]
    lane = lax.broadcasted_iota(I32, (tm, LANES), 1)
    lane_f = lane.astype(F32)
    work = logits
    sel = jnp.zeros((tm, LANES), F32)
    vals, idxs = [], []
    for _ in range(TOP_K):
        mx = jnp.max(work, axis=-1, keepdims=True)
        ik = jnp.min(jnp.where(work == mx, lane_f, float(LANES)), axis=-1, keepdims=True)
        hit = lane_f == ik
        sel = jnp.where(hit, 1.0, sel)
        work = jnp.where(hit, -jnp.inf, work)
        vals.append(mx)
        idxs.append(ik)
    es = [jnp.exp(v - vals[0]) for v in vals]
    den = es[0] + es[1] + es[2] + es[3]
    sub = cnt_ref.shape[0]
    mt = tm // sub
    r = lax.broadcasted_iota(I32, (tm, tm), 0)
    c = lax.broadcasted_iota(I32, (tm, tm), 1)
    tril = jnp.where((r > c) & (r // mt == c // mt), 1.0, 0.0).astype(BF16)
    cum = _dot(tril, sel.astype(BF16))
    idx_o = jnp.zeros((tm, LANES), F32)
    wt_o = jnp.zeros((tm, LANES), F32)
    rank_o = jnp.zeros((tm, LANES), F32)
    for k in range(TOP_K):
        rk = jnp.sum(jnp.where(lane_f == idxs[k], cum, 0.0), axis=-1, keepdims=True)
        idx_o = jnp.where(lane == k, idxs[k], idx_o)
        wt_o = jnp.where(lane == k, es[k] / den, wt_o)
        rank_o = jnp.where(lane == k, rk, rank_o)
    idx_ref[...] = idx_o.astype(I32)
    wt_ref[...] = wt_o
    rank_ref[...] = rank_o.astype(I32)
    for s in range(sub):
        cnt_ref[s] = jnp.broadcast_to(jnp.sum(sel[s * mt:(s + 1) * mt], axis=0, keepdims=True), (8, LANES))


def _router(x1, gain, w_r, b_r):
    t = x1.shape[0]
    mt = min(MOE_TILE, t)
    sub = 2 if t % (2 * mt) == 0 else 1
    tm = sub * mt
    row = lambda i: (i, 0)
    fixed = lambda i: (0, 0)
    return pl.pallas_call(
        _router_body,
        grid=(t // tm,),
        in_specs=[
            pl.BlockSpec((tm, D_MODEL), row),
            pl.BlockSpec((1, D_MODEL), fixed),
            pl.BlockSpec((D_MODEL, LANES), fixed),
            pl.BlockSpec((1, LANES), fixed),
        ],
        out_specs=[
            pl.BlockSpec((tm, D_MODEL), row),
            pl.BlockSpec((tm, LANES), row),
            pl.BlockSpec((tm, LANES), row),
            pl.BlockSpec((tm, LANES), row),
            pl.BlockSpec((sub, 8, LANES), lambda i: (i, 0, 0)),
        ],
        out_shape=[
            jax.ShapeDtypeStruct((t, D_MODEL), BF16),
            jax.ShapeDtypeStruct((t, LANES), I32),
            jax.ShapeDtypeStruct((t, LANES), F32),
            jax.ShapeDtypeStruct((t, LANES), I32),
            jax.ShapeDtypeStruct((t // mt, 8, LANES), F32),
        ],
        compiler_params=pltpu.CompilerParams(dimension_semantics=("arbitrary",)),
        name="router",
    )(x1, gain, w_r, b_r)


def _lane_cumsum(x):
    lane = lax.broadcasted_iota(I32, x.shape, 1)
    s = 1
    while s < N_EXPERTS:
        x = x + jnp.where(lane >= s, pltpu.roll(x, s, axis=1), 0.0)
        s *= 2
    return x


def _plan_body(cnt_ref, gs_ref, ls_ref, n8_ref, lsf_ref, tail_ref, meta_ref):
    nt = cnt_ref.shape[0]
    ga = float(GROUP_ALIGN)
    eb = float(EXPERT_BLOCK)
    lane = lax.broadcasted_iota(I32, (nt, LANES), 1)
    r8 = jnp.where(lane < N_EXPERTS, jnp.floor((cnt_ref[...] + (ga - 1.0)) * (1.0 / ga)) * ga, 0.0)
    ri = lax.broadcasted_iota(I32, (nt, nt), 0)
    ci = lax.broadcasted_iota(I32, (nt, nt), 1)
    before = _dot(jnp.where(ri > ci, 1.0, 0.0).astype(BF16), r8.astype(BF16))
    tot = jnp.sum(r8, axis=0, keepdims=True)
    region = jnp.floor((tot + (eb - 1.0)) * (1.0 / eb)) * eb
    pends = _lane_cumsum(jnp.broadcast_to(region, (8, LANES)))[0:1, :]
    pstart = pends - region
    lstart = _lane_cumsum(r8) - r8
    gs_ref[...] = (pstart + before).astype(I32)
    ls_ref[...] = lstart.astype(I32)
    pieces = r8 * (1.0 / ga)
    n8_ref[...] = jnp.where(lane == TOTAL_LANE, jnp.sum(pieces, axis=1, keepdims=True), pieces).astype(I32)
    lsf_ref[...] = lstart
    row8 = lax.broadcasted_iota(I32, (8, LANES), 0)
    tail_n = (region - tot) * (1.0 / ga)
    lane8 = lax.broadcasted_iota(I32, (8, LANES), 1)
    tail_n = jnp.where(lane8 == TOTAL_LANE, jnp.sum(tail_n, axis=1, keepdims=True), tail_n)
    tail = jnp.where(row8 == 0, pstart + tot, jnp.where(row8 == 1, tail_n, 0.0))
    tail_ref[...] = tail.astype(I32)
    nb = meta_ref.shape[0]
    ln = lax.broadcasted_iota(I32, (nb, LANES), 1)
    blk = lax.broadcasted_iota(I32, (nb, LANES), 0).astype(F32) * eb
    be = jnp.sum(jnp.where((ln < N_EXPERTS) & (pends <= blk), 1.0, 0.0), axis=-1, keepdims=True)
    be = jnp.minimum(be, float(N_EXPERTS - 1))
    used = jnp.sum(jnp.where(ln == N_EXPERTS - 1, pends, 0.0), axis=-1, keepdims=True) * (1.0 / eb)
    meta_ref[...] = jnp.where(ln == 0, be, jnp.where(ln == 1, used, 0.0)).astype(I32)


def _plan(cnt, n_blocks_pad):
    nt = cnt.shape[0]
    shp = jax.ShapeDtypeStruct((nt, LANES), I32)
    return pl.pallas_call(
        _plan_body,
        out_shape=[shp, shp, shp, jax.ShapeDtypeStruct((nt, LANES), F32),
                   jax.ShapeDtypeStruct((8, LANES), I32),
                   jax.ShapeDtypeStruct((n_blocks_pad, LANES), I32)],
        name="dispatch_plan",
    )(cnt)


def _local_positions(idx_ref, rank_ref, lsf_ref):
    tm = idx_ref.shape[0]
    lane = lax.broadcasted_iota(I32, (tm, LANES), 1)
    idx = idx_ref[...]
    rank = rank_ref[...].astype(F32)
    ls_row = lsf_ref[0]
    out = []
    for k in range(TOP_K):
        base = jnp.sum(jnp.where(lane == idx[:, k:k + 1], ls_row, 0.0), axis=-1, keepdims=True)
        out.append((base + rank[:, k:k + 1]).astype(I32))
    return out


def _local_positions_lanes(idx_ref, rank_ref, lsf_ref):
    tm = idx_ref.shape[0]
    idx_t = idx_ref[...].astype(F32).T
    rank_t = rank_ref[...].astype(F32).T
    ls_col = jnp.broadcast_to(lsf_ref[0], (LANES, LANES)).T[:, 0:1]
    expert = lax.broadcasted_iota(I32, (LANES, tm), 0).astype(F32)
    out = []
    for k in range(TOP_K):
        base = jnp.sum(jnp.where(expert == idx_t[k:k + 1, :], ls_col, 0.0), axis=0, keepdims=True)
        out.append((base + rank_t[k:k + 1, :]).astype(I32))
    return out


PACK_W = D_MODEL // 2
U32 = jnp.uint32


def _pack_rows(x):
    xb = x.astype(BF16).astype(F32)
    hi = lax.bitcast_convert_type(xb[:, :PACK_W], U32)
    lo = lax.bitcast_convert_type(xb[:, PACK_W:], U32)
    return hi | (lo >> 16)


def _unpack_rows(w):
    hi = lax.bitcast_convert_type(w & jnp.uint32(0xFFFF0000), F32)
    lo = lax.bitcast_convert_type(w << 16, F32)
    return hi.astype(BF16), lo.astype(BF16)


TOTAL_LANE = LANES - 1
WAIT_CHUNK = 128


def _start_group_copies(tables, tile, make_copy):
    gs_ref, ls_ref, n8_ref = tables

    def group(e, carry):
        g0 = gs_ref[tile, e]
        l0 = ls_ref[tile, e]

        def one(j, c):
            make_copy(pl.multiple_of(l0 + j * GROUP_ALIGN, GROUP_ALIGN),
                      pl.multiple_of(g0 + j * GROUP_ALIGN, GROUP_ALIGN)).start()
            return c

        lax.fori_loop(0, n8_ref[tile, e], one, 0)
        return carry

    lax.fori_loop(0, N_EXPERTS, group, 0)


def _wait_pieces(n, make_wait):
    def chunk(j, c):
        make_wait(WAIT_CHUNK).wait()
        return c

    lax.fori_loop(0, n // WAIT_CHUNK, chunk, 0)
    b = WAIT_CHUNK // 2
    while b >= 1:
        def _(b=b):
            make_wait(b).wait()
        pl.when((n & b) != 0)(_)
        b //= 2


def _dispatch_body(gs_ref, ls_ref, n8_ref, tail_ref, h_ref, idx_ref, rank_ref, lsf_ref, xs_ref,
                   xl, zbuf, sem):
    i = pl.program_id(0)
    last = i == pl.num_programs(0) - 1
    slot = i % 2
    tables = (gs_ref, ls_ref, n8_ref)
    tm = h_ref.shape[0]
    lrows = xl.shape[1]
    pos = lax.broadcasted_iota(I32, (lrows, tm), 0)
    lp = _local_positions_lanes(idx_ref, rank_ref, lsf_ref)
    hit = pos == lp[0]
    for k in range(1, TOP_K):
        hit = hit | (pos == lp[k])
    xl[slot] = _pack_rows(_dot(jnp.where(hit, 1.0, 0.0).astype(BF16), h_ref[...]))

    def copy_from(s):
        def make_copy(l, g):
            return pltpu.make_async_copy(xl.at[s, pl.ds(l, GROUP_ALIGN), :],
                                         xs_ref.at[pl.ds(g, GROUP_ALIGN), :], sem.at[s])
        return make_copy

    def wait_on(s):
        def make_wait(pieces):
            rows = pieces * GROUP_ALIGN
            return pltpu.make_async_copy(xl.at[0, pl.ds(0, rows), :], xs_ref.at[pl.ds(0, rows), :], sem.at[s])
        return make_wait

    @pl.when(i >= 1)
    def _():
        _wait_pieces(n8_ref[i - 1, TOTAL_LANE], wait_on(1 - slot))

    _start_group_copies(tables, i, copy_from(slot))

    def start_tails():
        def per_expert(e, carry):
            g0 = tail_ref[0, e]

            def one(j, c):
                pltpu.make_async_copy(
                    zbuf, xs_ref.at[pl.ds(pl.multiple_of(g0 + j * GROUP_ALIGN, GROUP_ALIGN), GROUP_ALIGN), :],
                    sem.at[2]).start()
                return c

            lax.fori_loop(0, tail_ref[1, e], one, 0)
            return carry

        lax.fori_loop(0, N_EXPERTS, per_expert, 0)

    @pl.when(last)
    def _():
        zbuf[...] = jnp.zeros_like(zbuf)
        start_tails()
        _wait_pieces(n8_ref[i, TOTAL_LANE], wait_on(slot))
        _wait_pieces(tail_ref[1, TOTAL_LANE], wait_on(2))


def _dispatch(gs, ls, n8, tail, h2, idx, rank, lsf, n_slots):
    t = h2.shape[0]
    tm = min(MOE_TILE, t)
    nt = t // tm
    lrows = tm * TOP_K + N_EXPERTS * GROUP_ALIGN
    row = lambda i, *_: (i, 0)
    grid_spec = pltpu.PrefetchScalarGridSpec(
        num_scalar_prefetch=4,
        grid=(nt,),
        in_specs=[
            pl.BlockSpec((tm, D_MODEL), row),
            pl.BlockSpec((tm, LANES), row),
            pl.BlockSpec((tm, LANES), row),
            pl.BlockSpec((1, 1, LANES), lambda i, *_: (i, 0, 0)),
        ],
        out_specs=pl.BlockSpec(memory_space=pl.ANY),
        scratch_shapes=[pltpu.VMEM((2, lrows, PACK_W), U32), pltpu.VMEM((GROUP_ALIGN, PACK_W), U32),
                        pltpu.SemaphoreType.DMA((3,))],
    )
    return pl.pallas_call(
        _dispatch_body,
        grid_spec=grid_spec,
        out_shape=jax.ShapeDtypeStruct((n_slots, PACK_W), U32),
        compiler_params=pltpu.CompilerParams(dimension_semantics=("arbitrary",)),
        name="moe_dispatch",
    )(gs, ls, n8, tail, h2, idx, rank, lsf)


def _expert_body(be_ref, nu_ref, x_ref, wgu_ref, bgu_ref, wd_ref, bd_ref, y_ref, wgu_b, wd_b):
    j = pl.program_id(0)

    @pl.when(j < nu_ref[0])
    def _():
        @pl.when(jnp.logical_or(j == 0, be_ref[j] != be_ref[jnp.maximum(j - 1, 0)]))
        def _():
            wgu_b[...] = wgu_ref[0].astype(BF16)
            wd_b[...] = wd_ref[0].astype(BF16)

        gu = _dot(jnp.concatenate(_unpack_rows(x_ref[...]), axis=1), wgu_b[...]) + bgu_ref[0]
        gate = jnp.minimum(gu[:, :D_FF], SWIGLU_LIMIT)
        up = jnp.clip(gu[:, D_FF:], -SWIGLU_LIMIT, SWIGLU_LIMIT)
        act = (up + 1.0) * gate * _sigmoid(SWIGLU_ALPHA * gate)
        y_ref[...] = _pack_rows(_dot(act.astype(BF16), wd_b[...]) + bd_ref[0])


def _experts(block_e, n_used, xs, wgu, bgu, wd, bd):
    n_slots = xs.shape[0]
    nb = n_slots // EXPERT_BLOCK

    def blk(j, be, nu):
        return jnp.minimum(j, nu[0] - 1)

    grid_spec = pltpu.PrefetchScalarGridSpec(
        num_scalar_prefetch=2,
        grid=(nb,),
        in_specs=[
            pl.BlockSpec((EXPERT_BLOCK, PACK_W), lambda j, be, nu: (blk(j, be, nu), 0)),
            pl.BlockSpec((1, D_MODEL, 2 * D_FF), lambda j, be, nu: (be[blk(j, be, nu)], 0, 0)),
            pl.BlockSpec((1, 1, 2 * D_FF), lambda j, be, nu: (be[blk(j, be, nu)], 0, 0)),
            pl.BlockSpec((1, D_FF, D_MODEL), lambda j, be, nu: (be[blk(j, be, nu)], 0, 0)),
            pl.BlockSpec((1, 1, D_MODEL), lambda j, be, nu: (be[blk(j, be, nu)], 0, 0)),
        ],
        out_specs=pl.BlockSpec((EXPERT_BLOCK, PACK_W), lambda j, be, nu: (blk(j, be, nu), 0)),
        scratch_shapes=[pltpu.VMEM((D_MODEL, 2 * D_FF), BF16), pltpu.VMEM((D_FF, D_MODEL), BF16)],
    )
    return pl.pallas_call(
        _expert_body,
        grid_spec=grid_spec,
        out_shape=jax.ShapeDtypeStruct((n_slots, PACK_W), U32),
        compiler_params=pltpu.CompilerParams(dimension_semantics=("arbitrary",)),
        name="moe_experts",
    )(block_e, n_used, xs, wgu, bgu, wd, bd)


def _combine_body(gs_ref, ls_ref, n8_ref, x_ref, wt_ref, idx_ref, rank_ref, lsf_ref, y_ref, out_ref,
                  yl, sem):
    i = pl.program_id(0)
    slot = i % 2
    tables = (gs_ref, ls_ref, n8_ref)
    tm = x_ref.shape[0]
    lrows = yl.shape[1]

    def copy_into(s):
        def make_copy(l, g):
            return pltpu.make_async_copy(y_ref.at[pl.ds(g, GROUP_ALIGN), :],
                                         yl.at[s, pl.ds(l, GROUP_ALIGN), :], sem.at[s])
        return make_copy

    def make_wait(pieces):
        rows = pieces * GROUP_ALIGN
        return pltpu.make_async_copy(y_ref.at[pl.ds(0, rows), :], yl.at[0, pl.ds(0, rows), :], sem.at[slot])

    @pl.when(i == 0)
    def _():
        yl[...] = jnp.zeros_like(yl)
        _start_group_copies(tables, i, copy_into(slot))

    @pl.when(i + 1 < pl.num_programs(0))
    def _():
        _start_group_copies(tables, i + 1, copy_into(1 - slot))

    pos = lax.broadcasted_iota(I32, (tm, lrows), 1)
    lp = _local_positions(idx_ref, rank_ref, lsf_ref)
    wt = wt_ref[...]
    wm = jnp.zeros((tm, lrows), F32)
    for k in range(TOP_K):
        wm = jnp.where(pos == lp[k], wt[:, k:k + 1], wm)
    _wait_pieces(n8_ref[i, TOTAL_LANE], make_wait)
    wmb = wm.astype(BF16)
    y_hi, y_lo = _unpack_rows(yl[slot])
    out_ref[:, :PACK_W] = x_ref[:, :PACK_W] + _dot(wmb, y_hi)
    out_ref[:, PACK_W:] = x_ref[:, PACK_W:] + _dot(wmb, y_lo)


def _combine(gs, ls, n8, x1, wts, idx, rank, lsf, y):
    t = x1.shape[0]
    tm = min(MOE_TILE, t)
    nt = t // tm
    lrows = tm * TOP_K + N_EXPERTS * GROUP_ALIGN
    row = lambda i, *_: (i, 0)
    grid_spec = pltpu.PrefetchScalarGridSpec(
        num_scalar_prefetch=3,
        grid=(nt,),
        in_specs=[
            pl.BlockSpec((tm, D_MODEL), row),
            pl.BlockSpec((tm, LANES), row),
            pl.BlockSpec((tm, LANES), row),
            pl.BlockSpec((tm, LANES), row),
            pl.BlockSpec((1, 1, LANES), lambda i, *_: (i, 0, 0)),
            pl.BlockSpec(memory_space=pl.ANY),
        ],
        out_specs=pl.BlockSpec((tm, D_MODEL), row),
        scratch_shapes=[pltpu.VMEM((2, lrows, PACK_W), U32), pltpu.SemaphoreType.DMA((2,))],
    )
    return pl.pallas_call(
        _combine_body,
        grid_spec=grid_spec,
        out_shape=jax.ShapeDtypeStruct((t, D_MODEL), F32),
        compiler_params=pltpu.CompilerParams(dimension_semantics=("arbitrary",)),
        name="moe_combine",
    )(gs, ls, n8, x1, wts, idx, rank, lsf, y)


def _pad_lanes(v, fill=0.0):
    v = v.astype(F32).reshape(1, -1)
    return jnp.pad(v, ((0, 0), (0, LANES - v.shape[1])), constant_values=fill)


def _mixer(x2, mem, rel_table, attn_norm, w_in, b_gate, dn_conv, dn_a_log, dn_dt_bias, dn_out_norm,
           da_q_norm, da_k_norm, da_lambda, da_subln, mem_norm, w_mem_kv, mx_q_norm, mx_k_norm,
           w_branch, w_out, batch, seq):
    wp = jnp.concatenate([w_in[:, :W_AB_LO], w_in[:, W_AB_HI:]], axis=1).astype(BF16)
    wab = jnp.pad(w_in[:, W_AB_LO:W_AB_HI], ((0, 0), (0, LANES - (W_AB_HI - W_AB_LO))))
    p, ab = _inproj(x2, attn_norm.reshape(1, -1), wp, wab)

    o_dn = _deltanet(p, ab, dn_conv, _pad_lanes(dn_a_log), _pad_lanes(dn_dt_bias),
                     dn_out_norm.reshape(1, -1), batch, seq)

    tq = min(ATT_BLOCK, seq)
    bias = _bias_tiles(rel_table.T, tq)
    o_da = _attention(p, bias, jnp.tile(da_q_norm, 2).reshape(1, -1), jnp.tile(da_k_norm, 2).reshape(1, -1),
                      da_lambda, da_subln.reshape(-1, 1), batch, seq)

    mk, mv = _memkv(mem, mem_norm.reshape(1, -1), w_mem_kv.astype(BF16), mx_k_norm.reshape(1, -1))
    return _merge(x2, o_dn, o_da, p, mk, mv, mx_q_norm.reshape(1, -1), b_gate.reshape(3, D_MODEL),
                  w_branch.astype(BF16), w_out.astype(BF16), seq)


def _moe(x1, ffn_norm, w_router, b_router, w_gate_up, b_gate_up, w_down, b_down):
    t = x1.shape[0]
    nt = t // min(MOE_TILE, t)
    max_rows = t * TOP_K + nt * N_EXPERTS * (GROUP_ALIGN - 1)
    n_blocks = -(-max_rows // EXPERT_BLOCK) + N_EXPERTS
    n_blocks_pad = -(-n_blocks // 8) * 8
    n_slots = n_blocks * EXPERT_BLOCK

    wr = jnp.pad(w_router, ((0, 0), (0, LANES - N_EXPERTS)))
    h2, idx, wts, rank, cnt = _router(x1, ffn_norm.reshape(1, -1), wr, _pad_lanes(b_router, NEG))
    gs, ls, n8, lsf, tail, meta = _plan(cnt[:, 0, :], n_blocks_pad)
    block_e = meta[:n_blocks, 0]
    n_used = meta[0:1, 1]
    lsf = lsf.reshape(nt, 1, LANES)

    xs = _dispatch(gs, ls, n8, tail, h2, idx, rank, lsf, n_slots)
    y = _experts(block_e, n_used, xs, w_gate_up, b_gate_up.reshape(N_EXPERTS, 1, -1),
                 w_down, b_down.reshape(N_EXPERTS, 1, -1))
    return _combine(gs, ls, n8, x1, wts, idx, rank, lsf, y)


def kernel(x, mem, rel_table, attn_norm, w_in, b_gate, dn_conv, dn_a_log, dn_dt_bias, dn_out_norm,
           da_q_norm, da_k_norm, da_lambda, da_subln, mem_norm, w_mem_kv, mx_q_norm, mx_k_norm,
           w_branch, w_out, ffn_norm, w_router, b_router, w_gate_up, b_gate_up, w_down, b_down):
    batch, seq, d = x.shape
    x2 = x.reshape(batch * seq, d)
    x1 = _mixer(x2, mem, rel_table, attn_norm[0], w_in[0], b_gate[0], dn_conv[0], dn_a_log[0],
                dn_dt_bias[0], dn_out_norm[0], da_q_norm[0], da_k_norm[0], da_lambda[0], da_subln[0],
                mem_norm[0], w_mem_kv[0], mx_q_norm[0], mx_k_norm[0], w_branch[0], w_out[0], batch, seq)
    out = _moe(x1, ffn_norm[0], w_router[0], b_router[0], w_gate_up[0], b_gate_up[0], w_down[0],
               b_down[0])
    return out.reshape(batch, seq, d)
```

```python
import functools
import math

import jax
import jax.numpy as jnp
from jax import lax
from jax.experimental import pallas as pl
from jax.experimental.pallas import tpu as pltpu

F32 = jnp.float32
BF16 = jnp.bfloat16
I32 = jnp.int32

D_MODEL = 1024
EPS = 1e-6
LANES = 128

DN_HEADS = 4
DN_DK = 128
DN_CHUNK = 64
DN_CONV = 4

DA_HEADS = 4
DA_DH = 64

MX_HEADS = 4
MX_DH = 128

REL_BUCKETS = 32
REL_MAX_DIST = 128

N_EXPERTS = 32
TOP_K = 4
D_FF = 1024
SWIGLU_LIMIT = 7.0
SWIGLU_ALPHA = 1.702
EXPERT_BLOCK = 512
MOE_TILE = 256
GROUP_ALIGN = 8

LAM_INIT = 0.8 - 0.6 * math.exp(-0.3 * 0)
LOG2E = 1.4426950408889634
NEG = -1e30

P_DNQ, P_DNK, P_DNV, P_DNZ = 0, 512, 1024, 1536
P_DAQ, P_DAK, P_DAV = 2048, 2560, 3072
P_MXQ = 3584
P_GATE = 4096
P_COLS = 7168
W_AB_LO, W_AB_HI = 2048, 2056


def _dot(a, b):
    return jnp.dot(a, b, preferred_element_type=F32)


def _dot_nt(a, b):
    return lax.dot_general(a, b, (((1,), (1,)), ((), ())), preferred_element_type=F32)


def _dot_tn(a, b):
    return lax.dot_general(a, b, (((0,), (0,)), ((), ())), preferred_element_type=F32)


def _split(x):
    hi = x.astype(BF16)
    lo = (x - hi.astype(F32)).astype(BF16)
    return hi, lo


def _dot3(a, b):
    ah, al = _split(a)
    bh, bl = _split(b)
    return _dot(ah, bh) + _dot(ah, bl) + _dot(al, bh)


def _sigmoid(x):
    return 1.0 / (1.0 + jnp.exp(-x))


def _rms(x, n):
    return lax.rsqrt(jnp.sum(x * x, axis=-1, keepdims=True) * (1.0 / n) + EPS)


def _inproj_body(x_ref, g_ref, w_ref, wab_ref, p_ref, ab_ref, h_scr):
    @pl.when(pl.program_id(1) == 0)
    def _():
        x = x_ref[...]
        h = x * _rms(x, D_MODEL) * g_ref[...]
        h_scr[...] = h.astype(BF16)
        ab_ref[...] = _dot3(h, wab_ref[...])

    p_ref[...] = _dot(h_scr[...], w_ref[...]).astype(p_ref.dtype)


def _inproj(x2, gain, wp, wab):
    t = x2.shape[0]
    tm = min(1024, t)
    tn = 1024
    return pl.pallas_call(
        _inproj_body,
        grid=(t // tm, P_COLS // tn),
        in_specs=[
            pl.BlockSpec((tm, D_MODEL), lambda i, j: (i, 0)),
            pl.BlockSpec((1, D_MODEL), lambda i, j: (0, 0)),
            pl.BlockSpec((D_MODEL, tn), lambda i, j: (0, j)),
            pl.BlockSpec((D_MODEL, LANES), lambda i, j: (0, 0)),
        ],
        out_specs=[
            pl.BlockSpec((tm, tn), lambda i, j: (i, j)),
            pl.BlockSpec((tm, LANES), lambda i, j: (i, 0)),
        ],
        out_shape=[
            jax.ShapeDtypeStruct((t, P_COLS), BF16),
            jax.ShapeDtypeStruct((t, LANES), F32),
        ],
        scratch_shapes=[pltpu.VMEM((tm, D_MODEL), BF16)],
        compiler_params=pltpu.CompilerParams(dimension_semantics=("parallel", "arbitrary")),
        name="inproj",
    )(x2, gain, wp, wab)


DN_HALO = 16
DN_SCAN_CHUNK = 256


def _deltanet_body(q_ref, k_ref, v_ref, z_ref, qh_ref, kh_ref, vh_ref, ab_ref, cw_ref, alog_ref,
                   dtb_ref, on_ref, o_ref, stage, qs, ks, vs, s_scr):
    i = pl.program_id(1)
    tc = q_ref.shape[0]
    hw = DN_HEADS * DN_DK

    @pl.when(i == 0)
    def _():
        s_scr[...] = jnp.zeros_like(s_scr)

    for src, halo, dst, off, kind in ((q_ref, qh_ref, qs, 0, "q"), (k_ref, kh_ref, ks, hw, "k"),
                                      (v_ref, vh_ref, vs, 2 * hw, "v")):
        hal = halo[...].astype(F32)
        stage[0:DN_HALO, :] = jnp.where(i == 0, 0.0, hal)
        stage[DN_HALO:DN_HALO + tc, :] = src[...].astype(F32)
        base = DN_HALO - (DN_CONV - 1)
        y = stage[base:base + tc, :] * cw_ref[0:1, off:off + hw]
        for j in range(1, DN_CONV):
            y = y + stage[base + j:base + j + tc, :] * cw_ref[j:j + 1, off:off + hw]
        y = y * _sigmoid(y)
        if kind == "v":
            dst[...] = y
        else:
            for h in range(DN_HEADS):
                sl = slice(h * DN_DK, (h + 1) * DN_DK)
                yh = y[:, sl]
                r = lax.rsqrt(jnp.sum(yh * yh, axis=-1, keepdims=True) + EPS)
                if kind == "q":
                    r = r * (DN_DK ** -0.5)
                dst[:, sl] = yh * r

    c = min(DN_SCAN_CHUNK, tc)
    row = lax.broadcasted_iota(I32, (c, c), 0)
    col = lax.broadcasted_iota(I32, (c, c), 1)
    incl = row >= col
    strict = row > col
    same_blk = (row // DN_CHUNK) == (col // DN_CHUNK)
    tri = jnp.where(incl, 1.0, 0.0).astype(BF16)
    eye = jnp.where(row == col, 1.0, 0.0)
    neg_a = -jnp.exp(alog_ref[...])
    dtb = dtb_ref[...]

    def chunk(ci, carry):
        r0 = pl.multiple_of(ci * c, c)
        abc = ab_ref[pl.ds(r0, c), :]
        a_in = abc + dtb
        g_all = neg_a * (jnp.maximum(a_in, 0.0) + jnp.log(1.0 + jnp.exp(-jnp.abs(a_in))))
        beta_all = _sigmoid(abc)
        zc = z_ref[pl.ds(r0, c), :].astype(F32)
        hs = range(DN_HEADS)
        sls = [slice(h * DN_DK, (h + 1) * DN_DK) for h in hs]
        q = [qs[pl.ds(r0, c), sl] for sl in sls]
        k = [ks[pl.ds(r0, c), sl] for sl in sls]
        v = [vs[pl.ds(r0, c), sl] for sl in sls]
        g = [g_all[:, h:h + 1] for h in hs]
        beta = [beta_all[:, DN_HEADS + h:DN_HEADS + h + 1] for h in hs]
        g_split = [_split(jnp.where(strict, g[h], 0.0)) for h in hs]
        diff = [_dot(tri, g_split[h][0]) + _dot(tri, g_split[h][1]) for h in hs]
        kb = [k[h].astype(BF16) for h in hs]
        qkk = [_dot_nt(jnp.concatenate([q[h].astype(BF16), kb[h]], axis=0), kb[h]) for h in hs]
        gc = [diff[h][:, 0:1] + g[h][0:1, :] for h in hs]
        decay = [jnp.where(incl, jnp.exp(diff[h]), 0.0) for h in hs]
        lower = [jnp.where(strict, qkk[h][c:] * decay[h] * beta[h], 0.0) for h in hs]
        pw = [jnp.where(same_blk, -lower[h], 0.0) for h in hs]
        dinv = [eye + pw[h] for h in hs]
        pwb = [pw[h].astype(BF16) for h in hs]
        for _ in range(int(math.log2(DN_CHUNK)) - 1):
            pwb = [_dot(pwb[h], pwb[h]).astype(BF16) for h in hs]
            dinv = [dinv[h] + _dot(dinv[h].astype(BF16), pwb[h]) for h in hs]
        dinv_b = [dinv[h].astype(BF16) for h in hs]
        pw = [-_dot(dinv_b[h], jnp.where(same_blk, 0.0, lower[h]).astype(BF16)) for h in hs]
        xm = [eye + pw[h] for h in hs]
        pwb = [pw[h].astype(BF16) for h in hs]
        for _ in range(int(math.log2(c // DN_CHUNK)) - 1):
            pwb = [_dot(pwb[h], pwb[h]).astype(BF16) for h in hs]
            xm = [xm[h] + _dot(xm[h].astype(BF16), pwb[h]) for h in hs]
        inv = [_dot(xm[h].astype(BF16), dinv_b[h]).astype(BF16) for h in hs]
        egc = [jnp.exp(gc[h]) for h in hs]
        rhs = [jnp.concatenate([v[h] * beta[h], k[h] * (beta[h] * egc[h])], axis=1).astype(BF16) for h in hs]
        sol = [_dot(inv[h], rhs[h]) for h in hs]
        qkm = [jnp.where(incl, qkk[h][:c] * decay[h], 0.0).astype(BF16) for h in hs]
        gl = [gc[h][c - 1:c, :] for h in hs]
        state = [s_scr[h] for h in hs]
        ws = [_dot(jnp.concatenate([sol[h][:, DN_DK:].astype(BF16), (q[h] * egc[h]).astype(BF16)], axis=0),
                   state[h].astype(BF16)) for h in hs]
        v_new = [sol[h][:, :DN_DK] - ws[h][:c] for h in hs]
        o = [ws[h][c:] + _dot(qkm[h], v_new[h].astype(BF16)) for h in hs]
        for h in hs:
            s_scr[h] = state[h] * jnp.exp(gl[h]) + _dot_tn(kb[h], (v_new[h] * jnp.exp(gl[h] - gc[h])).astype(BF16))
        for h in hs:
            zz = zc[:, sls[h]]
            on = o[h] * _rms(o[h], DN_DK) * on_ref[...]
            o_ref[pl.ds(r0, c), sls[h]] = (on * (zz * _sigmoid(zz))).astype(o_ref.dtype)
        return carry

    lax.fori_loop(0, tc // c, chunk, 0, unroll=True)


def _deltanet(p, ab, conv_w, alog_row, dtb_row, out_norm, batch, seq):
    t = batch * seq
    tc = min(512, seq)
    nt = seq // tc
    hw = DN_HEADS * DN_DK

    def main(cb):
        return pl.BlockSpec((tc, hw), lambda b, i: (b * nt + i, cb))

    def halo(cb):
        return pl.BlockSpec(
            (DN_HALO, hw),
            lambda b, i: (jnp.maximum((b * seq + i * tc) // DN_HALO - 1, 0), cb))

    return pl.pallas_call(
        _deltanet_body,
        grid=(batch, nt),
        in_specs=[
            main(P_DNQ // hw), main(P_DNK // hw), main(P_DNV // hw), main(P_DNZ // hw),
            halo(P_DNQ // hw), halo(P_DNK // hw), halo(P_DNV // hw),
            pl.BlockSpec((tc, LANES), lambda b, i: (b * nt + i, 0)),
            pl.BlockSpec((DN_CONV, 3 * hw), lambda b, i: (0, 0)),
            pl.BlockSpec((1, LANES), lambda b, i: (0, 0)),
            pl.BlockSpec((1, LANES), lambda b, i: (0, 0)),
            pl.BlockSpec((1, DN_DK), lambda b, i: (0, 0)),
        ],
        out_specs=pl.BlockSpec((tc, hw), lambda b, i: (b * nt + i, 0)),
        out_shape=jax.ShapeDtypeStruct((t, hw), BF16),
        scratch_shapes=[
            pltpu.VMEM((DN_HALO + tc, hw), F32),
            pltpu.VMEM((tc, hw), F32),
            pltpu.VMEM((tc, hw), F32),
            pltpu.VMEM((tc, hw), F32),
            pltpu.VMEM((DN_HEADS, DN_DK, DN_DK), F32),
        ],
        compiler_params=pltpu.CompilerParams(dimension_semantics=("parallel", "arbitrary")),
        name="deltanet",
    )(p, p, p, p, p, p, p, ab, conv_w, alog_row, dtb_row, out_norm)


ATT_BLOCK = 512


def _bias_body(tbl_ref, o_ref):
    h = pl.program_id(0)
    tq = o_ref.shape[2]
    key = lax.broadcasted_iota(I32, (tq, tq), 0)
    qry = lax.broadcasted_iota(I32, (tq, tq), 1)
    max_exact = REL_BUCKETS // 2
    far = tbl_ref[h, REL_BUCKETS - 1]
    for d in range(2):
        n = qry - key + d * tq
        nn = jnp.maximum(n, 0)
        nf = jnp.maximum(nn, 1).astype(F32)
        large = max_exact + (jnp.log(nf / max_exact) / math.log(REL_MAX_DIST / max_exact)
                             * (REL_BUCKETS - max_exact)).astype(I32)
        large = jnp.minimum(large, REL_BUCKETS - 1)
        bucket = jnp.where(nn < max_exact, nn, large)
        val = jnp.zeros((tq, tq), F32)
        for b in range(REL_BUCKETS):
            val = jnp.where(bucket == b, tbl_ref[h, b], val)
        o_ref[0, d] = jnp.where(n >= 0, (val - far) * LOG2E, NEG)


def _bias_tiles(tbl_t, tq):
    return pl.pallas_call(
        _bias_body,
        grid=(DA_HEADS,),
        in_specs=[pl.BlockSpec(memory_space=pltpu.SMEM)],
        out_specs=pl.BlockSpec((1, 2, tq, tq), lambda h: (h, 0, 0, 0)),
        out_shape=jax.ShapeDtypeStruct((DA_HEADS, 2, tq, tq), F32),
        name="t5_bias_tiles",
    )(tbl_t)


DA_DV = 2 * DA_DH
DA_VROWS = DA_DV + 16


BOUND_SLACK = 1.02
MAX_SHIFT_GAP = 110.0


def _attn_body(q_ref, k_ref, v_ref, bias_ref, qg_ref, kg_ref, lam_ref, sg_ref, o_ref,
               kn, vt, kst, m_s, acc_s):
    qi = pl.program_id(2)
    tq = q_ref.shape[0]
    seq = k_ref.shape[0]
    tk = tq
    lo_mask = lax.broadcasted_iota(I32, (1, DA_DV), 1) < DA_DH

    def group_norm(x, gain):
        x2 = x * x
        lo = jnp.sum(jnp.where(lo_mask, x2, 0.0), axis=-1, keepdims=True)
        hi = jnp.sum(jnp.where(lo_mask, 0.0, x2), axis=-1, keepdims=True)
        r = jnp.where(lo_mask, lax.rsqrt(lo * (1.0 / DA_DH) + EPS), lax.rsqrt(hi * (1.0 / DA_DH) + EPS))
        return x * r * gain

    @pl.when(qi == 0)
    def _():
        ones = jnp.ones((DA_VROWS - DA_DV, tk), BF16)

        def body(c, kmax2):
            r0 = pl.multiple_of(c * tk, tk)
            kb = group_norm(k_ref[pl.ds(r0, tk), :].astype(F32), kg_ref[...]).astype(BF16)
            kn[pl.ds(r0, tk), :] = kb
            vt[c, 0:DA_DV, :] = v_ref[pl.ds(r0, tk), :].astype(F32).T.astype(BF16)
            vt[c, DA_DV:DA_VROWS, :] = ones
            k2 = kb.astype(F32)
            k2 = k2 * k2
            lo = jnp.max(jnp.sum(jnp.where(lo_mask, k2, 0.0), axis=-1, keepdims=True), axis=0, keepdims=True)
            hi = jnp.max(jnp.sum(jnp.where(lo_mask, 0.0, k2), axis=-1, keepdims=True), axis=0, keepdims=True)
            return jnp.maximum(kmax2, jnp.where(lo_mask, lo, hi))
        kst[0:1, :] = lax.fori_loop(0, seq // tk, body, jnp.zeros((1, DA_DV), F32))
        b0 = bias_ref[0, 0]
        b1 = bias_ref[0, 1]
        bmax = jnp.maximum(jnp.max(jnp.maximum(b0, b1), axis=0, keepdims=True), 0.0)
        bmin = jnp.minimum(jnp.min(jnp.minimum(jnp.where(b0 > 0.5 * NEG, b0, 0.0), b1), axis=0, keepdims=True), 0.0)
        kst[1:2, :] = jnp.broadcast_to(jnp.max(bmax, axis=1, keepdims=True), (1, DA_DV))
        kst[2:3, :] = jnp.broadcast_to(jnp.min(bmin, axis=1, keepdims=True), (1, DA_DV))

    q = group_norm(q_ref[...].astype(F32), qg_ref[...]) * (DA_DH ** -0.5 * LOG2E)
    qcat = jnp.concatenate([jnp.where(lo_mask, q, 0.0), jnp.where(lo_mask, 0.0, q)], axis=0).astype(BF16)
    acc_s[...] = jnp.zeros_like(acc_s)

    q2 = q * q * kst[0:1, :]
    ones8 = jnp.ones((8, DA_DV), BF16)
    bmax = kst[1:2, 0:1]
    bmin = kst[2:3, 0:1]
    bound = []
    for m in range(2):
        q2m = jnp.where(lo_mask, q2, 0.0) if m == 0 else jnp.where(lo_mask, 0.0, q2)
        bound.append(jnp.sqrt(_dot_nt(ones8, q2m.astype(BF16))[0:1, :]) * BOUND_SLACK)
    bound = jnp.concatenate(bound, axis=1)
    worst = jnp.max(2.0 * bound, axis=1, keepdims=True) + bmax - bmin
    safe = worst[0, 0] <= MAX_SHIFT_GAP

    def block(j, d, fixed_shift):
        r0 = pl.multiple_of(j * tk, tk)
        st = _dot_nt(kn[pl.ds(r0, tk), :], qcat)
        if d is not None:
            bias = bias_ref[0, d]
            st = st + jnp.concatenate([bias, bias], axis=1)
        if fixed_shift:
            acc_s[...] = acc_s[...] + _dot(vt[j], jnp.exp2(st - m_s[...]).astype(BF16))
        else:
            m_prev = m_s[...]
            m_new = jnp.maximum(m_prev, jnp.max(st, axis=0, keepdims=True))
            alpha = jnp.exp2(m_prev - m_new)
            acc_s[...] = alpha * acc_s[...] + _dot(vt[j], jnp.exp2(st - m_new).astype(BF16))
            m_s[...] = m_new

    def far_fixed(js):
        sts = [_dot_nt(kn[pl.ds(pl.multiple_of(j * tk, tk), tk), :], qcat) for j in js]
        shift = m_s[...]
        pts = [jnp.exp2(st - shift).astype(BF16) for st in sts]
        tot = _dot(vt[js[0]], pts[0])
        for j, pt in zip(js[1:], pts[1:]):
            tot = tot + _dot(vt[j], pt)
        acc_s[...] = acc_s[...] + tot

    def run(fixed_shift):
        n_far = jnp.maximum(qi - 1, 0)

        if fixed_shift:
            def far_quad(jj, carry):
                far_fixed([4 * jj + u for u in range(4)])
                return carry

            lax.fori_loop(0, n_far // 4, far_quad, 0)
            rem = n_far % 4

            @pl.when(rem >= 2)
            def _():
                far_fixed([n_far - rem, n_far - rem + 1])

            @pl.when(rem % 2 == 1)
            def _():
                far_fixed([n_far - 1])
        else:
            def far_one(j, carry):
                block(j, None, False)
                return carry

            lax.fori_loop(0, n_far, far_one, 0)

        @pl.when(qi >= 1)
        def _():
            block(qi - 1, 1, fixed_shift)

        block(qi, 0, fixed_shift)

    @pl.when(safe)
    def _():
        m_s[...] = bound + bmax
        run(True)

    @pl.when(jnp.logical_not(safe))
    def _():
        m_s[...] = jnp.full(m_s.shape, NEG, F32)
        run(False)

    lam_p = lam_ref[...]
    lam = (jnp.exp(jnp.sum(lam_p[0:1, :] * lam_p[1:2, :], axis=-1, keepdims=True))
           - jnp.exp(jnp.sum(lam_p[2:3, :] * lam_p[3:4, :], axis=-1, keepdims=True)) + LAM_INIT)
    a0 = acc_s[:, 0:tq]
    a1 = acc_s[:, tq:2 * tq]
    ot = a0[0:DA_DV] / a0[DA_DV:DA_DV + 1] - lam * (a1[0:DA_DV] / a1[DA_DV:DA_DV + 1])
    r = lax.rsqrt(jnp.sum(ot * ot, axis=0, keepdims=True) * (1.0 / DA_DV) + EPS)
    ot = ot * r * (sg_ref[...] * (1.0 - LAM_INIT))
    o_ref[...] = ot.T.astype(o_ref.dtype)


def _attention(p, bias, qg, kg, lam_p, subln, batch, seq):
    t = batch * seq
    tq = min(ATT_BLOCK, seq)
    nq = seq // tq
    dv = DA_DV
    return pl.pallas_call(
        _attn_body,
        grid=(batch, DA_HEADS, nq),
        in_specs=[
            pl.BlockSpec((tq, dv), lambda b, h, i: (b * nq + i, P_DAQ // dv + h)),
            pl.BlockSpec((seq, dv), lambda b, h, i: (b, P_DAK // dv + h)),
            pl.BlockSpec((seq, dv), lambda b, h, i: (b, P_DAV // dv + h)),
            pl.BlockSpec((1, 2, tq, tq), lambda b, h, i: (h, 0, 0, 0)),
            pl.BlockSpec((1, dv), lambda b, h, i: (0, 0)),
            pl.BlockSpec((1, dv), lambda b, h, i: (0, 0)),
            pl.BlockSpec((4, DA_DH), lambda b, h, i: (0, 0)),
            pl.BlockSpec((dv, 1), lambda b, h, i: (0, 0)),
        ],
        out_specs=pl.BlockSpec((tq, dv), lambda b, h, i: (b * nq + i, h)),
        out_shape=jax.ShapeDtypeStruct((t, DA_HEADS * dv), BF16),
        scratch_shapes=[
            pltpu.VMEM((seq, dv), BF16),
            pltpu.VMEM((seq // tq, DA_VROWS, tq), BF16),
            pltpu.VMEM((8, dv), F32),
            pltpu.VMEM((1, 2 * tq), F32),
            pltpu.VMEM((DA_VROWS, 2 * tq), F32),
        ],
        compiler_params=pltpu.CompilerParams(dimension_semantics=("parallel", "parallel", "arbitrary")),
        name="diff_attention",
    )(p, p, p, bias, qg, kg, lam_p, subln)


def _memkv_body(mem_ref, mg_ref, w_ref, kg_ref, mk_ref, mv_ref):
    x = mem_ref[0]
    xn = x * _rms(x, D_MODEL) * mg_ref[...]
    kv = _dot(xn.astype(BF16), w_ref[...])
    hw = MX_HEADS * MX_DH
    for h in range(MX_HEADS):
        sl = slice(h * MX_DH, (h + 1) * MX_DH)
        kh = kv[:, sl]
        mk_ref[0, :, sl] = (kh * _rms(kh, MX_DH) * kg_ref[...]).astype(BF16)
    mv_ref[0] = kv[:, hw:].astype(BF16)


def _memkv(mem, mem_norm, w_kv, k_norm):
    b, n, _ = mem.shape
    hw = MX_HEADS * MX_DH
    return pl.pallas_call(
        _memkv_body,
        grid=(b,),
        in_specs=[
            pl.BlockSpec((1, n, D_MODEL), lambda i: (i, 0, 0)),
            pl.BlockSpec((1, D_MODEL), lambda i: (0, 0)),
            pl.BlockSpec((D_MODEL, 2 * hw), lambda i: (0, 0)),
            pl.BlockSpec((1, MX_DH), lambda i: (0, 0)),
        ],
        out_specs=[pl.BlockSpec((1, n, hw), lambda i: (i, 0, 0))] * 2,
        out_shape=[jax.ShapeDtypeStruct((b, n, hw), BF16)] * 2,
        name="memory_kv",
    )(mem, mem_norm, w_kv, k_norm)


def _merge_body(x_ref, odn_ref, oda_ref, mxq_ref, g0_ref, g1_ref, g2_ref, mk_ref, mv_ref, qg_ref,
                bg_ref, wb_ref, wo_ref, out_ref, omx):
    for h in range(MX_HEADS):
        sl = slice(h * MX_DH, (h + 1) * MX_DH)
        qh = mxq_ref[:, sl].astype(F32)
        qh = qh * _rms(qh, MX_DH) * qg_ref[...] * (MX_DH ** -0.5 * LOG2E)
        s = _dot_nt(qh.astype(BF16), mk_ref[0, :, sl])
        p = jnp.exp2(s - jnp.max(s, axis=-1, keepdims=True))
        oh = _dot(p.astype(BF16), mv_ref[0, :, sl]) / jnp.sum(p, axis=-1, keepdims=True)
        omx[:, sl] = oh.astype(BF16)
    y = None
    for r, (o_r, g_r) in enumerate(((odn_ref, g0_ref), (oda_ref, g1_ref), (omx, g2_ref))):
        gate = _sigmoid(g_r[...].astype(F32) + bg_ref[r:r + 1, :])
        term = gate * _dot(o_r[...], wb_ref[r])
        y = term if y is None else y + term
    out_ref[...] = x_ref[...] + _dot(y.astype(BF16), wo_ref[...])


def _merge(x2, o_dn, o_da, p, mk, mv, q_norm, b_gate, w_branch, w_out, seq):
    t = x2.shape[0]
    tm = min(512, seq)
    nt = seq // tm
    bw = 512
    n_mem = mk.shape[1]
    return pl.pallas_call(
        _merge_body,
        grid=(t // tm,),
        in_specs=[
            pl.BlockSpec((tm, D_MODEL), lambda i: (i, 0)),
            pl.BlockSpec((tm, bw), lambda i: (i, 0)),
            pl.BlockSpec((tm, bw), lambda i: (i, 0)),
            pl.BlockSpec((tm, bw), lambda i: (i, P_MXQ // bw)),
            pl.BlockSpec((tm, D_MODEL), lambda i: (i, P_GATE // D_MODEL)),
            pl.BlockSpec((tm, D_MODEL), lambda i: (i, P_GATE // D_MODEL + 1)),
            pl.BlockSpec((tm, D_MODEL), lambda i: (i, P_GATE // D_MODEL + 2)),
            pl.BlockSpec((1, n_mem, bw), lambda i: (i // nt, 0, 0)),
            pl.BlockSpec((1, n_mem, bw), lambda i: (i // nt, 0, 0)),
            pl.BlockSpec((1, MX_DH), lambda i: (0, 0)),
            pl.BlockSpec((3, D_MODEL), lambda i: (0, 0)),
            pl.BlockSpec((3, bw, D_MODEL), lambda i: (0, 0, 0)),
            pl.BlockSpec((D_MODEL, D_MODEL), lambda i: (0, 0)),
        ],
        out_specs=pl.BlockSpec((tm, D_MODEL), lambda i: (i, 0)),
        out_shape=jax.ShapeDtypeStruct((t, D_MODEL), F32),
        scratch_shapes=[pltpu.VMEM((tm, bw), BF16)],
        compiler_params=pltpu.CompilerParams(dimension_semantics=("parallel",)),
        name="merge",
    )(x2, o_dn, o_da, p, p, p, p, mk, mv, q_norm, b_gate, w_branch, w_out)


def _router_body(x_ref, g_ref, wr_ref, br_ref, h_ref, idx_ref, wt_ref, rank_ref, cnt_ref):
    tm = x_ref.shape[0]
    x = x_ref[...]
    h = x * _rms(x, D_MODEL) * g_ref[...]
    h_ref[...] = h.astype(BF16)
    logits = _dot3(h, wr_ref[...]) + br_ref[...]
    lane = lax.broadcasted_iota(I32, (tm, LANES), 1)
    lane_f = lane.astype(F32)
    work = logits
    sel = jnp.zeros((tm, LANES), F32)
    vals, idxs = [], []
    for _ in range(TOP_K):
        mx = jnp.max(work, axis=-1, keepdims=True)
        ik = jnp.min(jnp.where(work == mx, lane_f, float(LANES)), axis=-1, keepdims=True)
        hit = lane_f == ik
        sel = jnp.where(hit, 1.0, sel)
        work = jnp.where(hit, -jnp.inf, work)
        vals.append(mx)
        idxs.append(ik)
    es = [jnp.exp(v - vals[0]) for v in vals]
    den = es[0] + es[1] + es[2] + es[3]
    sub = cnt_ref.shape[0]
    mt = tm // sub
    r = lax.broadcasted_iota(I32, (tm, tm), 0)
    c = lax.broadcasted_iota(I32, (tm, tm), 1)
    tril = jnp.where((r > c) & (r // mt == c // mt), 1.0, 0.0).astype(BF16)
    cum = _dot(tril, sel.astype(BF16))
    idx_o = jnp.zeros((tm, LANES), F32)
    wt_o = jnp.zeros((tm, LANES), F32)
    rank_o = jnp.zeros((tm, LANES), F32)
    for k in range(TOP_K):
        rk = jnp.sum(jnp.where(lane_f == idxs[k], cum, 0.0), axis=-1, keepdims=True)
        idx_o = jnp.where(lane == k, idxs[k], idx_o)
        wt_o = jnp.where(lane == k, es[k] / den, wt_o)
        rank_o = jnp.where(lane == k, rk, rank_o)
    idx_ref[...] = idx_o.astype(I32)
    wt_ref[...] = wt_o
    rank_ref[...] = rank_o.astype(I32)
    for s in range(sub):
        cnt_ref[s] = jnp.broadcast_to(jnp.sum(sel[s * mt:(s + 1) * mt], axis=0, keepdims=True), (8, LANES))


def _router(x1, gain, w_r, b_r):
    t = x1.shape[0]
    mt = min(MOE_TILE, t)
    sub = 2 if t % (2 * mt) == 0 else 1
    tm = sub * mt
    row = lambda i: (i, 0)
    fixed = lambda i: (0, 0)
    return pl.pallas_call(
        _router_body,
        grid=(t // tm,),
        in_specs=[
            pl.BlockSpec((tm, D_MODEL), row),
            pl.BlockSpec((1, D_MODEL), fixed),
            pl.BlockSpec((D_MODEL, LANES), fixed),
            pl.BlockSpec((1, LANES), fixed),
        ],
        out_specs=[
            pl.BlockSpec((tm, D_MODEL), row),
            pl.BlockSpec((tm, LANES), row),
            pl.BlockSpec((tm, LANES), row),
            pl.BlockSpec((tm, LANES), row),
            pl.BlockSpec((sub, 8, LANES), lambda i: (i, 0, 0)),
        ],
        out_shape=[
            jax.ShapeDtypeStruct((t, D_MODEL), BF16),
            jax.ShapeDtypeStruct((t, LANES), I32),
            jax.ShapeDtypeStruct((t, LANES), F32),
            jax.ShapeDtypeStruct((t, LANES), I32),
            jax.ShapeDtypeStruct((t // mt, 8, LANES), F32),
        ],
        compiler_params=pltpu.CompilerParams(dimension_semantics=("arbitrary",)),
        name="router",
    )(x1, gain, w_r, b_r)


def _lane_cumsum(x):
    lane = lax.broadcasted_iota(I32, x.shape, 1)
    s = 1
    while s < N_EXPERTS:
        x = x + jnp.where(lane >= s, pltpu.roll(x, s, axis=1), 0.0)
        s *= 2
    return x


def _plan_body(cnt_ref, gs_ref, ls_ref, n8_ref, lsf_ref, tail_ref, meta_ref):
    nt = cnt_ref.shape[0]
    ga = float(GROUP_ALIGN)
    eb = float(EXPERT_BLOCK)
    lane = lax.broadcasted_iota(I32, (nt, LANES), 1)
    r8 = jnp.where(lane < N_EXPERTS, jnp.floor((cnt_ref[...] + (ga - 1.0)) * (1.0 / ga)) * ga, 0.0)
    ri = lax.broadcasted_iota(I32, (nt, nt), 0)
    ci = lax.broadcasted_iota(I32, (nt, nt), 1)
    before = _dot(jnp.where(ri > ci, 1.0, 0.0).astype(BF16), r8.astype(BF16))
    tot = jnp.sum(r8, axis=0, keepdims=True)
    region = jnp.floor((tot + (eb - 1.0)) * (1.0 / eb)) * eb
    pends = _lane_cumsum(jnp.broadcast_to(region, (8, LANES)))[0:1, :]
    pstart = pends - region
    lstart = _lane_cumsum(r8) - r8
    gs_ref[...] = (pstart + before).astype(I32)
    ls_ref[...] = lstart.astype(I32)
    pieces = r8 * (1.0 / ga)
    n8_ref[...] = jnp.where(lane == TOTAL_LANE, jnp.sum(pieces, axis=1, keepdims=True), pieces).astype(I32)
    lsf_ref[...] = lstart
    row8 = lax.broadcasted_iota(I32, (8, LANES), 0)
    tail_n = (region - tot) * (1.0 / ga)
    lane8 = lax.broadcasted_iota(I32, (8, LANES), 1)
    tail_n = jnp.where(lane8 == TOTAL_LANE, jnp.sum(tail_n, axis=1, keepdims=True), tail_n)
    tail = jnp.where(row8 == 0, pstart + tot, jnp.where(row8 == 1, tail_n, 0.0))
    tail_ref[...] = tail.astype(I32)
    nb = meta_ref.shape[0]
    ln = lax.broadcasted_iota(I32, (nb, LANES), 1)
    blk = lax.broadcasted_iota(I32, (nb, LANES), 0).astype(F32) * eb
    be = jnp.sum(jnp.where((ln < N_EXPERTS) & (pends <= blk), 1.0, 0.0), axis=-1, keepdims=True)
    be = jnp.minimum(be, float(N_EXPERTS - 1))
    used = jnp.sum(jnp.where(ln == N_EXPERTS - 1, pends, 0.0), axis=-1, keepdims=True) * (1.0 / eb)
    meta_ref[...] = jnp.where(ln == 0, be, jnp.where(ln == 1, used, 0.0)).astype(I32)


def _plan(cnt, n_blocks_pad):
    nt = cnt.shape[0]
    shp = jax.ShapeDtypeStruct((nt, LANES), I32)
    return pl.pallas_call(
        _plan_body,
        out_shape=[shp, shp, shp, jax.ShapeDtypeStruct((nt, LANES), F32),
                   jax.ShapeDtypeStruct((8, LANES), I32),
                   jax.ShapeDtypeStruct((n_blocks_pad, LANES), I32)],
        name="dispatch_plan",
    )(cnt)


def _local_positions(idx_ref, rank_ref, lsf_ref):
    tm = idx_ref.shape[0]
    lane = lax.broadcasted_iota(I32, (tm, LANES), 1)
    idx = idx_ref[...]
    rank = rank_ref[...].astype(F32)
    ls_row = lsf_ref[0]
    out = []
    for k in range(TOP_K):
        base = jnp.sum(jnp.where(lane == idx[:, k:k + 1], ls_row, 0.0), axis=-1, keepdims=True)
        out.append((base + rank[:, k:k + 1]).astype(I32))
    return out


def _local_positions_lanes(idx_ref, rank_ref, lsf_ref):
    tm = idx_ref.shape[0]
    idx_t = idx_ref[...].astype(F32).T
    rank_t = rank_ref[...].astype(F32).T
    ls_col = jnp.broadcast_to(lsf_ref[0], (LANES, LANES)).T[:, 0:1]
    expert = lax.broadcasted_iota(I32, (LANES, tm), 0).astype(F32)
    out = []
    for k in range(TOP_K):
        base = jnp.sum(jnp.where(expert == idx_t[k:k + 1, :], ls_col, 0.0), axis=0, keepdims=True)
        out.append((base + rank_t[k:k + 1, :]).astype(I32))
    return out


PACK_W = D_MODEL // 2
U32 = jnp.uint32


def _pack_rows(x):
    xb = x.astype(BF16).astype(F32)
    hi = lax.bitcast_convert_type(xb[:, :PACK_W], U32)
    lo = lax.bitcast_convert_type(xb[:, PACK_W:], U32)
    return hi | (lo >> 16)


def _unpack_rows(w):
    hi = lax.bitcast_convert_type(w & jnp.uint32(0xFFFF0000), F32)
    lo = lax.bitcast_convert_type(w << 16, F32)
    return hi.astype(BF16), lo.astype(BF16)


TOTAL_LANE = LANES - 1
WAIT_CHUNK = 128
BIG_PIECE = 4


def _start_group_copies(tables, tile, make_copy):
    gs_ref, ls_ref, n8_ref = tables

    def group(e, carry):
        g0 = gs_ref[tile, e]
        l0 = ls_ref[tile, e]
        n = n8_ref[tile, e]
        nbig = n // BIG_PIECE

        def big(j, c):
            off = j * (BIG_PIECE * GROUP_ALIGN)
            make_copy(pl.multiple_of(l0 + off, GROUP_ALIGN), pl.multiple_of(g0 + off, GROUP_ALIGN),
                      BIG_PIECE * GROUP_ALIGN).start()
            return c

        def one(j, c):
            off = j * GROUP_ALIGN
            make_copy(pl.multiple_of(l0 + off, GROUP_ALIGN), pl.multiple_of(g0 + off, GROUP_ALIGN),
                      GROUP_ALIGN).start()
            return c

        lax.fori_loop(0, nbig, big, 0)
        lax.fori_loop(nbig * BIG_PIECE, n, one, 0)
        return carry

    lax.fori_loop(0, N_EXPERTS, group, 0)


def _wait_pieces(n, make_wait):
    def chunk(j, c):
        make_wait(WAIT_CHUNK).wait()
        return c

    lax.fori_loop(0, n // WAIT_CHUNK, chunk, 0)
    b = WAIT_CHUNK // 2
    while b >= 1:
        def _(b=b):
            make_wait(b).wait()
        pl.when((n & b) != 0)(_)
        b //= 2


def _dispatch_body(gs_ref, ls_ref, n8_ref, tail_ref, h_ref, idx_ref, rank_ref, lsf_ref, xs_ref,
                   xl, zbuf, sem):
    i = pl.program_id(0)
    last = i == pl.num_programs(0) - 1
    slot = i % 2
    tables = (gs_ref, ls_ref, n8_ref)
    tm = h_ref.shape[0]
    lrows = xl.shape[1]
    pos = lax.broadcasted_iota(I32, (lrows, tm), 0)
    lp = _local_positions_lanes(idx_ref, rank_ref, lsf_ref)
    hit = pos == lp[0]
    for k in range(1, TOP_K):
        hit = hit | (pos == lp[k])
    xl[slot] = _pack_rows(_dot(jnp.where(hit, 1.0, 0.0).astype(BF16), h_ref[...]))

    def copy_from(s):
        def make_copy(l, g, rows):
            return pltpu.make_async_copy(xl.at[s, pl.ds(l, rows), :], xs_ref.at[pl.ds(g, rows), :], sem.at[s])
        return make_copy

    def wait_on(s):
        def make_wait(pieces):
            rows = pieces * GROUP_ALIGN
            return pltpu.make_async_copy(xl.at[0, pl.ds(0, rows), :], xs_ref.at[pl.ds(0, rows), :], sem.at[s])
        return make_wait

    @pl.when(i >= 1)
    def _():
        _wait_pieces(n8_ref[i - 1, TOTAL_LANE], wait_on(1 - slot))

    _start_group_copies(tables, i, copy_from(slot))

    def start_tails():
        def per_expert(e, carry):
            g0 = tail_ref[0, e]

            def one(j, c):
                pltpu.make_async_copy(
                    zbuf, xs_ref.at[pl.ds(pl.multiple_of(g0 + j * GROUP_ALIGN, GROUP_ALIGN), GROUP_ALIGN), :],
                    sem.at[2]).start()
                return c

            lax.fori_loop(0, tail_ref[1, e], one, 0)
            return carry

        lax.fori_loop(0, N_EXPERTS, per_expert, 0)

    @pl.when(last)
    def _():
        zbuf[...] = jnp.zeros_like(zbuf)
        start_tails()
        _wait_pieces(n8_ref[i, TOTAL_LANE], wait_on(slot))
        _wait_pieces(tail_ref[1, TOTAL_LANE], wait_on(2))


def _dispatch(gs, ls, n8, tail, h2, idx, rank, lsf, n_slots):
    t = h2.shape[0]
    tm = min(MOE_TILE, t)
    nt = t // tm
    lrows = tm * TOP_K + N_EXPERTS * GROUP_ALIGN
    row = lambda i, *_: (i, 0)
    grid_spec = pltpu.PrefetchScalarGridSpec(
        num_scalar_prefetch=4,
        grid=(nt,),
        in_specs=[
            pl.BlockSpec((tm, D_MODEL), row),
            pl.BlockSpec((tm, LANES), row),
            pl.BlockSpec((tm, LANES), row),
            pl.BlockSpec((1, 1, LANES), lambda i, *_: (i, 0, 0)),
        ],
        out_specs=pl.BlockSpec(memory_space=pl.ANY),
        scratch_shapes=[pltpu.VMEM((2, lrows, PACK_W), U32), pltpu.VMEM((GROUP_ALIGN, PACK_W), U32),
                        pltpu.SemaphoreType.DMA((3,))],
    )
    return pl.pallas_call(
        _dispatch_body,
        grid_spec=grid_spec,
        out_shape=jax.ShapeDtypeStruct((n_slots, PACK_W), U32),
        compiler_params=pltpu.CompilerParams(dimension_semantics=("arbitrary",)),
        name="moe_dispatch",
    )(gs, ls, n8, tail, h2, idx, rank, lsf)


def _expert_body(be_ref, nu_ref, x_ref, wgu_ref, bgu_ref, wd_ref, bd_ref, y_ref, wgu_b, wd_b):
    j = pl.program_id(0)

    @pl.when(j < nu_ref[0])
    def _():
        @pl.when(jnp.logical_or(j == 0, be_ref[j] != be_ref[jnp.maximum(j - 1, 0)]))
        def _():
            wgu_b[...] = wgu_ref[0].astype(BF16)
            wd_b[...] = wd_ref[0].astype(BF16)

        gu = _dot(jnp.concatenate(_unpack_rows(x_ref[...]), axis=1), wgu_b[...]) + bgu_ref[0]
        gate = jnp.minimum(gu[:, :D_FF], SWIGLU_LIMIT)
        up = jnp.clip(gu[:, D_FF:], -SWIGLU_LIMIT, SWIGLU_LIMIT)
        act = (up + 1.0) * gate * _sigmoid(SWIGLU_ALPHA * gate)
        y_ref[...] = _pack_rows(_dot(act.astype(BF16), wd_b[...]) + bd_ref[0])


def _experts(block_e, n_used, xs, wgu, bgu, wd, bd):
    n_slots = xs.shape[0]
    nb = n_slots // EXPERT_BLOCK

    def blk(j, be, nu):
        return jnp.minimum(j, nu[0] - 1)

    grid_spec = pltpu.PrefetchScalarGridSpec(
        num_scalar_prefetch=2,
        grid=(nb,),
        in_specs=[
            pl.BlockSpec((EXPERT_BLOCK, PACK_W), lambda j, be, nu: (blk(j, be, nu), 0)),
            pl.BlockSpec((1, D_MODEL, 2 * D_FF), lambda j, be, nu: (be[blk(j, be, nu)], 0, 0)),
            pl.BlockSpec((1, 1, 2 * D_FF), lambda j, be, nu: (be[blk(j, be, nu)], 0, 0)),
            pl.BlockSpec((1, D_FF, D_MODEL), lambda j, be, nu: (be[blk(j, be, nu)], 0, 0)),
            pl.BlockSpec((1, 1, D_MODEL), lambda j, be, nu: (be[blk(j, be, nu)], 0, 0)),
        ],
        out_specs=pl.BlockSpec((EXPERT_BLOCK, PACK_W), lambda j, be, nu: (blk(j, be, nu), 0)),
        scratch_shapes=[pltpu.VMEM((D_MODEL, 2 * D_FF), BF16), pltpu.VMEM((D_FF, D_MODEL), BF16)],
    )
    return pl.pallas_call(
        _expert_body,
        grid_spec=grid_spec,
        out_shape=jax.ShapeDtypeStruct((n_slots, PACK_W), U32),
        compiler_params=pltpu.CompilerParams(dimension_semantics=("arbitrary",)),
        name="moe_experts",
    )(block_e, n_used, xs, wgu, bgu, wd, bd)


def _combine_body(gs_ref, ls_ref, n8_ref, x_ref, wt_ref, idx_ref, rank_ref, lsf_ref, y_ref, out_ref,
                  yl, sem):
    i = pl.program_id(0)
    slot = i % 2
    tables = (gs_ref, ls_ref, n8_ref)
    tm = x_ref.shape[0]
    lrows = yl.shape[1]

    def copy_into(s):
        def make_copy(l, g, rows):
            return pltpu.make_async_copy(y_ref.at[pl.ds(g, rows), :], yl.at[s, pl.ds(l, rows), :], sem.at[s])
        return make_copy

    def make_wait(pieces):
        rows = pieces * GROUP_ALIGN
        return pltpu.make_async_copy(y_ref.at[pl.ds(0, rows), :], yl.at[0, pl.ds(0, rows), :], sem.at[slot])

    @pl.when(i == 0)
    def _():
        yl[...] = jnp.zeros_like(yl)
        _start_group_copies(tables, i, copy_into(slot))

    @pl.when(i + 1 < pl.num_programs(0))
    def _():
        _start_group_copies(tables, i + 1, copy_into(1 - slot))

    pos = lax.broadcasted_iota(I32, (tm, lrows), 1)
    lp = _local_positions(idx_ref, rank_ref, lsf_ref)
    wt = wt_ref[...]
    wm = jnp.zeros((tm, lrows), F32)
    for k in range(TOP_K):
        wm = jnp.where(pos == lp[k], wt[:, k:k + 1], wm)
    _wait_pieces(n8_ref[i, TOTAL_LANE], make_wait)
    wmb = wm.astype(BF16)
    y_hi, y_lo = _unpack_rows(yl[slot])
    out_ref[:, :PACK_W] = x_ref[:, :PACK_W] + _dot(wmb, y_hi)
    out_ref[:, PACK_W:] = x_ref[:, PACK_W:] + _dot(wmb, y_lo)


def _combine(gs, ls, n8, x1, wts, idx, rank, lsf, y):
    t = x1.shape[0]
    tm = min(MOE_TILE, t)
    nt = t // tm
    lrows = tm * TOP_K + N_EXPERTS * GROUP_ALIGN
    row = lambda i, *_: (i, 0)
    grid_spec = pltpu.PrefetchScalarGridSpec(
        num_scalar_prefetch=3,
        grid=(nt,),
        in_specs=[
            pl.BlockSpec((tm, D_MODEL), row),
            pl.BlockSpec((tm, LANES), row),
            pl.BlockSpec((tm, LANES), row),
            pl.BlockSpec((tm, LANES), row),
            pl.BlockSpec((1, 1, LANES), lambda i, *_: (i, 0, 0)),
            pl.BlockSpec(memory_space=pl.ANY),
        ],
        out_specs=pl.BlockSpec((tm, D_MODEL), row),
        scratch_shapes=[pltpu.VMEM((2, lrows, PACK_W), U32), pltpu.SemaphoreType.DMA((2,))],
    )
    return pl.pallas_call(
        _combine_body,
        grid_spec=grid_spec,
        out_shape=jax.ShapeDtypeStruct((t, D_MODEL), F32),
        compiler_params=pltpu.CompilerParams(dimension_semantics=("arbitrary",)),
        name="moe_combine",
    )(gs, ls, n8, x1, wts, idx, rank, lsf, y)


def _pad_lanes(v, fill=0.0):
    v = v.astype(F32).reshape(1, -1)
    return jnp.pad(v, ((0, 0), (0, LANES - v.shape[1])), constant_values=fill)


def _mixer(x2, mem, rel_table, attn_norm, w_in, b_gate, dn_conv, dn_a_log, dn_dt_bias, dn_out_norm,
           da_q_norm, da_k_norm, da_lambda, da_subln, mem_norm, w_mem_kv, mx_q_norm, mx_k_norm,
           w_branch, w_out, batch, seq):
    wp = jnp.concatenate([w_in[:, :W_AB_LO], w_in[:, W_AB_HI:]], axis=1).astype(BF16)
    wab = jnp.pad(w_in[:, W_AB_LO:W_AB_HI], ((0, 0), (0, LANES - (W_AB_HI - W_AB_LO))))
    p, ab = _inproj(x2, attn_norm.reshape(1, -1), wp, wab)

    o_dn = _deltanet(p, ab, dn_conv, _pad_lanes(dn_a_log), _pad_lanes(dn_dt_bias),
                     dn_out_norm.reshape(1, -1), batch, seq)

    tq = min(ATT_BLOCK, seq)
    bias = _bias_tiles(rel_table.T, tq)
    o_da = _attention(p, bias, jnp.tile(da_q_norm, 2).reshape(1, -1), jnp.tile(da_k_norm, 2).reshape(1, -1),
                      da_lambda, da_subln.reshape(-1, 1), batch, seq)

    mk, mv = _memkv(mem, mem_norm.reshape(1, -1), w_mem_kv.astype(BF16), mx_k_norm.reshape(1, -1))
    return _merge(x2, o_dn, o_da, p, mk, mv, mx_q_norm.reshape(1, -1), b_gate.reshape(3, D_MODEL),
                  w_branch.astype(BF16), w_out.astype(BF16), seq)


def _moe(x1, ffn_norm, w_router, b_router, w_gate_up, b_gate_up, w_down, b_down):
    t = x1.shape[0]
    nt = t // min(MOE_TILE, t)
    max_rows = t * TOP_K + nt * N_EXPERTS * (GROUP_ALIGN - 1)
    n_blocks = -(-max_rows // EXPERT_BLOCK) + N_EXPERTS
    n_blocks_pad = -(-n_blocks // 8) * 8
    n_slots = n_blocks * EXPERT_BLOCK

    wr = jnp.pad(w_router, ((0, 0), (0, LANES - N_EXPERTS)))
    h2, idx, wts, rank, cnt = _router(x1, ffn_norm.reshape(1, -1), wr, _pad_lanes(b_router, NEG))
    gs, ls, n8, lsf, tail, meta = _plan(cnt[:, 0, :], n_blocks_pad)
    block_e = meta[:n_blocks, 0]
    n_used = meta[0:1, 1]
    lsf = lsf.reshape(nt, 1, LANES)

    xs = _dispatch(gs, ls, n8, tail, h2, idx, rank, lsf, n_slots)
    y = _experts(block_e, n_used, xs, w_gate_up, b_gate_up.reshape(N_EXPERTS, 1, -1),
                 w_down, b_down.reshape(N_EXPERTS, 1, -1))
    return _combine(gs, ls, n8, x1, wts, idx, rank, lsf, y)


def kernel(x, mem, rel_table, attn_norm, w_in, b_gate, dn_conv, dn_a_log, dn_dt_bias, dn_out_norm,
           da_q_norm, da_k_norm, da_lambda, da_subln, mem_norm, w_mem_kv, mx_q_norm, mx_k_norm,
           w_branch, w_out, ffn_norm, w_router, b_router, w_gate_up, b_gate_up, w_down, b_down):
    batch, seq, d = x.shape
    x2 = x.reshape(batch * seq, d)
    x1 = _mixer(x2, mem, rel_table, attn_norm[0], w_in[0], b_gate[0], dn_conv[0], dn_a_log[0],
                dn_dt_bias[0], dn_out_norm[0], da_q_norm[0], da_k_norm[0], da_lambda[0], da_subln[0],
                mem_norm[0], w_mem_kv[0], mx_q_norm[0], mx_k_norm[0], w_branch[0], w_out[0], batch, seq)
    out = _moe(x1, ffn_norm[0], w_router[0], b_router[0], w_gate_up[0], b_gate_up[0], w_down[0],
               b_down[0])
    return out.reshape(batch, seq, d)
```

```python
import functools
import math

import jax
import jax.numpy as jnp
from jax import lax
from jax.experimental import pallas as pl
from jax.experimental.pallas import tpu as pltpu

F32 = jnp.float32
BF16 = jnp.bfloat16
I32 = jnp.int32

D_MODEL = 1024
EPS = 1e-6
LANES = 128

DN_HEADS = 4
DN_DK = 128
DN_CHUNK = 64
DN_CONV = 4

DA_HEADS = 4
DA_DH = 64

MX_HEADS = 4
MX_DH = 128

REL_BUCKETS = 32
REL_MAX_DIST = 128

N_EXPERTS = 32
TOP_K = 4
D_FF = 1024
SWIGLU_LIMIT = 7.0
SWIGLU_ALPHA = 1.702
EXPERT_BLOCK = 512
MOE_TILE = 256
GROUP_ALIGN = 8

LAM_INIT = 0.8 - 0.6 * math.exp(-0.3 * 0)
LOG2E = 1.4426950408889634
NEG = -1e30

P_DNQ, P_DNK, P_DNV, P_DNZ = 0, 512, 1024, 1536
P_DAQ, P_DAK, P_DAV = 2048, 2560, 3072
P_MXQ = 3584
P_GATE = 4096
P_COLS = 7168
W_AB_LO, W_AB_HI = 2048, 2056


def _dot(a, b):
    return jnp.dot(a, b, preferred_element_type=F32)


def _dot_nt(a, b):
    return lax.dot_general(a, b, (((1,), (1,)), ((), ())), preferred_element_type=F32)


def _dot_tn(a, b):
    return lax.dot_general(a, b, (((0,), (0,)), ((), ())), preferred_element_type=F32)


def _split(x):
    hi = x.astype(BF16)
    lo = (x - hi.astype(F32)).astype(BF16)
    return hi, lo


def _dot3(a, b):
    ah, al = _split(a)
    bh, bl = _split(b)
    return _dot(ah, bh) + _dot(ah, bl) + _dot(al, bh)


def _sigmoid(x):
    return 1.0 / (1.0 + jnp.exp(-x))


def _rms(x, n):
    return lax.rsqrt(jnp.sum(x * x, axis=-1, keepdims=True) * (1.0 / n) + EPS)


def _inproj_body(x_ref, g_ref, w_ref, wab_ref, p_ref, ab_ref, h_scr):
    @pl.when(pl.program_id(1) == 0)
    def _():
        x = x_ref[...]
        h = x * _rms(x, D_MODEL) * g_ref[...]
        h_scr[...] = h.astype(BF16)
        ab_ref[...] = _dot3(h, wab_ref[...])

    p_ref[...] = _dot(h_scr[...], w_ref[...]).astype(p_ref.dtype)


def _inproj(x2, gain, wp, wab):
    t = x2.shape[0]
    tm = min(1024, t)
    tn = 1024
    return pl.pallas_call(
        _inproj_body,
        grid=(t // tm, P_COLS // tn),
        in_specs=[
            pl.BlockSpec((tm, D_MODEL), lambda i, j: (i, 0)),
            pl.BlockSpec((1, D_MODEL), lambda i, j: (0, 0)),
            pl.BlockSpec((D_MODEL, tn), lambda i, j: (0, j)),
            pl.BlockSpec((D_MODEL, LANES), lambda i, j: (0, 0)),
        ],
        out_specs=[
            pl.BlockSpec((tm, tn), lambda i, j: (i, j)),
            pl.BlockSpec((tm, LANES), lambda i, j: (i, 0)),
        ],
        out_shape=[
            jax.ShapeDtypeStruct((t, P_COLS), BF16),
            jax.ShapeDtypeStruct((t, LANES), F32),
        ],
        scratch_shapes=[pltpu.VMEM((tm, D_MODEL), BF16)],
        compiler_params=pltpu.CompilerParams(dimension_semantics=("parallel", "arbitrary")),
        name="inproj",
    )(x2, gain, wp, wab)


DN_HALO = 16
DN_SCAN_CHUNK = 256


def _deltanet_body(q_ref, k_ref, v_ref, z_ref, qh_ref, kh_ref, vh_ref, ab_ref, cw_ref, alog_ref,
                   dtb_ref, on_ref, o_ref, stage, qs, ks, vs, s_scr):
    i = pl.program_id(1)
    tc = q_ref.shape[0]
    hw = DN_HEADS * DN_DK

    @pl.when(i == 0)
    def _():
        s_scr[...] = jnp.zeros_like(s_scr)

    for src, halo, dst, off, kind in ((q_ref, qh_ref, qs, 0, "q"), (k_ref, kh_ref, ks, hw, "k"),
                                      (v_ref, vh_ref, vs, 2 * hw, "v")):
        hal = halo[...].astype(F32)
        stage[0:DN_HALO, :] = jnp.where(i == 0, 0.0, hal)
        stage[DN_HALO:DN_HALO + tc, :] = src[...].astype(F32)
        base = DN_HALO - (DN_CONV - 1)
        y = stage[base:base + tc, :] * cw_ref[0:1, off:off + hw]
        for j in range(1, DN_CONV):
            y = y + stage[base + j:base + j + tc, :] * cw_ref[j:j + 1, off:off + hw]
        y = y * _sigmoid(y)
        if kind == "v":
            dst[...] = y
        else:
            for h in range(DN_HEADS):
                sl = slice(h * DN_DK, (h + 1) * DN_DK)
                yh = y[:, sl]
                r = lax.rsqrt(jnp.sum(yh * yh, axis=-1, keepdims=True) + EPS)
                if kind == "q":
                    r = r * (DN_DK ** -0.5)
                dst[:, sl] = yh * r

    c = min(DN_SCAN_CHUNK, tc)
    row = lax.broadcasted_iota(I32, (c, c), 0)
    col = lax.broadcasted_iota(I32, (c, c), 1)
    incl = row >= col
    strict = row > col
    same_blk = (row // DN_CHUNK) == (col // DN_CHUNK)
    tri = jnp.where(incl, 1.0, 0.0).astype(BF16)
    eye = jnp.where(row == col, 1.0, 0.0)
    neg_a = -jnp.exp(alog_ref[...])
    dtb = dtb_ref[...]

    def chunk(ci, carry):
        r0 = pl.multiple_of(ci * c, c)
        abc = ab_ref[pl.ds(r0, c), :]
        a_in = abc + dtb
        g_all = neg_a * (jnp.maximum(a_in, 0.0) + jnp.log(1.0 + jnp.exp(-jnp.abs(a_in))))
        beta_all = _sigmoid(abc)
        zc = z_ref[pl.ds(r0, c), :].astype(F32)
        hs = range(DN_HEADS)
        sls = [slice(h * DN_DK, (h + 1) * DN_DK) for h in hs]
        q = [qs[pl.ds(r0, c), sl] for sl in sls]
        k = [ks[pl.ds(r0, c), sl] for sl in sls]
        v = [vs[pl.ds(r0, c), sl] for sl in sls]
        g = [g_all[:, h:h + 1] for h in hs]
        beta = [beta_all[:, DN_HEADS + h:DN_HEADS + h + 1] for h in hs]
        g_split = [_split(jnp.where(strict, g[h], 0.0)) for h in hs]
        diff = [_dot(tri, g_split[h][0]) + _dot(tri, g_split[h][1]) for h in hs]
        kb = [k[h].astype(BF16) for h in hs]
        qkk = [_dot_nt(jnp.concatenate([q[h].astype(BF16), kb[h]], axis=0), kb[h]) for h in hs]
        gc = [diff[h][:, 0:1] + g[h][0:1, :] for h in hs]
        decay = [jnp.where(incl, jnp.exp(diff[h]), 0.0) for h in hs]
        lower = [jnp.where(strict, qkk[h][c:] * decay[h] * beta[h], 0.0) for h in hs]
        pw = [jnp.where(same_blk, -lower[h], 0.0) for h in hs]
        dinv = [eye + pw[h] for h in hs]
        pwb = [pw[h].astype(BF16) for h in hs]
        for _ in range(int(math.log2(DN_CHUNK)) - 1):
            pwb = [_dot(pwb[h], pwb[h]).astype(BF16) for h in hs]
            dinv = [dinv[h] + _dot(dinv[h].astype(BF16), pwb[h]) for h in hs]
        dinv_b = [dinv[h].astype(BF16) for h in hs]
        pw = [-_dot(dinv_b[h], jnp.where(same_blk, 0.0, lower[h]).astype(BF16)) for h in hs]
        xm = [eye + pw[h] for h in hs]
        pwb = [pw[h].astype(BF16) for h in hs]
        for _ in range(int(math.log2(c // DN_CHUNK)) - 1):
            pwb = [_dot(pwb[h], pwb[h]).astype(BF16) for h in hs]
            xm = [xm[h] + _dot(xm[h].astype(BF16), pwb[h]) for h in hs]
        inv = [_dot(xm[h].astype(BF16), dinv_b[h]).astype(BF16) for h in hs]
        egc = [jnp.exp(gc[h]) for h in hs]
        rhs = [jnp.concatenate([v[h] * beta[h], k[h] * (beta[h] * egc[h])], axis=1).astype(BF16) for h in hs]
        sol = [_dot(inv[h], rhs[h]) for h in hs]
        qkm = [jnp.where(incl, qkk[h][:c] * decay[h], 0.0).astype(BF16) for h in hs]
        gl = [gc[h][c - 1:c, :] for h in hs]
        state = [s_scr[h] for h in hs]
        ws = [_dot(jnp.concatenate([sol[h][:, DN_DK:].astype(BF16), (q[h] * egc[h]).astype(BF16)], axis=0),
                   state[h].astype(BF16)) for h in hs]
        v_new = [sol[h][:, :DN_DK] - ws[h][:c] for h in hs]
        o = [ws[h][c:] + _dot(qkm[h], v_new[h].astype(BF16)) for h in hs]
        for h in hs:
            s_scr[h] = state[h] * jnp.exp(gl[h]) + _dot_tn(kb[h], (v_new[h] * jnp.exp(gl[h] - gc[h])).astype(BF16))
        for h in hs:
            zz = zc[:, sls[h]]
            on = o[h] * _rms(o[h], DN_DK) * on_ref[...]
            o_ref[pl.ds(r0, c), sls[h]] = (on * (zz * _sigmoid(zz))).astype(o_ref.dtype)
        return carry

    lax.fori_loop(0, tc // c, chunk, 0, unroll=True)


def _deltanet(p, ab, conv_w, alog_row, dtb_row, out_norm, batch, seq):
    t = batch * seq
    tc = min(512, seq)
    nt = seq // tc
    hw = DN_HEADS * DN_DK

    def main(cb):
        return pl.BlockSpec((tc, hw), lambda b, i: (b * nt + i, cb))

    def halo(cb):
        return pl.BlockSpec(
            (DN_HALO, hw),
            lambda b, i: (jnp.maximum((b * seq + i * tc) // DN_HALO - 1, 0), cb))

    return pl.pallas_call(
        _deltanet_body,
        grid=(batch, nt),
        in_specs=[
            main(P_DNQ // hw), main(P_DNK // hw), main(P_DNV // hw), main(P_DNZ // hw),
            halo(P_DNQ // hw), halo(P_DNK // hw), halo(P_DNV // hw),
            pl.BlockSpec((tc, LANES), lambda b, i: (b * nt + i, 0)),
            pl.BlockSpec((DN_CONV, 3 * hw), lambda b, i: (0, 0)),
            pl.BlockSpec((1, LANES), lambda b, i: (0, 0)),
            pl.BlockSpec((1, LANES), lambda b, i: (0, 0)),
            pl.BlockSpec((1, DN_DK), lambda b, i: (0, 0)),
        ],
        out_specs=pl.BlockSpec((tc, hw), lambda b, i: (b * nt + i, 0)),
        out_shape=jax.ShapeDtypeStruct((t, hw), BF16),
        scratch_shapes=[
            pltpu.VMEM((DN_HALO + tc, hw), F32),
            pltpu.VMEM((tc, hw), F32),
            pltpu.VMEM((tc, hw), F32),
            pltpu.VMEM((tc, hw), F32),
            pltpu.VMEM((DN_HEADS, DN_DK, DN_DK), F32),
        ],
        compiler_params=pltpu.CompilerParams(dimension_semantics=("parallel", "arbitrary")),
        name="deltanet",
    )(p, p, p, p, p, p, p, ab, conv_w, alog_row, dtb_row, out_norm)


ATT_BLOCK = 512


def _bias_body(tbl_ref, o_ref):
    h = pl.program_id(0)
    tq = o_ref.shape[2]
    key = lax.broadcasted_iota(I32, (tq, tq), 0)
    qry = lax.broadcasted_iota(I32, (tq, tq), 1)
    max_exact = REL_BUCKETS // 2
    far = tbl_ref[h, REL_BUCKETS - 1]
    for d in range(2):
        n = qry - key + d * tq
        nn = jnp.maximum(n, 0)
        nf = jnp.maximum(nn, 1).astype(F32)
        large = max_exact + (jnp.log(nf / max_exact) / math.log(REL_MAX_DIST / max_exact)
                             * (REL_BUCKETS - max_exact)).astype(I32)
        large = jnp.minimum(large, REL_BUCKETS - 1)
        bucket = jnp.where(nn < max_exact, nn, large)
        val = jnp.zeros((tq, tq), F32)
        for b in range(REL_BUCKETS):
            val = jnp.where(bucket == b, tbl_ref[h, b], val)
        o_ref[0, d] = jnp.where(n >= 0, (val - far) * LOG2E, NEG)


def _bias_tiles(tbl_t, tq):
    return pl.pallas_call(
        _bias_body,
        grid=(DA_HEADS,),
        in_specs=[pl.BlockSpec(memory_space=pltpu.SMEM)],
        out_specs=pl.BlockSpec((1, 2, tq, tq), lambda h: (h, 0, 0, 0)),
        out_shape=jax.ShapeDtypeStruct((DA_HEADS, 2, tq, tq), F32),
        name="t5_bias_tiles",
    )(tbl_t)


DA_DV = 2 * DA_DH
DA_VROWS = DA_DV + 16


BOUND_SLACK = 1.02
MAX_SHIFT_GAP = 110.0


def _attn_body(q_ref, k_ref, v_ref, bias_ref, qg_ref, kg_ref, lam_ref, sg_ref, o_ref,
               kn, vt, kst, m_s, acc_s):
    qi = pl.program_id(2)
    tq = q_ref.shape[0]
    seq = k_ref.shape[0]
    tk = tq
    lo_mask = lax.broadcasted_iota(I32, (1, DA_DV), 1) < DA_DH

    def group_norm(x, gain):
        x2 = x * x
        lo = jnp.sum(jnp.where(lo_mask, x2, 0.0), axis=-1, keepdims=True)
        hi = jnp.sum(jnp.where(lo_mask, 0.0, x2), axis=-1, keepdims=True)
        r = jnp.where(lo_mask, lax.rsqrt(lo * (1.0 / DA_DH) + EPS), lax.rsqrt(hi * (1.0 / DA_DH) + EPS))
        return x * r * gain

    @pl.when(qi == 0)
    def _():
        ones = jnp.ones((DA_VROWS - DA_DV, tk), BF16)

        def body(c, kmax2):
            r0 = pl.multiple_of(c * tk, tk)
            kb = group_norm(k_ref[pl.ds(r0, tk), :].astype(F32), kg_ref[...]).astype(BF16)
            kn[pl.ds(r0, tk), :] = kb
            vt[c, 0:DA_DV, :] = v_ref[pl.ds(r0, tk), :].astype(F32).T.astype(BF16)
            vt[c, DA_DV:DA_VROWS, :] = ones
            k2 = kb.astype(F32)
            k2 = k2 * k2
            lo = jnp.max(jnp.sum(jnp.where(lo_mask, k2, 0.0), axis=-1, keepdims=True), axis=0, keepdims=True)
            hi = jnp.max(jnp.sum(jnp.where(lo_mask, 0.0, k2), axis=-1, keepdims=True), axis=0, keepdims=True)
            return jnp.maximum(kmax2, jnp.where(lo_mask, lo, hi))
        kst[0:1, :] = lax.fori_loop(0, seq // tk, body, jnp.zeros((1, DA_DV), F32))
        b0 = bias_ref[0, 0]
        b1 = bias_ref[0, 1]
        bmax = jnp.maximum(jnp.max(jnp.maximum(b0, b1), axis=0, keepdims=True), 0.0)
        bmin = jnp.minimum(jnp.min(jnp.minimum(jnp.where(b0 > 0.5 * NEG, b0, 0.0), b1), axis=0, keepdims=True), 0.0)
        kst[1:2, :] = jnp.broadcast_to(jnp.max(bmax, axis=1, keepdims=True), (1, DA_DV))
        kst[2:3, :] = jnp.broadcast_to(jnp.min(bmin, axis=1, keepdims=True), (1, DA_DV))

    q = group_norm(q_ref[...].astype(F32), qg_ref[...]) * (DA_DH ** -0.5 * LOG2E)
    qcat = jnp.concatenate([jnp.where(lo_mask, q, 0.0), jnp.where(lo_mask, 0.0, q)], axis=0).astype(BF16)
    acc_s[...] = jnp.zeros_like(acc_s)

    q2 = q * q * kst[0:1, :]
    ones8 = jnp.ones((8, DA_DV), BF16)
    bmax = kst[1:2, 0:1]
    bmin = kst[2:3, 0:1]
    bound = []
    for m in range(2):
        q2m = jnp.where(lo_mask, q2, 0.0) if m == 0 else jnp.where(lo_mask, 0.0, q2)
        bound.append(jnp.sqrt(_dot_nt(ones8, q2m.astype(BF16))[0:1, :]) * BOUND_SLACK)
    bound = jnp.concatenate(bound, axis=1)
    worst = jnp.max(2.0 * bound, axis=1, keepdims=True) + bmax - bmin
    safe = worst[0, 0] <= MAX_SHIFT_GAP

    def block(j, d):
        r0 = pl.multiple_of(j * tk, tk)
        st = _dot_nt(kn[pl.ds(r0, tk), :], qcat)
        if d is not None:
            bias = bias_ref[0, d]
            st = st + jnp.concatenate([bias, bias], axis=1)
        m_prev = m_s[...]
        m_new = jnp.maximum(m_prev, jnp.max(st, axis=0, keepdims=True))
        alpha = jnp.exp2(m_prev - m_new)
        acc_s[...] = alpha * acc_s[...] + _dot(vt[j], jnp.exp2(st - m_new).astype(BF16))
        m_s[...] = m_new

    def blocks_fixed(js, ds=None):
        sts = [_dot_nt(kn[pl.ds(pl.multiple_of(j * tk, tk), tk), :], qcat) for j in js]
        if ds is not None:
            biases = [bias_ref[0, d] for d in ds]
            sts = [st + jnp.concatenate([b, b], axis=1) for st, b in zip(sts, biases)]
        shift = m_s[...]
        pts = [jnp.exp2(st - shift).astype(BF16) for st in sts]
        tot = _dot(vt[js[0]], pts[0])
        for j, pt in zip(js[1:], pts[1:]):
            tot = tot + _dot(vt[j], pt)
        acc_s[...] = acc_s[...] + tot

    def run(fixed_shift):
        n_far = jnp.maximum(qi - 1, 0)

        if fixed_shift:
            def far_quad(jj, carry):
                blocks_fixed([4 * jj + u for u in range(4)])
                return carry

            lax.fori_loop(0, n_far // 4, far_quad, 0)
            rem = n_far % 4

            @pl.when(rem >= 2)
            def _():
                blocks_fixed([n_far - rem, n_far - rem + 1])

            @pl.when(rem % 2 == 1)
            def _():
                blocks_fixed([n_far - 1])

            @pl.when(qi >= 1)
            def _():
                blocks_fixed([qi - 1, qi], [1, 0])

            @pl.when(qi == 0)
            def _():
                blocks_fixed([qi], [0])
        else:
            def far_one(j, carry):
                block(j, None)
                return carry

            lax.fori_loop(0, n_far, far_one, 0)

            @pl.when(qi >= 1)
            def _():
                block(qi - 1, 1)

            block(qi, 0)

    @pl.when(safe)
    def _():
        m_s[...] = bound + bmax
        run(True)

    @pl.when(jnp.logical_not(safe))
    def _():
        m_s[...] = jnp.full(m_s.shape, NEG, F32)
        run(False)

    lam_p = lam_ref[...]
    lam = (jnp.exp(jnp.sum(lam_p[0:1, :] * lam_p[1:2, :], axis=-1, keepdims=True))
           - jnp.exp(jnp.sum(lam_p[2:3, :] * lam_p[3:4, :], axis=-1, keepdims=True)) + LAM_INIT)
    a0 = acc_s[:, 0:tq]
    a1 = acc_s[:, tq:2 * tq]
    ot = a0[0:DA_DV] / a0[DA_DV:DA_DV + 1] - lam * (a1[0:DA_DV] / a1[DA_DV:DA_DV + 1])
    r = lax.rsqrt(jnp.sum(ot * ot, axis=0, keepdims=True) * (1.0 / DA_DV) + EPS)
    ot = ot * r * (sg_ref[...] * (1.0 - LAM_INIT))
    o_ref[...] = ot.T.astype(o_ref.dtype)


def _attention(p, bias, qg, kg, lam_p, subln, batch, seq):
    t = batch * seq
    tq = min(ATT_BLOCK, seq)
    nq = seq // tq
    dv = DA_DV
    return pl.pallas_call(
        _attn_body,
        grid=(batch, DA_HEADS, nq),
        in_specs=[
            pl.BlockSpec((tq, dv), lambda b, h, i: (b * nq + i, P_DAQ // dv + h)),
            pl.BlockSpec((seq, dv), lambda b, h, i: (b, P_DAK // dv + h)),
            pl.BlockSpec((seq, dv), lambda b, h, i: (b, P_DAV // dv + h)),
            pl.BlockSpec((1, 2, tq, tq), lambda b, h, i: (h, 0, 0, 0)),
            pl.BlockSpec((1, dv), lambda b, h, i: (0, 0)),
            pl.BlockSpec((1, dv), lambda b, h, i: (0, 0)),
            pl.BlockSpec((4, DA_DH), lambda b, h, i: (0, 0)),
            pl.BlockSpec((dv, 1), lambda b, h, i: (0, 0)),
        ],
        out_specs=pl.BlockSpec((tq, dv), lambda b, h, i: (b * nq + i, h)),
        out_shape=jax.ShapeDtypeStruct((t, DA_HEADS * dv), BF16),
        scratch_shapes=[
            pltpu.VMEM((seq, dv), BF16),
            pltpu.VMEM((seq // tq, DA_VROWS, tq), BF16),
            pltpu.VMEM((8, dv), F32),
            pltpu.VMEM((1, 2 * tq), F32),
            pltpu.VMEM((DA_VROWS, 2 * tq), F32),
        ],
        compiler_params=pltpu.CompilerParams(dimension_semantics=("parallel", "parallel", "arbitrary")),
        name="diff_attention",
    )(p, p, p, bias, qg, kg, lam_p, subln)


def _memkv_body(mem_ref, mg_ref, w_ref, kg_ref, mk_ref, mv_ref):
    x = mem_ref[0]
    xn = x * _rms(x, D_MODEL) * mg_ref[...]
    kv = _dot(xn.astype(BF16), w_ref[...])
    hw = MX_HEADS * MX_DH
    for h in range(MX_HEADS):
        sl = slice(h * MX_DH, (h + 1) * MX_DH)
        kh = kv[:, sl]
        mk_ref[0, :, sl] = (kh * _rms(kh, MX_DH) * kg_ref[...]).astype(BF16)
    mv_ref[0] = kv[:, hw:].astype(BF16)


def _memkv(mem, mem_norm, w_kv, k_norm):
    b, n, _ = mem.shape
    hw = MX_HEADS * MX_DH
    return pl.pallas_call(
        _memkv_body,
        grid=(b,),
        in_specs=[
            pl.BlockSpec((1, n, D_MODEL), lambda i: (i, 0, 0)),
            pl.BlockSpec((1, D_MODEL), lambda i: (0, 0)),
            pl.BlockSpec((D_MODEL, 2 * hw), lambda i: (0, 0)),
            pl.BlockSpec((1, MX_DH), lambda i: (0, 0)),
        ],
        out_specs=[pl.BlockSpec((1, n, hw), lambda i: (i, 0, 0))] * 2,
        out_shape=[jax.ShapeDtypeStruct((b, n, hw), BF16)] * 2,
        name="memory_kv",
    )(mem, mem_norm, w_kv, k_norm)


def _merge_body(x_ref, odn_ref, oda_ref, mxq_ref, g0_ref, g1_ref, g2_ref, mk_ref, mv_ref, qg_ref,
                bg_ref, wb_ref, wo_ref, out_ref):
    def branch(r, o_r, g_r):
        return _sigmoid(g_r[...].astype(F32) + bg_ref[r:r + 1, :]) * _dot(o_r, wb_ref[r])

    y = branch(0, odn_ref[...], g0_ref) + branch(1, oda_ref[...], g1_ref)
    hs = range(MX_HEADS)
    sls = [slice(h * MX_DH, (h + 1) * MX_DH) for h in hs]
    qh = [mxq_ref[:, sl].astype(F32) for sl in sls]
    qh = [(q * _rms(q, MX_DH) * qg_ref[...] * (MX_DH ** -0.5 * LOG2E)).astype(BF16) for q in qh]
    s = [_dot_nt(qh[h], mk_ref[0, :, sls[h]]) for h in hs]
    p = [jnp.exp2(s[h] - jnp.max(s[h], axis=-1, keepdims=True)) for h in hs]
    den = [jnp.sum(p[h], axis=-1, keepdims=True) for h in hs]
    oh = [_dot(p[h].astype(BF16), mv_ref[0, :, sls[h]]) / den[h] for h in hs]
    y = y + branch(2, jnp.concatenate(oh, axis=1).astype(BF16), g2_ref)
    out_ref[...] = x_ref[...] + _dot(y.astype(BF16), wo_ref[...])


def _merge(x2, o_dn, o_da, p, mk, mv, q_norm, b_gate, w_branch, w_out, seq):
    t = x2.shape[0]
    tm = min(512, seq)
    nt = seq // tm
    bw = 512
    n_mem = mk.shape[1]
    return pl.pallas_call(
        _merge_body,
        grid=(t // tm,),
        in_specs=[
            pl.BlockSpec((tm, D_MODEL), lambda i: (i, 0)),
            pl.BlockSpec((tm, bw), lambda i: (i, 0)),
            pl.BlockSpec((tm, bw), lambda i: (i, 0)),
            pl.BlockSpec((tm, bw), lambda i: (i, P_MXQ // bw)),
            pl.BlockSpec((tm, D_MODEL), lambda i: (i, P_GATE // D_MODEL)),
            pl.BlockSpec((tm, D_MODEL), lambda i: (i, P_GATE // D_MODEL + 1)),
            pl.BlockSpec((tm, D_MODEL), lambda i: (i, P_GATE // D_MODEL + 2)),
            pl.BlockSpec((1, n_mem, bw), lambda i: (i // nt, 0, 0)),
            pl.BlockSpec((1, n_mem, bw), lambda i: (i // nt, 0, 0)),
            pl.BlockSpec((1, MX_DH), lambda i: (0, 0)),
            pl.BlockSpec((3, D_MODEL), lambda i: (0, 0)),
            pl.BlockSpec((3, bw, D_MODEL), lambda i: (0, 0, 0)),
            pl.BlockSpec((D_MODEL, D_MODEL), lambda i: (0, 0)),
        ],
        out_specs=pl.BlockSpec((tm, D_MODEL), lambda i: (i, 0)),
        out_shape=jax.ShapeDtypeStruct((t, D_MODEL), F32),
        compiler_params=pltpu.CompilerParams(dimension_semantics=("parallel",)),
        name="merge",
    )(x2, o_dn, o_da, p, p, p, p, mk, mv, q_norm, b_gate, w_branch, w_out)


def _router_body(x_ref, g_ref, wr_ref, br_ref, h_ref, idx_ref, wt_ref, rank_ref, cnt_ref):
    tm = x_ref.shape[0]
    x = x_ref[...]
    h = x * _rms(x, D_MODEL) * g_ref[...]
    h_ref[...] = h.astype(BF16)
    logits = _dot3(h, wr_ref[...]) + br_ref[...]
    lane = lax.broadcasted_iota(I32, (tm, LANES), 1)
    lane_f = lane.astype(F32)
    work = logits
    sel = jnp.zeros((tm, LANES), F32)
    vals, idxs = [], []
    for _ in range(TOP_K):
        mx = jnp.max(work, axis=-1, keepdims=True)
        ik = jnp.min(jnp.where(work == mx, lane_f, float(LANES)), axis=-1, keepdims=True)
        hit = lane_f == ik
        sel = jnp.where(hit, 1.0, sel)
        work = jnp.where(hit, -jnp.inf, work)
        vals.append(mx)
        idxs.append(ik)
    es = [jnp.exp(v - vals[0]) for v in vals]
    den = es[0] + es[1] + es[2] + es[3]
    sub = cnt_ref.shape[0]
    mt = tm // sub
    r = lax.broadcasted_iota(I32, (tm, tm), 0)
    c = lax.broadcasted_iota(I32, (tm, tm), 1)
    tril = jnp.where((r > c) & (r // mt == c // mt), 1.0, 0.0).astype(BF16)
    cum = _dot(tril, sel.astype(BF16))
    idx_o = jnp.zeros((tm, LANES), F32)
    wt_o = jnp.zeros((tm, LANES), F32)
    rank_o = jnp.zeros((tm, LANES), F32)
    for k in range(TOP_K):
        rk = jnp.sum(jnp.where(lane_f == idxs[k], cum, 0.0), axis=-1, keepdims=True)
        idx_o = jnp.where(lane == k, idxs[k], idx_o)
        wt_o = jnp.where(lane == k, es[k] / den, wt_o)
        rank_o = jnp.where(lane == k, rk, rank_o)
    idx_ref[...] = idx_o.astype(I32)
    wt_ref[...] = wt_o
    rank_ref[...] = rank_o.astype(I32)
    for s in range(sub):
        cnt_ref[s] = jnp.broadcast_to(jnp.sum(sel[s * mt:(s + 1) * mt], axis=0, keepdims=True), (8, LANES))


def _router(x1, gain, w_r, b_r):
    t = x1.shape[0]
    mt = min(MOE_TILE, t)
    sub = 2 if t % (2 * mt) == 0 else 1
    tm = sub * mt
    row = lambda i: (i, 0)
    fixed = lambda i: (0, 0)
    return pl.pallas_call(
        _router_body,
        grid=(t // tm,),
        in_specs=[
            pl.BlockSpec((tm, D_MODEL), row),
            pl.BlockSpec((1, D_MODEL), fixed),
            pl.BlockSpec((D_MODEL, LANES), fixed),
            pl.BlockSpec((1, LANES), fixed),
        ],
        out_specs=[
            pl.BlockSpec((tm, D_MODEL), row),
            pl.BlockSpec((tm, LANES), row),
            pl.BlockSpec((tm, LANES), row),
            pl.BlockSpec((tm, LANES), row),
            pl.BlockSpec((sub, 8, LANES), lambda i: (i, 0, 0)),
        ],
        out_shape=[
            jax.ShapeDtypeStruct((t, D_MODEL), BF16),
            jax.ShapeDtypeStruct((t, LANES), I32),
            jax.ShapeDtypeStruct((t, LANES), F32),
            jax.ShapeDtypeStruct((t, LANES), I32),
            jax.ShapeDtypeStruct((t // mt, 8, LANES), F32),
        ],
        compiler_params=pltpu.CompilerParams(dimension_semantics=("arbitrary",)),
        name="router",
    )(x1, gain, w_r, b_r)


def _lane_cumsum(x):
    lane = lax.broadcasted_iota(I32, x.shape, 1)
    s = 1
    while s < N_EXPERTS:
        x = x + jnp.where(lane >= s, pltpu.roll(x, s, axis=1), 0.0)
        s *= 2
    return x


def _plan_body(cnt_ref, gs_ref, ls_ref, n8_ref, lsf_ref, tail_ref, meta_ref):
    nt = cnt_ref.shape[0]
    ga = float(GROUP_ALIGN)
    eb = float(EXPERT_BLOCK)
    lane = lax.broadcasted_iota(I32, (nt, LANES), 1)
    r8 = jnp.where(lane < N_EXPERTS, jnp.floor((cnt_ref[...] + (ga - 1.0)) * (1.0 / ga)) * ga, 0.0)
    ri = lax.broadcasted_iota(I32, (nt, nt), 0)
    ci = lax.broadcasted_iota(I32, (nt, nt), 1)
    before = _dot(jnp.where(ri > ci, 1.0, 0.0).astype(BF16), r8.astype(BF16))
    tot = jnp.sum(r8, axis=0, keepdims=True)
    region = jnp.floor((tot + (eb - 1.0)) * (1.0 / eb)) * eb
    pends = _lane_cumsum(jnp.broadcast_to(region, (8, LANES)))[0:1, :]
    pstart = pends - region
    lstart = _lane_cumsum(r8) - r8
    gs_ref[...] = (pstart + before).astype(I32)
    ls_ref[...] = lstart.astype(I32)
    pieces = r8 * (1.0 / ga)
    n8_ref[...] = jnp.where(lane == TOTAL_LANE, jnp.sum(pieces, axis=1, keepdims=True), pieces).astype(I32)
    lsf_ref[...] = lstart
    row8 = lax.broadcasted_iota(I32, (8, LANES), 0)
    tail_n = (region - tot) * (1.0 / ga)
    lane8 = lax.broadcasted_iota(I32, (8, LANES), 1)
    tail_n = jnp.where(lane8 == TOTAL_LANE, jnp.sum(tail_n, axis=1, keepdims=True), tail_n)
    tail = jnp.where(row8 == 0, pstart + tot, jnp.where(row8 == 1, tail_n, 0.0))
    tail_ref[...] = tail.astype(I32)
    nb = meta_ref.shape[0]
    ln = lax.broadcasted_iota(I32, (nb, LANES), 1)
    blk = lax.broadcasted_iota(I32, (nb, LANES), 0).astype(F32) * eb
    be = jnp.sum(jnp.where((ln < N_EXPERTS) & (pends <= blk), 1.0, 0.0), axis=-1, keepdims=True)
    be = jnp.minimum(be, float(N_EXPERTS - 1))
    used = jnp.sum(jnp.where(ln == N_EXPERTS - 1, pends, 0.0), axis=-1, keepdims=True) * (1.0 / eb)
    meta_ref[...] = jnp.where(ln == 0, be, jnp.where(ln == 1, used, 0.0)).astype(I32)


def _plan(cnt, n_blocks_pad):
    nt = cnt.shape[0]
    shp = jax.ShapeDtypeStruct((nt, LANES), I32)
    return pl.pallas_call(
        _plan_body,
        out_shape=[shp, shp, shp, jax.ShapeDtypeStruct((nt, LANES), F32),
                   jax.ShapeDtypeStruct((8, LANES), I32),
                   jax.ShapeDtypeStruct((n_blocks_pad, LANES), I32)],
        name="dispatch_plan",
    )(cnt)


def _local_positions(idx_ref, rank_ref, lsf_ref):
    tm = idx_ref.shape[0]
    lane = lax.broadcasted_iota(I32, (tm, LANES), 1)
    idx = idx_ref[...]
    rank = rank_ref[...].astype(F32)
    ls_row = lsf_ref[0]
    out = []
    for k in range(TOP_K):
        base = jnp.sum(jnp.where(lane == idx[:, k:k + 1], ls_row, 0.0), axis=-1, keepdims=True)
        out.append((base + rank[:, k:k + 1]).astype(I32))
    return out


def _local_positions_lanes(idx_ref, rank_ref, lsf_ref):
    tm = idx_ref.shape[0]
    idx_t = idx_ref[...].astype(F32).T
    rank_t = rank_ref[...].astype(F32).T
    ls_col = jnp.broadcast_to(lsf_ref[0], (LANES, LANES)).T[:, 0:1]
    expert = lax.broadcasted_iota(I32, (LANES, tm), 0).astype(F32)
    out = []
    for k in range(TOP_K):
        base = jnp.sum(jnp.where(expert == idx_t[k:k + 1, :], ls_col, 0.0), axis=0, keepdims=True)
        out.append((base + rank_t[k:k + 1, :]).astype(I32))
    return out


PACK_W = D_MODEL // 2
U32 = jnp.uint32


def _pack_rows(x):
    xb = x.astype(BF16).astype(F32)
    hi = lax.bitcast_convert_type(xb[:, :PACK_W], U32)
    lo = lax.bitcast_convert_type(xb[:, PACK_W:], U32)
    return hi | (lo >> 16)


def _unpack_rows(w):
    hi = lax.bitcast_convert_type(w & jnp.uint32(0xFFFF0000), F32)
    lo = lax.bitcast_convert_type(w << 16, F32)
    return hi.astype(BF16), lo.astype(BF16)


TOTAL_LANE = LANES - 1
WAIT_CHUNK = 128
BIG_PIECE = 4


def _start_group_copies(tables, tile, make_copy):
    gs_ref, ls_ref, n8_ref = tables

    def group(e, carry):
        g0 = gs_ref[tile, e]
        l0 = ls_ref[tile, e]
        n = n8_ref[tile, e]
        nbig = n // BIG_PIECE

        def big(j, c):
            off = j * (BIG_PIECE * GROUP_ALIGN)
            make_copy(pl.multiple_of(l0 + off, GROUP_ALIGN), pl.multiple_of(g0 + off, GROUP_ALIGN),
                      BIG_PIECE * GROUP_ALIGN).start()
            return c

        def one(j, c):
            off = j * GROUP_ALIGN
            make_copy(pl.multiple_of(l0 + off, GROUP_ALIGN), pl.multiple_of(g0 + off, GROUP_ALIGN),
                      GROUP_ALIGN).start()
            return c

        lax.fori_loop(0, nbig, big, 0)
        lax.fori_loop(nbig * BIG_PIECE, n, one, 0)
        return carry

    lax.fori_loop(0, N_EXPERTS, group, 0)


def _wait_pieces(n, make_wait):
    def chunk(j, c):
        make_wait(WAIT_CHUNK).wait()
        return c

    lax.fori_loop(0, n // WAIT_CHUNK, chunk, 0)
    b = WAIT_CHUNK // 2
    while b >= 1:
        def _(b=b):
            make_wait(b).wait()
        pl.when((n & b) != 0)(_)
        b //= 2


def _dispatch_body(gs_ref, ls_ref, n8_ref, tail_ref, h_ref, idx_ref, rank_ref, lsf_ref, xs_ref,
                   xl, zbuf, sem):
    i = pl.program_id(0)
    last = i == pl.num_programs(0) - 1
    slot = i % 2
    tables = (gs_ref, ls_ref, n8_ref)
    tm = h_ref.shape[0]
    lrows = xl.shape[1]
    pos = lax.broadcasted_iota(I32, (lrows, tm), 0)
    lp = _local_positions_lanes(idx_ref, rank_ref, lsf_ref)
    hit = pos == lp[0]
    for k in range(1, TOP_K):
        hit = hit | (pos == lp[k])
    xl[slot] = _pack_rows(_dot(jnp.where(hit, 1.0, 0.0).astype(BF16), h_ref[...]))

    def copy_from(s):
        def make_copy(l, g, rows):
            return pltpu.make_async_copy(xl.at[s, pl.ds(l, rows), :], xs_ref.at[pl.ds(g, rows), :], sem.at[s])
        return make_copy

    def wait_on(s):
        def make_wait(pieces):
            rows = pieces * GROUP_ALIGN
            return pltpu.make_async_copy(xl.at[0, pl.ds(0, rows), :], xs_ref.at[pl.ds(0, rows), :], sem.at[s])
        return make_wait

    @pl.when(i >= 1)
    def _():
        _wait_pieces(n8_ref[i - 1, TOTAL_LANE], wait_on(1 - slot))

    _start_group_copies(tables, i, copy_from(slot))

    def start_tails():
        def per_expert(e, carry):
            g0 = tail_ref[0, e]

            def one(j, c):
                pltpu.make_async_copy(
                    zbuf, xs_ref.at[pl.ds(pl.multiple_of(g0 + j * GROUP_ALIGN, GROUP_ALIGN), GROUP_ALIGN), :],
                    sem.at[2]).start()
                return c

            lax.fori_loop(0, tail_ref[1, e], one, 0)
            return carry

        lax.fori_loop(0, N_EXPERTS, per_expert, 0)

    @pl.when(last)
    def _():
        zbuf[...] = jnp.zeros_like(zbuf)
        start_tails()
        _wait_pieces(n8_ref[i, TOTAL_LANE], wait_on(slot))
        _wait_pieces(tail_ref[1, TOTAL_LANE], wait_on(2))


def _dispatch(gs, ls, n8, tail, h2, idx, rank, lsf, n_slots):
    t = h2.shape[0]
    tm = min(MOE_TILE, t)
    nt = t // tm
    lrows = tm * TOP_K + N_EXPERTS * GROUP_ALIGN
    row = lambda i, *_: (i, 0)
    grid_spec = pltpu.PrefetchScalarGridSpec(
        num_scalar_prefetch=4,
        grid=(nt,),
        in_specs=[
            pl.BlockSpec((tm, D_MODEL), row),
            pl.BlockSpec((tm, LANES), row),
            pl.BlockSpec((tm, LANES), row),
            pl.BlockSpec((1, 1, LANES), lambda i, *_: (i, 0, 0)),
        ],
        out_specs=pl.BlockSpec(memory_space=pl.ANY),
        scratch_shapes=[pltpu.VMEM((2, lrows, PACK_W), U32), pltpu.VMEM((GROUP_ALIGN, PACK_W), U32),
                        pltpu.SemaphoreType.DMA((3,))],
    )
    return pl.pallas_call(
        _dispatch_body,
        grid_spec=grid_spec,
        out_shape=jax.ShapeDtypeStruct((n_slots, PACK_W), U32),
        compiler_params=pltpu.CompilerParams(dimension_semantics=("arbitrary",)),
        name="moe_dispatch",
    )(gs, ls, n8, tail, h2, idx, rank, lsf)


def _expert_body(be_ref, nu_ref, x_ref, wgu_ref, bgu_ref, wd_ref, bd_ref, y_ref, wgu_b, wd_b):
    j = pl.program_id(0)

    @pl.when(j < nu_ref[0])
    def _():
        @pl.when(jnp.logical_or(j == 0, be_ref[j] != be_ref[jnp.maximum(j - 1, 0)]))
        def _():
            wgu_b[...] = wgu_ref[0].astype(BF16)
            wd_b[...] = wd_ref[0].astype(BF16)

        gu = _dot(jnp.concatenate(_unpack_rows(x_ref[...]), axis=1), wgu_b[...]) + bgu_ref[0]
        gate = jnp.minimum(gu[:, :D_FF], SWIGLU_LIMIT)
        up = jnp.clip(gu[:, D_FF:], -SWIGLU_LIMIT, SWIGLU_LIMIT)
        act = (up + 1.0) * gate * _sigmoid(SWIGLU_ALPHA * gate)
        y_ref[...] = _pack_rows(_dot(act.astype(BF16), wd_b[...]) + bd_ref[0])


def _experts(block_e, n_used, xs, wgu, bgu, wd, bd):
    n_slots = xs.shape[0]
    nb = n_slots // EXPERT_BLOCK

    def blk(j, be, nu):
        return jnp.minimum(j, nu[0] - 1)

    grid_spec = pltpu.PrefetchScalarGridSpec(
        num_scalar_prefetch=2,
        grid=(nb,),
        in_specs=[
            pl.BlockSpec((EXPERT_BLOCK, PACK_W), lambda j, be, nu: (blk(j, be, nu), 0)),
            pl.BlockSpec((1, D_MODEL, 2 * D_FF), lambda j, be, nu: (be[blk(j, be, nu)], 0, 0)),
            pl.BlockSpec((1, 1, 2 * D_FF), lambda j, be, nu: (be[blk(j, be, nu)], 0, 0)),
            pl.BlockSpec((1, D_FF, D_MODEL), lambda j, be, nu: (be[blk(j, be, nu)], 0, 0)),
            pl.BlockSpec((1, 1, D_MODEL), lambda j, be, nu: (be[blk(j, be, nu)], 0, 0)),
        ],
        out_specs=pl.BlockSpec((EXPERT_BLOCK, PACK_W), lambda j, be, nu: (blk(j, be, nu), 0)),
        scratch_shapes=[pltpu.VMEM((D_MODEL, 2 * D_FF), BF16), pltpu.VMEM((D_FF, D_MODEL), BF16)],
    )
    return pl.pallas_call(
        _expert_body,
        grid_spec=grid_spec,
        out_shape=jax.ShapeDtypeStruct((n_slots, PACK_W), U32),
        compiler_params=pltpu.CompilerParams(dimension_semantics=("arbitrary",)),
        name="moe_experts",
    )(block_e, n_used, xs, wgu, bgu, wd, bd)


def _combine_body(gs_ref, ls_ref, n8_ref, x_ref, wt_ref, idx_ref, rank_ref, lsf_ref, y_ref, out_ref,
                  yl, sem):
    i = pl.program_id(0)
    slot = i % 2
    tables = (gs_ref, ls_ref, n8_ref)
    tm = x_ref.shape[0]
    lrows = yl.shape[1]

    def copy_into(s):
        def make_copy(l, g, rows):
            return pltpu.make_async_copy(y_ref.at[pl.ds(g, rows), :], yl.at[s, pl.ds(l, rows), :], sem.at[s])
        return make_copy

    def make_wait(pieces):
        rows = pieces * GROUP_ALIGN
        return pltpu.make_async_copy(y_ref.at[pl.ds(0, rows), :], yl.at[0, pl.ds(0, rows), :], sem.at[slot])

    @pl.when(i == 0)
    def _():
        yl[...] = jnp.zeros_like(yl)
        _start_group_copies(tables, i, copy_into(slot))

    @pl.when(i + 1 < pl.num_programs(0))
    def _():
        _start_group_copies(tables, i + 1, copy_into(1 - slot))

    pos = lax.broadcasted_iota(I32, (tm, lrows), 1)
    lp = _local_positions(idx_ref, rank_ref, lsf_ref)
    wt = wt_ref[...]
    wm = jnp.zeros((tm, lrows), F32)
    for k in range(TOP_K):
        wm = jnp.where(pos == lp[k], wt[:, k:k + 1], wm)
    _wait_pieces(n8_ref[i, TOTAL_LANE], make_wait)
    wmb = wm.astype(BF16)
    y_hi, y_lo = _unpack_rows(yl[slot])
    out_ref[:, :PACK_W] = x_ref[:, :PACK_W] + _dot(wmb, y_hi)
    out_ref[:, PACK_W:] = x_ref[:, PACK_W:] + _dot(wmb, y_lo)


def _combine(gs, ls, n8, x1, wts, idx, rank, lsf, y):
    t = x1.shape[0]
    tm = min(MOE_TILE, t)
    nt = t // tm
    lrows = tm * TOP_K + N_EXPERTS * GROUP_ALIGN
    row = lambda i, *_: (i, 0)
    grid_spec = pltpu.PrefetchScalarGridSpec(
        num_scalar_prefetch=3,
        grid=(nt,),
        in_specs=[
            pl.BlockSpec((tm, D_MODEL), row),
            pl.BlockSpec((tm, LANES), row),
            pl.BlockSpec((tm, LANES), row),
            pl.BlockSpec((tm, LANES), row),
            pl.BlockSpec((1, 1, LANES), lambda i, *_: (i, 0, 0)),
            pl.BlockSpec(memory_space=pl.ANY),
        ],
        out_specs=pl.BlockSpec((tm, D_MODEL), row),
        scratch_shapes=[pltpu.VMEM((2, lrows, PACK_W), U32), pltpu.SemaphoreType.DMA((2,))],
    )
    return pl.pallas_call(
        _combine_body,
        grid_spec=grid_spec,
        out_shape=jax.ShapeDtypeStruct((t, D_MODEL), F32),
        compiler_params=pltpu.CompilerParams(dimension_semantics=("arbitrary",)),
        name="moe_combine",
    )(gs, ls, n8, x1, wts, idx, rank, lsf, y)


def _pad_lanes(v, fill=0.0):
    v = v.astype(F32).reshape(1, -1)
    return jnp.pad(v, ((0, 0), (0, LANES - v.shape[1])), constant_values=fill)


def _mixer(x2, mem, rel_table, attn_norm, w_in, b_gate, dn_conv, dn_a_log, dn_dt_bias, dn_out_norm,
           da_q_norm, da_k_norm, da_lambda, da_subln, mem_norm, w_mem_kv, mx_q_norm, mx_k_norm,
           w_branch, w_out, batch, seq):
    wp = jnp.concatenate([w_in[:, :W_AB_LO], w_in[:, W_AB_HI:]], axis=1).astype(BF16)
    wab = jnp.pad(w_in[:, W_AB_LO:W_AB_HI], ((0, 0), (0, LANES - (W_AB_HI - W_AB_LO))))
    p, ab = _inproj(x2, attn_norm.reshape(1, -1), wp, wab)

    o_dn = _deltanet(p, ab, dn_conv, _pad_lanes(dn_a_log), _pad_lanes(dn_dt_bias),
                     dn_out_norm.reshape(1, -1), batch, seq)

    tq = min(ATT_BLOCK, seq)
    bias = _bias_tiles(rel_table.T, tq)
    o_da = _attention(p, bias, jnp.tile(da_q_norm, 2).reshape(1, -1), jnp.tile(da_k_norm, 2).reshape(1, -1),
                      da_lambda, da_subln.reshape(-1, 1), batch, seq)

    mk, mv = _memkv(mem, mem_norm.reshape(1, -1), w_mem_kv.astype(BF16), mx_k_norm.reshape(1, -1))
    return _merge(x2, o_dn, o_da, p, mk, mv, mx_q_norm.reshape(1, -1), b_gate.reshape(3, D_MODEL),
                  w_branch.astype(BF16), w_out.astype(BF16), seq)


def _moe(x1, ffn_norm, w_router, b_router, w_gate_up, b_gate_up, w_down, b_down):
    t = x1.shape[0]
    nt = t // min(MOE_TILE, t)
    max_rows = t * TOP_K + nt * N_EXPERTS * (GROUP_ALIGN - 1)
    n_blocks = -(-max_rows // EXPERT_BLOCK) + N_EXPERTS
    n_blocks_pad = -(-n_blocks // 8) * 8
    n_slots = n_blocks * EXPERT_BLOCK

    wr = jnp.pad(w_router, ((0, 0), (0, LANES - N_EXPERTS)))
    h2, idx, wts, rank, cnt = _router(x1, ffn_norm.reshape(1, -1), wr, _pad_lanes(b_router, NEG))
    gs, ls, n8, lsf, tail, meta = _plan(cnt[:, 0, :], n_blocks_pad)
    block_e = meta[:n_blocks, 0]
    n_used = meta[0:1, 1]
    lsf = lsf.reshape(nt, 1, LANES)

    xs = _dispatch(gs, ls, n8, tail, h2, idx, rank, lsf, n_slots)
    y = _experts(block_e, n_used, xs, w_gate_up, b_gate_up.reshape(N_EXPERTS, 1, -1),
                 w_down, b_down.reshape(N_EXPERTS, 1, -1))
    return _combine(gs, ls, n8, x1, wts, idx, rank, lsf, y)


def kernel(x, mem, rel_table, attn_norm, w_in, b_gate, dn_conv, dn_a_log, dn_dt_bias, dn_out_norm,
           da_q_norm, da_k_norm, da_lambda, da_subln, mem_norm, w_mem_kv, mx_q_norm, mx_k_norm,
           w_branch, w_out, ffn_norm, w_router, b_router, w_gate_up, b_gate_up, w_down, b_down):
    batch, seq, d = x.shape
    x2 = x.reshape(batch * seq, d)
    x1 = _mixer(x2, mem, rel_table, attn_norm[0], w_in[0], b_gate[0], dn_conv[0], dn_a_log[0],
                dn_dt_bias[0], dn_out_norm[0], da_q_norm[0], da_k_norm[0], da_lambda[0], da_subln[0],
                mem_norm[0], w_mem_kv[0], mx_q_norm[0], mx_k_norm[0], w_branch[0], w_out[0], batch, seq)
    out = _moe(x1, ffn_norm[0], w_router[0], b_router[0], w_gate_up[0], b_gate_up[0], w_down[0],
               b_down[0])
    return out.reshape(batch, seq, d)
```

```python
import functools
import math

import jax
import jax.numpy as jnp
from jax import lax
from jax.experimental import pallas as pl
from jax.experimental.pallas import tpu as pltpu

F32 = jnp.float32
BF16 = jnp.bfloat16
I32 = jnp.int32

D_MODEL = 1024
EPS = 1e-6
LANES = 128

DN_HEADS = 4
DN_DK = 128
DN_CHUNK = 64
DN_CONV = 4

DA_HEADS = 4
DA_DH = 64

MX_HEADS = 4
MX_DH = 128

REL_BUCKETS = 32
REL_MAX_DIST = 128

N_EXPERTS = 32
TOP_K = 4
D_FF = 1024
SWIGLU_LIMIT = 7.0
SWIGLU_ALPHA = 1.702
EXPERT_BLOCK = 512
MOE_TILE = 256
GROUP_ALIGN = 8

LAM_INIT = 0.8 - 0.6 * math.exp(-0.3 * 0)
LOG2E = 1.4426950408889634
NEG = -1e30

P_DNQ, P_DNK, P_DNV, P_DNZ = 0, 512, 1024, 1536
P_DAQ, P_DAK, P_DAV = 2048, 2560, 3072
P_MXQ = 3584
P_GATE = 4096
P_COLS = 7168
W_AB_LO, W_AB_HI = 2048, 2056


def _dot(a, b):
    return jnp.dot(a, b, preferred_element_type=F32)


def _dot_nt(a, b):
    return lax.dot_general(a, b, (((1,), (1,)), ((), ())), preferred_element_type=F32)


def _dot_tn(a, b):
    return lax.dot_general(a, b, (((0,), (0,)), ((), ())), preferred_element_type=F32)


def _split(x):
    hi = x.astype(BF16)
    lo = (x - hi.astype(F32)).astype(BF16)
    return hi, lo


def _dot3(a, b):
    ah, al = _split(a)
    bh, bl = _split(b)
    return _dot(ah, bh) + _dot(ah, bl) + _dot(al, bh)


def _sigmoid(x):
    return 1.0 / (1.0 + jnp.exp(-x))


def _rms(x, n):
    return lax.rsqrt(jnp.sum(x * x, axis=-1, keepdims=True) * (1.0 / n) + EPS)


def _inproj_body(x_ref, g_ref, w_ref, wab_ref, p_ref, ab_ref, h_scr):
    @pl.when(pl.program_id(1) == 0)
    def _():
        x = x_ref[...]
        h = x * _rms(x, D_MODEL) * g_ref[...]
        h_scr[...] = h.astype(BF16)
        ab_ref[...] = _dot3(h, wab_ref[...])

    p_ref[...] = _dot(h_scr[...], w_ref[...]).astype(p_ref.dtype)


def _inproj(x2, gain, wp, wab):
    t = x2.shape[0]
    tm = min(1024, t)
    tn = 1024
    return pl.pallas_call(
        _inproj_body,
        grid=(t // tm, P_COLS // tn),
        in_specs=[
            pl.BlockSpec((tm, D_MODEL), lambda i, j: (i, 0)),
            pl.BlockSpec((1, D_MODEL), lambda i, j: (0, 0)),
            pl.BlockSpec((D_MODEL, tn), lambda i, j: (0, j)),
            pl.BlockSpec((D_MODEL, LANES), lambda i, j: (0, 0)),
        ],
        out_specs=[
            pl.BlockSpec((tm, tn), lambda i, j: (i, j)),
            pl.BlockSpec((tm, LANES), lambda i, j: (i, 0)),
        ],
        out_shape=[
            jax.ShapeDtypeStruct((t, P_COLS), BF16),
            jax.ShapeDtypeStruct((t, LANES), F32),
        ],
        scratch_shapes=[pltpu.VMEM((tm, D_MODEL), BF16)],
        compiler_params=pltpu.CompilerParams(dimension_semantics=("parallel", "arbitrary")),
        name="inproj",
    )(x2, gain, wp, wab)


DN_HALO = 16
DN_SCAN_CHUNK = 256


def _deltanet_body(q_ref, k_ref, v_ref, z_ref, qh_ref, kh_ref, vh_ref, ab_ref, cw_ref, alog_ref,
                   dtb_ref, on_ref, o_ref, stage, qs, ks, vs, s_scr):
    i = pl.program_id(1)
    tc = q_ref.shape[0]
    hw = DN_HEADS * DN_DK

    @pl.when(i == 0)
    def _():
        s_scr[...] = jnp.zeros_like(s_scr)

    for src, halo, dst, off, kind in ((q_ref, qh_ref, qs, 0, "q"), (k_ref, kh_ref, ks, hw, "k"),
                                      (v_ref, vh_ref, vs, 2 * hw, "v")):
        hal = halo[...].astype(F32)
        stage[0:DN_HALO, :] = jnp.where(i == 0, 0.0, hal)
        stage[DN_HALO:DN_HALO + tc, :] = src[...].astype(F32)
        base = DN_HALO - (DN_CONV - 1)
        y = stage[base:base + tc, :] * cw_ref[0:1, off:off + hw]
        for j in range(1, DN_CONV):
            y = y + stage[base + j:base + j + tc, :] * cw_ref[j:j + 1, off:off + hw]
        y = y * _sigmoid(y)
        if kind == "v":
            dst[...] = y
        else:
            for h in range(DN_HEADS):
                sl = slice(h * DN_DK, (h + 1) * DN_DK)
                yh = y[:, sl]
                r = lax.rsqrt(jnp.sum(yh * yh, axis=-1, keepdims=True) + EPS)
                if kind == "q":
                    r = r * (DN_DK ** -0.5)
                dst[:, sl] = yh * r

    c = min(DN_SCAN_CHUNK, tc)
    row = lax.broadcasted_iota(I32, (c, c), 0)
    col = lax.broadcasted_iota(I32, (c, c), 1)
    incl = row >= col
    strict = row > col
    same_blk = (row // DN_CHUNK) == (col // DN_CHUNK)
    tri = jnp.where(incl, 1.0, 0.0).astype(BF16)
    eye = jnp.where(row == col, 1.0, 0.0)
    neg_a = -jnp.exp(alog_ref[...])
    dtb = dtb_ref[...]

    def chunk(ci, carry):
        r0 = pl.multiple_of(ci * c, c)
        abc = ab_ref[pl.ds(r0, c), :]
        a_in = abc + dtb
        g_all = neg_a * (jnp.maximum(a_in, 0.0) + jnp.log(1.0 + jnp.exp(-jnp.abs(a_in))))
        beta_all = _sigmoid(abc)
        zc = z_ref[pl.ds(r0, c), :].astype(F32)
        hs = range(DN_HEADS)
        sls = [slice(h * DN_DK, (h + 1) * DN_DK) for h in hs]
        q = [qs[pl.ds(r0, c), sl] for sl in sls]
        k = [ks[pl.ds(r0, c), sl] for sl in sls]
        v = [vs[pl.ds(r0, c), sl] for sl in sls]
        g = [g_all[:, h:h + 1] for h in hs]
        beta = [beta_all[:, DN_HEADS + h:DN_HEADS + h + 1] for h in hs]
        g_split = [_split(jnp.where(strict, g[h], 0.0)) for h in hs]
        diff = [_dot(tri, g_split[h][0]) + _dot(tri, g_split[h][1]) for h in hs]
        kb = [k[h].astype(BF16) for h in hs]
        qkk = [_dot_nt(jnp.concatenate([q[h].astype(BF16), kb[h]], axis=0), kb[h]) for h in hs]
        gc = [diff[h][:, 0:1] + g[h][0:1, :] for h in hs]
        decay = [jnp.where(incl, jnp.exp(diff[h]), 0.0) for h in hs]
        lower = [jnp.where(strict, qkk[h][c:] * decay[h] * beta[h], 0.0) for h in hs]
        pw = [jnp.where(same_blk, -lower[h], 0.0) for h in hs]
        dinv = [eye + pw[h] for h in hs]
        pwb = [pw[h].astype(BF16) for h in hs]
        for _ in range(int(math.log2(DN_CHUNK)) - 1):
            pwb = [_dot(pwb[h], pwb[h]).astype(BF16) for h in hs]
            dinv = [dinv[h] + _dot(dinv[h].astype(BF16), pwb[h]) for h in hs]
        dinv_b = [dinv[h].astype(BF16) for h in hs]
        pw = [-_dot(dinv_b[h], jnp.where(same_blk, 0.0, lower[h]).astype(BF16)) for h in hs]
        xm = [eye + pw[h] for h in hs]
        pwb = [pw[h].astype(BF16) for h in hs]
        for _ in range(int(math.log2(c // DN_CHUNK)) - 1):
            pwb = [_dot(pwb[h], pwb[h]).astype(BF16) for h in hs]
            xm = [xm[h] + _dot(xm[h].astype(BF16), pwb[h]) for h in hs]
        inv = [_dot(xm[h].astype(BF16), dinv_b[h]).astype(BF16) for h in hs]
        egc = [jnp.exp(gc[h]) for h in hs]
        rhs = [jnp.concatenate([v[h] * beta[h], k[h] * (beta[h] * egc[h])], axis=1).astype(BF16) for h in hs]
        sol = [_dot(inv[h], rhs[h]) for h in hs]
        qkm = [jnp.where(incl, qkk[h][:c] * decay[h], 0.0).astype(BF16) for h in hs]
        gl = [gc[h][c - 1:c, :] for h in hs]
        state = [s_scr[h] for h in hs]
        ws = [_dot(jnp.concatenate([sol[h][:, DN_DK:].astype(BF16), (q[h] * egc[h]).astype(BF16)], axis=0),
                   state[h].astype(BF16)) for h in hs]
        v_new = [sol[h][:, :DN_DK] - ws[h][:c] for h in hs]
        o = [ws[h][c:] + _dot(qkm[h], v_new[h].astype(BF16)) for h in hs]
        for h in hs:
            s_scr[h] = state[h] * jnp.exp(gl[h]) + _dot_tn(kb[h], (v_new[h] * jnp.exp(gl[h] - gc[h])).astype(BF16))
        for h in hs:
            zz = zc[:, sls[h]]
            on = o[h] * _rms(o[h], DN_DK) * on_ref[...]
            o_ref[pl.ds(r0, c), sls[h]] = (on * (zz * _sigmoid(zz))).astype(o_ref.dtype)
        return carry

    lax.fori_loop(0, tc // c, chunk, 0, unroll=True)


def _deltanet(p, ab, conv_w, alog_row, dtb_row, out_norm, batch, seq):
    t = batch * seq
    tc = min(512, seq)
    nt = seq // tc
    hw = DN_HEADS * DN_DK

    def main(cb):
        return pl.BlockSpec((tc, hw), lambda b, i: (b * nt + i, cb))

    def halo(cb):
        return pl.BlockSpec(
            (DN_HALO, hw),
            lambda b, i: (jnp.maximum((b * seq + i * tc) // DN_HALO - 1, 0), cb))

    return pl.pallas_call(
        _deltanet_body,
        grid=(batch, nt),
        in_specs=[
            main(P_DNQ // hw), main(P_DNK // hw), main(P_DNV // hw), main(P_DNZ // hw),
            halo(P_DNQ // hw), halo(P_DNK // hw), halo(P_DNV // hw),
            pl.BlockSpec((tc, LANES), lambda b, i: (b * nt + i, 0)),
            pl.BlockSpec((DN_CONV, 3 * hw), lambda b, i: (0, 0)),
            pl.BlockSpec((1, LANES), lambda b, i: (0, 0)),
            pl.BlockSpec((1, LANES), lambda b, i: (0, 0)),
            pl.BlockSpec((1, DN_DK), lambda b, i: (0, 0)),
        ],
        out_specs=pl.BlockSpec((tc, hw), lambda b, i: (b * nt + i, 0)),
        out_shape=jax.ShapeDtypeStruct((t, hw), BF16),
        scratch_shapes=[
            pltpu.VMEM((DN_HALO + tc, hw), F32),
            pltpu.VMEM((tc, hw), F32),
            pltpu.VMEM((tc, hw), F32),
            pltpu.VMEM((tc, hw), F32),
            pltpu.VMEM((DN_HEADS, DN_DK, DN_DK), F32),
        ],
        compiler_params=pltpu.CompilerParams(dimension_semantics=("parallel", "arbitrary")),
        name="deltanet",
    )(p, p, p, p, p, p, p, ab, conv_w, alog_row, dtb_row, out_norm)


ATT_BLOCK = 512


def _bias_body(tbl_ref, o_ref):
    h = pl.program_id(0)
    tq = o_ref.shape[2]
    key = lax.broadcasted_iota(I32, (tq, tq), 0)
    qry = lax.broadcasted_iota(I32, (tq, tq), 1)
    max_exact = REL_BUCKETS // 2
    far = tbl_ref[h, REL_BUCKETS - 1]
    for d in range(2):
        n = qry - key + d * tq
        nn = jnp.maximum(n, 0)
        nf = jnp.maximum(nn, 1).astype(F32)
        large = max_exact + (jnp.log(nf / max_exact) / math.log(REL_MAX_DIST / max_exact)
                             * (REL_BUCKETS - max_exact)).astype(I32)
        large = jnp.minimum(large, REL_BUCKETS - 1)
        bucket = jnp.where(nn < max_exact, nn, large)
        val = jnp.zeros((tq, tq), F32)
        for b in range(REL_BUCKETS):
            val = jnp.where(bucket == b, tbl_ref[h, b], val)
        o_ref[0, d] = jnp.where(n >= 0, (val - far) * LOG2E, NEG)


def _bias_tiles(tbl_t, tq):
    return pl.pallas_call(
        _bias_body,
        grid=(DA_HEADS,),
        in_specs=[pl.BlockSpec(memory_space=pltpu.SMEM)],
        out_specs=pl.BlockSpec((1, 2, tq, tq), lambda h: (h, 0, 0, 0)),
        out_shape=jax.ShapeDtypeStruct((DA_HEADS, 2, tq, tq), F32),
        name="t5_bias_tiles",
    )(tbl_t)


DA_DV = 2 * DA_DH
DA_VROWS = DA_DV + 16


BOUND_SLACK = 1.02
MAX_SHIFT_GAP = 110.0


def _attn_body(q_ref, qn_ref, k_ref, v_ref, bias_ref, qg_ref, kg_ref, lam_ref, sg_ref, o_ref,
               kn, vt, kst, qc_s, bd_s, flag_s, m_s, acc_s):
    qi = pl.program_id(2)
    slot = qi % 2
    tq = q_ref.shape[0]
    seq = k_ref.shape[0]
    tk = tq
    lo_mask = lax.broadcasted_iota(I32, (1, DA_DV), 1) < DA_DH

    def group_norm(x, gain):
        x2 = x * x
        lo = jnp.sum(jnp.where(lo_mask, x2, 0.0), axis=-1, keepdims=True)
        hi = jnp.sum(jnp.where(lo_mask, 0.0, x2), axis=-1, keepdims=True)
        r = jnp.where(lo_mask, lax.rsqrt(lo * (1.0 / DA_DH) + EPS), lax.rsqrt(hi * (1.0 / DA_DH) + EPS))
        return x * r * gain

    @pl.when(qi == 0)
    def _():
        ones = jnp.ones((DA_VROWS - DA_DV, tk), BF16)

        def body(c, kmax2):
            r0 = pl.multiple_of(c * tk, tk)
            kb = group_norm(k_ref[pl.ds(r0, tk), :].astype(F32), kg_ref[...]).astype(BF16)
            kn[pl.ds(r0, tk), :] = kb
            vt[c, 0:DA_DV, :] = v_ref[pl.ds(r0, tk), :].astype(F32).T.astype(BF16)
            vt[c, DA_DV:DA_VROWS, :] = ones
            k2 = kb.astype(F32)
            k2 = k2 * k2
            lo = jnp.max(jnp.sum(jnp.where(lo_mask, k2, 0.0), axis=-1, keepdims=True), axis=0, keepdims=True)
            hi = jnp.max(jnp.sum(jnp.where(lo_mask, 0.0, k2), axis=-1, keepdims=True), axis=0, keepdims=True)
            return jnp.maximum(kmax2, jnp.where(lo_mask, lo, hi))
        kst[0:1, :] = lax.fori_loop(0, seq // tk, body, jnp.zeros((1, DA_DV), F32))
        b0 = bias_ref[0, 0]
        b1 = bias_ref[0, 1]
        bmax = jnp.maximum(jnp.max(jnp.maximum(b0, b1), axis=0, keepdims=True), 0.0)
        bmin = jnp.minimum(jnp.min(jnp.minimum(jnp.where(b0 > 0.5 * NEG, b0, 0.0), b1), axis=0, keepdims=True), 0.0)
        kst[1:2, :] = jnp.broadcast_to(jnp.max(bmax, axis=1, keepdims=True), (1, DA_DV))
        kst[2:3, :] = jnp.broadcast_to(jnp.min(bmin, axis=1, keepdims=True), (1, DA_DV))

    def prepare(src_ref, s):
        q = group_norm(src_ref[...].astype(F32), qg_ref[...]) * (DA_DH ** -0.5 * LOG2E)
        qc_s[s] = jnp.concatenate([jnp.where(lo_mask, q, 0.0), jnp.where(lo_mask, 0.0, q)], axis=0).astype(BF16)
        q2 = q * q * kst[0:1, :]
        ones8 = jnp.ones((8, DA_DV), BF16)
        bound = []
        for m in range(2):
            q2m = jnp.where(lo_mask, q2, 0.0) if m == 0 else jnp.where(lo_mask, 0.0, q2)
            bound.append(jnp.sqrt(_dot_nt(ones8, q2m.astype(BF16))[0:1, :]) * BOUND_SLACK)
        bound = jnp.concatenate(bound, axis=1)
        bd_s[s] = bound
        worst = jnp.max(2.0 * bound, axis=1, keepdims=True) + kst[1:2, 0:1] - kst[2:3, 0:1]
        flag_s[s] = jnp.where(worst[0, 0] <= MAX_SHIFT_GAP, 1, 0).astype(I32)

    @pl.when(qi == 0)
    def _():
        prepare(q_ref, slot)

    qcat = qc_s[slot]
    bound = bd_s[slot]
    bmax = kst[1:2, 0:1]
    safe = flag_s[slot] == 1
    acc_s[...] = jnp.zeros_like(acc_s)

    def block(j, d):
        r0 = pl.multiple_of(j * tk, tk)
        st = _dot_nt(kn[pl.ds(r0, tk), :], qcat)
        if d is not None:
            bias = bias_ref[0, d]
            st = st + jnp.concatenate([bias, bias], axis=1)
        m_prev = m_s[...]
        m_new = jnp.maximum(m_prev, jnp.max(st, axis=0, keepdims=True))
        alpha = jnp.exp2(m_prev - m_new)
        acc_s[...] = alpha * acc_s[...] + _dot(vt[j], jnp.exp2(st - m_new).astype(BF16))
        m_s[...] = m_new

    def blocks_fixed(js, ds=None):
        sts = [_dot_nt(kn[pl.ds(pl.multiple_of(j * tk, tk), tk), :], qcat) for j in js]
        if ds is not None:
            biases = [bias_ref[0, d] for d in ds]
            sts = [st + jnp.concatenate([b, b], axis=1) for st, b in zip(sts, biases)]
        shift = m_s[...]
        pts = [jnp.exp2(st - shift).astype(BF16) for st in sts]
        tot = _dot(vt[js[0]], pts[0])
        for j, pt in zip(js[1:], pts[1:]):
            tot = tot + _dot(vt[j], pt)
        acc_s[...] = acc_s[...] + tot

    def run(fixed_shift):
        n_far = jnp.maximum(qi - 1, 0)

        if fixed_shift:
            def far_quad(jj, carry):
                blocks_fixed([4 * jj + u for u in range(4)])
                return carry

            lax.fori_loop(0, n_far // 4, far_quad, 0)
            rem = n_far % 4

            @pl.when(rem >= 2)
            def _():
                blocks_fixed([n_far - rem, n_far - rem + 1])

            @pl.when(rem % 2 == 1)
            def _():
                blocks_fixed([n_far - 1])

            @pl.when(qi >= 1)
            def _():
                blocks_fixed([qi - 1, qi], [1, 0])
                prepare(qn_ref, 1 - slot)

            @pl.when(qi == 0)
            def _():
                blocks_fixed([qi], [0])
                prepare(qn_ref, 1 - slot)
        else:
            def far_one(j, carry):
                block(j, None)
                return carry

            lax.fori_loop(0, n_far, far_one, 0)

            @pl.when(qi >= 1)
            def _():
                block(qi - 1, 1)

            block(qi, 0)
            prepare(qn_ref, 1 - slot)

    @pl.when(safe)
    def _():
        m_s[...] = bound + bmax
        run(True)

    @pl.when(jnp.logical_not(safe))
    def _():
        m_s[...] = jnp.full(m_s.shape, NEG, F32)
        run(False)

    lam_p = lam_ref[...]
    lam = (jnp.exp(jnp.sum(lam_p[0:1, :] * lam_p[1:2, :], axis=-1, keepdims=True))
           - jnp.exp(jnp.sum(lam_p[2:3, :] * lam_p[3:4, :], axis=-1, keepdims=True)) + LAM_INIT)
    a0 = acc_s[:, 0:tq]
    a1 = acc_s[:, tq:2 * tq]
    ot = a0[0:DA_DV] / a0[DA_DV:DA_DV + 1] - lam * (a1[0:DA_DV] / a1[DA_DV:DA_DV + 1])
    r = lax.rsqrt(jnp.sum(ot * ot, axis=0, keepdims=True) * (1.0 / DA_DV) + EPS)
    ot = ot * r * (sg_ref[...] * (1.0 - LAM_INIT))
    o_ref[...] = ot.T.astype(o_ref.dtype)


def _attention(p, bias, qg, kg, lam_p, subln, batch, seq):
    t = batch * seq
    tq = min(ATT_BLOCK, seq)
    nq = seq // tq
    dv = DA_DV
    return pl.pallas_call(
        _attn_body,
        grid=(batch, DA_HEADS, nq),
        in_specs=[
            pl.BlockSpec((tq, dv), lambda b, h, i: (b * nq + i, P_DAQ // dv + h)),
            pl.BlockSpec((tq, dv), lambda b, h, i: (b * nq + jnp.minimum(i + 1, nq - 1), P_DAQ // dv + h)),
            pl.BlockSpec((seq, dv), lambda b, h, i: (b, P_DAK // dv + h)),
            pl.BlockSpec((seq, dv), lambda b, h, i: (b, P_DAV // dv + h)),
            pl.BlockSpec((1, 2, tq, tq), lambda b, h, i: (h, 0, 0, 0)),
            pl.BlockSpec((1, dv), lambda b, h, i: (0, 0)),
            pl.BlockSpec((1, dv), lambda b, h, i: (0, 0)),
            pl.BlockSpec((4, DA_DH), lambda b, h, i: (0, 0)),
            pl.BlockSpec((dv, 1), lambda b, h, i: (0, 0)),
        ],
        out_specs=pl.BlockSpec((tq, dv), lambda b, h, i: (b * nq + i, h)),
        out_shape=jax.ShapeDtypeStruct((t, DA_HEADS * dv), BF16),
        scratch_shapes=[
            pltpu.VMEM((seq, dv), BF16),
            pltpu.VMEM((seq // tq, DA_VROWS, tq), BF16),
            pltpu.VMEM((8, dv), F32),
            pltpu.VMEM((2, 2 * tq, dv), BF16),
            pltpu.VMEM((2, 1, 2 * tq), F32),
            pltpu.SMEM((2,), I32),
            pltpu.VMEM((1, 2 * tq), F32),
            pltpu.VMEM((DA_VROWS, 2 * tq), F32),
        ],
        compiler_params=pltpu.CompilerParams(dimension_semantics=("parallel", "parallel", "arbitrary")),
        name="diff_attention",
    )(p, p, p, p, bias, qg, kg, lam_p, subln)


def _memkv_body(mem_ref, mg_ref, w_ref, kg_ref, mk_ref, mv_ref):
    x = mem_ref[0]
    xn = x * _rms(x, D_MODEL) * mg_ref[...]
    kv = _dot(xn.astype(BF16), w_ref[...])
    hw = MX_HEADS * MX_DH
    for h in range(MX_HEADS):
        sl = slice(h * MX_DH, (h + 1) * MX_DH)
        kh = kv[:, sl]
        mk_ref[0, :, sl] = (kh * _rms(kh, MX_DH) * kg_ref[...]).astype(BF16)
    mv_ref[0] = kv[:, hw:].astype(BF16)


def _memkv(mem, mem_norm, w_kv, k_norm):
    b, n, _ = mem.shape
    hw = MX_HEADS * MX_DH
    return pl.pallas_call(
        _memkv_body,
        grid=(b,),
        in_specs=[
            pl.BlockSpec((1, n, D_MODEL), lambda i: (i, 0, 0)),
            pl.BlockSpec((1, D_MODEL), lambda i: (0, 0)),
            pl.BlockSpec((D_MODEL, 2 * hw), lambda i: (0, 0)),
            pl.BlockSpec((1, MX_DH), lambda i: (0, 0)),
        ],
        out_specs=[pl.BlockSpec((1, n, hw), lambda i: (i, 0, 0))] * 2,
        out_shape=[jax.ShapeDtypeStruct((b, n, hw), BF16)] * 2,
        name="memory_kv",
    )(mem, mem_norm, w_kv, k_norm)


def _merge_body(x_ref, odn_ref, oda_ref, mxq_ref, g0_ref, g1_ref, g2_ref, mk_ref, mv_ref, qg_ref,
                bg_ref, wb_ref, wo_ref, out_ref, omx):
    for h in range(MX_HEADS):
        sl = slice(h * MX_DH, (h + 1) * MX_DH)
        qh = mxq_ref[:, sl].astype(F32)
        qh = qh * _rms(qh, MX_DH) * qg_ref[...] * (MX_DH ** -0.5 * LOG2E)
        s = _dot_nt(qh.astype(BF16), mk_ref[0, :, sl])
        p = jnp.exp2(s - jnp.max(s, axis=-1, keepdims=True))
        oh = _dot(p.astype(BF16), mv_ref[0, :, sl]) / jnp.sum(p, axis=-1, keepdims=True)
        omx[:, sl] = oh.astype(BF16)
    y = None
    for r, (o_r, g_r) in enumerate(((odn_ref, g0_ref), (oda_ref, g1_ref), (omx, g2_ref))):
        gate = _sigmoid(g_r[...].astype(F32) + bg_ref[r:r + 1, :])
        term = gate * _dot(o_r[...], wb_ref[r])
        y = term if y is None else y + term
    out_ref[...] = x_ref[...] + _dot(y.astype(BF16), wo_ref[...])


def _merge(x2, o_dn, o_da, p, mk, mv, q_norm, b_gate, w_branch, w_out, seq):
    t = x2.shape[0]
    tm = min(512, seq)
    nt = seq // tm
    bw = 512
    n_mem = mk.shape[1]
    return pl.pallas_call(
        _merge_body,
        grid=(t // tm,),
        in_specs=[
            pl.BlockSpec((tm, D_MODEL), lambda i: (i, 0)),
            pl.BlockSpec((tm, bw), lambda i: (i, 0)),
            pl.BlockSpec((tm, bw), lambda i: (i, 0)),
            pl.BlockSpec((tm, bw), lambda i: (i, P_MXQ // bw)),
            pl.BlockSpec((tm, D_MODEL), lambda i: (i, P_GATE // D_MODEL)),
            pl.BlockSpec((tm, D_MODEL), lambda i: (i, P_GATE // D_MODEL + 1)),
            pl.BlockSpec((tm, D_MODEL), lambda i: (i, P_GATE // D_MODEL + 2)),
            pl.BlockSpec((1, n_mem, bw), lambda i: (i // nt, 0, 0)),
            pl.BlockSpec((1, n_mem, bw), lambda i: (i // nt, 0, 0)),
            pl.BlockSpec((1, MX_DH), lambda i: (0, 0)),
            pl.BlockSpec((3, D_MODEL), lambda i: (0, 0)),
            pl.BlockSpec((3, bw, D_MODEL), lambda i: (0, 0, 0)),
            pl.BlockSpec((D_MODEL, D_MODEL), lambda i: (0, 0)),
        ],
        out_specs=pl.BlockSpec((tm, D_MODEL), lambda i: (i, 0)),
        out_shape=jax.ShapeDtypeStruct((t, D_MODEL), F32),
        scratch_shapes=[pltpu.VMEM((tm, bw), BF16)],
        compiler_params=pltpu.CompilerParams(dimension_semantics=("parallel",)),
        name="merge",
    )(x2, o_dn, o_da, p, p, p, p, mk, mv, q_norm, b_gate, w_branch, w_out)


def _router_body(x_ref, g_ref, wr_ref, br_ref, h_ref, idx_ref, wt_ref, rank_ref, cnt_ref):
    tm = x_ref.shape[0]
    x = x_ref[...]
    h = x * _rms(x, D_MODEL) * g_ref[...]
    h_ref[...] = h.astype(BF16)
    logits = _dot3(h, wr_ref[...]) + br_ref[...]
    lane = lax.broadcasted_iota(I32, (tm, LANES), 1)
    lane_f = lane.astype(F32)
    work = logits
    sel = jnp.zeros((tm, LANES), F32)
    vals, idxs = [], []
    for _ in range(TOP_K):
        mx = jnp.max(work, axis=-1, keepdims=True)
        ik = jnp.min(jnp.where(work == mx, lane_f, float(LANES)), axis=-1, keepdims=True)
        hit = lane_f == ik
        sel = jnp.where(hit, 1.0, sel)
        work = jnp.where(hit, -jnp.inf, work)
        vals.append(mx)
        idxs.append(ik)
    es = [jnp.exp(v - vals[0]) for v in vals]
    den = es[0] + es[1] + es[2] + es[3]
    sub = cnt_ref.shape[0]
    mt = tm // sub
    r = lax.broadcasted_iota(I32, (tm, tm), 0)
    c = lax.broadcasted_iota(I32, (tm, tm), 1)
    tril = jnp.where((r > c) & (r // mt == c // mt), 1.0, 0.0).astype(BF16)
    cum = _dot(tril, sel.astype(BF16))
    idx_o = jnp.zeros((tm, LANES), F32)
    wt_o = jnp.zeros((tm, LANES), F32)
    rank_o = jnp.zeros((tm, LANES), F32)
    for k in range(TOP_K):
        rk = jnp.sum(jnp.where(lane_f == idxs[k], cum, 0.0), axis=-1, keepdims=True)
        idx_o = jnp.where(lane == k, idxs[k], idx_o)
        wt_o = jnp.where(lane == k, es[k] / den, wt_o)
        rank_o = jnp.where(lane == k, rk, rank_o)
    idx_ref[...] = idx_o.astype(I32)
    wt_ref[...] = wt_o
    rank_ref[...] = rank_o.astype(I32)
    for s in range(sub):
        cnt_ref[s] = jnp.broadcast_to(jnp.sum(sel[s * mt:(s + 1) * mt], axis=0, keepdims=True), (8, LANES))


def _router(x1, gain, w_r, b_r):
    t = x1.shape[0]
    mt = min(MOE_TILE, t)
    sub = 2 if t % (2 * mt) == 0 else 1
    tm = sub * mt
    row = lambda i: (i, 0)
    fixed = lambda i: (0, 0)
    return pl.pallas_call(
        _router_body,
        grid=(t // tm,),
        in_specs=[
            pl.BlockSpec((tm, D_MODEL), row),
            pl.BlockSpec((1, D_MODEL), fixed),
            pl.BlockSpec((D_MODEL, LANES), fixed),
            pl.BlockSpec((1, LANES), fixed),
        ],
        out_specs=[
            pl.BlockSpec((tm, D_MODEL), row),
            pl.BlockSpec((tm, LANES), row),
            pl.BlockSpec((tm, LANES), row),
            pl.BlockSpec((tm, LANES), row),
            pl.BlockSpec((sub, 8, LANES), lambda i: (i, 0, 0)),
        ],
        out_shape=[
            jax.ShapeDtypeStruct((t, D_MODEL), BF16),
            jax.ShapeDtypeStruct((t, LANES), I32),
            jax.ShapeDtypeStruct((t, LANES), F32),
            jax.ShapeDtypeStruct((t, LANES), I32),
            jax.ShapeDtypeStruct((t // mt, 8, LANES), F32),
        ],
        compiler_params=pltpu.CompilerParams(dimension_semantics=("arbitrary",)),
        name="router",
    )(x1, gain, w_r, b_r)


def _lane_cumsum(x):
    lane = lax.broadcasted_iota(I32, x.shape, 1)
    s = 1
    while s < N_EXPERTS:
        x = x + jnp.where(lane >= s, pltpu.roll(x, s, axis=1), 0.0)
        s *= 2
    return x


def _plan_body(cnt_ref, gs_ref, ls_ref, n8_ref, lsf_ref, tail_ref, meta_ref):
    nt = cnt_ref.shape[0]
    ga = float(GROUP_ALIGN)
    eb = float(EXPERT_BLOCK)
    lane = lax.broadcasted_iota(I32, (nt, LANES), 1)
    r8 = jnp.where(lane < N_EXPERTS, jnp.floor((cnt_ref[...] + (ga - 1.0)) * (1.0 / ga)) * ga, 0.0)
    ri = lax.broadcasted_iota(I32, (nt, nt), 0)
    ci = lax.broadcasted_iota(I32, (nt, nt), 1)
    before = _dot(jnp.where(ri > ci, 1.0, 0.0).astype(BF16), r8.astype(BF16))
    tot = jnp.sum(r8, axis=0, keepdims=True)
    region = jnp.floor((tot + (eb - 1.0)) * (1.0 / eb)) * eb
    pends = _lane_cumsum(jnp.broadcast_to(region, (8, LANES)))[0:1, :]
    pstart = pends - region
    lstart = _lane_cumsum(r8) - r8
    gs_ref[...] = (pstart + before).astype(I32)
    ls_ref[...] = lstart.astype(I32)
    pieces = r8 * (1.0 / ga)
    n8_ref[...] = jnp.where(lane == TOTAL_LANE, jnp.sum(pieces, axis=1, keepdims=True), pieces).astype(I32)
    lsf_ref[...] = lstart
    row8 = lax.broadcasted_iota(I32, (8, LANES), 0)
    tail_n = (region - tot) * (1.0 / ga)
    lane8 = lax.broadcasted_iota(I32, (8, LANES), 1)
    tail_n = jnp.where(lane8 == TOTAL_LANE, jnp.sum(tail_n, axis=1, keepdims=True), tail_n)
    tail = jnp.where(row8 == 0, pstart + tot, jnp.where(row8 == 1, tail_n, 0.0))
    tail_ref[...] = tail.astype(I32)
    nb = meta_ref.shape[0]
    ln = lax.broadcasted_iota(I32, (nb, LANES), 1)
    blk = lax.broadcasted_iota(I32, (nb, LANES), 0).astype(F32) * eb
    be = jnp.sum(jnp.where((ln < N_EXPERTS) & (pends <= blk), 1.0, 0.0), axis=-1, keepdims=True)
    be = jnp.minimum(be, float(N_EXPERTS - 1))
    used = jnp.sum(jnp.where(ln == N_EXPERTS - 1, pends, 0.0), axis=-1, keepdims=True) * (1.0 / eb)
    meta_ref[...] = jnp.where(ln == 0, be, jnp.where(ln == 1, used, 0.0)).astype(I32)


def _plan(cnt, n_blocks_pad):
    nt = cnt.shape[0]
    shp = jax.ShapeDtypeStruct((nt, LANES), I32)
    return pl.pallas_call(
        _plan_body,
        out_shape=[shp, shp, shp, jax.ShapeDtypeStruct((nt, LANES), F32),
                   jax.ShapeDtypeStruct((8, LANES), I32),
                   jax.ShapeDtypeStruct((n_blocks_pad, LANES), I32)],
        name="dispatch_plan",
    )(cnt)


def _local_positions(idx_ref, rank_ref, lsf_ref):
    tm = idx_ref.shape[0]
    lane = lax.broadcasted_iota(I32, (tm, LANES), 1)
    idx = idx_ref[...]
    rank = rank_ref[...].astype(F32)
    ls_row = lsf_ref[0]
    out = []
    for k in range(TOP_K):
        base = jnp.sum(jnp.where(lane == idx[:, k:k + 1], ls_row, 0.0), axis=-1, keepdims=True)
        out.append((base + rank[:, k:k + 1]).astype(I32))
    return out


def _local_positions_lanes(idx_ref, rank_ref, lsf_ref):
    tm = idx_ref.shape[0]
    idx_t = idx_ref[...].astype(F32).T
    rank_t = rank_ref[...].astype(F32).T
    ls_col = jnp.broadcast_to(lsf_ref[0], (LANES, LANES)).T[:, 0:1]
    expert = lax.broadcasted_iota(I32, (LANES, tm), 0).astype(F32)
    out = []
    for k in range(TOP_K):
        base = jnp.sum(jnp.where(expert == idx_t[k:k + 1, :], ls_col, 0.0), axis=0, keepdims=True)
        out.append((base + rank_t[k:k + 1, :]).astype(I32))
    return out


PACK_W = D_MODEL // 2
U32 = jnp.uint32


def _pack_rows(x):
    xb = x.astype(BF16).astype(F32)
    hi = lax.bitcast_convert_type(xb[:, :PACK_W], U32)
    lo = lax.bitcast_convert_type(xb[:, PACK_W:], U32)
    return hi | (lo >> 16)


def _unpack_rows(w):
    hi = lax.bitcast_convert_type(w & jnp.uint32(0xFFFF0000), F32)
    lo = lax.bitcast_convert_type(w << 16, F32)
    return hi.astype(BF16), lo.astype(BF16)


TOTAL_LANE = LANES - 1
WAIT_CHUNK = 128
BIG_PIECE = 4


def _start_group_copies(tables, tile, make_copy):
    gs_ref, ls_ref, n8_ref = tables

    def group(e, carry):
        g0 = gs_ref[tile, e]
        l0 = ls_ref[tile, e]
        n = n8_ref[tile, e]
        nbig = n // BIG_PIECE

        def big(j, c):
            off = j * (BIG_PIECE * GROUP_ALIGN)
            make_copy(pl.multiple_of(l0 + off, GROUP_ALIGN), pl.multiple_of(g0 + off, GROUP_ALIGN),
                      BIG_PIECE * GROUP_ALIGN).start()
            return c

        def one(j, c):
            off = j * GROUP_ALIGN
            make_copy(pl.multiple_of(l0 + off, GROUP_ALIGN), pl.multiple_of(g0 + off, GROUP_ALIGN),
                      GROUP_ALIGN).start()
            return c

        lax.fori_loop(0, nbig, big, 0)
        lax.fori_loop(nbig * BIG_PIECE, n, one, 0)
        return carry

    lax.fori_loop(0, N_EXPERTS, group, 0)


def _wait_pieces(n, make_wait):
    def chunk(j, c):
        make_wait(WAIT_CHUNK).wait()
        return c

    lax.fori_loop(0, n // WAIT_CHUNK, chunk, 0)
    b = WAIT_CHUNK // 2
    while b >= 1:
        def _(b=b):
            make_wait(b).wait()
        pl.when((n & b) != 0)(_)
        b //= 2


def _dispatch_body(gs_ref, ls_ref, n8_ref, tail_ref, h_ref, idx_ref, rank_ref, lsf_ref, xs_ref,
                   xl, zbuf, sem):
    i = pl.program_id(0)
    last = i == pl.num_programs(0) - 1
    slot = i % 2
    tables = (gs_ref, ls_ref, n8_ref)
    tm = h_ref.shape[0]
    lrows = xl.shape[1]
    pos = lax.broadcasted_iota(I32, (lrows, tm), 0)
    lp = _local_positions_lanes(idx_ref, rank_ref, lsf_ref)
    hit = pos == lp[0]
    for k in range(1, TOP_K):
        hit = hit | (pos == lp[k])
    xl[slot] = _pack_rows(_dot(jnp.where(hit, 1.0, 0.0).astype(BF16), h_ref[...]))

    def copy_from(s):
        def make_copy(l, g, rows):
            return pltpu.make_async_copy(xl.at[s, pl.ds(l, rows), :], xs_ref.at[pl.ds(g, rows), :], sem.at[s])
        return make_copy

    def wait_on(s):
        def make_wait(pieces):
            rows = pieces * GROUP_ALIGN
            return pltpu.make_async_copy(xl.at[0, pl.ds(0, rows), :], xs_ref.at[pl.ds(0, rows), :], sem.at[s])
        return make_wait

    @pl.when(i >= 1)
    def _():
        _wait_pieces(n8_ref[i - 1, TOTAL_LANE], wait_on(1 - slot))

    _start_group_copies(tables, i, copy_from(slot))

    def start_tails():
        def per_expert(e, carry):
            g0 = tail_ref[0, e]

            def one(j, c):
                pltpu.make_async_copy(
                    zbuf, xs_ref.at[pl.ds(pl.multiple_of(g0 + j * GROUP_ALIGN, GROUP_ALIGN), GROUP_ALIGN), :],
                    sem.at[2]).start()
                return c

            lax.fori_loop(0, tail_ref[1, e], one, 0)
            return carry

        lax.fori_loop(0, N_EXPERTS, per_expert, 0)

    @pl.when(last)
    def _():
        zbuf[...] = jnp.zeros_like(zbuf)
        start_tails()
        _wait_pieces(n8_ref[i, TOTAL_LANE], wait_on(slot))
        _wait_pieces(tail_ref[1, TOTAL_LANE], wait_on(2))


def _dispatch(gs, ls, n8, tail, h2, idx, rank, lsf, n_slots):
    t = h2.shape[0]
    tm = min(MOE_TILE, t)
    nt = t // tm
    lrows = tm * TOP_K + N_EXPERTS * GROUP_ALIGN
    row = lambda i, *_: (i, 0)
    grid_spec = pltpu.PrefetchScalarGridSpec(
        num_scalar_prefetch=4,
        grid=(nt,),
        in_specs=[
            pl.BlockSpec((tm, D_MODEL), row),
            pl.BlockSpec((tm, LANES), row),
            pl.BlockSpec((tm, LANES), row),
            pl.BlockSpec((1, 1, LANES), lambda i, *_: (i, 0, 0)),
        ],
        out_specs=pl.BlockSpec(memory_space=pl.ANY),
        scratch_shapes=[pltpu.VMEM((2, lrows, PACK_W), U32), pltpu.VMEM((GROUP_ALIGN, PACK_W), U32),
                        pltpu.SemaphoreType.DMA((3,))],
    )
    return pl.pallas_call(
        _dispatch_body,
        grid_spec=grid_spec,
        out_shape=jax.ShapeDtypeStruct((n_slots, PACK_W), U32),
        compiler_params=pltpu.CompilerParams(dimension_semantics=("arbitrary",)),
        name="moe_dispatch",
    )(gs, ls, n8, tail, h2, idx, rank, lsf)


def _expert_body(be_ref, nu_ref, x_ref, wgu_ref, bgu_ref, wd_ref, bd_ref, y_ref, wgu_b, wd_b):
    j = pl.program_id(0)

    @pl.when(j < nu_ref[0])
    def _():
        @pl.when(jnp.logical_or(j == 0, be_ref[j] != be_ref[jnp.maximum(j - 1, 0)]))
        def _():
            wgu_b[...] = wgu_ref[0].astype(BF16)
            wd_b[...] = wd_ref[0].astype(BF16)

        gu = _dot(jnp.concatenate(_unpack_rows(x_ref[...]), axis=1), wgu_b[...]) + bgu_ref[0]
        gate = jnp.minimum(gu[:, :D_FF], SWIGLU_LIMIT)
        up = jnp.clip(gu[:, D_FF:], -SWIGLU_LIMIT, SWIGLU_LIMIT)
        act = (up + 1.0) * gate * _sigmoid(SWIGLU_ALPHA * gate)
        y_ref[...] = _pack_rows(_dot(act.astype(BF16), wd_b[...]) + bd_ref[0])


def _experts(block_e, n_used, xs, wgu, bgu, wd, bd):
    n_slots = xs.shape[0]
    nb = n_slots // EXPERT_BLOCK

    def blk(j, be, nu):
        return jnp.minimum(j, nu[0] - 1)

    grid_spec = pltpu.PrefetchScalarGridSpec(
        num_scalar_prefetch=2,
        grid=(nb,),
        in_specs=[
            pl.BlockSpec((EXPERT_BLOCK, PACK_W), lambda j, be, nu: (blk(j, be, nu), 0)),
            pl.BlockSpec((1, D_MODEL, 2 * D_FF), lambda j, be, nu: (be[blk(j, be, nu)], 0, 0)),
            pl.BlockSpec((1, 1, 2 * D_FF), lambda j, be, nu: (be[blk(j, be, nu)], 0, 0)),
            pl.BlockSpec((1, D_FF, D_MODEL), lambda j, be, nu: (be[blk(j, be, nu)], 0, 0)),
            pl.BlockSpec((1, 1, D_MODEL), lambda j, be, nu: (be[blk(j, be, nu)], 0, 0)),
        ],
        out_specs=pl.BlockSpec((EXPERT_BLOCK, PACK_W), lambda j, be, nu: (blk(j, be, nu), 0)),
        scratch_shapes=[pltpu.VMEM((D_MODEL, 2 * D_FF), BF16), pltpu.VMEM((D_FF, D_MODEL), BF16)],
    )
    return pl.pallas_call(
        _expert_body,
        grid_spec=grid_spec,
        out_shape=jax.ShapeDtypeStruct((n_slots, PACK_W), U32),
        compiler_params=pltpu.CompilerParams(dimension_semantics=("arbitrary",)),
        name="moe_experts",
    )(block_e, n_used, xs, wgu, bgu, wd, bd)


def _combine_body(gs_ref, ls_ref, n8_ref, x_ref, wt_ref, idx_ref, rank_ref, lsf_ref, y_ref, out_ref,
                  yl, sem):
    i = pl.program_id(0)
    slot = i % 2
    tables = (gs_ref, ls_ref, n8_ref)
    tm = x_ref.shape[0]
    lrows = yl.shape[1]

    def copy_into(s):
        def make_copy(l, g, rows):
            return pltpu.make_async_copy(y_ref.at[pl.ds(g, rows), :], yl.at[s, pl.ds(l, rows), :], sem.at[s])
        return make_copy

    def make_wait(pieces):
        rows = pieces * GROUP_ALIGN
        return pltpu.make_async_copy(y_ref.at[pl.ds(0, rows), :], yl.at[0, pl.ds(0, rows), :], sem.at[slot])

    @pl.when(i == 0)
    def _():
        yl[...] = jnp.zeros_like(yl)
        _start_group_copies(tables, i, copy_into(slot))

    @pl.when(i + 1 < pl.num_programs(0))
    def _():
        _start_group_copies(tables, i + 1, copy_into(1 - slot))

    pos = lax.broadcasted_iota(I32, (tm, lrows), 1)
    lp = _local_positions(idx_ref, rank_ref, lsf_ref)
    wt = wt_ref[...]
    wm = jnp.zeros((tm, lrows), F32)
    for k in range(TOP_K):
        wm = jnp.where(pos == lp[k], wt[:, k:k + 1], wm)
    _wait_pieces(n8_ref[i, TOTAL_LANE], make_wait)
    wmb = wm.astype(BF16)
    y_hi, y_lo = _unpack_rows(yl[slot])
    out_ref[:, :PACK_W] = x_ref[:, :PACK_W] + _dot(wmb, y_hi)
    out_ref[:, PACK_W:] = x_ref[:, PACK_W:] + _dot(wmb, y_lo)


def _combine(gs, ls, n8, x1, wts, idx, rank, lsf, y):
    t = x1.shape[0]
    tm = min(MOE_TILE, t)
    nt = t // tm
    lrows = tm * TOP_K + N_EXPERTS * GROUP_ALIGN
    row = lambda i, *_: (i, 0)
    grid_spec = pltpu.PrefetchScalarGridSpec(
        num_scalar_prefetch=3,
        grid=(nt,),
        in_specs=[
            pl.BlockSpec((tm, D_MODEL), row),
            pl.BlockSpec((tm, LANES), row),
            pl.BlockSpec((tm, LANES), row),
            pl.BlockSpec((tm, LANES), row),
            pl.BlockSpec((1, 1, LANES), lambda i, *_: (i, 0, 0)),
            pl.BlockSpec(memory_space=pl.ANY),
        ],
        out_specs=pl.BlockSpec((tm, D_MODEL), row),
        scratch_shapes=[pltpu.VMEM((2, lrows, PACK_W), U32), pltpu.SemaphoreType.DMA((2,))],
    )
    return pl.pallas_call(
        _combine_body,
        grid_spec=grid_spec,
        out_shape=jax.ShapeDtypeStruct((t, D_MODEL), F32),
        compiler_params=pltpu.CompilerParams(dimension_semantics=("arbitrary",)),
        name="moe_combine",
    )(gs, ls, n8, x1, wts, idx, rank, lsf, y)


def _pad_lanes(v, fill=0.0):
    v = v.astype(F32).reshape(1, -1)
    return jnp.pad(v, ((0, 0), (0, LANES - v.shape[1])), constant_values=fill)


def _mixer(x2, mem, rel_table, attn_norm, w_in, b_gate, dn_conv, dn_a_log, dn_dt_bias, dn_out_norm,
           da_q_norm, da_k_norm, da_lambda, da_subln, mem_norm, w_mem_kv, mx_q_norm, mx_k_norm,
           w_branch, w_out, batch, seq):
    wp = jnp.concatenate([w_in[:, :W_AB_LO], w_in[:, W_AB_HI:]], axis=1).astype(BF16)
    wab = jnp.pad(w_in[:, W_AB_LO:W_AB_HI], ((0, 0), (0, LANES - (W_AB_HI - W_AB_LO))))
    p, ab = _inproj(x2, attn_norm.reshape(1, -1), wp, wab)

    o_dn = _deltanet(p, ab, dn_conv, _pad_lanes(dn_a_log), _pad_lanes(dn_dt_bias),
                     dn_out_norm.reshape(1, -1), batch, seq)

    tq = min(ATT_BLOCK, seq)
    bias = _bias_tiles(rel_table.T, tq)
    o_da = _attention(p, bias, jnp.tile(da_q_norm, 2).reshape(1, -1), jnp.tile(da_k_norm, 2).reshape(1, -1),
                      da_lambda, da_subln.reshape(-1, 1), batch, seq)

    mk, mv = _memkv(mem, mem_norm.reshape(1, -1), w_mem_kv.astype(BF16), mx_k_norm.reshape(1, -1))
    return _merge(x2, o_dn, o_da, p, mk, mv, mx_q_norm.reshape(1, -1), b_gate.reshape(3, D_MODEL),
                  w_branch.astype(BF16), w_out.astype(BF16), seq)


def _moe(x1, ffn_norm, w_router, b_router, w_gate_up, b_gate_up, w_down, b_down):
    t = x1.shape[0]
    nt = t // min(MOE_TILE, t)
    max_rows = t * TOP_K + nt * N_EXPERTS * (GROUP_ALIGN - 1)
    n_blocks = -(-max_rows // EXPERT_BLOCK) + N_EXPERTS
    n_blocks_pad = -(-n_blocks // 8) * 8
    n_slots = n_blocks * EXPERT_BLOCK

    wr = jnp.pad(w_router, ((0, 0), (0, LANES - N_EXPERTS)))
    h2, idx, wts, rank, cnt = _router(x1, ffn_norm.reshape(1, -1), wr, _pad_lanes(b_router, NEG))
    gs, ls, n8, lsf, tail, meta = _plan(cnt[:, 0, :], n_blocks_pad)
    block_e = meta[:n_blocks, 0]
    n_used = meta[0:1, 1]
    lsf = lsf.reshape(nt, 1, LANES)

    xs = _dispatch(gs, ls, n8, tail, h2, idx, rank, lsf, n_slots)
    y = _experts(block_e, n_used, xs, w_gate_up, b_gate_up.reshape(N_EXPERTS, 1, -1),
                 w_down, b_down.reshape(N_EXPERTS, 1, -1))
    return _combine(gs, ls, n8, x1, wts, idx, rank, lsf, y)


def kernel(x, mem, rel_table, attn_norm, w_in, b_gate, dn_conv, dn_a_log, dn_dt_bias, dn_out_norm,
           da_q_norm, da_k_norm, da_lambda, da_subln, mem_norm, w_mem_kv, mx_q_norm, mx_k_norm,
           w_branch, w_out, ffn_norm, w_router, b_router, w_gate_up, b_gate_up, w_down, b_down):
    batch, seq, d = x.shape
    x2 = x.reshape(batch * seq, d)
    x1 = _mixer(x2, mem, rel_table, attn_norm[0], w_in[0], b_gate[0], dn_conv[0], dn_a_log[0],
                dn_dt_bias[0], dn_out_norm[0], da_q_norm[0], da_k_norm[0], da_lambda[0], da_subln[0],
                mem_norm[0], w_mem_kv[0], mx_q_norm[0], mx_k_norm[0], w_branch[0], w_out[0], batch, seq)
    out = _moe(x1, ffn_norm[0], w_router[0], b_router[0], w_gate_up[0], b_gate_up[0], w_down[0],
               b_down[0])
    return out.reshape(batch, seq, d)
```

```python
import functools
import math

import jax
import jax.numpy as jnp
from jax import lax
from jax.experimental import pallas as pl
from jax.experimental.pallas import tpu as pltpu

F32 = jnp.float32
BF16 = jnp.bfloat16
I32 = jnp.int32

D_MODEL = 1024
EPS = 1e-6
LANES = 128

DN_HEADS = 4
DN_DK = 128
DN_CHUNK = 64
DN_CONV = 4

DA_HEADS = 4
DA_DH = 64

MX_HEADS = 4
MX_DH = 128

REL_BUCKETS = 32
REL_MAX_DIST = 128

N_EXPERTS = 32
TOP_K = 4
D_FF = 1024
SWIGLU_LIMIT = 7.0
SWIGLU_ALPHA = 1.702
EXPERT_BLOCK = 512
MOE_TILE = 256
GROUP_ALIGN = 8

LAM_INIT = 0.8 - 0.6 * math.exp(-0.3 * 0)
LOG2E = 1.4426950408889634
NEG = -1e30

P_DNQ, P_DNK, P_DNV, P_DNZ = 0, 512, 1024, 1536
P_DAQ, P_DAK, P_DAV = 2048, 2560, 3072
P_MXQ = 3584
P_GATE = 4096
P_COLS = 7168
W_AB_LO, W_AB_HI = 2048, 2056


def _dot(a, b):
    return jnp.dot(a, b, preferred_element_type=F32)


def _dot_nt(a, b):
    return lax.dot_general(a, b, (((1,), (1,)), ((), ())), preferred_element_type=F32)


def _dot_tn(a, b):
    return lax.dot_general(a, b, (((0,), (0,)), ((), ())), preferred_element_type=F32)


def _split(x):
    hi = x.astype(BF16)
    lo = (x - hi.astype(F32)).astype(BF16)
    return hi, lo


def _dot3(a, b):
    ah, al = _split(a)
    bh, bl = _split(b)
    return _dot(ah, bh) + _dot(ah, bl) + _dot(al, bh)


def _sigmoid(x):
    return 1.0 / (1.0 + jnp.exp(-x))


def _rms(x, n):
    return lax.rsqrt(jnp.sum(x * x, axis=-1, keepdims=True) * (1.0 / n) + EPS)


def _inproj_body(x_ref, g_ref, w_ref, wab_ref, p_ref, ab_ref, h_scr):
    @pl.when(pl.program_id(1) == 0)
    def _():
        x = x_ref[...]
        h = x * _rms(x, D_MODEL) * g_ref[...]
        h_scr[...] = h.astype(BF16)
        ab_ref[...] = _dot3(h, wab_ref[...])

    p_ref[...] = _dot(h_scr[...], w_ref[...]).astype(p_ref.dtype)


def _inproj(x2, gain, wp, wab):
    t = x2.shape[0]
    tm = min(1024, t)
    tn = 1024
    return pl.pallas_call(
        _inproj_body,
        grid=(t // tm, P_COLS // tn),
        in_specs=[
            pl.BlockSpec((tm, D_MODEL), lambda i, j: (i, 0)),
            pl.BlockSpec((1, D_MODEL), lambda i, j: (0, 0)),
            pl.BlockSpec((D_MODEL, tn), lambda i, j: (0, j)),
            pl.BlockSpec((D_MODEL, LANES), lambda i, j: (0, 0)),
        ],
        out_specs=[
            pl.BlockSpec((tm, tn), lambda i, j: (i, j)),
            pl.BlockSpec((tm, LANES), lambda i, j: (i, 0)),
        ],
        out_shape=[
            jax.ShapeDtypeStruct((t, P_COLS), BF16),
            jax.ShapeDtypeStruct((t, LANES), F32),
        ],
        scratch_shapes=[pltpu.VMEM((tm, D_MODEL), BF16)],
        compiler_params=pltpu.CompilerParams(dimension_semantics=("parallel", "arbitrary")),
        name="inproj",
    )(x2, gain, wp, wab)


DN_HALO = 16
DN_SCAN_CHUNK = 256


def _deltanet_body(q_ref, k_ref, v_ref, z_ref, qh_ref, kh_ref, vh_ref, ab_ref, cw_ref, alog_ref,
                   dtb_ref, on_ref, o_ref, stage, qs, ks, vs, s_scr):
    i = pl.program_id(1)
    tc = q_ref.shape[0]
    hw = DN_HEADS * DN_DK

    @pl.when(i == 0)
    def _():
        s_scr[...] = jnp.zeros_like(s_scr)

    for src, halo, dst, off, kind in ((q_ref, qh_ref, qs, 0, "q"), (k_ref, kh_ref, ks, hw, "k"),
                                      (v_ref, vh_ref, vs, 2 * hw, "v")):
        hal = halo[...].astype(F32)
        stage[0:DN_HALO, :] = jnp.where(i == 0, 0.0, hal)
        stage[DN_HALO:DN_HALO + tc, :] = src[...].astype(F32)
        base = DN_HALO - (DN_CONV - 1)
        y = stage[base:base + tc, :] * cw_ref[0:1, off:off + hw]
        for j in range(1, DN_CONV):
            y = y + stage[base + j:base + j + tc, :] * cw_ref[j:j + 1, off:off + hw]
        y = y * _sigmoid(y)
        if kind == "v":
            dst[...] = y
        else:
            for h in range(DN_HEADS):
                sl = slice(h * DN_DK, (h + 1) * DN_DK)
                yh = y[:, sl]
                r = lax.rsqrt(jnp.sum(yh * yh, axis=-1, keepdims=True) + EPS)
                if kind == "q":
                    r = r * (DN_DK ** -0.5)
                dst[:, sl] = yh * r

    c = min(DN_SCAN_CHUNK, tc)
    row = lax.broadcasted_iota(I32, (c, c), 0)
    col = lax.broadcasted_iota(I32, (c, c), 1)
    incl = row >= col
    strict = row > col
    same_blk = (row // DN_CHUNK) == (col // DN_CHUNK)
    tri = jnp.where(incl, 1.0, 0.0).astype(BF16)
    eye = jnp.where(row == col, 1.0, 0.0)
    neg_a = -jnp.exp(alog_ref[...])
    dtb = dtb_ref[...]

    def chunk(ci, carry):
        r0 = pl.multiple_of(ci * c, c)
        abc = ab_ref[pl.ds(r0, c), :]
        a_in = abc + dtb
        g_all = neg_a * (jnp.maximum(a_in, 0.0) + jnp.log(1.0 + jnp.exp(-jnp.abs(a_in))))
        beta_all = _sigmoid(abc)
        zc = z_ref[pl.ds(r0, c), :].astype(F32)
        hs = range(DN_HEADS)
        sls = [slice(h * DN_DK, (h + 1) * DN_DK) for h in hs]
        q = [qs[pl.ds(r0, c), sl] for sl in sls]
        k = [ks[pl.ds(r0, c), sl] for sl in sls]
        v = [vs[pl.ds(r0, c), sl] for sl in sls]
        g = [g_all[:, h:h + 1] for h in hs]
        beta = [beta_all[:, DN_HEADS + h:DN_HEADS + h + 1] for h in hs]
        g_split = [_split(jnp.where(strict, g[h], 0.0)) for h in hs]
        diff = [_dot(tri, g_split[h][0]) + _dot(tri, g_split[h][1]) for h in hs]
        kb = [k[h].astype(BF16) for h in hs]
        qkk = [_dot_nt(jnp.concatenate([q[h].astype(BF16), kb[h]], axis=0), kb[h]) for h in hs]
        gc = [diff[h][:, 0:1] + g[h][0:1, :] for h in hs]
        decay = [jnp.where(incl, jnp.exp(diff[h]), 0.0) for h in hs]
        lower = [jnp.where(strict, qkk[h][c:] * decay[h] * beta[h], 0.0) for h in hs]
        pw = [jnp.where(same_blk, -lower[h], 0.0) for h in hs]
        dinv = [eye + pw[h] for h in hs]
        pwb = [pw[h].astype(BF16) for h in hs]
        for _ in range(int(math.log2(DN_CHUNK)) - 1):
            pwb = [_dot(pwb[h], pwb[h]).astype(BF16) for h in hs]
            dinv = [dinv[h] + _dot(dinv[h].astype(BF16), pwb[h]) for h in hs]
        dinv_b = [dinv[h].astype(BF16) for h in hs]
        pw = [-_dot(dinv_b[h], jnp.where(same_blk, 0.0, lower[h]).astype(BF16)) for h in hs]
        xm = [eye + pw[h] for h in hs]
        pwb = [pw[h].astype(BF16) for h in hs]
        for _ in range(int(math.log2(c // DN_CHUNK)) - 1):
            pwb = [_dot(pwb[h], pwb[h]).astype(BF16) for h in hs]
            xm = [xm[h] + _dot(xm[h].astype(BF16), pwb[h]) for h in hs]
        inv = [_dot(xm[h].astype(BF16), dinv_b[h]).astype(BF16) for h in hs]
        egc = [jnp.exp(gc[h]) for h in hs]
        rhs = [jnp.concatenate([v[h] * beta[h], k[h] * (beta[h] * egc[h])], axis=1).astype(BF16) for h in hs]
        sol = [_dot(inv[h], rhs[h]) for h in hs]
        qkm = [jnp.where(incl, qkk[h][:c] * decay[h], 0.0).astype(BF16) for h in hs]
        gl = [gc[h][c - 1:c, :] for h in hs]
        state = [s_scr[h] for h in hs]
        ws = [_dot(jnp.concatenate([sol[h][:, DN_DK:].astype(BF16), (q[h] * egc[h]).astype(BF16)], axis=0),
                   state[h].astype(BF16)) for h in hs]
        v_new = [sol[h][:, :DN_DK] - ws[h][:c] for h in hs]
        o = [ws[h][c:] + _dot(qkm[h], v_new[h].astype(BF16)) for h in hs]
        for h in hs:
            s_scr[h] = state[h] * jnp.exp(gl[h]) + _dot_tn(kb[h], (v_new[h] * jnp.exp(gl[h] - gc[h])).astype(BF16))
        for h in hs:
            zz = zc[:, sls[h]]
            on = o[h] * _rms(o[h], DN_DK) * on_ref[...]
            o_ref[pl.ds(r0, c), sls[h]] = (on * (zz * _sigmoid(zz))).astype(o_ref.dtype)
        return carry

    lax.fori_loop(0, tc // c, chunk, 0, unroll=True)


def _deltanet(p, ab, conv_w, alog_row, dtb_row, out_norm, batch, seq):
    t = batch * seq
    tc = min(512, seq)
    nt = seq // tc
    hw = DN_HEADS * DN_DK

    def main(cb):
        return pl.BlockSpec((tc, hw), lambda b, i: (b * nt + i, cb))

    def halo(cb):
        return pl.BlockSpec(
            (DN_HALO, hw),
            lambda b, i: (jnp.maximum((b * seq + i * tc) // DN_HALO - 1, 0), cb))

    return pl.pallas_call(
        _deltanet_body,
        grid=(batch, nt),
        in_specs=[
            main(P_DNQ // hw), main(P_DNK // hw), main(P_DNV // hw), main(P_DNZ // hw),
            halo(P_DNQ // hw), halo(P_DNK // hw), halo(P_DNV // hw),
            pl.BlockSpec((tc, LANES), lambda b, i: (b * nt + i, 0)),
            pl.BlockSpec((DN_CONV, 3 * hw), lambda b, i: (0, 0)),
            pl.BlockSpec((1, LANES), lambda b, i: (0, 0)),
            pl.BlockSpec((1, LANES), lambda b, i: (0, 0)),
            pl.BlockSpec((1, DN_DK), lambda b, i: (0, 0)),
        ],
        out_specs=pl.BlockSpec((tc, hw), lambda b, i: (b * nt + i, 0)),
        out_shape=jax.ShapeDtypeStruct((t, hw), BF16),
        scratch_shapes=[
            pltpu.VMEM((DN_HALO + tc, hw), F32),
            pltpu.VMEM((tc, hw), F32),
            pltpu.VMEM((tc, hw), F32),
            pltpu.VMEM((tc, hw), F32),
            pltpu.VMEM((DN_HEADS, DN_DK, DN_DK), F32),
        ],
        compiler_params=pltpu.CompilerParams(dimension_semantics=("parallel", "arbitrary")),
        name="deltanet",
    )(p, p, p, p, p, p, p, ab, conv_w, alog_row, dtb_row, out_norm)


ATT_BLOCK = 512


def _bias_body(tbl_ref, o_ref):
    h = pl.program_id(0)
    tq = o_ref.shape[2]
    key = lax.broadcasted_iota(I32, (tq, tq), 0)
    qry = lax.broadcasted_iota(I32, (tq, tq), 1)
    max_exact = REL_BUCKETS // 2
    far = tbl_ref[h, REL_BUCKETS - 1]
    for d in range(2):
        n = qry - key + d * tq
        nn = jnp.maximum(n, 0)
        nf = jnp.maximum(nn, 1).astype(F32)
        large = max_exact + (jnp.log(nf / max_exact) / math.log(REL_MAX_DIST / max_exact)
                             * (REL_BUCKETS - max_exact)).astype(I32)
        large = jnp.minimum(large, REL_BUCKETS - 1)
        bucket = jnp.where(nn < max_exact, nn, large)
        val = jnp.zeros((tq, tq), F32)
        for b in range(REL_BUCKETS):
            val = jnp.where(bucket == b, tbl_ref[h, b], val)
        o_ref[0, d] = jnp.where(n >= 0, (val - far) * LOG2E, NEG)


def _bias_tiles(tbl_t, tq):
    return pl.pallas_call(
        _bias_body,
        grid=(DA_HEADS,),
        in_specs=[pl.BlockSpec(memory_space=pltpu.SMEM)],
        out_specs=pl.BlockSpec((1, 2, tq, tq), lambda h: (h, 0, 0, 0)),
        out_shape=jax.ShapeDtypeStruct((DA_HEADS, 2, tq, tq), F32),
        name="t5_bias_tiles",
    )(tbl_t)


DA_DV = 2 * DA_DH
DA_VROWS = DA_DV + 16


BOUND_SLACK = 1.02
MAX_SHIFT_GAP = 110.0


def _attn_body(q_ref, qn_ref, k_ref, v_ref, bias_ref, qg_ref, kg_ref, lam_ref, sg_ref, o_ref,
               kn, vt, kst, qc_s, bd_s, flag_s, m_s, acc_s):
    qi = pl.program_id(2)
    slot = qi % 2
    tq = q_ref.shape[0]
    seq = k_ref.shape[0]
    tk = tq
    lo_mask = lax.broadcasted_iota(I32, (1, DA_DV), 1) < DA_DH

    def group_norm(x, gain):
        x2 = x * x
        lo = jnp.sum(jnp.where(lo_mask, x2, 0.0), axis=-1, keepdims=True)
        hi = jnp.sum(jnp.where(lo_mask, 0.0, x2), axis=-1, keepdims=True)
        r = jnp.where(lo_mask, lax.rsqrt(lo * (1.0 / DA_DH) + EPS), lax.rsqrt(hi * (1.0 / DA_DH) + EPS))
        return x * r * gain

    @pl.when(qi == 0)
    def _():
        ones = jnp.ones((DA_VROWS - DA_DV, tk), BF16)

        def body(c, kmax2):
            r0 = pl.multiple_of(c * tk, tk)
            kb = group_norm(k_ref[pl.ds(r0, tk), :].astype(F32), kg_ref[...]).astype(BF16)
            kn[pl.ds(r0, tk), :] = kb
            vt[c, 0:DA_DV, :] = v_ref[pl.ds(r0, tk), :].astype(F32).T.astype(BF16)
            vt[c, DA_DV:DA_VROWS, :] = ones
            k2 = kb.astype(F32)
            k2 = k2 * k2
            lo = jnp.max(jnp.sum(jnp.where(lo_mask, k2, 0.0), axis=-1, keepdims=True), axis=0, keepdims=True)
            hi = jnp.max(jnp.sum(jnp.where(lo_mask, 0.0, k2), axis=-1, keepdims=True), axis=0, keepdims=True)
            return jnp.maximum(kmax2, jnp.where(lo_mask, lo, hi))
        kst[0:1, :] = lax.fori_loop(0, seq // tk, body, jnp.zeros((1, DA_DV), F32), unroll=2)
        b0 = bias_ref[0, 0]
        b1 = bias_ref[0, 1]
        bmax = jnp.maximum(jnp.max(jnp.maximum(b0, b1), axis=0, keepdims=True), 0.0)
        bmin = jnp.minimum(jnp.min(jnp.minimum(jnp.where(b0 > 0.5 * NEG, b0, 0.0), b1), axis=0, keepdims=True), 0.0)
        kst[1:2, :] = jnp.broadcast_to(jnp.max(bmax, axis=1, keepdims=True), (1, DA_DV))
        kst[2:3, :] = jnp.broadcast_to(jnp.min(bmin, axis=1, keepdims=True), (1, DA_DV))

    def prepare(src_ref, s):
        q = group_norm(src_ref[...].astype(F32), qg_ref[...]) * (DA_DH ** -0.5 * LOG2E)
        qc_s[s] = jnp.concatenate([jnp.where(lo_mask, q, 0.0), jnp.where(lo_mask, 0.0, q)], axis=0).astype(BF16)
        q2 = q * q * kst[0:1, :]
        ones8 = jnp.ones((8, DA_DV), BF16)
        bound = []
        for m in range(2):
            q2m = jnp.where(lo_mask, q2, 0.0) if m == 0 else jnp.where(lo_mask, 0.0, q2)
            bound.append(jnp.sqrt(_dot_nt(ones8, q2m.astype(BF16))[0:1, :]) * BOUND_SLACK)
        bound = jnp.concatenate(bound, axis=1)
        bd_s[s] = bound
        worst = jnp.max(2.0 * bound, axis=1, keepdims=True) + kst[1:2, 0:1] - kst[2:3, 0:1]
        flag_s[s] = jnp.where(worst[0, 0] <= MAX_SHIFT_GAP, 1, 0).astype(I32)

    @pl.when(qi == 0)
    def _():
        prepare(q_ref, slot)

    qcat = qc_s[slot]
    bound = bd_s[slot]
    bmax = kst[1:2, 0:1]
    safe = flag_s[slot] == 1
    acc_s[...] = jnp.zeros_like(acc_s)

    def block(j, d):
        r0 = pl.multiple_of(j * tk, tk)
        st = _dot_nt(kn[pl.ds(r0, tk), :], qcat)
        if d is not None:
            bias = bias_ref[0, d]
            st = st + jnp.concatenate([bias, bias], axis=1)
        m_prev = m_s[...]
        m_new = jnp.maximum(m_prev, jnp.max(st, axis=0, keepdims=True))
        alpha = jnp.exp2(m_prev - m_new)
        acc_s[...] = alpha * acc_s[...] + _dot(vt[j], jnp.exp2(st - m_new).astype(BF16))
        m_s[...] = m_new

    def blocks_fixed(js, ds=None):
        sts = [_dot_nt(kn[pl.ds(pl.multiple_of(j * tk, tk), tk), :], qcat) for j in js]
        if ds is not None:
            biases = [bias_ref[0, d] for d in ds]
            sts = [st + jnp.concatenate([b, b], axis=1) for st, b in zip(sts, biases)]
        shift = m_s[...]
        pts = [jnp.exp2(st - shift).astype(BF16) for st in sts]
        tot = _dot(vt[js[0]], pts[0])
        for j, pt in zip(js[1:], pts[1:]):
            tot = tot + _dot(vt[j], pt)
        acc_s[...] = acc_s[...] + tot

    def run(fixed_shift):
        n_far = jnp.maximum(qi - 1, 0)

        if fixed_shift:
            def far_quad(jj, carry):
                blocks_fixed([4 * jj + u for u in range(4)])
                return carry

            lax.fori_loop(0, n_far // 4, far_quad, 0)
            rem = n_far % 4

            @pl.when(rem >= 2)
            def _():
                blocks_fixed([n_far - rem, n_far - rem + 1])

            @pl.when(rem % 2 == 1)
            def _():
                blocks_fixed([n_far - 1])

            @pl.when(qi >= 1)
            def _():
                blocks_fixed([qi - 1, qi], [1, 0])
                prepare(qn_ref, 1 - slot)

            @pl.when(qi == 0)
            def _():
                blocks_fixed([qi], [0])
                prepare(qn_ref, 1 - slot)
        else:
            def far_one(j, carry):
                block(j, None)
                return carry

            lax.fori_loop(0, n_far, far_one, 0)

            @pl.when(qi >= 1)
            def _():
                block(qi - 1, 1)

            block(qi, 0)
            prepare(qn_ref, 1 - slot)

    @pl.when(safe)
    def _():
        m_s[...] = bound + bmax
        run(True)

    @pl.when(jnp.logical_not(safe))
    def _():
        m_s[...] = jnp.full(m_s.shape, NEG, F32)
        run(False)

    lam_p = lam_ref[...]
    lam = (jnp.exp(jnp.sum(lam_p[0:1, :] * lam_p[1:2, :], axis=-1, keepdims=True))
           - jnp.exp(jnp.sum(lam_p[2:3, :] * lam_p[3:4, :], axis=-1, keepdims=True)) + LAM_INIT)
    a0 = acc_s[:, 0:tq]
    a1 = acc_s[:, tq:2 * tq]
    ot = a0[0:DA_DV] / a0[DA_DV:DA_DV + 1] - lam * (a1[0:DA_DV] / a1[DA_DV:DA_DV + 1])
    r = lax.rsqrt(jnp.sum(ot * ot, axis=0, keepdims=True) * (1.0 / DA_DV) + EPS)
    ot = ot * r * (sg_ref[...] * (1.0 - LAM_INIT))
    o_ref[...] = ot.T.astype(o_ref.dtype)


def _attention(p, bias, qg, kg, lam_p, subln, batch, seq):
    t = batch * seq
    tq = min(ATT_BLOCK, seq)
    nq = seq // tq
    dv = DA_DV
    return pl.pallas_call(
        _attn_body,
        grid=(batch, DA_HEADS, nq),
        in_specs=[
            pl.BlockSpec((tq, dv), lambda b, h, i: (b * nq + i, P_DAQ // dv + h)),
            pl.BlockSpec((tq, dv), lambda b, h, i: (b * nq + jnp.minimum(i + 1, nq - 1), P_DAQ // dv + h)),
            pl.BlockSpec((seq, dv), lambda b, h, i: (b, P_DAK // dv + h)),
            pl.BlockSpec((seq, dv), lambda b, h, i: (b, P_DAV // dv + h)),
            pl.BlockSpec((1, 2, tq, tq), lambda b, h, i: (h, 0, 0, 0)),
            pl.BlockSpec((1, dv), lambda b, h, i: (0, 0)),
            pl.BlockSpec((1, dv), lambda b, h, i: (0, 0)),
            pl.BlockSpec((4, DA_DH), lambda b, h, i: (0, 0)),
            pl.BlockSpec((dv, 1), lambda b, h, i: (0, 0)),
        ],
        out_specs=pl.BlockSpec((tq, dv), lambda b, h, i: (b * nq + i, h)),
        out_shape=jax.ShapeDtypeStruct((t, DA_HEADS * dv), BF16),
        scratch_shapes=[
            pltpu.VMEM((seq, dv), BF16),
            pltpu.VMEM((seq // tq, DA_VROWS, tq), BF16),
            pltpu.VMEM((8, dv), F32),
            pltpu.VMEM((2, 2 * tq, dv), BF16),
            pltpu.VMEM((2, 1, 2 * tq), F32),
            pltpu.SMEM((2,), I32),
            pltpu.VMEM((1, 2 * tq), F32),
            pltpu.VMEM((DA_VROWS, 2 * tq), F32),
        ],
        compiler_params=pltpu.CompilerParams(dimension_semantics=("parallel", "parallel", "arbitrary")),
        name="diff_attention",
    )(p, p, p, p, bias, qg, kg, lam_p, subln)


def _memkv_body(mem_ref, mg_ref, w_ref, kg_ref, mk_ref, mv_ref):
    x = mem_ref[0]
    xn = x * _rms(x, D_MODEL) * mg_ref[...]
    kv = _dot(xn.astype(BF16), w_ref[...])
    hw = MX_HEADS * MX_DH
    for h in range(MX_HEADS):
        sl = slice(h * MX_DH, (h + 1) * MX_DH)
        kh = kv[:, sl]
        mk_ref[0, :, sl] = (kh * _rms(kh, MX_DH) * kg_ref[...]).astype(BF16)
    mv_ref[0] = kv[:, hw:].astype(BF16)


def _memkv(mem, mem_norm, w_kv, k_norm):
    b, n, _ = mem.shape
    hw = MX_HEADS * MX_DH
    return pl.pallas_call(
        _memkv_body,
        grid=(b,),
        in_specs=[
            pl.BlockSpec((1, n, D_MODEL), lambda i: (i, 0, 0)),
            pl.BlockSpec((1, D_MODEL), lambda i: (0, 0)),
            pl.BlockSpec((D_MODEL, 2 * hw), lambda i: (0, 0)),
            pl.BlockSpec((1, MX_DH), lambda i: (0, 0)),
        ],
        out_specs=[pl.BlockSpec((1, n, hw), lambda i: (i, 0, 0))] * 2,
        out_shape=[jax.ShapeDtypeStruct((b, n, hw), BF16)] * 2,
        name="memory_kv",
    )(mem, mem_norm, w_kv, k_norm)


def _merge_body(x_ref, odn_ref, oda_ref, mxq_ref, g0_ref, g1_ref, g2_ref, mk_ref, mv_ref, qg_ref,
                bg_ref, wb_ref, wo_ref, out_ref, omx):
    for h in range(MX_HEADS):
        sl = slice(h * MX_DH, (h + 1) * MX_DH)
        qh = mxq_ref[:, sl].astype(F32)
        qh = qh * _rms(qh, MX_DH) * qg_ref[...] * (MX_DH ** -0.5 * LOG2E)
        s = _dot_nt(qh.astype(BF16), mk_ref[0, :, sl])
        p = jnp.exp2(s - jnp.max(s, axis=-1, keepdims=True))
        oh = _dot(p.astype(BF16), mv_ref[0, :, sl]) / jnp.sum(p, axis=-1, keepdims=True)
        omx[:, sl] = oh.astype(BF16)
    y = None
    for r, (o_r, g_r) in enumerate(((odn_ref, g0_ref), (oda_ref, g1_ref), (omx, g2_ref))):
        gate = _sigmoid(g_r[...].astype(F32) + bg_ref[r:r + 1, :])
        term = gate * _dot(o_r[...], wb_ref[r])
        y = term if y is None else y + term
    out_ref[...] = x_ref[...] + _dot(y.astype(BF16), wo_ref[...])


def _merge(x2, o_dn, o_da, p, mk, mv, q_norm, b_gate, w_branch, w_out, seq):
    t = x2.shape[0]
    tm = min(512, seq)
    nt = seq // tm
    bw = 512
    n_mem = mk.shape[1]
    return pl.pallas_call(
        _merge_body,
        grid=(t // tm,),
        in_specs=[
            pl.BlockSpec((tm, D_MODEL), lambda i: (i, 0)),
            pl.BlockSpec((tm, bw), lambda i: (i, 0)),
            pl.BlockSpec((tm, bw), lambda i: (i, 0)),
            pl.BlockSpec((tm, bw), lambda i: (i, P_MXQ // bw)),
            pl.BlockSpec((tm, D_MODEL), lambda i: (i, P_GATE // D_MODEL)),
            pl.BlockSpec((tm, D_MODEL), lambda i: (i, P_GATE // D_MODEL + 1)),
            pl.BlockSpec((tm, D_MODEL), lambda i: (i, P_GATE // D_MODEL + 2)),
            pl.BlockSpec((1, n_mem, bw), lambda i: (i // nt, 0, 0)),
            pl.BlockSpec((1, n_mem, bw), lambda i: (i // nt, 0, 0)),
            pl.BlockSpec((1, MX_DH), lambda i: (0, 0)),
            pl.BlockSpec((3, D_MODEL), lambda i: (0, 0)),
            pl.BlockSpec((3, bw, D_MODEL), lambda i: (0, 0, 0)),
            pl.BlockSpec((D_MODEL, D_MODEL), lambda i: (0, 0)),
        ],
        out_specs=pl.BlockSpec((tm, D_MODEL), lambda i: (i, 0)),
        out_shape=jax.ShapeDtypeStruct((t, D_MODEL), F32),
        scratch_shapes=[pltpu.VMEM((tm, bw), BF16)],
        compiler_params=pltpu.CompilerParams(dimension_semantics=("parallel",)),
        name="merge",
    )(x2, o_dn, o_da, p, p, p, p, mk, mv, q_norm, b_gate, w_branch, w_out)


def _router_body(x_ref, g_ref, wr_ref, br_ref, h_ref, idx_ref, wt_ref, rank_ref, cnt_ref):
    tm = x_ref.shape[0]
    x = x_ref[...]
    h = x * _rms(x, D_MODEL) * g_ref[...]
    h_ref[...] = h.astype(BF16)
    logits = _dot3(h, wr_ref[...]) + br_ref[...]
    lane = lax.broadcasted_iota(I32, (tm, LANES), 1)
    lane_f = lane.astype(F32)
    work = logits
    sel = jnp.zeros((tm, LANES), F32)
    vals, idxs = [], []
    for _ in range(TOP_K):
        mx = jnp.max(work, axis=-1, keepdims=True)
        ik = jnp.min(jnp.where(work == mx, lane_f, float(LANES)), axis=-1, keepdims=True)
        hit = lane_f == ik
        sel = jnp.where(hit, 1.0, sel)
        work = jnp.where(hit, -jnp.inf, work)
        vals.append(mx)
        idxs.append(ik)
    es = [jnp.exp(v - vals[0]) for v in vals]
    den = es[0] + es[1] + es[2] + es[3]
    sub = cnt_ref.shape[0]
    mt = tm // sub
    r = lax.broadcasted_iota(I32, (tm, tm), 0)
    c = lax.broadcasted_iota(I32, (tm, tm), 1)
    tril = jnp.where((r > c) & (r // mt == c // mt), 1.0, 0.0).astype(BF16)
    cum = _dot(tril, sel.astype(BF16))
    idx_o = jnp.zeros((tm, LANES), F32)
    wt_o = jnp.zeros((tm, LANES), F32)
    rank_o = jnp.zeros((tm, LANES), F32)
    for k in range(TOP_K):
        rk = jnp.sum(jnp.where(lane_f == idxs[k], cum, 0.0), axis=-1, keepdims=True)
        idx_o = jnp.where(lane == k, idxs[k], idx_o)
        wt_o = jnp.where(lane == k, es[k] / den, wt_o)
        rank_o = jnp.where(lane == k, rk, rank_o)
    idx_ref[...] = idx_o.astype(I32)
    wt_ref[...] = wt_o
    rank_ref[...] = rank_o.astype(I32)
    for s in range(sub):
        cnt_ref[s] = jnp.broadcast_to(jnp.sum(sel[s * mt:(s + 1) * mt], axis=0, keepdims=True), (8, LANES))


def _router(x1, gain, w_r, b_r):
    t = x1.shape[0]
    mt = min(MOE_TILE, t)
    sub = 2 if t % (2 * mt) == 0 else 1
    tm = sub * mt
    row = lambda i: (i, 0)
    fixed = lambda i: (0, 0)
    return pl.pallas_call(
        _router_body,
        grid=(t // tm,),
        in_specs=[
            pl.BlockSpec((tm, D_MODEL), row),
            pl.BlockSpec((1, D_MODEL), fixed),
            pl.BlockSpec((D_MODEL, LANES), fixed),
            pl.BlockSpec((1, LANES), fixed),
        ],
        out_specs=[
            pl.BlockSpec((tm, D_MODEL), row),
            pl.BlockSpec((tm, LANES), row),
            pl.BlockSpec((tm, LANES), row),
            pl.BlockSpec((tm, LANES), row),
            pl.BlockSpec((sub, 8, LANES), lambda i: (i, 0, 0)),
        ],
        out_shape=[
            jax.ShapeDtypeStruct((t, D_MODEL), BF16),
            jax.ShapeDtypeStruct((t, LANES), I32),
            jax.ShapeDtypeStruct((t, LANES), F32),
            jax.ShapeDtypeStruct((t, LANES), I32),
            jax.ShapeDtypeStruct((t // mt, 8, LANES), F32),
        ],
        compiler_params=pltpu.CompilerParams(dimension_semantics=("arbitrary",)),
        name="router",
    )(x1, gain, w_r, b_r)


def _lane_cumsum(x):
    lane = lax.broadcasted_iota(I32, x.shape, 1)
    s = 1
    while s < N_EXPERTS:
        x = x + jnp.where(lane >= s, pltpu.roll(x, s, axis=1), 0.0)
        s *= 2
    return x


def _plan_body(cnt_ref, gs_ref, ls_ref, n8_ref, lsf_ref, tail_ref, meta_ref):
    nt = cnt_ref.shape[0]
    ga = float(GROUP_ALIGN)
    eb = float(EXPERT_BLOCK)
    lane = lax.broadcasted_iota(I32, (nt, LANES), 1)
    r8 = jnp.where(lane < N_EXPERTS, jnp.floor((cnt_ref[...] + (ga - 1.0)) * (1.0 / ga)) * ga, 0.0)
    ri = lax.broadcasted_iota(I32, (nt, nt), 0)
    ci = lax.broadcasted_iota(I32, (nt, nt), 1)
    before = _dot(jnp.where(ri > ci, 1.0, 0.0).astype(BF16), r8.astype(BF16))
    tot = jnp.sum(r8, axis=0, keepdims=True)
    region = jnp.floor((tot + (eb - 1.0)) * (1.0 / eb)) * eb
    pends = _lane_cumsum(jnp.broadcast_to(region, (8, LANES)))[0:1, :]
    pstart = pends - region
    lstart = _lane_cumsum(r8) - r8
    gs_ref[...] = (pstart + before).astype(I32)
    ls_ref[...] = lstart.astype(I32)
    pieces = r8 * (1.0 / ga)
    n8_ref[...] = jnp.where(lane == TOTAL_LANE, jnp.sum(pieces, axis=1, keepdims=True), pieces).astype(I32)
    lsf_ref[...] = lstart
    row8 = lax.broadcasted_iota(I32, (8, LANES), 0)
    tail_n = (region - tot) * (1.0 / ga)
    lane8 = lax.broadcasted_iota(I32, (8, LANES), 1)
    tail_n = jnp.where(lane8 == TOTAL_LANE, jnp.sum(tail_n, axis=1, keepdims=True), tail_n)
    tail = jnp.where(row8 == 0, pstart + tot, jnp.where(row8 == 1, tail_n, 0.0))
    tail_ref[...] = tail.astype(I32)
    nb = meta_ref.shape[0]
    ln = lax.broadcasted_iota(I32, (nb, LANES), 1)
    blk = lax.broadcasted_iota(I32, (nb, LANES), 0).astype(F32) * eb
    be = jnp.sum(jnp.where((ln < N_EXPERTS) & (pends <= blk), 1.0, 0.0), axis=-1, keepdims=True)
    be = jnp.minimum(be, float(N_EXPERTS - 1))
    used = jnp.sum(jnp.where(ln == N_EXPERTS - 1, pends, 0.0), axis=-1, keepdims=True) * (1.0 / eb)
    meta_ref[...] = jnp.where(ln == 0, be, jnp.where(ln == 1, used, 0.0)).astype(I32)


def _plan(cnt, n_blocks_pad):
    nt = cnt.shape[0]
    shp = jax.ShapeDtypeStruct((nt, LANES), I32)
    return pl.pallas_call(
        _plan_body,
        out_shape=[shp, shp, shp, jax.ShapeDtypeStruct((nt, LANES), F32),
                   jax.ShapeDtypeStruct((8, LANES), I32),
                   jax.ShapeDtypeStruct((n_blocks_pad, LANES), I32)],
        name="dispatch_plan",
    )(cnt)


def _local_positions(idx_ref, rank_ref, lsf_ref):
    tm = idx_ref.shape[0]
    lane = lax.broadcasted_iota(I32, (tm, LANES), 1)
    idx = idx_ref[...]
    rank = rank_ref[...].astype(F32)
    ls_row = lsf_ref[0]
    out = []
    for k in range(TOP_K):
        base = jnp.sum(jnp.where(lane == idx[:, k:k + 1], ls_row, 0.0), axis=-1, keepdims=True)
        out.append((base + rank[:, k:k + 1]).astype(I32))
    return out


def _local_positions_lanes(idx_ref, rank_ref, lsf_ref):
    tm = idx_ref.shape[0]
    idx_t = idx_ref[...].astype(F32).T
    rank_t = rank_ref[...].astype(F32).T
    ls_col = jnp.broadcast_to(lsf_ref[0], (LANES, LANES)).T[:, 0:1]
    expert = lax.broadcasted_iota(I32, (LANES, tm), 0).astype(F32)
    out = []
    for k in range(TOP_K):
        base = jnp.sum(jnp.where(expert == idx_t[k:k + 1, :], ls_col, 0.0), axis=0, keepdims=True)
        out.append((base + rank_t[k:k + 1, :]).astype(I32))
    return out


PACK_W = D_MODEL // 2
U32 = jnp.uint32


def _pack_rows(x):
    xb = x.astype(BF16).astype(F32)
    hi = lax.bitcast_convert_type(xb[:, :PACK_W], U32)
    lo = lax.bitcast_convert_type(xb[:, PACK_W:], U32)
    return hi | (lo >> 16)


def _unpack_rows(w):
    hi = lax.bitcast_convert_type(w & jnp.uint32(0xFFFF0000), F32)
    lo = lax.bitcast_convert_type(w << 16, F32)
    return hi.astype(BF16), lo.astype(BF16)


TOTAL_LANE = LANES - 1
WAIT_CHUNK = 128
BIG_PIECE = 4


def _start_group_copies(tables, tile, make_copy):
    gs_ref, ls_ref, n8_ref = tables

    def group(e, carry):
        g0 = gs_ref[tile, e]
        l0 = ls_ref[tile, e]
        n = n8_ref[tile, e]
        nbig = n // BIG_PIECE

        def big(j, c):
            off = j * (BIG_PIECE * GROUP_ALIGN)
            make_copy(pl.multiple_of(l0 + off, GROUP_ALIGN), pl.multiple_of(g0 + off, GROUP_ALIGN),
                      BIG_PIECE * GROUP_ALIGN).start()
            return c

        def one(j, c):
            off = j * GROUP_ALIGN
            make_copy(pl.multiple_of(l0 + off, GROUP_ALIGN), pl.multiple_of(g0 + off, GROUP_ALIGN),
                      GROUP_ALIGN).start()
            return c

        lax.fori_loop(0, nbig, big, 0)
        lax.fori_loop(nbig * BIG_PIECE, n, one, 0)
        return carry

    lax.fori_loop(0, N_EXPERTS, group, 0)


def _wait_pieces(n, make_wait):
    def chunk(j, c):
        make_wait(WAIT_CHUNK).wait()
        return c

    lax.fori_loop(0, n // WAIT_CHUNK, chunk, 0)
    b = WAIT_CHUNK // 2
    while b >= 1:
        def _(b=b):
            make_wait(b).wait()
        pl.when((n & b) != 0)(_)
        b //= 2


def _dispatch_body(gs_ref, ls_ref, n8_ref, tail_ref, h_ref, idx_ref, rank_ref, lsf_ref, xs_ref,
                   xl, zbuf, sem):
    i = pl.program_id(0)
    last = i == pl.num_programs(0) - 1
    slot = i % 2
    tables = (gs_ref, ls_ref, n8_ref)
    tm = h_ref.shape[0]
    lrows = xl.shape[1]
    pos = lax.broadcasted_iota(I32, (lrows, tm), 0)
    lp = _local_positions_lanes(idx_ref, rank_ref, lsf_ref)
    hit = pos == lp[0]
    for k in range(1, TOP_K):
        hit = hit | (pos == lp[k])
    xl[slot] = _pack_rows(_dot(jnp.where(hit, 1.0, 0.0).astype(BF16), h_ref[...]))

    def copy_from(s):
        def make_copy(l, g, rows):
            return pltpu.make_async_copy(xl.at[s, pl.ds(l, rows), :], xs_ref.at[pl.ds(g, rows), :], sem.at[s])
        return make_copy

    def wait_on(s):
        def make_wait(pieces):
            rows = pieces * GROUP_ALIGN
            return pltpu.make_async_copy(xl.at[0, pl.ds(0, rows), :], xs_ref.at[pl.ds(0, rows), :], sem.at[s])
        return make_wait

    @pl.when(i >= 1)
    def _():
        _wait_pieces(n8_ref[i - 1, TOTAL_LANE], wait_on(1 - slot))

    _start_group_copies(tables, i, copy_from(slot))

    def start_tails():
        def per_expert(e, carry):
            g0 = tail_ref[0, e]

            def one(j, c):
                pltpu.make_async_copy(
                    zbuf, xs_ref.at[pl.ds(pl.multiple_of(g0 + j * GROUP_ALIGN, GROUP_ALIGN), GROUP_ALIGN), :],
                    sem.at[2]).start()
                return c

            lax.fori_loop(0, tail_ref[1, e], one, 0)
            return carry

        lax.fori_loop(0, N_EXPERTS, per_expert, 0)

    @pl.when(last)
    def _():
        zbuf[...] = jnp.zeros_like(zbuf)
        start_tails()
        _wait_pieces(n8_ref[i, TOTAL_LANE], wait_on(slot))
        _wait_pieces(tail_ref[1, TOTAL_LANE], wait_on(2))


def _dispatch(gs, ls, n8, tail, h2, idx, rank, lsf, n_slots):
    t = h2.shape[0]
    tm = min(MOE_TILE, t)
    nt = t // tm
    lrows = tm * TOP_K + N_EXPERTS * GROUP_ALIGN
    row = lambda i, *_: (i, 0)
    grid_spec = pltpu.PrefetchScalarGridSpec(
        num_scalar_prefetch=4,
        grid=(nt,),
        in_specs=[
            pl.BlockSpec((tm, D_MODEL), row),
            pl.BlockSpec((tm, LANES), row),
            pl.BlockSpec((tm, LANES), row),
            pl.BlockSpec((1, 1, LANES), lambda i, *_: (i, 0, 0)),
        ],
        out_specs=pl.BlockSpec(memory_space=pl.ANY),
        scratch_shapes=[pltpu.VMEM((2, lrows, PACK_W), U32), pltpu.VMEM((GROUP_ALIGN, PACK_W), U32),
                        pltpu.SemaphoreType.DMA((3,))],
    )
    return pl.pallas_call(
        _dispatch_body,
        grid_spec=grid_spec,
        out_shape=jax.ShapeDtypeStruct((n_slots, PACK_W), U32),
        compiler_params=pltpu.CompilerParams(dimension_semantics=("arbitrary",)),
        name="moe_dispatch",
    )(gs, ls, n8, tail, h2, idx, rank, lsf)


def _expert_body(be_ref, nu_ref, x_ref, wgu_ref, bgu_ref, wd_ref, bd_ref, y_ref, wgu_b, wd_b):
    j = pl.program_id(0)

    @pl.when(j < nu_ref[0])
    def _():
        @pl.when(jnp.logical_or(j == 0, be_ref[j] != be_ref[jnp.maximum(j - 1, 0)]))
        def _():
            wgu_b[...] = wgu_ref[0].astype(BF16)
            wd_b[...] = wd_ref[0].astype(BF16)

        gu = _dot(jnp.concatenate(_unpack_rows(x_ref[...]), axis=1), wgu_b[...]) + bgu_ref[0]
        gate = jnp.minimum(gu[:, :D_FF], SWIGLU_LIMIT)
        up = jnp.clip(gu[:, D_FF:], -SWIGLU_LIMIT, SWIGLU_LIMIT)
        act = (up + 1.0) * gate * _sigmoid(SWIGLU_ALPHA * gate)
        y_ref[...] = _pack_rows(_dot(act.astype(BF16), wd_b[...]) + bd_ref[0])


def _experts(block_e, n_used, xs, wgu, bgu, wd, bd):
    n_slots = xs.shape[0]
    nb = n_slots // EXPERT_BLOCK

    def blk(j, be, nu):
        return jnp.minimum(j, nu[0] - 1)

    grid_spec = pltpu.PrefetchScalarGridSpec(
        num_scalar_prefetch=2,
        grid=(nb,),
        in_specs=[
            pl.BlockSpec((EXPERT_BLOCK, PACK_W), lambda j, be, nu: (blk(j, be, nu), 0)),
            pl.BlockSpec((1, D_MODEL, 2 * D_FF), lambda j, be, nu: (be[blk(j, be, nu)], 0, 0)),
            pl.BlockSpec((1, 1, 2 * D_FF), lambda j, be, nu: (be[blk(j, be, nu)], 0, 0)),
            pl.BlockSpec((1, D_FF, D_MODEL), lambda j, be, nu: (be[blk(j, be, nu)], 0, 0)),
            pl.BlockSpec((1, 1, D_MODEL), lambda j, be, nu: (be[blk(j, be, nu)], 0, 0)),
        ],
        out_specs=pl.BlockSpec((EXPERT_BLOCK, PACK_W), lambda j, be, nu: (blk(j, be, nu), 0)),
        scratch_shapes=[pltpu.VMEM((D_MODEL, 2 * D_FF), BF16), pltpu.VMEM((D_FF, D_MODEL), BF16)],
    )
    return pl.pallas_call(
        _expert_body,
        grid_spec=grid_spec,
        out_shape=jax.ShapeDtypeStruct((n_slots, PACK_W), U32),
        compiler_params=pltpu.CompilerParams(dimension_semantics=("arbitrary",)),
        name="moe_experts",
    )(block_e, n_used, xs, wgu, bgu, wd, bd)


def _combine_body(gs_ref, ls_ref, n8_ref, x_ref, wt_ref, idx_ref, rank_ref, lsf_ref, y_ref, out_ref,
                  yl, sem):
    i = pl.program_id(0)
    slot = i % 2
    tables = (gs_ref, ls_ref, n8_ref)
    tm = x_ref.shape[0]
    lrows = yl.shape[1]

    def copy_into(s):
        def make_copy(l, g, rows):
            return pltpu.make_async_copy(y_ref.at[pl.ds(g, rows), :], yl.at[s, pl.ds(l, rows), :], sem.at[s])
        return make_copy

    def make_wait(pieces):
        rows = pieces * GROUP_ALIGN
        return pltpu.make_async_copy(y_ref.at[pl.ds(0, rows), :], yl.at[0, pl.ds(0, rows), :], sem.at[slot])

    @pl.when(i == 0)
    def _():
        yl[...] = jnp.zeros_like(yl)
        _start_group_copies(tables, i, copy_into(slot))

    @pl.when(i + 1 < pl.num_programs(0))
    def _():
        _start_group_copies(tables, i + 1, copy_into(1 - slot))

    pos = lax.broadcasted_iota(I32, (tm, lrows), 1)
    lp = _local_positions(idx_ref, rank_ref, lsf_ref)
    wt = wt_ref[...]
    wm = jnp.zeros((tm, lrows), F32)
    for k in range(TOP_K):
        wm = jnp.where(pos == lp[k], wt[:, k:k + 1], wm)
    _wait_pieces(n8_ref[i, TOTAL_LANE], make_wait)
    wmb = wm.astype(BF16)
    y_hi, y_lo = _unpack_rows(yl[slot])
    out_ref[:, :PACK_W] = x_ref[:, :PACK_W] + _dot(wmb, y_hi)
    out_ref[:, PACK_W:] = x_ref[:, PACK_W:] + _dot(wmb, y_lo)


def _combine(gs, ls, n8, x1, wts, idx, rank, lsf, y):
    t = x1.shape[0]
    tm = min(MOE_TILE, t)
    nt = t // tm
    lrows = tm * TOP_K + N_EXPERTS * GROUP_ALIGN
    row = lambda i, *_: (i, 0)
    grid_spec = pltpu.PrefetchScalarGridSpec(
        num_scalar_prefetch=3,
        grid=(nt,),
        in_specs=[
            pl.BlockSpec((tm, D_MODEL), row),
            pl.BlockSpec((tm, LANES), row),
            pl.BlockSpec((tm, LANES), row),
            pl.BlockSpec((tm, LANES), row),
            pl.BlockSpec((1, 1, LANES), lambda i, *_: (i, 0, 0)),
            pl.BlockSpec(memory_space=pl.ANY),
        ],
        out_specs=pl.BlockSpec((tm, D_MODEL), row),
        scratch_shapes=[pltpu.VMEM((2, lrows, PACK_W), U32), pltpu.SemaphoreType.DMA((2,))],
    )
    return pl.pallas_call(
        _combine_body,
        grid_spec=grid_spec,
        out_shape=jax.ShapeDtypeStruct((t, D_MODEL), F32),
        compiler_params=pltpu.CompilerParams(dimension_semantics=("arbitrary",)),
        name="moe_combine",
    )(gs, ls, n8, x1, wts, idx, rank, lsf, y)


def _pad_lanes(v, fill=0.0):
    v = v.astype(F32).reshape(1, -1)
    return jnp.pad(v, ((0, 0), (0, LANES - v.shape[1])), constant_values=fill)


def _mixer(x2, mem, rel_table, attn_norm, w_in, b_gate, dn_conv, dn_a_log, dn_dt_bias, dn_out_norm,
           da_q_norm, da_k_norm, da_lambda, da_subln, mem_norm, w_mem_kv, mx_q_norm, mx_k_norm,
           w_branch, w_out, batch, seq):
    wp = jnp.concatenate([w_in[:, :W_AB_LO], w_in[:, W_AB_HI:]], axis=1).astype(BF16)
    wab = jnp.pad(w_in[:, W_AB_LO:W_AB_HI], ((0, 0), (0, LANES - (W_AB_HI - W_AB_LO))))
    p, ab = _inproj(x2, attn_norm.reshape(1, -1), wp, wab)

    o_dn = _deltanet(p, ab, dn_conv, _pad_lanes(dn_a_log), _pad_lanes(dn_dt_bias),
                     dn_out_norm.reshape(1, -1), batch, seq)

    tq = min(ATT_BLOCK, seq)
    bias = _bias_tiles(rel_table.T, tq)
    o_da = _attention(p, bias, jnp.tile(da_q_norm, 2).reshape(1, -1), jnp.tile(da_k_norm, 2).reshape(1, -1),
                      da_lambda, da_subln.reshape(-1, 1), batch, seq)

    mk, mv = _memkv(mem, mem_norm.reshape(1, -1), w_mem_kv.astype(BF16), mx_k_norm.reshape(1, -1))
    return _merge(x2, o_dn, o_da, p, mk, mv, mx_q_norm.reshape(1, -1), b_gate.reshape(3, D_MODEL),
                  w_branch.astype(BF16), w_out.astype(BF16), seq)


def _moe(x1, ffn_norm, w_router, b_router, w_gate_up, b_gate_up, w_down, b_down):
    t = x1.shape[0]
    nt = t // min(MOE_TILE, t)
    max_rows = t * TOP_K + nt * N_EXPERTS * (GROUP_ALIGN - 1)
    n_blocks = -(-max_rows // EXPERT_BLOCK) + N_EXPERTS
    n_blocks_pad = -(-n_blocks // 8) * 8
    n_slots = n_blocks * EXPERT_BLOCK

    wr = jnp.pad(w_router, ((0, 0), (0, LANES - N_EXPERTS)))
    h2, idx, wts, rank, cnt = _router(x1, ffn_norm.reshape(1, -1), wr, _pad_lanes(b_router, NEG))
    gs, ls, n8, lsf, tail, meta = _plan(cnt[:, 0, :], n_blocks_pad)
    block_e = meta[:n_blocks, 0]
    n_used = meta[0:1, 1]
    lsf = lsf.reshape(nt, 1, LANES)

    xs = _dispatch(gs, ls, n8, tail, h2, idx, rank, lsf, n_slots)
    y = _experts(block_e, n_used, xs, w_gate_up, b_gate_up.reshape(N_EXPERTS, 1, -1),
                 w_down, b_down.reshape(N_EXPERTS, 1, -1))
    return _combine(gs, ls, n8, x1, wts, idx, rank, lsf, y)


def kernel(x, mem, rel_table, attn_norm, w_in, b_gate, dn_conv, dn_a_log, dn_dt_bias, dn_out_norm,
           da_q_norm, da_k_norm, da_lambda, da_subln, mem_norm, w_mem_kv, mx_q_norm, mx_k_norm,
           w_branch, w_out, ffn_norm, w_router, b_router, w_gate_up, b_gate_up, w_down, b_down):
    batch, seq, d = x.shape
    x2 = x.reshape(batch * seq, d)
    x1 = _mixer(x2, mem, rel_table, attn_norm[0], w_in[0], b_gate[0], dn_conv[0], dn_a_log[0],
                dn_dt_bias[0], dn_out_norm[0], da_q_norm[0], da_k_norm[0], da_lambda[0], da_subln[0],
                mem_norm[0], w_mem_kv[0], mx_q_norm[0], mx_k_norm[0], w_branch[0], w_out[0], batch, seq)
    out = _moe(x1, ffn_norm[0], w_router[0], b_router[0], w_gate_up[0], b_gate_up[0], w_down[0],
               b_down[0])
    return out.reshape(batch, seq, d)
```

```python
import functools
import math

import jax
import jax.numpy as jnp
from jax import lax
from jax.experimental import pallas as pl
from jax.experimental.pallas import tpu as pltpu

F32 = jnp.float32
BF16 = jnp.bfloat16
I32 = jnp.int32

D_MODEL = 1024
EPS = 1e-6
LANES = 128

DN_HEADS = 4
DN_DK = 128
DN_CHUNK = 64
DN_CONV = 4

DA_HEADS = 4
DA_DH = 64

MX_HEADS = 4
MX_DH = 128

REL_BUCKETS = 32
REL_MAX_DIST = 128

N_EXPERTS = 32
TOP_K = 4
D_FF = 1024
SWIGLU_LIMIT = 7.0
SWIGLU_ALPHA = 1.702
EXPERT_BLOCK = 512
MOE_TILE = 256
GROUP_ALIGN = 8

LAM_INIT = 0.8 - 0.6 * math.exp(-0.3 * 0)
LOG2E = 1.4426950408889634
NEG = -1e30

P_DNQ, P_DNK, P_DNV, P_DNZ = 0, 512, 1024, 1536
P_DAQ, P_DAK, P_DAV = 2048, 2560, 3072
P_MXQ = 3584
P_GATE = 4096
P_COLS = 7168
W_AB_LO, W_AB_HI = 2048, 2056


def _dot(a, b):
    return jnp.dot(a, b, preferred_element_type=F32)


def _dot_nt(a, b):
    return lax.dot_general(a, b, (((1,), (1,)), ((), ())), preferred_element_type=F32)


def _dot_tn(a, b):
    return lax.dot_general(a, b, (((0,), (0,)), ((), ())), preferred_element_type=F32)


def _split(x):
    hi = x.astype(BF16)
    lo = (x - hi.astype(F32)).astype(BF16)
    return hi, lo


def _dot3(a, b):
    ah, al = _split(a)
    bh, bl = _split(b)
    return _dot(ah, bh) + _dot(ah, bl) + _dot(al, bh)


def _sigmoid(x):
    return 1.0 / (1.0 + jnp.exp(-x))


def _rms(x, n):
    return lax.rsqrt(jnp.sum(x * x, axis=-1, keepdims=True) * (1.0 / n) + EPS)


def _inproj_body(x_ref, g_ref, w_ref, wab_ref, p_ref, ab_ref, h_scr):
    @pl.when(pl.program_id(1) == 0)
    def _():
        x = x_ref[...]
        h = x * _rms(x, D_MODEL) * g_ref[...]
        h_scr[...] = h.astype(BF16)
        ab_ref[...] = _dot3(h, wab_ref[...])

    p_ref[...] = _dot(h_scr[...], w_ref[...]).astype(p_ref.dtype)


def _inproj(x2, gain, wp, wab):
    t = x2.shape[0]
    tm = min(2048, t)
    tn = 1024
    return pl.pallas_call(
        _inproj_body,
        grid=(t // tm, P_COLS // tn),
        in_specs=[
            pl.BlockSpec((tm, D_MODEL), lambda i, j: (i, 0)),
            pl.BlockSpec((1, D_MODEL), lambda i, j: (0, 0)),
            pl.BlockSpec((D_MODEL, tn), lambda i, j: (0, j)),
            pl.BlockSpec((D_MODEL, LANES), lambda i, j: (0, 0)),
        ],
        out_specs=[
            pl.BlockSpec((tm, tn), lambda i, j: (i, j)),
            pl.BlockSpec((tm, LANES), lambda i, j: (i, 0)),
        ],
        out_shape=[
            jax.ShapeDtypeStruct((t, P_COLS), BF16),
            jax.ShapeDtypeStruct((t, LANES), F32),
        ],
        scratch_shapes=[pltpu.VMEM((tm, D_MODEL), BF16)],
        compiler_params=pltpu.CompilerParams(dimension_semantics=("parallel", "arbitrary")),
        name="inproj",
    )(x2, gain, wp, wab)


DN_HALO = 16
DN_SCAN_CHUNK = 256


def _deltanet_body(q_ref, k_ref, v_ref, z_ref, qh_ref, kh_ref, vh_ref, ab_ref, cw_ref, alog_ref,
                   dtb_ref, on_ref, o_ref, stage, qs, ks, vs, s_scr):
    i = pl.program_id(1)
    tc = q_ref.shape[0]
    hw = DN_HEADS * DN_DK

    @pl.when(i == 0)
    def _():
        s_scr[...] = jnp.zeros_like(s_scr)

    for src, halo, dst, off, kind in ((q_ref, qh_ref, qs, 0, "q"), (k_ref, kh_ref, ks, hw, "k"),
                                      (v_ref, vh_ref, vs, 2 * hw, "v")):
        hal = halo[...].astype(F32)
        stage[0:DN_HALO, :] = jnp.where(i == 0, 0.0, hal)
        stage[DN_HALO:DN_HALO + tc, :] = src[...].astype(F32)
        base = DN_HALO - (DN_CONV - 1)
        y = stage[base:base + tc, :] * cw_ref[0:1, off:off + hw]
        for j in range(1, DN_CONV):
            y = y + stage[base + j:base + j + tc, :] * cw_ref[j:j + 1, off:off + hw]
        y = y * _sigmoid(y)
        if kind == "v":
            dst[...] = y
        else:
            for h in range(DN_HEADS):
                sl = slice(h * DN_DK, (h + 1) * DN_DK)
                yh = y[:, sl]
                r = lax.rsqrt(jnp.sum(yh * yh, axis=-1, keepdims=True) + EPS)
                if kind == "q":
                    r = r * (DN_DK ** -0.5)
                dst[:, sl] = yh * r

    c = min(DN_SCAN_CHUNK, tc)
    row = lax.broadcasted_iota(I32, (c, c), 0)
    col = lax.broadcasted_iota(I32, (c, c), 1)
    incl = row >= col
    strict = row > col
    same_blk = (row // DN_CHUNK) == (col // DN_CHUNK)
    tri = jnp.where(incl, 1.0, 0.0).astype(BF16)
    eye = jnp.where(row == col, 1.0, 0.0)
    neg_a = -jnp.exp(alog_ref[...])
    dtb = dtb_ref[...]

    def chunk(ci, carry):
        r0 = pl.multiple_of(ci * c, c)
        abc = ab_ref[pl.ds(r0, c), :]
        a_in = abc + dtb
        g_all = neg_a * (jnp.maximum(a_in, 0.0) + jnp.log(1.0 + jnp.exp(-jnp.abs(a_in))))
        beta_all = _sigmoid(abc)
        zc = z_ref[pl.ds(r0, c), :].astype(F32)
        hs = range(DN_HEADS)
        sls = [slice(h * DN_DK, (h + 1) * DN_DK) for h in hs]
        q = [qs[pl.ds(r0, c), sl] for sl in sls]
        k = [ks[pl.ds(r0, c), sl] for sl in sls]
        v = [vs[pl.ds(r0, c), sl] for sl in sls]
        g = [g_all[:, h:h + 1] for h in hs]
        beta = [beta_all[:, DN_HEADS + h:DN_HEADS + h + 1] for h in hs]
        g_split = [_split(jnp.where(strict, g[h], 0.0)) for h in hs]
        diff = [_dot(tri, g_split[h][0]) + _dot(tri, g_split[h][1]) for h in hs]
        kb = [k[h].astype(BF16) for h in hs]
        qkk = [_dot_nt(jnp.concatenate([q[h].astype(BF16), kb[h]], axis=0), kb[h]) for h in hs]
        gc = [diff[h][:, 0:1] + g[h][0:1, :] for h in hs]
        decay = [jnp.where(incl, jnp.exp(diff[h]), 0.0) for h in hs]
        lower = [jnp.where(strict, qkk[h][c:] * decay[h] * beta[h], 0.0) for h in hs]
        pw = [jnp.where(same_blk, -lower[h], 0.0) for h in hs]
        dinv = [eye + pw[h] for h in hs]
        pwb = [pw[h].astype(BF16) for h in hs]
        for _ in range(int(math.log2(DN_CHUNK)) - 1):
            pwb = [_dot(pwb[h], pwb[h]).astype(BF16) for h in hs]
            dinv = [dinv[h] + _dot(dinv[h].astype(BF16), pwb[h]) for h in hs]
        dinv_b = [dinv[h].astype(BF16) for h in hs]
        pw = [-_dot(dinv_b[h], jnp.where(same_blk, 0.0, lower[h]).astype(BF16)) for h in hs]
        xm = [eye + pw[h] for h in hs]
        pwb = [pw[h].astype(BF16) for h in hs]
        for _ in range(int(math.log2(c // DN_CHUNK)) - 1):
            pwb = [_dot(pwb[h], pwb[h]).astype(BF16) for h in hs]
            xm = [xm[h] + _dot(xm[h].astype(BF16), pwb[h]) for h in hs]
        inv = [_dot(xm[h].astype(BF16), dinv_b[h]).astype(BF16) for h in hs]
        egc = [jnp.exp(gc[h]) for h in hs]
        rhs = [jnp.concatenate([v[h] * beta[h], k[h] * (beta[h] * egc[h])], axis=1).astype(BF16) for h in hs]
        sol = [_dot(inv[h], rhs[h]) for h in hs]
        qkm = [jnp.where(incl, qkk[h][:c] * decay[h], 0.0).astype(BF16) for h in hs]
        gl = [gc[h][c - 1:c, :] for h in hs]
        state = [s_scr[h] for h in hs]
        ws = [_dot(jnp.concatenate([sol[h][:, DN_DK:].astype(BF16), (q[h] * egc[h]).astype(BF16)], axis=0),
                   state[h].astype(BF16)) for h in hs]
        v_new = [sol[h][:, :DN_DK] - ws[h][:c] for h in hs]
        o = [ws[h][c:] + _dot(qkm[h], v_new[h].astype(BF16)) for h in hs]
        for h in hs:
            s_scr[h] = state[h] * jnp.exp(gl[h]) + _dot_tn(kb[h], (v_new[h] * jnp.exp(gl[h] - gc[h])).astype(BF16))
        for h in hs:
            zz = zc[:, sls[h]]
            on = o[h] * _rms(o[h], DN_DK) * on_ref[...]
            o_ref[pl.ds(r0, c), sls[h]] = (on * (zz * _sigmoid(zz))).astype(o_ref.dtype)
        return carry

    lax.fori_loop(0, tc // c, chunk, 0, unroll=True)


def _deltanet(p, ab, conv_w, alog_row, dtb_row, out_norm, batch, seq):
    t = batch * seq
    tc = min(512, seq)
    nt = seq // tc
    hw = DN_HEADS * DN_DK

    def main(cb):
        return pl.BlockSpec((tc, hw), lambda b, i: (b * nt + i, cb))

    def halo(cb):
        return pl.BlockSpec(
            (DN_HALO, hw),
            lambda b, i: (jnp.maximum((b * seq + i * tc) // DN_HALO - 1, 0), cb))

    return pl.pallas_call(
        _deltanet_body,
        grid=(batch, nt),
        in_specs=[
            main(P_DNQ // hw), main(P_DNK // hw), main(P_DNV // hw), main(P_DNZ // hw),
            halo(P_DNQ // hw), halo(P_DNK // hw), halo(P_DNV // hw),
            pl.BlockSpec((tc, LANES), lambda b, i: (b * nt + i, 0)),
            pl.BlockSpec((DN_CONV, 3 * hw), lambda b, i: (0, 0)),
            pl.BlockSpec((1, LANES), lambda b, i: (0, 0)),
            pl.BlockSpec((1, LANES), lambda b, i: (0, 0)),
            pl.BlockSpec((1, DN_DK), lambda b, i: (0, 0)),
        ],
        out_specs=pl.BlockSpec((tc, hw), lambda b, i: (b * nt + i, 0)),
        out_shape=jax.ShapeDtypeStruct((t, hw), BF16),
        scratch_shapes=[
            pltpu.VMEM((DN_HALO + tc, hw), F32),
            pltpu.VMEM((tc, hw), F32),
            pltpu.VMEM((tc, hw), F32),
            pltpu.VMEM((tc, hw), F32),
            pltpu.VMEM((DN_HEADS, DN_DK, DN_DK), F32),
        ],
        compiler_params=pltpu.CompilerParams(dimension_semantics=("parallel", "arbitrary")),
        name="deltanet",
    )(p, p, p, p, p, p, p, ab, conv_w, alog_row, dtb_row, out_norm)


ATT_BLOCK = 512


def _bias_body(tbl_ref, o_ref):
    h = pl.program_id(0)
    tq = o_ref.shape[2]
    key = lax.broadcasted_iota(I32, (tq, tq), 0)
    qry = lax.broadcasted_iota(I32, (tq, tq), 1)
    max_exact = REL_BUCKETS // 2
    far = tbl_ref[h, REL_BUCKETS - 1]
    for d in range(2):
        n = qry - key + d * tq
        nn = jnp.maximum(n, 0)
        nf = jnp.maximum(nn, 1).astype(F32)
        large = max_exact + (jnp.log(nf / max_exact) / math.log(REL_MAX_DIST / max_exact)
                             * (REL_BUCKETS - max_exact)).astype(I32)
        large = jnp.minimum(large, REL_BUCKETS - 1)
        bucket = jnp.where(nn < max_exact, nn, large)
        val = jnp.zeros((tq, tq), F32)
        for b in range(REL_BUCKETS):
            val = jnp.where(bucket == b, tbl_ref[h, b], val)
        o_ref[0, d] = jnp.where(n >= 0, (val - far) * LOG2E, NEG)


def _bias_tiles(tbl_t, tq):
    return pl.pallas_call(
        _bias_body,
        grid=(DA_HEADS,),
        in_specs=[pl.BlockSpec(memory_space=pltpu.SMEM)],
        out_specs=pl.BlockSpec((1, 2, tq, tq), lambda h: (h, 0, 0, 0)),
        out_shape=jax.ShapeDtypeStruct((DA_HEADS, 2, tq, tq), F32),
        name="t5_bias_tiles",
    )(tbl_t)


DA_DV = 2 * DA_DH
DA_VROWS = DA_DV + 16


BOUND_SLACK = 1.02
MAX_SHIFT_GAP = 110.0


def _attn_body(q_ref, qn_ref, k_ref, v_ref, bias_ref, qg_ref, kg_ref, lam_ref, sg_ref, o_ref,
               kn, vt, kst, qc_s, bd_s, flag_s, m_s, acc_s):
    qi = pl.program_id(2)
    slot = qi % 2
    tq = q_ref.shape[0]
    seq = k_ref.shape[0]
    tk = tq
    lo_mask = lax.broadcasted_iota(I32, (1, DA_DV), 1) < DA_DH

    def group_norm(x, gain):
        x2 = x * x
        lo = jnp.sum(jnp.where(lo_mask, x2, 0.0), axis=-1, keepdims=True)
        hi = jnp.sum(jnp.where(lo_mask, 0.0, x2), axis=-1, keepdims=True)
        r = jnp.where(lo_mask, lax.rsqrt(lo * (1.0 / DA_DH) + EPS), lax.rsqrt(hi * (1.0 / DA_DH) + EPS))
        return x * r * gain

    @pl.when(qi == 0)
    def _():
        ones = jnp.ones((DA_VROWS - DA_DV, tk), BF16)

        def body(c, kmax2):
            r0 = pl.multiple_of(c * tk, tk)
            kb = group_norm(k_ref[pl.ds(r0, tk), :].astype(F32), kg_ref[...]).astype(BF16)
            kn[pl.ds(r0, tk), :] = kb
            vt[c, 0:DA_DV, :] = v_ref[pl.ds(r0, tk), :].astype(F32).T.astype(BF16)
            vt[c, DA_DV:DA_VROWS, :] = ones
            k2 = kb.astype(F32)
            k2 = k2 * k2
            lo = jnp.max(jnp.sum(jnp.where(lo_mask, k2, 0.0), axis=-1, keepdims=True), axis=0, keepdims=True)
            hi = jnp.max(jnp.sum(jnp.where(lo_mask, 0.0, k2), axis=-1, keepdims=True), axis=0, keepdims=True)
            return jnp.maximum(kmax2, jnp.where(lo_mask, lo, hi))
        kst[0:1, :] = lax.fori_loop(0, seq // tk, body, jnp.zeros((1, DA_DV), F32), unroll=2)
        b0 = bias_ref[0, 0]
        b1 = bias_ref[0, 1]
        bmax = jnp.maximum(jnp.max(jnp.maximum(b0, b1), axis=0, keepdims=True), 0.0)
        bmin = jnp.minimum(jnp.min(jnp.minimum(jnp.where(b0 > 0.5 * NEG, b0, 0.0), b1), axis=0, keepdims=True), 0.0)
        kst[1:2, :] = jnp.broadcast_to(jnp.max(bmax, axis=1, keepdims=True), (1, DA_DV))
        kst[2:3, :] = jnp.broadcast_to(jnp.min(bmin, axis=1, keepdims=True), (1, DA_DV))

    def prepare(src_ref, s):
        q = group_norm(src_ref[...].astype(F32), qg_ref[...]) * (DA_DH ** -0.5 * LOG2E)
        qc_s[s] = jnp.concatenate([jnp.where(lo_mask, q, 0.0), jnp.where(lo_mask, 0.0, q)], axis=0).astype(BF16)
        q2 = q * q * kst[0:1, :]
        ones8 = jnp.ones((8, DA_DV), BF16)
        bound = []
        for m in range(2):
            q2m = jnp.where(lo_mask, q2, 0.0) if m == 0 else jnp.where(lo_mask, 0.0, q2)
            bound.append(jnp.sqrt(_dot_nt(ones8, q2m.astype(BF16))[0:1, :]) * BOUND_SLACK)
        bound = jnp.concatenate(bound, axis=1)
        bd_s[s] = bound
        worst = jnp.max(2.0 * bound, axis=1, keepdims=True) + kst[1:2, 0:1] - kst[2:3, 0:1]
        flag_s[s] = jnp.where(worst[0, 0] <= MAX_SHIFT_GAP, 1, 0).astype(I32)

    @pl.when(qi == 0)
    def _():
        prepare(q_ref, slot)

    qcat = qc_s[slot]
    bound = bd_s[slot]
    bmax = kst[1:2, 0:1]
    safe = flag_s[slot] == 1
    acc_s[...] = jnp.zeros_like(acc_s)

    def block(j, d):
        r0 = pl.multiple_of(j * tk, tk)
        st = _dot_nt(kn[pl.ds(r0, tk), :], qcat)
        if d is not None:
            bias = bias_ref[0, d]
            st = st + jnp.concatenate([bias, bias], axis=1)
        m_prev = m_s[...]
        m_new = jnp.maximum(m_prev, jnp.max(st, axis=0, keepdims=True))
        alpha = jnp.exp2(m_prev - m_new)
        acc_s[...] = alpha * acc_s[...] + _dot(vt[j], jnp.exp2(st - m_new).astype(BF16))
        m_s[...] = m_new

    def blocks_fixed(js, ds=None):
        sts = [_dot_nt(kn[pl.ds(pl.multiple_of(j * tk, tk), tk), :], qcat) for j in js]
        if ds is not None:
            biases = [bias_ref[0, d] for d in ds]
            sts = [st + jnp.concatenate([b, b], axis=1) for st, b in zip(sts, biases)]
        shift = m_s[...]
        pts = [jnp.exp2(st - shift).astype(BF16) for st in sts]
        tot = _dot(vt[js[0]], pts[0])
        for j, pt in zip(js[1:], pts[1:]):
            tot = tot + _dot(vt[j], pt)
        acc_s[...] = acc_s[...] + tot

    def run(fixed_shift):
        n_far = jnp.maximum(qi - 1, 0)

        if fixed_shift:
            def far_quad(jj, carry):
                blocks_fixed([4 * jj + u for u in range(4)])
                return carry

            lax.fori_loop(0, n_far // 4, far_quad, 0)
            rem = n_far % 4

            @pl.when(rem >= 2)
            def _():
                blocks_fixed([n_far - rem, n_far - rem + 1])

            @pl.when(rem % 2 == 1)
            def _():
                blocks_fixed([n_far - 1])

            @pl.when(qi >= 1)
            def _():
                blocks_fixed([qi - 1, qi], [1, 0])
                prepare(qn_ref, 1 - slot)

            @pl.when(qi == 0)
            def _():
                blocks_fixed([qi], [0])
                prepare(qn_ref, 1 - slot)
        else:
            def far_one(j, carry):
                block(j, None)
                return carry

            lax.fori_loop(0, n_far, far_one, 0)

            @pl.when(qi >= 1)
            def _():
                block(qi - 1, 1)

            block(qi, 0)
            prepare(qn_ref, 1 - slot)

    @pl.when(safe)
    def _():
        m_s[...] = bound + bmax
        run(True)

    @pl.when(jnp.logical_not(safe))
    def _():
        m_s[...] = jnp.full(m_s.shape, NEG, F32)
        run(False)

    lam_p = lam_ref[...]
    lam = (jnp.exp(jnp.sum(lam_p[0:1, :] * lam_p[1:2, :], axis=-1, keepdims=True))
           - jnp.exp(jnp.sum(lam_p[2:3, :] * lam_p[3:4, :], axis=-1, keepdims=True)) + LAM_INIT)
    a0 = acc_s[:, 0:tq]
    a1 = acc_s[:, tq:2 * tq]
    ot = a0[0:DA_DV] / a0[DA_DV:DA_DV + 1] - lam * (a1[0:DA_DV] / a1[DA_DV:DA_DV + 1])
    r = lax.rsqrt(jnp.sum(ot * ot, axis=0, keepdims=True) * (1.0 / DA_DV) + EPS)
    ot = ot * r * (sg_ref[...] * (1.0 - LAM_INIT))
    o_ref[...] = ot.T.astype(o_ref.dtype)


def _attention(p, bias, qg, kg, lam_p, subln, batch, seq):
    t = batch * seq
    tq = min(ATT_BLOCK, seq)
    nq = seq // tq
    dv = DA_DV
    return pl.pallas_call(
        _attn_body,
        grid=(batch, DA_HEADS, nq),
        in_specs=[
            pl.BlockSpec((tq, dv), lambda b, h, i: (b * nq + i, P_DAQ // dv + h)),
            pl.BlockSpec((tq, dv), lambda b, h, i: (b * nq + jnp.minimum(i + 1, nq - 1), P_DAQ // dv + h)),
            pl.BlockSpec((seq, dv), lambda b, h, i: (b, P_DAK // dv + h)),
            pl.BlockSpec((seq, dv), lambda b, h, i: (b, P_DAV // dv + h)),
            pl.BlockSpec((1, 2, tq, tq), lambda b, h, i: (h, 0, 0, 0)),
            pl.BlockSpec((1, dv), lambda b, h, i: (0, 0)),
            pl.BlockSpec((1, dv), lambda b, h, i: (0, 0)),
            pl.BlockSpec((4, DA_DH), lambda b, h, i: (0, 0)),
            pl.BlockSpec((dv, 1), lambda b, h, i: (0, 0)),
        ],
        out_specs=pl.BlockSpec((tq, dv), lambda b, h, i: (b * nq + i, h)),
        out_shape=jax.ShapeDtypeStruct((t, DA_HEADS * dv), BF16),
        scratch_shapes=[
            pltpu.VMEM((seq, dv), BF16),
            pltpu.VMEM((seq // tq, DA_VROWS, tq), BF16),
            pltpu.VMEM((8, dv), F32),
            pltpu.VMEM((2, 2 * tq, dv), BF16),
            pltpu.VMEM((2, 1, 2 * tq), F32),
            pltpu.SMEM((2,), I32),
            pltpu.VMEM((1, 2 * tq), F32),
            pltpu.VMEM((DA_VROWS, 2 * tq), F32),
        ],
        compiler_params=pltpu.CompilerParams(dimension_semantics=("parallel", "parallel", "arbitrary")),
        name="diff_attention",
    )(p, p, p, p, bias, qg, kg, lam_p, subln)


def _memkv_body(mem_ref, mg_ref, w_ref, kg_ref, mk_ref, mv_ref):
    x = mem_ref[0]
    xn = x * _rms(x, D_MODEL) * mg_ref[...]
    kv = _dot(xn.astype(BF16), w_ref[...])
    hw = MX_HEADS * MX_DH
    for h in range(MX_HEADS):
        sl = slice(h * MX_DH, (h + 1) * MX_DH)
        kh = kv[:, sl]
        mk_ref[0, :, sl] = (kh * _rms(kh, MX_DH) * kg_ref[...]).astype(BF16)
    mv_ref[0] = kv[:, hw:].astype(BF16)


def _memkv(mem, mem_norm, w_kv, k_norm):
    b, n, _ = mem.shape
    hw = MX_HEADS * MX_DH
    return pl.pallas_call(
        _memkv_body,
        grid=(b,),
        in_specs=[
            pl.BlockSpec((1, n, D_MODEL), lambda i: (i, 0, 0)),
            pl.BlockSpec((1, D_MODEL), lambda i: (0, 0)),
            pl.BlockSpec((D_MODEL, 2 * hw), lambda i: (0, 0)),
            pl.BlockSpec((1, MX_DH), lambda i: (0, 0)),
        ],
        out_specs=[pl.BlockSpec((1, n, hw), lambda i: (i, 0, 0))] * 2,
        out_shape=[jax.ShapeDtypeStruct((b, n, hw), BF16)] * 2,
        name="memory_kv",
    )(mem, mem_norm, w_kv, k_norm)


def _merge_body(x_ref, odn_ref, oda_ref, mxq_ref, g0_ref, g1_ref, g2_ref, mk_ref, mv_ref, qg_ref,
                bg_ref, wb_ref, wo_ref, out_ref, omx):
    for h in range(MX_HEADS):
        sl = slice(h * MX_DH, (h + 1) * MX_DH)
        qh = mxq_ref[:, sl].astype(F32)
        qh = qh * _rms(qh, MX_DH) * qg_ref[...] * (MX_DH ** -0.5 * LOG2E)
        s = _dot_nt(qh.astype(BF16), mk_ref[0, :, sl])
        p = jnp.exp2(s - jnp.max(s, axis=-1, keepdims=True))
        oh = _dot(p.astype(BF16), mv_ref[0, :, sl]) / jnp.sum(p, axis=-1, keepdims=True)
        omx[:, sl] = oh.astype(BF16)
    y = None
    for r, (o_r, g_r) in enumerate(((odn_ref, g0_ref), (oda_ref, g1_ref), (omx, g2_ref))):
        gate = _sigmoid(g_r[...].astype(F32) + bg_ref[r:r + 1, :])
        term = gate * _dot(o_r[...], wb_ref[r])
        y = term if y is None else y + term
    out_ref[...] = x_ref[...] + _dot(y.astype(BF16), wo_ref[...])


def _merge(x2, o_dn, o_da, p, mk, mv, q_norm, b_gate, w_branch, w_out, seq):
    t = x2.shape[0]
    tm = min(1024, seq)
    nt = seq // tm
    bw = 512
    n_mem = mk.shape[1]
    return pl.pallas_call(
        _merge_body,
        grid=(t // tm,),
        in_specs=[
            pl.BlockSpec((tm, D_MODEL), lambda i: (i, 0)),
            pl.BlockSpec((tm, bw), lambda i: (i, 0)),
            pl.BlockSpec((tm, bw), lambda i: (i, 0)),
            pl.BlockSpec((tm, bw), lambda i: (i, P_MXQ // bw)),
            pl.BlockSpec((tm, D_MODEL), lambda i: (i, P_GATE // D_MODEL)),
            pl.BlockSpec((tm, D_MODEL), lambda i: (i, P_GATE // D_MODEL + 1)),
            pl.BlockSpec((tm, D_MODEL), lambda i: (i, P_GATE // D_MODEL + 2)),
            pl.BlockSpec((1, n_mem, bw), lambda i: (i // nt, 0, 0)),
            pl.BlockSpec((1, n_mem, bw), lambda i: (i // nt, 0, 0)),
            pl.BlockSpec((1, MX_DH), lambda i: (0, 0)),
            pl.BlockSpec((3, D_MODEL), lambda i: (0, 0)),
            pl.BlockSpec((3, bw, D_MODEL), lambda i: (0, 0, 0)),
            pl.BlockSpec((D_MODEL, D_MODEL), lambda i: (0, 0)),
        ],
        out_specs=pl.BlockSpec((tm, D_MODEL), lambda i: (i, 0)),
        out_shape=jax.ShapeDtypeStruct((t, D_MODEL), F32),
        scratch_shapes=[pltpu.VMEM((tm, bw), BF16)],
        compiler_params=pltpu.CompilerParams(dimension_semantics=("parallel",)),
        name="merge",
    )(x2, o_dn, o_da, p, p, p, p, mk, mv, q_norm, b_gate, w_branch, w_out)


def _router_body(x_ref, g_ref, wr_ref, br_ref, h_ref, idx_ref, wt_ref, rank_ref, cnt_ref):
    tm = x_ref.shape[0]
    x = x_ref[...]
    h = x * _rms(x, D_MODEL) * g_ref[...]
    h_ref[...] = h.astype(BF16)
    logits = _dot3(h, wr_ref[...]) + br_ref[...]
    lane = lax.broadcasted_iota(I32, (tm, LANES), 1)
    lane_f = lane.astype(F32)
    work = logits
    sel = jnp.zeros((tm, LANES), F32)
    vals, idxs = [], []
    for _ in range(TOP_K):
        mx = jnp.max(work, axis=-1, keepdims=True)
        ik = jnp.min(jnp.where(work == mx, lane_f, float(LANES)), axis=-1, keepdims=True)
        hit = lane_f == ik
        sel = jnp.where(hit, 1.0, sel)
        work = jnp.where(hit, -jnp.inf, work)
        vals.append(mx)
        idxs.append(ik)
    es = [jnp.exp(v - vals[0]) for v in vals]
    den = es[0] + es[1] + es[2] + es[3]
    sub = cnt_ref.shape[0]
    mt = tm // sub
    r = lax.broadcasted_iota(I32, (tm, tm), 0)
    c = lax.broadcasted_iota(I32, (tm, tm), 1)
    tril = jnp.where((r > c) & (r // mt == c // mt), 1.0, 0.0).astype(BF16)
    cum = _dot(tril, sel.astype(BF16))
    idx_o = jnp.zeros((tm, LANES), F32)
    wt_o = jnp.zeros((tm, LANES), F32)
    rank_o = jnp.zeros((tm, LANES), F32)
    for k in range(TOP_K):
        rk = jnp.sum(jnp.where(lane_f == idxs[k], cum, 0.0), axis=-1, keepdims=True)
        idx_o = jnp.where(lane == k, idxs[k], idx_o)
        wt_o = jnp.where(lane == k, es[k] / den, wt_o)
        rank_o = jnp.where(lane == k, rk, rank_o)
    idx_ref[...] = idx_o.astype(I32)
    wt_ref[...] = wt_o
    rank_ref[...] = rank_o.astype(I32)
    for s in range(sub):
        cnt_ref[s] = jnp.broadcast_to(jnp.sum(sel[s * mt:(s + 1) * mt], axis=0, keepdims=True), (8, LANES))


def _router(x1, gain, w_r, b_r):
    t = x1.shape[0]
    mt = min(MOE_TILE, t)
    sub = 2 if t % (2 * mt) == 0 else 1
    tm = sub * mt
    row = lambda i: (i, 0)
    fixed = lambda i: (0, 0)
    return pl.pallas_call(
        _router_body,
        grid=(t // tm,),
        in_specs=[
            pl.BlockSpec((tm, D_MODEL), row),
            pl.BlockSpec((1, D_MODEL), fixed),
            pl.BlockSpec((D_MODEL, LANES), fixed),
            pl.BlockSpec((1, LANES), fixed),
        ],
        out_specs=[
            pl.BlockSpec((tm, D_MODEL), row),
            pl.BlockSpec((tm, LANES), row),
            pl.BlockSpec((tm, LANES), row),
            pl.BlockSpec((tm, LANES), row),
            pl.BlockSpec((sub, 8, LANES), lambda i: (i, 0, 0)),
        ],
        out_shape=[
            jax.ShapeDtypeStruct((t, D_MODEL), BF16),
            jax.ShapeDtypeStruct((t, LANES), I32),
            jax.ShapeDtypeStruct((t, LANES), F32),
            jax.ShapeDtypeStruct((t, LANES), I32),
            jax.ShapeDtypeStruct((t // mt, 8, LANES), F32),
        ],
        compiler_params=pltpu.CompilerParams(dimension_semantics=("arbitrary",)),
        name="router",
    )(x1, gain, w_r, b_r)


def _lane_cumsum(x):
    lane = lax.broadcasted_iota(I32, x.shape, 1)
    s = 1
    while s < N_EXPERTS:
        x = x + jnp.where(lane >= s, pltpu.roll(x, s, axis=1), 0.0)
        s *= 2
    return x


def _plan_body(cnt_ref, gs_ref, ls_ref, n8_ref, lsf_ref, tail_ref, meta_ref):
    nt = cnt_ref.shape[0]
    ga = float(GROUP_ALIGN)
    eb = float(EXPERT_BLOCK)
    lane = lax.broadcasted_iota(I32, (nt, LANES), 1)
    r8 = jnp.where(lane < N_EXPERTS, jnp.floor((cnt_ref[...] + (ga - 1.0)) * (1.0 / ga)) * ga, 0.0)
    ri = lax.broadcasted_iota(I32, (nt, nt), 0)
    ci = lax.broadcasted_iota(I32, (nt, nt), 1)
    before = _dot(jnp.where(ri > ci, 1.0, 0.0).astype(BF16), r8.astype(BF16))
    tot = jnp.sum(r8, axis=0, keepdims=True)
    region = jnp.floor((tot + (eb - 1.0)) * (1.0 / eb)) * eb
    pends = _lane_cumsum(jnp.broadcast_to(region, (8, LANES)))[0:1, :]
    pstart = pends - region
    lstart = _lane_cumsum(r8) - r8
    gs_ref[...] = (pstart + before).astype(I32)
    ls_ref[...] = lstart.astype(I32)
    pieces = r8 * (1.0 / ga)
    n8_ref[...] = jnp.where(lane == TOTAL_LANE, jnp.sum(pieces, axis=1, keepdims=True), pieces).astype(I32)
    lsf_ref[...] = lstart
    row8 = lax.broadcasted_iota(I32, (8, LANES), 0)
    tail_n = (region - tot) * (1.0 / ga)
    lane8 = lax.broadcasted_iota(I32, (8, LANES), 1)
    tail_n = jnp.where(lane8 == TOTAL_LANE, jnp.sum(tail_n, axis=1, keepdims=True), tail_n)
    tail = jnp.where(row8 == 0, pstart + tot, jnp.where(row8 == 1, tail_n, 0.0))
    tail_ref[...] = tail.astype(I32)
    nb = meta_ref.shape[0]
    ln = lax.broadcasted_iota(I32, (nb, LANES), 1)
    blk = lax.broadcasted_iota(I32, (nb, LANES), 0).astype(F32) * eb
    be = jnp.sum(jnp.where((ln < N_EXPERTS) & (pends <= blk), 1.0, 0.0), axis=-1, keepdims=True)
    be = jnp.minimum(be, float(N_EXPERTS - 1))
    used = jnp.sum(jnp.where(ln == N_EXPERTS - 1, pends, 0.0), axis=-1, keepdims=True) * (1.0 / eb)
    meta_ref[...] = jnp.where(ln == 0, be, jnp.where(ln == 1, used, 0.0)).astype(I32)


def _plan(cnt, n_blocks_pad):
    nt = cnt.shape[0]
    shp = jax.ShapeDtypeStruct((nt, LANES), I32)
    return pl.pallas_call(
        _plan_body,
        out_shape=[shp, shp, shp, jax.ShapeDtypeStruct((nt, LANES), F32),
                   jax.ShapeDtypeStruct((8, LANES), I32),
                   jax.ShapeDtypeStruct((n_blocks_pad, LANES), I32)],
        name="dispatch_plan",
    )(cnt)


def _local_positions(idx_ref, rank_ref, lsf_ref):
    tm = idx_ref.shape[0]
    lane = lax.broadcasted_iota(I32, (tm, LANES), 1)
    idx = idx_ref[...]
    rank = rank_ref[...].astype(F32)
    ls_row = lsf_ref[0]
    out = []
    for k in range(TOP_K):
        base = jnp.sum(jnp.where(lane == idx[:, k:k + 1], ls_row, 0.0), axis=-1, keepdims=True)
        out.append((base + rank[:, k:k + 1]).astype(I32))
    return out


def _local_positions_lanes(idx_ref, rank_ref, lsf_ref):
    tm = idx_ref.shape[0]
    idx_t = idx_ref[...].astype(F32).T
    rank_t = rank_ref[...].astype(F32).T
    ls_col = jnp.broadcast_to(lsf_ref[0], (LANES, LANES)).T[:, 0:1]
    expert = lax.broadcasted_iota(I32, (LANES, tm), 0).astype(F32)
    out = []
    for k in range(TOP_K):
        base = jnp.sum(jnp.where(expert == idx_t[k:k + 1, :], ls_col, 0.0), axis=0, keepdims=True)
        out.append((base + rank_t[k:k + 1, :]).astype(I32))
    return out


PACK_W = D_MODEL // 2
U32 = jnp.uint32


def _pack_rows(x):
    xb = x.astype(BF16).astype(F32)
    hi = lax.bitcast_convert_type(xb[:, :PACK_W], U32)
    lo = lax.bitcast_convert_type(xb[:, PACK_W:], U32)
    return hi | (lo >> 16)


def _unpack_rows(w):
    hi = lax.bitcast_convert_type(w & jnp.uint32(0xFFFF0000), F32)
    lo = lax.bitcast_convert_type(w << 16, F32)
    return hi.astype(BF16), lo.astype(BF16)


TOTAL_LANE = LANES - 1
WAIT_CHUNK = 128
BIG_PIECE = 4


def _start_group_copies(tables, tile, make_copy):
    gs_ref, ls_ref, n8_ref = tables

    def group(e, carry):
        g0 = gs_ref[tile, e]
        l0 = ls_ref[tile, e]
        n = n8_ref[tile, e]
        nbig = n // BIG_PIECE

        def big(j, c):
            off = j * (BIG_PIECE * GROUP_ALIGN)
            make_copy(pl.multiple_of(l0 + off, GROUP_ALIGN), pl.multiple_of(g0 + off, GROUP_ALIGN),
                      BIG_PIECE * GROUP_ALIGN).start()
            return c

        def one(j, c):
            off = j * GROUP_ALIGN
            make_copy(pl.multiple_of(l0 + off, GROUP_ALIGN), pl.multiple_of(g0 + off, GROUP_ALIGN),
                      GROUP_ALIGN).start()
            return c

        lax.fori_loop(0, nbig, big, 0)
        lax.fori_loop(nbig * BIG_PIECE, n, one, 0)
        return carry

    lax.fori_loop(0, N_EXPERTS, group, 0)


def _wait_pieces(n, make_wait):
    def chunk(j, c):
        make_wait(WAIT_CHUNK).wait()
        return c

    lax.fori_loop(0, n // WAIT_CHUNK, chunk, 0)
    b = WAIT_CHUNK // 2
    while b >= 1:
        def _(b=b):
            make_wait(b).wait()
        pl.when((n & b) != 0)(_)
        b //= 2


def _dispatch_body(gs_ref, ls_ref, n8_ref, tail_ref, h_ref, idx_ref, rank_ref, lsf_ref, xs_ref,
                   xl, zbuf, sem):
    i = pl.program_id(0)
    last = i == pl.num_programs(0) - 1
    slot = i % 2
    tables = (gs_ref, ls_ref, n8_ref)
    tm = h_ref.shape[0]
    lrows = xl.shape[1]
    pos = lax.broadcasted_iota(I32, (lrows, tm), 0)
    lp = _local_positions_lanes(idx_ref, rank_ref, lsf_ref)
    hit = pos == lp[0]
    for k in range(1, TOP_K):
        hit = hit | (pos == lp[k])
    xl[slot] = _pack_rows(_dot(jnp.where(hit, 1.0, 0.0).astype(BF16), h_ref[...]))

    def copy_from(s):
        def make_copy(l, g, rows):
            return pltpu.make_async_copy(xl.at[s, pl.ds(l, rows), :], xs_ref.at[pl.ds(g, rows), :], sem.at[s])
        return make_copy

    def wait_on(s):
        def make_wait(pieces):
            rows = pieces * GROUP_ALIGN
            return pltpu.make_async_copy(xl.at[0, pl.ds(0, rows), :], xs_ref.at[pl.ds(0, rows), :], sem.at[s])
        return make_wait

    @pl.when(i >= 1)
    def _():
        _wait_pieces(n8_ref[i - 1, TOTAL_LANE], wait_on(1 - slot))

    _start_group_copies(tables, i, copy_from(slot))

    def start_tails():
        def per_expert(e, carry):
            g0 = tail_ref[0, e]

            def one(j, c):
                pltpu.make_async_copy(
                    zbuf, xs_ref.at[pl.ds(pl.multiple_of(g0 + j * GROUP_ALIGN, GROUP_ALIGN), GROUP_ALIGN), :],
                    sem.at[2]).start()
                return c

            lax.fori_loop(0, tail_ref[1, e], one, 0)
            return carry

        lax.fori_loop(0, N_EXPERTS, per_expert, 0)

    @pl.when(last)
    def _():
        zbuf[...] = jnp.zeros_like(zbuf)
        start_tails()
        _wait_pieces(n8_ref[i, TOTAL_LANE], wait_on(slot))
        _wait_pieces(tail_ref[1, TOTAL_LANE], wait_on(2))


def _dispatch(gs, ls, n8, tail, h2, idx, rank, lsf, n_slots):
    t = h2.shape[0]
    tm = min(MOE_TILE, t)
    nt = t // tm
    lrows = tm * TOP_K + N_EXPERTS * GROUP_ALIGN
    row = lambda i, *_: (i, 0)
    grid_spec = pltpu.PrefetchScalarGridSpec(
        num_scalar_prefetch=4,
        grid=(nt,),
        in_specs=[
            pl.BlockSpec((tm, D_MODEL), row),
            pl.BlockSpec((tm, LANES), row),
            pl.BlockSpec((tm, LANES), row),
            pl.BlockSpec((1, 1, LANES), lambda i, *_: (i, 0, 0)),
        ],
        out_specs=pl.BlockSpec(memory_space=pl.ANY),
        scratch_shapes=[pltpu.VMEM((2, lrows, PACK_W), U32), pltpu.VMEM((GROUP_ALIGN, PACK_W), U32),
                        pltpu.SemaphoreType.DMA((3,))],
    )
    return pl.pallas_call(
        _dispatch_body,
        grid_spec=grid_spec,
        out_shape=jax.ShapeDtypeStruct((n_slots, PACK_W), U32),
        compiler_params=pltpu.CompilerParams(dimension_semantics=("arbitrary",)),
        name="moe_dispatch",
    )(gs, ls, n8, tail, h2, idx, rank, lsf)


def _expert_body(be_ref, nu_ref, x_ref, wgu_ref, bgu_ref, wd_ref, bd_ref, y_ref, wgu_b, wd_b):
    j = pl.program_id(0)

    @pl.when(j < nu_ref[0])
    def _():
        @pl.when(jnp.logical_or(j == 0, be_ref[j] != be_ref[jnp.maximum(j - 1, 0)]))
        def _():
            wgu_b[...] = wgu_ref[0].astype(BF16)
            wd_b[...] = wd_ref[0].astype(BF16)

        gu = _dot(jnp.concatenate(_unpack_rows(x_ref[...]), axis=1), wgu_b[...]) + bgu_ref[0]
        gate = jnp.minimum(gu[:, :D_FF], SWIGLU_LIMIT)
        up = jnp.clip(gu[:, D_FF:], -SWIGLU_LIMIT, SWIGLU_LIMIT)
        act = (up + 1.0) * gate * _sigmoid(SWIGLU_ALPHA * gate)
        y_ref[...] = _pack_rows(_dot(act.astype(BF16), wd_b[...]) + bd_ref[0])


def _experts(block_e, n_used, xs, wgu, bgu, wd, bd):
    n_slots = xs.shape[0]
    nb = n_slots // EXPERT_BLOCK

    def blk(j, be, nu):
        return jnp.minimum(j, nu[0] - 1)

    grid_spec = pltpu.PrefetchScalarGridSpec(
        num_scalar_prefetch=2,
        grid=(nb,),
        in_specs=[
            pl.BlockSpec((EXPERT_BLOCK, PACK_W), lambda j, be, nu: (blk(j, be, nu), 0)),
            pl.BlockSpec((1, D_MODEL, 2 * D_FF), lambda j, be, nu: (be[blk(j, be, nu)], 0, 0)),
            pl.BlockSpec((1, 1, 2 * D_FF), lambda j, be, nu: (be[blk(j, be, nu)], 0, 0)),
            pl.BlockSpec((1, D_FF, D_MODEL), lambda j, be, nu: (be[blk(j, be, nu)], 0, 0)),
            pl.BlockSpec((1, 1, D_MODEL), lambda j, be, nu: (be[blk(j, be, nu)], 0, 0)),
        ],
        out_specs=pl.BlockSpec((EXPERT_BLOCK, PACK_W), lambda j, be, nu: (blk(j, be, nu), 0)),
        scratch_shapes=[pltpu.VMEM((D_MODEL, 2 * D_FF), BF16), pltpu.VMEM((D_FF, D_MODEL), BF16)],
    )
    return pl.pallas_call(
        _expert_body,
        grid_spec=grid_spec,
        out_shape=jax.ShapeDtypeStruct((n_slots, PACK_W), U32),
        compiler_params=pltpu.CompilerParams(dimension_semantics=("arbitrary",)),
        name="moe_experts",
    )(block_e, n_used, xs, wgu, bgu, wd, bd)


def _combine_body(gs_ref, ls_ref, n8_ref, x_ref, wt_ref, idx_ref, rank_ref, lsf_ref, y_ref, out_ref,
                  yl, sem):
    i = pl.program_id(0)
    slot = i % 2
    tables = (gs_ref, ls_ref, n8_ref)
    tm = x_ref.shape[0]
    lrows = yl.shape[1]

    def copy_into(s):
        def make_copy(l, g, rows):
            return pltpu.make_async_copy(y_ref.at[pl.ds(g, rows), :], yl.at[s, pl.ds(l, rows), :], sem.at[s])
        return make_copy

    def make_wait(pieces):
        rows = pieces * GROUP_ALIGN
        return pltpu.make_async_copy(y_ref.at[pl.ds(0, rows), :], yl.at[0, pl.ds(0, rows), :], sem.at[slot])

    @pl.when(i == 0)
    def _():
        yl[...] = jnp.zeros_like(yl)
        _start_group_copies(tables, i, copy_into(slot))

    @pl.when(i + 1 < pl.num_programs(0))
    def _():
        _start_group_copies(tables, i + 1, copy_into(1 - slot))

    pos = lax.broadcasted_iota(I32, (tm, lrows), 1)
    lp = _local_positions(idx_ref, rank_ref, lsf_ref)
    wt = wt_ref[...]
    wm = jnp.zeros((tm, lrows), F32)
    for k in range(TOP_K):
        wm = jnp.where(pos == lp[k], wt[:, k:k + 1], wm)
    _wait_pieces(n8_ref[i, TOTAL_LANE], make_wait)
    wmb = wm.astype(BF16)
    y_hi, y_lo = _unpack_rows(yl[slot])
    out_ref[:, :PACK_W] = x_ref[:, :PACK_W] + _dot(wmb, y_hi)
    out_ref[:, PACK_W:] = x_ref[:, PACK_W:] + _dot(wmb, y_lo)


def _combine(gs, ls, n8, x1, wts, idx, rank, lsf, y):
    t = x1.shape[0]
    tm = min(MOE_TILE, t)
    nt = t // tm
    lrows = tm * TOP_K + N_EXPERTS * GROUP_ALIGN
    row = lambda i, *_: (i, 0)
    grid_spec = pltpu.PrefetchScalarGridSpec(
        num_scalar_prefetch=3,
        grid=(nt,),
        in_specs=[
            pl.BlockSpec((tm, D_MODEL), row),
            pl.BlockSpec((tm, LANES), row),
            pl.BlockSpec((tm, LANES), row),
            pl.BlockSpec((tm, LANES), row),
            pl.BlockSpec((1, 1, LANES), lambda i, *_: (i, 0, 0)),
            pl.BlockSpec(memory_space=pl.ANY),
        ],
        out_specs=pl.BlockSpec((tm, D_MODEL), row),
        scratch_shapes=[pltpu.VMEM((2, lrows, PACK_W), U32), pltpu.SemaphoreType.DMA((2,))],
    )
    return pl.pallas_call(
        _combine_body,
        grid_spec=grid_spec,
        out_shape=jax.ShapeDtypeStruct((t, D_MODEL), F32),
        compiler_params=pltpu.CompilerParams(dimension_semantics=("arbitrary",)),
        name="moe_combine",
    )(gs, ls, n8, x1, wts, idx, rank, lsf, y)


def _pad_lanes(v, fill=0.0):
    v = v.astype(F32).reshape(1, -1)
    return jnp.pad(v, ((0, 0), (0, LANES - v.shape[1])), constant_values=fill)


def _mixer(x2, mem, rel_table, attn_norm, w_in, b_gate, dn_conv, dn_a_log, dn_dt_bias, dn_out_norm,
           da_q_norm, da_k_norm, da_lambda, da_subln, mem_norm, w_mem_kv, mx_q_norm, mx_k_norm,
           w_branch, w_out, batch, seq):
    wp = jnp.concatenate([w_in[:, :W_AB_LO], w_in[:, W_AB_HI:]], axis=1).astype(BF16)
    wab = jnp.pad(w_in[:, W_AB_LO:W_AB_HI], ((0, 0), (0, LANES - (W_AB_HI - W_AB_LO))))
    p, ab = _inproj(x2, attn_norm.reshape(1, -1), wp, wab)

    o_dn = _deltanet(p, ab, dn_conv, _pad_lanes(dn_a_log), _pad_lanes(dn_dt_bias),
                     dn_out_norm.reshape(1, -1), batch, seq)

    tq = min(ATT_BLOCK, seq)
    bias = _bias_tiles(rel_table.T, tq)
    o_da = _attention(p, bias, jnp.tile(da_q_norm, 2).reshape(1, -1), jnp.tile(da_k_norm, 2).reshape(1, -1),
                      da_lambda, da_subln.reshape(-1, 1), batch, seq)

    mk, mv = _memkv(mem, mem_norm.reshape(1, -1), w_mem_kv.astype(BF16), mx_k_norm.reshape(1, -1))
    return _merge(x2, o_dn, o_da, p, mk, mv, mx_q_norm.reshape(1, -1), b_gate.reshape(3, D_MODEL),
                  w_branch.astype(BF16), w_out.astype(BF16), seq)


def _moe(x1, ffn_norm, w_router, b_router, w_gate_up, b_gate_up, w_down, b_down):
    t = x1.shape[0]
    nt = t // min(MOE_TILE, t)
    max_rows = t * TOP_K + nt * N_EXPERTS * (GROUP_ALIGN - 1)
    n_blocks = -(-max_rows // EXPERT_BLOCK) + N_EXPERTS
    n_blocks_pad = -(-n_blocks // 8) * 8
    n_slots = n_blocks * EXPERT_BLOCK

    wr = jnp.pad(w_router, ((0, 0), (0, LANES - N_EXPERTS)))
    h2, idx, wts, rank, cnt = _router(x1, ffn_norm.reshape(1, -1), wr, _pad_lanes(b_router, NEG))
    gs, ls, n8, lsf, tail, meta = _plan(cnt[:, 0, :], n_blocks_pad)
    block_e = meta[:n_blocks, 0]
    n_used = meta[0:1, 1]
    lsf = lsf.reshape(nt, 1, LANES)

    xs = _dispatch(gs, ls, n8, tail, h2, idx, rank, lsf, n_slots)
    y = _experts(block_e, n_used, xs, w_gate_up, b_gate_up.reshape(N_EXPERTS, 1, -1),
                 w_down, b_down.reshape(N_EXPERTS, 1, -1))
    return _combine(gs, ls, n8, x1, wts, idx, rank, lsf, y)


def kernel(x, mem, rel_table, attn_norm, w_in, b_gate, dn_conv, dn_a_log, dn_dt_bias, dn_out_norm,
           da_q_norm, da_k_norm, da_lambda, da_subln, mem_norm, w_mem_kv, mx_q_norm, mx_k_norm,
           w_branch, w_out, ffn_norm, w_router, b_router, w_gate_up, b_gate_up, w_down, b_down):
    batch, seq, d = x.shape
    x2 = x.reshape(batch * seq, d)
    x1 = _mixer(x2, mem, rel_table, attn_norm[0], w_in[0], b_gate[0], dn_conv[0], dn_a_log[0],
                dn_dt_bias[0], dn_out_norm[0], da_q_norm[0], da_k_norm[0], da_lambda[0], da_subln[0],
                mem_norm[0], w_mem_kv[0], mx_q_norm[0], mx_k_norm[0], w_branch[0], w_out[0], batch, seq)
    out = _moe(x1, ffn_norm[0], w_router[0], b_router[0], w_gate_up[0], b_gate_up[0], w_down[0],
               b_down[0])
    return out.reshape(batch, seq, d)
```

```python
import functools
import math

import jax
import jax.numpy as jnp
from jax import lax
from jax.experimental import pallas as pl
from jax.experimental.pallas import tpu as pltpu

F32 = jnp.float32
BF16 = jnp.bfloat16
I32 = jnp.int32

D_MODEL = 1024
EPS = 1e-6
LANES = 128

DN_HEADS = 4
DN_DK = 128
DN_CHUNK = 64
DN_CONV = 4

DA_HEADS = 4
DA_DH = 64

MX_HEADS = 4
MX_DH = 128

REL_BUCKETS = 32
REL_MAX_DIST = 128

N_EXPERTS = 32
TOP_K = 4
D_FF = 1024
SWIGLU_LIMIT = 7.0
SWIGLU_ALPHA = 1.702
EXPERT_BLOCK = 512
MOE_TILE = 256
GROUP_ALIGN = 8

LAM_INIT = 0.8 - 0.6 * math.exp(-0.3 * 0)
LOG2E = 1.4426950408889634
NEG = -1e30

P_DNQ, P_DNK, P_DNV, P_DNZ = 0, 512, 1024, 1536
P_DAQ, P_DAK, P_DAV = 2048, 2560, 3072
P_MXQ = 3584
P_GATE = 4096
P_COLS = 7168
W_AB_LO, W_AB_HI = 2048, 2056


def _dot(a, b):
    return jnp.dot(a, b, preferred_element_type=F32)


def _dot_nt(a, b):
    return lax.dot_general(a, b, (((1,), (1,)), ((), ())), preferred_element_type=F32)


def _dot_tn(a, b):
    return lax.dot_general(a, b, (((0,), (0,)), ((), ())), preferred_element_type=F32)


def _split(x):
    hi = x.astype(BF16)
    lo = (x - hi.astype(F32)).astype(BF16)
    return hi, lo


def _dot3(a, b):
    ah, al = _split(a)
    bh, bl = _split(b)
    return _dot(ah, bh) + _dot(ah, bl) + _dot(al, bh)


def _sigmoid(x):
    return 1.0 / (1.0 + jnp.exp(-x))


def _rms(x, n):
    return lax.rsqrt(jnp.sum(x * x, axis=-1, keepdims=True) * (1.0 / n) + EPS)


def _inproj_body(x_ref, g_ref, w_ref, wab_ref, p_ref, ab_ref, h_scr):
    @pl.when(pl.program_id(1) == 0)
    def _():
        x = x_ref[...]
        h = x * _rms(x, D_MODEL) * g_ref[...]
        h_scr[...] = h.astype(BF16)
        ab_ref[...] = _dot3(h, wab_ref[...])

    p_ref[...] = _dot(h_scr[...], w_ref[...]).astype(p_ref.dtype)


def _inproj(x2, gain, wp, wab):
    t = x2.shape[0]
    tm = min(2048, t)
    tn = 1024
    return pl.pallas_call(
        _inproj_body,
        grid=(t // tm, P_COLS // tn),
        in_specs=[
            pl.BlockSpec((tm, D_MODEL), lambda i, j: (i, 0)),
            pl.BlockSpec((1, D_MODEL), lambda i, j: (0, 0)),
            pl.BlockSpec((D_MODEL, tn), lambda i, j: (0, j)),
            pl.BlockSpec((D_MODEL, LANES), lambda i, j: (0, 0)),
        ],
        out_specs=[
            pl.BlockSpec((tm, tn), lambda i, j: (i, j)),
            pl.BlockSpec((tm, LANES), lambda i, j: (i, 0)),
        ],
        out_shape=[
            jax.ShapeDtypeStruct((t, P_COLS), BF16),
            jax.ShapeDtypeStruct((t, LANES), F32),
        ],
        scratch_shapes=[pltpu.VMEM((tm, D_MODEL), BF16)],
        compiler_params=pltpu.CompilerParams(dimension_semantics=("parallel", "arbitrary")),
        name="inproj",
    )(x2, gain, wp, wab)


DN_HALO = 16
DN_SCAN_CHUNK = 256


def _deltanet_body(q_ref, k_ref, v_ref, z_ref, qh_ref, kh_ref, vh_ref, ab_ref, cw_ref, alog_ref,
                   dtb_ref, on_ref, o_ref, stage, qs, ks, vs, s_scr):
    i = pl.program_id(1)
    tc = q_ref.shape[0]
    hw = DN_HEADS * DN_DK

    @pl.when(i == 0)
    def _():
        s_scr[...] = jnp.zeros_like(s_scr)

    for src, halo, dst, off, kind in ((q_ref, qh_ref, qs, 0, "q"), (k_ref, kh_ref, ks, hw, "k"),
                                      (v_ref, vh_ref, vs, 2 * hw, "v")):
        hal = halo[...].astype(F32)
        stage[0:DN_HALO, :] = jnp.where(i == 0, 0.0, hal)
        stage[DN_HALO:DN_HALO + tc, :] = src[...].astype(F32)
        base = DN_HALO - (DN_CONV - 1)
        y = stage[base:base + tc, :] * cw_ref[0:1, off:off + hw]
        for j in range(1, DN_CONV):
            y = y + stage[base + j:base + j + tc, :] * cw_ref[j:j + 1, off:off + hw]
        y = y * _sigmoid(y)
        if kind == "v":
            dst[...] = y
        else:
            for h in range(DN_HEADS):
                sl = slice(h * DN_DK, (h + 1) * DN_DK)
                yh = y[:, sl]
                r = lax.rsqrt(jnp.sum(yh * yh, axis=-1, keepdims=True) + EPS)
                if kind == "q":
                    r = r * (DN_DK ** -0.5)
                dst[:, sl] = yh * r

    c = min(DN_SCAN_CHUNK, tc)
    row = lax.broadcasted_iota(I32, (c, c), 0)
    col = lax.broadcasted_iota(I32, (c, c), 1)
    incl = row >= col
    strict = row > col
    same_blk = (row // DN_CHUNK) == (col // DN_CHUNK)
    tri = jnp.where(incl, 1.0, 0.0).astype(BF16)
    eye = jnp.where(row == col, 1.0, 0.0)
    neg_a = -jnp.exp(alog_ref[...])
    dtb = dtb_ref[...]

    def chunk(ci, carry):
        r0 = pl.multiple_of(ci * c, c)
        abc = ab_ref[pl.ds(r0, c), :]
        a_in = abc + dtb
        g_all = neg_a * (jnp.maximum(a_in, 0.0) + jnp.log(1.0 + jnp.exp(-jnp.abs(a_in))))
        beta_all = _sigmoid(abc)
        zc = z_ref[pl.ds(r0, c), :].astype(F32)
        hs = range(DN_HEADS)
        sls = [slice(h * DN_DK, (h + 1) * DN_DK) for h in hs]
        q = [qs[pl.ds(r0, c), sl] for sl in sls]
        k = [ks[pl.ds(r0, c), sl] for sl in sls]
        v = [vs[pl.ds(r0, c), sl] for sl in sls]
        g = [g_all[:, h:h + 1] for h in hs]
        beta = [beta_all[:, DN_HEADS + h:DN_HEADS + h + 1] for h in hs]
        g_split = [_split(jnp.where(strict, g[h], 0.0)) for h in hs]
        diff = [_dot(tri, g_split[h][0]) + _dot(tri, g_split[h][1]) for h in hs]
        kb = [k[h].astype(BF16) for h in hs]
        qkk = [_dot_nt(jnp.concatenate([q[h].astype(BF16), kb[h]], axis=0), kb[h]) for h in hs]
        gc = [diff[h][:, 0:1] + g[h][0:1, :] for h in hs]
        decay = [jnp.where(incl, jnp.exp(diff[h]), 0.0) for h in hs]
        lower = [jnp.where(strict, qkk[h][c:] * decay[h] * beta[h], 0.0) for h in hs]
        pw = [jnp.where(same_blk, -lower[h], 0.0) for h in hs]
        dinv = [eye + pw[h] for h in hs]
        pwb = [pw[h].astype(BF16) for h in hs]
        for _ in range(int(math.log2(DN_CHUNK)) - 1):
            pwb = [_dot(pwb[h], pwb[h]).astype(BF16) for h in hs]
            dinv = [dinv[h] + _dot(dinv[h].astype(BF16), pwb[h]) for h in hs]
        dinv_b = [dinv[h].astype(BF16) for h in hs]
        pw = [-_dot(dinv_b[h], jnp.where(same_blk, 0.0, lower[h]).astype(BF16)) for h in hs]
        xm = [eye + pw[h] for h in hs]
        pwb = [pw[h].astype(BF16) for h in hs]
        for _ in range(int(math.log2(c // DN_CHUNK)) - 1):
            pwb = [_dot(pwb[h], pwb[h]).astype(BF16) for h in hs]
            xm = [xm[h] + _dot(xm[h].astype(BF16), pwb[h]) for h in hs]
        inv = [_dot(xm[h].astype(BF16), dinv_b[h]).astype(BF16) for h in hs]
        egc = [jnp.exp(gc[h]) for h in hs]
        rhs = [jnp.concatenate([v[h] * beta[h], k[h] * (beta[h] * egc[h])], axis=1).astype(BF16) for h in hs]
        sol = [_dot(inv[h], rhs[h]) for h in hs]
        qkm = [jnp.where(incl, qkk[h][:c] * decay[h], 0.0).astype(BF16) for h in hs]
        gl = [gc[h][c - 1:c, :] for h in hs]
        state = [s_scr[h] for h in hs]
        ws = [_dot(jnp.concatenate([sol[h][:, DN_DK:].astype(BF16), (q[h] * egc[h]).astype(BF16)], axis=0),
                   state[h].astype(BF16)) for h in hs]
        v_new = [sol[h][:, :DN_DK] - ws[h][:c] for h in hs]
        o = [ws[h][c:] + _dot(qkm[h], v_new[h].astype(BF16)) for h in hs]
        for h in hs:
            s_scr[h] = state[h] * jnp.exp(gl[h]) + _dot_tn(kb[h], (v_new[h] * jnp.exp(gl[h] - gc[h])).astype(BF16))
        for h in hs:
            zz = zc[:, sls[h]]
            on = o[h] * _rms(o[h], DN_DK) * on_ref[...]
            o_ref[pl.ds(r0, c), sls[h]] = (on * (zz * _sigmoid(zz))).astype(o_ref.dtype)
        return carry

    lax.fori_loop(0, tc // c, chunk, 0, unroll=True)


def _deltanet(p, ab, conv_w, alog_row, dtb_row, out_norm, batch, seq):
    t = batch * seq
    tc = min(1024, seq)
    nt = seq // tc
    hw = DN_HEADS * DN_DK

    def main(cb):
        return pl.BlockSpec((tc, hw), lambda b, i: (b * nt + i, cb))

    def halo(cb):
        return pl.BlockSpec(
            (DN_HALO, hw),
            lambda b, i: (jnp.maximum((b * seq + i * tc) // DN_HALO - 1, 0), cb))

    return pl.pallas_call(
        _deltanet_body,
        grid=(batch, nt),
        in_specs=[
            main(P_DNQ // hw), main(P_DNK // hw), main(P_DNV // hw), main(P_DNZ // hw),
            halo(P_DNQ // hw), halo(P_DNK // hw), halo(P_DNV // hw),
            pl.BlockSpec((tc, LANES), lambda b, i: (b * nt + i, 0)),
            pl.BlockSpec((DN_CONV, 3 * hw), lambda b, i: (0, 0)),
            pl.BlockSpec((1, LANES), lambda b, i: (0, 0)),
            pl.BlockSpec((1, LANES), lambda b, i: (0, 0)),
            pl.BlockSpec((1, DN_DK), lambda b, i: (0, 0)),
        ],
        out_specs=pl.BlockSpec((tc, hw), lambda b, i: (b * nt + i, 0)),
        out_shape=jax.ShapeDtypeStruct((t, hw), BF16),
        scratch_shapes=[
            pltpu.VMEM((DN_HALO + tc, hw), F32),
            pltpu.VMEM((tc, hw), F32),
            pltpu.VMEM((tc, hw), F32),
            pltpu.VMEM((tc, hw), F32),
            pltpu.VMEM((DN_HEADS, DN_DK, DN_DK), F32),
        ],
        compiler_params=pltpu.CompilerParams(dimension_semantics=("parallel", "arbitrary")),
        name="deltanet",
    )(p, p, p, p, p, p, p, ab, conv_w, alog_row, dtb_row, out_norm)


ATT_BLOCK = 512


def _bias_body(tbl_ref, o_ref):
    h = pl.program_id(0)
    tq = o_ref.shape[2]
    key = lax.broadcasted_iota(I32, (tq, tq), 0)
    qry = lax.broadcasted_iota(I32, (tq, tq), 1)
    max_exact = REL_BUCKETS // 2
    far = tbl_ref[h, REL_BUCKETS - 1]
    for d in range(2):
        n = qry - key + d * tq
        nn = jnp.maximum(n, 0)
        nf = jnp.maximum(nn, 1).astype(F32)
        large = max_exact + (jnp.log(nf / max_exact) / math.log(REL_MAX_DIST / max_exact)
                             * (REL_BUCKETS - max_exact)).astype(I32)
        large = jnp.minimum(large, REL_BUCKETS - 1)
        bucket = jnp.where(nn < max_exact, nn, large)
        val = jnp.zeros((tq, tq), F32)
        for b in range(REL_BUCKETS):
            val = jnp.where(bucket == b, tbl_ref[h, b], val)
        o_ref[0, d] = jnp.where(n >= 0, (val - far) * LOG2E, NEG)


def _bias_tiles(tbl_t, tq):
    return pl.pallas_call(
        _bias_body,
        grid=(DA_HEADS,),
        in_specs=[pl.BlockSpec(memory_space=pltpu.SMEM)],
        out_specs=pl.BlockSpec((1, 2, tq, tq), lambda h: (h, 0, 0, 0)),
        out_shape=jax.ShapeDtypeStruct((DA_HEADS, 2, tq, tq), F32),
        name="t5_bias_tiles",
    )(tbl_t)


DA_DV = 2 * DA_DH
DA_VROWS = DA_DV + 16


BOUND_SLACK = 1.02
MAX_SHIFT_GAP = 110.0


def _attn_body(q_ref, qn_ref, k_ref, v_ref, bias_ref, qg_ref, kg_ref, lam_ref, sg_ref, o_ref,
               kn, vt, kst, qc_s, bd_s, flag_s, m_s, acc_s):
    qi = pl.program_id(2)
    slot = qi % 2
    tq = q_ref.shape[0]
    seq = k_ref.shape[0]
    tk = tq
    lo_mask = lax.broadcasted_iota(I32, (1, DA_DV), 1) < DA_DH

    def group_norm(x, gain):
        x2 = x * x
        lo = jnp.sum(jnp.where(lo_mask, x2, 0.0), axis=-1, keepdims=True)
        hi = jnp.sum(jnp.where(lo_mask, 0.0, x2), axis=-1, keepdims=True)
        r = jnp.where(lo_mask, lax.rsqrt(lo * (1.0 / DA_DH) + EPS), lax.rsqrt(hi * (1.0 / DA_DH) + EPS))
        return x * r * gain

    @pl.when(qi == 0)
    def _():
        ones = jnp.ones((DA_VROWS - DA_DV, tk), BF16)

        def body(c, kmax2):
            r0 = pl.multiple_of(c * tk, tk)
            kb = group_norm(k_ref[pl.ds(r0, tk), :].astype(F32), kg_ref[...]).astype(BF16)
            kn[pl.ds(r0, tk), :] = kb
            vt[c, 0:DA_DV, :] = v_ref[pl.ds(r0, tk), :].astype(F32).T.astype(BF16)
            vt[c, DA_DV:DA_VROWS, :] = ones
            k2 = kb.astype(F32)
            k2 = k2 * k2
            lo = jnp.max(jnp.sum(jnp.where(lo_mask, k2, 0.0), axis=-1, keepdims=True), axis=0, keepdims=True)
            hi = jnp.max(jnp.sum(jnp.where(lo_mask, 0.0, k2), axis=-1, keepdims=True), axis=0, keepdims=True)
            return jnp.maximum(kmax2, jnp.where(lo_mask, lo, hi))
        kst[0:1, :] = lax.fori_loop(0, seq // tk, body, jnp.zeros((1, DA_DV), F32), unroll=2)
        b0 = bias_ref[0, 0]
        b1 = bias_ref[0, 1]
        bmax = jnp.maximum(jnp.max(jnp.maximum(b0, b1), axis=0, keepdims=True), 0.0)
        bmin = jnp.minimum(jnp.min(jnp.minimum(jnp.where(b0 > 0.5 * NEG, b0, 0.0), b1), axis=0, keepdims=True), 0.0)
        kst[1:2, :] = jnp.broadcast_to(jnp.max(bmax, axis=1, keepdims=True), (1, DA_DV))
        kst[2:3, :] = jnp.broadcast_to(jnp.min(bmin, axis=1, keepdims=True), (1, DA_DV))

    def prepare(src_ref, s):
        q = group_norm(src_ref[...].astype(F32), qg_ref[...]) * (DA_DH ** -0.5 * LOG2E)
        qc_s[s] = jnp.concatenate([jnp.where(lo_mask, q, 0.0), jnp.where(lo_mask, 0.0, q)], axis=0).astype(BF16)
        q2 = q * q * kst[0:1, :]
        ones8 = jnp.ones((8, DA_DV), BF16)
        bound = []
        for m in range(2):
            q2m = jnp.where(lo_mask, q2, 0.0) if m == 0 else jnp.where(lo_mask, 0.0, q2)
            bound.append(jnp.sqrt(_dot_nt(ones8, q2m.astype(BF16))[0:1, :]) * BOUND_SLACK)
        bound = jnp.concatenate(bound, axis=1)
        bd_s[s] = bound
        worst = jnp.max(2.0 * bound, axis=1, keepdims=True) + kst[1:2, 0:1] - kst[2:3, 0:1]
        flag_s[s] = jnp.where(worst[0, 0] <= MAX_SHIFT_GAP, 1, 0).astype(I32)

    @pl.when(qi == 0)
    def _():
        prepare(q_ref, slot)

    qcat = qc_s[slot]
    bound = bd_s[slot]
    bmax = kst[1:2, 0:1]
    safe = flag_s[slot] == 1
    acc_s[...] = jnp.zeros_like(acc_s)

    def block(j, d):
        r0 = pl.multiple_of(j * tk, tk)
        st = _dot_nt(kn[pl.ds(r0, tk), :], qcat)
        if d is not None:
            bias = bias_ref[0, d]
            st = st + jnp.concatenate([bias, bias], axis=1)
        m_prev = m_s[...]
        m_new = jnp.maximum(m_prev, jnp.max(st, axis=0, keepdims=True))
        alpha = jnp.exp2(m_prev - m_new)
        acc_s[...] = alpha * acc_s[...] + _dot(vt[j], jnp.exp2(st - m_new).astype(BF16))
        m_s[...] = m_new

    def blocks_fixed(js, ds=None):
        sts = [_dot_nt(kn[pl.ds(pl.multiple_of(j * tk, tk), tk), :], qcat) for j in js]
        if ds is not None:
            biases = [bias_ref[0, d] for d in ds]
            sts = [st + jnp.concatenate([b, b], axis=1) for st, b in zip(sts, biases)]
        shift = m_s[...]
        pts = [jnp.exp2(st - shift).astype(BF16) for st in sts]
        tot = _dot(vt[js[0]], pts[0])
        for j, pt in zip(js[1:], pts[1:]):
            tot = tot + _dot(vt[j], pt)
        acc_s[...] = acc_s[...] + tot

    def run(fixed_shift):
        n_far = jnp.maximum(qi - 1, 0)

        if fixed_shift:
            def far_quad(jj, carry):
                blocks_fixed([4 * jj + u for u in range(4)])
                return carry

            lax.fori_loop(0, n_far // 4, far_quad, 0)
            rem = n_far % 4

            @pl.when(rem >= 2)
            def _():
                blocks_fixed([n_far - rem, n_far - rem + 1])

            @pl.when(rem % 2 == 1)
            def _():
                blocks_fixed([n_far - 1])

            @pl.when(qi >= 1)
            def _():
                blocks_fixed([qi - 1, qi], [1, 0])
                prepare(qn_ref, 1 - slot)

            @pl.when(qi == 0)
            def _():
                blocks_fixed([qi], [0])
                prepare(qn_ref, 1 - slot)
        else:
            def far_one(j, carry):
                block(j, None)
                return carry

            lax.fori_loop(0, n_far, far_one, 0)

            @pl.when(qi >= 1)
            def _():
                block(qi - 1, 1)

            block(qi, 0)
            prepare(qn_ref, 1 - slot)

    @pl.when(safe)
    def _():
        m_s[...] = bound + bmax
        run(True)

    @pl.when(jnp.logical_not(safe))
    def _():
        m_s[...] = jnp.full(m_s.shape, NEG, F32)
        run(False)

    lam_p = lam_ref[...]
    lam = (jnp.exp(jnp.sum(lam_p[0:1, :] * lam_p[1:2, :], axis=-1, keepdims=True))
           - jnp.exp(jnp.sum(lam_p[2:3, :] * lam_p[3:4, :], axis=-1, keepdims=True)) + LAM_INIT)
    a0 = acc_s[:, 0:tq]
    a1 = acc_s[:, tq:2 * tq]
    ot = a0[0:DA_DV] / a0[DA_DV:DA_DV + 1] - lam * (a1[0:DA_DV] / a1[DA_DV:DA_DV + 1])
    r = lax.rsqrt(jnp.sum(ot * ot, axis=0, keepdims=True) * (1.0 / DA_DV) + EPS)
    ot = ot * r * (sg_ref[...] * (1.0 - LAM_INIT))
    o_ref[...] = ot.T.astype(o_ref.dtype)


def _attention(p, bias, qg, kg, lam_p, subln, batch, seq):
    t = batch * seq
    tq = min(ATT_BLOCK, seq)
    nq = seq // tq
    dv = DA_DV
    return pl.pallas_call(
        _attn_body,
        grid=(batch, DA_HEADS, nq),
        in_specs=[
            pl.BlockSpec((tq, dv), lambda b, h, i: (b * nq + i, P_DAQ // dv + h)),
            pl.BlockSpec((tq, dv), lambda b, h, i: (b * nq + jnp.minimum(i + 1, nq - 1), P_DAQ // dv + h)),
            pl.BlockSpec((seq, dv), lambda b, h, i: (b, P_DAK // dv + h)),
            pl.BlockSpec((seq, dv), lambda b, h, i: (b, P_DAV // dv + h)),
            pl.BlockSpec((1, 2, tq, tq), lambda b, h, i: (h, 0, 0, 0)),
            pl.BlockSpec((1, dv), lambda b, h, i: (0, 0)),
            pl.BlockSpec((1, dv), lambda b, h, i: (0, 0)),
            pl.BlockSpec((4, DA_DH), lambda b, h, i: (0, 0)),
            pl.BlockSpec((dv, 1), lambda b, h, i: (0, 0)),
        ],
        out_specs=pl.BlockSpec((tq, dv), lambda b, h, i: (b * nq + i, h)),
        out_shape=jax.ShapeDtypeStruct((t, DA_HEADS * dv), BF16),
        scratch_shapes=[
            pltpu.VMEM((seq, dv), BF16),
            pltpu.VMEM((seq // tq, DA_VROWS, tq), BF16),
            pltpu.VMEM((8, dv), F32),
            pltpu.VMEM((2, 2 * tq, dv), BF16),
            pltpu.VMEM((2, 1, 2 * tq), F32),
            pltpu.SMEM((2,), I32),
            pltpu.VMEM((1, 2 * tq), F32),
            pltpu.VMEM((DA_VROWS, 2 * tq), F32),
        ],
        compiler_params=pltpu.CompilerParams(dimension_semantics=("parallel", "parallel", "arbitrary")),
        name="diff_attention",
    )(p, p, p, p, bias, qg, kg, lam_p, subln)


def _memkv_body(mem_ref, mg_ref, w_ref, kg_ref, mk_ref, mv_ref):
    x = mem_ref[0]
    xn = x * _rms(x, D_MODEL) * mg_ref[...]
    kv = _dot(xn.astype(BF16), w_ref[...])
    hw = MX_HEADS * MX_DH
    for h in range(MX_HEADS):
        sl = slice(h * MX_DH, (h + 1) * MX_DH)
        kh = kv[:, sl]
        mk_ref[0, :, sl] = (kh * _rms(kh, MX_DH) * kg_ref[...]).astype(BF16)
    mv_ref[0] = kv[:, hw:].astype(BF16)


def _memkv(mem, mem_norm, w_kv, k_norm):
    b, n, _ = mem.shape
    hw = MX_HEADS * MX_DH
    return pl.pallas_call(
        _memkv_body,
        grid=(b,),
        in_specs=[
            pl.BlockSpec((1, n, D_MODEL), lambda i: (i, 0, 0)),
            pl.BlockSpec((1, D_MODEL), lambda i: (0, 0)),
            pl.BlockSpec((D_MODEL, 2 * hw), lambda i: (0, 0)),
            pl.BlockSpec((1, MX_DH), lambda i: (0, 0)),
        ],
        out_specs=[pl.BlockSpec((1, n, hw), lambda i: (i, 0, 0))] * 2,
        out_shape=[jax.ShapeDtypeStruct((b, n, hw), BF16)] * 2,
        name="memory_kv",
    )(mem, mem_norm, w_kv, k_norm)


def _merge_body(x_ref, odn_ref, oda_ref, mxq_ref, g0_ref, g1_ref, g2_ref, mk_ref, mv_ref, qg_ref,
                bg_ref, wb_ref, wo_ref, out_ref, omx):
    for h in range(MX_HEADS):
        sl = slice(h * MX_DH, (h + 1) * MX_DH)
        qh = mxq_ref[:, sl].astype(F32)
        qh = qh * _rms(qh, MX_DH) * qg_ref[...] * (MX_DH ** -0.5 * LOG2E)
        s = _dot_nt(qh.astype(BF16), mk_ref[0, :, sl])
        p = jnp.exp2(s - jnp.max(s, axis=-1, keepdims=True))
        oh = _dot(p.astype(BF16), mv_ref[0, :, sl]) / jnp.sum(p, axis=-1, keepdims=True)
        omx[:, sl] = oh.astype(BF16)
    y = None
    for r, (o_r, g_r) in enumerate(((odn_ref, g0_ref), (oda_ref, g1_ref), (omx, g2_ref))):
        gate = _sigmoid(g_r[...].astype(F32) + bg_ref[r:r + 1, :])
        term = gate * _dot(o_r[...], wb_ref[r])
        y = term if y is None else y + term
    out_ref[...] = x_ref[...] + _dot(y.astype(BF16), wo_ref[...])


def _merge(x2, o_dn, o_da, p, mk, mv, q_norm, b_gate, w_branch, w_out, seq):
    t = x2.shape[0]
    tm = min(1024, seq)
    nt = seq // tm
    bw = 512
    n_mem = mk.shape[1]
    return pl.pallas_call(
        _merge_body,
        grid=(t // tm,),
        in_specs=[
            pl.BlockSpec((tm, D_MODEL), lambda i: (i, 0)),
            pl.BlockSpec((tm, bw), lambda i: (i, 0)),
            pl.BlockSpec((tm, bw), lambda i: (i, 0)),
            pl.BlockSpec((tm, bw), lambda i: (i, P_MXQ // bw)),
            pl.BlockSpec((tm, D_MODEL), lambda i: (i, P_GATE // D_MODEL)),
            pl.BlockSpec((tm, D_MODEL), lambda i: (i, P_GATE // D_MODEL + 1)),
            pl.BlockSpec((tm, D_MODEL), lambda i: (i, P_GATE // D_MODEL + 2)),
            pl.BlockSpec((1, n_mem, bw), lambda i: (i // nt, 0, 0)),
            pl.BlockSpec((1, n_mem, bw), lambda i: (i // nt, 0, 0)),
            pl.BlockSpec((1, MX_DH), lambda i: (0, 0)),
            pl.BlockSpec((3, D_MODEL), lambda i: (0, 0)),
            pl.BlockSpec((3, bw, D_MODEL), lambda i: (0, 0, 0)),
            pl.BlockSpec((D_MODEL, D_MODEL), lambda i: (0, 0)),
        ],
        out_specs=pl.BlockSpec((tm, D_MODEL), lambda i: (i, 0)),
        out_shape=jax.ShapeDtypeStruct((t, D_MODEL), F32),
        scratch_shapes=[pltpu.VMEM((tm, bw), BF16)],
        compiler_params=pltpu.CompilerParams(dimension_semantics=("parallel",)),
        name="merge",
    )(x2, o_dn, o_da, p, p, p, p, mk, mv, q_norm, b_gate, w_branch, w_out)


def _router_body(x_ref, g_ref, wr_ref, br_ref, h_ref, idx_ref, wt_ref, rank_ref, cnt_ref):
    tm = x_ref.shape[0]
    x = x_ref[...]
    h = x * _rms(x, D_MODEL) * g_ref[...]
    h_ref[...] = h.astype(BF16)
    logits = _dot3(h, wr_ref[...]) + br_ref[...]
    lane = lax.broadcasted_iota(I32, (tm, LANES), 1)
    lane_f = lane.astype(F32)
    work = logits
    sel = jnp.zeros((tm, LANES), F32)
    vals, idxs = [], []
    for _ in range(TOP_K):
        mx = jnp.max(work, axis=-1, keepdims=True)
        ik = jnp.min(jnp.where(work == mx, lane_f, float(LANES)), axis=-1, keepdims=True)
        hit = lane_f == ik
        sel = jnp.where(hit, 1.0, sel)
        work = jnp.where(hit, -jnp.inf, work)
        vals.append(mx)
        idxs.append(ik)
    es = [jnp.exp(v - vals[0]) for v in vals]
    den = es[0] + es[1] + es[2] + es[3]
    sub = cnt_ref.shape[0]
    mt = tm // sub
    r = lax.broadcasted_iota(I32, (tm, tm), 0)
    c = lax.broadcasted_iota(I32, (tm, tm), 1)
    tril = jnp.where((r > c) & (r // mt == c // mt), 1.0, 0.0).astype(BF16)
    cum = _dot(tril, sel.astype(BF16))
    idx_o = jnp.zeros((tm, LANES), F32)
    wt_o = jnp.zeros((tm, LANES), F32)
    rank_o = jnp.zeros((tm, LANES), F32)
    for k in range(TOP_K):
        rk = jnp.sum(jnp.where(lane_f == idxs[k], cum, 0.0), axis=-1, keepdims=True)
        idx_o = jnp.where(lane == k, idxs[k], idx_o)
        wt_o = jnp.where(lane == k, es[k] / den, wt_o)
        rank_o = jnp.where(lane == k, rk, rank_o)
    idx_ref[...] = idx_o.astype(I32)
    wt_ref[...] = wt_o
    rank_ref[...] = rank_o.astype(I32)
    for s in range(sub):
        cnt_ref[s] = jnp.broadcast_to(jnp.sum(sel[s * mt:(s + 1) * mt], axis=0, keepdims=True), (8, LANES))


def _router(x1, gain, w_r, b_r):
    t = x1.shape[0]
    mt = min(MOE_TILE, t)
    sub = 2 if t % (2 * mt) == 0 else 1
    tm = sub * mt
    row = lambda i: (i, 0)
    fixed = lambda i: (0, 0)
    return pl.pallas_call(
        _router_body,
        grid=(t // tm,),
        in_specs=[
            pl.BlockSpec((tm, D_MODEL), row),
            pl.BlockSpec((1, D_MODEL), fixed),
            pl.BlockSpec((D_MODEL, LANES), fixed),
            pl.BlockSpec((1, LANES), fixed),
        ],
        out_specs=[
            pl.BlockSpec((tm, D_MODEL), row),
            pl.BlockSpec((tm, LANES), row),
            pl.BlockSpec((tm, LANES), row),
            pl.BlockSpec((tm, LANES), row),
            pl.BlockSpec((sub, 8, LANES), lambda i: (i, 0, 0)),
        ],
        out_shape=[
            jax.ShapeDtypeStruct((t, D_MODEL), BF16),
            jax.ShapeDtypeStruct((t, LANES), I32),
            jax.ShapeDtypeStruct((t, LANES), F32),
            jax.ShapeDtypeStruct((t, LANES), I32),
            jax.ShapeDtypeStruct((t // mt, 8, LANES), F32),
        ],
        compiler_params=pltpu.CompilerParams(dimension_semantics=("arbitrary",)),
        name="router",
    )(x1, gain, w_r, b_r)


def _lane_cumsum(x):
    lane = lax.broadcasted_iota(I32, x.shape, 1)
    s = 1
    while s < N_EXPERTS:
        x = x + jnp.where(lane >= s, pltpu.roll(x, s, axis=1), 0.0)
        s *= 2
    return x


def _plan_body(cnt_ref, gs_ref, ls_ref, n8_ref, lsf_ref, tail_ref, meta_ref):
    nt = cnt_ref.shape[0]
    ga = float(GROUP_ALIGN)
    eb = float(EXPERT_BLOCK)
    lane = lax.broadcasted_iota(I32, (nt, LANES), 1)
    r8 = jnp.where(lane < N_EXPERTS, jnp.floor((cnt_ref[...] + (ga - 1.0)) * (1.0 / ga)) * ga, 0.0)
    ri = lax.broadcasted_iota(I32, (nt, nt), 0)
    ci = lax.broadcasted_iota(I32, (nt, nt), 1)
    before = _dot(jnp.where(ri > ci, 1.0, 0.0).astype(BF16), r8.astype(BF16))
    tot = jnp.sum(r8, axis=0, keepdims=True)
    region = jnp.floor((tot + (eb - 1.0)) * (1.0 / eb)) * eb
    pends = _lane_cumsum(jnp.broadcast_to(region, (8, LANES)))[0:1, :]
    pstart = pends - region
    lstart = _lane_cumsum(r8) - r8
    gs_ref[...] = (pstart + before).astype(I32)
    ls_ref[...] = lstart.astype(I32)
    pieces = r8 * (1.0 / ga)
    n8_ref[...] = jnp.where(lane == TOTAL_LANE, jnp.sum(pieces, axis=1, keepdims=True), pieces).astype(I32)
    lsf_ref[...] = lstart
    row8 = lax.broadcasted_iota(I32, (8, LANES), 0)
    tail_n = (region - tot) * (1.0 / ga)
    lane8 = lax.broadcasted_iota(I32, (8, LANES), 1)
    tail_n = jnp.where(lane8 == TOTAL_LANE, jnp.sum(tail_n, axis=1, keepdims=True), tail_n)
    tail = jnp.where(row8 == 0, pstart + tot, jnp.where(row8 == 1, tail_n, 0.0))
    tail_ref[...] = tail.astype(I32)
    nb = meta_ref.shape[0]
    ln = lax.broadcasted_iota(I32, (nb, LANES), 1)
    blk = lax.broadcasted_iota(I32, (nb, LANES), 0).astype(F32) * eb
    be = jnp.sum(jnp.where((ln < N_EXPERTS) & (pends <= blk), 1.0, 0.0), axis=-1, keepdims=True)
    be = jnp.minimum(be, float(N_EXPERTS - 1))
    used = jnp.sum(jnp.where(ln == N_EXPERTS - 1, pends, 0.0), axis=-1, keepdims=True) * (1.0 / eb)
    meta_ref[...] = jnp.where(ln == 0, be, jnp.where(ln == 1, used, 0.0)).astype(I32)


def _plan(cnt, n_blocks_pad):
    nt = cnt.shape[0]
    shp = jax.ShapeDtypeStruct((nt, LANES), I32)
    return pl.pallas_call(
        _plan_body,
        out_shape=[shp, shp, shp, jax.ShapeDtypeStruct((nt, LANES), F32),
                   jax.ShapeDtypeStruct((8, LANES), I32),
                   jax.ShapeDtypeStruct((n_blocks_pad, LANES), I32)],
        name="dispatch_plan",
    )(cnt)


def _local_positions(idx_ref, rank_ref, lsf_ref):
    tm = idx_ref.shape[0]
    lane = lax.broadcasted_iota(I32, (tm, LANES), 1)
    idx = idx_ref[...]
    rank = rank_ref[...].astype(F32)
    ls_row = lsf_ref[0]
    out = []
    for k in range(TOP_K):
        base = jnp.sum(jnp.where(lane == idx[:, k:k + 1], ls_row, 0.0), axis=-1, keepdims=True)
        out.append((base + rank[:, k:k + 1]).astype(I32))
    return out


def _local_positions_lanes(idx_ref, rank_ref, lsf_ref):
    tm = idx_ref.shape[0]
    idx_t = idx_ref[...].astype(F32).T
    rank_t = rank_ref[...].astype(F32).T
    ls_col = jnp.broadcast_to(lsf_ref[0], (LANES, LANES)).T[:, 0:1]
    expert = lax.broadcasted_iota(I32, (LANES, tm), 0).astype(F32)
    out = []
    for k in range(TOP_K):
        base = jnp.sum(jnp.where(expert == idx_t[k:k + 1, :], ls_col, 0.0), axis=0, keepdims=True)
        out.append((base + rank_t[k:k + 1, :]).astype(I32))
    return out


PACK_W = D_MODEL // 2
U32 = jnp.uint32


def _pack_rows(x):
    xb = x.astype(BF16).astype(F32)
    hi = lax.bitcast_convert_type(xb[:, :PACK_W], U32)
    lo = lax.bitcast_convert_type(xb[:, PACK_W:], U32)
    return hi | (lo >> 16)


def _unpack_rows(w):
    hi = lax.bitcast_convert_type(w & jnp.uint32(0xFFFF0000), F32)
    lo = lax.bitcast_convert_type(w << 16, F32)
    return hi.astype(BF16), lo.astype(BF16)


TOTAL_LANE = LANES - 1
WAIT_CHUNK = 128
BIG_PIECE = 4


def _start_group_copies(tables, tile, make_copy):
    gs_ref, ls_ref, n8_ref = tables

    def group(e, carry):
        g0 = gs_ref[tile, e]
        l0 = ls_ref[tile, e]
        n = n8_ref[tile, e]
        nbig = n // BIG_PIECE

        def big(j, c):
            off = j * (BIG_PIECE * GROUP_ALIGN)
            make_copy(pl.multiple_of(l0 + off, GROUP_ALIGN), pl.multiple_of(g0 + off, GROUP_ALIGN),
                      BIG_PIECE * GROUP_ALIGN).start()
            return c

        def one(j, c):
            off = j * GROUP_ALIGN
            make_copy(pl.multiple_of(l0 + off, GROUP_ALIGN), pl.multiple_of(g0 + off, GROUP_ALIGN),
                      GROUP_ALIGN).start()
            return c

        lax.fori_loop(0, nbig, big, 0)
        lax.fori_loop(nbig * BIG_PIECE, n, one, 0)
        return carry

    lax.fori_loop(0, N_EXPERTS, group, 0)


def _wait_pieces(n, make_wait):
    def chunk(j, c):
        make_wait(WAIT_CHUNK).wait()
        return c

    lax.fori_loop(0, n // WAIT_CHUNK, chunk, 0)
    b = WAIT_CHUNK // 2
    while b >= 1:
        def _(b=b):
            make_wait(b).wait()
        pl.when((n & b) != 0)(_)
        b //= 2


def _dispatch_body(gs_ref, ls_ref, n8_ref, tail_ref, h_ref, idx_ref, rank_ref, lsf_ref, xs_ref,
                   xl, zbuf, sem):
    i = pl.program_id(0)
    last = i == pl.num_programs(0) - 1
    slot = i % 2
    tables = (gs_ref, ls_ref, n8_ref)
    tm = h_ref.shape[0]
    lrows = xl.shape[1]
    pos = lax.broadcasted_iota(I32, (lrows, tm), 0)
    lp = _local_positions_lanes(idx_ref, rank_ref, lsf_ref)
    hit = pos == lp[0]
    for k in range(1, TOP_K):
        hit = hit | (pos == lp[k])
    xl[slot] = _pack_rows(_dot(jnp.where(hit, 1.0, 0.0).astype(BF16), h_ref[...]))

    def copy_from(s):
        def make_copy(l, g, rows):
            return pltpu.make_async_copy(xl.at[s, pl.ds(l, rows), :], xs_ref.at[pl.ds(g, rows), :], sem.at[s])
        return make_copy

    def wait_on(s):
        def make_wait(pieces):
            rows = pieces * GROUP_ALIGN
            return pltpu.make_async_copy(xl.at[0, pl.ds(0, rows), :], xs_ref.at[pl.ds(0, rows), :], sem.at[s])
        return make_wait

    @pl.when(i >= 1)
    def _():
        _wait_pieces(n8_ref[i - 1, TOTAL_LANE], wait_on(1 - slot))

    _start_group_copies(tables, i, copy_from(slot))

    def start_tails():
        def per_expert(e, carry):
            g0 = tail_ref[0, e]

            def one(j, c):
                pltpu.make_async_copy(
                    zbuf, xs_ref.at[pl.ds(pl.multiple_of(g0 + j * GROUP_ALIGN, GROUP_ALIGN), GROUP_ALIGN), :],
                    sem.at[2]).start()
                return c

            lax.fori_loop(0, tail_ref[1, e], one, 0)
            return carry

        lax.fori_loop(0, N_EXPERTS, per_expert, 0)

    @pl.when(last)
    def _():
        zbuf[...] = jnp.zeros_like(zbuf)
        start_tails()
        _wait_pieces(n8_ref[i, TOTAL_LANE], wait_on(slot))
        _wait_pieces(tail_ref[1, TOTAL_LANE], wait_on(2))


def _dispatch(gs, ls, n8, tail, h2, idx, rank, lsf, n_slots):
    t = h2.shape[0]
    tm = min(MOE_TILE, t)
    nt = t // tm
    lrows = tm * TOP_K + N_EXPERTS * GROUP_ALIGN
    row = lambda i, *_: (i, 0)
    grid_spec = pltpu.PrefetchScalarGridSpec(
        num_scalar_prefetch=4,
        grid=(nt,),
        in_specs=[
            pl.BlockSpec((tm, D_MODEL), row),
            pl.BlockSpec((tm, LANES), row),
            pl.BlockSpec((tm, LANES), row),
            pl.BlockSpec((1, 1, LANES), lambda i, *_: (i, 0, 0)),
        ],
        out_specs=pl.BlockSpec(memory_space=pl.ANY),
        scratch_shapes=[pltpu.VMEM((2, lrows, PACK_W), U32), pltpu.VMEM((GROUP_ALIGN, PACK_W), U32),
                        pltpu.SemaphoreType.DMA((3,))],
    )
    return pl.pallas_call(
        _dispatch_body,
        grid_spec=grid_spec,
        out_shape=jax.ShapeDtypeStruct((n_slots, PACK_W), U32),
        compiler_params=pltpu.CompilerParams(dimension_semantics=("arbitrary",)),
        name="moe_dispatch",
    )(gs, ls, n8, tail, h2, idx, rank, lsf)


def _expert_body(be_ref, nu_ref, x_ref, wgu_ref, bgu_ref, wd_ref, bd_ref, y_ref, wgu_b, wd_b):
    j = pl.program_id(0)

    @pl.when(j < nu_ref[0])
    def _():
        @pl.when(jnp.logical_or(j == 0, be_ref[j] != be_ref[jnp.maximum(j - 1, 0)]))
        def _():
            wgu_b[...] = wgu_ref[0].astype(BF16)
            wd_b[...] = wd_ref[0].astype(BF16)

        gu = _dot(jnp.concatenate(_unpack_rows(x_ref[...]), axis=1), wgu_b[...]) + bgu_ref[0]
        gate = jnp.minimum(gu[:, :D_FF], SWIGLU_LIMIT)
        up = jnp.clip(gu[:, D_FF:], -SWIGLU_LIMIT, SWIGLU_LIMIT)
        act = (up + 1.0) * gate * _sigmoid(SWIGLU_ALPHA * gate)
        y_ref[...] = _pack_rows(_dot(act.astype(BF16), wd_b[...]) + bd_ref[0])


def _experts(block_e, n_used, xs, wgu, bgu, wd, bd):
    n_slots = xs.shape[0]
    nb = n_slots // EXPERT_BLOCK

    def blk(j, be, nu):
        return jnp.minimum(j, nu[0] - 1)

    grid_spec = pltpu.PrefetchScalarGridSpec(
        num_scalar_prefetch=2,
        grid=(nb,),
        in_specs=[
            pl.BlockSpec((EXPERT_BLOCK, PACK_W), lambda j, be, nu: (blk(j, be, nu), 0)),
            pl.BlockSpec((1, D_MODEL, 2 * D_FF), lambda j, be, nu: (be[blk(j, be, nu)], 0, 0)),
            pl.BlockSpec((1, 1, 2 * D_FF), lambda j, be, nu: (be[blk(j, be, nu)], 0, 0)),
            pl.BlockSpec((1, D_FF, D_MODEL), lambda j, be, nu: (be[blk(j, be, nu)], 0, 0)),
            pl.BlockSpec((1, 1, D_MODEL), lambda j, be, nu: (be[blk(j, be, nu)], 0, 0)),
        ],
        out_specs=pl.BlockSpec((EXPERT_BLOCK, PACK_W), lambda j, be, nu: (blk(j, be, nu), 0)),
        scratch_shapes=[pltpu.VMEM((D_MODEL, 2 * D_FF), BF16), pltpu.VMEM((D_FF, D_MODEL), BF16)],
    )
    return pl.pallas_call(
        _expert_body,
        grid_spec=grid_spec,
        out_shape=jax.ShapeDtypeStruct((n_slots, PACK_W), U32),
        compiler_params=pltpu.CompilerParams(dimension_semantics=("arbitrary",)),
        name="moe_experts",
    )(block_e, n_used, xs, wgu, bgu, wd, bd)


def _combine_body(gs_ref, ls_ref, n8_ref, x_ref, wt_ref, idx_ref, rank_ref, lsf_ref, y_ref, out_ref,
                  yl, sem):
    i = pl.program_id(0)
    slot = i % 2
    tables = (gs_ref, ls_ref, n8_ref)
    tm = x_ref.shape[0]
    lrows = yl.shape[1]

    def copy_into(s):
        def make_copy(l, g, rows):
            return pltpu.make_async_copy(y_ref.at[pl.ds(g, rows), :], yl.at[s, pl.ds(l, rows), :], sem.at[s])
        return make_copy

    def make_wait(pieces):
        rows = pieces * GROUP_ALIGN
        return pltpu.make_async_copy(y_ref.at[pl.ds(0, rows), :], yl.at[0, pl.ds(0, rows), :], sem.at[slot])

    @pl.when(i == 0)
    def _():
        yl[...] = jnp.zeros_like(yl)
        _start_group_copies(tables, i, copy_into(slot))

    @pl.when(i + 1 < pl.num_programs(0))
    def _():
        _start_group_copies(tables, i + 1, copy_into(1 - slot))

    pos = lax.broadcasted_iota(I32, (tm, lrows), 1)
    lp = _local_positions(idx_ref, rank_ref, lsf_ref)
    wt = wt_ref[...]
    wm = jnp.zeros((tm, lrows), F32)
    for k in range(TOP_K):
        wm = jnp.where(pos == lp[k], wt[:, k:k + 1], wm)
    _wait_pieces(n8_ref[i, TOTAL_LANE], make_wait)
    wmb = wm.astype(BF16)
    y_hi, y_lo = _unpack_rows(yl[slot])
    out_ref[:, :PACK_W] = x_ref[:, :PACK_W] + _dot(wmb, y_hi)
    out_ref[:, PACK_W:] = x_ref[:, PACK_W:] + _dot(wmb, y_lo)


def _combine(gs, ls, n8, x1, wts, idx, rank, lsf, y):
    t = x1.shape[0]
    tm = min(MOE_TILE, t)
    nt = t // tm
    lrows = tm * TOP_K + N_EXPERTS * GROUP_ALIGN
    row = lambda i, *_: (i, 0)
    grid_spec = pltpu.PrefetchScalarGridSpec(
        num_scalar_prefetch=3,
        grid=(nt,),
        in_specs=[
            pl.BlockSpec((tm, D_MODEL), row),
            pl.BlockSpec((tm, LANES), row),
            pl.BlockSpec((tm, LANES), row),
            pl.BlockSpec((tm, LANES), row),
            pl.BlockSpec((1, 1, LANES), lambda i, *_: (i, 0, 0)),
            pl.BlockSpec(memory_space=pl.ANY),
        ],
        out_specs=pl.BlockSpec((tm, D_MODEL), row),
        scratch_shapes=[pltpu.VMEM((2, lrows, PACK_W), U32), pltpu.SemaphoreType.DMA((2,))],
    )
    return pl.pallas_call(
        _combine_body,
        grid_spec=grid_spec,
        out_shape=jax.ShapeDtypeStruct((t, D_MODEL), F32),
        compiler_params=pltpu.CompilerParams(dimension_semantics=("arbitrary",)),
        name="moe_combine",
    )(gs, ls, n8, x1, wts, idx, rank, lsf, y)


def _pad_lanes(v, fill=0.0):
    v = v.astype(F32).reshape(1, -1)
    return jnp.pad(v, ((0, 0), (0, LANES - v.shape[1])), constant_values=fill)


def _mixer(x2, mem, rel_table, attn_norm, w_in, b_gate, dn_conv, dn_a_log, dn_dt_bias, dn_out_norm,
           da_q_norm, da_k_norm, da_lambda, da_subln, mem_norm, w_mem_kv, mx_q_norm, mx_k_norm,
           w_branch, w_out, batch, seq):
    wp = jnp.concatenate([w_in[:, :W_AB_LO], w_in[:, W_AB_HI:]], axis=1).astype(BF16)
    wab = jnp.pad(w_in[:, W_AB_LO:W_AB_HI], ((0, 0), (0, LANES - (W_AB_HI - W_AB_LO))))
    p, ab = _inproj(x2, attn_norm.reshape(1, -1), wp, wab)

    o_dn = _deltanet(p, ab, dn_conv, _pad_lanes(dn_a_log), _pad_lanes(dn_dt_bias),
                     dn_out_norm.reshape(1, -1), batch, seq)

    tq = min(ATT_BLOCK, seq)
    bias = _bias_tiles(rel_table.T, tq)
    o_da = _attention(p, bias, jnp.tile(da_q_norm, 2).reshape(1, -1), jnp.tile(da_k_norm, 2).reshape(1, -1),
                      da_lambda, da_subln.reshape(-1, 1), batch, seq)

    mk, mv = _memkv(mem, mem_norm.reshape(1, -1), w_mem_kv.astype(BF16), mx_k_norm.reshape(1, -1))
    return _merge(x2, o_dn, o_da, p, mk, mv, mx_q_norm.reshape(1, -1), b_gate.reshape(3, D_MODEL),
                  w_branch.astype(BF16), w_out.astype(BF16), seq)


def _moe(x1, ffn_norm, w_router, b_router, w_gate_up, b_gate_up, w_down, b_down):
    t = x1.shape[0]
    nt = t // min(MOE_TILE, t)
    max_rows = t * TOP_K + nt * N_EXPERTS * (GROUP_ALIGN - 1)
    n_blocks = -(-max_rows // EXPERT_BLOCK) + N_EXPERTS
    n_blocks_pad = -(-n_blocks // 8) * 8
    n_slots = n_blocks * EXPERT_BLOCK

    wr = jnp.pad(w_router, ((0, 0), (0, LANES - N_EXPERTS)))
    h2, idx, wts, rank, cnt = _router(x1, ffn_norm.reshape(1, -1), wr, _pad_lanes(b_router, NEG))
    gs, ls, n8, lsf, tail, meta = _plan(cnt[:, 0, :], n_blocks_pad)
    block_e = meta[:n_blocks, 0]
    n_used = meta[0:1, 1]
    lsf = lsf.reshape(nt, 1, LANES)

    xs = _dispatch(gs, ls, n8, tail, h2, idx, rank, lsf, n_slots)
    y = _experts(block_e, n_used, xs, w_gate_up, b_gate_up.reshape(N_EXPERTS, 1, -1),
                 w_down, b_down.reshape(N_EXPERTS, 1, -1))
    return _combine(gs, ls, n8, x1, wts, idx, rank, lsf, y)


def kernel(x, mem, rel_table, attn_norm, w_in, b_gate, dn_conv, dn_a_log, dn_dt_bias, dn_out_norm,
           da_q_norm, da_k_norm, da_lambda, da_subln, mem_norm, w_mem_kv, mx_q_norm, mx_k_norm,
           w_branch, w_out, ffn_norm, w_router, b_router, w_gate_up, b_gate_up, w_down, b_down):
    batch, seq, d = x.shape
    x2 = x.reshape(batch * seq, d)
    x1 = _mixer(x2, mem, rel_table, attn_norm[0], w_in[0], b_gate[0], dn_conv[0], dn_a_log[0],
                dn_dt_bias[0], dn_out_norm[0], da_q_norm[0], da_k_norm[0], da_lambda[0], da_subln[0],
                mem_norm[0], w_mem_kv[0], mx_q_norm[0], mx_k_norm[0], w_branch[0], w_out[0], batch, seq)
    out = _moe(x1, ffn_norm[0], w_router[0], b_router[0], w_gate_up[0], b_gate_up[0], w_down[0],
               b_down[0])
    return out.reshape(batch, seq, d)
```

```python
import functools
import math

import jax
import jax.numpy as jnp
from jax import lax
from jax.experimental import pallas as pl
from jax.experimental.pallas import tpu as pltpu

F32 = jnp.float32
BF16 = jnp.bfloat16
I32 = jnp.int32

D_MODEL = 1024
EPS = 1e-6
LANES = 128

DN_HEADS = 4
DN_DK = 128
DN_CHUNK = 64
DN_CONV = 4

DA_HEADS = 4
DA_DH = 64

MX_HEADS = 4
MX_DH = 128

REL_BUCKETS = 32
REL_MAX_DIST = 128

N_EXPERTS = 32
TOP_K = 4
D_FF = 1024
SWIGLU_LIMIT = 7.0
SWIGLU_ALPHA = 1.702
EXPERT_BLOCK = 512
MOE_TILE = 256
GROUP_ALIGN = 8

LAM_INIT = 0.8 - 0.6 * math.exp(-0.3 * 0)
LOG2E = 1.4426950408889634
NEG = -1e30

P_DNQ, P_DNK, P_DNV, P_DNZ = 0, 512, 1024, 1536
P_DAQ, P_DAK, P_DAV = 2048, 2560, 3072
P_MXQ = 3584
P_GATE = 4096
P_COLS = 7168
W_AB_LO, W_AB_HI = 2048, 2056


def _dot(a, b):
    return jnp.dot(a, b, preferred_element_type=F32)


def _dot_nt(a, b):
    return lax.dot_general(a, b, (((1,), (1,)), ((), ())), preferred_element_type=F32)


def _dot_tn(a, b):
    return lax.dot_general(a, b, (((0,), (0,)), ((), ())), preferred_element_type=F32)


def _split(x):
    hi = x.astype(BF16)
    lo = (x - hi.astype(F32)).astype(BF16)
    return hi, lo


def _dot3(a, b):
    ah, al = _split(a)
    bh, bl = _split(b)
    return _dot(ah, bh) + _dot(ah, bl) + _dot(al, bh)


def _sigmoid(x):
    return 1.0 / (1.0 + jnp.exp(-x))


def _rms(x, n):
    return lax.rsqrt(jnp.sum(x * x, axis=-1, keepdims=True) * (1.0 / n) + EPS)


def _inproj_body(x_ref, g_ref, w_ref, wab_ref, p_ref, ab_ref, h_scr):
    @pl.when(pl.program_id(1) == 0)
    def _():
        x = x_ref[...]
        h = x * _rms(x, D_MODEL) * g_ref[...]
        h_scr[...] = h.astype(BF16)
        ab_ref[...] = _dot3(h, wab_ref[...])

    p_ref[...] = _dot(h_scr[...], w_ref[...]).astype(p_ref.dtype)


def _inproj(x2, gain, wp, wab):
    t = x2.shape[0]
    tm = min(2048, t)
    tn = 1024
    return pl.pallas_call(
        _inproj_body,
        grid=(t // tm, P_COLS // tn),
        in_specs=[
            pl.BlockSpec((tm, D_MODEL), lambda i, j: (i, 0)),
            pl.BlockSpec((1, D_MODEL), lambda i, j: (0, 0)),
            pl.BlockSpec((D_MODEL, tn), lambda i, j: (0, j)),
            pl.BlockSpec((D_MODEL, LANES), lambda i, j: (0, 0)),
        ],
        out_specs=[
            pl.BlockSpec((tm, tn), lambda i, j: (i, j)),
            pl.BlockSpec((tm, LANES), lambda i, j: (i, 0)),
        ],
        out_shape=[
            jax.ShapeDtypeStruct((t, P_COLS), BF16),
            jax.ShapeDtypeStruct((t, LANES), F32),
        ],
        scratch_shapes=[pltpu.VMEM((tm, D_MODEL), BF16)],
        compiler_params=pltpu.CompilerParams(dimension_semantics=("parallel", "arbitrary")),
        name="inproj",
    )(x2, gain, wp, wab)


DN_HALO = 16
DN_SCAN_CHUNK = 256


def _deltanet_body(q_ref, k_ref, v_ref, z_ref, qh_ref, kh_ref, vh_ref, ab_ref, cw_ref, alog_ref,
                   dtb_ref, on_ref, o_ref, stage, qs, ks, vs, s_scr):
    i = pl.program_id(1)
    tc = q_ref.shape[0]
    hw = DN_HEADS * DN_DK

    @pl.when(i == 0)
    def _():
        s_scr[...] = jnp.zeros_like(s_scr)

    for src, halo, dst, off, kind in ((q_ref, qh_ref, qs, 0, "q"), (k_ref, kh_ref, ks, hw, "k"),
                                      (v_ref, vh_ref, vs, 2 * hw, "v")):
        hal = halo[...].astype(F32)
        stage[0:DN_HALO, :] = jnp.where(i == 0, 0.0, hal)
        stage[DN_HALO:DN_HALO + tc, :] = src[...].astype(F32)
        base = DN_HALO - (DN_CONV - 1)
        y = stage[base:base + tc, :] * cw_ref[0:1, off:off + hw]
        for j in range(1, DN_CONV):
            y = y + stage[base + j:base + j + tc, :] * cw_ref[j:j + 1, off:off + hw]
        y = y * _sigmoid(y)
        if kind == "v":
            dst[...] = y
        else:
            for h in range(DN_HEADS):
                sl = slice(h * DN_DK, (h + 1) * DN_DK)
                yh = y[:, sl]
                r = lax.rsqrt(jnp.sum(yh * yh, axis=-1, keepdims=True) + EPS)
                if kind == "q":
                    r = r * (DN_DK ** -0.5)
                dst[:, sl] = yh * r

    c = min(DN_SCAN_CHUNK, tc)
    row = lax.broadcasted_iota(I32, (c, c), 0)
    col = lax.broadcasted_iota(I32, (c, c), 1)
    incl = row >= col
    strict = row > col
    same_blk = (row // DN_CHUNK) == (col // DN_CHUNK)
    tri = jnp.where(incl, 1.0, 0.0).astype(BF16)
    eye = jnp.where(row == col, 1.0, 0.0)
    neg_a = -jnp.exp(alog_ref[...])
    dtb = dtb_ref[...]

    def chunk(ci, carry):
        r0 = pl.multiple_of(ci * c, c)
        abc = ab_ref[pl.ds(r0, c), :]
        a_in = abc + dtb
        g_all = neg_a * (jnp.maximum(a_in, 0.0) + jnp.log(1.0 + jnp.exp(-jnp.abs(a_in))))
        beta_all = _sigmoid(abc)
        zc = z_ref[pl.ds(r0, c), :].astype(F32)
        hs = range(DN_HEADS)
        sls = [slice(h * DN_DK, (h + 1) * DN_DK) for h in hs]
        q = [qs[pl.ds(r0, c), sl] for sl in sls]
        k = [ks[pl.ds(r0, c), sl] for sl in sls]
        v = [vs[pl.ds(r0, c), sl] for sl in sls]
        beta = [beta_all[:, DN_HEADS + h:DN_HEADS + h + 1] for h in hs]
        g_hi, g_lo = _split(g_all)
        gc_all = _dot(tri, g_hi) + _dot(tri, g_lo)
        gc_rows = gc_all.T
        kb = [k[h].astype(BF16) for h in hs]
        qkk = [_dot_nt(jnp.concatenate([q[h].astype(BF16), kb[h]], axis=0), kb[h]) for h in hs]
        gc = [gc_all[:, h:h + 1] for h in hs]
        decay = [jnp.where(incl, jnp.exp(gc[h] - gc_rows[h:h + 1, :]), 0.0) for h in hs]
        lower = [jnp.where(strict, qkk[h][c:] * decay[h] * beta[h], 0.0) for h in hs]
        pw = [jnp.where(same_blk, -lower[h], 0.0) for h in hs]
        dinv = [eye + pw[h] for h in hs]
        pwb = [pw[h].astype(BF16) for h in hs]
        for _ in range(int(math.log2(DN_CHUNK)) - 1):
            pwb = [_dot(pwb[h], pwb[h]).astype(BF16) for h in hs]
            dinv = [dinv[h] + _dot(dinv[h].astype(BF16), pwb[h]) for h in hs]
        dinv_b = [dinv[h].astype(BF16) for h in hs]
        pw = [-_dot(dinv_b[h], jnp.where(same_blk, 0.0, lower[h]).astype(BF16)) for h in hs]
        xm = [eye + pw[h] for h in hs]
        pwb = [pw[h].astype(BF16) for h in hs]
        for _ in range(int(math.log2(c // DN_CHUNK)) - 1):
            pwb = [_dot(pwb[h], pwb[h]).astype(BF16) for h in hs]
            xm = [xm[h] + _dot(xm[h].astype(BF16), pwb[h]) for h in hs]
        inv = [_dot(xm[h].astype(BF16), dinv_b[h]).astype(BF16) for h in hs]
        egc = [jnp.exp(gc[h]) for h in hs]
        rhs = [jnp.concatenate([v[h] * beta[h], k[h] * (beta[h] * egc[h])], axis=1).astype(BF16) for h in hs]
        sol = [_dot(inv[h], rhs[h]) for h in hs]
        qkm = [jnp.where(incl, qkk[h][:c] * decay[h], 0.0).astype(BF16) for h in hs]
        gl = [gc[h][c - 1:c, :] for h in hs]
        state = [s_scr[h] for h in hs]
        ws = [_dot(jnp.concatenate([sol[h][:, DN_DK:].astype(BF16), (q[h] * egc[h]).astype(BF16)], axis=0),
                   state[h].astype(BF16)) for h in hs]
        v_new = [sol[h][:, :DN_DK] - ws[h][:c] for h in hs]
        o = [ws[h][c:] + _dot(qkm[h], v_new[h].astype(BF16)) for h in hs]
        for h in hs:
            s_scr[h] = state[h] * jnp.exp(gl[h]) + _dot_tn(kb[h], (v_new[h] * jnp.exp(gl[h] - gc[h])).astype(BF16))
        for h in hs:
            zz = zc[:, sls[h]]
            on = o[h] * _rms(o[h], DN_DK) * on_ref[...]
            o_ref[pl.ds(r0, c), sls[h]] = (on * (zz * _sigmoid(zz))).astype(o_ref.dtype)
        return carry

    lax.fori_loop(0, tc // c, chunk, 0, unroll=True)


def _deltanet(p, ab, conv_w, alog_row, dtb_row, out_norm, batch, seq):
    t = batch * seq
    tc = min(1024, seq)
    nt = seq // tc
    hw = DN_HEADS * DN_DK

    def main(cb):
        return pl.BlockSpec((tc, hw), lambda b, i: (b * nt + i, cb))

    def halo(cb):
        return pl.BlockSpec(
            (DN_HALO, hw),
            lambda b, i: (jnp.maximum((b * seq + i * tc) // DN_HALO - 1, 0), cb))

    return pl.pallas_call(
        _deltanet_body,
        grid=(batch, nt),
        in_specs=[
            main(P_DNQ // hw), main(P_DNK // hw), main(P_DNV // hw), main(P_DNZ // hw),
            halo(P_DNQ // hw), halo(P_DNK // hw), halo(P_DNV // hw),
            pl.BlockSpec((tc, LANES), lambda b, i: (b * nt + i, 0)),
            pl.BlockSpec((DN_CONV, 3 * hw), lambda b, i: (0, 0)),
            pl.BlockSpec((1, LANES), lambda b, i: (0, 0)),
            pl.BlockSpec((1, LANES), lambda b, i: (0, 0)),
            pl.BlockSpec((1, DN_DK), lambda b, i: (0, 0)),
        ],
        out_specs=pl.BlockSpec((tc, hw), lambda b, i: (b * nt + i, 0)),
        out_shape=jax.ShapeDtypeStruct((t, hw), BF16),
        scratch_shapes=[
            pltpu.VMEM((DN_HALO + tc, hw), F32),
            pltpu.VMEM((tc, hw), F32),
            pltpu.VMEM((tc, hw), F32),
            pltpu.VMEM((tc, hw), F32),
            pltpu.VMEM((DN_HEADS, DN_DK, DN_DK), F32),
        ],
        compiler_params=pltpu.CompilerParams(dimension_semantics=("parallel", "arbitrary")),
        name="deltanet",
    )(p, p, p, p, p, p, p, ab, conv_w, alog_row, dtb_row, out_norm)


ATT_BLOCK = 512


def _bias_body(tbl_ref, o_ref):
    h = pl.program_id(0)
    tq = o_ref.shape[2]
    key = lax.broadcasted_iota(I32, (tq, tq), 0)
    qry = lax.broadcasted_iota(I32, (tq, tq), 1)
    max_exact = REL_BUCKETS // 2
    far = tbl_ref[h, REL_BUCKETS - 1]
    for d in range(2):
        n = qry - key + d * tq
        nn = jnp.maximum(n, 0)
        nf = jnp.maximum(nn, 1).astype(F32)
        large = max_exact + (jnp.log(nf / max_exact) / math.log(REL_MAX_DIST / max_exact)
                             * (REL_BUCKETS - max_exact)).astype(I32)
        large = jnp.minimum(large, REL_BUCKETS - 1)
        bucket = jnp.where(nn < max_exact, nn, large)
        val = jnp.zeros((tq, tq), F32)
        for b in range(REL_BUCKETS):
            val = jnp.where(bucket == b, tbl_ref[h, b], val)
        o_ref[0, d] = jnp.where(n >= 0, (val - far) * LOG2E, NEG)


def _bias_tiles(tbl_t, tq):
    return pl.pallas_call(
        _bias_body,
        grid=(DA_HEADS,),
        in_specs=[pl.BlockSpec(memory_space=pltpu.SMEM)],
        out_specs=pl.BlockSpec((1, 2, tq, tq), lambda h: (h, 0, 0, 0)),
        out_shape=jax.ShapeDtypeStruct((DA_HEADS, 2, tq, tq), F32),
        name="t5_bias_tiles",
    )(tbl_t)


DA_DV = 2 * DA_DH
DA_VROWS = DA_DV + 16


BOUND_SLACK = 1.02
MAX_SHIFT_GAP = 110.0


def _attn_body(q_ref, qn_ref, k_ref, v_ref, bias_ref, qg_ref, kg_ref, lam_ref, sg_ref, o_ref,
               kn, vt, kst, qc_s, bd_s, flag_s, m_s, acc_s):
    qi = pl.program_id(2)
    slot = qi % 2
    tq = q_ref.shape[0]
    seq = k_ref.shape[0]
    tk = tq
    lo_mask = lax.broadcasted_iota(I32, (1, DA_DV), 1) < DA_DH

    def group_norm(x, gain):
        x2 = x * x
        lo = jnp.sum(jnp.where(lo_mask, x2, 0.0), axis=-1, keepdims=True)
        hi = jnp.sum(jnp.where(lo_mask, 0.0, x2), axis=-1, keepdims=True)
        r = jnp.where(lo_mask, lax.rsqrt(lo * (1.0 / DA_DH) + EPS), lax.rsqrt(hi * (1.0 / DA_DH) + EPS))
        return x * r * gain

    @pl.when(qi == 0)
    def _():
        ones = jnp.ones((DA_VROWS - DA_DV, tk), BF16)

        def body(c, kmax2):
            r0 = pl.multiple_of(c * tk, tk)
            kb = group_norm(k_ref[pl.ds(r0, tk), :].astype(F32), kg_ref[...]).astype(BF16)
            kn[pl.ds(r0, tk), :] = kb
            vt[c, 0:DA_DV, :] = v_ref[pl.ds(r0, tk), :].astype(F32).T.astype(BF16)
            vt[c, DA_DV:DA_VROWS, :] = ones
            k2 = kb.astype(F32)
            k2 = k2 * k2
            lo = jnp.max(jnp.sum(jnp.where(lo_mask, k2, 0.0), axis=-1, keepdims=True), axis=0, keepdims=True)
            hi = jnp.max(jnp.sum(jnp.where(lo_mask, 0.0, k2), axis=-1, keepdims=True), axis=0, keepdims=True)
            return jnp.maximum(kmax2, jnp.where(lo_mask, lo, hi))
        kst[0:1, :] = lax.fori_loop(0, seq // tk, body, jnp.zeros((1, DA_DV), F32), unroll=2)
        b0 = bias_ref[0, 0]
        b1 = bias_ref[0, 1]
        bmax = jnp.maximum(jnp.max(jnp.maximum(b0, b1), axis=0, keepdims=True), 0.0)
        bmin = jnp.minimum(jnp.min(jnp.minimum(jnp.where(b0 > 0.5 * NEG, b0, 0.0), b1), axis=0, keepdims=True), 0.0)
        kst[1:2, :] = jnp.broadcast_to(jnp.max(bmax, axis=1, keepdims=True), (1, DA_DV))
        kst[2:3, :] = jnp.broadcast_to(jnp.min(bmin, axis=1, keepdims=True), (1, DA_DV))

    def prepare(src_ref, s):
        q = group_norm(src_ref[...].astype(F32), qg_ref[...]) * (DA_DH ** -0.5 * LOG2E)
        qc_s[s] = jnp.concatenate([jnp.where(lo_mask, q, 0.0), jnp.where(lo_mask, 0.0, q)], axis=0).astype(BF16)
        q2 = q * q * kst[0:1, :]
        ones8 = jnp.ones((8, DA_DV), BF16)
        bound = []
        for m in range(2):
            q2m = jnp.where(lo_mask, q2, 0.0) if m == 0 else jnp.where(lo_mask, 0.0, q2)
            bound.append(jnp.sqrt(_dot_nt(ones8, q2m.astype(BF16))[0:1, :]) * BOUND_SLACK)
        bound = jnp.concatenate(bound, axis=1)
        bd_s[s] = bound
        worst = jnp.max(2.0 * bound, axis=1, keepdims=True) + kst[1:2, 0:1] - kst[2:3, 0:1]
        flag_s[s] = jnp.where(worst[0, 0] <= MAX_SHIFT_GAP, 1, 0).astype(I32)

    @pl.when(qi == 0)
    def _():
        prepare(q_ref, slot)

    qcat = qc_s[slot]
    bound = bd_s[slot]
    bmax = kst[1:2, 0:1]
    safe = flag_s[slot] == 1
    acc_s[...] = jnp.zeros_like(acc_s)

    def block(j, d):
        r0 = pl.multiple_of(j * tk, tk)
        st = _dot_nt(kn[pl.ds(r0, tk), :], qcat)
        if d is not None:
            bias = bias_ref[0, d]
            st = st + jnp.concatenate([bias, bias], axis=1)
        m_prev = m_s[...]
        m_new = jnp.maximum(m_prev, jnp.max(st, axis=0, keepdims=True))
        alpha = jnp.exp2(m_prev - m_new)
        acc_s[...] = alpha * acc_s[...] + _dot(vt[j], jnp.exp2(st - m_new).astype(BF16))
        m_s[...] = m_new

    def blocks_fixed(js, ds=None):
        sts = [_dot_nt(kn[pl.ds(pl.multiple_of(j * tk, tk), tk), :], qcat) for j in js]
        if ds is not None:
            biases = [bias_ref[0, d] for d in ds]
            sts = [st + jnp.concatenate([b, b], axis=1) for st, b in zip(sts, biases)]
        shift = m_s[...]
        pts = [jnp.exp2(st - shift).astype(BF16) for st in sts]
        tot = _dot(vt[js[0]], pts[0])
        for j, pt in zip(js[1:], pts[1:]):
            tot = tot + _dot(vt[j], pt)
        acc_s[...] = acc_s[...] + tot

    def run(fixed_shift):
        n_far = jnp.maximum(qi - 1, 0)

        if fixed_shift:
            def far_quad(jj, carry):
                blocks_fixed([4 * jj + u for u in range(4)])
                return carry

            lax.fori_loop(0, n_far // 4, far_quad, 0)
            rem = n_far % 4

            @pl.when(rem >= 2)
            def _():
                blocks_fixed([n_far - rem, n_far - rem + 1])

            @pl.when(rem % 2 == 1)
            def _():
                blocks_fixed([n_far - 1])

            @pl.when(qi >= 1)
            def _():
                blocks_fixed([qi - 1, qi], [1, 0])
                prepare(qn_ref, 1 - slot)

            @pl.when(qi == 0)
            def _():
                blocks_fixed([qi], [0])
                prepare(qn_ref, 1 - slot)
        else:
            def far_one(j, carry):
                block(j, None)
                return carry

            lax.fori_loop(0, n_far, far_one, 0)

            @pl.when(qi >= 1)
            def _():
                block(qi - 1, 1)

            block(qi, 0)
            prepare(qn_ref, 1 - slot)

    @pl.when(safe)
    def _():
        m_s[...] = bound + bmax
        run(True)

    @pl.when(jnp.logical_not(safe))
    def _():
        m_s[...] = jnp.full(m_s.shape, NEG, F32)
        run(False)

    lam_p = lam_ref[...]
    lam = (jnp.exp(jnp.sum(lam_p[0:1, :] * lam_p[1:2, :], axis=-1, keepdims=True))
           - jnp.exp(jnp.sum(lam_p[2:3, :] * lam_p[3:4, :], axis=-1, keepdims=True)) + LAM_INIT)
    a0 = acc_s[:, 0:tq]
    a1 = acc_s[:, tq:2 * tq]
    ot = a0[0:DA_DV] / a0[DA_DV:DA_DV + 1] - lam * (a1[0:DA_DV] / a1[DA_DV:DA_DV + 1])
    r = lax.rsqrt(jnp.sum(ot * ot, axis=0, keepdims=True) * (1.0 / DA_DV) + EPS)
    ot = ot * r * (sg_ref[...] * (1.0 - LAM_INIT))
    o_ref[...] = ot.T.astype(o_ref.dtype)


def _attention(p, bias, qg, kg, lam_p, subln, batch, seq):
    t = batch * seq
    tq = min(ATT_BLOCK, seq)
    nq = seq // tq
    dv = DA_DV
    return pl.pallas_call(
        _attn_body,
        grid=(batch, DA_HEADS, nq),
        in_specs=[
            pl.BlockSpec((tq, dv), lambda b, h, i: (b * nq + i, P_DAQ // dv + h)),
            pl.BlockSpec((tq, dv), lambda b, h, i: (b * nq + jnp.minimum(i + 1, nq - 1), P_DAQ // dv + h)),
            pl.BlockSpec((seq, dv), lambda b, h, i: (b, P_DAK // dv + h)),
            pl.BlockSpec((seq, dv), lambda b, h, i: (b, P_DAV // dv + h)),
            pl.BlockSpec((1, 2, tq, tq), lambda b, h, i: (h, 0, 0, 0)),
            pl.BlockSpec((1, dv), lambda b, h, i: (0, 0)),
            pl.BlockSpec((1, dv), lambda b, h, i: (0, 0)),
            pl.BlockSpec((4, DA_DH), lambda b, h, i: (0, 0)),
            pl.BlockSpec((dv, 1), lambda b, h, i: (0, 0)),
        ],
        out_specs=pl.BlockSpec((tq, dv), lambda b, h, i: (b * nq + i, h)),
        out_shape=jax.ShapeDtypeStruct((t, DA_HEADS * dv), BF16),
        scratch_shapes=[
            pltpu.VMEM((seq, dv), BF16),
            pltpu.VMEM((seq // tq, DA_VROWS, tq), BF16),
            pltpu.VMEM((8, dv), F32),
            pltpu.VMEM((2, 2 * tq, dv), BF16),
            pltpu.VMEM((2, 1, 2 * tq), F32),
            pltpu.SMEM((2,), I32),
            pltpu.VMEM((1, 2 * tq), F32),
            pltpu.VMEM((DA_VROWS, 2 * tq), F32),
        ],
        compiler_params=pltpu.CompilerParams(dimension_semantics=("parallel", "parallel", "arbitrary")),
        name="diff_attention",
    )(p, p, p, p, bias, qg, kg, lam_p, subln)


def _memkv_body(mem_ref, mg_ref, w_ref, kg_ref, mk_ref, mv_ref):
    x = mem_ref[0]
    xn = x * _rms(x, D_MODEL) * mg_ref[...]
    kv = _dot(xn.astype(BF16), w_ref[...])
    hw = MX_HEADS * MX_DH
    for h in range(MX_HEADS):
        sl = slice(h * MX_DH, (h + 1) * MX_DH)
        kh = kv[:, sl]
        mk_ref[0, :, sl] = (kh * _rms(kh, MX_DH) * kg_ref[...]).astype(BF16)
    mv_ref[0] = kv[:, hw:].astype(BF16)


def _memkv(mem, mem_norm, w_kv, k_norm):
    b, n, _ = mem.shape
    hw = MX_HEADS * MX_DH
    return pl.pallas_call(
        _memkv_body,
        grid=(b,),
        in_specs=[
            pl.BlockSpec((1, n, D_MODEL), lambda i: (i, 0, 0)),
            pl.BlockSpec((1, D_MODEL), lambda i: (0, 0)),
            pl.BlockSpec((D_MODEL, 2 * hw), lambda i: (0, 0)),
            pl.BlockSpec((1, MX_DH), lambda i: (0, 0)),
        ],
        out_specs=[pl.BlockSpec((1, n, hw), lambda i: (i, 0, 0))] * 2,
        out_shape=[jax.ShapeDtypeStruct((b, n, hw), BF16)] * 2,
        name="memory_kv",
    )(mem, mem_norm, w_kv, k_norm)


def _merge_body(x_ref, odn_ref, oda_ref, mxq_ref, g0_ref, g1_ref, g2_ref, mk_ref, mv_ref, qg_ref,
                bg_ref, wb_ref, wo_ref, out_ref, omx):
    for h in range(MX_HEADS):
        sl = slice(h * MX_DH, (h + 1) * MX_DH)
        qh = mxq_ref[:, sl].astype(F32)
        qh = qh * _rms(qh, MX_DH) * qg_ref[...] * (MX_DH ** -0.5 * LOG2E)
        s = _dot_nt(qh.astype(BF16), mk_ref[0, :, sl])
        p = jnp.exp2(s - jnp.max(s, axis=-1, keepdims=True))
        oh = _dot(p.astype(BF16), mv_ref[0, :, sl]) / jnp.sum(p, axis=-1, keepdims=True)
        omx[:, sl] = oh.astype(BF16)
    y = None
    for r, (o_r, g_r) in enumerate(((odn_ref, g0_ref), (oda_ref, g1_ref), (omx, g2_ref))):
        gate = _sigmoid(g_r[...].astype(F32) + bg_ref[r:r + 1, :])
        term = gate * _dot(o_r[...], wb_ref[r])
        y = term if y is None else y + term
    out_ref[...] = x_ref[...] + _dot(y.astype(BF16), wo_ref[...])


def _merge(x2, o_dn, o_da, p, mk, mv, q_norm, b_gate, w_branch, w_out, seq):
    t = x2.shape[0]
    tm = min(1024, seq)
    nt = seq // tm
    bw = 512
    n_mem = mk.shape[1]
    return pl.pallas_call(
        _merge_body,
        grid=(t // tm,),
        in_specs=[
            pl.BlockSpec((tm, D_MODEL), lambda i: (i, 0)),
            pl.BlockSpec((tm, bw), lambda i: (i, 0)),
            pl.BlockSpec((tm, bw), lambda i: (i, 0)),
            pl.BlockSpec((tm, bw), lambda i: (i, P_MXQ // bw)),
            pl.BlockSpec((tm, D_MODEL), lambda i: (i, P_GATE // D_MODEL)),
            pl.BlockSpec((tm, D_MODEL), lambda i: (i, P_GATE // D_MODEL + 1)),
            pl.BlockSpec((tm, D_MODEL), lambda i: (i, P_GATE // D_MODEL + 2)),
            pl.BlockSpec((1, n_mem, bw), lambda i: (i // nt, 0, 0)),
            pl.BlockSpec((1, n_mem, bw), lambda i: (i // nt, 0, 0)),
            pl.BlockSpec((1, MX_DH), lambda i: (0, 0)),
            pl.BlockSpec((3, D_MODEL), lambda i: (0, 0)),
            pl.BlockSpec((3, bw, D_MODEL), lambda i: (0, 0, 0)),
            pl.BlockSpec((D_MODEL, D_MODEL), lambda i: (0, 0)),
        ],
        out_specs=pl.BlockSpec((tm, D_MODEL), lambda i: (i, 0)),
        out_shape=jax.ShapeDtypeStruct((t, D_MODEL), F32),
        scratch_shapes=[pltpu.VMEM((tm, bw), BF16)],
        compiler_params=pltpu.CompilerParams(dimension_semantics=("parallel",)),
        name="merge",
    )(x2, o_dn, o_da, p, p, p, p, mk, mv, q_norm, b_gate, w_branch, w_out)


def _router_body(x_ref, g_ref, wr_ref, br_ref, h_ref, idx_ref, wt_ref, rank_ref, cnt_ref):
    tm = x_ref.shape[0]
    x = x_ref[...]
    h = x * _rms(x, D_MODEL) * g_ref[...]
    h_ref[...] = h.astype(BF16)
    logits = _dot3(h, wr_ref[...]) + br_ref[...]
    lane = lax.broadcasted_iota(I32, (tm, LANES), 1)
    lane_f = lane.astype(F32)
    work = logits
    sel = jnp.zeros((tm, LANES), F32)
    vals, idxs = [], []
    for _ in range(TOP_K):
        mx = jnp.max(work, axis=-1, keepdims=True)
        ik = jnp.min(jnp.where(work == mx, lane_f, float(LANES)), axis=-1, keepdims=True)
        hit = lane_f == ik
        sel = jnp.where(hit, 1.0, sel)
        work = jnp.where(hit, -jnp.inf, work)
        vals.append(mx)
        idxs.append(ik)
    es = [jnp.exp(v - vals[0]) for v in vals]
    den = es[0] + es[1] + es[2] + es[3]
    sub = cnt_ref.shape[0]
    mt = tm // sub
    r = lax.broadcasted_iota(I32, (tm, tm), 0)
    c = lax.broadcasted_iota(I32, (tm, tm), 1)
    tril = jnp.where((r > c) & (r // mt == c // mt), 1.0, 0.0).astype(BF16)
    cum = _dot(tril, sel.astype(BF16))
    idx_o = jnp.zeros((tm, LANES), F32)
    wt_o = jnp.zeros((tm, LANES), F32)
    rank_o = jnp.zeros((tm, LANES), F32)
    for k in range(TOP_K):
        rk = jnp.sum(jnp.where(lane_f == idxs[k], cum, 0.0), axis=-1, keepdims=True)
        idx_o = jnp.where(lane == k, idxs[k], idx_o)
        wt_o = jnp.where(lane == k, es[k] / den, wt_o)
        rank_o = jnp.where(lane == k, rk, rank_o)
    idx_ref[...] = idx_o.astype(I32)
    wt_ref[...] = wt_o
    rank_ref[...] = rank_o.astype(I32)
    for s in range(sub):
        cnt_ref[s] = jnp.broadcast_to(jnp.sum(sel[s * mt:(s + 1) * mt], axis=0, keepdims=True), (8, LANES))


def _router(x1, gain, w_r, b_r):
    t = x1.shape[0]
    mt = min(MOE_TILE, t)
    sub = 2 if t % (2 * mt) == 0 else 1
    tm = sub * mt
    row = lambda i: (i, 0)
    fixed = lambda i: (0, 0)
    return pl.pallas_call(
        _router_body,
        grid=(t // tm,),
        in_specs=[
            pl.BlockSpec((tm, D_MODEL), row),
            pl.BlockSpec((1, D_MODEL), fixed),
            pl.BlockSpec((D_MODEL, LANES), fixed),
            pl.BlockSpec((1, LANES), fixed),
        ],
        out_specs=[
            pl.BlockSpec((tm, D_MODEL), row),
            pl.BlockSpec((tm, LANES), row),
            pl.BlockSpec((tm, LANES), row),
            pl.BlockSpec((tm, LANES), row),
            pl.BlockSpec((sub, 8, LANES), lambda i: (i, 0, 0)),
        ],
        out_shape=[
            jax.ShapeDtypeStruct((t, D_MODEL), BF16),
            jax.ShapeDtypeStruct((t, LANES), I32),
            jax.ShapeDtypeStruct((t, LANES), F32),
            jax.ShapeDtypeStruct((t, LANES), I32),
            jax.ShapeDtypeStruct((t // mt, 8, LANES), F32),
        ],
        compiler_params=pltpu.CompilerParams(dimension_semantics=("arbitrary",)),
        name="router",
    )(x1, gain, w_r, b_r)


def _lane_cumsum(x):
    lane = lax.broadcasted_iota(I32, x.shape, 1)
    s = 1
    while s < N_EXPERTS:
        x = x + jnp.where(lane >= s, pltpu.roll(x, s, axis=1), 0.0)
        s *= 2
    return x


def _plan_body(cnt_ref, gs_ref, ls_ref, n8_ref, lsf_ref, tail_ref, meta_ref):
    nt = cnt_ref.shape[0]
    ga = float(GROUP_ALIGN)
    eb = float(EXPERT_BLOCK)
    lane = lax.broadcasted_iota(I32, (nt, LANES), 1)
    r8 = jnp.where(lane < N_EXPERTS, jnp.floor((cnt_ref[...] + (ga - 1.0)) * (1.0 / ga)) * ga, 0.0)
    ri = lax.broadcasted_iota(I32, (nt, nt), 0)
    ci = lax.broadcasted_iota(I32, (nt, nt), 1)
    before = _dot(jnp.where(ri > ci, 1.0, 0.0).astype(BF16), r8.astype(BF16))
    tot = jnp.sum(r8, axis=0, keepdims=True)
    region = jnp.floor((tot + (eb - 1.0)) * (1.0 / eb)) * eb
    pends = _lane_cumsum(jnp.broadcast_to(region, (8, LANES)))[0:1, :]
    pstart = pends - region
    lstart = _lane_cumsum(r8) - r8
    gs_ref[...] = (pstart + before).astype(I32)
    ls_ref[...] = lstart.astype(I32)
    pieces = r8 * (1.0 / ga)
    n8_ref[...] = jnp.where(lane == TOTAL_LANE, jnp.sum(pieces, axis=1, keepdims=True), pieces).astype(I32)
    lsf_ref[...] = lstart
    row8 = lax.broadcasted_iota(I32, (8, LANES), 0)
    tail_n = (region - tot) * (1.0 / ga)
    lane8 = lax.broadcasted_iota(I32, (8, LANES), 1)
    tail_n = jnp.where(lane8 == TOTAL_LANE, jnp.sum(tail_n, axis=1, keepdims=True), tail_n)
    tail = jnp.where(row8 == 0, pstart + tot, jnp.where(row8 == 1, tail_n, 0.0))
    tail_ref[...] = tail.astype(I32)
    nb = meta_ref.shape[0]
    ln = lax.broadcasted_iota(I32, (nb, LANES), 1)
    blk = lax.broadcasted_iota(I32, (nb, LANES), 0).astype(F32) * eb
    be = jnp.sum(jnp.where((ln < N_EXPERTS) & (pends <= blk), 1.0, 0.0), axis=-1, keepdims=True)
    be = jnp.minimum(be, float(N_EXPERTS - 1))
    used = jnp.sum(jnp.where(ln == N_EXPERTS - 1, pends, 0.0), axis=-1, keepdims=True) * (1.0 / eb)
    meta_ref[...] = jnp.where(ln == 0, be, jnp.where(ln == 1, used, 0.0)).astype(I32)


def _plan(cnt, n_blocks_pad):
    nt = cnt.shape[0]
    shp = jax.ShapeDtypeStruct((nt, LANES), I32)
    return pl.pallas_call(
        _plan_body,
        out_shape=[shp, shp, shp, jax.ShapeDtypeStruct((nt, LANES), F32),
                   jax.ShapeDtypeStruct((8, LANES), I32),
                   jax.ShapeDtypeStruct((n_blocks_pad, LANES), I32)],
        name="dispatch_plan",
    )(cnt)


def _local_positions(idx_ref, rank_ref, lsf_ref):
    tm = idx_ref.shape[0]
    lane = lax.broadcasted_iota(I32, (tm, LANES), 1)
    idx = idx_ref[...]
    rank = rank_ref[...].astype(F32)
    ls_row = lsf_ref[0]
    out = []
    for k in range(TOP_K):
        base = jnp.sum(jnp.where(lane == idx[:, k:k + 1], ls_row, 0.0), axis=-1, keepdims=True)
        out.append((base + rank[:, k:k + 1]).astype(I32))
    return out


def _local_positions_lanes(idx_ref, rank_ref, lsf_ref):
    tm = idx_ref.shape[0]
    idx_t = idx_ref[...].astype(F32).T
    rank_t = rank_ref[...].astype(F32).T
    ls_col = jnp.broadcast_to(lsf_ref[0], (LANES, LANES)).T[:, 0:1]
    expert = lax.broadcasted_iota(I32, (LANES, tm), 0).astype(F32)
    out = []
    for k in range(TOP_K):
        base = jnp.sum(jnp.where(expert == idx_t[k:k + 1, :], ls_col, 0.0), axis=0, keepdims=True)
        out.append((base + rank_t[k:k + 1, :]).astype(I32))
    return out


PACK_W = D_MODEL // 2
U32 = jnp.uint32


def _pack_rows(x):
    xb = x.astype(BF16).astype(F32)
    hi = lax.bitcast_convert_type(xb[:, :PACK_W], U32)
    lo = lax.bitcast_convert_type(xb[:, PACK_W:], U32)
    return hi | (lo >> 16)


def _unpack_rows(w):
    hi = lax.bitcast_convert_type(w & jnp.uint32(0xFFFF0000), F32)
    lo = lax.bitcast_convert_type(w << 16, F32)
    return hi.astype(BF16), lo.astype(BF16)


TOTAL_LANE = LANES - 1
WAIT_CHUNK = 128
BIG_PIECE = 4


def _start_group_copies(tables, tile, make_copy):
    gs_ref, ls_ref, n8_ref = tables

    def group(e, carry):
        g0 = gs_ref[tile, e]
        l0 = ls_ref[tile, e]
        n = n8_ref[tile, e]
        nbig = n // BIG_PIECE

        def big(j, c):
            off = j * (BIG_PIECE * GROUP_ALIGN)
            make_copy(pl.multiple_of(l0 + off, GROUP_ALIGN), pl.multiple_of(g0 + off, GROUP_ALIGN),
                      BIG_PIECE * GROUP_ALIGN).start()
            return c

        def one(j, c):
            off = j * GROUP_ALIGN
            make_copy(pl.multiple_of(l0 + off, GROUP_ALIGN), pl.multiple_of(g0 + off, GROUP_ALIGN),
                      GROUP_ALIGN).start()
            return c

        lax.fori_loop(0, nbig, big, 0)
        lax.fori_loop(nbig * BIG_PIECE, n, one, 0)
        return carry

    lax.fori_loop(0, N_EXPERTS, group, 0)


def _wait_pieces(n, make_wait):
    def chunk(j, c):
        make_wait(WAIT_CHUNK).wait()
        return c

    lax.fori_loop(0, n // WAIT_CHUNK, chunk, 0)
    b = WAIT_CHUNK // 2
    while b >= 1:
        def _(b=b):
            make_wait(b).wait()
        pl.when((n & b) != 0)(_)
        b //= 2


def _dispatch_body(gs_ref, ls_ref, n8_ref, tail_ref, h_ref, idx_ref, rank_ref, lsf_ref, xs_ref,
                   xl, zbuf, sem):
    i = pl.program_id(0)
    last = i == pl.num_programs(0) - 1
    slot = i % 2
    tables = (gs_ref, ls_ref, n8_ref)
    tm = h_ref.shape[0]
    lrows = xl.shape[1]
    pos = lax.broadcasted_iota(I32, (lrows, tm), 0)
    lp = _local_positions_lanes(idx_ref, rank_ref, lsf_ref)
    hit = pos == lp[0]
    for k in range(1, TOP_K):
        hit = hit | (pos == lp[k])
    xl[slot] = _pack_rows(_dot(jnp.where(hit, 1.0, 0.0).astype(BF16), h_ref[...]))

    def copy_from(s):
        def make_copy(l, g, rows):
            return pltpu.make_async_copy(xl.at[s, pl.ds(l, rows), :], xs_ref.at[pl.ds(g, rows), :], sem.at[s])
        return make_copy

    def wait_on(s):
        def make_wait(pieces):
            rows = pieces * GROUP_ALIGN
            return pltpu.make_async_copy(xl.at[0, pl.ds(0, rows), :], xs_ref.at[pl.ds(0, rows), :], sem.at[s])
        return make_wait

    @pl.when(i >= 1)
    def _():
        _wait_pieces(n8_ref[i - 1, TOTAL_LANE], wait_on(1 - slot))

    _start_group_copies(tables, i, copy_from(slot))

    def start_tails():
        def per_expert(e, carry):
            g0 = tail_ref[0, e]

            def one(j, c):
                pltpu.make_async_copy(
                    zbuf, xs_ref.at[pl.ds(pl.multiple_of(g0 + j * GROUP_ALIGN, GROUP_ALIGN), GROUP_ALIGN), :],
                    sem.at[2]).start()
                return c

            lax.fori_loop(0, tail_ref[1, e], one, 0)
            return carry

        lax.fori_loop(0, N_EXPERTS, per_expert, 0)

    @pl.when(last)
    def _():
        zbuf[...] = jnp.zeros_like(zbuf)
        start_tails()
        _wait_pieces(n8_ref[i, TOTAL_LANE], wait_on(slot))
        _wait_pieces(tail_ref[1, TOTAL_LANE], wait_on(2))


def _dispatch(gs, ls, n8, tail, h2, idx, rank, lsf, n_slots):
    t = h2.shape[0]
    tm = min(MOE_TILE, t)
    nt = t // tm
    lrows = tm * TOP_K + N_EXPERTS * GROUP_ALIGN
    row = lambda i, *_: (i, 0)
    grid_spec = pltpu.PrefetchScalarGridSpec(
        num_scalar_prefetch=4,
        grid=(nt,),
        in_specs=[
            pl.BlockSpec((tm, D_MODEL), row),
            pl.BlockSpec((tm, LANES), row),
            pl.BlockSpec((tm, LANES), row),
            pl.BlockSpec((1, 1, LANES), lambda i, *_: (i, 0, 0)),
        ],
        out_specs=pl.BlockSpec(memory_space=pl.ANY),
        scratch_shapes=[pltpu.VMEM((2, lrows, PACK_W), U32), pltpu.VMEM((GROUP_ALIGN, PACK_W), U32),
                        pltpu.SemaphoreType.DMA((3,))],
    )
    return pl.pallas_call(
        _dispatch_body,
        grid_spec=grid_spec,
        out_shape=jax.ShapeDtypeStruct((n_slots, PACK_W), U32),
        compiler_params=pltpu.CompilerParams(dimension_semantics=("arbitrary",)),
        name="moe_dispatch",
    )(gs, ls, n8, tail, h2, idx, rank, lsf)


def _expert_body(be_ref, nu_ref, x_ref, wgu_ref, bgu_ref, wd_ref, bd_ref, y_ref, wgu_b, wd_b):
    j = pl.program_id(0)

    @pl.when(j < nu_ref[0])
    def _():
        @pl.when(jnp.logical_or(j == 0, be_ref[j] != be_ref[jnp.maximum(j - 1, 0)]))
        def _():
            wgu_b[...] = wgu_ref[0].astype(BF16)
            wd_b[...] = wd_ref[0].astype(BF16)

        gu = _dot(jnp.concatenate(_unpack_rows(x_ref[...]), axis=1), wgu_b[...]) + bgu_ref[0]
        gate = jnp.minimum(gu[:, :D_FF], SWIGLU_LIMIT)
        up = jnp.clip(gu[:, D_FF:], -SWIGLU_LIMIT, SWIGLU_LIMIT)
        act = (up + 1.0) * gate * _sigmoid(SWIGLU_ALPHA * gate)
        y_ref[...] = _pack_rows(_dot(act.astype(BF16), wd_b[...]) + bd_ref[0])


def _experts(block_e, n_used, xs, wgu, bgu, wd, bd):
    n_slots = xs.shape[0]
    nb = n_slots // EXPERT_BLOCK

    def blk(j, be, nu):
        return jnp.minimum(j, nu[0] - 1)

    grid_spec = pltpu.PrefetchScalarGridSpec(
        num_scalar_prefetch=2,
        grid=(nb,),
        in_specs=[
            pl.BlockSpec((EXPERT_BLOCK, PACK_W), lambda j, be, nu: (blk(j, be, nu), 0)),
            pl.BlockSpec((1, D_MODEL, 2 * D_FF), lambda j, be, nu: (be[blk(j, be, nu)], 0, 0)),
            pl.BlockSpec((1, 1, 2 * D_FF), lambda j, be, nu: (be[blk(j, be, nu)], 0, 0)),
            pl.BlockSpec((1, D_FF, D_MODEL), lambda j, be, nu: (be[blk(j, be, nu)], 0, 0)),
            pl.BlockSpec((1, 1, D_MODEL), lambda j, be, nu: (be[blk(j, be, nu)], 0, 0)),
        ],
        out_specs=pl.BlockSpec((EXPERT_BLOCK, PACK_W), lambda j, be, nu: (blk(j, be, nu), 0)),
        scratch_shapes=[pltpu.VMEM((D_MODEL, 2 * D_FF), BF16), pltpu.VMEM((D_FF, D_MODEL), BF16)],
    )
    return pl.pallas_call(
        _expert_body,
        grid_spec=grid_spec,
        out_shape=jax.ShapeDtypeStruct((n_slots, PACK_W), U32),
        compiler_params=pltpu.CompilerParams(dimension_semantics=("arbitrary",)),
        name="moe_experts",
    )(block_e, n_used, xs, wgu, bgu, wd, bd)


def _combine_body(gs_ref, ls_ref, n8_ref, x_ref, wt_ref, idx_ref, rank_ref, lsf_ref, y_ref, out_ref,
                  yl, sem):
    i = pl.program_id(0)
    slot = i % 2
    tables = (gs_ref, ls_ref, n8_ref)
    tm = x_ref.shape[0]
    lrows = yl.shape[1]

    def copy_into(s):
        def make_copy(l, g, rows):
            return pltpu.make_async_copy(y_ref.at[pl.ds(g, rows), :], yl.at[s, pl.ds(l, rows), :], sem.at[s])
        return make_copy

    def make_wait(pieces):
        rows = pieces * GROUP_ALIGN
        return pltpu.make_async_copy(y_ref.at[pl.ds(0, rows), :], yl.at[0, pl.ds(0, rows), :], sem.at[slot])

    @pl.when(i == 0)
    def _():
        yl[...] = jnp.zeros_like(yl)
        _start_group_copies(tables, i, copy_into(slot))

    @pl.when(i + 1 < pl.num_programs(0))
    def _():
        _start_group_copies(tables, i + 1, copy_into(1 - slot))

    pos = lax.broadcasted_iota(I32, (tm, lrows), 1)
    lp = _local_positions(idx_ref, rank_ref, lsf_ref)
    wt = wt_ref[...]
    wm = jnp.zeros((tm, lrows), F32)
    for k in range(TOP_K):
        wm = jnp.where(pos == lp[k], wt[:, k:k + 1], wm)
    _wait_pieces(n8_ref[i, TOTAL_LANE], make_wait)
    wmb = wm.astype(BF16)
    y_hi, y_lo = _unpack_rows(yl[slot])
    out_ref[:, :PACK_W] = x_ref[:, :PACK_W] + _dot(wmb, y_hi)
    out_ref[:, PACK_W:] = x_ref[:, PACK_W:] + _dot(wmb, y_lo)


def _combine(gs, ls, n8, x1, wts, idx, rank, lsf, y):
    t = x1.shape[0]
    tm = min(MOE_TILE, t)
    nt = t // tm
    lrows = tm * TOP_K + N_EXPERTS * GROUP_ALIGN
    row = lambda i, *_: (i, 0)
    grid_spec = pltpu.PrefetchScalarGridSpec(
        num_scalar_prefetch=3,
        grid=(nt,),
        in_specs=[
            pl.BlockSpec((tm, D_MODEL), row),
            pl.BlockSpec((tm, LANES), row),
            pl.BlockSpec((tm, LANES), row),
            pl.BlockSpec((tm, LANES), row),
            pl.BlockSpec((1, 1, LANES), lambda i, *_: (i, 0, 0)),
            pl.BlockSpec(memory_space=pl.ANY),
        ],
        out_specs=pl.BlockSpec((tm, D_MODEL), row),
        scratch_shapes=[pltpu.VMEM((2, lrows, PACK_W), U32), pltpu.SemaphoreType.DMA((2,))],
    )
    return pl.pallas_call(
        _combine_body,
        grid_spec=grid_spec,
        out_shape=jax.ShapeDtypeStruct((t, D_MODEL), F32),
        compiler_params=pltpu.CompilerParams(dimension_semantics=("arbitrary",)),
        name="moe_combine",
    )(gs, ls, n8, x1, wts, idx, rank, lsf, y)


def _pad_lanes(v, fill=0.0):
    v = v.astype(F32).reshape(1, -1)
    return jnp.pad(v, ((0, 0), (0, LANES - v.shape[1])), constant_values=fill)


def _mixer(x2, mem, rel_table, attn_norm, w_in, b_gate, dn_conv, dn_a_log, dn_dt_bias, dn_out_norm,
           da_q_norm, da_k_norm, da_lambda, da_subln, mem_norm, w_mem_kv, mx_q_norm, mx_k_norm,
           w_branch, w_out, batch, seq):
    wp = jnp.concatenate([w_in[:, :W_AB_LO], w_in[:, W_AB_HI:]], axis=1).astype(BF16)
    wab = jnp.pad(w_in[:, W_AB_LO:W_AB_HI], ((0, 0), (0, LANES - (W_AB_HI - W_AB_LO))))
    p, ab = _inproj(x2, attn_norm.reshape(1, -1), wp, wab)

    o_dn = _deltanet(p, ab, dn_conv, _pad_lanes(dn_a_log), _pad_lanes(dn_dt_bias),
                     dn_out_norm.reshape(1, -1), batch, seq)

    tq = min(ATT_BLOCK, seq)
    bias = _bias_tiles(rel_table.T, tq)
    o_da = _attention(p, bias, jnp.tile(da_q_norm, 2).reshape(1, -1), jnp.tile(da_k_norm, 2).reshape(1, -1),
                      da_lambda, da_subln.reshape(-1, 1), batch, seq)

    mk, mv = _memkv(mem, mem_norm.reshape(1, -1), w_mem_kv.astype(BF16), mx_k_norm.reshape(1, -1))
    return _merge(x2, o_dn, o_da, p, mk, mv, mx_q_norm.reshape(1, -1), b_gate.reshape(3, D_MODEL),
                  w_branch.astype(BF16), w_out.astype(BF16), seq)


def _moe(x1, ffn_norm, w_router, b_router, w_gate_up, b_gate_up, w_down, b_down):
    t = x1.shape[0]
    nt = t // min(MOE_TILE, t)
    max_rows = t * TOP_K + nt * N_EXPERTS * (GROUP_ALIGN - 1)
    n_blocks = -(-max_rows // EXPERT_BLOCK) + N_EXPERTS
    n_blocks_pad = -(-n_blocks // 8) * 8
    n_slots = n_blocks * EXPERT_BLOCK

    wr = jnp.pad(w_router, ((0, 0), (0, LANES - N_EXPERTS)))
    h2, idx, wts, rank, cnt = _router(x1, ffn_norm.reshape(1, -1), wr, _pad_lanes(b_router, NEG))
    gs, ls, n8, lsf, tail, meta = _plan(cnt[:, 0, :], n_blocks_pad)
    block_e = meta[:n_blocks, 0]
    n_used = meta[0:1, 1]
    lsf = lsf.reshape(nt, 1, LANES)

    xs = _dispatch(gs, ls, n8, tail, h2, idx, rank, lsf, n_slots)
    y = _experts(block_e, n_used, xs, w_gate_up, b_gate_up.reshape(N_EXPERTS, 1, -1),
                 w_down, b_down.reshape(N_EXPERTS, 1, -1))
    return _combine(gs, ls, n8, x1, wts, idx, rank, lsf, y)


def kernel(x, mem, rel_table, attn_norm, w_in, b_gate, dn_conv, dn_a_log, dn_dt_bias, dn_out_norm,
           da_q_norm, da_k_norm, da_lambda, da_subln, mem_norm, w_mem_kv, mx_q_norm, mx_k_norm,
           w_branch, w_out, ffn_norm, w_router, b_router, w_gate_up, b_gate_up, w_down, b_down):
    batch, seq, d = x.shape
    x2 = x.reshape(batch * seq, d)
    x1 = _mixer(x2, mem, rel_table, attn_norm[0], w_in[0], b_gate[0], dn_conv[0], dn_a_log[0],
                dn_dt_bias[0], dn_out_norm[0], da_q_norm[0], da_k_norm[0], da_lambda[0], da_subln[0],
                mem_norm[0], w_mem_kv[0], mx_q_norm[0], mx_k_norm[0], w_branch[0], w_out[0], batch, seq)
    out = _moe(x1, ffn_norm[0], w_router[0], b_router[0], w_gate_up[0], b_gate_up[0], w_down[0],
               b_down[0])
    return out.reshape(batch, seq, d)
```

```python
import functools
import math

import jax
import jax.numpy as jnp
from jax import lax
from jax.experimental import pallas as pl
from jax.experimental.pallas import tpu as pltpu

F32 = jnp.float32
BF16 = jnp.bfloat16
I32 = jnp.int32

D_MODEL = 1024
EPS = 1e-6
LANES = 128

DN_HEADS = 4
DN_DK = 128
DN_CHUNK = 64
DN_CONV = 4

DA_HEADS = 4
DA_DH = 64

MX_HEADS = 4
MX_DH = 128

REL_BUCKETS = 32
REL_MAX_DIST = 128

N_EXPERTS = 32
TOP_K = 4
D_FF = 1024
SWIGLU_LIMIT = 7.0
SWIGLU_ALPHA = 1.702
EXPERT_BLOCK = 512
MOE_TILE = 256
GROUP_ALIGN = 8

LAM_INIT = 0.8 - 0.6 * math.exp(-0.3 * 0)
LOG2E = 1.4426950408889634
NEG = -1e30

P_DNQ, P_DNK, P_DNV, P_DNZ = 0, 512, 1024, 1536
P_DAQ, P_DAK, P_DAV = 2048, 2560, 3072
P_MXQ = 3584
P_GATE = 4096
P_COLS = 7168
W_AB_LO, W_AB_HI = 2048, 2056


def _dot(a, b):
    return jnp.dot(a, b, preferred_element_type=F32)


def _dot_nt(a, b):
    return lax.dot_general(a, b, (((1,), (1,)), ((), ())), preferred_element_type=F32)


def _dot_tn(a, b):
    return lax.dot_general(a, b, (((0,), (0,)), ((), ())), preferred_element_type=F32)


def _split(x):
    hi = x.astype(BF16)
    lo = (x - hi.astype(F32)).astype(BF16)
    return hi, lo


def _dot3(a, b):
    ah, al = _split(a)
    bh, bl = _split(b)
    return _dot(ah, bh) + _dot(ah, bl) + _dot(al, bh)


def _sigmoid(x):
    return 1.0 / (1.0 + jnp.exp(-x))


def _rms(x, n):
    return lax.rsqrt(jnp.sum(x * x, axis=-1, keepdims=True) * (1.0 / n) + EPS)


def _inproj_body(x_ref, g_ref, w_ref, wab_ref, p_ref, ab_ref, h_scr):
    @pl.when(pl.program_id(1) == 0)
    def _():
        x = x_ref[...]
        h = x * _rms(x, D_MODEL) * g_ref[...]
        h_scr[...] = h.astype(BF16)
        ab_ref[...] = _dot3(h, wab_ref[...])

    p_ref[...] = _dot(h_scr[...], w_ref[...]).astype(p_ref.dtype)


def _inproj(x2, gain, wp, wab):
    t = x2.shape[0]
    tm = min(2048, t)
    tn = 1024
    return pl.pallas_call(
        _inproj_body,
        grid=(t // tm, P_COLS // tn),
        in_specs=[
            pl.BlockSpec((tm, D_MODEL), lambda i, j: (i, 0)),
            pl.BlockSpec((1, D_MODEL), lambda i, j: (0, 0)),
            pl.BlockSpec((D_MODEL, tn), lambda i, j: (0, j)),
            pl.BlockSpec((D_MODEL, LANES), lambda i, j: (0, 0)),
        ],
        out_specs=[
            pl.BlockSpec((tm, tn), lambda i, j: (i, j)),
            pl.BlockSpec((tm, LANES), lambda i, j: (i, 0)),
        ],
        out_shape=[
            jax.ShapeDtypeStruct((t, P_COLS), BF16),
            jax.ShapeDtypeStruct((t, LANES), F32),
        ],
        scratch_shapes=[pltpu.VMEM((tm, D_MODEL), BF16)],
        compiler_params=pltpu.CompilerParams(dimension_semantics=("parallel", "arbitrary")),
        name="inproj",
    )(x2, gain, wp, wab)


DN_HALO = 16
DN_SCAN_CHUNK = 256


def _deltanet_body(q_ref, k_ref, v_ref, z_ref, qh_ref, kh_ref, vh_ref, ab_ref, cw_ref, alog_ref,
                   dtb_ref, on_ref, o_ref, stage, qs, ks, vs, s_scr):
    i = pl.program_id(1)
    tc = q_ref.shape[0]
    hw = DN_HEADS * DN_DK

    @pl.when(i == 0)
    def _():
        s_scr[...] = jnp.zeros_like(s_scr)

    for src, halo, dst, off, kind in ((q_ref, qh_ref, qs, 0, "q"), (k_ref, kh_ref, ks, hw, "k"),
                                      (v_ref, vh_ref, vs, 2 * hw, "v")):
        hal = halo[...].astype(F32)
        stage[0:DN_HALO, :] = jnp.where(i == 0, 0.0, hal)
        stage[DN_HALO:DN_HALO + tc, :] = src[...].astype(F32)
        base = DN_HALO - (DN_CONV - 1)
        y = stage[base:base + tc, :] * cw_ref[0:1, off:off + hw]
        for j in range(1, DN_CONV):
            y = y + stage[base + j:base + j + tc, :] * cw_ref[j:j + 1, off:off + hw]
        y = y * _sigmoid(y)
        if kind == "v":
            dst[...] = y
        else:
            for h in range(DN_HEADS):
                sl = slice(h * DN_DK, (h + 1) * DN_DK)
                yh = y[:, sl]
                r = lax.rsqrt(jnp.sum(yh * yh, axis=-1, keepdims=True) + EPS)
                if kind == "q":
                    r = r * (DN_DK ** -0.5)
                dst[:, sl] = yh * r

    c = min(DN_SCAN_CHUNK, tc)
    row = lax.broadcasted_iota(I32, (c, c), 0)
    col = lax.broadcasted_iota(I32, (c, c), 1)
    incl = row >= col
    strict = row > col
    same_blk = (row // DN_CHUNK) == (col // DN_CHUNK)
    tri = jnp.where(incl, 1.0, 0.0).astype(BF16)
    eye = jnp.where(row == col, 1.0, 0.0)
    neg_a = -jnp.exp(alog_ref[...])
    dtb = dtb_ref[...]

    def chunk(ci, carry):
        r0 = pl.multiple_of(ci * c, c)
        abc = ab_ref[pl.ds(r0, c), :]
        a_in = abc + dtb
        g_all = neg_a * (jnp.maximum(a_in, 0.0) + jnp.log(1.0 + jnp.exp(-jnp.abs(a_in))))
        beta_all = _sigmoid(abc)
        zc = z_ref[pl.ds(r0, c), :].astype(F32)
        hs = range(DN_HEADS)
        sls = [slice(h * DN_DK, (h + 1) * DN_DK) for h in hs]
        q = [qs[pl.ds(r0, c), sl] for sl in sls]
        k = [ks[pl.ds(r0, c), sl] for sl in sls]
        v = [vs[pl.ds(r0, c), sl] for sl in sls]
        beta = [beta_all[:, DN_HEADS + h:DN_HEADS + h + 1] for h in hs]
        g_hi, g_lo = _split(g_all)
        gc_all = _dot(tri, g_hi) + _dot(tri, g_lo)
        gc_rows = gc_all.T
        kb = [k[h].astype(BF16) for h in hs]
        qkk = [_dot_nt(jnp.concatenate([q[h].astype(BF16), kb[h]], axis=0), kb[h]) for h in hs]
        gc = [gc_all[:, h:h + 1] for h in hs]
        decay = [jnp.where(incl, jnp.exp(gc[h] - gc_rows[h:h + 1, :]), 0.0) for h in hs]
        lower = [jnp.where(strict, qkk[h][c:] * decay[h] * beta[h], 0.0) for h in hs]
        pw = [jnp.where(same_blk, -lower[h], 0.0) for h in hs]
        dinv = [eye + pw[h] for h in hs]
        pwb = [pw[h].astype(BF16) for h in hs]
        for _ in range(int(math.log2(DN_CHUNK)) - 1):
            pwb = [_dot(pwb[h], pwb[h]).astype(BF16) for h in hs]
            dinv = [dinv[h] + _dot(dinv[h].astype(BF16), pwb[h]) for h in hs]
        dinv_b = [dinv[h].astype(BF16) for h in hs]
        pw = [-_dot(dinv_b[h], jnp.where(same_blk, 0.0, lower[h]).astype(BF16)) for h in hs]
        xm = [eye + pw[h] for h in hs]
        pwb = [pw[h].astype(BF16) for h in hs]
        for _ in range(int(math.log2(c // DN_CHUNK)) - 1):
            pwb = [_dot(pwb[h], pwb[h]).astype(BF16) for h in hs]
            xm = [xm[h] + _dot(xm[h].astype(BF16), pwb[h]) for h in hs]
        inv = [_dot(xm[h].astype(BF16), dinv_b[h]).astype(BF16) for h in hs]
        egc = [jnp.exp(gc[h]) for h in hs]
        rhs = [jnp.concatenate([v[h] * beta[h], k[h] * (beta[h] * egc[h])], axis=1).astype(BF16) for h in hs]
        sol = [_dot(inv[h], rhs[h]) for h in hs]
        qkm = [jnp.where(incl, qkk[h][:c] * decay[h], 0.0).astype(BF16) for h in hs]
        gl = [gc[h][c - 1:c, :] for h in hs]
        state = [s_scr[h] for h in hs]
        ws = [_dot(jnp.concatenate([sol[h][:, DN_DK:].astype(BF16), (q[h] * egc[h]).astype(BF16)], axis=0),
                   state[h].astype(BF16)) for h in hs]
        v_new = [sol[h][:, :DN_DK] - ws[h][:c] for h in hs]
        o = [ws[h][c:] + _dot(qkm[h], v_new[h].astype(BF16)) for h in hs]
        for h in hs:
            s_scr[h] = state[h] * jnp.exp(gl[h]) + _dot_tn(kb[h], (v_new[h] * jnp.exp(gl[h] - gc[h])).astype(BF16))
        for h in hs:
            zz = zc[:, sls[h]]
            on = o[h] * _rms(o[h], DN_DK) * on_ref[...]
            o_ref[pl.ds(r0, c), sls[h]] = (on * (zz * _sigmoid(zz))).astype(o_ref.dtype)
        return carry

    lax.fori_loop(0, tc // c, chunk, 0, unroll=True)


def _deltanet(p, ab, conv_w, alog_row, dtb_row, out_norm, batch, seq):
    t = batch * seq
    tc = min(1024, seq)
    nt = seq // tc
    hw = DN_HEADS * DN_DK

    def main(cb):
        return pl.BlockSpec((tc, hw), lambda b, i: (b * nt + i, cb))

    def halo(cb):
        return pl.BlockSpec(
            (DN_HALO, hw),
            lambda b, i: (jnp.maximum((b * seq + i * tc) // DN_HALO - 1, 0), cb))

    return pl.pallas_call(
        _deltanet_body,
        grid=(batch, nt),
        in_specs=[
            main(P_DNQ // hw), main(P_DNK // hw), main(P_DNV // hw), main(P_DNZ // hw),
            halo(P_DNQ // hw), halo(P_DNK // hw), halo(P_DNV // hw),
            pl.BlockSpec((tc, LANES), lambda b, i: (b * nt + i, 0)),
            pl.BlockSpec((DN_CONV, 3 * hw), lambda b, i: (0, 0)),
            pl.BlockSpec((1, LANES), lambda b, i: (0, 0)),
            pl.BlockSpec((1, LANES), lambda b, i: (0, 0)),
            pl.BlockSpec((1, DN_DK), lambda b, i: (0, 0)),
        ],
        out_specs=pl.BlockSpec((tc, hw), lambda b, i: (b * nt + i, 0)),
        out_shape=jax.ShapeDtypeStruct((t, hw), BF16),
        scratch_shapes=[
            pltpu.VMEM((DN_HALO + tc, hw), F32),
            pltpu.VMEM((tc, hw), F32),
            pltpu.VMEM((tc, hw), F32),
            pltpu.VMEM((tc, hw), F32),
            pltpu.VMEM((DN_HEADS, DN_DK, DN_DK), F32),
        ],
        compiler_params=pltpu.CompilerParams(dimension_semantics=("parallel", "arbitrary")),
        name="deltanet",
    )(p, p, p, p, p, p, p, ab, conv_w, alog_row, dtb_row, out_norm)


ATT_BLOCK = 512


def _bias_body(tbl_ref, o_ref):
    h = pl.program_id(0)
    tq = o_ref.shape[2]
    key = lax.broadcasted_iota(I32, (tq, tq), 0)
    qry = lax.broadcasted_iota(I32, (tq, tq), 1)
    max_exact = REL_BUCKETS // 2
    far = tbl_ref[h, REL_BUCKETS - 1]
    for d in range(2):
        n = qry - key + d * tq
        nn = jnp.maximum(n, 0)
        nf = jnp.maximum(nn, 1).astype(F32)
        large = max_exact + (jnp.log(nf / max_exact) / math.log(REL_MAX_DIST / max_exact)
                             * (REL_BUCKETS - max_exact)).astype(I32)
        large = jnp.minimum(large, REL_BUCKETS - 1)
        bucket = jnp.where(nn < max_exact, nn, large)
        val = jnp.zeros((tq, tq), F32)
        for b in range(REL_BUCKETS):
            val = jnp.where(bucket == b, tbl_ref[h, b], val)
        o_ref[0, d] = jnp.where(n >= 0, (val - far) * LOG2E, NEG)


def _bias_tiles(tbl_t, tq):
    return pl.pallas_call(
        _bias_body,
        grid=(DA_HEADS,),
        in_specs=[pl.BlockSpec(memory_space=pltpu.SMEM)],
        out_specs=pl.BlockSpec((1, 2, tq, tq), lambda h: (h, 0, 0, 0)),
        out_shape=jax.ShapeDtypeStruct((DA_HEADS, 2, tq, tq), F32),
        name="t5_bias_tiles",
    )(tbl_t)


DA_DV = 2 * DA_DH
DA_VROWS = DA_DV + 16


BOUND_SLACK = 1.02
MAX_SHIFT_GAP = 110.0


def _attn_body(q_ref, qn_ref, k_ref, v_ref, bias_ref, qg_ref, kg_ref, lam_ref, sg_ref, o_ref,
               kn, vt, kst, qc_s, bd_s, flag_s, m_s, acc_s):
    qi = pl.program_id(2)
    slot = qi % 2
    tq = q_ref.shape[0]
    seq = k_ref.shape[0]
    tk = tq
    lo_mask = lax.broadcasted_iota(I32, (1, DA_DV), 1) < DA_DH

    def group_norm(x, gain):
        x2 = x * x
        lo = jnp.sum(jnp.where(lo_mask, x2, 0.0), axis=-1, keepdims=True)
        hi = jnp.sum(jnp.where(lo_mask, 0.0, x2), axis=-1, keepdims=True)
        r = jnp.where(lo_mask, lax.rsqrt(lo * (1.0 / DA_DH) + EPS), lax.rsqrt(hi * (1.0 / DA_DH) + EPS))
        return x * r * gain

    @pl.when(qi == 0)
    def _():
        ones = jnp.ones((DA_VROWS - DA_DV, tk), BF16)

        def body(c, kmax2):
            r0 = pl.multiple_of(c * tk, tk)
            kb = group_norm(k_ref[pl.ds(r0, tk), :].astype(F32), kg_ref[...]).astype(BF16)
            kn[pl.ds(r0, tk), :] = kb
            vt[c, 0:DA_DV, :] = v_ref[pl.ds(r0, tk), :].astype(F32).T.astype(BF16)
            vt[c, DA_DV:DA_VROWS, :] = ones
            k2 = kb.astype(F32)
            k2 = k2 * k2
            lo = jnp.max(jnp.sum(jnp.where(lo_mask, k2, 0.0), axis=-1, keepdims=True), axis=0, keepdims=True)
            hi = jnp.max(jnp.sum(jnp.where(lo_mask, 0.0, k2), axis=-1, keepdims=True), axis=0, keepdims=True)
            return jnp.maximum(kmax2, jnp.where(lo_mask, lo, hi))
        kst[0:1, :] = lax.fori_loop(0, seq // tk, body, jnp.zeros((1, DA_DV), F32), unroll=2)
        b0 = bias_ref[0, 0]
        b1 = bias_ref[0, 1]
        bmax = jnp.maximum(jnp.max(jnp.maximum(b0, b1), axis=0, keepdims=True), 0.0)
        bmin = jnp.minimum(jnp.min(jnp.minimum(jnp.where(b0 > 0.5 * NEG, b0, 0.0), b1), axis=0, keepdims=True), 0.0)
        kst[1:2, :] = jnp.broadcast_to(jnp.max(bmax, axis=1, keepdims=True), (1, DA_DV))
        kst[2:3, :] = jnp.broadcast_to(jnp.min(bmin, axis=1, keepdims=True), (1, DA_DV))

    def prepare(src_ref, s):
        q = group_norm(src_ref[...].astype(F32), qg_ref[...]) * (DA_DH ** -0.5 * LOG2E)
        qc_s[s] = jnp.concatenate([jnp.where(lo_mask, q, 0.0), jnp.where(lo_mask, 0.0, q)], axis=0).astype(BF16)
        q2 = q * q * kst[0:1, :]
        ones8 = jnp.ones((8, DA_DV), BF16)
        bound = []
        for m in range(2):
            q2m = jnp.where(lo_mask, q2, 0.0) if m == 0 else jnp.where(lo_mask, 0.0, q2)
            bound.append(jnp.sqrt(_dot_nt(ones8, q2m.astype(BF16))[0:1, :]) * BOUND_SLACK)
        bound = jnp.concatenate(bound, axis=1)
        bd_s[s] = bound
        worst = jnp.max(2.0 * bound, axis=1, keepdims=True) + kst[1:2, 0:1] - kst[2:3, 0:1]
        flag_s[s] = jnp.where(worst[0, 0] <= MAX_SHIFT_GAP, 1, 0).astype(I32)

    @pl.when(qi == 0)
    def _():
        prepare(q_ref, slot)

    qcat = qc_s[slot]
    bound = bd_s[slot]
    bmax = kst[1:2, 0:1]
    safe = flag_s[slot] == 1
    acc_s[...] = jnp.zeros_like(acc_s)

    def block(j, d):
        r0 = pl.multiple_of(j * tk, tk)
        st = _dot_nt(kn[pl.ds(r0, tk), :], qcat)
        if d is not None:
            bias = bias_ref[0, d]
            st = st + jnp.concatenate([bias, bias], axis=1)
        m_prev = m_s[...]
        m_new = jnp.maximum(m_prev, jnp.max(st, axis=0, keepdims=True))
        alpha = jnp.exp2(m_prev - m_new)
        acc_s[...] = alpha * acc_s[...] + _dot(vt[j], jnp.exp2(st - m_new).astype(BF16))
        m_s[...] = m_new

    def blocks_fixed(js, ds=None):
        sts = [_dot_nt(kn[pl.ds(pl.multiple_of(j * tk, tk), tk), :], qcat) for j in js]
        if ds is not None:
            biases = [bias_ref[0, d] for d in ds]
            sts = [st + jnp.concatenate([b, b], axis=1) for st, b in zip(sts, biases)]
        shift = m_s[...]
        pts = [jnp.exp2(st - shift).astype(BF16) for st in sts]
        tot = _dot(vt[js[0]], pts[0])
        for j, pt in zip(js[1:], pts[1:]):
            tot = tot + _dot(vt[j], pt)
        acc_s[...] = acc_s[...] + tot

    def run(fixed_shift):
        n_far = jnp.maximum(qi - 1, 0)

        if fixed_shift:
            def far_quad(jj, carry):
                blocks_fixed([4 * jj + u for u in range(4)])
                return carry

            lax.fori_loop(0, n_far // 4, far_quad, 0)
            rem = n_far % 4

            @pl.when(rem >= 2)
            def _():
                blocks_fixed([n_far - rem, n_far - rem + 1])

            @pl.when(rem % 2 == 1)
            def _():
                blocks_fixed([n_far - 1])

            @pl.when(qi >= 1)
            def _():
                blocks_fixed([qi - 1, qi], [1, 0])
                prepare(qn_ref, 1 - slot)

            @pl.when(qi == 0)
            def _():
                blocks_fixed([qi], [0])
                prepare(qn_ref, 1 - slot)
        else:
            def far_one(j, carry):
                block(j, None)
                return carry

            lax.fori_loop(0, n_far, far_one, 0)

            @pl.when(qi >= 1)
            def _():
                block(qi - 1, 1)

            block(qi, 0)
            prepare(qn_ref, 1 - slot)

    @pl.when(safe)
    def _():
        m_s[...] = bound + bmax
        run(True)

    @pl.when(jnp.logical_not(safe))
    def _():
        m_s[...] = jnp.full(m_s.shape, NEG, F32)
        run(False)

    lam_p = lam_ref[...]
    lam = (jnp.exp(jnp.sum(lam_p[0:1, :] * lam_p[1:2, :], axis=-1, keepdims=True))
           - jnp.exp(jnp.sum(lam_p[2:3, :] * lam_p[3:4, :], axis=-1, keepdims=True)) + LAM_INIT)
    a0 = acc_s[:, 0:tq]
    a1 = acc_s[:, tq:2 * tq]
    ot = a0[0:DA_DV] / a0[DA_DV:DA_DV + 1] - lam * (a1[0:DA_DV] / a1[DA_DV:DA_DV + 1])
    r = lax.rsqrt(jnp.sum(ot * ot, axis=0, keepdims=True) * (1.0 / DA_DV) + EPS)
    ot = ot * r * (sg_ref[...] * (1.0 - LAM_INIT))
    o_ref[...] = ot.T.astype(o_ref.dtype)


def _attention(p, bias, qg, kg, lam_p, subln, batch, seq):
    t = batch * seq
    tq = min(ATT_BLOCK, seq)
    nq = seq // tq
    dv = DA_DV
    return pl.pallas_call(
        _attn_body,
        grid=(batch, DA_HEADS, nq),
        in_specs=[
            pl.BlockSpec((tq, dv), lambda b, h, i: (b * nq + i, P_DAQ // dv + h)),
            pl.BlockSpec((tq, dv), lambda b, h, i: (b * nq + jnp.minimum(i + 1, nq - 1), P_DAQ // dv + h)),
            pl.BlockSpec((seq, dv), lambda b, h, i: (b, P_DAK // dv + h)),
            pl.BlockSpec((seq, dv), lambda b, h, i: (b, P_DAV // dv + h)),
            pl.BlockSpec((1, 2, tq, tq), lambda b, h, i: (h, 0, 0, 0)),
            pl.BlockSpec((1, dv), lambda b, h, i: (0, 0)),
            pl.BlockSpec((1, dv), lambda b, h, i: (0, 0)),
            pl.BlockSpec((4, DA_DH), lambda b, h, i: (0, 0)),
            pl.BlockSpec((dv, 1), lambda b, h, i: (0, 0)),
        ],
        out_specs=pl.BlockSpec((tq, dv), lambda b, h, i: (b * nq + i, h)),
        out_shape=jax.ShapeDtypeStruct((t, DA_HEADS * dv), BF16),
        scratch_shapes=[
            pltpu.VMEM((seq, dv), BF16),
            pltpu.VMEM((seq // tq, DA_VROWS, tq), BF16),
            pltpu.VMEM((8, dv), F32),
            pltpu.VMEM((2, 2 * tq, dv), BF16),
            pltpu.VMEM((2, 1, 2 * tq), F32),
            pltpu.SMEM((2,), I32),
            pltpu.VMEM((1, 2 * tq), F32),
            pltpu.VMEM((DA_VROWS, 2 * tq), F32),
        ],
        compiler_params=pltpu.CompilerParams(dimension_semantics=("parallel", "parallel", "arbitrary")),
        name="diff_attention",
    )(p, p, p, p, bias, qg, kg, lam_p, subln)


def _memkv_body(mem_ref, mg_ref, w_ref, kg_ref, mk_ref, mv_ref):
    x = mem_ref[0]
    xn = x * _rms(x, D_MODEL) * mg_ref[...]
    kv = _dot(xn.astype(BF16), w_ref[...])
    hw = MX_HEADS * MX_DH
    for h in range(MX_HEADS):
        sl = slice(h * MX_DH, (h + 1) * MX_DH)
        kh = kv[:, sl]
        mk_ref[0, :, sl] = (kh * _rms(kh, MX_DH) * kg_ref[...]).astype(BF16)
    mv_ref[0] = kv[:, hw:].astype(BF16)


def _memkv(mem, mem_norm, w_kv, k_norm):
    b, n, _ = mem.shape
    hw = MX_HEADS * MX_DH
    return pl.pallas_call(
        _memkv_body,
        grid=(b,),
        in_specs=[
            pl.BlockSpec((1, n, D_MODEL), lambda i: (i, 0, 0)),
            pl.BlockSpec((1, D_MODEL), lambda i: (0, 0)),
            pl.BlockSpec((D_MODEL, 2 * hw), lambda i: (0, 0)),
            pl.BlockSpec((1, MX_DH), lambda i: (0, 0)),
        ],
        out_specs=[pl.BlockSpec((1, n, hw), lambda i: (i, 0, 0))] * 2,
        out_shape=[jax.ShapeDtypeStruct((b, n, hw), BF16)] * 2,
        name="memory_kv",
    )(mem, mem_norm, w_kv, k_norm)


def _merge_body(x_ref, odn_ref, oda_ref, mxq_ref, g0_ref, g1_ref, g2_ref, mk_ref, mv_ref, qg_ref,
                bg_ref, wb_ref, wo_ref, out_ref, omx):
    for h in range(MX_HEADS):
        sl = slice(h * MX_DH, (h + 1) * MX_DH)
        qh = mxq_ref[:, sl].astype(F32)
        qh = qh * _rms(qh, MX_DH) * qg_ref[...] * (MX_DH ** -0.5 * LOG2E)
        s = _dot_nt(qh.astype(BF16), mk_ref[0, :, sl])
        p = jnp.exp2(s - jnp.max(s, axis=-1, keepdims=True))
        oh = _dot(p.astype(BF16), mv_ref[0, :, sl]) / jnp.sum(p, axis=-1, keepdims=True)
        omx[:, sl] = oh.astype(BF16)
    y = None
    for r, (o_r, g_r) in enumerate(((odn_ref, g0_ref), (oda_ref, g1_ref), (omx, g2_ref))):
        gate = _sigmoid(g_r[...].astype(F32) + bg_ref[r:r + 1, :])
        term = gate * _dot(o_r[...], wb_ref[r])
        y = term if y is None else y + term
    out_ref[...] = x_ref[...] + _dot(y.astype(BF16), wo_ref[...])


def _merge(x2, o_dn, o_da, p, mk, mv, q_norm, b_gate, w_branch, w_out, seq):
    t = x2.shape[0]
    tm = min(1024, seq)
    nt = seq // tm
    bw = 512
    n_mem = mk.shape[1]
    return pl.pallas_call(
        _merge_body,
        grid=(t // tm,),
        in_specs=[
            pl.BlockSpec((tm, D_MODEL), lambda i: (i, 0)),
            pl.BlockSpec((tm, bw), lambda i: (i, 0)),
            pl.BlockSpec((tm, bw), lambda i: (i, 0)),
            pl.BlockSpec((tm, bw), lambda i: (i, P_MXQ // bw)),
            pl.BlockSpec((tm, D_MODEL), lambda i: (i, P_GATE // D_MODEL)),
            pl.BlockSpec((tm, D_MODEL), lambda i: (i, P_GATE // D_MODEL + 1)),
            pl.BlockSpec((tm, D_MODEL), lambda i: (i, P_GATE // D_MODEL + 2)),
            pl.BlockSpec((1, n_mem, bw), lambda i: (i // nt, 0, 0)),
            pl.BlockSpec((1, n_mem, bw), lambda i: (i // nt, 0, 0)),
            pl.BlockSpec((1, MX_DH), lambda i: (0, 0)),
            pl.BlockSpec((3, D_MODEL), lambda i: (0, 0)),
            pl.BlockSpec((3, bw, D_MODEL), lambda i: (0, 0, 0)),
            pl.BlockSpec((D_MODEL, D_MODEL), lambda i: (0, 0)),
        ],
        out_specs=pl.BlockSpec((tm, D_MODEL), lambda i: (i, 0)),
        out_shape=jax.ShapeDtypeStruct((t, D_MODEL), F32),
        scratch_shapes=[pltpu.VMEM((tm, bw), BF16)],
        compiler_params=pltpu.CompilerParams(dimension_semantics=("parallel",)),
        name="merge",
    )(x2, o_dn, o_da, p, p, p, p, mk, mv, q_norm, b_gate, w_branch, w_out)


def _router_body(x_ref, g_ref, wr_ref, br_ref, h_ref, idx_ref, wt_ref, rank_ref, cnt_ref):
    tm = x_ref.shape[0]
    x = x_ref[...]
    h = x * _rms(x, D_MODEL) * g_ref[...]
    h_ref[...] = h.astype(BF16)
    logits = _dot3(h, wr_ref[...]) + br_ref[...]
    lane = lax.broadcasted_iota(I32, (tm, LANES), 1)
    lane_f = lane.astype(F32)
    work = logits
    sel = jnp.zeros((tm, LANES), F32)
    vals, idxs = [], []
    for _ in range(TOP_K):
        mx = jnp.max(work, axis=-1, keepdims=True)
        ik = jnp.min(jnp.where(work == mx, lane_f, float(LANES)), axis=-1, keepdims=True)
        hit = lane_f == ik
        sel = jnp.where(hit, 1.0, sel)
        work = jnp.where(hit, -jnp.inf, work)
        vals.append(mx)
        idxs.append(ik)
    es = [jnp.exp(v - vals[0]) for v in vals]
    den = es[0] + es[1] + es[2] + es[3]
    sub = cnt_ref.shape[0]
    mt = tm // sub
    r = lax.broadcasted_iota(I32, (tm, tm), 0)
    c = lax.broadcasted_iota(I32, (tm, tm), 1)
    tril = jnp.where((r > c) & (r // mt == c // mt), 1.0, 0.0).astype(BF16)
    cum = _dot(tril, sel.astype(BF16))
    idx_o = jnp.zeros((tm, LANES), F32)
    wt_o = jnp.zeros((tm, LANES), F32)
    rank_o = jnp.zeros((tm, LANES), F32)
    for k in range(TOP_K):
        rk = jnp.sum(jnp.where(lane_f == idxs[k], cum, 0.0), axis=-1, keepdims=True)
        idx_o = jnp.where(lane == k, idxs[k], idx_o)
        wt_o = jnp.where(lane == k, es[k] / den, wt_o)
        rank_o = jnp.where(lane == k, rk, rank_o)
    idx_ref[...] = idx_o.astype(I32)
    wt_ref[...] = wt_o
    rank_ref[...] = rank_o.astype(I32)
    for s in range(sub):
        cnt_ref[s] = jnp.broadcast_to(jnp.sum(sel[s * mt:(s + 1) * mt], axis=0, keepdims=True), (8, LANES))


def _router(x1, gain, w_r, b_r):
    t = x1.shape[0]
    mt = min(MOE_TILE, t)
    sub = 2 if t % (2 * mt) == 0 else 1
    tm = sub * mt
    row = lambda i: (i, 0)
    fixed = lambda i: (0, 0)
    return pl.pallas_call(
        _router_body,
        grid=(t // tm,),
        in_specs=[
            pl.BlockSpec((tm, D_MODEL), row),
            pl.BlockSpec((1, D_MODEL), fixed),
            pl.BlockSpec((D_MODEL, LANES), fixed),
            pl.BlockSpec((1, LANES), fixed),
        ],
        out_specs=[
            pl.BlockSpec((tm, D_MODEL), row),
            pl.BlockSpec((tm, LANES), row),
            pl.BlockSpec((tm, LANES), row),
            pl.BlockSpec((tm, LANES), row),
            pl.BlockSpec((sub, 8, LANES), lambda i: (i, 0, 0)),
        ],
        out_shape=[
            jax.ShapeDtypeStruct((t, D_MODEL), BF16),
            jax.ShapeDtypeStruct((t, LANES), I32),
            jax.ShapeDtypeStruct((t, LANES), F32),
            jax.ShapeDtypeStruct((t, LANES), I32),
            jax.ShapeDtypeStruct((t // mt, 8, LANES), F32),
        ],
        compiler_params=pltpu.CompilerParams(dimension_semantics=("arbitrary",)),
        name="router",
    )(x1, gain, w_r, b_r)


def _lane_cumsum(x):
    lane = lax.broadcasted_iota(I32, x.shape, 1)
    s = 1
    while s < N_EXPERTS:
        x = x + jnp.where(lane >= s, pltpu.roll(x, s, axis=1), 0.0)
        s *= 2
    return x


def _plan_body(cnt_ref, gs_ref, ls_ref, n8_ref, lsf_ref, tail_ref, meta_ref):
    nt = cnt_ref.shape[0]
    ga = float(GROUP_ALIGN)
    eb = float(EXPERT_BLOCK)
    lane = lax.broadcasted_iota(I32, (nt, LANES), 1)
    r8 = jnp.where(lane < N_EXPERTS, jnp.floor((cnt_ref[...] + (ga - 1.0)) * (1.0 / ga)) * ga, 0.0)
    ri = lax.broadcasted_iota(I32, (nt, nt), 0)
    ci = lax.broadcasted_iota(I32, (nt, nt), 1)
    before = _dot(jnp.where(ri > ci, 1.0, 0.0).astype(BF16), r8.astype(BF16))
    tot = jnp.sum(r8, axis=0, keepdims=True)
    region = jnp.floor((tot + (eb - 1.0)) * (1.0 / eb)) * eb
    pends = _lane_cumsum(jnp.broadcast_to(region, (8, LANES)))[0:1, :]
    pstart = pends - region
    lstart = _lane_cumsum(r8) - r8
    gs_ref[...] = (pstart + before).astype(I32)
    ls_ref[...] = lstart.astype(I32)
    pieces = r8 * (1.0 / ga)
    n8_ref[...] = jnp.where(lane == TOTAL_LANE, jnp.sum(pieces, axis=1, keepdims=True), pieces).astype(I32)
    lsf_ref[...] = lstart
    row8 = lax.broadcasted_iota(I32, (8, LANES), 0)
    tail_n = (region - tot) * (1.0 / ga)
    lane8 = lax.broadcasted_iota(I32, (8, LANES), 1)
    tail_n = jnp.where(lane8 == TOTAL_LANE, jnp.sum(tail_n, axis=1, keepdims=True), tail_n)
    tail = jnp.where(row8 == 0, pstart + tot, jnp.where(row8 == 1, tail_n, 0.0))
    tail_ref[...] = tail.astype(I32)
    nb = meta_ref.shape[0]
    ln = lax.broadcasted_iota(I32, (nb, LANES), 1)
    blk = lax.broadcasted_iota(I32, (nb, LANES), 0).astype(F32) * eb
    be = jnp.sum(jnp.where((ln < N_EXPERTS) & (pends <= blk), 1.0, 0.0), axis=-1, keepdims=True)
    be = jnp.minimum(be, float(N_EXPERTS - 1))
    used = jnp.sum(jnp.where(ln == N_EXPERTS - 1, pends, 0.0), axis=-1, keepdims=True) * (1.0 / eb)
    meta_ref[...] = jnp.where(ln == 0, be, jnp.where(ln == 1, used, 0.0)).astype(I32)


def _plan(cnt, n_blocks_pad):
    nt = cnt.shape[0]
    shp = jax.ShapeDtypeStruct((nt, LANES), I32)
    return pl.pallas_call(
        _plan_body,
        out_shape=[shp, shp, shp, jax.ShapeDtypeStruct((nt, LANES), F32),
                   jax.ShapeDtypeStruct((8, LANES), I32),
                   jax.ShapeDtypeStruct((n_blocks_pad, LANES), I32)],
        name="dispatch_plan",
    )(cnt)


def _local_positions(idx_ref, rank_ref, lsf_ref):
    tm = idx_ref.shape[0]
    lane = lax.broadcasted_iota(I32, (tm, LANES), 1)
    idx = idx_ref[...]
    rank = rank_ref[...].astype(F32)
    ls_row = lsf_ref[0]
    out = []
    for k in range(TOP_K):
        base = jnp.sum(jnp.where(lane == idx[:, k:k + 1], ls_row, 0.0), axis=-1, keepdims=True)
        out.append((base + rank[:, k:k + 1]).astype(I32))
    return out


def _local_positions_lanes(idx, rank, ls_row):
    tm = idx.shape[0]
    idx_t = idx.astype(F32).T
    rank_t = rank.astype(F32).T
    ls_col = jnp.broadcast_to(ls_row, (LANES, LANES)).T[:, 0:1]
    expert = lax.broadcasted_iota(I32, (LANES, tm), 0).astype(F32)
    out = []
    for k in range(TOP_K):
        base = jnp.sum(jnp.where(expert == idx_t[k:k + 1, :], ls_col, 0.0), axis=0, keepdims=True)
        out.append((base + rank_t[k:k + 1, :]).astype(I32))
    return out


PACK_W = D_MODEL // 2
U32 = jnp.uint32


def _pack_rows(x):
    xb = x.astype(BF16).astype(F32)
    hi = lax.bitcast_convert_type(xb[:, :PACK_W], U32)
    lo = lax.bitcast_convert_type(xb[:, PACK_W:], U32)
    return hi | (lo >> 16)


def _unpack_rows(w):
    hi = lax.bitcast_convert_type(w & jnp.uint32(0xFFFF0000), F32)
    lo = lax.bitcast_convert_type(w << 16, F32)
    return hi.astype(BF16), lo.astype(BF16)


TOTAL_LANE = LANES - 1
WAIT_CHUNK = 128
BIG_PIECE = 4


def _start_group_copies(tables, tile, make_copy):
    gs_ref, ls_ref, n8_ref = tables

    def group(e, carry):
        g0 = gs_ref[tile, e]
        l0 = ls_ref[tile, e]
        n = n8_ref[tile, e]
        nbig = n // BIG_PIECE

        def big(j, c):
            off = j * (BIG_PIECE * GROUP_ALIGN)
            make_copy(pl.multiple_of(l0 + off, GROUP_ALIGN), pl.multiple_of(g0 + off, GROUP_ALIGN),
                      BIG_PIECE * GROUP_ALIGN).start()
            return c

        def one(j, c):
            off = j * GROUP_ALIGN
            make_copy(pl.multiple_of(l0 + off, GROUP_ALIGN), pl.multiple_of(g0 + off, GROUP_ALIGN),
                      GROUP_ALIGN).start()
            return c

        lax.fori_loop(0, nbig, big, 0)
        lax.fori_loop(nbig * BIG_PIECE, n, one, 0)
        return carry

    lax.fori_loop(0, N_EXPERTS, group, 0)


def _wait_pieces(n, make_wait):
    def chunk(j, c):
        make_wait(WAIT_CHUNK).wait()
        return c

    lax.fori_loop(0, n // WAIT_CHUNK, chunk, 0)
    b = WAIT_CHUNK // 2
    while b >= 1:
        def _(b=b):
            make_wait(b).wait()
        pl.when((n & b) != 0)(_)
        b //= 2


def _dispatch_body(gs_ref, ls_ref, n8_ref, tail_ref, h_ref, idx_ref, rank_ref, lsf_ref, xs_ref,
                   xl, zbuf, sem):
    i = pl.program_id(0)
    last = i == pl.num_programs(0) - 1
    sub = xl.shape[0]
    tables = (gs_ref, ls_ref, n8_ref)
    tm = h_ref.shape[0] // sub
    lrows = xl.shape[1]
    rows = [slice(s * tm, (s + 1) * tm) for s in range(sub)]
    pos = lax.broadcasted_iota(I32, (lrows, tm), 0)
    lps = [_local_positions_lanes(idx_ref[r, :], rank_ref[r, :], lsf_ref[s]) for s, r in enumerate(rows)]
    hits = []
    for lp in lps:
        hit = pos == lp[0]
        for k in range(1, TOP_K):
            hit = hit | (pos == lp[k])
        hits.append(jnp.where(hit, 1.0, 0.0).astype(BF16))
    sorted_rows = [_dot(hits[s], h_ref[rows[s], :]) for s in range(sub)]
    packed = [_pack_rows(x) for x in sorted_rows]

    def copy_from(s):
        def make_copy(l, g, n):
            return pltpu.make_async_copy(xl.at[s, pl.ds(l, n), :], xs_ref.at[pl.ds(g, n), :], sem.at[s])
        return make_copy

    def wait_on(s):
        def make_wait(pieces):
            n = pieces * GROUP_ALIGN
            return pltpu.make_async_copy(xl.at[0, pl.ds(0, n), :], xs_ref.at[pl.ds(0, n), :], sem.at[s])
        return make_wait

    @pl.when(i >= 1)
    def _():
        for s in range(sub):
            _wait_pieces(n8_ref[(i - 1) * sub + s, TOTAL_LANE], wait_on(s))

    for s in range(sub):
        xl[s] = packed[s]
    for s in range(sub):
        _start_group_copies(tables, i * sub + s, copy_from(s))

    def start_tails():
        def per_expert(e, carry):
            g0 = tail_ref[0, e]

            def one(j, c):
                pltpu.make_async_copy(
                    zbuf, xs_ref.at[pl.ds(pl.multiple_of(g0 + j * GROUP_ALIGN, GROUP_ALIGN), GROUP_ALIGN), :],
                    sem.at[sub]).start()
                return c

            lax.fori_loop(0, tail_ref[1, e], one, 0)
            return carry

        lax.fori_loop(0, N_EXPERTS, per_expert, 0)

    @pl.when(last)
    def _():
        zbuf[...] = jnp.zeros_like(zbuf)
        start_tails()
        for s in range(sub):
            _wait_pieces(n8_ref[i * sub + s, TOTAL_LANE], wait_on(s))
        _wait_pieces(tail_ref[1, TOTAL_LANE], wait_on(sub))


def _dispatch(gs, ls, n8, tail, h2, idx, rank, lsf, n_slots):
    t = h2.shape[0]
    tm = min(MOE_TILE, t)
    nt = t // tm
    sub = 2 if nt % 2 == 0 else 1
    lrows = tm * TOP_K + N_EXPERTS * GROUP_ALIGN
    row = lambda i, *_: (i, 0)
    grid_spec = pltpu.PrefetchScalarGridSpec(
        num_scalar_prefetch=4,
        grid=(nt // sub,),
        in_specs=[
            pl.BlockSpec((sub * tm, D_MODEL), row),
            pl.BlockSpec((sub * tm, LANES), row),
            pl.BlockSpec((sub * tm, LANES), row),
            pl.BlockSpec((sub, 1, LANES), lambda i, *_: (i, 0, 0)),
        ],
        out_specs=pl.BlockSpec(memory_space=pl.ANY),
        scratch_shapes=[pltpu.VMEM((sub, lrows, PACK_W), U32), pltpu.VMEM((GROUP_ALIGN, PACK_W), U32),
                        pltpu.SemaphoreType.DMA((sub + 1,))],
    )
    return pl.pallas_call(
        _dispatch_body,
        grid_spec=grid_spec,
        out_shape=jax.ShapeDtypeStruct((n_slots, PACK_W), U32),
        compiler_params=pltpu.CompilerParams(dimension_semantics=("arbitrary",)),
        name="moe_dispatch",
    )(gs, ls, n8, tail, h2, idx, rank, lsf)


def _expert_body(be_ref, nu_ref, x_ref, wgu_ref, bgu_ref, wd_ref, bd_ref, y_ref, wgu_b, wd_b):
    j = pl.program_id(0)

    @pl.when(j < nu_ref[0])
    def _():
        @pl.when(jnp.logical_or(j == 0, be_ref[j] != be_ref[jnp.maximum(j - 1, 0)]))
        def _():
            wgu_b[...] = wgu_ref[0].astype(BF16)
            wd_b[...] = wd_ref[0].astype(BF16)

        gu = _dot(jnp.concatenate(_unpack_rows(x_ref[...]), axis=1), wgu_b[...]) + bgu_ref[0]
        gate = jnp.minimum(gu[:, :D_FF], SWIGLU_LIMIT)
        up = jnp.clip(gu[:, D_FF:], -SWIGLU_LIMIT, SWIGLU_LIMIT)
        act = (up + 1.0) * gate * _sigmoid(SWIGLU_ALPHA * gate)
        y_ref[...] = _pack_rows(_dot(act.astype(BF16), wd_b[...]) + bd_ref[0])


def _experts(block_e, n_used, xs, wgu, bgu, wd, bd):
    n_slots = xs.shape[0]
    nb = n_slots // EXPERT_BLOCK

    def blk(j, be, nu):
        return jnp.minimum(j, nu[0] - 1)

    grid_spec = pltpu.PrefetchScalarGridSpec(
        num_scalar_prefetch=2,
        grid=(nb,),
        in_specs=[
            pl.BlockSpec((EXPERT_BLOCK, PACK_W), lambda j, be, nu: (blk(j, be, nu), 0)),
            pl.BlockSpec((1, D_MODEL, 2 * D_FF), lambda j, be, nu: (be[blk(j, be, nu)], 0, 0)),
            pl.BlockSpec((1, 1, 2 * D_FF), lambda j, be, nu: (be[blk(j, be, nu)], 0, 0)),
            pl.BlockSpec((1, D_FF, D_MODEL), lambda j, be, nu: (be[blk(j, be, nu)], 0, 0)),
            pl.BlockSpec((1, 1, D_MODEL), lambda j, be, nu: (be[blk(j, be, nu)], 0, 0)),
        ],
        out_specs=pl.BlockSpec((EXPERT_BLOCK, PACK_W), lambda j, be, nu: (blk(j, be, nu), 0)),
        scratch_shapes=[pltpu.VMEM((D_MODEL, 2 * D_FF), BF16), pltpu.VMEM((D_FF, D_MODEL), BF16)],
    )
    return pl.pallas_call(
        _expert_body,
        grid_spec=grid_spec,
        out_shape=jax.ShapeDtypeStruct((n_slots, PACK_W), U32),
        compiler_params=pltpu.CompilerParams(dimension_semantics=("arbitrary",)),
        name="moe_experts",
    )(block_e, n_used, xs, wgu, bgu, wd, bd)


def _combine_body(gs_ref, ls_ref, n8_ref, x_ref, wt_ref, idx_ref, rank_ref, lsf_ref, y_ref, out_ref,
                  yl, sem):
    i = pl.program_id(0)
    slot = i % 2
    tables = (gs_ref, ls_ref, n8_ref)
    tm = x_ref.shape[0]
    lrows = yl.shape[1]

    def copy_into(s):
        def make_copy(l, g, rows):
            return pltpu.make_async_copy(y_ref.at[pl.ds(g, rows), :], yl.at[s, pl.ds(l, rows), :], sem.at[s])
        return make_copy

    def make_wait(pieces):
        rows = pieces * GROUP_ALIGN
        return pltpu.make_async_copy(y_ref.at[pl.ds(0, rows), :], yl.at[0, pl.ds(0, rows), :], sem.at[slot])

    @pl.when(i == 0)
    def _():
        yl[...] = jnp.zeros_like(yl)
        _start_group_copies(tables, i, copy_into(slot))

    @pl.when(i + 1 < pl.num_programs(0))
    def _():
        _start_group_copies(tables, i + 1, copy_into(1 - slot))

    pos = lax.broadcasted_iota(I32, (tm, lrows), 1)
    lp = _local_positions(idx_ref, rank_ref, lsf_ref)
    wt = wt_ref[...]
    wm = jnp.zeros((tm, lrows), F32)
    for k in range(TOP_K):
        wm = jnp.where(pos == lp[k], wt[:, k:k + 1], wm)
    _wait_pieces(n8_ref[i, TOTAL_LANE], make_wait)
    wmb = wm.astype(BF16)
    y_hi, y_lo = _unpack_rows(yl[slot])
    out_ref[:, :PACK_W] = x_ref[:, :PACK_W] + _dot(wmb, y_hi)
    out_ref[:, PACK_W:] = x_ref[:, PACK_W:] + _dot(wmb, y_lo)


def _combine(gs, ls, n8, x1, wts, idx, rank, lsf, y):
    t = x1.shape[0]
    tm = min(MOE_TILE, t)
    nt = t // tm
    lrows = tm * TOP_K + N_EXPERTS * GROUP_ALIGN
    row = lambda i, *_: (i, 0)
    grid_spec = pltpu.PrefetchScalarGridSpec(
        num_scalar_prefetch=3,
        grid=(nt,),
        in_specs=[
            pl.BlockSpec((tm, D_MODEL), row),
            pl.BlockSpec((tm, LANES), row),
            pl.BlockSpec((tm, LANES), row),
            pl.BlockSpec((tm, LANES), row),
            pl.BlockSpec((1, 1, LANES), lambda i, *_: (i, 0, 0)),
            pl.BlockSpec(memory_space=pl.ANY),
        ],
        out_specs=pl.BlockSpec((tm, D_MODEL), row),
        scratch_shapes=[pltpu.VMEM((2, lrows, PACK_W), U32), pltpu.SemaphoreType.DMA((2,))],
    )
    return pl.pallas_call(
        _combine_body,
        grid_spec=grid_spec,
        out_shape=jax.ShapeDtypeStruct((t, D_MODEL), F32),
        compiler_params=pltpu.CompilerParams(dimension_semantics=("arbitrary",)),
        name="moe_combine",
    )(gs, ls, n8, x1, wts, idx, rank, lsf, y)


def _pad_lanes(v, fill=0.0):
    v = v.astype(F32).reshape(1, -1)
    return jnp.pad(v, ((0, 0), (0, LANES - v.shape[1])), constant_values=fill)


def _mixer(x2, mem, rel_table, attn_norm, w_in, b_gate, dn_conv, dn_a_log, dn_dt_bias, dn_out_norm,
           da_q_norm, da_k_norm, da_lambda, da_subln, mem_norm, w_mem_kv, mx_q_norm, mx_k_norm,
           w_branch, w_out, batch, seq):
    wp = jnp.concatenate([w_in[:, :W_AB_LO], w_in[:, W_AB_HI:]], axis=1).astype(BF16)
    wab = jnp.pad(w_in[:, W_AB_LO:W_AB_HI], ((0, 0), (0, LANES - (W_AB_HI - W_AB_LO))))
    p, ab = _inproj(x2, attn_norm.reshape(1, -1), wp, wab)

    o_dn = _deltanet(p, ab, dn_conv, _pad_lanes(dn_a_log), _pad_lanes(dn_dt_bias),
                     dn_out_norm.reshape(1, -1), batch, seq)

    tq = min(ATT_BLOCK, seq)
    bias = _bias_tiles(rel_table.T, tq)
    o_da = _attention(p, bias, jnp.tile(da_q_norm, 2).reshape(1, -1), jnp.tile(da_k_norm, 2).reshape(1, -1),
                      da_lambda, da_subln.reshape(-1, 1), batch, seq)

    mk, mv = _memkv(mem, mem_norm.reshape(1, -1), w_mem_kv.astype(BF16), mx_k_norm.reshape(1, -1))
    return _merge(x2, o_dn, o_da, p, mk, mv, mx_q_norm.reshape(1, -1), b_gate.reshape(3, D_MODEL),
                  w_branch.astype(BF16), w_out.astype(BF16), seq)


def _moe(x1, ffn_norm, w_router, b_router, w_gate_up, b_gate_up, w_down, b_down):
    t = x1.shape[0]
    nt = t // min(MOE_TILE, t)
    max_rows = t * TOP_K + nt * N_EXPERTS * (GROUP_ALIGN - 1)
    n_blocks = -(-max_rows // EXPERT_BLOCK) + N_EXPERTS
    n_blocks_pad = -(-n_blocks // 8) * 8
    n_slots = n_blocks * EXPERT_BLOCK

    wr = jnp.pad(w_router, ((0, 0), (0, LANES - N_EXPERTS)))
    h2, idx, wts, rank, cnt = _router(x1, ffn_norm.reshape(1, -1), wr, _pad_lanes(b_router, NEG))
    gs, ls, n8, lsf, tail, meta = _plan(cnt[:, 0, :], n_blocks_pad)
    block_e = meta[:n_blocks, 0]
    n_used = meta[0:1, 1]
    lsf = lsf.reshape(nt, 1, LANES)

    xs = _dispatch(gs, ls, n8, tail, h2, idx, rank, lsf, n_slots)
    y = _experts(block_e, n_used, xs, w_gate_up, b_gate_up.reshape(N_EXPERTS, 1, -1),
                 w_down, b_down.reshape(N_EXPERTS, 1, -1))
    return _combine(gs, ls, n8, x1, wts, idx, rank, lsf, y)


def kernel(x, mem, rel_table, attn_norm, w_in, b_gate, dn_conv, dn_a_log, dn_dt_bias, dn_out_norm,
           da_q_norm, da_k_norm, da_lambda, da_subln, mem_norm, w_mem_kv, mx_q_norm, mx_k_norm,
           w_branch, w_out, ffn_norm, w_router, b_router, w_gate_up, b_gate_up, w_down, b_down):
    batch, seq, d = x.shape
    x2 = x.reshape(batch * seq, d)
    x1 = _mixer(x2, mem, rel_table, attn_norm[0], w_in[0], b_gate[0], dn_conv[0], dn_a_log[0],
                dn_dt_bias[0], dn_out_norm[0], da_q_norm[0], da_k_norm[0], da_lambda[0], da_subln[0],
                mem_norm[0], w_mem_kv[0], mx_q_norm[0], mx_k_norm[0], w_branch[0], w_out[0], batch, seq)
    out = _moe(x1, ffn_norm[0], w_router[0], b_router[0], w_gate_up[0], b_gate_up[0], w_down[0],
               b_down[0])
    return out.reshape(batch, seq, d)
```

```python
import math

import jax
import jax.numpy as jnp
from jax import lax
from jax.experimental import pallas as pl
from jax.experimental.pallas import tpu as pltpu

F32 = jnp.float32
BF16 = jnp.bfloat16
I32 = jnp.int32

D_MODEL = 1024
EPS = 1e-6
LANES = 128

DN_HEADS = 4
DN_DK = 128
DN_CHUNK = 64
DN_CONV = 4

DA_HEADS = 4
DA_DH = 64

MX_HEADS = 4
MX_DH = 128

REL_BUCKETS = 32
REL_MAX_DIST = 128

N_EXPERTS = 32
TOP_K = 4
D_FF = 1024
SWIGLU_LIMIT = 7.0
SWIGLU_ALPHA = 1.702
EXPERT_BLOCK = 512
MOE_TILE = 256
GROUP_ALIGN = 8

LAM_INIT = 0.8 - 0.6 * math.exp(-0.3 * 0)
LOG2E = 1.4426950408889634
NEG = -1e30

P_DNQ, P_DNK, P_DNV, P_DNZ = 0, 512, 1024, 1536
P_DAQ, P_DAK, P_DAV = 2048, 2560, 3072
P_MXQ = 3584
P_GATE = 4096
P_COLS = 7168
W_AB_LO, W_AB_HI = 2048, 2056


def _dot(a, b):
    return jnp.dot(a, b, preferred_element_type=F32)


def _dot_nt(a, b):
    return lax.dot_general(a, b, (((1,), (1,)), ((), ())), preferred_element_type=F32)


def _dot_tn(a, b):
    return lax.dot_general(a, b, (((0,), (0,)), ((), ())), preferred_element_type=F32)


def _split(x):
    hi = x.astype(BF16)
    lo = (x - hi.astype(F32)).astype(BF16)
    return hi, lo


def _dot3(a, b):
    ah, al = _split(a)
    bh, bl = _split(b)
    return _dot(ah, bh) + _dot(ah, bl) + _dot(al, bh)


def _sigmoid(x):
    return 1.0 / (1.0 + jnp.exp(-x))


def _rms(x, n):
    return lax.rsqrt(jnp.sum(x * x, axis=-1, keepdims=True) * (1.0 / n) + EPS)


def _inproj_body(x_ref, g_ref, w_ref, wab_ref, p_ref, ab_ref, h_scr):
    @pl.when(pl.program_id(1) == 0)
    def _():
        x = x_ref[...]
        h = x * _rms(x, D_MODEL) * g_ref[...]
        h_scr[...] = h.astype(BF16)
        ab_ref[...] = _dot3(h, wab_ref[...])

    p_ref[...] = _dot(h_scr[...], w_ref[...]).astype(p_ref.dtype)


def _inproj(x2, gain, wp, wab):
    t = x2.shape[0]
    tm = min(2048, t)
    tn = 1024
    return pl.pallas_call(
        _inproj_body,
        grid=(t // tm, P_COLS // tn),
        in_specs=[
            pl.BlockSpec((tm, D_MODEL), lambda i, j: (i, 0)),
            pl.BlockSpec((1, D_MODEL), lambda i, j: (0, 0)),
            pl.BlockSpec((D_MODEL, tn), lambda i, j: (0, j)),
            pl.BlockSpec((D_MODEL, LANES), lambda i, j: (0, 0)),
        ],
        out_specs=[
            pl.BlockSpec((tm, tn), lambda i, j: (i, j)),
            pl.BlockSpec((tm, LANES), lambda i, j: (i, 0)),
        ],
        out_shape=[
            jax.ShapeDtypeStruct((t, P_COLS), BF16),
            jax.ShapeDtypeStruct((t, LANES), F32),
        ],
        scratch_shapes=[pltpu.VMEM((tm, D_MODEL), BF16)],
        compiler_params=pltpu.CompilerParams(dimension_semantics=("parallel", "arbitrary")),
        name="inproj",
    )(x2, gain, wp, wab)


DN_HALO = 16
DN_SCAN_CHUNK = 256


def _deltanet_body(q_ref, k_ref, v_ref, z_ref, qh_ref, kh_ref, vh_ref, ab_ref, cw_ref, alog_ref,
                   dtb_ref, on_ref, o_ref, stage, qs, ks, vs, s_scr):
    i = pl.program_id(1)
    tc = q_ref.shape[0]
    hw = DN_HEADS * DN_DK

    @pl.when(i == 0)
    def _():
        s_scr[...] = jnp.zeros_like(s_scr)

    for src, halo, dst, off, kind in ((q_ref, qh_ref, qs, 0, "q"), (k_ref, kh_ref, ks, hw, "k"),
                                      (v_ref, vh_ref, vs, 2 * hw, "v")):
        hal = halo[...].astype(F32)
        stage[0:DN_HALO, :] = jnp.where(i == 0, 0.0, hal)
        stage[DN_HALO:DN_HALO + tc, :] = src[...].astype(F32)
        base = DN_HALO - (DN_CONV - 1)
        y = stage[base:base + tc, :] * cw_ref[0:1, off:off + hw]
        for j in range(1, DN_CONV):
            y = y + stage[base + j:base + j + tc, :] * cw_ref[j:j + 1, off:off + hw]
        y = y * _sigmoid(y)
        if kind == "v":
            dst[...] = y
        else:
            for h in range(DN_HEADS):
                sl = slice(h * DN_DK, (h + 1) * DN_DK)
                yh = y[:, sl]
                r = lax.rsqrt(jnp.sum(yh * yh, axis=-1, keepdims=True) + EPS)
                if kind == "q":
                    r = r * (DN_DK ** -0.5)
                dst[:, sl] = yh * r

    c = min(DN_SCAN_CHUNK, tc)
    row = lax.broadcasted_iota(I32, (c, c), 0)
    col = lax.broadcasted_iota(I32, (c, c), 1)
    incl = row >= col
    strict = row > col
    same_blk = (row // DN_CHUNK) == (col // DN_CHUNK)
    tri = jnp.where(incl, 1.0, 0.0).astype(BF16)
    eye = jnp.where(row == col, 1.0, 0.0)
    neg_a = -jnp.exp(alog_ref[...])
    dtb = dtb_ref[...]

    def chunk(ci, carry):
        r0 = pl.multiple_of(ci * c, c)
        abc = ab_ref[pl.ds(r0, c), :]
        a_in = abc + dtb
        g_all = neg_a * (jnp.maximum(a_in, 0.0) + jnp.log(1.0 + jnp.exp(-jnp.abs(a_in))))
        beta_all = _sigmoid(abc)
        zc = z_ref[pl.ds(r0, c), :].astype(F32)
        hs = range(DN_HEADS)
        sls = [slice(h * DN_DK, (h + 1) * DN_DK) for h in hs]
        q = [qs[pl.ds(r0, c), sl] for sl in sls]
        k = [ks[pl.ds(r0, c), sl] for sl in sls]
        v = [vs[pl.ds(r0, c), sl] for sl in sls]
        beta = [beta_all[:, DN_HEADS + h:DN_HEADS + h + 1] for h in hs]
        g_hi, g_lo = _split(g_all)
        gc_all = _dot(tri, g_hi) + _dot(tri, g_lo)
        gc_rows = gc_all.T
        kb = [k[h].astype(BF16) for h in hs]
        qkk = [_dot_nt(jnp.concatenate([q[h].astype(BF16), kb[h]], axis=0), kb[h]) for h in hs]
        gc = [gc_all[:, h:h + 1] for h in hs]
        decay = [jnp.where(incl, jnp.exp(gc[h] - gc_rows[h:h + 1, :]), 0.0) for h in hs]
        lower = [jnp.where(strict, qkk[h][c:] * decay[h] * beta[h], 0.0) for h in hs]
        pw = [jnp.where(same_blk, -lower[h], 0.0) for h in hs]
        dinv = [eye + pw[h] for h in hs]
        pwb = [pw[h].astype(BF16) for h in hs]
        for _ in range(int(math.log2(DN_CHUNK)) - 1):
            pwb = [_dot(pwb[h], pwb[h]).astype(BF16) for h in hs]
            dinv = [dinv[h] + _dot(dinv[h].astype(BF16), pwb[h]) for h in hs]
        dinv_b = [dinv[h].astype(BF16) for h in hs]
        pw = [-_dot(dinv_b[h], jnp.where(same_blk, 0.0, lower[h]).astype(BF16)) for h in hs]
        xm = [eye + pw[h] for h in hs]
        pwb = [pw[h].astype(BF16) for h in hs]
        for _ in range(int(math.log2(c // DN_CHUNK)) - 1):
            pwb = [_dot(pwb[h], pwb[h]).astype(BF16) for h in hs]
            xm = [xm[h] + _dot(xm[h].astype(BF16), pwb[h]) for h in hs]
        inv = [_dot(xm[h].astype(BF16), dinv_b[h]).astype(BF16) for h in hs]
        egc = [jnp.exp(gc[h]) for h in hs]
        rhs = [jnp.concatenate([v[h] * beta[h], k[h] * (beta[h] * egc[h])], axis=1).astype(BF16) for h in hs]
        sol = [_dot(inv[h], rhs[h]) for h in hs]
        qkm = [jnp.where(incl, qkk[h][:c] * decay[h], 0.0).astype(BF16) for h in hs]
        gl = [gc[h][c - 1:c, :] for h in hs]
        state = [s_scr[h] for h in hs]
        ws = [_dot(jnp.concatenate([sol[h][:, DN_DK:].astype(BF16), (q[h] * egc[h]).astype(BF16)], axis=0),
                   state[h].astype(BF16)) for h in hs]
        v_new = [sol[h][:, :DN_DK] - ws[h][:c] for h in hs]
        o = [ws[h][c:] + _dot(qkm[h], v_new[h].astype(BF16)) for h in hs]
        for h in hs:
            s_scr[h] = state[h] * jnp.exp(gl[h]) + _dot_tn(kb[h], (v_new[h] * jnp.exp(gl[h] - gc[h])).astype(BF16))
        for h in hs:
            zz = zc[:, sls[h]]
            on = o[h] * _rms(o[h], DN_DK) * on_ref[...]
            o_ref[pl.ds(r0, c), sls[h]] = (on * (zz * _sigmoid(zz))).astype(o_ref.dtype)
        return carry

    lax.fori_loop(0, tc // c, chunk, 0, unroll=True)


def _deltanet(p, ab, conv_w, alog_row, dtb_row, out_norm, batch, seq):
    t = batch * seq
    tc = min(1024, seq)
    nt = seq // tc
    hw = DN_HEADS * DN_DK

    def main(cb):
        return pl.BlockSpec((tc, hw), lambda b, i: (b * nt + i, cb))

    def halo(cb):
        return pl.BlockSpec(
            (DN_HALO, hw),
            lambda b, i: (jnp.maximum((b * seq + i * tc) // DN_HALO - 1, 0), cb))

    return pl.pallas_call(
        _deltanet_body,
        grid=(batch, nt),
        in_specs=[
            main(P_DNQ // hw), main(P_DNK // hw), main(P_DNV // hw), main(P_DNZ // hw),
            halo(P_DNQ // hw), halo(P_DNK // hw), halo(P_DNV // hw),
            pl.BlockSpec((tc, LANES), lambda b, i: (b * nt + i, 0)),
            pl.BlockSpec((DN_CONV, 3 * hw), lambda b, i: (0, 0)),
            pl.BlockSpec((1, LANES), lambda b, i: (0, 0)),
            pl.BlockSpec((1, LANES), lambda b, i: (0, 0)),
            pl.BlockSpec((1, DN_DK), lambda b, i: (0, 0)),
        ],
        out_specs=pl.BlockSpec((tc, hw), lambda b, i: (b * nt + i, 0)),
        out_shape=jax.ShapeDtypeStruct((t, hw), BF16),
        scratch_shapes=[
            pltpu.VMEM((DN_HALO + tc, hw), F32),
            pltpu.VMEM((tc, hw), F32),
            pltpu.VMEM((tc, hw), F32),
            pltpu.VMEM((tc, hw), F32),
            pltpu.VMEM((DN_HEADS, DN_DK, DN_DK), F32),
        ],
        compiler_params=pltpu.CompilerParams(dimension_semantics=("parallel", "arbitrary")),
        name="deltanet",
    )(p, p, p, p, p, p, p, ab, conv_w, alog_row, dtb_row, out_norm)


ATT_BLOCK = 512


def _bias_body(tbl_ref, o_ref):
    h = pl.program_id(0)
    tq = o_ref.shape[2]
    key = lax.broadcasted_iota(I32, (tq, tq), 0)
    qry = lax.broadcasted_iota(I32, (tq, tq), 1)
    max_exact = REL_BUCKETS // 2
    far = tbl_ref[h, REL_BUCKETS - 1]
    for d in range(2):
        n = qry - key + d * tq
        nn = jnp.maximum(n, 0)
        nf = jnp.maximum(nn, 1).astype(F32)
        large = max_exact + (jnp.log(nf / max_exact) / math.log(REL_MAX_DIST / max_exact)
                             * (REL_BUCKETS - max_exact)).astype(I32)
        large = jnp.minimum(large, REL_BUCKETS - 1)
        bucket = jnp.where(nn < max_exact, nn, large)
        val = jnp.zeros((tq, tq), F32)
        for b in range(REL_BUCKETS):
            val = jnp.where(bucket == b, tbl_ref[h, b], val)
        o_ref[0, d] = jnp.where(n >= 0, (val - far) * LOG2E, NEG)


def _bias_tiles(tbl_t, tq):
    return pl.pallas_call(
        _bias_body,
        grid=(DA_HEADS,),
        in_specs=[pl.BlockSpec(memory_space=pltpu.SMEM)],
        out_specs=pl.BlockSpec((1, 2, tq, tq), lambda h: (h, 0, 0, 0)),
        out_shape=jax.ShapeDtypeStruct((DA_HEADS, 2, tq, tq), F32),
        name="t5_bias_tiles",
    )(tbl_t)


DA_DV = 2 * DA_DH
DA_VROWS = DA_DV + 16


BOUND_SLACK = 1.02
MAX_SHIFT_GAP = 110.0


def _attn_body(q_ref, qn_ref, k_ref, v_ref, bias_ref, qg_ref, kg_ref, lam_ref, sg_ref, o_ref,
               kn, vt, kst, qc_s, bd_s, flag_s, m_s, acc_s):
    qi = pl.program_id(2)
    slot = qi % 2
    tq = q_ref.shape[0]
    seq = k_ref.shape[0]
    tk = tq
    lo_mask = lax.broadcasted_iota(I32, (1, DA_DV), 1) < DA_DH

    def group_norm(x, gain):
        x2 = x * x
        lo = jnp.sum(jnp.where(lo_mask, x2, 0.0), axis=-1, keepdims=True)
        hi = jnp.sum(jnp.where(lo_mask, 0.0, x2), axis=-1, keepdims=True)
        r = jnp.where(lo_mask, lax.rsqrt(lo * (1.0 / DA_DH) + EPS), lax.rsqrt(hi * (1.0 / DA_DH) + EPS))
        return x * r * gain

    @pl.when(qi == 0)
    def _():
        ones = jnp.ones((DA_VROWS - DA_DV, tk), BF16)

        def body(c, kmax2):
            r0 = pl.multiple_of(c * tk, tk)
            kb = group_norm(k_ref[pl.ds(r0, tk), :].astype(F32), kg_ref[...]).astype(BF16)
            kn[pl.ds(r0, tk), :] = kb
            vt[c, 0:DA_DV, :] = v_ref[pl.ds(r0, tk), :].astype(F32).T.astype(BF16)
            vt[c, DA_DV:DA_VROWS, :] = ones
            k2 = kb.astype(F32)
            k2 = k2 * k2
            lo = jnp.max(jnp.sum(jnp.where(lo_mask, k2, 0.0), axis=-1, keepdims=True), axis=0, keepdims=True)
            hi = jnp.max(jnp.sum(jnp.where(lo_mask, 0.0, k2), axis=-1, keepdims=True), axis=0, keepdims=True)
            return jnp.maximum(kmax2, jnp.where(lo_mask, lo, hi))
        kst[0:1, :] = lax.fori_loop(0, seq // tk, body, jnp.zeros((1, DA_DV), F32), unroll=2)
        b0 = bias_ref[0, 0]
        b1 = bias_ref[0, 1]
        bmax = jnp.maximum(jnp.max(jnp.maximum(b0, b1), axis=0, keepdims=True), 0.0)
        bmin = jnp.minimum(jnp.min(jnp.minimum(jnp.where(b0 > 0.5 * NEG, b0, 0.0), b1), axis=0, keepdims=True), 0.0)
        kst[1:2, :] = jnp.broadcast_to(jnp.max(bmax, axis=1, keepdims=True), (1, DA_DV))
        kst[2:3, :] = jnp.broadcast_to(jnp.min(bmin, axis=1, keepdims=True), (1, DA_DV))

    def prepare(src_ref, s):
        q = group_norm(src_ref[...].astype(F32), qg_ref[...]) * (DA_DH ** -0.5 * LOG2E)
        qc_s[s] = jnp.concatenate([jnp.where(lo_mask, q, 0.0), jnp.where(lo_mask, 0.0, q)], axis=0).astype(BF16)
        q2 = q * q * kst[0:1, :]
        ones8 = jnp.ones((8, DA_DV), BF16)
        bound = []
        for m in range(2):
            q2m = jnp.where(lo_mask, q2, 0.0) if m == 0 else jnp.where(lo_mask, 0.0, q2)
            bound.append(jnp.sqrt(_dot_nt(ones8, q2m.astype(BF16))[0:1, :]) * BOUND_SLACK)
        bound = jnp.concatenate(bound, axis=1)
        bd_s[s] = bound
        worst = jnp.max(2.0 * bound, axis=1, keepdims=True) + kst[1:2, 0:1] - kst[2:3, 0:1]
        flag_s[s] = jnp.where(worst[0, 0] <= MAX_SHIFT_GAP, 1, 0).astype(I32)

    @pl.when(qi == 0)
    def _():
        prepare(q_ref, slot)

    qcat = qc_s[slot]
    bound = bd_s[slot]
    bmax = kst[1:2, 0:1]
    safe = flag_s[slot] == 1
    acc_s[...] = jnp.zeros_like(acc_s)

    def block(j, d):
        r0 = pl.multiple_of(j * tk, tk)
        st = _dot_nt(kn[pl.ds(r0, tk), :], qcat)
        if d is not None:
            bias = bias_ref[0, d]
            st = st + jnp.concatenate([bias, bias], axis=1)
        m_prev = m_s[...]
        m_new = jnp.maximum(m_prev, jnp.max(st, axis=0, keepdims=True))
        alpha = jnp.exp2(m_prev - m_new)
        acc_s[...] = alpha * acc_s[...] + _dot(vt[j], jnp.exp2(st - m_new).astype(BF16))
        m_s[...] = m_new

    def blocks_fixed(js, ds=None):
        sts = [_dot_nt(kn[pl.ds(pl.multiple_of(j * tk, tk), tk), :], qcat) for j in js]
        if ds is not None:
            biases = [bias_ref[0, d] for d in ds]
            sts = [st + jnp.concatenate([b, b], axis=1) for st, b in zip(sts, biases)]
        shift = m_s[...]
        pts = [jnp.exp2(st - shift).astype(BF16) for st in sts]
        tot = _dot(vt[js[0]], pts[0])
        for j, pt in zip(js[1:], pts[1:]):
            tot = tot + _dot(vt[j], pt)
        acc_s[...] = acc_s[...] + tot

    def run(fixed_shift):
        n_far = jnp.maximum(qi - 1, 0)

        if fixed_shift:
            def far_quad(jj, carry):
                blocks_fixed([4 * jj + u for u in range(4)])
                return carry

            lax.fori_loop(0, n_far // 4, far_quad, 0)
            rem = n_far % 4

            @pl.when(rem >= 2)
            def _():
                blocks_fixed([n_far - rem, n_far - rem + 1])

            @pl.when(rem % 2 == 1)
            def _():
                blocks_fixed([n_far - 1])

            @pl.when(qi >= 1)
            def _():
                blocks_fixed([qi - 1, qi], [1, 0])
                prepare(qn_ref, 1 - slot)

            @pl.when(qi == 0)
            def _():
                blocks_fixed([qi], [0])
                prepare(qn_ref, 1 - slot)
        else:
            def far_one(j, carry):
                block(j, None)
                return carry

            lax.fori_loop(0, n_far, far_one, 0)

            @pl.when(qi >= 1)
            def _():
                block(qi - 1, 1)

            block(qi, 0)
            prepare(qn_ref, 1 - slot)

    @pl.when(safe)
    def _():
        m_s[...] = bound + bmax
        run(True)

    @pl.when(jnp.logical_not(safe))
    def _():
        m_s[...] = jnp.full(m_s.shape, NEG, F32)
        run(False)

    lam_p = lam_ref[...]
    lam = (jnp.exp(jnp.sum(lam_p[0:1, :] * lam_p[1:2, :], axis=-1, keepdims=True))
           - jnp.exp(jnp.sum(lam_p[2:3, :] * lam_p[3:4, :], axis=-1, keepdims=True)) + LAM_INIT)
    a0 = acc_s[:, 0:tq]
    a1 = acc_s[:, tq:2 * tq]
    ot = a0[0:DA_DV] / a0[DA_DV:DA_DV + 1] - lam * (a1[0:DA_DV] / a1[DA_DV:DA_DV + 1])
    r = lax.rsqrt(jnp.sum(ot * ot, axis=0, keepdims=True) * (1.0 / DA_DV) + EPS)
    ot = ot * r * (sg_ref[...] * (1.0 - LAM_INIT))
    o_ref[...] = ot.T.astype(o_ref.dtype)


def _attention(p, bias, qg, kg, lam_p, subln, batch, seq):
    t = batch * seq
    tq = min(ATT_BLOCK, seq)
    nq = seq // tq
    dv = DA_DV
    return pl.pallas_call(
        _attn_body,
        grid=(batch, DA_HEADS, nq),
        in_specs=[
            pl.BlockSpec((tq, dv), lambda b, h, i: (b * nq + i, P_DAQ // dv + h)),
            pl.BlockSpec((tq, dv), lambda b, h, i: (b * nq + jnp.minimum(i + 1, nq - 1), P_DAQ // dv + h)),
            pl.BlockSpec((seq, dv), lambda b, h, i: (b, P_DAK // dv + h)),
            pl.BlockSpec((seq, dv), lambda b, h, i: (b, P_DAV // dv + h)),
            pl.BlockSpec((1, 2, tq, tq), lambda b, h, i: (h, 0, 0, 0)),
            pl.BlockSpec((1, dv), lambda b, h, i: (0, 0)),
            pl.BlockSpec((1, dv), lambda b, h, i: (0, 0)),
            pl.BlockSpec((4, DA_DH), lambda b, h, i: (0, 0)),
            pl.BlockSpec((dv, 1), lambda b, h, i: (0, 0)),
        ],
        out_specs=pl.BlockSpec((tq, dv), lambda b, h, i: (b * nq + i, h)),
        out_shape=jax.ShapeDtypeStruct((t, DA_HEADS * dv), BF16),
        scratch_shapes=[
            pltpu.VMEM((seq, dv), BF16),
            pltpu.VMEM((seq // tq, DA_VROWS, tq), BF16),
            pltpu.VMEM((8, dv), F32),
            pltpu.VMEM((2, 2 * tq, dv), BF16),
            pltpu.VMEM((2, 1, 2 * tq), F32),
            pltpu.SMEM((2,), I32),
            pltpu.VMEM((1, 2 * tq), F32),
            pltpu.VMEM((DA_VROWS, 2 * tq), F32),
        ],
        compiler_params=pltpu.CompilerParams(dimension_semantics=("parallel", "parallel", "arbitrary")),
        name="diff_attention",
    )(p, p, p, p, bias, qg, kg, lam_p, subln)


def _memkv_body(mem_ref, mg_ref, w_ref, kg_ref, mk_ref, mv_ref):
    x = mem_ref[0]
    xn = x * _rms(x, D_MODEL) * mg_ref[...]
    kv = _dot(xn.astype(BF16), w_ref[...])
    hw = MX_HEADS * MX_DH
    for h in range(MX_HEADS):
        sl = slice(h * MX_DH, (h + 1) * MX_DH)
        kh = kv[:, sl]
        mk_ref[0, :, sl] = (kh * _rms(kh, MX_DH) * kg_ref[...]).astype(BF16)
    mv_ref[0] = kv[:, hw:].astype(BF16)


def _memkv(mem, mem_norm, w_kv, k_norm):
    b, n, _ = mem.shape
    hw = MX_HEADS * MX_DH
    return pl.pallas_call(
        _memkv_body,
        grid=(b,),
        in_specs=[
            pl.BlockSpec((1, n, D_MODEL), lambda i: (i, 0, 0)),
            pl.BlockSpec((1, D_MODEL), lambda i: (0, 0)),
            pl.BlockSpec((D_MODEL, 2 * hw), lambda i: (0, 0)),
            pl.BlockSpec((1, MX_DH), lambda i: (0, 0)),
        ],
        out_specs=[pl.BlockSpec((1, n, hw), lambda i: (i, 0, 0))] * 2,
        out_shape=[jax.ShapeDtypeStruct((b, n, hw), BF16)] * 2,
        name="memory_kv",
    )(mem, mem_norm, w_kv, k_norm)


def _merge_body(x_ref, odn_ref, oda_ref, mxq_ref, g0_ref, g1_ref, g2_ref, mk_ref, mv_ref, qg_ref,
                bg_ref, wb_ref, wo_ref, out_ref, omx):
    for h in range(MX_HEADS):
        sl = slice(h * MX_DH, (h + 1) * MX_DH)
        qh = mxq_ref[:, sl].astype(F32)
        qh = qh * _rms(qh, MX_DH) * qg_ref[...] * (MX_DH ** -0.5 * LOG2E)
        s = _dot_nt(qh.astype(BF16), mk_ref[0, :, sl])
        p = jnp.exp2(s - jnp.max(s, axis=-1, keepdims=True))
        oh = _dot(p.astype(BF16), mv_ref[0, :, sl]) / jnp.sum(p, axis=-1, keepdims=True)
        omx[:, sl] = oh.astype(BF16)
    y = None
    for r, (o_r, g_r) in enumerate(((odn_ref, g0_ref), (oda_ref, g1_ref), (omx, g2_ref))):
        gate = _sigmoid(g_r[...].astype(F32) + bg_ref[r:r + 1, :])
        term = gate * _dot(o_r[...], wb_ref[r])
        y = term if y is None else y + term
    out_ref[...] = x_ref[...] + _dot(y.astype(BF16), wo_ref[...])


def _merge(x2, o_dn, o_da, p, mk, mv, q_norm, b_gate, w_branch, w_out, seq):
    t = x2.shape[0]
    tm = min(1024, seq)
    nt = seq // tm
    bw = 512
    n_mem = mk.shape[1]
    return pl.pallas_call(
        _merge_body,
        grid=(t // tm,),
        in_specs=[
            pl.BlockSpec((tm, D_MODEL), lambda i: (i, 0)),
            pl.BlockSpec((tm, bw), lambda i: (i, 0)),
            pl.BlockSpec((tm, bw), lambda i: (i, 0)),
            pl.BlockSpec((tm, bw), lambda i: (i, P_MXQ // bw)),
            pl.BlockSpec((tm, D_MODEL), lambda i: (i, P_GATE // D_MODEL)),
            pl.BlockSpec((tm, D_MODEL), lambda i: (i, P_GATE // D_MODEL + 1)),
            pl.BlockSpec((tm, D_MODEL), lambda i: (i, P_GATE // D_MODEL + 2)),
            pl.BlockSpec((1, n_mem, bw), lambda i: (i // nt, 0, 0)),
            pl.BlockSpec((1, n_mem, bw), lambda i: (i // nt, 0, 0)),
            pl.BlockSpec((1, MX_DH), lambda i: (0, 0)),
            pl.BlockSpec((3, D_MODEL), lambda i: (0, 0)),
            pl.BlockSpec((3, bw, D_MODEL), lambda i: (0, 0, 0)),
            pl.BlockSpec((D_MODEL, D_MODEL), lambda i: (0, 0)),
        ],
        out_specs=pl.BlockSpec((tm, D_MODEL), lambda i: (i, 0)),
        out_shape=jax.ShapeDtypeStruct((t, D_MODEL), F32),
        scratch_shapes=[pltpu.VMEM((tm, bw), BF16)],
        compiler_params=pltpu.CompilerParams(dimension_semantics=("parallel",)),
        name="merge",
    )(x2, o_dn, o_da, p, p, p, p, mk, mv, q_norm, b_gate, w_branch, w_out)


def _router_body(x_ref, g_ref, wr_ref, br_ref, h_ref, idx_ref, wt_ref, rank_ref, cnt_ref):
    tm = x_ref.shape[0]
    x = x_ref[...]
    h = x * _rms(x, D_MODEL) * g_ref[...]
    h_ref[...] = h.astype(BF16)
    logits = _dot3(h, wr_ref[...]) + br_ref[...]
    lane = lax.broadcasted_iota(I32, (tm, LANES), 1)
    lane_f = lane.astype(F32)
    work = logits
    sel = jnp.zeros((tm, LANES), F32)
    vals, idxs = [], []
    for _ in range(TOP_K):
        mx = jnp.max(work, axis=-1, keepdims=True)
        ik = jnp.min(jnp.where(work == mx, lane_f, float(LANES)), axis=-1, keepdims=True)
        hit = lane_f == ik
        sel = jnp.where(hit, 1.0, sel)
        work = jnp.where(hit, -jnp.inf, work)
        vals.append(mx)
        idxs.append(ik)
    es = [jnp.exp(v - vals[0]) for v in vals]
    den = es[0] + es[1] + es[2] + es[3]
    sub = cnt_ref.shape[0]
    mt = tm // sub
    r = lax.broadcasted_iota(I32, (tm, tm), 0)
    c = lax.broadcasted_iota(I32, (tm, tm), 1)
    tril = jnp.where((r > c) & (r // mt == c // mt), 1.0, 0.0).astype(BF16)
    cum = _dot(tril, sel.astype(BF16))
    idx_o = jnp.zeros((tm, LANES), F32)
    wt_o = jnp.zeros((tm, LANES), F32)
    rank_o = jnp.zeros((tm, LANES), F32)
    for k in range(TOP_K):
        rk = jnp.sum(jnp.where(lane_f == idxs[k], cum, 0.0), axis=-1, keepdims=True)
        idx_o = jnp.where(lane == k, idxs[k], idx_o)
        wt_o = jnp.where(lane == k, es[k] / den, wt_o)
        rank_o = jnp.where(lane == k, rk, rank_o)
    idx_ref[...] = idx_o.astype(I32)
    wt_ref[...] = wt_o
    rank_ref[...] = rank_o.astype(I32)
    for s in range(sub):
        cnt_ref[s] = jnp.broadcast_to(jnp.sum(sel[s * mt:(s + 1) * mt], axis=0, keepdims=True), (8, LANES))


def _router(x1, gain, w_r, b_r):
    t = x1.shape[0]
    mt = min(MOE_TILE, t)
    sub = 2 if t % (2 * mt) == 0 else 1
    tm = sub * mt
    row = lambda i: (i, 0)
    fixed = lambda i: (0, 0)
    return pl.pallas_call(
        _router_body,
        grid=(t // tm,),
        in_specs=[
            pl.BlockSpec((tm, D_MODEL), row),
            pl.BlockSpec((1, D_MODEL), fixed),
            pl.BlockSpec((D_MODEL, LANES), fixed),
            pl.BlockSpec((1, LANES), fixed),
        ],
        out_specs=[
            pl.BlockSpec((tm, D_MODEL), row),
            pl.BlockSpec((tm, LANES), row),
            pl.BlockSpec((tm, LANES), row),
            pl.BlockSpec((tm, LANES), row),
            pl.BlockSpec((sub, 8, LANES), lambda i: (i, 0, 0)),
        ],
        out_shape=[
            jax.ShapeDtypeStruct((t, D_MODEL), BF16),
            jax.ShapeDtypeStruct((t, LANES), I32),
            jax.ShapeDtypeStruct((t, LANES), F32),
            jax.ShapeDtypeStruct((t, LANES), I32),
            jax.ShapeDtypeStruct((t // mt, 8, LANES), F32),
        ],
        compiler_params=pltpu.CompilerParams(dimension_semantics=("arbitrary",)),
        name="router",
    )(x1, gain, w_r, b_r)


def _lane_cumsum(x):
    lane = lax.broadcasted_iota(I32, x.shape, 1)
    s = 1
    while s < N_EXPERTS:
        x = x + jnp.where(lane >= s, pltpu.roll(x, s, axis=1), 0.0)
        s *= 2
    return x


def _plan_body(cnt_ref, gs_ref, ls_ref, n8_ref, lsf_ref, tail_ref, meta_ref):
    nt = cnt_ref.shape[0]
    ga = float(GROUP_ALIGN)
    eb = float(EXPERT_BLOCK)
    lane = lax.broadcasted_iota(I32, (nt, LANES), 1)
    r8 = jnp.where(lane < N_EXPERTS, jnp.floor((cnt_ref[...] + (ga - 1.0)) * (1.0 / ga)) * ga, 0.0)
    ri = lax.broadcasted_iota(I32, (nt, nt), 0)
    ci = lax.broadcasted_iota(I32, (nt, nt), 1)
    before = _dot(jnp.where(ri > ci, 1.0, 0.0).astype(BF16), r8.astype(BF16))
    tot = jnp.sum(r8, axis=0, keepdims=True)
    region = jnp.floor((tot + (eb - 1.0)) * (1.0 / eb)) * eb
    pends = _lane_cumsum(jnp.broadcast_to(region, (8, LANES)))[0:1, :]
    pstart = pends - region
    lstart = _lane_cumsum(r8) - r8
    gs_ref[...] = (pstart + before).astype(I32)
    ls_ref[...] = lstart.astype(I32)
    pieces = r8 * (1.0 / ga)
    n8_ref[...] = jnp.where(lane == TOTAL_LANE, jnp.sum(pieces, axis=1, keepdims=True), pieces).astype(I32)
    lsf_ref[...] = lstart
    row8 = lax.broadcasted_iota(I32, (8, LANES), 0)
    tail_n = (region - tot) * (1.0 / ga)
    lane8 = lax.broadcasted_iota(I32, (8, LANES), 1)
    tail_n = jnp.where(lane8 == TOTAL_LANE, jnp.sum(tail_n, axis=1, keepdims=True), tail_n)
    tail = jnp.where(row8 == 0, pstart + tot, jnp.where(row8 == 1, tail_n, 0.0))
    tail_ref[...] = tail.astype(I32)
    nb = meta_ref.shape[0]
    ln = lax.broadcasted_iota(I32, (nb, LANES), 1)
    blk = lax.broadcasted_iota(I32, (nb, LANES), 0).astype(F32) * eb
    be = jnp.sum(jnp.where((ln < N_EXPERTS) & (pends <= blk), 1.0, 0.0), axis=-1, keepdims=True)
    be = jnp.minimum(be, float(N_EXPERTS - 1))
    used = jnp.sum(jnp.where(ln == N_EXPERTS - 1, pends, 0.0), axis=-1, keepdims=True) * (1.0 / eb)
    meta_ref[...] = jnp.where(ln == 0, be, jnp.where(ln == 1, used, 0.0)).astype(I32)


def _plan(cnt, n_blocks_pad):
    nt = cnt.shape[0]
    shp = jax.ShapeDtypeStruct((nt, LANES), I32)
    return pl.pallas_call(
        _plan_body,
        out_shape=[shp, shp, shp, jax.ShapeDtypeStruct((nt, LANES), F32),
                   jax.ShapeDtypeStruct((8, LANES), I32),
                   jax.ShapeDtypeStruct((n_blocks_pad, LANES), I32)],
        name="dispatch_plan",
    )(cnt)


def _local_positions(idx_ref, rank_ref, lsf_ref):
    tm = idx_ref.shape[0]
    lane = lax.broadcasted_iota(I32, (tm, LANES), 1)
    idx = idx_ref[...]
    rank = rank_ref[...].astype(F32)
    ls_row = lsf_ref[0]
    out = []
    for k in range(TOP_K):
        base = jnp.sum(jnp.where(lane == idx[:, k:k + 1], ls_row, 0.0), axis=-1, keepdims=True)
        out.append((base + rank[:, k:k + 1]).astype(I32))
    return out


def _local_positions_lanes(idx_ref, rank_ref, lsf_ref):
    tm = idx_ref.shape[0]
    idx_t = idx_ref[...].astype(F32).T
    rank_t = rank_ref[...].astype(F32).T
    ls_col = jnp.broadcast_to(lsf_ref[0], (LANES, LANES)).T[:, 0:1]
    expert = lax.broadcasted_iota(I32, (LANES, tm), 0).astype(F32)
    out = []
    for k in range(TOP_K):
        base = jnp.sum(jnp.where(expert == idx_t[k:k + 1, :], ls_col, 0.0), axis=0, keepdims=True)
        out.append((base + rank_t[k:k + 1, :]).astype(I32))
    return out


PACK_W = D_MODEL // 2
U32 = jnp.uint32


def _pack_rows(x):
    xb = x.astype(BF16).astype(F32)
    hi = lax.bitcast_convert_type(xb[:, :PACK_W], U32)
    lo = lax.bitcast_convert_type(xb[:, PACK_W:], U32)
    return hi | (lo >> 16)


def _unpack_rows(w):
    hi = lax.bitcast_convert_type(w & jnp.uint32(0xFFFF0000), F32)
    lo = lax.bitcast_convert_type(w << 16, F32)
    return hi.astype(BF16), lo.astype(BF16)


TOTAL_LANE = LANES - 1
WAIT_CHUNK = 128
BIG_PIECE = 4


def _start_group_copies(tables, tile, make_copy):
    gs_ref, ls_ref, n8_ref = tables

    def group(e, carry):
        g0 = gs_ref[tile, e]
        l0 = ls_ref[tile, e]
        n = n8_ref[tile, e]
        nbig = n // BIG_PIECE

        def big(j, c):
            off = j * (BIG_PIECE * GROUP_ALIGN)
            make_copy(pl.multiple_of(l0 + off, GROUP_ALIGN), pl.multiple_of(g0 + off, GROUP_ALIGN),
                      BIG_PIECE * GROUP_ALIGN).start()
            return c

        def one(j, c):
            off = j * GROUP_ALIGN
            make_copy(pl.multiple_of(l0 + off, GROUP_ALIGN), pl.multiple_of(g0 + off, GROUP_ALIGN),
                      GROUP_ALIGN).start()
            return c

        lax.fori_loop(0, nbig, big, 0)
        lax.fori_loop(nbig * BIG_PIECE, n, one, 0)
        return carry

    lax.fori_loop(0, N_EXPERTS, group, 0)


def _wait_pieces(n, make_wait):
    def chunk(j, c):
        make_wait(WAIT_CHUNK).wait()
        return c

    lax.fori_loop(0, n // WAIT_CHUNK, chunk, 0)
    b = WAIT_CHUNK // 2
    while b >= 1:
        def _(b=b):
            make_wait(b).wait()
        pl.when((n & b) != 0)(_)
        b //= 2


def _dispatch_body(gs_ref, ls_ref, n8_ref, tail_ref, h_ref, idx_ref, rank_ref, lsf_ref, xs_ref,
                   xl, zbuf, sem):
    i = pl.program_id(0)
    last = i == pl.num_programs(0) - 1
    slot = i % 2
    tables = (gs_ref, ls_ref, n8_ref)
    tm = h_ref.shape[0]
    lrows = xl.shape[1]
    pos = lax.broadcasted_iota(I32, (lrows, tm), 0)
    lp = _local_positions_lanes(idx_ref, rank_ref, lsf_ref)
    hit = pos == lp[0]
    for k in range(1, TOP_K):
        hit = hit | (pos == lp[k])
    xl[slot] = _pack_rows(_dot(jnp.where(hit, 1.0, 0.0).astype(BF16), h_ref[...]))

    def copy_from(s):
        def make_copy(l, g, rows):
            return pltpu.make_async_copy(xl.at[s, pl.ds(l, rows), :], xs_ref.at[pl.ds(g, rows), :], sem.at[s])
        return make_copy

    def wait_on(s):
        def make_wait(pieces):
            rows = pieces * GROUP_ALIGN
            return pltpu.make_async_copy(xl.at[0, pl.ds(0, rows), :], xs_ref.at[pl.ds(0, rows), :], sem.at[s])
        return make_wait

    @pl.when(i >= 1)
    def _():
        _wait_pieces(n8_ref[i - 1, TOTAL_LANE], wait_on(1 - slot))

    _start_group_copies(tables, i, copy_from(slot))

    def start_tails():
        def per_expert(e, carry):
            g0 = tail_ref[0, e]

            def one(j, c):
                pltpu.make_async_copy(
                    zbuf, xs_ref.at[pl.ds(pl.multiple_of(g0 + j * GROUP_ALIGN, GROUP_ALIGN), GROUP_ALIGN), :],
                    sem.at[2]).start()
                return c

            lax.fori_loop(0, tail_ref[1, e], one, 0)
            return carry

        lax.fori_loop(0, N_EXPERTS, per_expert, 0)

    @pl.when(last)
    def _():
        zbuf[...] = jnp.zeros_like(zbuf)
        start_tails()
        _wait_pieces(n8_ref[i, TOTAL_LANE], wait_on(slot))
        _wait_pieces(tail_ref[1, TOTAL_LANE], wait_on(2))


def _dispatch(gs, ls, n8, tail, h2, idx, rank, lsf, n_slots):
    t = h2.shape[0]
    tm = min(MOE_TILE, t)
    nt = t // tm
    lrows = tm * TOP_K + N_EXPERTS * GROUP_ALIGN
    row = lambda i, *_: (i, 0)
    grid_spec = pltpu.PrefetchScalarGridSpec(
        num_scalar_prefetch=4,
        grid=(nt,),
        in_specs=[
            pl.BlockSpec((tm, D_MODEL), row),
            pl.BlockSpec((tm, LANES), row),
            pl.BlockSpec((tm, LANES), row),
            pl.BlockSpec((1, 1, LANES), lambda i, *_: (i, 0, 0)),
        ],
        out_specs=pl.BlockSpec(memory_space=pl.ANY),
        scratch_shapes=[pltpu.VMEM((2, lrows, PACK_W), U32), pltpu.VMEM((GROUP_ALIGN, PACK_W), U32),
                        pltpu.SemaphoreType.DMA((3,))],
    )
    return pl.pallas_call(
        _dispatch_body,
        grid_spec=grid_spec,
        out_shape=jax.ShapeDtypeStruct((n_slots, PACK_W), U32),
        compiler_params=pltpu.CompilerParams(dimension_semantics=("arbitrary",)),
        name="moe_dispatch",
    )(gs, ls, n8, tail, h2, idx, rank, lsf)


def _expert_body(be_ref, nu_ref, x_ref, wgu_ref, bgu_ref, wd_ref, bd_ref, y_ref, wgu_b, wd_b):
    j = pl.program_id(0)

    @pl.when(j < nu_ref[0])
    def _():
        @pl.when(jnp.logical_or(j == 0, be_ref[j] != be_ref[jnp.maximum(j - 1, 0)]))
        def _():
            wgu_b[...] = wgu_ref[0].astype(BF16)
            wd_b[...] = wd_ref[0].astype(BF16)

        gu = _dot(jnp.concatenate(_unpack_rows(x_ref[...]), axis=1), wgu_b[...]) + bgu_ref[0]
        gate = jnp.minimum(gu[:, :D_FF], SWIGLU_LIMIT)
        up = jnp.clip(gu[:, D_FF:], -SWIGLU_LIMIT, SWIGLU_LIMIT)
        act = (up + 1.0) * gate * _sigmoid(SWIGLU_ALPHA * gate)
        y_ref[...] = _pack_rows(_dot(act.astype(BF16), wd_b[...]) + bd_ref[0])


def _experts(block_e, n_used, xs, wgu, bgu, wd, bd):
    n_slots = xs.shape[0]
    nb = n_slots // EXPERT_BLOCK

    def blk(j, be, nu):
        return jnp.minimum(j, nu[0] - 1)

    grid_spec = pltpu.PrefetchScalarGridSpec(
        num_scalar_prefetch=2,
        grid=(nb,),
        in_specs=[
            pl.BlockSpec((EXPERT_BLOCK, PACK_W), lambda j, be, nu: (blk(j, be, nu), 0)),
            pl.BlockSpec((1, D_MODEL, 2 * D_FF), lambda j, be, nu: (be[blk(j, be, nu)], 0, 0)),
            pl.BlockSpec((1, 1, 2 * D_FF), lambda j, be, nu: (be[blk(j, be, nu)], 0, 0)),
            pl.BlockSpec((1, D_FF, D_MODEL), lambda j, be, nu: (be[blk(j, be, nu)], 0, 0)),
            pl.BlockSpec((1, 1, D_MODEL), lambda j, be, nu: (be[blk(j, be, nu)], 0, 0)),
        ],
        out_specs=pl.BlockSpec((EXPERT_BLOCK, PACK_W), lambda j, be, nu: (blk(j, be, nu), 0)),
        scratch_shapes=[pltpu.VMEM((D_MODEL, 2 * D_FF), BF16), pltpu.VMEM((D_FF, D_MODEL), BF16)],
    )
    return pl.pallas_call(
        _expert_body,
        grid_spec=grid_spec,
        out_shape=jax.ShapeDtypeStruct((n_slots, PACK_W), U32),
        compiler_params=pltpu.CompilerParams(dimension_semantics=("arbitrary",)),
        name="moe_experts",
    )(block_e, n_used, xs, wgu, bgu, wd, bd)


def _combine_body(gs_ref, ls_ref, n8_ref, x_ref, wt_ref, idx_ref, rank_ref, lsf_ref, y_ref, out_ref,
                  yl, sem):
    i = pl.program_id(0)
    slot = i % 2
    tables = (gs_ref, ls_ref, n8_ref)
    tm = x_ref.shape[0]
    lrows = yl.shape[1]

    def copy_into(s):
        def make_copy(l, g, rows):
            return pltpu.make_async_copy(y_ref.at[pl.ds(g, rows), :], yl.at[s, pl.ds(l, rows), :], sem.at[s])
        return make_copy

    def make_wait(pieces):
        rows = pieces * GROUP_ALIGN
        return pltpu.make_async_copy(y_ref.at[pl.ds(0, rows), :], yl.at[0, pl.ds(0, rows), :], sem.at[slot])

    @pl.when(i == 0)
    def _():
        yl[...] = jnp.zeros_like(yl)
        _start_group_copies(tables, i, copy_into(slot))

    @pl.when(i + 1 < pl.num_programs(0))
    def _():
        _start_group_copies(tables, i + 1, copy_into(1 - slot))

    pos = lax.broadcasted_iota(I32, (tm, lrows), 1)
    lp = _local_positions(idx_ref, rank_ref, lsf_ref)
    wt = wt_ref[...]
    wm = jnp.zeros((tm, lrows), F32)
    for k in range(TOP_K):
        wm = jnp.where(pos == lp[k], wt[:, k:k + 1], wm)
    _wait_pieces(n8_ref[i, TOTAL_LANE], make_wait)
    wmb = wm.astype(BF16)
    y_hi, y_lo = _unpack_rows(yl[slot])
    out_ref[:, :PACK_W] = x_ref[:, :PACK_W] + _dot(wmb, y_hi)
    out_ref[:, PACK_W:] = x_ref[:, PACK_W:] + _dot(wmb, y_lo)


def _combine(gs, ls, n8, x1, wts, idx, rank, lsf, y):
    t = x1.shape[0]
    tm = min(MOE_TILE, t)
    nt = t // tm
    lrows = tm * TOP_K + N_EXPERTS * GROUP_ALIGN
    row = lambda i, *_: (i, 0)
    grid_spec = pltpu.PrefetchScalarGridSpec(
        num_scalar_prefetch=3,
        grid=(nt,),
        in_specs=[
            pl.BlockSpec((tm, D_MODEL), row),
            pl.BlockSpec((tm, LANES), row),
            pl.BlockSpec((tm, LANES), row),
            pl.BlockSpec((tm, LANES), row),
            pl.BlockSpec((1, 1, LANES), lambda i, *_: (i, 0, 0)),
            pl.BlockSpec(memory_space=pl.ANY),
        ],
        out_specs=pl.BlockSpec((tm, D_MODEL), row),
        scratch_shapes=[pltpu.VMEM((2, lrows, PACK_W), U32), pltpu.SemaphoreType.DMA((2,))],
    )
    return pl.pallas_call(
        _combine_body,
        grid_spec=grid_spec,
        out_shape=jax.ShapeDtypeStruct((t, D_MODEL), F32),
        compiler_params=pltpu.CompilerParams(dimension_semantics=("arbitrary",)),
        name="moe_combine",
    )(gs, ls, n8, x1, wts, idx, rank, lsf, y)


def _pad_lanes(v, fill=0.0):
    v = v.astype(F32).reshape(1, -1)
    return jnp.pad(v, ((0, 0), (0, LANES - v.shape[1])), constant_values=fill)


def _mixer(x2, mem, rel_table, attn_norm, w_in, b_gate, dn_conv, dn_a_log, dn_dt_bias, dn_out_norm,
           da_q_norm, da_k_norm, da_lambda, da_subln, mem_norm, w_mem_kv, mx_q_norm, mx_k_norm,
           w_branch, w_out, batch, seq):
    wp = jnp.concatenate([w_in[:, :W_AB_LO], w_in[:, W_AB_HI:]], axis=1).astype(BF16)
    wab = jnp.pad(w_in[:, W_AB_LO:W_AB_HI], ((0, 0), (0, LANES - (W_AB_HI - W_AB_LO))))
    p, ab = _inproj(x2, attn_norm.reshape(1, -1), wp, wab)

    o_dn = _deltanet(p, ab, dn_conv, _pad_lanes(dn_a_log), _pad_lanes(dn_dt_bias),
                     dn_out_norm.reshape(1, -1), batch, seq)

    tq = min(ATT_BLOCK, seq)
    bias = _bias_tiles(rel_table.T, tq)
    o_da = _attention(p, bias, jnp.tile(da_q_norm, 2).reshape(1, -1), jnp.tile(da_k_norm, 2).reshape(1, -1),
                      da_lambda, da_subln.reshape(-1, 1), batch, seq)

    mk, mv = _memkv(mem, mem_norm.reshape(1, -1), w_mem_kv.astype(BF16), mx_k_norm.reshape(1, -1))
    return _merge(x2, o_dn, o_da, p, mk, mv, mx_q_norm.reshape(1, -1), b_gate.reshape(3, D_MODEL),
                  w_branch.astype(BF16), w_out.astype(BF16), seq)


def _moe(x1, ffn_norm, w_router, b_router, w_gate_up, b_gate_up, w_down, b_down):
    t = x1.shape[0]
    nt = t // min(MOE_TILE, t)
    max_rows = t * TOP_K + nt * N_EXPERTS * (GROUP_ALIGN - 1)
    n_blocks = -(-max_rows // EXPERT_BLOCK) + N_EXPERTS
    n_blocks_pad = -(-n_blocks // 8) * 8
    n_slots = n_blocks * EXPERT_BLOCK

    wr = jnp.pad(w_router, ((0, 0), (0, LANES - N_EXPERTS)))
    h2, idx, wts, rank, cnt = _router(x1, ffn_norm.reshape(1, -1), wr, _pad_lanes(b_router, NEG))
    gs, ls, n8, lsf, tail, meta = _plan(cnt[:, 0, :], n_blocks_pad)
    block_e = meta[:n_blocks, 0]
    n_used = meta[0:1, 1]
    lsf = lsf.reshape(nt, 1, LANES)

    xs = _dispatch(gs, ls, n8, tail, h2, idx, rank, lsf, n_slots)
    y = _experts(block_e, n_used, xs, w_gate_up, b_gate_up.reshape(N_EXPERTS, 1, -1),
                 w_down, b_down.reshape(N_EXPERTS, 1, -1))
    return _combine(gs, ls, n8, x1, wts, idx, rank, lsf, y)


def kernel(x, mem, rel_table, attn_norm, w_in, b_gate, dn_conv, dn_a_log, dn_dt_bias, dn_out_norm,
           da_q_norm, da_k_norm, da_lambda, da_subln, mem_norm, w_mem_kv, mx_q_norm, mx_k_norm,
           w_branch, w_out, ffn_norm, w_router, b_router, w_gate_up, b_gate_up, w_down, b_down):
    batch, seq, d = x.shape
    x2 = x.reshape(batch * seq, d)
    x1 = _mixer(x2, mem, rel_table, attn_norm[0], w_in[0], b_gate[0], dn_conv[0], dn_a_log[0],
                dn_dt_bias[0], dn_out_norm[0], da_q_norm[0], da_k_norm[0], da_lambda[0], da_subln[0],
                mem_norm[0], w_mem_kv[0], mx_q_norm[0], mx_k_norm[0], w_branch[0], w_out[0], batch, seq)
    out = _moe(x1, ffn_norm[0], w_router[0], b_router[0], w_gate_up[0], b_gate_up[0], w_down[0],
               b_down[0])
    return out.reshape(batch, seq, d)
```

```python
import functools
import math

import jax
import jax.numpy as jnp
from jax import lax
from jax.experimental import pallas as pl
from jax.experimental.pallas import tpu as pltpu

F32 = jnp.float32
BF16 = jnp.bfloat16
I32 = jnp.int32

D_MODEL = 1024
EPS = 1e-6
LANES = 128

DN_HEADS = 4
DN_DK = 128
DN_CHUNK = 64
DN_CONV = 4

DA_HEADS = 4
DA_DH = 64

MX_HEADS = 4
MX_DH = 128

REL_BUCKETS = 32
REL_MAX_DIST = 128

N_EXPERTS = 32
TOP_K = 4
D_FF = 1024
SWIGLU_LIMIT = 7.0
SWIGLU_ALPHA = 1.702
EXPERT_BLOCK = 512
MOE_TILE = 256
GROUP_ALIGN = 8

LAM_INIT = 0.8 - 0.6 * math.exp(-0.3 * 0)
LOG2E = 1.4426950408889634
NEG = -1e30

P_DNQ, P_DNK, P_DNV, P_DNZ = 0, 512, 1024, 1536
P_DAQ, P_DAK, P_DAV = 2048, 2560, 3072
P_MXQ = 3584
P_GATE = 4096
P_COLS = 7168
W_AB_LO, W_AB_HI = 2048, 2056


def _dot(a, b):
    return jnp.dot(a, b, preferred_element_type=F32)


def _dot_nt(a, b):
    return lax.dot_general(a, b, (((1,), (1,)), ((), ())), preferred_element_type=F32)


def _dot_tn(a, b):
    return lax.dot_general(a, b, (((0,), (0,)), ((), ())), preferred_element_type=F32)


def _split(x):
    hi = x.astype(BF16)
    lo = (x - hi.astype(F32)).astype(BF16)
    return hi, lo


def _dot3(a, b):
    ah, al = _split(a)
    bh, bl = _split(b)
    return _dot(ah, bh) + _dot(ah, bl) + _dot(al, bh)


def _sigmoid(x):
    return 1.0 / (1.0 + jnp.exp(-x))


def _rms(x, n):
    return lax.rsqrt(jnp.sum(x * x, axis=-1, keepdims=True) * (1.0 / n) + EPS)


def _inproj_body(x_ref, g_ref, w_ref, wab_ref, p_ref, ab_ref, h_scr):
    @pl.when(pl.program_id(1) == 0)
    def _():
        x = x_ref[...]
        h = x * _rms(x, D_MODEL) * g_ref[...]
        h_scr[...] = h.astype(BF16)
        ab_ref[...] = _dot3(h, wab_ref[...])

    p_ref[...] = _dot(h_scr[...], w_ref[...]).astype(p_ref.dtype)


def _inproj(x2, gain, wp, wab):
    t = x2.shape[0]
    tm = min(2048, t)
    tn = 1024
    return pl.pallas_call(
        _inproj_body,
        grid=(t // tm, P_COLS // tn),
        in_specs=[
            pl.BlockSpec((tm, D_MODEL), lambda i, j: (i, 0)),
            pl.BlockSpec((1, D_MODEL), lambda i, j: (0, 0)),
            pl.BlockSpec((D_MODEL, tn), lambda i, j: (0, j)),
            pl.BlockSpec((D_MODEL, LANES), lambda i, j: (0, 0)),
        ],
        out_specs=[
            pl.BlockSpec((tm, tn), lambda i, j: (i, j)),
            pl.BlockSpec((tm, LANES), lambda i, j: (i, 0)),
        ],
        out_shape=[
            jax.ShapeDtypeStruct((t, P_COLS), BF16),
            jax.ShapeDtypeStruct((t, LANES), F32),
        ],
        scratch_shapes=[pltpu.VMEM((tm, D_MODEL), BF16)],
        compiler_params=pltpu.CompilerParams(dimension_semantics=("parallel", "arbitrary")),
        name="inproj",
    )(x2, gain, wp, wab)


DN_HALO = 16
DN_SCAN_CHUNK = 256


def _deltanet_body(q_ref, k_ref, v_ref, z_ref, qh_ref, kh_ref, vh_ref, ab_ref, cw_ref, alog_ref,
                   dtb_ref, on_ref, o_ref, stage, qs, ks, vs, s_scr):
    i = pl.program_id(1)
    tc = q_ref.shape[0]
    hw = DN_HEADS * DN_DK

    @pl.when(i == 0)
    def _():
        s_scr[...] = jnp.zeros_like(s_scr)

    for src, halo, dst, off, kind in ((q_ref, qh_ref, qs, 0, "q"), (k_ref, kh_ref, ks, hw, "k"),
                                      (v_ref, vh_ref, vs, 2 * hw, "v")):
        hal = halo[...].astype(F32)
        stage[0:DN_HALO, :] = jnp.where(i == 0, 0.0, hal)
        stage[DN_HALO:DN_HALO + tc, :] = src[...].astype(F32)
        base = DN_HALO - (DN_CONV - 1)
        y = stage[base:base + tc, :] * cw_ref[0:1, off:off + hw]
        for j in range(1, DN_CONV):
            y = y + stage[base + j:base + j + tc, :] * cw_ref[j:j + 1, off:off + hw]
        y = y * _sigmoid(y)
        if kind == "v":
            dst[...] = y
        else:
            for h in range(DN_HEADS):
                sl = slice(h * DN_DK, (h + 1) * DN_DK)
                yh = y[:, sl]
                r = lax.rsqrt(jnp.sum(yh * yh, axis=-1, keepdims=True) + EPS)
                if kind == "q":
                    r = r * (DN_DK ** -0.5)
                dst[:, sl] = yh * r

    c = min(DN_SCAN_CHUNK, tc)
    row = lax.broadcasted_iota(I32, (c, c), 0)
    col = lax.broadcasted_iota(I32, (c, c), 1)
    incl = row >= col
    strict = row > col
    same_blk = (row // DN_CHUNK) == (col // DN_CHUNK)
    tri = jnp.where(incl, 1.0, 0.0).astype(BF16)
    eye = jnp.where(row == col, 1.0, 0.0)
    neg_a = -jnp.exp(alog_ref[...])
    dtb = dtb_ref[...]

    def chunk(ci, carry):
        r0 = pl.multiple_of(ci * c, c)
        abc = ab_ref[pl.ds(r0, c), :]
        a_in = abc + dtb
        g_all = neg_a * (jnp.maximum(a_in, 0.0) + jnp.log(1.0 + jnp.exp(-jnp.abs(a_in))))
        beta_all = _sigmoid(abc)
        zc = z_ref[pl.ds(r0, c), :].astype(F32)
        hs = range(DN_HEADS)
        sls = [slice(h * DN_DK, (h + 1) * DN_DK) for h in hs]
        q = [qs[pl.ds(r0, c), sl] for sl in sls]
        k = [ks[pl.ds(r0, c), sl] for sl in sls]
        v = [vs[pl.ds(r0, c), sl] for sl in sls]
        beta = [beta_all[:, DN_HEADS + h:DN_HEADS + h + 1] for h in hs]
        g_hi, g_lo = _split(g_all)
        gc_all = _dot(tri, g_hi) + _dot(tri, g_lo)
        gc_rows = gc_all.T
        kb = [k[h].astype(BF16) for h in hs]
        qkk = [_dot_nt(jnp.concatenate([q[h].astype(BF16), kb[h]], axis=0), kb[h]) for h in hs]
        gc = [gc_all[:, h:h + 1] for h in hs]
        decay = [jnp.where(incl, jnp.exp(gc[h] - gc_rows[h:h + 1, :]), 0.0) for h in hs]
        lower = [jnp.where(strict, qkk[h][c:] * decay[h] * beta[h], 0.0) for h in hs]
        pw = [jnp.where(same_blk, -lower[h], 0.0) for h in hs]
        dinv = [eye + pw[h] for h in hs]
        pwb = [pw[h].astype(BF16) for h in hs]
        for _ in range(int(math.log2(DN_CHUNK)) - 1):
            pwb = [_dot(pwb[h], pwb[h]).astype(BF16) for h in hs]
            dinv = [dinv[h] + _dot(dinv[h].astype(BF16), pwb[h]) for h in hs]
        dinv_b = [dinv[h].astype(BF16) for h in hs]
        pw = [-_dot(dinv_b[h], jnp.where(same_blk, 0.0, lower[h]).astype(BF16)) for h in hs]
        xm = [eye + pw[h] for h in hs]
        pwb = [pw[h].astype(BF16) for h in hs]
        for _ in range(int(math.log2(c // DN_CHUNK)) - 1):
            pwb = [_dot(pwb[h], pwb[h]).astype(BF16) for h in hs]
            xm = [xm[h] + _dot(xm[h].astype(BF16), pwb[h]) for h in hs]
        inv = [_dot(xm[h].astype(BF16), dinv_b[h]).astype(BF16) for h in hs]
        egc = [jnp.exp(gc[h]) for h in hs]
        rhs = [jnp.concatenate([v[h] * beta[h], k[h] * (beta[h] * egc[h])], axis=1).astype(BF16) for h in hs]
        sol = [_dot(inv[h], rhs[h]) for h in hs]
        qkm = [jnp.where(incl, qkk[h][:c] * decay[h], 0.0).astype(BF16) for h in hs]
        gl = [gc[h][c - 1:c, :] for h in hs]
        state = [s_scr[h] for h in hs]
        ws = [_dot(jnp.concatenate([sol[h][:, DN_DK:].astype(BF16), (q[h] * egc[h]).astype(BF16)], axis=0),
                   state[h].astype(BF16)) for h in hs]
        v_new = [sol[h][:, :DN_DK] - ws[h][:c] for h in hs]
        o = [ws[h][c:] + _dot(qkm[h], v_new[h].astype(BF16)) for h in hs]
        for h in hs:
            s_scr[h] = state[h] * jnp.exp(gl[h]) + _dot_tn(kb[h], (v_new[h] * jnp.exp(gl[h] - gc[h])).astype(BF16))
        for h in hs:
            zz = zc[:, sls[h]]
            on = o[h] * _rms(o[h], DN_DK) * on_ref[...]
            o_ref[pl.ds(r0, c), sls[h]] = (on * (zz * _sigmoid(zz))).astype(o_ref.dtype)
        return carry

    lax.fori_loop(0, tc // c, chunk, 0, unroll=True)


def _deltanet(p, ab, conv_w, alog_row, dtb_row, out_norm, batch, seq):
    t = batch * seq
    tc = min(1024, seq)
    nt = seq // tc
    hw = DN_HEADS * DN_DK

    def main(cb):
        return pl.BlockSpec((tc, hw), lambda b, i: (b * nt + i, cb))

    def halo(cb):
        return pl.BlockSpec(
            (DN_HALO, hw),
            lambda b, i: (jnp.maximum((b * seq + i * tc) // DN_HALO - 1, 0), cb))

    return pl.pallas_call(
        _deltanet_body,
        grid=(batch, nt),
        in_specs=[
            main(P_DNQ // hw), main(P_DNK // hw), main(P_DNV // hw), main(P_DNZ // hw),
            halo(P_DNQ // hw), halo(P_DNK // hw), halo(P_DNV // hw),
            pl.BlockSpec((tc, LANES), lambda b, i: (b * nt + i, 0)),
            pl.BlockSpec((DN_CONV, 3 * hw), lambda b, i: (0, 0)),
            pl.BlockSpec((1, LANES), lambda b, i: (0, 0)),
            pl.BlockSpec((1, LANES), lambda b, i: (0, 0)),
            pl.BlockSpec((1, DN_DK), lambda b, i: (0, 0)),
        ],
        out_specs=pl.BlockSpec((tc, hw), lambda b, i: (b * nt + i, 0)),
        out_shape=jax.ShapeDtypeStruct((t, hw), BF16),
        scratch_shapes=[
            pltpu.VMEM((DN_HALO + tc, hw), F32),
            pltpu.VMEM((tc, hw), F32),
            pltpu.VMEM((tc, hw), F32),
            pltpu.VMEM((tc, hw), F32),
            pltpu.VMEM((DN_HEADS, DN_DK, DN_DK), F32),
        ],
        compiler_params=pltpu.CompilerParams(dimension_semantics=("parallel", "arbitrary")),
        name="deltanet",
    )(p, p, p, p, p, p, p, ab, conv_w, alog_row, dtb_row, out_norm)


ATT_BLOCK = 512


def _bias_body(tbl_ref, o_ref):
    h = pl.program_id(0)
    tq = o_ref.shape[2]
    key = lax.broadcasted_iota(I32, (tq, tq), 0)
    qry = lax.broadcasted_iota(I32, (tq, tq), 1)
    max_exact = REL_BUCKETS // 2
    far = tbl_ref[h, REL_BUCKETS - 1]
    for d in range(2):
        n = qry - key + d * tq
        nn = jnp.maximum(n, 0)
        nf = jnp.maximum(nn, 1).astype(F32)
        large = max_exact + (jnp.log(nf / max_exact) / math.log(REL_MAX_DIST / max_exact)
                             * (REL_BUCKETS - max_exact)).astype(I32)
        large = jnp.minimum(large, REL_BUCKETS - 1)
        bucket = jnp.where(nn < max_exact, nn, large)
        val = jnp.zeros((tq, tq), F32)
        for b in range(REL_BUCKETS):
            val = jnp.where(bucket == b, tbl_ref[h, b], val)
        o_ref[0, d] = jnp.where(n >= 0, (val - far) * LOG2E, NEG)


def _bias_tiles(tbl_t, tq):
    return pl.pallas_call(
        _bias_body,
        grid=(DA_HEADS,),
        in_specs=[pl.BlockSpec(memory_space=pltpu.SMEM)],
        out_specs=pl.BlockSpec((1, 2, tq, tq), lambda h: (h, 0, 0, 0)),
        out_shape=jax.ShapeDtypeStruct((DA_HEADS, 2, tq, tq), F32),
        name="t5_bias_tiles",
    )(tbl_t)


DA_DV = 2 * DA_DH
DA_VROWS = DA_DV + 16


BOUND_SLACK = 1.02
MAX_SHIFT_GAP = 110.0


def _attn_body(q_ref, qn_ref, k_ref, v_ref, bias_ref, qg_ref, kg_ref, lam_ref, sg_ref, o_ref,
               kn, vt, kst, qc_s, flag_s, m_s, acc_s):
    qi = pl.program_id(2)
    slot = qi % 2
    tq = q_ref.shape[0]
    seq = k_ref.shape[0]
    tk = tq
    lo_mask = lax.broadcasted_iota(I32, (1, DA_DV), 1) < DA_DH

    def group_norm(x, gain):
        x2 = x * x
        lo = jnp.sum(jnp.where(lo_mask, x2, 0.0), axis=-1, keepdims=True)
        hi = jnp.sum(jnp.where(lo_mask, 0.0, x2), axis=-1, keepdims=True)
        r = jnp.where(lo_mask, lax.rsqrt(lo * (1.0 / DA_DH) + EPS), lax.rsqrt(hi * (1.0 / DA_DH) + EPS))
        return x * r * gain

    @pl.when(qi == 0)
    def _():
        ones = jnp.ones((DA_VROWS - DA_DV, tk), BF16)

        def body(c, carry):
            r0 = pl.multiple_of(c * tk, tk)
            kn[pl.ds(r0, tk), :] = group_norm(k_ref[pl.ds(r0, tk), :].astype(F32), kg_ref[...]).astype(BF16)
            vt[c, 0:DA_DV, :] = v_ref[pl.ds(r0, tk), :].astype(F32).T.astype(BF16)
            vt[c, DA_DV:DA_VROWS, :] = ones
            return carry
        lax.fori_loop(0, seq // tk, body, 0, unroll=2)

        g2 = (qg_ref[...] * kg_ref[...]) ** 2
        g_lo = jnp.max(jnp.where(lo_mask, g2, 0.0), axis=1, keepdims=True)
        g_hi = jnp.max(jnp.where(lo_mask, 0.0, g2), axis=1, keepdims=True)
        bound = jnp.sqrt(jnp.where(lo_mask, g_lo, g_hi)) * (DA_DH * DA_DH ** -0.5 * LOG2E * BOUND_SLACK)
        b0 = bias_ref[0, 0]
        b1 = bias_ref[0, 1]
        bmax = jnp.maximum(jnp.max(jnp.maximum(b0, b1), axis=0, keepdims=True), 0.0)
        bmin = jnp.minimum(jnp.min(jnp.minimum(jnp.where(b0 > 0.5 * NEG, b0, 0.0), b1), axis=0, keepdims=True), 0.0)
        bmax = jnp.max(bmax, axis=1, keepdims=True)
        bmin = jnp.min(bmin, axis=1, keepdims=True)
        kst[0:1, :] = bound + bmax
        worst = 2.0 * jnp.max(bound, axis=1, keepdims=True) + bmax - bmin
        flag_s[0] = jnp.where(worst[0, 0] <= MAX_SHIFT_GAP, 1, 0).astype(I32)

    def prepare(src_ref, s):
        q = group_norm(src_ref[...].astype(F32), qg_ref[...]) * (DA_DH ** -0.5 * LOG2E)
        qc_s[s] = jnp.concatenate([jnp.where(lo_mask, q, 0.0), jnp.where(lo_mask, 0.0, q)], axis=0).astype(BF16)

    @pl.when(qi == 0)
    def _():
        prepare(q_ref, slot)

    qcat = qc_s[slot]
    safe = flag_s[0] == 1
    acc_s[...] = jnp.zeros_like(acc_s)

    def block(j, d):
        r0 = pl.multiple_of(j * tk, tk)
        st = _dot_nt(kn[pl.ds(r0, tk), :], qcat)
        if d is not None:
            bias = bias_ref[0, d]
            st = st + jnp.concatenate([bias, bias], axis=1)
        m_prev = m_s[...]
        m_new = jnp.maximum(m_prev, jnp.max(st, axis=0, keepdims=True))
        alpha = jnp.exp2(m_prev - m_new)
        acc_s[...] = alpha * acc_s[...] + _dot(vt[j], jnp.exp2(st - m_new).astype(BF16))
        m_s[...] = m_new

    def blocks_fixed(js, ds=None):
        sts = [_dot_nt(kn[pl.ds(pl.multiple_of(j * tk, tk), tk), :], qcat) for j in js]
        if ds is not None:
            biases = [bias_ref[0, d] for d in ds]
            sts = [st + jnp.concatenate([b, b], axis=1) for st, b in zip(sts, biases)]
        shift = m_s[...]
        pts = [jnp.exp2(st - shift).astype(BF16) for st in sts]
        tot = _dot(vt[js[0]], pts[0])
        for j, pt in zip(js[1:], pts[1:]):
            tot = tot + _dot(vt[j], pt)
        acc_s[...] = acc_s[...] + tot

    def run(fixed_shift):
        n_far = jnp.maximum(qi - 1, 0)

        if fixed_shift:
            def far_quad(jj, carry):
                blocks_fixed([4 * jj + u for u in range(4)])
                return carry

            lax.fori_loop(0, n_far // 4, far_quad, 0)
            rem = n_far % 4

            @pl.when(rem >= 2)
            def _():
                blocks_fixed([n_far - rem, n_far - rem + 1])

            @pl.when(rem % 2 == 1)
            def _():
                blocks_fixed([n_far - 1])

            @pl.when(qi >= 1)
            def _():
                blocks_fixed([qi - 1, qi], [1, 0])
                prepare(qn_ref, 1 - slot)

            @pl.when(qi == 0)
            def _():
                blocks_fixed([qi], [0])
                prepare(qn_ref, 1 - slot)
        else:
            def far_one(j, carry):
                block(j, None)
                return carry

            lax.fori_loop(0, n_far, far_one, 0)

            @pl.when(qi >= 1)
            def _():
                block(qi - 1, 1)

            block(qi, 0)
            prepare(qn_ref, 1 - slot)

    @pl.when(safe)
    def _():
        m_s[...] = jnp.concatenate([jnp.broadcast_to(kst[0:1, 0:1], (1, tq)),
                                    jnp.broadcast_to(kst[0:1, DA_DH:DA_DH + 1], (1, tq))], axis=1)
        run(True)

    @pl.when(jnp.logical_not(safe))
    def _():
        m_s[...] = jnp.full(m_s.shape, NEG, F32)
        run(False)

    lam_p = lam_ref[...]
    lam = (jnp.exp(jnp.sum(lam_p[0:1, :] * lam_p[1:2, :], axis=-1, keepdims=True))
           - jnp.exp(jnp.sum(lam_p[2:3, :] * lam_p[3:4, :], axis=-1, keepdims=True)) + LAM_INIT)
    a0 = acc_s[:, 0:tq]
    a1 = acc_s[:, tq:2 * tq]
    ot = a0[0:DA_DV] / a0[DA_DV:DA_DV + 1] - lam * (a1[0:DA_DV] / a1[DA_DV:DA_DV + 1])
    r = lax.rsqrt(jnp.sum(ot * ot, axis=0, keepdims=True) * (1.0 / DA_DV) + EPS)
    ot = ot * r * (sg_ref[...] * (1.0 - LAM_INIT))
    o_ref[...] = ot.T.astype(o_ref.dtype)


def _attention(p, bias, qg, kg, lam_p, subln, batch, seq):
    t = batch * seq
    tq = min(ATT_BLOCK, seq)
    nq = seq // tq
    dv = DA_DV
    return pl.pallas_call(
        _attn_body,
        grid=(batch, DA_HEADS, nq),
        in_specs=[
            pl.BlockSpec((tq, dv), lambda b, h, i: (b * nq + i, P_DAQ // dv + h)),
            pl.BlockSpec((tq, dv), lambda b, h, i: (b * nq + jnp.minimum(i + 1, nq - 1), P_DAQ // dv + h)),
            pl.BlockSpec((seq, dv), lambda b, h, i: (b, P_DAK // dv + h)),
            pl.BlockSpec((seq, dv), lambda b, h, i: (b, P_DAV // dv + h)),
            pl.BlockSpec((1, 2, tq, tq), lambda b, h, i: (h, 0, 0, 0)),
            pl.BlockSpec((1, dv), lambda b, h, i: (0, 0)),
            pl.BlockSpec((1, dv), lambda b, h, i: (0, 0)),
            pl.BlockSpec((4, DA_DH), lambda b, h, i: (0, 0)),
            pl.BlockSpec((dv, 1), lambda b, h, i: (0, 0)),
        ],
        out_specs=pl.BlockSpec((tq, dv), lambda b, h, i: (b * nq + i, h)),
        out_shape=jax.ShapeDtypeStruct((t, DA_HEADS * dv), BF16),
        scratch_shapes=[
            pltpu.VMEM((seq, dv), BF16),
            pltpu.VMEM((seq // tq, DA_VROWS, tq), BF16),
            pltpu.VMEM((8, dv), F32),
            pltpu.VMEM((2, 2 * tq, dv), BF16),
            pltpu.SMEM((1,), I32),
            pltpu.VMEM((1, 2 * tq), F32),
            pltpu.VMEM((DA_VROWS, 2 * tq), F32),
        ],
        compiler_params=pltpu.CompilerParams(dimension_semantics=("parallel", "parallel", "arbitrary")),
        name="diff_attention",
    )(p, p, p, p, bias, qg, kg, lam_p, subln)


def _memkv_body(mem_ref, mg_ref, w_ref, kg_ref, mk_ref, mv_ref):
    x = mem_ref[0]
    xn = x * _rms(x, D_MODEL) * mg_ref[...]
    kv = _dot(xn.astype(BF16), w_ref[...])
    hw = MX_HEADS * MX_DH
    for h in range(MX_HEADS):
        sl = slice(h * MX_DH, (h + 1) * MX_DH)
        kh = kv[:, sl]
        mk_ref[0, :, sl] = (kh * _rms(kh, MX_DH) * kg_ref[...]).astype(BF16)
    mv_ref[0] = kv[:, hw:].astype(BF16)


def _memkv(mem, mem_norm, w_kv, k_norm):
    b, n, _ = mem.shape
    hw = MX_HEADS * MX_DH
    return pl.pallas_call(
        _memkv_body,
        grid=(b,),
        in_specs=[
            pl.BlockSpec((1, n, D_MODEL), lambda i: (i, 0, 0)),
            pl.BlockSpec((1, D_MODEL), lambda i: (0, 0)),
            pl.BlockSpec((D_MODEL, 2 * hw), lambda i: (0, 0)),
            pl.BlockSpec((1, MX_DH), lambda i: (0, 0)),
        ],
        out_specs=[pl.BlockSpec((1, n, hw), lambda i: (i, 0, 0))] * 2,
        out_shape=[jax.ShapeDtypeStruct((b, n, hw), BF16)] * 2,
        name="memory_kv",
    )(mem, mem_norm, w_kv, k_norm)


def _merge_body(x_ref, odn_ref, oda_ref, mxq_ref, g0_ref, g1_ref, g2_ref, mk_ref, mv_ref, qg_ref,
                bg_ref, wb_ref, wo_ref, out_ref, omx):
    for h in range(MX_HEADS):
        sl = slice(h * MX_DH, (h + 1) * MX_DH)
        qh = mxq_ref[:, sl].astype(F32)
        qh = qh * _rms(qh, MX_DH) * qg_ref[...] * (MX_DH ** -0.5 * LOG2E)
        s = _dot_nt(qh.astype(BF16), mk_ref[0, :, sl])
        p = jnp.exp2(s - jnp.max(s, axis=-1, keepdims=True))
        oh = _dot(p.astype(BF16), mv_ref[0, :, sl]) / jnp.sum(p, axis=-1, keepdims=True)
        omx[:, sl] = oh.astype(BF16)
    y = None
    for r, (o_r, g_r) in enumerate(((odn_ref, g0_ref), (oda_ref, g1_ref), (omx, g2_ref))):
        gate = _sigmoid(g_r[...].astype(F32) + bg_ref[r:r + 1, :])
        term = gate * _dot(o_r[...], wb_ref[r])
        y = term if y is None else y + term
    out_ref[...] = x_ref[...] + _dot(y.astype(BF16), wo_ref[...])


def _merge(x2, o_dn, o_da, p, mk, mv, q_norm, b_gate, w_branch, w_out, seq):
    t = x2.shape[0]
    tm = min(1024, seq)
    nt = seq // tm
    bw = 512
    n_mem = mk.shape[1]
    return pl.pallas_call(
        _merge_body,
        grid=(t // tm,),
        in_specs=[
            pl.BlockSpec((tm, D_MODEL), lambda i: (i, 0)),
            pl.BlockSpec((tm, bw), lambda i: (i, 0)),
            pl.BlockSpec((tm, bw), lambda i: (i, 0)),
            pl.BlockSpec((tm, bw), lambda i: (i, P_MXQ // bw)),
            pl.BlockSpec((tm, D_MODEL), lambda i: (i, P_GATE // D_MODEL)),
            pl.BlockSpec((tm, D_MODEL), lambda i: (i, P_GATE // D_MODEL + 1)),
            pl.BlockSpec((tm, D_MODEL), lambda i: (i, P_GATE // D_MODEL + 2)),
            pl.BlockSpec((1, n_mem, bw), lambda i: (i // nt, 0, 0)),
            pl.BlockSpec((1, n_mem, bw), lambda i: (i // nt, 0, 0)),
            pl.BlockSpec((1, MX_DH), lambda i: (0, 0)),
            pl.BlockSpec((3, D_MODEL), lambda i: (0, 0)),
            pl.BlockSpec((3, bw, D_MODEL), lambda i: (0, 0, 0)),
            pl.BlockSpec((D_MODEL, D_MODEL), lambda i: (0, 0)),
        ],
        out_specs=pl.BlockSpec((tm, D_MODEL), lambda i: (i, 0)),
        out_shape=jax.ShapeDtypeStruct((t, D_MODEL), F32),
        scratch_shapes=[pltpu.VMEM((tm, bw), BF16)],
        compiler_params=pltpu.CompilerParams(dimension_semantics=("parallel",)),
        name="merge",
    )(x2, o_dn, o_da, p, p, p, p, mk, mv, q_norm, b_gate, w_branch, w_out)


def _router_body(x_ref, g_ref, wr_ref, br_ref, h_ref, idx_ref, wt_ref, rank_ref, cnt_ref):
    tm = x_ref.shape[0]
    x = x_ref[...]
    h = x * _rms(x, D_MODEL) * g_ref[...]
    h_ref[...] = h.astype(BF16)
    logits = _dot3(h, wr_ref[...]) + br_ref[...]
    lane = lax.broadcasted_iota(I32, (tm, LANES), 1)
    lane_f = lane.astype(F32)
    work = logits
    sel = jnp.zeros((tm, LANES), F32)
    vals, idxs = [], []
    for _ in range(TOP_K):
        mx = jnp.max(work, axis=-1, keepdims=True)
        ik = jnp.min(jnp.where(work == mx, lane_f, float(LANES)), axis=-1, keepdims=True)
        hit = lane_f == ik
        sel = jnp.where(hit, 1.0, sel)
        work = jnp.where(hit, -jnp.inf, work)
        vals.append(mx)
        idxs.append(ik)
    es = [jnp.exp(v - vals[0]) for v in vals]
    den = es[0] + es[1] + es[2] + es[3]
    sub = cnt_ref.shape[0]
    mt = tm // sub
    r = lax.broadcasted_iota(I32, (tm, tm), 0)
    c = lax.broadcasted_iota(I32, (tm, tm), 1)
    tril = jnp.where((r > c) & (r // mt == c // mt), 1.0, 0.0).astype(BF16)
    cum = _dot(tril, sel.astype(BF16))
    idx_o = jnp.zeros((tm, LANES), F32)
    wt_o = jnp.zeros((tm, LANES), F32)
    rank_o = jnp.zeros((tm, LANES), F32)
    for k in range(TOP_K):
        rk = jnp.sum(jnp.where(lane_f == idxs[k], cum, 0.0), axis=-1, keepdims=True)
        idx_o = jnp.where(lane == k, idxs[k], idx_o)
        wt_o = jnp.where(lane == k, es[k] / den, wt_o)
        rank_o = jnp.where(lane == k, rk, rank_o)
    idx_ref[...] = idx_o.astype(I32)
    wt_ref[...] = wt_o
    rank_ref[...] = rank_o.astype(I32)
    for s in range(sub):
        cnt_ref[s] = jnp.broadcast_to(jnp.sum(sel[s * mt:(s + 1) * mt], axis=0, keepdims=True), (8, LANES))


def _router(x1, gain, w_r, b_r):
    t = x1.shape[0]
    mt = min(MOE_TILE, t)
    sub = 2 if t % (2 * mt) == 0 else 1
    tm = sub * mt
    row = lambda i: (i, 0)
    fixed = lambda i: (0, 0)
    return pl.pallas_call(
        _router_body,
        grid=(t // tm,),
        in_specs=[
            pl.BlockSpec((tm, D_MODEL), row),
            pl.BlockSpec((1, D_MODEL), fixed),
            pl.BlockSpec((D_MODEL, LANES), fixed),
            pl.BlockSpec((1, LANES), fixed),
        ],
        out_specs=[
            pl.BlockSpec((tm, D_MODEL), row),
            pl.BlockSpec((tm, LANES), row),
            pl.BlockSpec((tm, LANES), row),
            pl.BlockSpec((tm, LANES), row),
            pl.BlockSpec((sub, 8, LANES), lambda i: (i, 0, 0)),
        ],
        out_shape=[
            jax.ShapeDtypeStruct((t, D_MODEL), BF16),
            jax.ShapeDtypeStruct((t, LANES), I32),
            jax.ShapeDtypeStruct((t, LANES), F32),
            jax.ShapeDtypeStruct((t, LANES), I32),
            jax.ShapeDtypeStruct((t // mt, 8, LANES), F32),
        ],
        compiler_params=pltpu.CompilerParams(dimension_semantics=("arbitrary",)),
        name="router",
    )(x1, gain, w_r, b_r)


def _lane_cumsum(x):
    lane = lax.broadcasted_iota(I32, x.shape, 1)
    s = 1
    while s < N_EXPERTS:
        x = x + jnp.where(lane >= s, pltpu.roll(x, s, axis=1), 0.0)
        s *= 2
    return x


def _plan_body(cnt_ref, gs_ref, ls_ref, n8_ref, lsf_ref, tail_ref, meta_ref):
    nt = cnt_ref.shape[0]
    ga = float(GROUP_ALIGN)
    eb = float(EXPERT_BLOCK)
    lane = lax.broadcasted_iota(I32, (nt, LANES), 1)
    r8 = jnp.where(lane < N_EXPERTS, jnp.floor((cnt_ref[...] + (ga - 1.0)) * (1.0 / ga)) * ga, 0.0)
    ri = lax.broadcasted_iota(I32, (nt, nt), 0)
    ci = lax.broadcasted_iota(I32, (nt, nt), 1)
    before = _dot(jnp.where(ri > ci, 1.0, 0.0).astype(BF16), r8.astype(BF16))
    tot = jnp.sum(r8, axis=0, keepdims=True)
    region = jnp.floor((tot + (eb - 1.0)) * (1.0 / eb)) * eb
    pends = _lane_cumsum(jnp.broadcast_to(region, (8, LANES)))[0:1, :]
    pstart = pends - region
    lstart = _lane_cumsum(r8) - r8
    gs_ref[...] = (pstart + before).astype(I32)
    ls_ref[...] = lstart.astype(I32)
    pieces = r8 * (1.0 / ga)
    n8_ref[...] = jnp.where(lane == TOTAL_LANE, jnp.sum(pieces, axis=1, keepdims=True), pieces).astype(I32)
    lsf_ref[...] = lstart
    row8 = lax.broadcasted_iota(I32, (8, LANES), 0)
    tail_n = (region - tot) * (1.0 / ga)
    lane8 = lax.broadcasted_iota(I32, (8, LANES), 1)
    tail_n = jnp.where(lane8 == TOTAL_LANE, jnp.sum(tail_n, axis=1, keepdims=True), tail_n)
    tail = jnp.where(row8 == 0, pstart + tot, jnp.where(row8 == 1, tail_n, 0.0))
    tail_ref[...] = tail.astype(I32)
    nb = meta_ref.shape[0]
    ln = lax.broadcasted_iota(I32, (nb, LANES), 1)
    blk = lax.broadcasted_iota(I32, (nb, LANES), 0).astype(F32) * eb
    be = jnp.sum(jnp.where((ln < N_EXPERTS) & (pends <= blk), 1.0, 0.0), axis=-1, keepdims=True)
    be = jnp.minimum(be, float(N_EXPERTS - 1))
    used = jnp.sum(jnp.where(ln == N_EXPERTS - 1, pends, 0.0), axis=-1, keepdims=True) * (1.0 / eb)
    meta_ref[...] = jnp.where(ln == 0, be, jnp.where(ln == 1, used, 0.0)).astype(I32)


def _plan(cnt, n_blocks_pad):
    nt = cnt.shape[0]
    shp = jax.ShapeDtypeStruct((nt, LANES), I32)
    return pl.pallas_call(
        _plan_body,
        out_shape=[shp, shp, shp, jax.ShapeDtypeStruct((nt, LANES), F32),
                   jax.ShapeDtypeStruct((8, LANES), I32),
                   jax.ShapeDtypeStruct((n_blocks_pad, LANES), I32)],
        name="dispatch_plan",
    )(cnt)


def _local_positions(idx_ref, rank_ref, lsf_ref):
    tm = idx_ref.shape[0]
    lane = lax.broadcasted_iota(I32, (tm, LANES), 1)
    idx = idx_ref[...]
    rank = rank_ref[...].astype(F32)
    ls_row = lsf_ref[0]
    out = []
    for k in range(TOP_K):
        base = jnp.sum(jnp.where(lane == idx[:, k:k + 1], ls_row, 0.0), axis=-1, keepdims=True)
        out.append((base + rank[:, k:k + 1]).astype(I32))
    return out


def _local_positions_lanes(idx_ref, rank_ref, lsf_ref):
    tm = idx_ref.shape[0]
    idx_t = idx_ref[...].astype(F32).T
    rank_t = rank_ref[...].astype(F32).T
    ls_col = jnp.broadcast_to(lsf_ref[0], (LANES, LANES)).T[:, 0:1]
    expert = lax.broadcasted_iota(I32, (LANES, tm), 0).astype(F32)
    out = []
    for k in range(TOP_K):
        base = jnp.sum(jnp.where(expert == idx_t[k:k + 1, :], ls_col, 0.0), axis=0, keepdims=True)
        out.append((base + rank_t[k:k + 1, :]).astype(I32))
    return out


PACK_W = D_MODEL // 2
U32 = jnp.uint32


def _pack_rows(x):
    xb = x.astype(BF16).astype(F32)
    hi = lax.bitcast_convert_type(xb[:, :PACK_W], U32)
    lo = lax.bitcast_convert_type(xb[:, PACK_W:], U32)
    return hi | (lo >> 16)


def _unpack_rows(w):
    hi = lax.bitcast_convert_type(w & jnp.uint32(0xFFFF0000), F32)
    lo = lax.bitcast_convert_type(w << 16, F32)
    return hi.astype(BF16), lo.astype(BF16)


TOTAL_LANE = LANES - 1
WAIT_CHUNK = 128
BIG_PIECE = 4


def _start_group_copies(tables, tile, make_copy):
    gs_ref, ls_ref, n8_ref = tables

    def group(e, carry):
        g0 = gs_ref[tile, e]
        l0 = ls_ref[tile, e]
        n = n8_ref[tile, e]
        nbig = n // BIG_PIECE

        def big(j, c):
            off = j * (BIG_PIECE * GROUP_ALIGN)
            make_copy(pl.multiple_of(l0 + off, GROUP_ALIGN), pl.multiple_of(g0 + off, GROUP_ALIGN),
                      BIG_PIECE * GROUP_ALIGN).start()
            return c

        def one(j, c):
            off = j * GROUP_ALIGN
            make_copy(pl.multiple_of(l0 + off, GROUP_ALIGN), pl.multiple_of(g0 + off, GROUP_ALIGN),
                      GROUP_ALIGN).start()
            return c

        lax.fori_loop(0, nbig, big, 0)
        lax.fori_loop(nbig * BIG_PIECE, n, one, 0)
        return carry

    lax.fori_loop(0, N_EXPERTS, group, 0)


def _wait_pieces(n, make_wait):
    def chunk(j, c):
        make_wait(WAIT_CHUNK).wait()
        return c

    lax.fori_loop(0, n // WAIT_CHUNK, chunk, 0)
    b = WAIT_CHUNK // 2
    while b >= 1:
        def _(b=b):
            make_wait(b).wait()
        pl.when((n & b) != 0)(_)
        b //= 2


def _dispatch_body(gs_ref, ls_ref, n8_ref, tail_ref, h_ref, idx_ref, rank_ref, lsf_ref, xs_ref,
                   xl, zbuf, sem):
    i = pl.program_id(0)
    last = i == pl.num_programs(0) - 1
    slot = i % 2
    tables = (gs_ref, ls_ref, n8_ref)
    tm = h_ref.shape[0]
    lrows = xl.shape[1]
    pos = lax.broadcasted_iota(I32, (lrows, tm), 0)
    lp = _local_positions_lanes(idx_ref, rank_ref, lsf_ref)
    hit = pos == lp[0]
    for k in range(1, TOP_K):
        hit = hit | (pos == lp[k])
    xl[slot] = _pack_rows(_dot(jnp.where(hit, 1.0, 0.0).astype(BF16), h_ref[...]))

    def copy_from(s):
        def make_copy(l, g, rows):
            return pltpu.make_async_copy(xl.at[s, pl.ds(l, rows), :], xs_ref.at[pl.ds(g, rows), :], sem.at[s])
        return make_copy

    def wait_on(s):
        def make_wait(pieces):
            rows = pieces * GROUP_ALIGN
            return pltpu.make_async_copy(xl.at[0, pl.ds(0, rows), :], xs_ref.at[pl.ds(0, rows), :], sem.at[s])
        return make_wait

    @pl.when(i >= 1)
    def _():
        _wait_pieces(n8_ref[i - 1, TOTAL_LANE], wait_on(1 - slot))

    _start_group_copies(tables, i, copy_from(slot))

    def start_tails():
        def per_expert(e, carry):
            g0 = tail_ref[0, e]

            def one(j, c):
                pltpu.make_async_copy(
                    zbuf, xs_ref.at[pl.ds(pl.multiple_of(g0 + j * GROUP_ALIGN, GROUP_ALIGN), GROUP_ALIGN), :],
                    sem.at[2]).start()
                return c

            lax.fori_loop(0, tail_ref[1, e], one, 0)
            return carry

        lax.fori_loop(0, N_EXPERTS, per_expert, 0)

    @pl.when(last)
    def _():
        zbuf[...] = jnp.zeros_like(zbuf)
        start_tails()
        _wait_pieces(n8_ref[i, TOTAL_LANE], wait_on(slot))
        _wait_pieces(tail_ref[1, TOTAL_LANE], wait_on(2))


def _dispatch(gs, ls, n8, tail, h2, idx, rank, lsf, n_slots):
    t = h2.shape[0]
    tm = min(MOE_TILE, t)
    nt = t // tm
    lrows = tm * TOP_K + N_EXPERTS * GROUP_ALIGN
    row = lambda i, *_: (i, 0)
    grid_spec = pltpu.PrefetchScalarGridSpec(
        num_scalar_prefetch=4,
        grid=(nt,),
        in_specs=[
            pl.BlockSpec((tm, D_MODEL), row),
            pl.BlockSpec((tm, LANES), row),
            pl.BlockSpec((tm, LANES), row),
            pl.BlockSpec((1, 1, LANES), lambda i, *_: (i, 0, 0)),
        ],
        out_specs=pl.BlockSpec(memory_space=pl.ANY),
        scratch_shapes=[pltpu.VMEM((2, lrows, PACK_W), U32), pltpu.VMEM((GROUP_ALIGN, PACK_W), U32),
                        pltpu.SemaphoreType.DMA((3,))],
    )
    return pl.pallas_call(
        _dispatch_body,
        grid_spec=grid_spec,
        out_shape=jax.ShapeDtypeStruct((n_slots, PACK_W), U32),
        compiler_params=pltpu.CompilerParams(dimension_semantics=("arbitrary",)),
        name="moe_dispatch",
    )(gs, ls, n8, tail, h2, idx, rank, lsf)


def _expert_body(be_ref, nu_ref, x_ref, wgu_ref, bgu_ref, wd_ref, bd_ref, y_ref, wgu_b, wd_b):
    j = pl.program_id(0)

    @pl.when(j < nu_ref[0])
    def _():
        @pl.when(jnp.logical_or(j == 0, be_ref[j] != be_ref[jnp.maximum(j - 1, 0)]))
        def _():
            wgu_b[...] = wgu_ref[0].astype(BF16)
            wd_b[...] = wd_ref[0].astype(BF16)

        gu = _dot(jnp.concatenate(_unpack_rows(x_ref[...]), axis=1), wgu_b[...]) + bgu_ref[0]
        gate = jnp.minimum(gu[:, :D_FF], SWIGLU_LIMIT)
        up = jnp.clip(gu[:, D_FF:], -SWIGLU_LIMIT, SWIGLU_LIMIT)
        act = (up + 1.0) * gate * _sigmoid(SWIGLU_ALPHA * gate)
        y_ref[...] = _pack_rows(_dot(act.astype(BF16), wd_b[...]) + bd_ref[0])


def _experts(block_e, n_used, xs, wgu, bgu, wd, bd):
    n_slots = xs.shape[0]
    nb = n_slots // EXPERT_BLOCK

    def blk(j, be, nu):
        return jnp.minimum(j, nu[0] - 1)

    grid_spec = pltpu.PrefetchScalarGridSpec(
        num_scalar_prefetch=2,
        grid=(nb,),
        in_specs=[
            pl.BlockSpec((EXPERT_BLOCK, PACK_W), lambda j, be, nu: (blk(j, be, nu), 0)),
            pl.BlockSpec((1, D_MODEL, 2 * D_FF), lambda j, be, nu: (be[blk(j, be, nu)], 0, 0)),
            pl.BlockSpec((1, 1, 2 * D_FF), lambda j, be, nu: (be[blk(j, be, nu)], 0, 0)),
            pl.BlockSpec((1, D_FF, D_MODEL), lambda j, be, nu: (be[blk(j, be, nu)], 0, 0)),
            pl.BlockSpec((1, 1, D_MODEL), lambda j, be, nu: (be[blk(j, be, nu)], 0, 0)),
        ],
        out_specs=pl.BlockSpec((EXPERT_BLOCK, PACK_W), lambda j, be, nu: (blk(j, be, nu), 0)),
        scratch_shapes=[pltpu.VMEM((D_MODEL, 2 * D_FF), BF16), pltpu.VMEM((D_FF, D_MODEL), BF16)],
    )
    return pl.pallas_call(
        _expert_body,
        grid_spec=grid_spec,
        out_shape=jax.ShapeDtypeStruct((n_slots, PACK_W), U32),
        compiler_params=pltpu.CompilerParams(dimension_semantics=("arbitrary",)),
        name="moe_experts",
    )(block_e, n_used, xs, wgu, bgu, wd, bd)


def _combine_body(gs_ref, ls_ref, n8_ref, x_ref, wt_ref, idx_ref, rank_ref, lsf_ref, y_ref, out_ref,
                  yl, sem):
    i = pl.program_id(0)
    slot = i % 2
    tables = (gs_ref, ls_ref, n8_ref)
    tm = x_ref.shape[0]
    lrows = yl.shape[1]

    def copy_into(s):
        def make_copy(l, g, rows):
            return pltpu.make_async_copy(y_ref.at[pl.ds(g, rows), :], yl.at[s, pl.ds(l, rows), :], sem.at[s])
        return make_copy

    def make_wait(pieces):
        rows = pieces * GROUP_ALIGN
        return pltpu.make_async_copy(y_ref.at[pl.ds(0, rows), :], yl.at[0, pl.ds(0, rows), :], sem.at[slot])

    @pl.when(i == 0)
    def _():
        yl[...] = jnp.zeros_like(yl)
        _start_group_copies(tables, i, copy_into(slot))

    @pl.when(i + 1 < pl.num_programs(0))
    def _():
        _start_group_copies(tables, i + 1, copy_into(1 - slot))

    pos = lax.broadcasted_iota(I32, (tm, lrows), 1)
    lp = _local_positions(idx_ref, rank_ref, lsf_ref)
    wt = wt_ref[...]
    wm = jnp.zeros((tm, lrows), F32)
    for k in range(TOP_K):
        wm = jnp.where(pos == lp[k], wt[:, k:k + 1], wm)
    _wait_pieces(n8_ref[i, TOTAL_LANE], make_wait)
    wmb = wm.astype(BF16)
    y_hi, y_lo = _unpack_rows(yl[slot])
    out_ref[:, :PACK_W] = x_ref[:, :PACK_W] + _dot(wmb, y_hi)
    out_ref[:, PACK_W:] = x_ref[:, PACK_W:] + _dot(wmb, y_lo)


def _combine(gs, ls, n8, x1, wts, idx, rank, lsf, y):
    t = x1.shape[0]
    tm = min(MOE_TILE, t)
    nt = t // tm
    lrows = tm * TOP_K + N_EXPERTS * GROUP_ALIGN
    row = lambda i, *_: (i, 0)
    grid_spec = pltpu.PrefetchScalarGridSpec(
        num_scalar_prefetch=3,
        grid=(nt,),
        in_specs=[
            pl.BlockSpec((tm, D_MODEL), row),
            pl.BlockSpec((tm, LANES), row),
            pl.BlockSpec((tm, LANES), row),
            pl.BlockSpec((tm, LANES), row),
            pl.BlockSpec((1, 1, LANES), lambda i, *_: (i, 0, 0)),
            pl.BlockSpec(memory_space=pl.ANY),
        ],
        out_specs=pl.BlockSpec((tm, D_MODEL), row),
        scratch_shapes=[pltpu.VMEM((2, lrows, PACK_W), U32), pltpu.SemaphoreType.DMA((2,))],
    )
    return pl.pallas_call(
        _combine_body,
        grid_spec=grid_spec,
        out_shape=jax.ShapeDtypeStruct((t, D_MODEL), F32),
        compiler_params=pltpu.CompilerParams(dimension_semantics=("arbitrary",)),
        name="moe_combine",
    )(gs, ls, n8, x1, wts, idx, rank, lsf, y)


def _pad_lanes(v, fill=0.0):
    v = v.astype(F32).reshape(1, -1)
    return jnp.pad(v, ((0, 0), (0, LANES - v.shape[1])), constant_values=fill)


def _mixer(x2, mem, rel_table, attn_norm, w_in, b_gate, dn_conv, dn_a_log, dn_dt_bias, dn_out_norm,
           da_q_norm, da_k_norm, da_lambda, da_subln, mem_norm, w_mem_kv, mx_q_norm, mx_k_norm,
           w_branch, w_out, batch, seq):
    wp = jnp.concatenate([w_in[:, :W_AB_LO], w_in[:, W_AB_HI:]], axis=1).astype(BF16)
    wab = jnp.pad(w_in[:, W_AB_LO:W_AB_HI], ((0, 0), (0, LANES - (W_AB_HI - W_AB_LO))))
    p, ab = _inproj(x2, attn_norm.reshape(1, -1), wp, wab)

    o_dn = _deltanet(p, ab, dn_conv, _pad_lanes(dn_a_log), _pad_lanes(dn_dt_bias),
                     dn_out_norm.reshape(1, -1), batch, seq)

    tq = min(ATT_BLOCK, seq)
    bias = _bias_tiles(rel_table.T, tq)
    o_da = _attention(p, bias, jnp.tile(da_q_norm, 2).reshape(1, -1), jnp.tile(da_k_norm, 2).reshape(1, -1),
                      da_lambda, da_subln.reshape(-1, 1), batch, seq)

    mk, mv = _memkv(mem, mem_norm.reshape(1, -1), w_mem_kv.astype(BF16), mx_k_norm.reshape(1, -1))
    return _merge(x2, o_dn, o_da, p, mk, mv, mx_q_norm.reshape(1, -1), b_gate.reshape(3, D_MODEL),
                  w_branch.astype(BF16), w_out.astype(BF16), seq)


def _moe(x1, ffn_norm, w_router, b_router, w_gate_up, b_gate_up, w_down, b_down):
    t = x1.shape[0]
    nt = t // min(MOE_TILE, t)
    max_rows = t * TOP_K + nt * N_EXPERTS * (GROUP_ALIGN - 1)
    n_blocks = -(-max_rows // EXPERT_BLOCK) + N_EXPERTS
    n_blocks_pad = -(-n_blocks // 8) * 8
    n_slots = n_blocks * EXPERT_BLOCK

    wr = jnp.pad(w_router, ((0, 0), (0, LANES - N_EXPERTS)))
    h2, idx, wts, rank, cnt = _router(x1, ffn_norm.reshape(1, -1), wr, _pad_lanes(b_router, NEG))
    gs, ls, n8, lsf, tail, meta = _plan(cnt[:, 0, :], n_blocks_pad)
    block_e = meta[:n_blocks, 0]
    n_used = meta[0:1, 1]
    lsf = lsf.reshape(nt, 1, LANES)

    xs = _dispatch(gs, ls, n8, tail, h2, idx, rank, lsf, n_slots)
    y = _experts(block_e, n_used, xs, w_gate_up, b_gate_up.reshape(N_EXPERTS, 1, -1),
                 w_down, b_down.reshape(N_EXPERTS, 1, -1))
    return _combine(gs, ls, n8, x1, wts, idx, rank, lsf, y)


def kernel(x, mem, rel_table, attn_norm, w_in, b_gate, dn_conv, dn_a_log, dn_dt_bias, dn_out_norm,
           da_q_norm, da_k_norm, da_lambda, da_subln, mem_norm, w_mem_kv, mx_q_norm, mx_k_norm,
           w_branch, w_out, ffn_norm, w_router, b_router, w_gate_up, b_gate_up, w_down, b_down):
    batch, seq, d = x.shape
    x2 = x.reshape(batch * seq, d)
    x1 = _mixer(x2, mem, rel_table, attn_norm[0], w_in[0], b_gate[0], dn_conv[0], dn_a_log[0],
                dn_dt_bias[0], dn_out_norm[0], da_q_norm[0], da_k_norm[0], da_lambda[0], da_subln[0],
                mem_norm[0], w_mem_kv[0], mx_q_norm[0], mx_k_norm[0], w_branch[0], w_out[0], batch, seq)
    out = _moe(x1, ffn_norm[0], w_router[0], b_router[0], w_gate_up[0], b_gate_up[0], w_down[0],
               b_down[0])
    return out.reshape(batch, seq, d)
```

```python
import functools
import math

import jax
import jax.numpy as jnp
from jax import lax
from jax.experimental import pallas as pl
from jax.experimental.pallas import tpu as pltpu

F32 = jnp.float32
BF16 = jnp.bfloat16
I32 = jnp.int32

D_MODEL = 1024
EPS = 1e-6
LANES = 128

DN_HEADS = 4
DN_DK = 128
DN_CHUNK = 64
DN_CONV = 4

DA_HEADS = 4
DA_DH = 64

MX_HEADS = 4
MX_DH = 128

REL_BUCKETS = 32
REL_MAX_DIST = 128

N_EXPERTS = 32
TOP_K = 4
D_FF = 1024
SWIGLU_LIMIT = 7.0
SWIGLU_ALPHA = 1.702
EXPERT_BLOCK = 512
MOE_TILE = 256
GROUP_ALIGN = 8

LAM_INIT = 0.8 - 0.6 * math.exp(-0.3 * 0)
LOG2E = 1.4426950408889634
NEG = -1e30

P_DNQ, P_DNK, P_DNV, P_DNZ = 0, 512, 1024, 1536
P_DAQ, P_DAK, P_DAV = 2048, 2560, 3072
P_MXQ = 3584
P_GATE = 4096
P_COLS = 7168
W_AB_LO, W_AB_HI = 2048, 2056


def _dot(a, b):
    return jnp.dot(a, b, preferred_element_type=F32)


def _dot_nt(a, b):
    return lax.dot_general(a, b, (((1,), (1,)), ((), ())), preferred_element_type=F32)


def _dot_tn(a, b):
    return lax.dot_general(a, b, (((0,), (0,)), ((), ())), preferred_element_type=F32)


def _split(x):
    hi = x.astype(BF16)
    lo = (x - hi.astype(F32)).astype(BF16)
    return hi, lo


def _dot3(a, b):
    ah, al = _split(a)
    bh, bl = _split(b)
    return _dot(ah, bh) + _dot(ah, bl) + _dot(al, bh)


def _sigmoid(x):
    return 1.0 / (1.0 + jnp.exp(-x))


def _rms(x, n):
    return lax.rsqrt(jnp.sum(x * x, axis=-1, keepdims=True) * (1.0 / n) + EPS)


def _inproj_body(x_ref, g_ref, w_ref, wab_ref, p_ref, ab_ref, h_scr):
    @pl.when(pl.program_id(1) == 0)
    def _():
        x = x_ref[...]
        h = x * _rms(x, D_MODEL) * g_ref[...]
        h_scr[...] = h.astype(BF16)
        ab_ref[...] = _dot3(h, wab_ref[...])

    p_ref[...] = _dot(h_scr[...], w_ref[...]).astype(p_ref.dtype)


def _inproj(x2, gain, wp, wab):
    t = x2.shape[0]
    tm = min(2048, t)
    tn = 1024
    return pl.pallas_call(
        _inproj_body,
        grid=(t // tm, P_COLS // tn),
        in_specs=[
            pl.BlockSpec((tm, D_MODEL), lambda i, j: (i, 0)),
            pl.BlockSpec((1, D_MODEL), lambda i, j: (0, 0)),
            pl.BlockSpec((D_MODEL, tn), lambda i, j: (0, j)),
            pl.BlockSpec((D_MODEL, LANES), lambda i, j: (0, 0)),
        ],
        out_specs=[
            pl.BlockSpec((tm, tn), lambda i, j: (i, j)),
            pl.BlockSpec((tm, LANES), lambda i, j: (i, 0)),
        ],
        out_shape=[
            jax.ShapeDtypeStruct((t, P_COLS), BF16),
            jax.ShapeDtypeStruct((t, LANES), F32),
        ],
        scratch_shapes=[pltpu.VMEM((tm, D_MODEL), BF16)],
        compiler_params=pltpu.CompilerParams(dimension_semantics=("parallel", "arbitrary")),
        name="inproj",
    )(x2, gain, wp, wab)


DN_HALO = 16
DN_SCAN_CHUNK = 256


def _deltanet_body(q_ref, k_ref, v_ref, z_ref, qh_ref, kh_ref, vh_ref, ab_ref, cw_ref, alog_ref,
                   dtb_ref, on_ref, o_ref, stage, qs, ks, vs, s_scr):
    i = pl.program_id(1)
    tc = q_ref.shape[0]
    hw = DN_HEADS * DN_DK

    @pl.when(i == 0)
    def _():
        s_scr[...] = jnp.zeros_like(s_scr)

    for src, halo, dst, off, kind in ((q_ref, qh_ref, qs, 0, "q"), (k_ref, kh_ref, ks, hw, "k"),
                                      (v_ref, vh_ref, vs, 2 * hw, "v")):
        hal = halo[...].astype(F32)
        stage[0:DN_HALO, :] = jnp.where(i == 0, 0.0, hal)
        stage[DN_HALO:DN_HALO + tc, :] = src[...].astype(F32)
        base = DN_HALO - (DN_CONV - 1)
        y = stage[base:base + tc, :] * cw_ref[0:1, off:off + hw]
        for j in range(1, DN_CONV):
            y = y + stage[base + j:base + j + tc, :] * cw_ref[j:j + 1, off:off + hw]
        y = y * _sigmoid(y)
        if kind == "v":
            dst[...] = y
        else:
            for h in range(DN_HEADS):
                sl = slice(h * DN_DK, (h + 1) * DN_DK)
                yh = y[:, sl]
                r = lax.rsqrt(jnp.sum(yh * yh, axis=-1, keepdims=True) + EPS)
                if kind == "q":
                    r = r * (DN_DK ** -0.5)
                dst[:, sl] = yh * r

    c = min(DN_SCAN_CHUNK, tc)
    row = lax.broadcasted_iota(I32, (c, c), 0)
    col = lax.broadcasted_iota(I32, (c, c), 1)
    incl = row >= col
    strict = row > col
    same_blk = (row // DN_CHUNK) == (col // DN_CHUNK)
    tri = jnp.where(incl, 1.0, 0.0).astype(BF16)
    eye = jnp.where(row == col, 1.0, 0.0)
    neg_a = -jnp.exp(alog_ref[...])
    dtb = dtb_ref[...]

    def chunk(ci, carry):
        r0 = pl.multiple_of(ci * c, c)
        abc = ab_ref[pl.ds(r0, c), :]
        a_in = abc + dtb
        g_all = neg_a * (jnp.maximum(a_in, 0.0) + jnp.log(1.0 + jnp.exp(-jnp.abs(a_in))))
        beta_all = _sigmoid(abc)
        zc = z_ref[pl.ds(r0, c), :].astype(F32)
        hs = range(DN_HEADS)
        sls = [slice(h * DN_DK, (h + 1) * DN_DK) for h in hs]
        q = [qs[pl.ds(r0, c), sl] for sl in sls]
        k = [ks[pl.ds(r0, c), sl] for sl in sls]
        v = [vs[pl.ds(r0, c), sl] for sl in sls]
        beta = [beta_all[:, DN_HEADS + h:DN_HEADS + h + 1] for h in hs]
        g_hi, g_lo = _split(g_all)
        gc_all = _dot(tri, g_hi) + _dot(tri, g_lo)
        gc_rows = gc_all.T
        kb = [k[h].astype(BF16) for h in hs]
        qkk = [_dot_nt(jnp.concatenate([q[h].astype(BF16), kb[h]], axis=0), kb[h]) for h in hs]
        gc = [gc_all[:, h:h + 1] for h in hs]
        decay = [jnp.where(incl, jnp.exp(gc[h] - gc_rows[h:h + 1, :]), 0.0) for h in hs]
        lower = [jnp.where(strict, qkk[h][c:] * decay[h] * beta[h], 0.0) for h in hs]
        pw = [jnp.where(same_blk, -lower[h], 0.0) for h in hs]
        dinv = [eye + pw[h] for h in hs]
        pwb = [pw[h].astype(BF16) for h in hs]
        for _ in range(int(math.log2(DN_CHUNK)) - 1):
            pwb = [_dot(pwb[h], pwb[h]).astype(BF16) for h in hs]
            dinv = [dinv[h] + _dot(dinv[h].astype(BF16), pwb[h]) for h in hs]
        dinv_b = [dinv[h].astype(BF16) for h in hs]
        pw = [-_dot(dinv_b[h], jnp.where(same_blk, 0.0, lower[h]).astype(BF16)) for h in hs]
        xm = [eye + pw[h] for h in hs]
        pwb = [pw[h].astype(BF16) for h in hs]
        for _ in range(int(math.log2(c // DN_CHUNK)) - 1):
            pwb = [_dot(pwb[h], pwb[h]).astype(BF16) for h in hs]
            xm = [xm[h] + _dot(xm[h].astype(BF16), pwb[h]) for h in hs]
        inv = [_dot(xm[h].astype(BF16), dinv_b[h]).astype(BF16) for h in hs]
        egc = [jnp.exp(gc[h]) for h in hs]
        rhs = [jnp.concatenate([v[h] * beta[h], k[h] * (beta[h] * egc[h])], axis=1).astype(BF16) for h in hs]
        sol = [_dot(inv[h], rhs[h]) for h in hs]
        qkm = [jnp.where(incl, qkk[h][:c] * decay[h], 0.0).astype(BF16) for h in hs]
        gl = [gc[h][c - 1:c, :] for h in hs]
        state = [s_scr[h] for h in hs]
        ws = [_dot(jnp.concatenate([sol[h][:, DN_DK:].astype(BF16), (q[h] * egc[h]).astype(BF16)], axis=0),
                   state[h].astype(BF16)) for h in hs]
        v_new = [sol[h][:, :DN_DK] - ws[h][:c] for h in hs]
        o = [ws[h][c:] + _dot(qkm[h], v_new[h].astype(BF16)) for h in hs]
        for h in hs:
            s_scr[h] = state[h] * jnp.exp(gl[h]) + _dot_tn(kb[h], (v_new[h] * jnp.exp(gl[h] - gc[h])).astype(BF16))
        for h in hs:
            zz = zc[:, sls[h]]
            on = o[h] * _rms(o[h], DN_DK) * on_ref[...]
            o_ref[pl.ds(r0, c), sls[h]] = (on * (zz * _sigmoid(zz))).astype(o_ref.dtype)
        return carry

    lax.fori_loop(0, tc // c, chunk, 0, unroll=True)


def _deltanet(p, ab, conv_w, alog_row, dtb_row, out_norm, batch, seq):
    t = batch * seq
    tc = min(1024, seq)
    nt = seq // tc
    hw = DN_HEADS * DN_DK

    def main(cb):
        return pl.BlockSpec((tc, hw), lambda b, i: (b * nt + i, cb))

    def halo(cb):
        return pl.BlockSpec(
            (DN_HALO, hw),
            lambda b, i: (jnp.maximum((b * seq + i * tc) // DN_HALO - 1, 0), cb))

    return pl.pallas_call(
        _deltanet_body,
        grid=(batch, nt),
        in_specs=[
            main(P_DNQ // hw), main(P_DNK // hw), main(P_DNV // hw), main(P_DNZ // hw),
            halo(P_DNQ // hw), halo(P_DNK // hw), halo(P_DNV // hw),
            pl.BlockSpec((tc, LANES), lambda b, i: (b * nt + i, 0)),
            pl.BlockSpec((DN_CONV, 3 * hw), lambda b, i: (0, 0)),
            pl.BlockSpec((1, LANES), lambda b, i: (0, 0)),
            pl.BlockSpec((1, LANES), lambda b, i: (0, 0)),
            pl.BlockSpec((1, DN_DK), lambda b, i: (0, 0)),
        ],
        out_specs=pl.BlockSpec((tc, hw), lambda b, i: (b * nt + i, 0)),
        out_shape=jax.ShapeDtypeStruct((t, hw), BF16),
        scratch_shapes=[
            pltpu.VMEM((DN_HALO + tc, hw), F32),
            pltpu.VMEM((tc, hw), F32),
            pltpu.VMEM((tc, hw), F32),
            pltpu.VMEM((tc, hw), F32),
            pltpu.VMEM((DN_HEADS, DN_DK, DN_DK), F32),
        ],
        compiler_params=pltpu.CompilerParams(dimension_semantics=("parallel", "arbitrary")),
        name="deltanet",
    )(p, p, p, p, p, p, p, ab, conv_w, alog_row, dtb_row, out_norm)


ATT_BLOCK = 512


def _bias_body(tbl_ref, o_ref):
    h = pl.program_id(0)
    tq = o_ref.shape[2]
    key = lax.broadcasted_iota(I32, (tq, tq), 0)
    qry = lax.broadcasted_iota(I32, (tq, tq), 1)
    max_exact = REL_BUCKETS // 2
    far = tbl_ref[h, REL_BUCKETS - 1]
    for d in range(2):
        n = qry - key + d * tq
        nn = jnp.maximum(n, 0)
        nf = jnp.maximum(nn, 1).astype(F32)
        large = max_exact + (jnp.log(nf / max_exact) / math.log(REL_MAX_DIST / max_exact)
                             * (REL_BUCKETS - max_exact)).astype(I32)
        large = jnp.minimum(large, REL_BUCKETS - 1)
        bucket = jnp.where(nn < max_exact, nn, large)
        val = jnp.zeros((tq, tq), F32)
        for b in range(REL_BUCKETS):
            val = jnp.where(bucket == b, tbl_ref[h, b], val)
        o_ref[0, d] = jnp.where(n >= 0, (val - far) * LOG2E, NEG)


def _bias_tiles(tbl_t, tq):
    return pl.pallas_call(
        _bias_body,
        grid=(DA_HEADS,),
        in_specs=[pl.BlockSpec(memory_space=pltpu.SMEM)],
        out_specs=pl.BlockSpec((1, 2, tq, tq), lambda h: (h, 0, 0, 0)),
        out_shape=jax.ShapeDtypeStruct((DA_HEADS, 2, tq, tq), F32),
        name="t5_bias_tiles",
    )(tbl_t)


DA_DV = 2 * DA_DH
DA_VROWS = DA_DV + 16


BOUND_SLACK = 1.02
MAX_SHIFT_GAP = 110.0


def _attn_body(q_ref, qn_ref, k_ref, v_ref, bias_ref, qg_ref, kg_ref, lam_ref, sg_ref, o_ref,
               kn, vt, kst, qc_s, flag_s, m_s, acc_s):
    qi = pl.program_id(2)
    slot = qi % 2
    tq = q_ref.shape[0]
    seq = k_ref.shape[0]
    tk = tq
    lo_mask = lax.broadcasted_iota(I32, (1, DA_DV), 1) < DA_DH

    def group_norm(x, gain):
        x2 = x * x
        lo = jnp.sum(jnp.where(lo_mask, x2, 0.0), axis=-1, keepdims=True)
        hi = jnp.sum(jnp.where(lo_mask, 0.0, x2), axis=-1, keepdims=True)
        r = jnp.where(lo_mask, lax.rsqrt(lo * (1.0 / DA_DH) + EPS), lax.rsqrt(hi * (1.0 / DA_DH) + EPS))
        return x * r * gain

    @pl.when(qi == 0)
    def _():
        ones = jnp.ones((DA_VROWS - DA_DV, tk), BF16)

        def body(c, carry):
            r0 = pl.multiple_of(c * tk, tk)
            kn[pl.ds(r0, tk), :] = group_norm(k_ref[pl.ds(r0, tk), :].astype(F32), kg_ref[...]).astype(BF16)
            vt[c, 0:DA_DV, :] = v_ref[pl.ds(r0, tk), :].astype(F32).T.astype(BF16)
            vt[c, DA_DV:DA_VROWS, :] = ones
            return carry
        lax.fori_loop(0, seq // tk, body, 0, unroll=2)

        g2 = (qg_ref[...] * kg_ref[...]) ** 2
        g_lo = jnp.max(jnp.where(lo_mask, g2, 0.0), axis=1, keepdims=True)
        g_hi = jnp.max(jnp.where(lo_mask, 0.0, g2), axis=1, keepdims=True)
        bound = jnp.sqrt(jnp.where(lo_mask, g_lo, g_hi)) * (DA_DH * DA_DH ** -0.5 * LOG2E * BOUND_SLACK)
        b0 = bias_ref[0, 0]
        b1 = bias_ref[0, 1]
        bmax = jnp.maximum(jnp.max(jnp.maximum(b0, b1), axis=0, keepdims=True), 0.0)
        bmin = jnp.minimum(jnp.min(jnp.minimum(jnp.where(b0 > 0.5 * NEG, b0, 0.0), b1), axis=0, keepdims=True), 0.0)
        bmax = jnp.max(bmax, axis=1, keepdims=True)
        bmin = jnp.min(bmin, axis=1, keepdims=True)
        kst[0:1, :] = bound + bmax
        worst = 2.0 * jnp.max(bound, axis=1, keepdims=True) + bmax - bmin
        flag_s[0] = jnp.where(worst[0, 0] <= MAX_SHIFT_GAP, 1, 0).astype(I32)

    def prepare(src_ref, s):
        q = group_norm(src_ref[...].astype(F32), qg_ref[...]) * (DA_DH ** -0.5 * LOG2E)
        qc_s[s] = jnp.concatenate([jnp.where(lo_mask, q, 0.0), jnp.where(lo_mask, 0.0, q)], axis=0).astype(BF16)

    @pl.when(qi == 0)
    def _():
        prepare(q_ref, slot)

    qcat = qc_s[slot]
    safe = flag_s[0] == 1
    acc_s[...] = jnp.zeros_like(acc_s)

    def block(j, d):
        r0 = pl.multiple_of(j * tk, tk)
        st = _dot_nt(kn[pl.ds(r0, tk), :], qcat)
        if d is not None:
            bias = bias_ref[0, d]
            st = st + jnp.concatenate([bias, bias], axis=1)
        m_prev = m_s[...]
        m_new = jnp.maximum(m_prev, jnp.max(st, axis=0, keepdims=True))
        alpha = jnp.exp2(m_prev - m_new)
        acc_s[...] = alpha * acc_s[...] + _dot(vt[j], jnp.exp2(st - m_new).astype(BF16))
        m_s[...] = m_new

    def blocks_fixed(js, ds=None):
        sts = [_dot_nt(kn[pl.ds(pl.multiple_of(j * tk, tk), tk), :], qcat) for j in js]
        if ds is not None:
            biases = [bias_ref[0, d] for d in ds]
            sts = [st + jnp.concatenate([b, b], axis=1) for st, b in zip(sts, biases)]
        shift = m_s[...]
        pts = [jnp.exp2(st - shift).astype(BF16) for st in sts]
        tot = _dot(vt[js[0]], pts[0])
        for j, pt in zip(js[1:], pts[1:]):
            tot = tot + _dot(vt[j], pt)
        acc_s[...] = acc_s[...] + tot

    def run(fixed_shift):
        n_far = jnp.maximum(qi - 1, 0)

        if fixed_shift:
            def far_quad(jj, carry):
                blocks_fixed([4 * jj + u for u in range(4)])
                return carry

            lax.fori_loop(0, n_far // 4, far_quad, 0)
            rem = n_far % 4

            @pl.when(rem >= 2)
            def _():
                blocks_fixed([n_far - rem, n_far - rem + 1])

            @pl.when(rem % 2 == 1)
            def _():
                blocks_fixed([n_far - 1])

            @pl.when(qi >= 1)
            def _():
                blocks_fixed([qi - 1, qi], [1, 0])
                prepare(qn_ref, 1 - slot)

            @pl.when(qi == 0)
            def _():
                blocks_fixed([qi], [0])
                prepare(qn_ref, 1 - slot)
        else:
            def far_one(j, carry):
                block(j, None)
                return carry

            lax.fori_loop(0, n_far, far_one, 0)

            @pl.when(qi >= 1)
            def _():
                block(qi - 1, 1)

            block(qi, 0)
            prepare(qn_ref, 1 - slot)

    @pl.when(safe)
    def _():
        m_s[...] = jnp.concatenate([jnp.broadcast_to(kst[0:1, 0:1], (1, tq)),
                                    jnp.broadcast_to(kst[0:1, DA_DH:DA_DH + 1], (1, tq))], axis=1)
        run(True)

    @pl.when(jnp.logical_not(safe))
    def _():
        m_s[...] = jnp.full(m_s.shape, NEG, F32)
        run(False)

    lam_p = lam_ref[...]
    lam = (jnp.exp(jnp.sum(lam_p[0:1, :] * lam_p[1:2, :], axis=-1, keepdims=True))
           - jnp.exp(jnp.sum(lam_p[2:3, :] * lam_p[3:4, :], axis=-1, keepdims=True)) + LAM_INIT)
    a0 = acc_s[:, 0:tq]
    a1 = acc_s[:, tq:2 * tq]
    ot = a0[0:DA_DV] / a0[DA_DV:DA_DV + 1] - lam * (a1[0:DA_DV] / a1[DA_DV:DA_DV + 1])
    r = lax.rsqrt(jnp.sum(ot * ot, axis=0, keepdims=True) * (1.0 / DA_DV) + EPS)
    ot = ot * r * (sg_ref[...] * (1.0 - LAM_INIT))
    o_ref[...] = ot.T.astype(o_ref.dtype)


def _attention(p, bias, qg, kg, lam_p, subln, batch, seq):
    t = batch * seq
    tq = min(ATT_BLOCK, seq)
    nq = seq // tq
    dv = DA_DV
    return pl.pallas_call(
        _attn_body,
        grid=(batch, DA_HEADS, nq),
        in_specs=[
            pl.BlockSpec((tq, dv), lambda b, h, i: (b * nq + i, P_DAQ // dv + h)),
            pl.BlockSpec((tq, dv), lambda b, h, i: (b * nq + jnp.minimum(i + 1, nq - 1), P_DAQ // dv + h)),
            pl.BlockSpec((seq, dv), lambda b, h, i: (b, P_DAK // dv + h)),
            pl.BlockSpec((seq, dv), lambda b, h, i: (b, P_DAV // dv + h)),
            pl.BlockSpec((1, 2, tq, tq), lambda b, h, i: (h, 0, 0, 0)),
            pl.BlockSpec((1, dv), lambda b, h, i: (0, 0)),
            pl.BlockSpec((1, dv), lambda b, h, i: (0, 0)),
            pl.BlockSpec((4, DA_DH), lambda b, h, i: (0, 0)),
            pl.BlockSpec((dv, 1), lambda b, h, i: (0, 0)),
        ],
        out_specs=pl.BlockSpec((tq, dv), lambda b, h, i: (b * nq + i, h)),
        out_shape=jax.ShapeDtypeStruct((t, DA_HEADS * dv), BF16),
        scratch_shapes=[
            pltpu.VMEM((seq, dv), BF16),
            pltpu.VMEM((seq // tq, DA_VROWS, tq), BF16),
            pltpu.VMEM((8, dv), F32),
            pltpu.VMEM((2, 2 * tq, dv), BF16),
            pltpu.SMEM((1,), I32),
            pltpu.VMEM((1, 2 * tq), F32),
            pltpu.VMEM((DA_VROWS, 2 * tq), F32),
        ],
        compiler_params=pltpu.CompilerParams(dimension_semantics=("parallel", "parallel", "arbitrary")),
        name="diff_attention",
    )(p, p, p, p, bias, qg, kg, lam_p, subln)


def _memkv_body(mem_ref, mg_ref, w_ref, kg_ref, mk_ref, mv_ref):
    x = mem_ref[0]
    xn = x * _rms(x, D_MODEL) * mg_ref[...]
    kv = _dot(xn.astype(BF16), w_ref[...])
    hw = MX_HEADS * MX_DH
    for h in range(MX_HEADS):
        sl = slice(h * MX_DH, (h + 1) * MX_DH)
        kh = kv[:, sl]
        mk_ref[0, :, sl] = (kh * _rms(kh, MX_DH) * kg_ref[...]).astype(BF16)
    mv_ref[0] = kv[:, hw:].astype(BF16)


def _memkv(mem, mem_norm, w_kv, k_norm):
    b, n, _ = mem.shape
    hw = MX_HEADS * MX_DH
    return pl.pallas_call(
        _memkv_body,
        grid=(b,),
        in_specs=[
            pl.BlockSpec((1, n, D_MODEL), lambda i: (i, 0, 0)),
            pl.BlockSpec((1, D_MODEL), lambda i: (0, 0)),
            pl.BlockSpec((D_MODEL, 2 * hw), lambda i: (0, 0)),
            pl.BlockSpec((1, MX_DH), lambda i: (0, 0)),
        ],
        out_specs=[pl.BlockSpec((1, n, hw), lambda i: (i, 0, 0))] * 2,
        out_shape=[jax.ShapeDtypeStruct((b, n, hw), BF16)] * 2,
        name="memory_kv",
    )(mem, mem_norm, w_kv, k_norm)


def _merge_body(x_ref, odn_ref, oda_ref, mxq_ref, g0_ref, g1_ref, g2_ref, mk_ref, mv_ref, qg_ref,
                bg_ref, wb_ref, wo_ref, out_ref, omx):
    for h in range(MX_HEADS):
        sl = slice(h * MX_DH, (h + 1) * MX_DH)
        qh = mxq_ref[:, sl].astype(F32)
        qh = qh * _rms(qh, MX_DH) * qg_ref[...] * (MX_DH ** -0.5 * LOG2E)
        s = _dot_nt(qh.astype(BF16), mk_ref[0, :, sl])
        p = jnp.exp2(s - jnp.max(s, axis=-1, keepdims=True))
        oh = _dot(p.astype(BF16), mv_ref[0, :, sl]) / jnp.sum(p, axis=-1, keepdims=True)
        omx[:, sl] = oh.astype(BF16)
    y = None
    for r, (o_r, g_r) in enumerate(((odn_ref, g0_ref), (oda_ref, g1_ref), (omx, g2_ref))):
        gate = _sigmoid(g_r[...].astype(F32) + bg_ref[r:r + 1, :])
        term = gate * _dot(o_r[...], wb_ref[r])
        y = term if y is None else y + term
    out_ref[...] = x_ref[...] + _dot(y.astype(BF16), wo_ref[...])


def _merge(x2, o_dn, o_da, p, mk, mv, q_norm, b_gate, w_branch, w_out, seq):
    t = x2.shape[0]
    tm = min(1024, seq)
    nt = seq // tm
    bw = 512
    n_mem = mk.shape[1]
    return pl.pallas_call(
        _merge_body,
        grid=(t // tm,),
        in_specs=[
            pl.BlockSpec((tm, D_MODEL), lambda i: (i, 0)),
            pl.BlockSpec((tm, bw), lambda i: (i, 0)),
            pl.BlockSpec((tm, bw), lambda i: (i, 0)),
            pl.BlockSpec((tm, bw), lambda i: (i, P_MXQ // bw)),
            pl.BlockSpec((tm, D_MODEL), lambda i: (i, P_GATE // D_MODEL)),
            pl.BlockSpec((tm, D_MODEL), lambda i: (i, P_GATE // D_MODEL + 1)),
            pl.BlockSpec((tm, D_MODEL), lambda i: (i, P_GATE // D_MODEL + 2)),
            pl.BlockSpec((1, n_mem, bw), lambda i: (i // nt, 0, 0)),
            pl.BlockSpec((1, n_mem, bw), lambda i: (i // nt, 0, 0)),
            pl.BlockSpec((1, MX_DH), lambda i: (0, 0)),
            pl.BlockSpec((3, D_MODEL), lambda i: (0, 0)),
            pl.BlockSpec((3, bw, D_MODEL), lambda i: (0, 0, 0)),
            pl.BlockSpec((D_MODEL, D_MODEL), lambda i: (0, 0)),
        ],
        out_specs=pl.BlockSpec((tm, D_MODEL), lambda i: (i, 0)),
        out_shape=jax.ShapeDtypeStruct((t, D_MODEL), F32),
        scratch_shapes=[pltpu.VMEM((tm, bw), BF16)],
        compiler_params=pltpu.CompilerParams(dimension_semantics=("parallel",)),
        name="merge",
    )(x2, o_dn, o_da, p, p, p, p, mk, mv, q_norm, b_gate, w_branch, w_out)


def _router_body(x_ref, g_ref, wr_ref, br_ref, h_ref, idx_ref, wt_ref, rank_ref, cnt_ref):
    tm = x_ref.shape[0]
    x = x_ref[...]
    h = x * _rms(x, D_MODEL) * g_ref[...]
    h_ref[...] = h.astype(BF16)
    logits = _dot3(h, wr_ref[...]) + br_ref[...]
    lane = lax.broadcasted_iota(I32, (tm, LANES), 1)
    lane_f = lane.astype(F32)
    work = logits
    sel = jnp.zeros((tm, LANES), F32)
    vals, idxs = [], []
    for _ in range(TOP_K):
        mx = jnp.max(work, axis=-1, keepdims=True)
        ik = jnp.min(jnp.where(work == mx, lane_f, float(LANES)), axis=-1, keepdims=True)
        hit = lane_f == ik
        sel = jnp.where(hit, 1.0, sel)
        work = jnp.where(hit, -jnp.inf, work)
        vals.append(mx)
        idxs.append(ik)
    es = [jnp.exp(v - vals[0]) for v in vals]
    den = es[0] + es[1] + es[2] + es[3]
    sub = cnt_ref.shape[0]
    mt = tm // sub
    r = lax.broadcasted_iota(I32, (tm, tm), 0)
    c = lax.broadcasted_iota(I32, (tm, tm), 1)
    tril = jnp.where((r > c) & (r // mt == c // mt), 1.0, 0.0).astype(BF16)
    cum = _dot(tril, sel.astype(BF16))
    idx_o = jnp.zeros((tm, LANES), F32)
    wt_o = jnp.zeros((tm, LANES), F32)
    rank_o = jnp.zeros((tm, LANES), F32)
    for k in range(TOP_K):
        rk = jnp.sum(jnp.where(lane_f == idxs[k], cum, 0.0), axis=-1, keepdims=True)
        idx_o = jnp.where(lane == k, idxs[k], idx_o)
        wt_o = jnp.where(lane == k, es[k] / den, wt_o)
        rank_o = jnp.where(lane == k, rk, rank_o)
    idx_ref[...] = idx_o.astype(I32)
    wt_ref[...] = wt_o
    rank_ref[...] = rank_o.astype(I32)
    for s in range(sub):
        cnt_ref[s] = jnp.broadcast_to(jnp.sum(sel[s * mt:(s + 1) * mt], axis=0, keepdims=True), (8, LANES))


def _router(x1, gain, w_r, b_r):
    t = x1.shape[0]
    mt = min(MOE_TILE, t)
    sub = 2 if t % (2 * mt) == 0 else 1
    tm = sub * mt
    row = lambda i: (i, 0)
    fixed = lambda i: (0, 0)
    return pl.pallas_call(
        _router_body,
        grid=(t // tm,),
        in_specs=[
            pl.BlockSpec((tm, D_MODEL), row),
            pl.BlockSpec((1, D_MODEL), fixed),
            pl.BlockSpec((D_MODEL, LANES), fixed),
            pl.BlockSpec((1, LANES), fixed),
        ],
        out_specs=[
            pl.BlockSpec((tm, D_MODEL), row),
            pl.BlockSpec((tm, LANES), row),
            pl.BlockSpec((tm, LANES), row),
            pl.BlockSpec((tm, LANES), row),
            pl.BlockSpec((sub, 8, LANES), lambda i: (i, 0, 0)),
        ],
        out_shape=[
            jax.ShapeDtypeStruct((t, D_MODEL), BF16),
            jax.ShapeDtypeStruct((t, LANES), I32),
            jax.ShapeDtypeStruct((t, LANES), F32),
            jax.ShapeDtypeStruct((t, LANES), I32),
            jax.ShapeDtypeStruct((t // mt, 8, LANES), F32),
        ],
        compiler_params=pltpu.CompilerParams(dimension_semantics=("arbitrary",)),
        name="router",
    )(x1, gain, w_r, b_r)


def _lane_cumsum(x):
    lane = lax.broadcasted_iota(I32, x.shape, 1)
    s = 1
    while s < N_EXPERTS:
        x = x + jnp.where(lane >= s, pltpu.roll(x, s, axis=1), 0.0)
        s *= 2
    return x


def _plan_body(cnt_ref, gs_ref, ls_ref, n8_ref, lsf_ref, tail_ref, meta_ref):
    nt = cnt_ref.shape[0]
    ga = float(GROUP_ALIGN)
    eb = float(EXPERT_BLOCK)
    lane = lax.broadcasted_iota(I32, (nt, LANES), 1)
    r8 = jnp.where(lane < N_EXPERTS, jnp.floor((cnt_ref[...] + (ga - 1.0)) * (1.0 / ga)) * ga, 0.0)
    ri = lax.broadcasted_iota(I32, (nt, nt), 0)
    ci = lax.broadcasted_iota(I32, (nt, nt), 1)
    before = _dot(jnp.where(ri > ci, 1.0, 0.0).astype(BF16), r8.astype(BF16))
    tot = jnp.sum(r8, axis=0, keepdims=True)
    region = jnp.floor((tot + (eb - 1.0)) * (1.0 / eb)) * eb
    pends = _lane_cumsum(jnp.broadcast_to(region, (8, LANES)))[0:1, :]
    pstart = pends - region
    lstart = _lane_cumsum(r8) - r8
    gs_ref[...] = (pstart + before).astype(I32)
    ls_ref[...] = lstart.astype(I32)
    pieces = r8 * (1.0 / ga)
    n8_ref[...] = jnp.where(lane == TOTAL_LANE, jnp.sum(pieces, axis=1, keepdims=True), pieces).astype(I32)
    lsf_ref[...] = lstart
    row8 = lax.broadcasted_iota(I32, (8, LANES), 0)
    tail_n = (region - tot) * (1.0 / ga)
    lane8 = lax.broadcasted_iota(I32, (8, LANES), 1)
    tail_n = jnp.where(lane8 == TOTAL_LANE, jnp.sum(tail_n, axis=1, keepdims=True), tail_n)
    used_blocks = jnp.sum(jnp.where(lane8 == N_EXPERTS - 1, pends, 0.0), axis=1, keepdims=True) * (1.0 / eb)
    tail = jnp.where(row8 == 0, pstart + tot, jnp.where(row8 == 1, tail_n, jnp.where(row8 == 2, used_blocks, 0.0)))
    tail_ref[...] = tail.astype(I32)
    nb = meta_ref.shape[0]
    ln = lax.broadcasted_iota(I32, (nb, LANES), 1)
    blk = lax.broadcasted_iota(I32, (nb, LANES), 0).astype(F32) * eb
    be = jnp.sum(jnp.where((ln < N_EXPERTS) & (pends <= blk), 1.0, 0.0), axis=-1, keepdims=True)
    be = jnp.minimum(be, float(N_EXPERTS - 1))
    used = jnp.sum(jnp.where(ln == N_EXPERTS - 1, pends, 0.0), axis=-1, keepdims=True) * (1.0 / eb)
    meta_ref[...] = jnp.where(ln == 0, be, jnp.where(ln == 1, used, 0.0)).astype(I32)


def _plan(cnt, n_blocks_pad):
    nt = cnt.shape[0]
    shp = jax.ShapeDtypeStruct((nt, LANES), I32)
    return pl.pallas_call(
        _plan_body,
        out_shape=[shp, shp, shp, jax.ShapeDtypeStruct((nt, LANES), F32),
                   jax.ShapeDtypeStruct((8, LANES), I32),
                   jax.ShapeDtypeStruct((n_blocks_pad, LANES), I32)],
        name="dispatch_plan",
    )(cnt)


def _local_positions(idx_ref, rank_ref, lsf_ref):
    tm = idx_ref.shape[0]
    lane = lax.broadcasted_iota(I32, (tm, LANES), 1)
    idx = idx_ref[...]
    rank = rank_ref[...].astype(F32)
    ls_row = lsf_ref[0]
    out = []
    for k in range(TOP_K):
        base = jnp.sum(jnp.where(lane == idx[:, k:k + 1], ls_row, 0.0), axis=-1, keepdims=True)
        out.append((base + rank[:, k:k + 1]).astype(I32))
    return out


def _local_positions_lanes(idx_ref, rank_ref, lsf_ref):
    tm = idx_ref.shape[0]
    idx_t = idx_ref[...].astype(F32).T
    rank_t = rank_ref[...].astype(F32).T
    ls_col = jnp.broadcast_to(lsf_ref[0], (LANES, LANES)).T[:, 0:1]
    expert = lax.broadcasted_iota(I32, (LANES, tm), 0).astype(F32)
    out = []
    for k in range(TOP_K):
        base = jnp.sum(jnp.where(expert == idx_t[k:k + 1, :], ls_col, 0.0), axis=0, keepdims=True)
        out.append((base + rank_t[k:k + 1, :]).astype(I32))
    return out


PACK_W = D_MODEL // 2
U32 = jnp.uint32


def _pack_rows(x):
    xb = x.astype(BF16).astype(F32)
    hi = lax.bitcast_convert_type(xb[:, :PACK_W], U32)
    lo = lax.bitcast_convert_type(xb[:, PACK_W:], U32)
    return hi | (lo >> 16)


def _unpack_rows(w):
    hi = lax.bitcast_convert_type(w & jnp.uint32(0xFFFF0000), F32)
    lo = lax.bitcast_convert_type(w << 16, F32)
    return hi.astype(BF16), lo.astype(BF16)


TOTAL_LANE = LANES - 1
WAIT_CHUNK = 128
BIG_PIECE = 4


def _start_group_copies(tables, tile, make_copy):
    gs_ref, ls_ref, n8_ref = tables

    def group(e, carry):
        g0 = gs_ref[tile, e]
        l0 = ls_ref[tile, e]
        n = n8_ref[tile, e]
        nbig = n // BIG_PIECE

        def big(j, c):
            off = j * (BIG_PIECE * GROUP_ALIGN)
            make_copy(pl.multiple_of(l0 + off, GROUP_ALIGN), pl.multiple_of(g0 + off, GROUP_ALIGN),
                      BIG_PIECE * GROUP_ALIGN).start()
            return c

        def one(j, c):
            off = j * GROUP_ALIGN
            make_copy(pl.multiple_of(l0 + off, GROUP_ALIGN), pl.multiple_of(g0 + off, GROUP_ALIGN),
                      GROUP_ALIGN).start()
            return c

        lax.fori_loop(0, nbig, big, 0)
        lax.fori_loop(nbig * BIG_PIECE, n, one, 0)
        return carry

    lax.fori_loop(0, N_EXPERTS, group, 0)


def _wait_pieces(n, make_wait):
    def chunk(j, c):
        make_wait(WAIT_CHUNK).wait()
        return c

    lax.fori_loop(0, n // WAIT_CHUNK, chunk, 0)
    b = WAIT_CHUNK // 2
    while b >= 1:
        def _(b=b):
            make_wait(b).wait()
        pl.when((n & b) != 0)(_)
        b //= 2


def _dispatch_body(gs_ref, ls_ref, n8_ref, tail_ref, h_ref, idx_ref, rank_ref, lsf_ref, xs_ref,
                   xl, zbuf, sem):
    i = pl.program_id(0)
    last = i == pl.num_programs(0) - 1
    slot = i % 2
    tables = (gs_ref, ls_ref, n8_ref)
    tm = h_ref.shape[0]
    lrows = xl.shape[1]
    pos = lax.broadcasted_iota(I32, (lrows, tm), 0)
    lp = _local_positions_lanes(idx_ref, rank_ref, lsf_ref)
    hit = pos == lp[0]
    for k in range(1, TOP_K):
        hit = hit | (pos == lp[k])
    xl[slot] = _pack_rows(_dot(jnp.where(hit, 1.0, 0.0).astype(BF16), h_ref[...]))

    def copy_from(s):
        def make_copy(l, g, rows):
            return pltpu.make_async_copy(xl.at[s, pl.ds(l, rows), :], xs_ref.at[pl.ds(g, rows), :], sem.at[s])
        return make_copy

    def wait_on(s):
        def make_wait(pieces):
            rows = pieces * GROUP_ALIGN
            return pltpu.make_async_copy(xl.at[0, pl.ds(0, rows), :], xs_ref.at[pl.ds(0, rows), :], sem.at[s])
        return make_wait

    @pl.when(i >= 1)
    def _():
        _wait_pieces(n8_ref[i - 1, TOTAL_LANE], wait_on(1 - slot))

    _start_group_copies(tables, i, copy_from(slot))

    def start_tails():
        def per_expert(e, carry):
            g0 = tail_ref[0, e]

            def one(j, c):
                pltpu.make_async_copy(
                    zbuf.at[pl.ds(0, GROUP_ALIGN), :],
                    xs_ref.at[pl.ds(pl.multiple_of(g0 + j * GROUP_ALIGN, GROUP_ALIGN), GROUP_ALIGN), :],
                    sem.at[2]).start()
                return c

            lax.fori_loop(0, tail_ref[1, e], one, 0)
            return carry

        lax.fori_loop(0, N_EXPERTS, per_expert, 0)

    def unused_block(j):
        return pltpu.make_async_copy(
            zbuf, xs_ref.at[pl.ds(pl.multiple_of(j * EXPERT_BLOCK, EXPERT_BLOCK), EXPERT_BLOCK), :], sem.at[2])

    @pl.when(last)
    def _():
        zbuf[...] = jnp.zeros_like(zbuf)
        start_tails()
        used = tail_ref[2, 0]
        n_blocks = xs_ref.shape[0] // EXPERT_BLOCK

        def start_fill(j, c):
            unused_block(j).start()
            return c

        def wait_fill(j, c):
            unused_block(j).wait()
            return c

        lax.fori_loop(used, n_blocks, start_fill, 0)
        _wait_pieces(n8_ref[i, TOTAL_LANE], wait_on(slot))
        _wait_pieces(tail_ref[1, TOTAL_LANE], wait_on(2))
        lax.fori_loop(used, n_blocks, wait_fill, 0)


def _dispatch(gs, ls, n8, tail, h2, idx, rank, lsf, n_slots):
    t = h2.shape[0]
    tm = min(MOE_TILE, t)
    nt = t // tm
    lrows = tm * TOP_K + N_EXPERTS * GROUP_ALIGN
    row = lambda i, *_: (i, 0)
    grid_spec = pltpu.PrefetchScalarGridSpec(
        num_scalar_prefetch=4,
        grid=(nt,),
        in_specs=[
            pl.BlockSpec((tm, D_MODEL), row),
            pl.BlockSpec((tm, LANES), row),
            pl.BlockSpec((tm, LANES), row),
            pl.BlockSpec((1, 1, LANES), lambda i, *_: (i, 0, 0)),
        ],
        out_specs=pl.BlockSpec(memory_space=pl.ANY),
        scratch_shapes=[pltpu.VMEM((2, lrows, PACK_W), U32), pltpu.VMEM((EXPERT_BLOCK, PACK_W), U32),
                        pltpu.SemaphoreType.DMA((3,))],
    )
    return pl.pallas_call(
        _dispatch_body,
        grid_spec=grid_spec,
        out_shape=jax.ShapeDtypeStruct((n_slots, PACK_W), U32),
        compiler_params=pltpu.CompilerParams(dimension_semantics=("arbitrary",)),
        name="moe_dispatch",
    )(gs, ls, n8, tail, h2, idx, rank, lsf)


def _expert_body(be_ref, nu_ref, x_ref, wgu_ref, bgu_ref, wd_ref, bd_ref, y_ref, wgu_b, wd_b):
    j = pl.program_id(0)

    @pl.when(j >= nu_ref[0])
    def _():
        y_ref[...] = jnp.zeros_like(y_ref)

    @pl.when(j < nu_ref[0])
    def _():
        @pl.when(jnp.logical_or(j == 0, be_ref[j] != be_ref[jnp.maximum(j - 1, 0)]))
        def _():
            wgu_b[...] = wgu_ref[0].astype(BF16)
            wd_b[...] = wd_ref[0].astype(BF16)

        gu = _dot(jnp.concatenate(_unpack_rows(x_ref[...]), axis=1), wgu_b[...]) + bgu_ref[0]
        gate = jnp.minimum(gu[:, :D_FF], SWIGLU_LIMIT)
        up = jnp.clip(gu[:, D_FF:], -SWIGLU_LIMIT, SWIGLU_LIMIT)
        act = (up + 1.0) * gate * _sigmoid(SWIGLU_ALPHA * gate)
        y_ref[...] = _pack_rows(_dot(act.astype(BF16), wd_b[...]) + bd_ref[0])


def _experts(block_e, n_used, xs, wgu, bgu, wd, bd):
    n_slots = xs.shape[0]
    nb = n_slots // EXPERT_BLOCK

    def blk(j, be, nu):
        return jnp.minimum(j, nu[0] - 1)

    grid_spec = pltpu.PrefetchScalarGridSpec(
        num_scalar_prefetch=2,
        grid=(nb,),
        in_specs=[
            pl.BlockSpec((EXPERT_BLOCK, PACK_W), lambda j, be, nu: (blk(j, be, nu), 0)),
            pl.BlockSpec((1, D_MODEL, 2 * D_FF), lambda j, be, nu: (be[blk(j, be, nu)], 0, 0)),
            pl.BlockSpec((1, 1, 2 * D_FF), lambda j, be, nu: (be[blk(j, be, nu)], 0, 0)),
            pl.BlockSpec((1, D_FF, D_MODEL), lambda j, be, nu: (be[blk(j, be, nu)], 0, 0)),
            pl.BlockSpec((1, 1, D_MODEL), lambda j, be, nu: (be[blk(j, be, nu)], 0, 0)),
        ],
        out_specs=pl.BlockSpec((EXPERT_BLOCK, PACK_W), lambda j, be, nu: (j, 0)),
        scratch_shapes=[pltpu.VMEM((D_MODEL, 2 * D_FF), BF16), pltpu.VMEM((D_FF, D_MODEL), BF16)],
    )
    return pl.pallas_call(
        _expert_body,
        grid_spec=grid_spec,
        out_shape=jax.ShapeDtypeStruct((n_slots, PACK_W), U32),
        compiler_params=pltpu.CompilerParams(dimension_semantics=("arbitrary",)),
        name="moe_experts",
    )(block_e, n_used, xs, wgu, bgu, wd, bd)


def _combine_body(gs_ref, ls_ref, n8_ref, x_ref, wt_ref, idx_ref, rank_ref, lsf_ref, y_ref, out_ref,
                  yl, sem):
    i = pl.program_id(0)
    slot = i % 2
    tables = (gs_ref, ls_ref, n8_ref)
    tm = x_ref.shape[0]
    lrows = yl.shape[1]

    def copy_into(s):
        def make_copy(l, g, rows):
            return pltpu.make_async_copy(y_ref.at[pl.ds(g, rows), :], yl.at[s, pl.ds(l, rows), :], sem.at[s])
        return make_copy

    def make_wait(pieces):
        rows = pieces * GROUP_ALIGN
        return pltpu.make_async_copy(y_ref.at[pl.ds(0, rows), :], yl.at[0, pl.ds(0, rows), :], sem.at[slot])

    @pl.when(i == 0)
    def _():
        yl[...] = jnp.zeros_like(yl)
        _start_group_copies(tables, i, copy_into(slot))

    @pl.when(i + 1 < pl.num_programs(0))
    def _():
        _start_group_copies(tables, i + 1, copy_into(1 - slot))

    pos = lax.broadcasted_iota(I32, (tm, lrows), 1)
    lp = _local_positions(idx_ref, rank_ref, lsf_ref)
    wt = wt_ref[...]
    wm = jnp.zeros((tm, lrows), F32)
    for k in range(TOP_K):
        wm = jnp.where(pos == lp[k], wt[:, k:k + 1], wm)
    _wait_pieces(n8_ref[i, TOTAL_LANE], make_wait)
    wmb = wm.astype(BF16)
    y_hi, y_lo = _unpack_rows(yl[slot])
    out_ref[:, :PACK_W] = x_ref[:, :PACK_W] + _dot(wmb, y_hi)
    out_ref[:, PACK_W:] = x_ref[:, PACK_W:] + _dot(wmb, y_lo)


def _combine(gs, ls, n8, x1, wts, idx, rank, lsf, y):
    t = x1.shape[0]
    tm = min(MOE_TILE, t)
    nt = t // tm
    lrows = tm * TOP_K + N_EXPERTS * GROUP_ALIGN
    row = lambda i, *_: (i, 0)
    grid_spec = pltpu.PrefetchScalarGridSpec(
        num_scalar_prefetch=3,
        grid=(nt,),
        in_specs=[
            pl.BlockSpec((tm, D_MODEL), row),
            pl.BlockSpec((tm, LANES), row),
            pl.BlockSpec((tm, LANES), row),
            pl.BlockSpec((tm, LANES), row),
            pl.BlockSpec((1, 1, LANES), lambda i, *_: (i, 0, 0)),
            pl.BlockSpec(memory_space=pl.ANY),
        ],
        out_specs=pl.BlockSpec((tm, D_MODEL), row),
        scratch_shapes=[pltpu.VMEM((2, lrows, PACK_W), U32), pltpu.SemaphoreType.DMA((2,))],
    )
    return pl.pallas_call(
        _combine_body,
        grid_spec=grid_spec,
        out_shape=jax.ShapeDtypeStruct((t, D_MODEL), F32),
        compiler_params=pltpu.CompilerParams(dimension_semantics=("arbitrary",)),
        name="moe_combine",
    )(gs, ls, n8, x1, wts, idx, rank, lsf, y)


def _pad_lanes(v, fill=0.0):
    v = v.astype(F32).reshape(1, -1)
    return jnp.pad(v, ((0, 0), (0, LANES - v.shape[1])), constant_values=fill)


def _mixer(x2, mem, rel_table, attn_norm, w_in, b_gate, dn_conv, dn_a_log, dn_dt_bias, dn_out_norm,
           da_q_norm, da_k_norm, da_lambda, da_subln, mem_norm, w_mem_kv, mx_q_norm, mx_k_norm,
           w_branch, w_out, batch, seq):
    wp = jnp.concatenate([w_in[:, :W_AB_LO], w_in[:, W_AB_HI:]], axis=1).astype(BF16)
    wab = jnp.pad(w_in[:, W_AB_LO:W_AB_HI], ((0, 0), (0, LANES - (W_AB_HI - W_AB_LO))))
    p, ab = _inproj(x2, attn_norm.reshape(1, -1), wp, wab)

    o_dn = _deltanet(p, ab, dn_conv, _pad_lanes(dn_a_log), _pad_lanes(dn_dt_bias),
                     dn_out_norm.reshape(1, -1), batch, seq)

    tq = min(ATT_BLOCK, seq)
    bias = _bias_tiles(rel_table.T, tq)
    o_da = _attention(p, bias, jnp.tile(da_q_norm, 2).reshape(1, -1), jnp.tile(da_k_norm, 2).reshape(1, -1),
                      da_lambda, da_subln.reshape(-1, 1), batch, seq)

    mk, mv = _memkv(mem, mem_norm.reshape(1, -1), w_mem_kv.astype(BF16), mx_k_norm.reshape(1, -1))
    return _merge(x2, o_dn, o_da, p, mk, mv, mx_q_norm.reshape(1, -1), b_gate.reshape(3, D_MODEL),
                  w_branch.astype(BF16), w_out.astype(BF16), seq)


def _moe(x1, ffn_norm, w_router, b_router, w_gate_up, b_gate_up, w_down, b_down):
    t = x1.shape[0]
    nt = t // min(MOE_TILE, t)
    max_rows = t * TOP_K + nt * N_EXPERTS * (GROUP_ALIGN - 1)
    n_blocks = -(-max_rows // EXPERT_BLOCK) + N_EXPERTS
    n_blocks_pad = -(-n_blocks // 8) * 8
    n_slots = n_blocks * EXPERT_BLOCK

    wr = jnp.pad(w_router, ((0, 0), (0, LANES - N_EXPERTS)))
    h2, idx, wts, rank, cnt = _router(x1, ffn_norm.reshape(1, -1), wr, _pad_lanes(b_router, NEG))
    gs, ls, n8, lsf, tail, meta = _plan(cnt[:, 0, :], n_blocks_pad)
    block_e = meta[:n_blocks, 0]
    n_used = meta[0:1, 1]
    lsf = lsf.reshape(nt, 1, LANES)

    xs = _dispatch(gs, ls, n8, tail, h2, idx, rank, lsf, n_slots)
    y = _experts(block_e, n_used, xs, w_gate_up, b_gate_up.reshape(N_EXPERTS, 1, -1),
                 w_down, b_down.reshape(N_EXPERTS, 1, -1))
    return _combine(gs, ls, n8, x1, wts, idx, rank, lsf, y)


def kernel(x, mem, rel_table, attn_norm, w_in, b_gate, dn_conv, dn_a_log, dn_dt_bias, dn_out_norm,
           da_q_norm, da_k_norm, da_lambda, da_subln, mem_norm, w_mem_kv, mx_q_norm, mx_k_norm,
           w_branch, w_out, ffn_norm, w_router, b_router, w_gate_up, b_gate_up, w_down, b_down):
    batch, seq, d = x.shape
    x2 = x.reshape(batch * seq, d)
    x1 = _mixer(x2, mem, rel_table, attn_norm[0], w_in[0], b_gate[0], dn_conv[0], dn_a_log[0],
                dn_dt_bias[0], dn_out_norm[0], da_q_norm[0], da_k_norm[0], da_lambda[0], da_subln[0],
                mem_norm[0], w_mem_kv[0], mx_q_norm[0], mx_k_norm[0], w_branch[0], w_out[0], batch, seq)
    out = _moe(x1, ffn_norm[0], w_router[0], b_router[0], w_gate_up[0], b_gate_up[0], w_down[0],
               b_down[0])
    return out.reshape(batch, seq, d)
```
